```python
import jax, jax.numpy as jnp
from jax import lax
import numpy as np

D_MODEL = 1024
BATCH = 4
SEQ = 8192
DEPTH = 2

GRID_W = 64
CTX_LEN = 256

RET_HEADS = 4
RET_DK = 128
RET_DV = 128
RET_CHUNK = 128
RET_W = RET_HEADS * RET_DV
MLA_HEADS = 8
MLA_Q_LORA = 384
MLA_KV_LORA = 256
MLA_NOPE = 64
MLA_ROPE = 32
MLA_V = 64
MLA_QK = MLA_NOPE + MLA_ROPE
MLA_W = MLA_HEADS * MLA_V
ATTN_Q_BLOCK = 128
ROPE_THETA = 10000.0
GMLP_GROUPS = 4
GMLP_GROUP_W = 128
GMLP_W = GMLP_GROUPS * GMLP_GROUP_W
GMLP_CHUNK = 128

N_BRANCH = 3

N_EXPERTS = 64
TOP_K = 6
D_EXPERT = 256
D_SHARED = 256
ROUTED_SCALE = 2.5
MOE_BLOCK = 128

EPS = 1e-6

OFF_RET_Q = 0
OFF_RET_K = OFF_RET_Q + RET_HEADS * RET_DK
OFF_RET_V = OFF_RET_K + RET_HEADS * RET_DK
OFF_RET_G = OFF_RET_V + RET_W
OFF_MLA_CQ = OFF_RET_G + RET_W
OFF_MLA_CKV = OFF_MLA_CQ + MLA_Q_LORA
OFF_MLA_KR = OFF_MLA_CKV + MLA_KV_LORA
OFF_GMLP_UV = OFF_MLA_KR + MLA_ROPE
OFF_MERGE = OFF_GMLP_UV + 2 * GMLP_W
N_IN = OFF_MERGE + N_BRANCH * D_MODEL

kernel_name = 'hybrid_retention_mla_gmlp_moe_dit'


def _rmsnorm(x, g):
    x32 = x.astype(jnp.float32)
    y = x32 * lax.rsqrt(jnp.mean(x32 * x32, axis=-1, keepdims=True) + EPS)
    return (y * g.astype(jnp.float32)).astype(x.dtype)


def _layernorm(x, g, b):
    x32 = x.astype(jnp.float32)
    mu = jnp.mean(x32, axis=-1, keepdims=True)
    var = jnp.mean(jnp.square(x32 - mu), axis=-1, keepdims=True)
    y = (x32 - mu) * lax.rsqrt(var + EPS) * g.astype(jnp.float32) + b.astype(jnp.float32)
    return y.astype(x.dtype)


def _heads(t, n):
    B, L, _ = t.shape
    return t.reshape(B, L, n, -1).transpose(0, 2, 1, 3)


def _merge_heads(t):
    B, n, L, d = t.shape
    return t.transpose(0, 2, 1, 3).reshape(B, L, n * d)


def _flip(t):
    return jnp.flip(t, axis=2)


def _axial_rope_tables(L):
    rows = L // GRID_W
    half = MLA_ROPE // 2
    n_freq = half // 2
    inv = ROPE_THETA ** (-2.0 * jnp.arange(n_freq, dtype=jnp.float32) / half)
    ang_r = jnp.arange(rows, dtype=jnp.float32)[:, None, None] * inv
    ang_c = jnp.arange(GRID_W, dtype=jnp.float32)[None, :, None] * inv
    ang_r = jnp.broadcast_to(ang_r, (rows, GRID_W, n_freq)).reshape(L, n_freq)
    ang_c = jnp.broadcast_to(ang_c, (rows, GRID_W, n_freq)).reshape(L, n_freq)
    return (jnp.cos(ang_r), jnp.sin(ang_r), jnp.cos(ang_c), jnp.sin(ang_c))


def _rope_half(x, cos, sin):
    x1, x2 = jnp.split(x, 2, axis=-1)
    return jnp.concatenate([x1 * cos - x2 * sin, x1 * sin + x2 * cos], axis=-1)


def _apply_axial_rope(x, tables):
    cos_r, sin_r, cos_c, sin_c = tables
    xr, xc = jnp.split(x.astype(jnp.float32), 2, axis=-1)
    out = jnp.concatenate([_rope_half(xr, cos_r, sin_r), _rope_half(xc, cos_c, sin_c)], axis=-1)
    return out.astype(x.dtype)


def _retention_dir(q, k, v, log_g, s0, strict):
    B, H, L, dk = q.shape
    dv = v.shape[-1]
    n = L // RET_CHUNK
    qc = q.astype(jnp.float32).reshape(B, H, n, RET_CHUNK, dk)
    kc = k.astype(jnp.float32).reshape(B, H, n, RET_CHUNK, dk)
    vc = v.astype(jnp.float32).reshape(B, H, n, RET_CHUNK, dv)
    pos = jnp.arange(RET_CHUNK, dtype=jnp.float32)
    diff = pos[:, None] - pos[None, :]
    keep = diff > 0 if strict else diff >= 0
    dmat = jnp.where(keep, jnp.exp(log_g[:, None, None] * jnp.maximum(diff, 0.0)), 0.0)
    att = jnp.einsum('bhnid,bhnjd->bhnij', qc, kc) * dmat[None, :, None]
    intra = jnp.einsum('bhnij,bhnjv->bhniv', att, vc)
    k_dec = jnp.exp(log_g[:, None] * (RET_CHUNK - 1 - pos))
    q_dec = jnp.exp(log_g[:, None] * (pos + 1.0))
    chunk_kv = jnp.einsum('bhnjd,hj,bhnjv->nbhdv', kc, k_dec, vc)
    c_dec = jnp.exp(log_g * RET_CHUNK)[None, :, None, None]

    def step(s, kv):
        return s * c_dec + kv, s

    s_fin, s_prev = lax.scan(step, s0, chunk_kv)
    inter = jnp.einsum('bhnid,hi,nbhdv->bhniv', qc, q_dec, s_prev)
    return (intra + inter).reshape(B, H, L, dv), s_fin


def _retention_state(k, v, log_g):
    L = k.shape[2]
    dec = jnp.exp(log_g[:, None] * (L - 1 - jnp.arange(L, dtype=jnp.float32)))
    return jnp.einsum('bhjd,hj,bhjv->bhdv', k.astype(jnp.float32), dec, v.astype(jnp.float32))


def _retention_bidir(q, k, v, lg_f, lg_b, s0_f, s0_b):
    o_f, s_f = _retention_dir(q, k, v, lg_f, s0_f, False)
    o_b, s_b = _retention_dir(_flip(q), _flip(k), _flip(v), lg_b, s0_b, True)
    return o_f + _flip(o_b), s_f, s_b


def _retention_out(o, g, gn_g, gn_b, w_br):
    mu = jnp.mean(o, axis=-1, keepdims=True)
    var = jnp.mean(jnp.square(o - mu), axis=-1, keepdims=True)
    on = _merge_heads((o - mu) * lax.rsqrt(var + EPS))
    y = jax.nn.silu(g.astype(jnp.float32)) * (on * gn_g.astype(jnp.float32) + gn_b.astype(jnp.float32))
    return y.astype(g.dtype) @ w_br


def _mla_kv(ckv, kr, kva_g, w_ukv, kn_g, kr_g, tables):
    B, L, _ = ckv.shape
    kv = _heads(_rmsnorm(ckv, kva_g) @ w_ukv, MLA_HEADS)
    k_nope = _rmsnorm(kv[..., :MLA_NOPE], kn_g)
    v = kv[..., MLA_NOPE:]
    k_rope = _rmsnorm(kr, kr_g)[:, None]
    if tables is not None:
        k_rope = _apply_axial_rope(k_rope, tables)
    k = jnp.concatenate([k_nope, jnp.broadcast_to(k_rope, (B, MLA_HEADS, L, MLA_ROPE))], axis=-1)
    return k, v


def _mla_q(cq, qa_g, w_uq, qn_g, tables):
    q = _rmsnorm(_heads(_rmsnorm(cq, qa_g) @ w_uq, MLA_HEADS), qn_g)
    if tables is not None:
        q = jnp.concatenate([q[..., :MLA_NOPE], _apply_axial_rope(q[..., MLA_NOPE:], tables)], axis=-1)
    return q


def _attend(q, k, v):
    s = jnp.einsum('bhqd,bhkd->bhqk', q, k, preferred_element_type=jnp.float32) * (MLA_QK ** -0.5)
    p = jax.nn.softmax(s, axis=-1).astype(v.dtype)
    return jnp.einsum('bhqk,bhkv->bhqv', p, v)


def _attend_blocks(q, k, v):
    B, H, L, d = q.shape
    nb = L // ATTN_Q_BLOCK
    qb = q.reshape(B, H, nb, ATTN_Q_BLOCK, d).transpose(2, 0, 1, 3, 4)
    ob = lax.map(lambda t: _attend(t, k, v), qb)
    return ob.transpose(1, 2, 0, 3, 4).reshape(B, H, L, -1)


def _gmlp_branch(uv, ln_g, ln_b, ws, bs, w_br):
    z = jax.nn.gelu(uv)
    u, v = jnp.split(z, 2, axis=-1)
    B, L, _ = v.shape
    n = L // GMLP_CHUNK
    vn = _layernorm(v, ln_g, ln_b).reshape(B, n, GMLP_CHUNK, GMLP_GROUPS, GMLP_GROUP_W)
    sv = jnp.einsum('gij,bnjgc->bnigc', ws, vn) + bs.T[None, None, :, :, None]
    return (u * sv.reshape(B, L, GMLP_W)) @ w_br


def _gated_merge(gate_logits, y_a, y_b, y_c, w_out):
    g = jax.nn.sigmoid(gate_logits.astype(jnp.float32))
    y = (g[..., :D_MODEL] * y_a + g[..., D_MODEL:2 * D_MODEL] * y_b + g[..., 2 * D_MODEL:] * y_c)
    return y.astype(y_a.dtype) @ w_out


def _mixer(h, hc, tables, w_in, dec_f, dec_b, gn_g, gn_b, w_br_ret,
           qa_g, w_uq, kva_g, w_ukv, qn_g, kn_g, kr_g, w_br_mla,
           ln_g, ln_b, ws, bs, w_br_gmlp, w_out, ctx_out):
    B = h.shape[0]
    p = h @ w_in
    pc = hc @ w_in if ctx_out else None

    def lat(off, size):
        return p[..., off:off + size]

    def con(off, size):
        if ctx_out:
            return pc[..., off:off + size]
        return hc @ w_in[:, off:off + size]

    lg_f = jax.nn.log_sigmoid(dec_f.astype(jnp.float32))
    lg_b = jax.nn.log_sigmoid(dec_b.astype(jnp.float32))
    qk_w = RET_HEADS * RET_DK
    k_scale = RET_DK ** -0.5

    rk_c = _heads(con(OFF_RET_K, qk_w), RET_HEADS) * k_scale
    rv_c = _heads(con(OFF_RET_V, RET_W), RET_HEADS)
    if ctx_out:
        zero = jnp.zeros((B, RET_HEADS, RET_DK, RET_DV), jnp.float32)
        o_c, s_f, s_b = _retention_bidir(_heads(con(OFF_RET_Q, qk_w), RET_HEADS), rk_c, rv_c,
                                         lg_f, lg_b, zero, zero)
        yr_c = _retention_out(o_c, con(OFF_RET_G, RET_W), gn_g, gn_b, w_br_ret)
    else:
        s_f = _retention_state(rk_c, rv_c, lg_f)
        s_b = _retention_state(_flip(rk_c), _flip(rv_c), lg_b)
    o, _, _ = _retention_bidir(_heads(lat(OFF_RET_Q, qk_w), RET_HEADS),
                               _heads(lat(OFF_RET_K, qk_w), RET_HEADS) * k_scale,
                               _heads(lat(OFF_RET_V, RET_W), RET_HEADS),
                               lg_f, lg_b, s_f, s_b)
    yr = _retention_out(o, lat(OFF_RET_G, RET_W), gn_g, gn_b, w_br_ret)

    k_c, v_c = _mla_kv(con(OFF_MLA_CKV, MLA_KV_LORA), con(OFF_MLA_KR, MLA_ROPE),
                       kva_g, w_ukv, kn_g, kr_g, None)
    k_l, v_l = _mla_kv(lat(OFF_MLA_CKV, MLA_KV_LORA), lat(OFF_MLA_KR, MLA_ROPE),
                       kva_g, w_ukv, kn_g, kr_g, tables)
    q_l = _mla_q(lat(OFF_MLA_CQ, MLA_Q_LORA), qa_g, w_uq, qn_g, tables)
    o_m = _attend_blocks(q_l, jnp.concatenate([k_c, k_l], axis=2), jnp.concatenate([v_c, v_l], axis=2))
    ym = _merge_heads(o_m) @ w_br_mla

    yg = _gmlp_branch(lat(OFF_GMLP_UV, 2 * GMLP_W), ln_g, ln_b, ws, bs, w_br_gmlp)

    y = _gated_merge(lat(OFF_MERGE, N_BRANCH * D_MODEL), yr, ym, yg, w_out)
    if not ctx_out:
        return y, None
    q_c = _mla_q(con(OFF_MLA_CQ, MLA_Q_LORA), qa_g, w_uq, qn_g, None)
    ym_c = _merge_heads(_attend(q_c, k_c, v_c)) @ w_br_mla
    yg_c = _gmlp_branch(con(OFF_GMLP_UV, 2 * GMLP_W), ln_g, ln_b, ws, bs, w_br_gmlp)
    y_c = _gated_merge(con(OFF_MERGE, N_BRANCH * D_MODEL), yr_c, ym_c, yg_c, w_out)
    return y, y_c


def _routed_experts(t, idx, w, wg, wu, wd):
    N, D = t.shape
    A = N * TOP_K
    n_blocks = -(-(A + N_EXPERTS * (MOE_BLOCK - 1)) // MOE_BLOCK)
    n_rows = n_blocks * MOE_BLOCK
    flat_e = idx.reshape(-1)
    order = jnp.argsort(flat_e)
    e_sorted = flat_e[order]
    counts = jnp.zeros((N_EXPERTS,), jnp.int32).at[flat_e].add(1)
    padded = (counts + MOE_BLOCK - 1) // MOE_BLOCK * MOE_BLOCK
    pad_end = jnp.cumsum(padded)
    pad_start = pad_end - padded
    start = jnp.cumsum(counts) - counts
    dest = pad_start[e_sorted] + jnp.arange(A, dtype=jnp.int32) - start[e_sorted]
    row_tok = jnp.zeros((n_rows,), jnp.int32).at[dest].set((order // TOP_K).astype(jnp.int32))
    row_w = jnp.zeros((n_rows,), jnp.float32).at[dest].set(w.reshape(-1)[order])
    blk_e = jnp.minimum(jnp.searchsorted(pad_end, jnp.arange(n_blocks, dtype=jnp.int32) * MOE_BLOCK,
                                         side='right'), N_EXPERTS - 1)

    def step(acc, blk):
        tok_b, w_b, e = blk
        xb = t[tok_b]
        hb = jax.nn.silu(xb @ wg[e]) * (xb @ wu[e])
        yb = (hb @ wd[e]).astype(jnp.float32) * w_b[:, None]
        return acc.at[tok_b].add(yb), None

    acc, _ = lax.scan(step, jnp.zeros((N, D), jnp.float32),
                      (row_tok.reshape(n_blocks, MOE_BLOCK), row_w.reshape(n_blocks, MOE_BLOCK), blk_e))
    return acc.astype(t.dtype)


def _moe(t, router, bias, wg, wu, wd, sg, su, sd):
    scores = jax.nn.sigmoid((t @ router).astype(jnp.float32))
    _, idx = lax.top_k(scores + bias.astype(jnp.float32), TOP_K)
    w = jnp.take_along_axis(scores, idx, axis=-1)
    w = w / jnp.sum(w, axis=-1, keepdims=True) * ROUTED_SCALE
    shared = (jax.nn.silu(t @ sg) * (t @ su)) @ sd
    return _routed_experts(t, idx, w, wg, wu, wd) + shared


def setup_inputs(seed: int = 0) -> dict:
    key = jax.random.key(seed)
    kit = iter(jax.random.split(key, 40))
    f32 = jnp.float32
    D = D_MODEL

    def nrm(shape, scale):
        return jax.random.normal(next(kit), shape, f32) * scale

    def gain(shape):
        return 1.0 + 0.05 * jax.random.normal(next(kit), shape, f32)

    gam = 1.0 - 2.0 ** (-5.0 - jnp.arange(RET_HEADS, dtype=f32))
    decay_logit = jnp.log(gam) - jnp.log1p(-gam)
    x = nrm((BATCH, SEQ, D), 1.0)
    c = nrm((BATCH, D), 1.0)
    ctx = nrm((BATCH, CTX_LEN, D), 1.0)
    c_ctx = nrm((D,), 1.0)
    ada_w = nrm((DEPTH, D, 6 * D), 0.5 * D ** -0.5)
    ada_b = nrm((DEPTH, 6 * D), 0.02)
    norm1_g = gain((DEPTH, D))
    norm2_g = gain((DEPTH, D))
    w_in = nrm((DEPTH, D, N_IN), D ** -0.5)
    ret_decay_fwd = decay_logit + nrm((DEPTH, RET_HEADS), 0.1)
    ret_decay_bwd = decay_logit + nrm((DEPTH, RET_HEADS), 0.1)
    ret_gn_g = gain((DEPTH, RET_W))
    ret_gn_b = nrm((DEPTH, RET_W), 0.02)
    w_br_ret = nrm((DEPTH, RET_W, D), RET_W ** -0.5)
    mla_qa_g = gain((DEPTH, MLA_Q_LORA))
    mla_w_uq = nrm((DEPTH, MLA_Q_LORA, MLA_HEADS * MLA_QK), MLA_Q_LORA ** -0.5)
    mla_kva_g = gain((DEPTH, MLA_KV_LORA))
    mla_w_ukv = nrm((DEPTH, MLA_KV_LORA, MLA_HEADS * (MLA_NOPE + MLA_V)), MLA_KV_LORA ** -0.5)
    mla_qn_g = gain((DEPTH, MLA_QK))
    mla_kn_g = gain((DEPTH, MLA_NOPE))
    mla_kr_g = gain((DEPTH, MLA_ROPE))
    w_br_mla = nrm((DEPTH, MLA_W, D), MLA_W ** -0.5)
    gmlp_ln_g = gain((DEPTH, GMLP_W))
    gmlp_ln_b = nrm((DEPTH, GMLP_W), 0.02)
    gmlp_ws = nrm((DEPTH, GMLP_GROUPS, GMLP_CHUNK, GMLP_CHUNK), GMLP_CHUNK ** -0.5)
    gmlp_bs = gain((DEPTH, GMLP_GROUPS, GMLP_CHUNK))
    w_br_gmlp = nrm((DEPTH, GMLP_W, D), GMLP_W ** -0.5)
    w_out = nrm((DEPTH, D, D), D ** -0.5)
    moe_router = nrm((DEPTH, D, N_EXPERTS), D ** -0.5)
    moe_bias = nrm((DEPTH, N_EXPERTS), 0.01)
    moe_w_gate = nrm((DEPTH, N_EXPERTS, D, D_EXPERT), D ** -0.5)
    moe_w_up = nrm((DEPTH, N_EXPERTS, D, D_EXPERT), D ** -0.5)
    moe_w_down = nrm((DEPTH, N_EXPERTS, D_EXPERT, D), D_EXPERT ** -0.5)
    sh_w_gate = nrm((DEPTH, D, D_SHARED), D ** -0.5)
    sh_w_up = nrm((DEPTH, D, D_SHARED), D ** -0.5)
    sh_w_down = nrm((DEPTH, D_SHARED, D), D_SHARED ** -0.5)
    return {'x': x, 'c': c, 'ctx': ctx, 'c_ctx': c_ctx, 'ada_w': ada_w, 'ada_b': ada_b,
            'norm1_g': norm1_g, 'norm2_g': norm2_g, 'w_in': w_in,
            'ret_decay_fwd': ret_decay_fwd, 'ret_decay_bwd': ret_decay_bwd,
            'ret_gn_g': ret_gn_g, 'ret_gn_b': ret_gn_b, 'w_br_ret': w_br_ret,
            'mla_qa_g': mla_qa_g, 'mla_w_uq': mla_w_uq, 'mla_kva_g': mla_kva_g, 'mla_w_ukv': mla_w_ukv,
            'mla_qn_g': mla_qn_g, 'mla_kn_g': mla_kn_g, 'mla_kr_g': mla_kr_g, 'w_br_mla': w_br_mla,
            'gmlp_ln_g': gmlp_ln_g, 'gmlp_ln_b': gmlp_ln_b, 'gmlp_ws': gmlp_ws, 'gmlp_bs': gmlp_bs,
            'w_br_gmlp': w_br_gmlp, 'w_out': w_out,
            'moe_router': moe_router, 'moe_bias': moe_bias, 'moe_w_gate': moe_w_gate,
            'moe_w_up': moe_w_up, 'moe_w_down': moe_w_down,
            'sh_w_gate': sh_w_gate, 'sh_w_up': sh_w_up, 'sh_w_down': sh_w_down}


def reference(x, c, ctx, c_ctx, ada_w, ada_b, norm1_g, norm2_g, w_in,
              ret_decay_fwd, ret_decay_bwd, ret_gn_g, ret_gn_b, w_br_ret,
              mla_qa_g, mla_w_uq, mla_kva_g, mla_w_ukv, mla_qn_g, mla_kn_g, mla_kr_g, w_br_mla,
              gmlp_ln_g, gmlp_ln_b, gmlp_ws, gmlp_bs, w_br_gmlp, w_out,
              moe_router, moe_bias, moe_w_gate, moe_w_up, moe_w_down,
              sh_w_gate, sh_w_up, sh_w_down):
    B, L, D = x.shape
    tables = _axial_rope_tables(L)
    s_lat = jax.nn.silu(c)
    s_ctx = jax.nn.silu(c_ctx)
    cs = ctx
    for l in range(DEPTH):
        ctx_out = l < DEPTH - 1
        mod = (s_lat @ ada_w[l] + ada_b[l])[:, None, :]
        mod_c = s_ctx @ ada_w[l] + ada_b[l]
        shift_a, scale_a, gate_a, shift_f, scale_f, gate_f = jnp.split(mod, 6, axis=-1)
        shift_ac, scale_ac, gate_ac, shift_fc, scale_fc, gate_fc = jnp.split(mod_c, 6, axis=-1)

        h = _rmsnorm(x, norm1_g[l]) * (1.0 + scale_a) + shift_a
        hc = _rmsnorm(cs, norm1_g[l]) * (1.0 + scale_ac) + shift_ac
        y, y_c = _mixer(h, hc, tables, w_in[l], ret_decay_fwd[l], ret_decay_bwd[l],
                        ret_gn_g[l], ret_gn_b[l], w_br_ret[l],
                        mla_qa_g[l], mla_w_uq[l], mla_kva_g[l], mla_w_ukv[l],
                        mla_qn_g[l], mla_kn_g[l], mla_kr_g[l], w_br_mla[l],
                        gmlp_ln_g[l], gmlp_ln_b[l], gmlp_ws[l], gmlp_bs[l], w_br_gmlp[l],
                        w_out[l], ctx_out)
        x = x + gate_a * y
        h2 = _rmsnorm(x, norm2_g[l]) * (1.0 + scale_f) + shift_f
        moe_args = (moe_router[l], moe_bias[l], moe_w_gate[l], moe_w_up[l], moe_w_down[l],
                    sh_w_gate[l], sh_w_up[l], sh_w_down[l])
        if ctx_out:
            cs = cs + gate_ac * y_c
            h2c = _rmsnorm(cs, norm2_g[l]) * (1.0 + scale_fc) + shift_fc
            n_lat = B * L
            f = _moe(jnp.concatenate([h2.reshape(n_lat, D), h2c.reshape(-1, D)], axis=0), *moe_args)
            x = x + gate_f * f[:n_lat].reshape(B, L, D)
            cs = cs + gate_fc * f[n_lat:].reshape(cs.shape)
        else:
            x = x + gate_f * _moe(h2.reshape(B * L, D), *moe_args).reshape(B, L, D)
    return x
```

```python
import functools

import jax
import jax.numpy as jnp
from jax import lax
from jax.experimental import pallas as pl
from jax.experimental.pallas import tpu as pltpu

F32 = jnp.float32
BF16 = jnp.bfloat16

D = 1024
GRID_W = 64
RET_HEADS = 4
RET_D = 128
RET_CHUNK = 128
MLA_HEADS = 8
MLA_Q_LORA = 384
MLA_KV_LORA = 256
MLA_NOPE = 64
MLA_ROPE = 32
MLA_V = 64
MLA_QK = MLA_NOPE + MLA_ROPE
HEAD_PAD = 128
ROPE_THETA = 10000.0
GMLP_GROUPS = 4
GMLP_W = 512
GMLP_CHUNK = 128
N_EXPERTS = 64
TOP_K = 6
D_EXPERT = 256
ROUTED_SCALE = 2.5
EPS = 1e-6

TM = 256
MOE_ROWS = 256
ATT_KV_CHUNK = 512

C_MERGE = 0
C_UV = 3072
C_RET = 4096
C_CKV = 6144
C_KR = 6400
C_CQ = 6528
N_IN_PAD = 6912
IN_CHUNK = 768

VMEM_LIMIT = 56 * 1024 * 1024


def _cparams(n_axes, vmem=VMEM_LIMIT):
    return pltpu.CompilerParams(dimension_semantics=("arbitrary",) * n_axes, vmem_limit_bytes=vmem)


def _silu(x):
    return x * jax.nn.sigmoid(x)


def _dot(a, b):
    return jnp.dot(a, b, preferred_element_type=F32)


def _dot_nt(a, b):
    return lax.dot_general(a, b, (((1,), (1,)), ((), ())), preferred_element_type=F32)


def _dot_tn(a, b):
    return lax.dot_general(a, b, (((0,), (0,)), ((), ())), preferred_element_type=F32)


def _ada_kernel(c_ref, w_ref, b_ref, o_ref):
    s = _silu(c_ref[...])
    o_ref[...] = _dot(s.astype(BF16), w_ref[...].astype(BF16)) + b_ref[...]


def _ada(c_rows, ada_w, ada_b):
    depth = ada_w.shape[0]
    n = ada_w.shape[2]
    cw = 1536
    return pl.pallas_call(
        _ada_kernel,
        grid=(depth, n // cw),
        in_specs=[pl.BlockSpec((8, D), lambda l, j: (0, 0)),
                  pl.BlockSpec((None, D, cw), lambda l, j: (l, 0, j)),
                  pl.BlockSpec((None, 1, cw), lambda l, j: (l, 0, j))],
        out_specs=pl.BlockSpec((None, 8, cw), lambda l, j: (l, 0, j)),
        out_shape=jax.ShapeDtypeStruct((depth, 8, n), F32),
        compiler_params=_cparams(2),
        name="ada_mod",
    )(c_rows, ada_w, ada_b.reshape(depth, 1, n))


def _modulated_rmsnorm(x, g, shift, scale):
    y = x * lax.rsqrt(jnp.mean(x * x, axis=-1, keepdims=True) + EPS) * g
    return y * (1.0 + scale) + shift


def _in_proj_kernel(x_ref, mod_ref, g_ref, w_ref, o_ref, h_scr):
    h = _modulated_rmsnorm(x_ref[...], g_ref[...], mod_ref[:, 0:D], mod_ref[:, D:2 * D])
    h_scr[...] = h.astype(BF16)
    for c in range(N_IN_PAD // IN_CHUNK):
        cols = slice(c * IN_CHUNK, (c + 1) * IN_CHUNK)
        o_ref[:, cols] = _dot(h_scr[...], w_ref[:, cols]).astype(BF16)


def _in_proj(xs, mod3, g, w_in_r, n_tiles, mod_row):
    n_rows = xs.shape[0]
    return pl.pallas_call(
        _in_proj_kernel,
        grid=(n_tiles,),
        in_specs=[pl.BlockSpec((TM, D), lambda t: (t, 0)),
                  pl.BlockSpec((None, 1, 6 * D), lambda t: (mod_row(t), 0, 0)),
                  pl.BlockSpec((1, D), lambda t: (0, 0)),
                  pl.BlockSpec((D, N_IN_PAD), lambda t: (0, 0))],
        out_specs=pl.BlockSpec((TM, N_IN_PAD), lambda t: (t, 0)),
        out_shape=jax.ShapeDtypeStruct((n_rows, N_IN_PAD), BF16),
        scratch_shapes=[pltpu.VMEM((TM, D), BF16)],
        compiler_params=_cparams(1),
        name="in_proj",
    )(xs, mod3, g, w_in_r)


def _retention_kernel(lg_ref, q_ref, k_ref, v_ref, g_ref, gng_ref, gnb_ref, y_ref, o_scr,
                      *, n_lat_chunks, n_ctx_chunks):
    h = pl.program_id(1)
    lg_f = lg_ref[0, h]
    lg_b = lg_ref[1, h]
    C = RET_CHUNK
    k_scale = RET_D ** -0.5
    ri = lax.broadcasted_iota(jnp.int32, (C, C), 0).astype(F32)
    ci = lax.broadcasted_iota(jnp.int32, (C, C), 1).astype(F32)
    pos = lax.broadcasted_iota(jnp.int32, (C, 1), 0).astype(F32)
    diff = ri - ci
    d_f = jnp.where(diff >= 0, jnp.exp(lg_f * jnp.maximum(diff, 0.0)), 0.0) * k_scale
    d_b = jnp.where(diff < 0, jnp.exp(lg_b * jnp.maximum(-diff, 0.0)), 0.0) * k_scale
    qdec_f = jnp.exp(lg_f * (pos + 1.0))
    kdec_f = jnp.exp(lg_f * (C - 1.0 - pos)) * k_scale
    cdec_f = jnp.exp(lg_f * C)
    qdec_b = jnp.exp(lg_b * (C - pos))
    kdec_b = jnp.exp(lg_b * pos) * k_scale
    cdec_b = jnp.exp(lg_b * C)

    def chunk(c, state, dmat, qdec, kdec, cdec):
        rows = pl.ds(pl.multiple_of(c * C, C), C)
        q = q_ref[rows, :]
        k = k_ref[rows, :]
        v = v_ref[rows, :]
        att = (_dot_nt(q, k) * dmat).astype(BF16)
        o = _dot(att, v) + _dot((q.astype(F32) * qdec).astype(BF16), state.astype(BF16))
        kd = (k.astype(F32) * kdec).astype(BF16)
        return rows, o, state * cdec + _dot_tn(kd, v)

    def fwd_body(c, state):
        rows, o, state = chunk(c, state, d_f, qdec_f, kdec_f, cdec_f)
        o_scr[rows, :] = o
        return state

    def bwd_body(c, state):
        rows, o, state = chunk(c, state, d_b, qdec_b, kdec_b, cdec_b)
        o = o + o_scr[rows, :]
        mu = jnp.mean(o, axis=-1, keepdims=True)
        var = jnp.mean(jnp.square(o - mu), axis=-1, keepdims=True)
        on = (o - mu) * lax.rsqrt(var + EPS)
        y = _silu(g_ref[rows, :].astype(F32)) * (on * gng_ref[...] + gnb_ref[...])
        y_ref[rows, :] = y.astype(BF16)
        return state

    zero = jnp.zeros((RET_D, RET_D), F32)
    n_all = n_lat_chunks + n_ctx_chunks
    s = lax.fori_loop(n_lat_chunks, n_all, fwd_body, zero)
    lax.fori_loop(0, n_lat_chunks, fwd_body, s)
    s = lax.fori_loop(0, n_ctx_chunks, lambda i, st: bwd_body(n_all - 1 - i, st), zero)
    lax.fori_loop(0, n_lat_chunks, lambda i, st: bwd_body(n_lat_chunks - 1 - i, st), s)


def _retention(p3, lg, gn_g, gn_b, seq, ctx):
    B, lt, _ = p3.shape
    base = C_RET // RET_D
    kern = functools.partial(_retention_kernel, n_lat_chunks=seq // RET_CHUNK, n_ctx_chunks=ctx // RET_CHUNK)

    def col(off):
        return pl.BlockSpec((None, lt, RET_D), lambda b, h: (b, 0, base + off * RET_HEADS + h))

    return pl.pallas_call(
        kern,
        grid=(B, RET_HEADS),
        in_specs=[pl.BlockSpec(memory_space=pltpu.SMEM),
                  col(0), col(1), col(2), col(3),
                  pl.BlockSpec((1, RET_D), lambda b, h: (0, h)),
                  pl.BlockSpec((1, RET_D), lambda b, h: (0, h))],
        out_specs=pl.BlockSpec((None, lt, RET_D), lambda b, h: (b, 0, h)),
        out_shape=jax.ShapeDtypeStruct((B, lt, RET_HEADS * RET_D), BF16),
        scratch_shapes=[pltpu.VMEM((lt, RET_D), F32)],
        compiler_params=_cparams(2),
        name="retention",
    )(lg, p3, p3, p3, p3, gn_g, gn_b)


def _rope_rotate(x, first_half):
    return jnp.where(first_half, pltpu.roll(x, HEAD_PAD - 8, 1), pltpu.roll(x, 8, 1))


def _mla_prep_kernel(cq_ref, ckv_ref, kr_ref, cos_ref, sin_ref, qa_ref, kva_ref, qn_ref, kn_ref, krg_ref,
                     wq_ref, wk_ref, wv_ref, q_ref, k_ref, v_ref):
    lane = lax.broadcasted_iota(jnp.int32, (1, HEAD_PAD), 1)
    first_half = (lane % 16) < 8
    cos = cos_ref[...]
    sin = sin_ref[...]

    def rms(x, n):
        return x * lax.rsqrt(jnp.sum(x * x, axis=-1, keepdims=True) * (1.0 / n) + EPS)

    def rope(x):
        return x * cos + _rope_rotate(x, first_half) * sin

    cq = cq_ref[...].astype(F32)
    cqn = (rms(cq, MLA_Q_LORA) * qa_ref[...]).astype(BF16)
    q_all = _dot(cqn, wq_ref[...])
    ckv = ckv_ref[...].astype(F32)
    ckvn = (rms(ckv, MLA_KV_LORA) * kva_ref[...]).astype(BF16)
    k_all = _dot(ckvn, wk_ref[...])
    v_all = _dot(ckvn, wv_ref[...])
    k_rope = rope(rms(kr_ref[...].astype(F32), MLA_ROPE) * krg_ref[...])
    scale = MLA_QK ** -0.5
    for h in range(MLA_HEADS):
        cols = slice(h * HEAD_PAD, (h + 1) * HEAD_PAD)
        qh = rope(rms(q_all[:, cols], MLA_QK) * qn_ref[...]) * scale
        q_ref[h] = qh.astype(BF16)
        kh = rms(k_all[:, cols], MLA_NOPE) * kn_ref[...] + k_rope
        k_ref[h] = kh.astype(BF16)
        v_ref[h] = v_all[:, cols].astype(BF16)


def _mla_prep(p, cos_t, sin_t, qa_g, kva_g, qn_p, kn_p, kr_p, wq_p, wk_p, wv_p, B, lt):
    tiles_per_b = lt // TM
    hw = MLA_HEADS * HEAD_PAD
    const = lambda shape: pl.BlockSpec(shape, lambda b, j: (0,) * len(shape))
    head_out = pl.BlockSpec((None, MLA_HEADS, TM, HEAD_PAD), lambda b, j: (b, 0, j, 0))
    shp = jax.ShapeDtypeStruct((B, MLA_HEADS, lt, HEAD_PAD), BF16)
    return pl.pallas_call(
        _mla_prep_kernel,
        grid=(B, tiles_per_b),
        in_specs=[pl.BlockSpec((TM, MLA_Q_LORA), lambda b, j: (b * tiles_per_b + j, C_CQ // MLA_Q_LORA)),
                  pl.BlockSpec((TM, MLA_KV_LORA), lambda b, j: (b * tiles_per_b + j, C_CKV // MLA_KV_LORA)),
                  pl.BlockSpec((TM, HEAD_PAD), lambda b, j: (b * tiles_per_b + j, C_KR // HEAD_PAD)),
                  pl.BlockSpec((TM, HEAD_PAD), lambda b, j: (j, 0)),
                  pl.BlockSpec((TM, HEAD_PAD), lambda b, j: (j, 0)),
                  const((1, MLA_Q_LORA)), const((1, MLA_KV_LORA)),
                  const((1, HEAD_PAD)), const((1, HEAD_PAD)), const((1, HEAD_PAD)),
                  const((MLA_Q_LORA, hw)), const((MLA_KV_LORA, hw)), const((MLA_KV_LORA, hw))],
        out_specs=[head_out, head_out, head_out],
        out_shape=[shp, shp, shp],
        compiler_params=_cparams(2),
        name="mla_prep",
    )(p, p, p, cos_t, sin_t, qa_g, kva_g, qn_p, kn_p, kr_p, wq_p, wk_p, wv_p)


def _attention_kernel(q_ref, k_ref, v_ref, o_ref, *, seq, ctx, ctx_tile):
    i = pl.program_id(2)

    def attend(q, k, v, carry):
        m, l, acc = carry
        s = _dot_nt(q, k)
        m_new = jnp.maximum(m, jnp.max(s, axis=-1, keepdims=True))
        alpha = jnp.exp(m - m_new)
        p = jnp.exp(s - m_new)
        l = alpha * l + jnp.sum(p, axis=-1, keepdims=True)
        acc = alpha * acc + _dot(p.astype(BF16), v)
        return m_new, l, acc

    def init():
        return (jnp.full((TM, 1), -jnp.inf, F32), jnp.zeros((TM, 1), F32), jnp.zeros((TM, HEAD_PAD), F32))

    def head_out(hh, with_latent):
        q = q_ref[hh]
        carry = init()
        if with_latent:
            def body(c, carry):
                rows = pl.ds(pl.multiple_of(c * ATT_KV_CHUNK, ATT_KV_CHUNK), ATT_KV_CHUNK)
                return attend(q, k_ref[hh, rows, :], v_ref[hh, rows, :], carry)
            carry = lax.fori_loop(0, seq // ATT_KV_CHUNK, body, carry)
        _, l, acc = attend(q, k_ref[hh, seq:seq + ctx, :], v_ref[hh, seq:seq + ctx, :], carry)
        return acc / l

    @pl.when(i != ctx_tile)
    def _():
        o_ref[...] = (head_out(0, True) + head_out(1, True)).astype(BF16)

    @pl.when(i == ctx_tile)
    def _():
        o_ref[...] = (head_out(0, False) + head_out(1, False)).astype(BF16)


def _attention(q, k, v, seq, ctx, n_q_tiles):
    B, H, lt, _ = q.shape
    kern = functools.partial(_attention_kernel, seq=seq, ctx=ctx, ctx_tile=seq // TM)
    return pl.pallas_call(
        kern,
        grid=(B, H // 2, n_q_tiles),
        in_specs=[pl.BlockSpec((None, 2, TM, HEAD_PAD), lambda b, h, i: (b, h, i, 0)),
                  pl.BlockSpec((None, 2, lt, HEAD_PAD), lambda b, h, i: (b, h, 0, 0)),
                  pl.BlockSpec((None, 2, lt, HEAD_PAD), lambda b, h, i: (b, h, 0, 0))],
        out_specs=pl.BlockSpec((None, TM, HEAD_PAD), lambda b, h, i: (b, i, h)),
        out_shape=jax.ShapeDtypeStruct((B, lt, (H // 2) * HEAD_PAD), BF16),
        compiler_params=_cparams(3),
        name="attention",
    )(q, k, v)


def _merge_kernel(x_ref, mod_ref, mg_ref, uv_ref, yr_ref, om_ref, lng_ref, lnb_ref, ws_ref, bs_ref,
                  wr_ref, wm_ref, wg_ref, wo_ref, n2_ref, x1_ref, h2_ref):
    yr = _dot(yr_ref[...], wr_ref[...])
    ym = _dot(om_ref[...], wm_ref[...])
    z = jax.nn.gelu(uv_ref[...].astype(F32))
    u = z[:, :GMLP_W]
    v = z[:, GMLP_W:]
    mu = jnp.mean(v, axis=-1, keepdims=True)
    var = jnp.mean(jnp.square(v - mu), axis=-1, keepdims=True)
    vn = ((v - mu) * lax.rsqrt(var + EPS) * lng_ref[...] + lnb_ref[...]).astype(BF16)
    gw = GMLP_W // GMLP_GROUPS
    chunks = []
    for c in range(TM // GMLP_CHUNK):
        rows = slice(c * GMLP_CHUNK, (c + 1) * GMLP_CHUNK)
        groups = [_dot(ws_ref[g], vn[rows, g * gw:(g + 1) * gw]) + bs_ref[g] for g in range(GMLP_GROUPS)]
        chunks.append(jnp.concatenate(groups, axis=1))
    sv = jnp.concatenate(chunks, axis=0)
    yg = _dot((u * sv).astype(BF16), wg_ref[...])
    gate = jax.nn.sigmoid(mg_ref[...].astype(F32))
    y = gate[:, :D] * yr + gate[:, D:2 * D] * ym + gate[:, 2 * D:] * yg
    out = _dot(y.astype(BF16), wo_ref[...])
    x1 = x_ref[...] + mod_ref[:, 2 * D:3 * D] * out
    x1_ref[...] = x1
    h2_ref[...] = _modulated_rmsnorm(x1, n2_ref[...], mod_ref[:, 3 * D:4 * D], mod_ref[:, 4 * D:5 * D])


def _merge(xs, mod3, p, y_ret, o_mla, ln_g, ln_b, ws, bs_full, w_br_ret, w_br_mla, w_br_gmlp, w_out, n2_g,
           n_tiles, tile, mod_row):
    n_rows = xs.shape[0]
    const = lambda shape: pl.BlockSpec(shape, lambda t: (0,) * len(shape))
    row = lambda w, cb=0: pl.BlockSpec((TM, w), lambda t: (tile(t), cb))
    shp = jax.ShapeDtypeStruct((n_rows, D), F32)
    return pl.pallas_call(
        _merge_kernel,
        grid=(n_tiles,),
        in_specs=[row(D),
                  pl.BlockSpec((None, 1, 6 * D), lambda t: (mod_row(tile(t)), 0, 0)),
                  row(3 * D, C_MERGE // (3 * D)), row(D, C_UV // D),
                  row(RET_HEADS * RET_D), row(MLA_HEADS * MLA_V),
                  const((1, GMLP_W)), const((1, GMLP_W)),
                  const((GMLP_GROUPS, GMLP_CHUNK, GMLP_CHUNK)), const((GMLP_GROUPS, GMLP_CHUNK, GMLP_CHUNK)),
                  const((RET_HEADS * RET_D, D)), const((MLA_HEADS * MLA_V, D)), const((GMLP_W, D)),
                  const((D, D)), const((1, D))],
        out_specs=[row(D), row(D)],
        out_shape=[shp, shp],
        compiler_params=_cparams(1),
        name="merge",
    )(xs, mod3, p, p, y_ret, o_mla, ln_g, ln_b, ws, bs_full, w_br_ret, w_br_mla, w_br_gmlp, w_out, n2_g)


def _route_kernel(h_ref, r_ref, b_ref, sg_ref, su_ref, sd_ref, idx_ref, w_ref, sh_ref):
    h = h_ref[...]
    logits = jnp.dot(h, r_ref[...], preferred_element_type=F32, precision=lax.Precision.HIGHEST)
    scores = jax.nn.sigmoid(logits)
    sel = scores + b_ref[...]
    lane_e = lax.broadcasted_iota(jnp.int32, (TM, N_EXPERTS), 1).astype(F32)
    lane_o = lax.broadcasted_iota(jnp.int32, (TM, 128), 1)
    idx_out = jnp.zeros((TM, 128), F32)
    w_out = jnp.zeros((TM, 128), F32)
    for k in range(TOP_K):
        best = jnp.max(sel, axis=-1, keepdims=True)
        pick = jnp.min(jnp.where(sel == best, lane_e, float(N_EXPERTS)), axis=-1, keepdims=True)
        hit = lane_e == pick
        wk = jnp.sum(jnp.where(hit, scores, 0.0), axis=-1, keepdims=True)
        sel = jnp.where(hit, -jnp.inf, sel)
        idx_out = jnp.where(lane_o == k, pick, idx_out)
        w_out = jnp.where(lane_o == k, wk, w_out)
    w_out = w_out / jnp.sum(w_out, axis=-1, keepdims=True) * ROUTED_SCALE
    idx_ref[...] = idx_out.astype(jnp.int32)
    w_ref[...] = w_out
    hb = h.astype(BF16)
    a = _silu(_dot(hb, sg_ref[...])) * _dot(hb, su_ref[...])
    sh_ref[...] = _dot(a.astype(BF16), sd_ref[...])


def _route(h2, router, bias, sg, su, sd, n_tiles, tile):
    const = lambda shape: pl.BlockSpec(shape, lambda t: (0,) * len(shape))
    n_act = n_tiles * TM
    return pl.pallas_call(
        _route_kernel,
        grid=(n_tiles,),
        in_specs=[pl.BlockSpec((TM, D), lambda t: (tile(t), 0)),
                  const((D, N_EXPERTS)), const((1, N_EXPERTS)),
                  const((D, D_EXPERT)), const((D, D_EXPERT)), const((D_EXPERT, D))],
        out_specs=[pl.BlockSpec((TM, 128), lambda t: (t, 0)),
                   pl.BlockSpec((TM, 128), lambda t: (t, 0)),
                   pl.BlockSpec((TM, D), lambda t: (t, 0))],
        out_shape=[jax.ShapeDtypeStruct((n_act, 128), jnp.int32),
                   jax.ShapeDtypeStruct((n_act, 128), F32),
                   jax.ShapeDtypeStruct((n_act, D), F32)],
        compiler_params=_cparams(1),
        name="route_shared",
    )(h2, router, bias, sg, su, sd)


def _row_copy(src_hbm, row, dst, slot, sem):
    return pltpu.make_async_copy(src_hbm.at[pl.ds(row, 1)], dst.at[pl.ds(slot, 1)], sem)


def _expert_kernel(blk_e_ref, n_used_ref, tok_ref, h_hbm, w_ref, wg_ref, wu_ref, wd_ref, y_ref, x_buf, sem):
    i = pl.program_id(0)

    @pl.when(i < n_used_ref[0])
    def _():
        def start(r, _):
            _row_copy(h_hbm, tok_ref[0, r], x_buf, r, sem).start()
            return 0

        def wait(r, _):
            _row_copy(h_hbm, 0, x_buf, r, sem).wait()
            return 0

        lax.fori_loop(0, MOE_ROWS, start, 0)
        lax.fori_loop(0, MOE_ROWS, wait, 0)
        x = x_buf[...].astype(BF16)
        hb = _silu(_dot(x, wg_ref[...].astype(BF16))) * _dot(x, wu_ref[...].astype(BF16))
        y_ref[...] = _dot(hb.astype(BF16), wd_ref[...].astype(BF16)) * w_ref[...]

    @pl.when(i >= n_used_ref[0])
    def _():
        y_ref[...] = jnp.zeros_like(y_ref)


def _experts(blk_e, n_used, row_tok, row_w, h2, wg, wu, wd, n_blocks):
    n_rows = n_blocks * MOE_ROWS
    grid_spec = pltpu.PrefetchScalarGridSpec(
        num_scalar_prefetch=2,
        grid=(n_blocks,),
        in_specs=[pl.BlockSpec((None, 1, MOE_ROWS), lambda i, be, nu: (i, 0, 0), memory_space=pltpu.SMEM),
                  pl.BlockSpec(memory_space=pl.ANY),
                  pl.BlockSpec((MOE_ROWS, 1), lambda i, be, nu: (i, 0)),
                  pl.BlockSpec((None, D, D_EXPERT), lambda i, be, nu: (be[i], 0, 0)),
                  pl.BlockSpec((None, D, D_EXPERT), lambda i, be, nu: (be[i], 0, 0)),
                  pl.BlockSpec((None, D_EXPERT, D), lambda i, be, nu: (be[i], 0, 0))],
        out_specs=pl.BlockSpec((MOE_ROWS, D), lambda i, be, nu: (i, 0)),
        scratch_shapes=[pltpu.VMEM((MOE_ROWS, D), F32), pltpu.SemaphoreType.DMA(())],
    )
    return pl.pallas_call(
        _expert_kernel,
        grid_spec=grid_spec,
        out_shape=jax.ShapeDtypeStruct((n_rows, D), F32),
        compiler_params=_cparams(1),
        name="routed_experts",
    )(blk_e, n_used, row_tok.reshape(n_blocks, 1, MOE_ROWS), h2, row_w.reshape(n_rows, 1), wg, wu, wd)


def _combine_kernel(pos_ref, y_hbm, x1_ref, sh_ref, mod_ref, o_ref, buf, sem):
    n = TOP_K * TM

    def start(r, _):
        _row_copy(y_hbm, pos_ref[0, r], buf, r, sem).start()
        return 0

    def wait(r, _):
        _row_copy(y_hbm, 0, buf, r, sem).wait()
        return 0

    lax.fori_loop(0, n, start, 0)
    lax.fori_loop(0, n, wait, 0)
    f = sh_ref[...]
    for k in range(TOP_K):
        f = f + buf[k * TM:(k + 1) * TM, :]
    o_ref[...] = x1_ref[...] + mod_ref[:, 5 * D:6 * D] * f


def _combine(pos_tiles, ys, x1, shared, mod3, n_tiles, tile, mod_row, out_rows, out_tile):
    return pl.pallas_call(
        _combine_kernel,
        grid=(n_tiles,),
        in_specs=[pl.BlockSpec((None, 1, TOP_K * TM), lambda t: (t, 0, 0), memory_space=pltpu.SMEM),
                  pl.BlockSpec(memory_space=pl.ANY),
                  pl.BlockSpec((TM, D), lambda t: (tile(t), 0)),
                  pl.BlockSpec((TM, D), lambda t: (t, 0)),
                  pl.BlockSpec((None, 1, 6 * D), lambda t: (mod_row(tile(t)), 0, 0))],
        out_specs=pl.BlockSpec((TM, D), lambda t: (out_tile(t), 0)),
        out_shape=jax.ShapeDtypeStruct((out_rows, D), F32),
        scratch_shapes=[pltpu.VMEM((TOP_K * TM, D), F32), pltpu.SemaphoreType.DMA(())],
        compiler_params=_cparams(1),
        name="moe_combine",
    )(pos_tiles, ys, x1, shared, mod3)


def _moe_plan(idx, w, tok_rows, n_blocks):
    n = idx.shape[0]
    a = n * TOP_K
    flat_e = idx.reshape(-1)
    order = jnp.argsort(flat_e)
    e_sorted = flat_e[order]
    counts = jnp.zeros((N_EXPERTS,), jnp.int32).at[flat_e].add(1)
    padded = (counts + MOE_ROWS - 1) // MOE_ROWS * MOE_ROWS
    pad_end = jnp.cumsum(padded)
    pad_start = pad_end - padded
    start = jnp.cumsum(counts) - counts
    dest = (pad_start[e_sorted] + jnp.arange(a, dtype=jnp.int32) - start[e_sorted]).astype(jnp.int32)
    n_rows = n_blocks * MOE_ROWS
    row_tok = jnp.zeros((n_rows,), jnp.int32).at[dest].set(tok_rows[order // TOP_K])
    row_w = jnp.zeros((n_rows,), F32).at[dest].set(w.reshape(-1)[order])
    pos = jnp.zeros((a,), jnp.int32).at[order].set(dest)
    blk_e = jnp.minimum(jnp.searchsorted(pad_end, jnp.arange(n_blocks, dtype=jnp.int32) * MOE_ROWS, side='right'),
                        N_EXPERTS - 1).astype(jnp.int32)
    n_used = (pad_end[-1:] // MOE_ROWS).astype(jnp.int32)
    pos_tiles = pos.reshape(n // TM, TM, TOP_K).transpose(0, 2, 1).reshape(n // TM, 1, TOP_K * TM)
    return blk_e, n_used, row_tok, row_w, pos_tiles


def _rope_tables(seq, ctx):
    half = MLA_ROPE // 2
    n_freq = half // 2
    inv = ROPE_THETA ** (-2.0 * jnp.arange(n_freq, dtype=F32) / half)
    t = jnp.arange(seq)
    ang_r = (t // GRID_W).astype(F32)[:, None] * inv
    ang_c = (t % GRID_W).astype(F32)[:, None] * inv
    cos = jnp.concatenate([jnp.cos(ang_r), jnp.cos(ang_r), jnp.cos(ang_c), jnp.cos(ang_c)], axis=1)
    sin = jnp.concatenate([-jnp.sin(ang_r), jnp.sin(ang_r), -jnp.sin(ang_c), jnp.sin(ang_c)], axis=1)
    pad_l = MLA_NOPE
    pad_r = HEAD_PAD - MLA_NOPE - MLA_ROPE
    cos = jnp.pad(cos, ((0, ctx), (pad_l, pad_r)), constant_values=1.0)
    cos = cos.at[seq:, :].set(1.0)
    sin = jnp.pad(sin, ((0, ctx), (pad_l, pad_r)))
    return cos, sin


def _pad_heads(w, n_heads, width, offset=0):
    k = w.shape[0]
    w = w.reshape(k, n_heads, width)
    w = jnp.pad(w, ((0, 0), (0, 0), (offset, HEAD_PAD - width - offset)))
    return w.reshape(k, n_heads * HEAD_PAD)


def _pad_vec(g, offset):
    return jnp.pad(g, (offset, HEAD_PAD - g.shape[0] - offset)).reshape(1, HEAD_PAD)


def _reorder_w_in(w):
    off_cq, off_ckv, off_kr, off_uv, off_merge = 2048, 2432, 2688, 2720, 3744
    kr = jnp.pad(w[:, off_kr:off_uv], ((0, 0), (MLA_NOPE, HEAD_PAD - MLA_NOPE - MLA_ROPE)))
    return jnp.concatenate([w[:, off_merge:], w[:, off_uv:off_merge], w[:, :off_cq],
                            w[:, off_ckv:off_kr], kr, w[:, off_cq:off_ckv]], axis=1).astype(BF16)


def kernel(x, c, ctx, c_ctx, ada_w, ada_b, norm1_g, norm2_g, w_in, ret_decay_fwd, ret_decay_bwd, ret_gn_g,
           ret_gn_b, w_br_ret, mla_qa_g, mla_w_uq, mla_kva_g, mla_w_ukv, mla_qn_g, mla_kn_g, mla_kr_g, w_br_mla,
           gmlp_ln_g, gmlp_ln_b, gmlp_ws, gmlp_bs, w_br_gmlp, w_out, moe_router, moe_bias, moe_w_gate, moe_w_up,
           moe_w_down, sh_w_gate, sh_w_up, sh_w_down):
    B, seq, _ = x.shape
    n_ctx = ctx.shape[1]
    depth = ada_w.shape[0]
    assert n_ctx == TM and seq % ATT_KV_CHUNK == 0 and seq % TM == 0
    lt = seq + n_ctx
    tiles_per_b = lt // TM
    lat_tiles_per_b = seq // TM
    ctx_tile = lat_tiles_per_b

    def mod_row(t):
        return jnp.where(t % tiles_per_b == ctx_tile, B, t // tiles_per_b)

    c_rows = jnp.concatenate([c, c_ctx[None, :], jnp.zeros((8 - B - 1, D), F32)], axis=0)
    mod = _ada(c_rows, ada_w, ada_b)
    cos_t, sin_t = _rope_tables(seq, n_ctx)
    xs = jnp.concatenate([x, ctx], axis=1).reshape(B * lt, D)

    for l in range(depth):
        last = l == depth - 1
        mod3 = mod[l].reshape(8, 1, 6 * D)
        p = _in_proj(xs, mod3, norm1_g[l].reshape(1, D), _reorder_w_in(w_in[l]), B * tiles_per_b, mod_row)

        lg = jnp.stack([jax.nn.log_sigmoid(ret_decay_fwd[l].astype(F32)),
                        jax.nn.log_sigmoid(ret_decay_bwd[l].astype(F32))])
        y_ret = _retention(p.reshape(B, lt, N_IN_PAD), lg, ret_gn_g[l].reshape(1, -1), ret_gn_b[l].reshape(1, -1),
                           seq, n_ctx)

        w_ukv = mla_w_ukv[l].reshape(MLA_KV_LORA, MLA_HEADS, MLA_NOPE + MLA_V)
        wk_p = _pad_heads(w_ukv[:, :, :MLA_NOPE].reshape(MLA_KV_LORA, -1), MLA_HEADS, MLA_NOPE).astype(BF16)
        wv = jnp.pad(w_ukv[:, :, MLA_NOPE:].reshape(MLA_KV_LORA, MLA_HEADS // 2, 2, MLA_V),
                     ((0, 0), (0, 0), (0, 0), (0, MLA_V)))
        wv = wv.at[:, :, 1, :].set(jnp.roll(wv[:, :, 1, :], MLA_V, axis=-1))
        wv_p = wv.reshape(MLA_KV_LORA, MLA_HEADS * HEAD_PAD).astype(BF16)
        wq_p = _pad_heads(mla_w_uq[l], MLA_HEADS, MLA_QK).astype(BF16)
        q, k, v = _mla_prep(p, cos_t, sin_t, mla_qa_g[l].reshape(1, -1), mla_kva_g[l].reshape(1, -1),
                            _pad_vec(mla_qn_g[l], 0), _pad_vec(mla_kn_g[l], 0), _pad_vec(mla_kr_g[l], MLA_NOPE),
                            wq_p, wk_p, wv_p, B, lt)
        o_mla = _attention(q, k, v, seq, n_ctx, lat_tiles_per_b if last else tiles_per_b)

        if last:
            n_tiles = B * lat_tiles_per_b
            tile = lambda t: (t // lat_tiles_per_b) * tiles_per_b + t % lat_tiles_per_b
        else:
            n_tiles = B * tiles_per_b
            tile = lambda t: t
        bs_full = jnp.broadcast_to(gmlp_bs[l][:, :, None], (GMLP_GROUPS, GMLP_CHUNK, GMLP_CHUNK))
        x1, h2 = _merge(xs, mod3, p, y_ret.reshape(B * lt, -1), o_mla.reshape(B * lt, -1),
                        gmlp_ln_g[l].reshape(1, -1), gmlp_ln_b[l].reshape(1, -1), gmlp_ws[l].astype(BF16), bs_full,
                        w_br_ret[l].astype(BF16), w_br_mla[l].astype(BF16), w_br_gmlp[l].astype(BF16),
                        w_out[l].astype(BF16), norm2_g[l].reshape(1, D), n_tiles, tile, mod_row)

        idx, w, shared = _route(h2, moe_router[l], moe_bias[l].reshape(1, -1), sh_w_gate[l].astype(BF16),
                                sh_w_up[l].astype(BF16), sh_w_down[l].astype(BF16), n_tiles, tile)
        n_act = n_tiles * TM
        n_blocks = -(-(n_act * TOP_K + N_EXPERTS * (MOE_ROWS - 1)) // MOE_ROWS)
        tok = jnp.arange(n_act, dtype=jnp.int32)
        tok_rows = tile(tok // TM) * TM + tok % TM
        blk_e, n_used, row_tok, row_w, pos_tiles = _moe_plan(idx[:, :TOP_K], w[:, :TOP_K], tok_rows, n_blocks)
        ys = _experts(blk_e, n_used, row_tok, row_w, h2, moe_w_gate[l], moe_w_up[l], moe_w_down[l], n_blocks)
        if last:
            xs = _combine(pos_tiles, ys, x1, shared, mod3, n_tiles, tile, mod_row, B * seq, lambda t: t)
        else:
            xs = _combine(pos_tiles, ys, x1, shared, mod3, n_tiles, tile, mod_row, B * lt, tile)
    return xs.reshape(B, seq, D)
```

```python
import functools

import jax
import jax.numpy as jnp
from jax import lax
from jax.experimental import pallas as pl
from jax.experimental.pallas import tpu as pltpu

F32 = jnp.float32
BF16 = jnp.bfloat16

D = 1024
GRID_W = 64
RET_HEADS = 4
RET_D = 128
RET_CHUNK = 128
MLA_HEADS = 8
MLA_Q_LORA = 384
MLA_KV_LORA = 256
MLA_NOPE = 64
MLA_ROPE = 32
MLA_V = 64
MLA_QK = MLA_NOPE + MLA_ROPE
HEAD_PAD = 128
ROPE_THETA = 10000.0
GMLP_GROUPS = 4
GMLP_W = 512
GMLP_CHUNK = 128
N_EXPERTS = 64
TOP_K = 6
D_EXPERT = 256
ROUTED_SCALE = 2.5
EPS = 1e-6

TM = 256
MOE_ROWS = 256
ATT_KV_CHUNK = 512
ATT_UNROLL = 2

C_MERGE = 0
C_UV = 3072
C_RET = 4096
C_CKV = 6144
C_KR = 6400
C_CQ = 6528
N_IN_PAD = 6912
IN_CHUNK = 768

VMEM_LIMIT = 56 * 1024 * 1024


def _cparams(n_axes, vmem=VMEM_LIMIT):
    return pltpu.CompilerParams(dimension_semantics=("arbitrary",) * n_axes, vmem_limit_bytes=vmem)


def _silu(x):
    return x * jax.nn.sigmoid(x)


def _dot(a, b):
    return jnp.dot(a, b, preferred_element_type=F32)


def _dot_nt(a, b):
    return lax.dot_general(a, b, (((1,), (1,)), ((), ())), preferred_element_type=F32)


def _dot_tn(a, b):
    return lax.dot_general(a, b, (((0,), (0,)), ((), ())), preferred_element_type=F32)


def _ada_kernel(c_ref, w_ref, b_ref, o_ref):
    s = _silu(c_ref[...])
    o_ref[...] = _dot(s.astype(BF16), w_ref[...].astype(BF16)) + b_ref[...]


def _ada(c_rows, ada_w, ada_b):
    depth = ada_w.shape[0]
    n = ada_w.shape[2]
    cw = 1536
    return pl.pallas_call(
        _ada_kernel,
        grid=(depth, n // cw),
        in_specs=[pl.BlockSpec((8, D), lambda l, j: (0, 0)),
                  pl.BlockSpec((None, D, cw), lambda l, j: (l, 0, j)),
                  pl.BlockSpec((None, 1, cw), lambda l, j: (l, 0, j))],
        out_specs=pl.BlockSpec((None, 8, cw), lambda l, j: (l, 0, j)),
        out_shape=jax.ShapeDtypeStruct((depth, 8, n), F32),
        compiler_params=_cparams(2),
        name="ada_mod",
    )(c_rows, ada_w, ada_b.reshape(depth, 1, n))


def _modulated_rmsnorm(x, g, shift, scale):
    y = x * lax.rsqrt(jnp.mean(x * x, axis=-1, keepdims=True) + EPS) * g
    return y * (1.0 + scale) + shift


def _in_proj_kernel(x_ref, mod_ref, g_ref, w_ref, o_ref, h_scr):
    h = _modulated_rmsnorm(x_ref[...], g_ref[...], mod_ref[:, 0:D], mod_ref[:, D:2 * D])
    h_scr[...] = h.astype(BF16)
    for c in range(N_IN_PAD // IN_CHUNK):
        cols = slice(c * IN_CHUNK, (c + 1) * IN_CHUNK)
        o_ref[:, cols] = _dot(h_scr[...], w_ref[:, cols]).astype(BF16)


def _in_proj(xs, mod3, g, w_in_r, n_tiles, mod_row):
    n_rows = xs.shape[0]
    return pl.pallas_call(
        _in_proj_kernel,
        grid=(n_tiles,),
        in_specs=[pl.BlockSpec((TM, D), lambda t: (t, 0)),
                  pl.BlockSpec((None, 1, 6 * D), lambda t: (mod_row(t), 0, 0)),
                  pl.BlockSpec((1, D), lambda t: (0, 0)),
                  pl.BlockSpec((D, N_IN_PAD), lambda t: (0, 0))],
        out_specs=pl.BlockSpec((TM, N_IN_PAD), lambda t: (t, 0)),
        out_shape=jax.ShapeDtypeStruct((n_rows, N_IN_PAD), BF16),
        scratch_shapes=[pltpu.VMEM((TM, D), BF16)],
        compiler_params=_cparams(1),
        name="in_proj",
    )(xs, mod3, g, w_in_r)


def _retention_kernel(lg_ref, q_ref, k_ref, v_ref, g_ref, gng_ref, gnb_ref, y_ref, o_scr,
                      *, n_lat_chunks, n_ctx_chunks):
    h = pl.program_id(1)
    lg_f = lg_ref[0, h]
    lg_b = lg_ref[1, h]
    C = RET_CHUNK
    k_scale = RET_D ** -0.5
    ri = lax.broadcasted_iota(jnp.int32, (C, C), 0).astype(F32)
    ci = lax.broadcasted_iota(jnp.int32, (C, C), 1).astype(F32)
    pos = lax.broadcasted_iota(jnp.int32, (C, 1), 0).astype(F32)
    diff = ri - ci
    d_f = jnp.where(diff >= 0, jnp.exp(lg_f * jnp.maximum(diff, 0.0)), 0.0) * k_scale
    d_b = jnp.where(diff < 0, jnp.exp(lg_b * jnp.maximum(-diff, 0.0)), 0.0) * k_scale
    qdec_f = jnp.exp(lg_f * (pos + 1.0))
    kdec_f = jnp.exp(lg_f * (C - 1.0 - pos)) * k_scale
    cdec_f = jnp.exp(lg_f * C)
    qdec_b = jnp.exp(lg_b * (C - pos))
    kdec_b = jnp.exp(lg_b * pos) * k_scale
    cdec_b = jnp.exp(lg_b * C)

    def chunk(c, state, dmat, qdec, kdec, cdec):
        rows = pl.ds(pl.multiple_of(c * C, C), C)
        q = q_ref[rows, :]
        k = k_ref[rows, :]
        v = v_ref[rows, :]
        att = (_dot_nt(q, k) * dmat).astype(BF16)
        o = _dot(att, v) + _dot((q.astype(F32) * qdec).astype(BF16), state.astype(BF16))
        kd = (k.astype(F32) * kdec).astype(BF16)
        return rows, o, state * cdec + _dot_tn(kd, v)

    def fwd_body(c, state):
        rows, o, state = chunk(c, state, d_f, qdec_f, kdec_f, cdec_f)
        o_scr[rows, :] = o
        return state

    def bwd_body(c, state):
        rows, o, state = chunk(c, state, d_b, qdec_b, kdec_b, cdec_b)
        o = o + o_scr[rows, :]
        mu = jnp.mean(o, axis=-1, keepdims=True)
        var = jnp.mean(jnp.square(o - mu), axis=-1, keepdims=True)
        on = (o - mu) * lax.rsqrt(var + EPS)
        y = _silu(g_ref[rows, :].astype(F32)) * (on * gng_ref[...] + gnb_ref[...])
        y_ref[rows, :] = y.astype(BF16)
        return state

    zero = jnp.zeros((RET_D, RET_D), F32)
    n_all = n_lat_chunks + n_ctx_chunks
    s = lax.fori_loop(n_lat_chunks, n_all, fwd_body, zero)
    lax.fori_loop(0, n_lat_chunks, fwd_body, s)
    s = lax.fori_loop(0, n_ctx_chunks, lambda i, st: bwd_body(n_all - 1 - i, st), zero)
    lax.fori_loop(0, n_lat_chunks, lambda i, st: bwd_body(n_lat_chunks - 1 - i, st), s)


def _retention(p3, lg, gn_g, gn_b, seq, ctx):
    B, lt, _ = p3.shape
    base = C_RET // RET_D
    kern = functools.partial(_retention_kernel, n_lat_chunks=seq // RET_CHUNK, n_ctx_chunks=ctx // RET_CHUNK)

    def col(off):
        return pl.BlockSpec((None, lt, RET_D), lambda b, h: (b, 0, base + off * RET_HEADS + h))

    return pl.pallas_call(
        kern,
        grid=(B, RET_HEADS),
        in_specs=[pl.BlockSpec(memory_space=pltpu.SMEM),
                  col(0), col(1), col(2), col(3),
                  pl.BlockSpec((1, RET_D), lambda b, h: (0, h)),
                  pl.BlockSpec((1, RET_D), lambda b, h: (0, h))],
        out_specs=pl.BlockSpec((None, lt, RET_D), lambda b, h: (b, 0, h)),
        out_shape=jax.ShapeDtypeStruct((B, lt, RET_HEADS * RET_D), BF16),
        scratch_shapes=[pltpu.VMEM((lt, RET_D), F32)],
        compiler_params=_cparams(2),
        name="retention",
    )(lg, p3, p3, p3, p3, gn_g, gn_b)


def _rope_rotate(x, first_half):
    return jnp.where(first_half, pltpu.roll(x, HEAD_PAD - 8, 1), pltpu.roll(x, 8, 1))


def _mla_prep_kernel(cq_ref, ckv_ref, kr_ref, cos_ref, sin_ref, qa_ref, kva_ref, qn_ref, kn_ref, krg_ref,
                     wq_ref, wk_ref, wv_ref, q_ref, k_ref, v_ref):
    lane = lax.broadcasted_iota(jnp.int32, (1, HEAD_PAD), 1)
    first_half = (lane % 16) < 8
    cos = cos_ref[...]
    sin = sin_ref[...]

    def rms(x, n):
        return x * lax.rsqrt(jnp.sum(x * x, axis=-1, keepdims=True) * (1.0 / n) + EPS)

    def rope(x):
        return x * cos + _rope_rotate(x, first_half) * sin

    cq = cq_ref[...].astype(F32)
    cqn = (rms(cq, MLA_Q_LORA) * qa_ref[...]).astype(BF16)
    q_all = _dot(cqn, wq_ref[...])
    ckv = ckv_ref[...].astype(F32)
    ckvn = (rms(ckv, MLA_KV_LORA) * kva_ref[...]).astype(BF16)
    k_all = _dot(ckvn, wk_ref[...])
    v_all = _dot(ckvn, wv_ref[...])
    k_rope = rope(rms(kr_ref[...].astype(F32), MLA_ROPE) * krg_ref[...])
    scale = MLA_QK ** -0.5
    for h in range(MLA_HEADS):
        cols = slice(h * HEAD_PAD, (h + 1) * HEAD_PAD)
        qh = rope(rms(q_all[:, cols], MLA_QK) * qn_ref[...]) * scale
        q_ref[h] = qh.astype(BF16)
        kh = rms(k_all[:, cols], MLA_NOPE) * kn_ref[...] + k_rope
        k_ref[h] = kh.astype(BF16)
        v_ref[h] = v_all[:, cols].astype(BF16)


def _mla_prep(p, cos_t, sin_t, qa_g, kva_g, qn_p, kn_p, kr_p, wq_p, wk_p, wv_p, B, lt):
    tiles_per_b = lt // TM
    hw = MLA_HEADS * HEAD_PAD
    const = lambda shape: pl.BlockSpec(shape, lambda b, j: (0,) * len(shape))
    head_out = pl.BlockSpec((None, MLA_HEADS, TM, HEAD_PAD), lambda b, j: (b, 0, j, 0))
    shp = jax.ShapeDtypeStruct((B, MLA_HEADS, lt, HEAD_PAD), BF16)
    return pl.pallas_call(
        _mla_prep_kernel,
        grid=(B, tiles_per_b),
        in_specs=[pl.BlockSpec((TM, MLA_Q_LORA), lambda b, j: (b * tiles_per_b + j, C_CQ // MLA_Q_LORA)),
                  pl.BlockSpec((TM, MLA_KV_LORA), lambda b, j: (b * tiles_per_b + j, C_CKV // MLA_KV_LORA)),
                  pl.BlockSpec((TM, HEAD_PAD), lambda b, j: (b * tiles_per_b + j, C_KR // HEAD_PAD)),
                  pl.BlockSpec((TM, HEAD_PAD), lambda b, j: (j, 0)),
                  pl.BlockSpec((TM, HEAD_PAD), lambda b, j: (j, 0)),
                  const((1, MLA_Q_LORA)), const((1, MLA_KV_LORA)),
                  const((1, HEAD_PAD)), const((1, HEAD_PAD)), const((1, HEAD_PAD)),
                  const((MLA_Q_LORA, hw)), const((MLA_KV_LORA, hw)), const((MLA_KV_LORA, hw))],
        out_specs=[head_out, head_out, head_out],
        out_shape=[shp, shp, shp],
        compiler_params=_cparams(2),
        name="mla_prep",
    )(p, p, p, cos_t, sin_t, qa_g, kva_g, qn_p, kn_p, kr_p, wq_p, wk_p, wv_p)


def _attention_kernel(q_ref, k_ref, v_ref, o_ref, *, seq, ctx, ctx_tile):
    i = pl.program_id(2)

    def attend(hh, rows, carry):
        m, l, acc = carry
        s = _dot_nt(q_ref[hh], k_ref[hh, rows, :])
        m_new = jnp.maximum(m, jnp.max(s, axis=-1, keepdims=True))
        alpha = jnp.exp(m - m_new)
        p = jnp.exp(s - m_new)
        l = alpha * l + jnp.sum(p, axis=-1, keepdims=True)
        acc = alpha * acc + _dot(p.astype(BF16), v_ref[hh, rows, :])
        return m_new, l, acc

    def init():
        return (jnp.full((TM, 1), -jnp.inf, F32), jnp.zeros((TM, 1), F32), jnp.zeros((TM, HEAD_PAD), F32))

    def finish(carries):
        outs = []
        for hh in range(2):
            _, l, acc = attend(hh, pl.ds(seq, ctx), carries[hh])
            outs.append(acc / l)
        o_ref[...] = (outs[0] + outs[1]).astype(BF16)

    @pl.when(i != ctx_tile)
    def _():
        def body(c, carries):
            carries = list(carries)
            for sub in range(ATT_UNROLL):
                start = pl.multiple_of((c * ATT_UNROLL + sub) * ATT_KV_CHUNK, ATT_KV_CHUNK)
                for hh in range(2):
                    carries[hh] = attend(hh, pl.ds(start, ATT_KV_CHUNK), carries[hh])
            return tuple(carries)

        finish(lax.fori_loop(0, seq // (ATT_KV_CHUNK * ATT_UNROLL), body, (init(), init())))

    @pl.when(i == ctx_tile)
    def _():
        finish((init(), init()))


def _attention(q, k, v, seq, ctx, n_q_tiles):
    B, H, lt, _ = q.shape
    kern = functools.partial(_attention_kernel, seq=seq, ctx=ctx, ctx_tile=seq // TM)
    return pl.pallas_call(
        kern,
        grid=(B, H // 2, n_q_tiles),
        in_specs=[pl.BlockSpec((None, 2, TM, HEAD_PAD), lambda b, h, i: (b, h, i, 0)),
                  pl.BlockSpec((None, 2, lt, HEAD_PAD), lambda b, h, i: (b, h, 0, 0)),
                  pl.BlockSpec((None, 2, lt, HEAD_PAD), lambda b, h, i: (b, h, 0, 0))],
        out_specs=pl.BlockSpec((None, TM, HEAD_PAD), lambda b, h, i: (b, i, h)),
        out_shape=jax.ShapeDtypeStruct((B, lt, (H // 2) * HEAD_PAD), BF16),
        compiler_params=_cparams(3),
        name="attention",
    )(q, k, v)


def _merge_kernel(x_ref, mod_ref, mg_ref, uv_ref, yr_ref, om_ref, lng_ref, lnb_ref, ws_ref, bs_ref,
                  wr_ref, wm_ref, wg_ref, wo_ref, n2_ref, x1_ref, h2_ref):
    yr = _dot(yr_ref[...], wr_ref[...])
    ym = _dot(om_ref[...], wm_ref[...])
    z = jax.nn.gelu(uv_ref[...].astype(F32))
    u = z[:, :GMLP_W]
    v = z[:, GMLP_W:]
    mu = jnp.mean(v, axis=-1, keepdims=True)
    var = jnp.mean(jnp.square(v - mu), axis=-1, keepdims=True)
    vn = ((v - mu) * lax.rsqrt(var + EPS) * lng_ref[...] + lnb_ref[...]).astype(BF16)
    gw = GMLP_W // GMLP_GROUPS
    chunks = []
    for c in range(TM // GMLP_CHUNK):
        rows = slice(c * GMLP_CHUNK, (c + 1) * GMLP_CHUNK)
        groups = [_dot(ws_ref[g], vn[rows, g * gw:(g + 1) * gw]) + bs_ref[g] for g in range(GMLP_GROUPS)]
        chunks.append(jnp.concatenate(groups, axis=1))
    sv = jnp.concatenate(chunks, axis=0)
    yg = _dot((u * sv).astype(BF16), wg_ref[...])
    gate = jax.nn.sigmoid(mg_ref[...].astype(F32))
    y = gate[:, :D] * yr + gate[:, D:2 * D] * ym + gate[:, 2 * D:] * yg
    out = _dot(y.astype(BF16), wo_ref[...])
    x1 = x_ref[...] + mod_ref[:, 2 * D:3 * D] * out
    x1_ref[...] = x1
    h2_ref[...] = _modulated_rmsnorm(x1, n2_ref[...], mod_ref[:, 3 * D:4 * D], mod_ref[:, 4 * D:5 * D])


def _merge(xs, mod3, p, y_ret, o_mla, ln_g, ln_b, ws, bs_full, w_br_ret, w_br_mla, w_br_gmlp, w_out, n2_g,
           n_tiles, tile, mod_row):
    n_rows = xs.shape[0]
    const = lambda shape: pl.BlockSpec(shape, lambda t: (0,) * len(shape))
    row = lambda w, cb=0: pl.BlockSpec((TM, w), lambda t: (tile(t), cb))
    shp = jax.ShapeDtypeStruct((n_rows, D), F32)
    return pl.pallas_call(
        _merge_kernel,
        grid=(n_tiles,),
        in_specs=[row(D),
                  pl.BlockSpec((None, 1, 6 * D), lambda t: (mod_row(tile(t)), 0, 0)),
                  row(3 * D, C_MERGE // (3 * D)), row(D, C_UV // D),
                  row(RET_HEADS * RET_D), row(MLA_HEADS * MLA_V),
                  const((1, GMLP_W)), const((1, GMLP_W)),
                  const((GMLP_GROUPS, GMLP_CHUNK, GMLP_CHUNK)), const((GMLP_GROUPS, GMLP_CHUNK, GMLP_CHUNK)),
                  const((RET_HEADS * RET_D, D)), const((MLA_HEADS * MLA_V, D)), const((GMLP_W, D)),
                  const((D, D)), const((1, D))],
        out_specs=[row(D), row(D)],
        out_shape=[shp, shp],
        compiler_params=_cparams(1),
        name="merge",
    )(xs, mod3, p, p, y_ret, o_mla, ln_g, ln_b, ws, bs_full, w_br_ret, w_br_mla, w_br_gmlp, w_out, n2_g)


def _route_kernel(h_ref, r_ref, b_ref, sg_ref, su_ref, sd_ref, idx_ref, w_ref, rank_ref, cnt_ref, sh_ref, cnt_scr):
    @pl.when(pl.program_id(0) == 0)
    def _():
        cnt_scr[...] = jnp.zeros_like(cnt_scr)

    h = h_ref[...]
    logits = jnp.dot(h, r_ref[...], preferred_element_type=F32, precision=lax.Precision.HIGHEST)
    scores = jax.nn.sigmoid(logits)
    sel = scores + b_ref[...]
    lane_e = lax.broadcasted_iota(jnp.int32, (TM, N_EXPERTS), 1).astype(F32)
    lane_o = lax.broadcasted_iota(jnp.int32, (TM, 128), 1)
    idx_out = jnp.zeros((TM, 128), F32)
    w_out = jnp.zeros((TM, 128), F32)
    hits = []
    for k in range(TOP_K):
        best = jnp.max(sel, axis=-1, keepdims=True)
        pick = jnp.min(jnp.where(sel == best, lane_e, float(N_EXPERTS)), axis=-1, keepdims=True)
        hit = lane_e == pick
        hits.append(hit)
        wk = jnp.sum(jnp.where(hit, scores, 0.0), axis=-1, keepdims=True)
        sel = jnp.where(hit, -jnp.inf, sel)
        idx_out = jnp.where(lane_o == k, pick, idx_out)
        w_out = jnp.where(lane_o == k, wk, w_out)
    w_out = w_out / jnp.sum(w_out, axis=-1, keepdims=True) * ROUTED_SCALE
    idx_ref[...] = idx_out.astype(jnp.int32)
    w_ref[...] = w_out
    chosen = jnp.zeros((TM, N_EXPERTS), F32)
    for hit in hits:
        chosen = jnp.where(hit, 1.0, chosen)
    earlier = (lax.broadcasted_iota(jnp.int32, (TM, TM), 0) > lax.broadcasted_iota(jnp.int32, (TM, TM), 1))
    before = _dot(jnp.where(earlier, 1.0, 0.0).astype(BF16), chosen.astype(BF16)) + cnt_scr[...]
    rank_out = jnp.zeros((TM, 128), F32)
    for k, hit in enumerate(hits):
        rank_out = jnp.where(lane_o == k, jnp.sum(jnp.where(hit, before, 0.0), axis=-1, keepdims=True), rank_out)
    rank_ref[...] = rank_out.astype(jnp.int32)
    cnt_scr[...] += jnp.sum(chosen, axis=0, keepdims=True)
    cnt_ref[...] = cnt_scr[...]
    hb = h.astype(BF16)
    a = _silu(_dot(hb, sg_ref[...])) * _dot(hb, su_ref[...])
    sh_ref[...] = _dot(a.astype(BF16), sd_ref[...])


def _route(h2, router, bias, sg, su, sd, n_tiles, tile):
    const = lambda shape: pl.BlockSpec(shape, lambda t: (0,) * len(shape))
    n_act = n_tiles * TM
    return pl.pallas_call(
        _route_kernel,
        grid=(n_tiles,),
        in_specs=[pl.BlockSpec((TM, D), lambda t: (tile(t), 0)),
                  const((D, N_EXPERTS)), const((1, N_EXPERTS)),
                  const((D, D_EXPERT)), const((D, D_EXPERT)), const((D_EXPERT, D))],
        out_specs=[pl.BlockSpec((TM, 128), lambda t: (t, 0)),
                   pl.BlockSpec((TM, 128), lambda t: (t, 0)),
                   pl.BlockSpec((TM, 128), lambda t: (t, 0)),
                   pl.BlockSpec((1, N_EXPERTS), lambda t: (0, 0)),
                   pl.BlockSpec((TM, D), lambda t: (t, 0))],
        out_shape=[jax.ShapeDtypeStruct((n_act, 128), jnp.int32),
                   jax.ShapeDtypeStruct((n_act, 128), F32),
                   jax.ShapeDtypeStruct((n_act, 128), jnp.int32),
                   jax.ShapeDtypeStruct((1, N_EXPERTS), F32),
                   jax.ShapeDtypeStruct((n_act, D), F32)],
        scratch_shapes=[pltpu.VMEM((1, N_EXPERTS), F32)],
        compiler_params=_cparams(1),
        name="route_shared",
    )(h2, router, bias, sg, su, sd)


def _dispatch_kernel(pos_ref, h_ref, xs_hbm, sem):
    for k in range(TOP_K):
        def start(t, _, k=k):
            pltpu.make_async_copy(h_ref.at[pl.ds(t, 1)], xs_hbm.at[pl.ds(pos_ref[0, k * TM + t], 1)], sem).start()
            return 0

        lax.fori_loop(0, TM, start, 0, unroll=8)
    for k in range(TOP_K):
        pltpu.make_async_copy(h_ref, xs_hbm.at[pl.ds(0, TM)], sem).wait()


def _dispatch(pos_tiles, h2, n_tiles, tile, n_rows):
    return pl.pallas_call(
        _dispatch_kernel,
        grid=(n_tiles,),
        in_specs=[pl.BlockSpec((None, 1, TOP_K * TM), lambda t: (t, 0, 0), memory_space=pltpu.SMEM),
                  pl.BlockSpec((TM, D), lambda t: (tile(t), 0))],
        out_specs=pl.BlockSpec(memory_space=pl.ANY),
        out_shape=jax.ShapeDtypeStruct((n_rows, D), F32),
        scratch_shapes=[pltpu.SemaphoreType.DMA(())],
        compiler_params=_cparams(1),
        name="moe_dispatch",
    )(pos_tiles, h2)


def _expert_kernel(blk_e_ref, valid_ref, x_ref, wg_ref, wu_ref, wd_ref, y_ref):
    n_valid = valid_ref[pl.program_id(0)]

    @pl.when(n_valid > 0)
    def _():
        row = lax.broadcasted_iota(jnp.int32, (MOE_ROWS, 1), 0)
        x = jnp.where(row < n_valid, x_ref[...], 0.0).astype(BF16)
        hb = _silu(_dot(x, wg_ref[...].astype(BF16))) * _dot(x, wu_ref[...].astype(BF16))
        y_ref[...] = _dot(hb.astype(BF16), wd_ref[...].astype(BF16))

    @pl.when(n_valid == 0)
    def _():
        y_ref[...] = jnp.zeros_like(y_ref)


def _experts(blk_e, valid, xs, wg, wu, wd, n_blocks):
    grid_spec = pltpu.PrefetchScalarGridSpec(
        num_scalar_prefetch=2,
        grid=(n_blocks,),
        in_specs=[pl.BlockSpec((MOE_ROWS, D), lambda i, be, nv: (i, 0)),
                  pl.BlockSpec((None, D, D_EXPERT), lambda i, be, nv: (be[i], 0, 0)),
                  pl.BlockSpec((None, D, D_EXPERT), lambda i, be, nv: (be[i], 0, 0)),
                  pl.BlockSpec((None, D_EXPERT, D), lambda i, be, nv: (be[i], 0, 0))],
        out_specs=pl.BlockSpec((MOE_ROWS, D), lambda i, be, nv: (i, 0)),
    )
    return pl.pallas_call(
        _expert_kernel,
        grid_spec=grid_spec,
        out_shape=jax.ShapeDtypeStruct((n_blocks * MOE_ROWS, D), F32),
        compiler_params=_cparams(1),
        name="routed_experts",
    )(blk_e, valid, xs, wg, wu, wd)


def _combine_kernel(pos_ref, y_hbm, w_ref, x1_ref, sh_ref, mod_ref, o_ref, buf, sem):
    for k in range(TOP_K):
        def start(t, _, k=k):
            pltpu.make_async_copy(y_hbm.at[pl.ds(pos_ref[0, k * TM + t], 1)], buf.at[pl.ds(k * TM + t, 1)], sem).start()
            return 0

        lax.fori_loop(0, TM, start, 0, unroll=8)
    for k in range(TOP_K):
        pltpu.make_async_copy(y_hbm.at[pl.ds(0, TM)], buf.at[pl.ds(k * TM, TM)], sem).wait()
    f = sh_ref[...]
    for k in range(TOP_K):
        f = f + buf[k * TM:(k + 1) * TM, :] * w_ref[:, k:k + 1]
    o_ref[...] = x1_ref[...] + mod_ref[:, 5 * D:6 * D] * f


def _combine(pos_tiles, ys, w, x1, shared, mod3, n_tiles, tile, mod_row, out_rows, out_tile):
    return pl.pallas_call(
        _combine_kernel,
        grid=(n_tiles,),
        in_specs=[pl.BlockSpec((None, 1, TOP_K * TM), lambda t: (t, 0, 0), memory_space=pltpu.SMEM),
                  pl.BlockSpec(memory_space=pl.ANY),
                  pl.BlockSpec((TM, 128), lambda t: (t, 0)),
                  pl.BlockSpec((TM, D), lambda t: (tile(t), 0)),
                  pl.BlockSpec((TM, D), lambda t: (t, 0)),
                  pl.BlockSpec((None, 1, 6 * D), lambda t: (mod_row(tile(t)), 0, 0))],
        out_specs=pl.BlockSpec((TM, D), lambda t: (out_tile(t), 0)),
        out_shape=jax.ShapeDtypeStruct((out_rows, D), F32),
        scratch_shapes=[pltpu.VMEM((TOP_K * TM, D), F32), pltpu.SemaphoreType.DMA(())],
        compiler_params=_cparams(1),
        name="moe_combine",
    )(pos_tiles, ys, w, x1, shared, mod3)


def _moe_plan(idx, rank, counts, n_blocks):
    n = idx.shape[0]
    cnt = counts.reshape(N_EXPERTS).astype(jnp.int32)
    padded = (cnt + MOE_ROWS - 1) // MOE_ROWS * MOE_ROWS
    pad_end = jnp.cumsum(padded)
    pad_start = pad_end - padded
    experts = jnp.arange(N_EXPERTS, dtype=jnp.int32)
    pos = rank + jnp.sum(jnp.where(idx[:, :, None] == experts, pad_start, 0), axis=-1)
    blk_start = jnp.arange(n_blocks, dtype=jnp.int32) * MOE_ROWS
    blk_e = jnp.minimum(jnp.sum(blk_start[:, None] >= pad_end[None, :], axis=1), N_EXPERTS - 1).astype(jnp.int32)
    mine = blk_e[:, None] == experts
    in_expert = blk_start - jnp.sum(jnp.where(mine, pad_start, 0), axis=1)
    valid = jnp.clip(jnp.sum(jnp.where(mine, cnt, 0), axis=1) - in_expert, 0, MOE_ROWS).astype(jnp.int32)
    pos_tiles = pos.reshape(n // TM, TM, TOP_K).transpose(0, 2, 1).reshape(n // TM, 1, TOP_K * TM)
    return blk_e, valid, pos_tiles.astype(jnp.int32)


def _rope_tables(seq, ctx):
    half = MLA_ROPE // 2
    n_freq = half // 2
    inv = ROPE_THETA ** (-2.0 * jnp.arange(n_freq, dtype=F32) / half)
    t = jnp.arange(seq)
    ang_r = (t // GRID_W).astype(F32)[:, None] * inv
    ang_c = (t % GRID_W).astype(F32)[:, None] * inv
    cos = jnp.concatenate([jnp.cos(ang_r), jnp.cos(ang_r), jnp.cos(ang_c), jnp.cos(ang_c)], axis=1)
    sin = jnp.concatenate([-jnp.sin(ang_r), jnp.sin(ang_r), -jnp.sin(ang_c), jnp.sin(ang_c)], axis=1)
    pad_l = MLA_NOPE
    pad_r = HEAD_PAD - MLA_NOPE - MLA_ROPE
    cos = jnp.pad(cos, ((0, ctx), (pad_l, pad_r)), constant_values=1.0)
    cos = cos.at[seq:, :].set(1.0)
    sin = jnp.pad(sin, ((0, ctx), (pad_l, pad_r)))
    return cos, sin


def _pad_heads(w, n_heads, width, offset=0):
    k = w.shape[0]
    w = w.reshape(k, n_heads, width)
    w = jnp.pad(w, ((0, 0), (0, 0), (offset, HEAD_PAD - width - offset)))
    return w.reshape(k, n_heads * HEAD_PAD)


def _pad_vec(g, offset):
    return jnp.pad(g, (offset, HEAD_PAD - g.shape[0] - offset)).reshape(1, HEAD_PAD)


def _reorder_w_in(w):
    off_cq, off_ckv, off_kr, off_uv, off_merge = 2048, 2432, 2688, 2720, 3744
    kr = jnp.pad(w[:, off_kr:off_uv], ((0, 0), (MLA_NOPE, HEAD_PAD - MLA_NOPE - MLA_ROPE)))
    return jnp.concatenate([w[:, off_merge:], w[:, off_uv:off_merge], w[:, :off_cq],
                            w[:, off_ckv:off_kr], kr, w[:, off_cq:off_ckv]], axis=1).astype(BF16)


def kernel(x, c, ctx, c_ctx, ada_w, ada_b, norm1_g, norm2_g, w_in, ret_decay_fwd, ret_decay_bwd, ret_gn_g,
           ret_gn_b, w_br_ret, mla_qa_g, mla_w_uq, mla_kva_g, mla_w_ukv, mla_qn_g, mla_kn_g, mla_kr_g, w_br_mla,
           gmlp_ln_g, gmlp_ln_b, gmlp_ws, gmlp_bs, w_br_gmlp, w_out, moe_router, moe_bias, moe_w_gate, moe_w_up,
           moe_w_down, sh_w_gate, sh_w_up, sh_w_down):
    B, seq, _ = x.shape
    n_ctx = ctx.shape[1]
    depth = ada_w.shape[0]
    assert n_ctx == TM and seq % (ATT_KV_CHUNK * ATT_UNROLL) == 0 and seq % TM == 0
    lt = seq + n_ctx
    tiles_per_b = lt // TM
    lat_tiles_per_b = seq // TM
    ctx_tile = lat_tiles_per_b

    def mod_row(t):
        return jnp.where(t % tiles_per_b == ctx_tile, B, t // tiles_per_b)

    c_rows = jnp.concatenate([c, c_ctx[None, :], jnp.zeros((8 - B - 1, D), F32)], axis=0)
    mod = _ada(c_rows, ada_w, ada_b)
    cos_t, sin_t = _rope_tables(seq, n_ctx)
    xs = jnp.concatenate([x, ctx], axis=1).reshape(B * lt, D)

    for l in range(depth):
        last = l == depth - 1
        mod3 = mod[l].reshape(8, 1, 6 * D)
        p = _in_proj(xs, mod3, norm1_g[l].reshape(1, D), _reorder_w_in(w_in[l]), B * tiles_per_b, mod_row)

        lg = jnp.stack([jax.nn.log_sigmoid(ret_decay_fwd[l].astype(F32)),
                        jax.nn.log_sigmoid(ret_decay_bwd[l].astype(F32))])
        y_ret = _retention(p.reshape(B, lt, N_IN_PAD), lg, ret_gn_g[l].reshape(1, -1), ret_gn_b[l].reshape(1, -1),
                           seq, n_ctx)

        w_ukv = mla_w_ukv[l].reshape(MLA_KV_LORA, MLA_HEADS, MLA_NOPE + MLA_V)
        wk_p = _pad_heads(w_ukv[:, :, :MLA_NOPE].reshape(MLA_KV_LORA, -1), MLA_HEADS, MLA_NOPE).astype(BF16)
        wv = jnp.pad(w_ukv[:, :, MLA_NOPE:].reshape(MLA_KV_LORA, MLA_HEADS // 2, 2, MLA_V),
                     ((0, 0), (0, 0), (0, 0), (0, MLA_V)))
        wv = wv.at[:, :, 1, :].set(jnp.roll(wv[:, :, 1, :], MLA_V, axis=-1))
        wv_p = wv.reshape(MLA_KV_LORA, MLA_HEADS * HEAD_PAD).astype(BF16)
        wq_p = _pad_heads(mla_w_uq[l], MLA_HEADS, MLA_QK).astype(BF16)
        q, k, v = _mla_prep(p, cos_t, sin_t, mla_qa_g[l].reshape(1, -1), mla_kva_g[l].reshape(1, -1),
                            _pad_vec(mla_qn_g[l], 0), _pad_vec(mla_kn_g[l], 0), _pad_vec(mla_kr_g[l], MLA_NOPE),
                            wq_p, wk_p, wv_p, B, lt)
        o_mla = _attention(q, k, v, seq, n_ctx, lat_tiles_per_b if last else tiles_per_b)

        if last:
            n_tiles = B * lat_tiles_per_b
            tile = lambda t: (t // lat_tiles_per_b) * tiles_per_b + t % lat_tiles_per_b
        else:
            n_tiles = B * tiles_per_b
            tile = lambda t: t
        bs_full = jnp.broadcast_to(gmlp_bs[l][:, :, None], (GMLP_GROUPS, GMLP_CHUNK, GMLP_CHUNK))
        x1, h2 = _merge(xs, mod3, p, y_ret.reshape(B * lt, -1), o_mla.reshape(B * lt, -1),
                        gmlp_ln_g[l].reshape(1, -1), gmlp_ln_b[l].reshape(1, -1), gmlp_ws[l].astype(BF16), bs_full,
                        w_br_ret[l].astype(BF16), w_br_mla[l].astype(BF16), w_br_gmlp[l].astype(BF16),
                        w_out[l].astype(BF16), norm2_g[l].reshape(1, D), n_tiles, tile, mod_row)

        idx, w, rank, counts, shared = _route(h2, moe_router[l], moe_bias[l].reshape(1, -1),
                                              sh_w_gate[l].astype(BF16), sh_w_up[l].astype(BF16),
                                              sh_w_down[l].astype(BF16), n_tiles, tile)
        n_act = n_tiles * TM
        n_blocks = -(-(n_act * TOP_K + N_EXPERTS * (MOE_ROWS - 1)) // MOE_ROWS)
        blk_e, valid, pos_tiles = _moe_plan(idx[:, :TOP_K], rank[:, :TOP_K], counts, n_blocks)
        xg = _dispatch(pos_tiles, h2, n_tiles, tile, n_blocks * MOE_ROWS)
        ys = _experts(blk_e, valid, xg, moe_w_gate[l], moe_w_up[l], moe_w_down[l], n_blocks)
        if last:
            xs = _combine(pos_tiles, ys, w, x1, shared, mod3, n_tiles, tile, mod_row, B * seq, lambda t: t)
        else:
            xs = _combine(pos_tiles, ys, w, x1, shared, mod3, n_tiles, tile, mod_row, B * lt, tile)
    return xs.reshape(B, seq, D)
```

```python
import functools

import jax
import jax.numpy as jnp
from jax import lax
from jax.experimental import pallas as pl
from jax.experimental.pallas import tpu as pltpu

F32 = jnp.float32
BF16 = jnp.bfloat16

D = 1024
GRID_W = 64
RET_HEADS = 4
RET_D = 128
RET_CHUNK = 128
MLA_HEADS = 8
MLA_Q_LORA = 384
MLA_KV_LORA = 256
MLA_NOPE = 64
MLA_ROPE = 32
MLA_V = 64
MLA_V_EXT = MLA_V + 16
MLA_QK = MLA_NOPE + MLA_ROPE
HEAD_PAD = 128
ROPE_THETA = 10000.0
GMLP_GROUPS = 4
GMLP_W = 512
GMLP_CHUNK = 128
N_EXPERTS = 64
TOP_K = 6
D_EXPERT = 256
ROUTED_SCALE = 2.5
EPS = 1e-6
LOG2_E = 1.4426950408889634

TM = 256
MOE_ROWS = 256
ATT_KV_CHUNK = 512
ATT_UNROLL = 4

C_MERGE = 0
C_UV = 3072
C_RET = 4096
C_CKV = 6144
C_KR = 6400
C_CQ = 6528
N_IN_PAD = 6912
IN_CHUNK = 768

VMEM_LIMIT = 56 * 1024 * 1024


def _cparams(n_axes, vmem=VMEM_LIMIT):
    return pltpu.CompilerParams(dimension_semantics=("arbitrary",) * n_axes, vmem_limit_bytes=vmem)


def _silu(x):
    return x * jax.nn.sigmoid(x)


def _dot(a, b):
    return jnp.dot(a, b, preferred_element_type=F32)


def _dot_nt(a, b):
    return lax.dot_general(a, b, (((1,), (1,)), ((), ())), preferred_element_type=F32)


def _dot_tn(a, b):
    return lax.dot_general(a, b, (((0,), (0,)), ((), ())), preferred_element_type=F32)


def _ada_kernel(c_ref, w_ref, b_ref, o_ref):
    s = _silu(c_ref[...])
    o_ref[...] = _dot(s.astype(BF16), w_ref[...].astype(BF16)) + b_ref[...]


def _ada(c_rows, ada_w, ada_b):
    depth = ada_w.shape[0]
    n = ada_w.shape[2]
    cw = 1536
    return pl.pallas_call(
        _ada_kernel,
        grid=(depth, n // cw),
        in_specs=[pl.BlockSpec((8, D), lambda l, j: (0, 0)),
                  pl.BlockSpec((None, D, cw), lambda l, j: (l, 0, j)),
                  pl.BlockSpec((None, 1, cw), lambda l, j: (l, 0, j))],
        out_specs=pl.BlockSpec((None, 8, cw), lambda l, j: (l, 0, j)),
        out_shape=jax.ShapeDtypeStruct((depth, 8, n), F32),
        compiler_params=_cparams(2),
        name="ada_mod",
    )(c_rows, ada_w, ada_b.reshape(depth, 1, n))


def _modulated_rmsnorm(x, g, shift, scale):
    y = x * lax.rsqrt(jnp.mean(x * x, axis=-1, keepdims=True) + EPS) * g
    return y * (1.0 + scale) + shift


def _in_proj_kernel(x_ref, mod_ref, g_ref, w_ref, o_ref, h_scr):
    h = _modulated_rmsnorm(x_ref[...], g_ref[...], mod_ref[:, 0:D], mod_ref[:, D:2 * D])
    h_scr[...] = h.astype(BF16)
    for c in range(N_IN_PAD // IN_CHUNK):
        cols = slice(c * IN_CHUNK, (c + 1) * IN_CHUNK)
        o_ref[:, cols] = _dot(h_scr[...], w_ref[:, cols]).astype(BF16)


def _in_proj(xs, mod3, g, w_in_r, n_tiles, mod_row):
    n_rows = xs.shape[0]
    return pl.pallas_call(
        _in_proj_kernel,
        grid=(n_tiles,),
        in_specs=[pl.BlockSpec((TM, D), lambda t: (t, 0)),
                  pl.BlockSpec((None, 1, 6 * D), lambda t: (mod_row(t), 0, 0)),
                  pl.BlockSpec((1, D), lambda t: (0, 0)),
                  pl.BlockSpec((D, N_IN_PAD), lambda t: (0, 0))],
        out_specs=pl.BlockSpec((TM, N_IN_PAD), lambda t: (t, 0)),
        out_shape=jax.ShapeDtypeStruct((n_rows, N_IN_PAD), BF16),
        scratch_shapes=[pltpu.VMEM((TM, D), BF16)],
        compiler_params=_cparams(1),
        name="in_proj",
    )(xs, mod3, g, w_in_r)


def _retention_kernel(lg_ref, q_ref, k_ref, v_ref, g_ref, gng_ref, gnb_ref, y_ref, o_scr,
                      *, n_lat_chunks, n_ctx_chunks):
    h = pl.program_id(1)
    lg_f = lg_ref[0, h]
    lg_b = lg_ref[1, h]
    C = RET_CHUNK
    k_scale = RET_D ** -0.5
    ri = lax.broadcasted_iota(jnp.int32, (C, C), 0).astype(F32)
    ci = lax.broadcasted_iota(jnp.int32, (C, C), 1).astype(F32)
    pos = lax.broadcasted_iota(jnp.int32, (C, 1), 0).astype(F32)
    diff = ri - ci
    d_f = jnp.where(diff >= 0, jnp.exp(lg_f * jnp.maximum(diff, 0.0)), 0.0) * k_scale
    d_b = jnp.where(diff < 0, jnp.exp(lg_b * jnp.maximum(-diff, 0.0)), 0.0) * k_scale
    qdec_f = jnp.exp(lg_f * (pos + 1.0))
    kdec_f = jnp.exp(lg_f * (C - 1.0 - pos)) * k_scale
    cdec_f = jnp.exp(lg_f * C)
    qdec_b = jnp.exp(lg_b * (C - pos))
    kdec_b = jnp.exp(lg_b * pos) * k_scale
    cdec_b = jnp.exp(lg_b * C)

    def chunk(c, state, dmat, qdec, kdec, cdec):
        rows = pl.ds(pl.multiple_of(c * C, C), C)
        q = q_ref[rows, :]
        k = k_ref[rows, :]
        v = v_ref[rows, :]
        att = (_dot_nt(q, k) * dmat).astype(BF16)
        o = _dot(att, v) + _dot((q.astype(F32) * qdec).astype(BF16), state.astype(BF16))
        kd = (k.astype(F32) * kdec).astype(BF16)
        return rows, o, state * cdec + _dot_tn(kd, v)

    def fwd_body(c, state):
        rows, o, state = chunk(c, state, d_f, qdec_f, kdec_f, cdec_f)
        o_scr[rows, :] = o
        return state

    def bwd_body(c, state):
        rows, o, state = chunk(c, state, d_b, qdec_b, kdec_b, cdec_b)
        o = o + o_scr[rows, :]
        mu = jnp.mean(o, axis=-1, keepdims=True)
        var = jnp.mean(jnp.square(o - mu), axis=-1, keepdims=True)
        on = (o - mu) * lax.rsqrt(var + EPS)
        y = _silu(g_ref[rows, :].astype(F32)) * (on * gng_ref[...] + gnb_ref[...])
        y_ref[rows, :] = y.astype(BF16)
        return state

    zero = jnp.zeros((RET_D, RET_D), F32)
    n_all = n_lat_chunks + n_ctx_chunks
    s = lax.fori_loop(n_lat_chunks, n_all, fwd_body, zero)
    lax.fori_loop(0, n_lat_chunks, fwd_body, s)
    s = lax.fori_loop(0, n_ctx_chunks, lambda i, st: bwd_body(n_all - 1 - i, st), zero)
    lax.fori_loop(0, n_lat_chunks, lambda i, st: bwd_body(n_lat_chunks - 1 - i, st), s)


def _retention(p3, lg, gn_g, gn_b, seq, ctx):
    B, lt, _ = p3.shape
    base = C_RET // RET_D
    kern = functools.partial(_retention_kernel, n_lat_chunks=seq // RET_CHUNK, n_ctx_chunks=ctx // RET_CHUNK)

    def col(off):
        return pl.BlockSpec((None, lt, RET_D), lambda b, h: (b, 0, base + off * RET_HEADS + h))

    return pl.pallas_call(
        kern,
        grid=(B, RET_HEADS),
        in_specs=[pl.BlockSpec(memory_space=pltpu.SMEM),
                  col(0), col(1), col(2), col(3),
                  pl.BlockSpec((1, RET_D), lambda b, h: (0, h)),
                  pl.BlockSpec((1, RET_D), lambda b, h: (0, h))],
        out_specs=pl.BlockSpec((None, lt, RET_D), lambda b, h: (b, 0, h)),
        out_shape=jax.ShapeDtypeStruct((B, lt, RET_HEADS * RET_D), BF16),
        scratch_shapes=[pltpu.VMEM((lt, RET_D), F32)],
        compiler_params=_cparams(2),
        name="retention",
    )(lg, p3, p3, p3, p3, gn_g, gn_b)


def _rope_rotate(x, first_half):
    return jnp.where(first_half, pltpu.roll(x, HEAD_PAD - 8, 1), pltpu.roll(x, 8, 1))


def _mla_prep_kernel(cq_ref, ckv_ref, kr_ref, cos_ref, sin_ref, qa_ref, kva_ref, qn_ref, kn_ref, krg_ref,
                     wq_ref, wk_ref, wv_ref, q_ref, k_ref, v_ref):
    lane = lax.broadcasted_iota(jnp.int32, (1, HEAD_PAD), 1)
    first_half = (lane % 16) < 8
    cos = cos_ref[...]
    sin = sin_ref[...]

    def rms(x, n):
        return x * lax.rsqrt(jnp.sum(x * x, axis=-1, keepdims=True) * (1.0 / n) + EPS)

    def rope(x):
        return x * cos + _rope_rotate(x, first_half) * sin

    cq = cq_ref[...].astype(F32)
    cqn = (rms(cq, MLA_Q_LORA) * qa_ref[...]).astype(BF16)
    q_all = _dot(cqn, wq_ref[...])
    ckv = ckv_ref[...].astype(F32)
    ckvn = (rms(ckv, MLA_KV_LORA) * kva_ref[...]).astype(BF16)
    k_all = _dot(ckvn, wk_ref[...])
    v_all = _dot(ckvn, wv_ref[...])
    k_rope = rope(rms(kr_ref[...].astype(F32), MLA_ROPE) * krg_ref[...])
    scale = MLA_QK ** -0.5 * LOG2_E
    v_t = v_all.T
    ones_row = jnp.where(lax.broadcasted_iota(jnp.int32, (MLA_V_EXT - MLA_V, TM), 0) == 0, 1.0, 0.0)
    for h in range(MLA_HEADS):
        cols = slice(h * HEAD_PAD, (h + 1) * HEAD_PAD)
        qh = rope(rms(q_all[:, cols], MLA_QK) * qn_ref[...]) * scale
        q_ref[h] = qh.astype(BF16)
        kh = rms(k_all[:, cols], MLA_NOPE) * kn_ref[...] + k_rope
        k_ref[h] = kh.astype(BF16)
        v_ref[h] = jnp.concatenate([v_t[h * MLA_V:(h + 1) * MLA_V, :], ones_row], axis=0).astype(BF16)


def _mla_prep(p, cos_t, sin_t, qa_g, kva_g, qn_p, kn_p, kr_p, wq_p, wk_p, wv, B, lt):
    tiles_per_b = lt // TM
    hw = MLA_HEADS * HEAD_PAD
    const = lambda shape: pl.BlockSpec(shape, lambda b, j: (0,) * len(shape))
    head_out = pl.BlockSpec((None, MLA_HEADS, TM, HEAD_PAD), lambda b, j: (b, 0, j, 0))
    shp = jax.ShapeDtypeStruct((B, MLA_HEADS, lt, HEAD_PAD), BF16)
    v_out = pl.BlockSpec((None, MLA_HEADS, None, MLA_V_EXT, TM), lambda b, j: (b, 0, j, 0, 0))
    v_shp = jax.ShapeDtypeStruct((B, MLA_HEADS, tiles_per_b, MLA_V_EXT, TM), BF16)
    return pl.pallas_call(
        _mla_prep_kernel,
        grid=(B, tiles_per_b),
        in_specs=[pl.BlockSpec((TM, MLA_Q_LORA), lambda b, j: (b * tiles_per_b + j, C_CQ // MLA_Q_LORA)),
                  pl.BlockSpec((TM, MLA_KV_LORA), lambda b, j: (b * tiles_per_b + j, C_CKV // MLA_KV_LORA)),
                  pl.BlockSpec((TM, HEAD_PAD), lambda b, j: (b * tiles_per_b + j, C_KR // HEAD_PAD)),
                  pl.BlockSpec((TM, HEAD_PAD), lambda b, j: (j, 0)),
                  pl.BlockSpec((TM, HEAD_PAD), lambda b, j: (j, 0)),
                  const((1, MLA_Q_LORA)), const((1, MLA_KV_LORA)),
                  const((1, HEAD_PAD)), const((1, HEAD_PAD)), const((1, HEAD_PAD)),
                  const((MLA_Q_LORA, hw)), const((MLA_KV_LORA, hw)), const((MLA_KV_LORA, MLA_HEADS * MLA_V))],
        out_specs=[head_out, head_out, v_out],
        out_shape=[shp, shp, v_shp],
        compiler_params=_cparams(2),
        name="mla_prep",
    )(p, p, p, cos_t, sin_t, qa_g, kva_g, qn_p, kn_p, kr_p, wq_p, wk_p, wv)


def _attention_kernel(q_ref, k_ref, v_ref, o_ref, s_scr, *, seq, ctx, ctx_tile):
    i = pl.program_id(2)
    n_blk = ATT_KV_CHUNK // TM
    n_chunks = seq // ATT_KV_CHUNK

    def scores(hh, slot, blk, nb):
        start = blk * TM if isinstance(blk, int) else pl.multiple_of(blk * TM, TM)
        s_scr[hh, slot, 0:nb * TM, :] = _dot_nt(k_ref[hh, pl.ds(start, nb * TM), :], q_ref[hh])

    def absorb(hh, slot, blk, nb, carry):
        m, acc = carry
        s = s_scr[hh, slot, 0:nb * TM, :]
        m_new = jnp.maximum(m, jnp.max(s, axis=0, keepdims=True))
        p = jnp.exp2(s - m_new).astype(BF16)
        acc = jnp.exp2(m - m_new) * acc
        for j in range(nb):
            acc = acc + _dot(v_ref[hh, blk + j], p[j * TM:(j + 1) * TM, :])
        return m_new, acc

    def init():
        return (jnp.full((1, TM), -jnp.inf, F32), jnp.zeros((MLA_V_EXT, TM), F32))

    def write(carries):
        outs = [acc[0:MLA_V, :] / acc[MLA_V:MLA_V + 1, :] for _, acc in carries]
        o_ref[...] = jnp.concatenate(outs, axis=0).T.astype(BF16)

    def step(carries, slot, blk, nb, next_blk, next_nb):
        out = []
        for hh in range(2):
            if next_blk is not None:
                scores(hh, 1 - slot, next_blk, next_nb)
            out.append(absorb(hh, slot, blk, nb, carries[hh]))
        return tuple(out)

    ctx_blk = seq // TM
    ctx_nb = ctx // TM

    @pl.when(i != ctx_tile)
    def _():
        for hh in range(2):
            scores(hh, 0, ctx_blk, ctx_nb)
        carries = step((init(), init()), 0, ctx_blk, ctx_nb, 0, n_blk)
        last_blk = (n_chunks - 1) * n_blk

        def body(c, carries):
            for u in range(ATT_UNROLL):
                blk = (c * ATT_UNROLL + u) * n_blk
                carries = step(carries, (1 + u) % 2, blk, n_blk, jnp.minimum(blk + n_blk, last_blk), n_blk)
            return carries

        write(lax.fori_loop(0, n_chunks // ATT_UNROLL, body, carries))

    @pl.when(i == ctx_tile)
    def _():
        for hh in range(2):
            scores(hh, 0, ctx_blk, ctx_nb)
        write(step((init(), init()), 0, ctx_blk, ctx_nb, None, None))


def _attention(q, k, v, seq, ctx, n_q_tiles):
    B, H, lt, _ = q.shape
    kern = functools.partial(_attention_kernel, seq=seq, ctx=ctx, ctx_tile=seq // TM)
    return pl.pallas_call(
        kern,
        grid=(B, H // 2, n_q_tiles),
        in_specs=[pl.BlockSpec((None, 2, TM, HEAD_PAD), lambda b, h, i: (b, h, i, 0)),
                  pl.BlockSpec((None, 2, lt, HEAD_PAD), lambda b, h, i: (b, h, 0, 0)),
                  pl.BlockSpec((None, 2, lt // TM, MLA_V_EXT, TM), lambda b, h, i: (b, h, 0, 0, 0))],
        out_specs=pl.BlockSpec((None, TM, HEAD_PAD), lambda b, h, i: (b, i, h)),
        out_shape=jax.ShapeDtypeStruct((B, lt, (H // 2) * HEAD_PAD), BF16),
        scratch_shapes=[pltpu.VMEM((2, 2, ATT_KV_CHUNK, TM), F32)],
        compiler_params=_cparams(3),
        name="attention",
    )(q, k, v)


def _merge_kernel(x_ref, mod_ref, mg_ref, uv_ref, yr_ref, om_ref, lng_ref, lnb_ref, ws_ref, bs_ref,
                  wr_ref, wm_ref, wg_ref, wo_ref, n2_ref, x1_ref, h2_ref):
    yr = _dot(yr_ref[...], wr_ref[...])
    ym = _dot(om_ref[...], wm_ref[...])
    z = jax.nn.gelu(uv_ref[...].astype(F32))
    u = z[:, :GMLP_W]
    v = z[:, GMLP_W:]
    mu = jnp.mean(v, axis=-1, keepdims=True)
    var = jnp.mean(jnp.square(v - mu), axis=-1, keepdims=True)
    vn = ((v - mu) * lax.rsqrt(var + EPS) * lng_ref[...] + lnb_ref[...]).astype(BF16)
    gw = GMLP_W // GMLP_GROUPS
    chunks = []
    for c in range(TM // GMLP_CHUNK):
        rows = slice(c * GMLP_CHUNK, (c + 1) * GMLP_CHUNK)
        groups = [_dot(ws_ref[g], vn[rows, g * gw:(g + 1) * gw]) + bs_ref[g] for g in range(GMLP_GROUPS)]
        chunks.append(jnp.concatenate(groups, axis=1))
    sv = jnp.concatenate(chunks, axis=0)
    yg = _dot((u * sv).astype(BF16), wg_ref[...])
    gate = jax.nn.sigmoid(mg_ref[...].astype(F32))
    y = gate[:, :D] * yr + gate[:, D:2 * D] * ym + gate[:, 2 * D:] * yg
    out = _dot(y.astype(BF16), wo_ref[...])
    x1 = x_ref[...] + mod_ref[:, 2 * D:3 * D] * out
    x1_ref[...] = x1
    h2_ref[...] = _modulated_rmsnorm(x1, n2_ref[...], mod_ref[:, 3 * D:4 * D], mod_ref[:, 4 * D:5 * D])


def _merge(xs, mod3, p, y_ret, o_mla, ln_g, ln_b, ws, bs_full, w_br_ret, w_br_mla, w_br_gmlp, w_out, n2_g,
           n_tiles, tile, mod_row):
    n_rows = xs.shape[0]
    const = lambda shape: pl.BlockSpec(shape, lambda t: (0,) * len(shape))
    row = lambda w, cb=0: pl.BlockSpec((TM, w), lambda t: (tile(t), cb))
    shp = jax.ShapeDtypeStruct((n_rows, D), F32)
    return pl.pallas_call(
        _merge_kernel,
        grid=(n_tiles,),
        in_specs=[row(D),
                  pl.BlockSpec((None, 1, 6 * D), lambda t: (mod_row(tile(t)), 0, 0)),
                  row(3 * D, C_MERGE // (3 * D)), row(D, C_UV // D),
                  row(RET_HEADS * RET_D), row(MLA_HEADS * MLA_V),
                  const((1, GMLP_W)), const((1, GMLP_W)),
                  const((GMLP_GROUPS, GMLP_CHUNK, GMLP_CHUNK)), const((GMLP_GROUPS, GMLP_CHUNK, GMLP_CHUNK)),
                  const((RET_HEADS * RET_D, D)), const((MLA_HEADS * MLA_V, D)), const((GMLP_W, D)),
                  const((D, D)), const((1, D))],
        out_specs=[row(D), row(D)],
        out_shape=[shp, shp],
        compiler_params=_cparams(1),
        name="merge",
    )(xs, mod3, p, p, y_ret, o_mla, ln_g, ln_b, ws, bs_full, w_br_ret, w_br_mla, w_br_gmlp, w_out, n2_g)


def _route_kernel(h_ref, r_ref, b_ref, sg_ref, su_ref, sd_ref, idx_ref, w_ref, rank_ref, cnt_ref, sh_ref, cnt_scr):
    @pl.when(pl.program_id(0) == 0)
    def _():
        cnt_scr[...] = jnp.zeros_like(cnt_scr)

    h = h_ref[...]
    logits = jnp.dot(h, r_ref[...], preferred_element_type=F32, precision=lax.Precision.HIGHEST)
    scores = jax.nn.sigmoid(logits)
    sel = scores + b_ref[...]
    lane_e = lax.broadcasted_iota(jnp.int32, (TM, N_EXPERTS), 1).astype(F32)
    lane_o = lax.broadcasted_iota(jnp.int32, (TM, 128), 1)
    idx_out = jnp.zeros((TM, 128), F32)
    w_out = jnp.zeros((TM, 128), F32)
    hits = []
    for k in range(TOP_K):
        best = jnp.max(sel, axis=-1, keepdims=True)
        pick = jnp.min(jnp.where(sel == best, lane_e, float(N_EXPERTS)), axis=-1, keepdims=True)
        hit = lane_e == pick
        hits.append(hit)
        wk = jnp.sum(jnp.where(hit, scores, 0.0), axis=-1, keepdims=True)
        sel = jnp.where(hit, -jnp.inf, sel)
        idx_out = jnp.where(lane_o == k, pick, idx_out)
        w_out = jnp.where(lane_o == k, wk, w_out)
    w_out = w_out / jnp.sum(w_out, axis=-1, keepdims=True) * ROUTED_SCALE
    idx_ref[...] = idx_out.astype(jnp.int32)
    w_ref[...] = w_out
    chosen = jnp.zeros((TM, N_EXPERTS), F32)
    for hit in hits:
        chosen = jnp.where(hit, 1.0, chosen)
    earlier = (lax.broadcasted_iota(jnp.int32, (TM, TM), 0) > lax.broadcasted_iota(jnp.int32, (TM, TM), 1))
    before = _dot(jnp.where(earlier, 1.0, 0.0).astype(BF16), chosen.astype(BF16)) + cnt_scr[...]
    rank_out = jnp.zeros((TM, 128), F32)
    for k, hit in enumerate(hits):
        rank_out = jnp.where(lane_o == k, jnp.sum(jnp.where(hit, before, 0.0), axis=-1, keepdims=True), rank_out)
    rank_ref[...] = rank_out.astype(jnp.int32)
    cnt_scr[...] += jnp.sum(chosen, axis=0, keepdims=True)
    cnt_ref[...] = cnt_scr[...]
    hb = h.astype(BF16)
    a = _silu(_dot(hb, sg_ref[...])) * _dot(hb, su_ref[...])
    sh_ref[...] = _dot(a.astype(BF16), sd_ref[...])


def _route(h2, router, bias, sg, su, sd, n_tiles, tile):
    const = lambda shape: pl.BlockSpec(shape, lambda t: (0,) * len(shape))
    n_act = n_tiles * TM
    return pl.pallas_call(
        _route_kernel,
        grid=(n_tiles,),
        in_specs=[pl.BlockSpec((TM, D), lambda t: (tile(t), 0)),
                  const((D, N_EXPERTS)), const((1, N_EXPERTS)),
                  const((D, D_EXPERT)), const((D, D_EXPERT)), const((D_EXPERT, D))],
        out_specs=[pl.BlockSpec((TM, 128), lambda t: (t, 0)),
                   pl.BlockSpec((TM, 128), lambda t: (t, 0)),
                   pl.BlockSpec((TM, 128), lambda t: (t, 0)),
                   pl.BlockSpec((1, N_EXPERTS), lambda t: (0, 0)),
                   pl.BlockSpec((TM, D), lambda t: (t, 0))],
        out_shape=[jax.ShapeDtypeStruct((n_act, 128), jnp.int32),
                   jax.ShapeDtypeStruct((n_act, 128), F32),
                   jax.ShapeDtypeStruct((n_act, 128), jnp.int32),
                   jax.ShapeDtypeStruct((1, N_EXPERTS), F32),
                   jax.ShapeDtypeStruct((n_act, D), F32)],
        scratch_shapes=[pltpu.VMEM((1, N_EXPERTS), F32)],
        compiler_params=_cparams(1),
        name="route_shared",
    )(h2, router, bias, sg, su, sd)


def _dispatch_kernel(pos_ref, h_ref, xs_hbm, sem):
    for k in range(TOP_K):
        def start(t, _, k=k):
            pltpu.make_async_copy(h_ref.at[pl.ds(t, 1)], xs_hbm.at[pl.ds(pos_ref[0, k * TM + t], 1)], sem).start()
            return 0

        lax.fori_loop(0, TM, start, 0, unroll=8)
    for k in range(TOP_K):
        pltpu.make_async_copy(h_ref, xs_hbm.at[pl.ds(0, TM)], sem).wait()


def _dispatch(pos_tiles, h2, n_tiles, tile, n_rows):
    return pl.pallas_call(
        _dispatch_kernel,
        grid=(n_tiles,),
        in_specs=[pl.BlockSpec((None, 1, TOP_K * TM), lambda t: (t, 0, 0), memory_space=pltpu.SMEM),
                  pl.BlockSpec((TM, D), lambda t: (tile(t), 0))],
        out_specs=pl.BlockSpec(memory_space=pl.ANY),
        out_shape=jax.ShapeDtypeStruct((n_rows, D), F32),
        scratch_shapes=[pltpu.SemaphoreType.DMA(())],
        compiler_params=_cparams(1),
        name="moe_dispatch",
    )(pos_tiles, h2)


def _expert_kernel(blk_e_ref, valid_ref, x_ref, wg_ref, wu_ref, wd_ref, y_ref):
    n_valid = valid_ref[pl.program_id(0)]

    @pl.when(n_valid > 0)
    def _():
        row = lax.broadcasted_iota(jnp.int32, (MOE_ROWS, 1), 0)
        x = jnp.where(row < n_valid, x_ref[...], 0.0).astype(BF16)
        hb = _silu(_dot(x, wg_ref[...].astype(BF16))) * _dot(x, wu_ref[...].astype(BF16))
        y_ref[...] = _dot(hb.astype(BF16), wd_ref[...].astype(BF16))

    @pl.when(n_valid == 0)
    def _():
        y_ref[...] = jnp.zeros_like(y_ref)


def _experts(blk_e, valid, xs, wg, wu, wd, n_blocks):
    grid_spec = pltpu.PrefetchScalarGridSpec(
        num_scalar_prefetch=2,
        grid=(n_blocks,),
        in_specs=[pl.BlockSpec((MOE_ROWS, D), lambda i, be, nv: (i, 0)),
                  pl.BlockSpec((None, D, D_EXPERT), lambda i, be, nv: (be[i], 0, 0)),
                  pl.BlockSpec((None, D, D_EXPERT), lambda i, be, nv: (be[i], 0, 0)),
                  pl.BlockSpec((None, D_EXPERT, D), lambda i, be, nv: (be[i], 0, 0))],
        out_specs=pl.BlockSpec((MOE_ROWS, D), lambda i, be, nv: (i, 0)),
    )
    return pl.pallas_call(
        _expert_kernel,
        grid_spec=grid_spec,
        out_shape=jax.ShapeDtypeStruct((n_blocks * MOE_ROWS, D), F32),
        compiler_params=_cparams(1),
        name="routed_experts",
    )(blk_e, valid, xs, wg, wu, wd)


def _combine_kernel(pos_ref, y_hbm, w_ref, x1_ref, sh_ref, mod_ref, o_ref, buf, sem):
    for k in range(TOP_K):
        def start(t, _, k=k):
            pltpu.make_async_copy(y_hbm.at[pl.ds(pos_ref[0, k * TM + t], 1)], buf.at[pl.ds(k * TM + t, 1)], sem).start()
            return 0

        lax.fori_loop(0, TM, start, 0, unroll=8)
    for k in range(TOP_K):
        pltpu.make_async_copy(y_hbm.at[pl.ds(0, TM)], buf.at[pl.ds(k * TM, TM)], sem).wait()
    f = sh_ref[...]
    for k in range(TOP_K):
        f = f + buf[k * TM:(k + 1) * TM, :] * w_ref[:, k:k + 1]
    o_ref[...] = x1_ref[...] + mod_ref[:, 5 * D:6 * D] * f


def _combine(pos_tiles, ys, w, x1, shared, mod3, n_tiles, tile, mod_row, out_rows, out_tile):
    return pl.pallas_call(
        _combine_kernel,
        grid=(n_tiles,),
        in_specs=[pl.BlockSpec((None, 1, TOP_K * TM), lambda t: (t, 0, 0), memory_space=pltpu.SMEM),
                  pl.BlockSpec(memory_space=pl.ANY),
                  pl.BlockSpec((TM, 128), lambda t: (t, 0)),
                  pl.BlockSpec((TM, D), lambda t: (tile(t), 0)),
                  pl.BlockSpec((TM, D), lambda t: (t, 0)),
                  pl.BlockSpec((None, 1, 6 * D), lambda t: (mod_row(tile(t)), 0, 0))],
        out_specs=pl.BlockSpec((TM, D), lambda t: (out_tile(t), 0)),
        out_shape=jax.ShapeDtypeStruct((out_rows, D), F32),
        scratch_shapes=[pltpu.VMEM((TOP_K * TM, D), F32), pltpu.SemaphoreType.DMA(())],
        compiler_params=_cparams(1),
        name="moe_combine",
    )(pos_tiles, ys, w, x1, shared, mod3)


def _moe_plan(idx, rank, counts, n_blocks):
    n = idx.shape[0]
    cnt = counts.reshape(N_EXPERTS).astype(jnp.int32)
    padded = (cnt + MOE_ROWS - 1) // MOE_ROWS * MOE_ROWS
    pad_end = jnp.cumsum(padded)
    pad_start = pad_end - padded
    experts = jnp.arange(N_EXPERTS, dtype=jnp.int32)
    pos = rank + jnp.sum(jnp.where(idx[:, :, None] == experts, pad_start, 0), axis=-1)
    blk_start = jnp.arange(n_blocks, dtype=jnp.int32) * MOE_ROWS
    blk_e = jnp.minimum(jnp.sum(blk_start[:, None] >= pad_end[None, :], axis=1), N_EXPERTS - 1).astype(jnp.int32)
    mine = blk_e[:, None] == experts
    in_expert = blk_start - jnp.sum(jnp.where(mine, pad_start, 0), axis=1)
    valid = jnp.clip(jnp.sum(jnp.where(mine, cnt, 0), axis=1) - in_expert, 0, MOE_ROWS).astype(jnp.int32)
    pos_tiles = pos.reshape(n // TM, TM, TOP_K).transpose(0, 2, 1).reshape(n // TM, 1, TOP_K * TM)
    return blk_e, valid, pos_tiles.astype(jnp.int32)


def _rope_tables(seq, ctx):
    half = MLA_ROPE // 2
    n_freq = half // 2
    inv = ROPE_THETA ** (-2.0 * jnp.arange(n_freq, dtype=F32) / half)
    t = jnp.arange(seq)
    ang_r = (t // GRID_W).astype(F32)[:, None] * inv
    ang_c = (t % GRID_W).astype(F32)[:, None] * inv
    cos = jnp.concatenate([jnp.cos(ang_r), jnp.cos(ang_r), jnp.cos(ang_c), jnp.cos(ang_c)], axis=1)
    sin = jnp.concatenate([-jnp.sin(ang_r), jnp.sin(ang_r), -jnp.sin(ang_c), jnp.sin(ang_c)], axis=1)
    pad_l = MLA_NOPE
    pad_r = HEAD_PAD - MLA_NOPE - MLA_ROPE
    cos = jnp.pad(cos, ((0, ctx), (pad_l, pad_r)), constant_values=1.0)
    cos = cos.at[seq:, :].set(1.0)
    sin = jnp.pad(sin, ((0, ctx), (pad_l, pad_r)))
    return cos, sin


def _pad_heads(w, n_heads, width, offset=0):
    k = w.shape[0]
    w = w.reshape(k, n_heads, width)
    w = jnp.pad(w, ((0, 0), (0, 0), (offset, HEAD_PAD - width - offset)))
    return w.reshape(k, n_heads * HEAD_PAD)


def _pad_vec(g, offset):
    return jnp.pad(g, (offset, HEAD_PAD - g.shape[0] - offset)).reshape(1, HEAD_PAD)


def _reorder_w_in(w):
    off_cq, off_ckv, off_kr, off_uv, off_merge = 2048, 2432, 2688, 2720, 3744
    kr = jnp.pad(w[:, off_kr:off_uv], ((0, 0), (MLA_NOPE, HEAD_PAD - MLA_NOPE - MLA_ROPE)))
    return jnp.concatenate([w[:, off_merge:], w[:, off_uv:off_merge], w[:, :off_cq],
                            w[:, off_ckv:off_kr], kr, w[:, off_cq:off_ckv]], axis=1).astype(BF16)


def kernel(x, c, ctx, c_ctx, ada_w, ada_b, norm1_g, norm2_g, w_in, ret_decay_fwd, ret_decay_bwd, ret_gn_g,
           ret_gn_b, w_br_ret, mla_qa_g, mla_w_uq, mla_kva_g, mla_w_ukv, mla_qn_g, mla_kn_g, mla_kr_g, w_br_mla,
           gmlp_ln_g, gmlp_ln_b, gmlp_ws, gmlp_bs, w_br_gmlp, w_out, moe_router, moe_bias, moe_w_gate, moe_w_up,
           moe_w_down, sh_w_gate, sh_w_up, sh_w_down):
    B, seq, _ = x.shape
    n_ctx = ctx.shape[1]
    depth = ada_w.shape[0]
    assert n_ctx == TM and seq % (ATT_UNROLL * ATT_KV_CHUNK) == 0 and seq % TM == 0
    lt = seq + n_ctx
    tiles_per_b = lt // TM
    lat_tiles_per_b = seq // TM
    ctx_tile = lat_tiles_per_b

    def mod_row(t):
        return jnp.where(t % tiles_per_b == ctx_tile, B, t // tiles_per_b)

    c_rows = jnp.concatenate([c, c_ctx[None, :], jnp.zeros((8 - B - 1, D), F32)], axis=0)
    mod = _ada(c_rows, ada_w, ada_b)
    cos_t, sin_t = _rope_tables(seq, n_ctx)
    xs = jnp.concatenate([x, ctx], axis=1).reshape(B * lt, D)

    for l in range(depth):
        last = l == depth - 1
        mod3 = mod[l].reshape(8, 1, 6 * D)
        p = _in_proj(xs, mod3, norm1_g[l].reshape(1, D), _reorder_w_in(w_in[l]), B * tiles_per_b, mod_row)

        lg = jnp.stack([jax.nn.log_sigmoid(ret_decay_fwd[l].astype(F32)),
                        jax.nn.log_sigmoid(ret_decay_bwd[l].astype(F32))])
        y_ret = _retention(p.reshape(B, lt, N_IN_PAD), lg, ret_gn_g[l].reshape(1, -1), ret_gn_b[l].reshape(1, -1),
                           seq, n_ctx)

        w_ukv = mla_w_ukv[l].reshape(MLA_KV_LORA, MLA_HEADS, MLA_NOPE + MLA_V)
        wk_p = _pad_heads(w_ukv[:, :, :MLA_NOPE].reshape(MLA_KV_LORA, -1), MLA_HEADS, MLA_NOPE).astype(BF16)
        wv = w_ukv[:, :, MLA_NOPE:].reshape(MLA_KV_LORA, MLA_HEADS * MLA_V).astype(BF16)
        wq_p = _pad_heads(mla_w_uq[l], MLA_HEADS, MLA_QK).astype(BF16)
        q, k, v = _mla_prep(p, cos_t, sin_t, mla_qa_g[l].reshape(1, -1), mla_kva_g[l].reshape(1, -1),
                            _pad_vec(mla_qn_g[l], 0), _pad_vec(mla_kn_g[l], 0), _pad_vec(mla_kr_g[l], MLA_NOPE),
                            wq_p, wk_p, wv, B, lt)
        o_mla = _attention(q, k, v, seq, n_ctx, lat_tiles_per_b if last else tiles_per_b)

        if last:
            n_tiles = B * lat_tiles_per_b
            tile = lambda t: (t // lat_tiles_per_b) * tiles_per_b + t % lat_tiles_per_b
        else:
            n_tiles = B * tiles_per_b
            tile = lambda t: t
        bs_full = jnp.broadcast_to(gmlp_bs[l][:, :, None], (GMLP_GROUPS, GMLP_CHUNK, GMLP_CHUNK))
        x1, h2 = _merge(xs, mod3, p, y_ret.reshape(B * lt, -1), o_mla.reshape(B * lt, -1),
                        gmlp_ln_g[l].reshape(1, -1), gmlp_ln_b[l].reshape(1, -1), gmlp_ws[l].astype(BF16), bs_full,
                        w_br_ret[l].astype(BF16), w_br_mla[l].astype(BF16), w_br_gmlp[l].astype(BF16),
                        w_out[l].astype(BF16), norm2_g[l].reshape(1, D), n_tiles, tile, mod_row)

        idx, w, rank, counts, shared = _route(h2, moe_router[l], moe_bias[l].reshape(1, -1),
                                              sh_w_gate[l].astype(BF16), sh_w_up[l].astype(BF16),
                                              sh_w_down[l].astype(BF16), n_tiles, tile)
        n_act = n_tiles * TM
        n_blocks = -(-(n_act * TOP_K + N_EXPERTS * (MOE_ROWS - 1)) // MOE_ROWS)
        blk_e, valid, pos_tiles = _moe_plan(idx[:, :TOP_K], rank[:, :TOP_K], counts, n_blocks)
        xg = _dispatch(pos_tiles, h2, n_tiles, tile, n_blocks * MOE_ROWS)
        ys = _experts(blk_e, valid, xg, moe_w_gate[l], moe_w_up[l], moe_w_down[l], n_blocks)
        if last:
            xs = _combine(pos_tiles, ys, w, x1, shared, mod3, n_tiles, tile, mod_row, B * seq, lambda t: t)
        else:
            xs = _combine(pos_tiles, ys, w, x1, shared, mod3, n_tiles, tile, mod_row, B * lt, tile)
    return xs.reshape(B, seq, D)
```

```python
import functools

import jax
import jax.numpy as jnp
from jax import lax
from jax.experimental import pallas as pl
from jax.experimental.pallas import tpu as pltpu

F32 = jnp.float32
BF16 = jnp.bfloat16

D = 1024
GRID_W = 64
RET_HEADS = 4
RET_D = 128
RET_CHUNK = 128
RET_OUT_ROWS = 256
MLA_HEADS = 8
MLA_Q_LORA = 384
MLA_KV_LORA = 256
MLA_NOPE = 64
MLA_ROPE = 32
MLA_V = 64
MLA_V_EXT = MLA_V + 16
MLA_QK = MLA_NOPE + MLA_ROPE
HEAD_PAD = 128
ROPE_THETA = 10000.0
GMLP_GROUPS = 4
GMLP_W = 512
GMLP_CHUNK = 128
N_EXPERTS = 64
TOP_K = 6
D_EXPERT = 256
ROUTED_SCALE = 2.5
EPS = 1e-6
LOG2_E = 1.4426950408889634

TM = 256
MOE_ROWS = 256
ATT_KV_CHUNK = 512
ATT_UNROLL = 4

C_MERGE = 0
C_UV = 3072
C_RET = 4096
C_CKV = 6144
C_KR = 6400
C_CQ = 6528
N_IN_PAD = 6912
IN_CHUNK = 768

VMEM_LIMIT = 56 * 1024 * 1024


def _cparams(n_axes, vmem=VMEM_LIMIT):
    return pltpu.CompilerParams(dimension_semantics=("arbitrary",) * n_axes, vmem_limit_bytes=vmem)


def _silu(x):
    return x * jax.nn.sigmoid(x)


def _dot(a, b):
    return jnp.dot(a, b, preferred_element_type=F32)


def _dot_nt(a, b):
    return lax.dot_general(a, b, (((1,), (1,)), ((), ())), preferred_element_type=F32)


def _dot_tn(a, b):
    return lax.dot_general(a, b, (((0,), (0,)), ((), ())), preferred_element_type=F32)


def _pack_bf16_pairs(x):
    n = x.shape[1] // 2
    lo = lax.bitcast_convert_type(x[:, :n].astype(BF16).astype(F32), jnp.uint32)
    hi = lax.bitcast_convert_type(x[:, n:].astype(BF16).astype(F32), jnp.uint32)
    return (lo >> 16) | hi


def _unpack_bf16_pairs(u):
    lo = lax.bitcast_convert_type(u << 16, F32)
    hi = lax.bitcast_convert_type(u & jnp.uint32(0xFFFF0000), F32)
    return lo, hi


def _ada_kernel(c_ref, w_ref, b_ref, o_ref):
    s = _silu(c_ref[...])
    o_ref[...] = _dot(s.astype(BF16), w_ref[...].astype(BF16)) + b_ref[...]


def _ada(c_rows, ada_w, ada_b):
    depth = ada_w.shape[0]
    n = ada_w.shape[2]
    cw = 1536
    return pl.pallas_call(
        _ada_kernel,
        grid=(depth, n // cw),
        in_specs=[pl.BlockSpec((8, D), lambda l, j: (0, 0)),
                  pl.BlockSpec((None, D, cw), lambda l, j: (l, 0, j)),
                  pl.BlockSpec((None, 1, cw), lambda l, j: (l, 0, j))],
        out_specs=pl.BlockSpec((None, 8, cw), lambda l, j: (l, 0, j)),
        out_shape=jax.ShapeDtypeStruct((depth, 8, n), F32),
        compiler_params=_cparams(2),
        name="ada_mod",
    )(c_rows, ada_w, ada_b.reshape(depth, 1, n))


def _modulated_rmsnorm(x, g, shift, scale):
    y = x * lax.rsqrt(jnp.mean(x * x, axis=-1, keepdims=True) + EPS) * g
    return y * (1.0 + scale) + shift


def _in_proj_kernel(x_ref, mod_ref, g_ref, w_ref, o_ref, h_scr):
    h = _modulated_rmsnorm(x_ref[...], g_ref[...], mod_ref[:, 0:D], mod_ref[:, D:2 * D])
    h_scr[...] = h.astype(BF16)
    for c in range(N_IN_PAD // IN_CHUNK):
        cols = slice(c * IN_CHUNK, (c + 1) * IN_CHUNK)
        o_ref[:, cols] = _dot(h_scr[...], w_ref[:, cols]).astype(BF16)


def _in_proj(xs, mod3, g, w_in_r, n_tiles, mod_row):
    n_rows = xs.shape[0]
    return pl.pallas_call(
        _in_proj_kernel,
        grid=(n_tiles,),
        in_specs=[pl.BlockSpec((TM, D), lambda t: (t, 0)),
                  pl.BlockSpec((None, 1, 6 * D), lambda t: (mod_row(t), 0, 0)),
                  pl.BlockSpec((1, D), lambda t: (0, 0)),
                  pl.BlockSpec((D, N_IN_PAD), lambda t: (0, 0))],
        out_specs=pl.BlockSpec((TM, N_IN_PAD), lambda t: (t, 0)),
        out_shape=jax.ShapeDtypeStruct((n_rows, N_IN_PAD), BF16),
        scratch_shapes=[pltpu.VMEM((TM, D), BF16)],
        compiler_params=_cparams(1),
        name="in_proj",
    )(xs, mod3, g, w_in_r)


def _retention_kernel(lg_ref, q_ref, k_ref, v_ref, g_ref, gng_ref, gnb_ref, y_ref, of_scr, ob_scr,
                      *, n_lat_chunks, n_ctx_chunks):
    h = pl.program_id(1)
    lg_f = lg_ref[0, h]
    lg_b = lg_ref[1, h]
    C = RET_CHUNK
    k_scale = RET_D ** -0.5
    ri = lax.broadcasted_iota(jnp.int32, (C, C), 0).astype(F32)
    ci = lax.broadcasted_iota(jnp.int32, (C, C), 1).astype(F32)
    pos = lax.broadcasted_iota(jnp.int32, (C, 1), 0).astype(F32)
    diff = ri - ci
    d_f = jnp.where(diff >= 0, jnp.exp(lg_f * jnp.maximum(diff, 0.0)), 0.0) * k_scale
    d_b = jnp.where(diff < 0, jnp.exp(lg_b * jnp.maximum(-diff, 0.0)), 0.0) * k_scale
    qdec_f = jnp.exp(lg_f * (pos + 1.0))
    kdec_f = jnp.exp(lg_f * (C - 1.0 - pos)) * k_scale
    cdec_f = jnp.exp(lg_f * C)
    qdec_b = jnp.exp(lg_b * (C - pos))
    kdec_b = jnp.exp(lg_b * pos) * k_scale
    cdec_b = jnp.exp(lg_b * C)

    def chunk(c, state, dmat, qdec, kdec, cdec):
        rows = pl.ds(pl.multiple_of(c * C, C), C)
        q = q_ref[rows, :]
        k = k_ref[rows, :]
        v = v_ref[rows, :]
        att = (_dot_nt(q, k) * dmat).astype(BF16)
        o = _dot(att, v) + _dot((q.astype(F32) * qdec).astype(BF16), state.astype(BF16))
        kd = (k.astype(F32) * kdec).astype(BF16)
        return rows, o, state * cdec + _dot_tn(kd, v)

    n_all = n_lat_chunks + n_ctx_chunks

    def scan_body(i, states):
        s_f, s_b = states
        c_f = jnp.where(i < n_ctx_chunks, n_lat_chunks + i, i - n_ctx_chunks)
        rows, o, s_f = chunk(c_f, s_f, d_f, qdec_f, kdec_f, cdec_f)
        of_scr[rows, :] = o
        rows, o, s_b = chunk(n_all - 1 - i, s_b, d_b, qdec_b, kdec_b, cdec_b)
        ob_scr[rows, :] = o
        return s_f, s_b

    zero = jnp.zeros((RET_D, RET_D), F32)
    lax.fori_loop(0, n_all, scan_body, (zero, zero), unroll=2)

    def out_body(c, _):
        rows = pl.ds(pl.multiple_of(c * RET_OUT_ROWS, RET_OUT_ROWS), RET_OUT_ROWS)
        o = of_scr[rows, :] + ob_scr[rows, :]
        mu = jnp.mean(o, axis=-1, keepdims=True)
        var = jnp.mean(jnp.square(o - mu), axis=-1, keepdims=True)
        on = (o - mu) * lax.rsqrt(var + EPS)
        y = _silu(g_ref[rows, :].astype(F32)) * (on * gng_ref[...] + gnb_ref[...])
        y_ref[rows, :] = y.astype(BF16)
        return 0

    lax.fori_loop(0, n_all * C // RET_OUT_ROWS, out_body, 0)


def _retention(p3, lg, gn_g, gn_b, seq, ctx):
    B, lt, _ = p3.shape
    base = C_RET // RET_D
    kern = functools.partial(_retention_kernel, n_lat_chunks=seq // RET_CHUNK, n_ctx_chunks=ctx // RET_CHUNK)

    def col(off):
        return pl.BlockSpec((None, lt, RET_D), lambda b, h: (b, 0, base + off * RET_HEADS + h))

    return pl.pallas_call(
        kern,
        grid=(B, RET_HEADS),
        in_specs=[pl.BlockSpec(memory_space=pltpu.SMEM),
                  col(0), col(1), col(2), col(3),
                  pl.BlockSpec((1, RET_D), lambda b, h: (0, h)),
                  pl.BlockSpec((1, RET_D), lambda b, h: (0, h))],
        out_specs=pl.BlockSpec((None, lt, RET_D), lambda b, h: (b, 0, h)),
        out_shape=jax.ShapeDtypeStruct((B, lt, RET_HEADS * RET_D), BF16),
        scratch_shapes=[pltpu.VMEM((lt, RET_D), F32), pltpu.VMEM((lt, RET_D), F32)],
        compiler_params=_cparams(2),
        name="retention",
    )(lg, p3, p3, p3, p3, gn_g, gn_b)


def _rope_rotate(x, first_half):
    return jnp.where(first_half, pltpu.roll(x, HEAD_PAD - 8, 1), pltpu.roll(x, 8, 1))


def _mla_prep_kernel(cq_ref, ckv_ref, kr_ref, cos_ref, sin_ref, qa_ref, kva_ref, qn_ref, kn_ref, krg_ref,
                     wq_ref, wk_ref, wv_ref, q_ref, k_ref, v_ref):
    lane = lax.broadcasted_iota(jnp.int32, (1, HEAD_PAD), 1)
    first_half = (lane % 16) < 8
    cos = cos_ref[...]
    sin = sin_ref[...]

    def rms(x, n):
        return x * lax.rsqrt(jnp.sum(x * x, axis=-1, keepdims=True) * (1.0 / n) + EPS)

    def rope(x):
        return x * cos + _rope_rotate(x, first_half) * sin

    cq = cq_ref[...].astype(F32)
    cqn = (rms(cq, MLA_Q_LORA) * qa_ref[...]).astype(BF16)
    q_all = _dot(cqn, wq_ref[...])
    ckv = ckv_ref[...].astype(F32)
    ckvn = (rms(ckv, MLA_KV_LORA) * kva_ref[...]).astype(BF16)
    k_all = _dot(ckvn, wk_ref[...])
    v_all = _dot(ckvn, wv_ref[...])
    k_rope = rope(rms(kr_ref[...].astype(F32), MLA_ROPE) * krg_ref[...])
    scale = MLA_QK ** -0.5 * LOG2_E
    v_t = v_all.T
    ones_row = jnp.where(lax.broadcasted_iota(jnp.int32, (MLA_V_EXT - MLA_V, TM), 0) == 0, 1.0, 0.0)
    for h in range(MLA_HEADS):
        cols = slice(h * HEAD_PAD, (h + 1) * HEAD_PAD)
        qh = rope(rms(q_all[:, cols], MLA_QK) * qn_ref[...]) * scale
        q_ref[h] = qh.astype(BF16)
        kh = rms(k_all[:, cols], MLA_NOPE) * kn_ref[...] + k_rope
        k_ref[h] = kh.astype(BF16)
        v_ref[h] = jnp.concatenate([v_t[h * MLA_V:(h + 1) * MLA_V, :], ones_row], axis=0).astype(BF16)


def _mla_prep(p, cos_t, sin_t, qa_g, kva_g, qn_p, kn_p, kr_p, wq_p, wk_p, wv, B, lt):
    tiles_per_b = lt // TM
    hw = MLA_HEADS * HEAD_PAD
    const = lambda shape: pl.BlockSpec(shape, lambda b, j: (0,) * len(shape))
    head_out = pl.BlockSpec((None, MLA_HEADS, TM, HEAD_PAD), lambda b, j: (b, 0, j, 0))
    shp = jax.ShapeDtypeStruct((B, MLA_HEADS, lt, HEAD_PAD), BF16)
    v_out = pl.BlockSpec((None, MLA_HEADS, None, MLA_V_EXT, TM), lambda b, j: (b, 0, j, 0, 0))
    v_shp = jax.ShapeDtypeStruct((B, MLA_HEADS, tiles_per_b, MLA_V_EXT, TM), BF16)
    return pl.pallas_call(
        _mla_prep_kernel,
        grid=(B, tiles_per_b),
        in_specs=[pl.BlockSpec((TM, MLA_Q_LORA), lambda b, j: (b * tiles_per_b + j, C_CQ // MLA_Q_LORA)),
                  pl.BlockSpec((TM, MLA_KV_LORA), lambda b, j: (b * tiles_per_b + j, C_CKV // MLA_KV_LORA)),
                  pl.BlockSpec((TM, HEAD_PAD), lambda b, j: (b * tiles_per_b + j, C_KR // HEAD_PAD)),
                  pl.BlockSpec((TM, HEAD_PAD), lambda b, j: (j, 0)),
                  pl.BlockSpec((TM, HEAD_PAD), lambda b, j: (j, 0)),
                  const((1, MLA_Q_LORA)), const((1, MLA_KV_LORA)),
                  const((1, HEAD_PAD)), const((1, HEAD_PAD)), const((1, HEAD_PAD)),
                  const((MLA_Q_LORA, hw)), const((MLA_KV_LORA, hw)), const((MLA_KV_LORA, MLA_HEADS * MLA_V))],
        out_specs=[head_out, head_out, v_out],
        out_shape=[shp, shp, v_shp],
        compiler_params=_cparams(2),
        name="mla_prep",
    )(p, p, p, cos_t, sin_t, qa_g, kva_g, qn_p, kn_p, kr_p, wq_p, wk_p, wv)


def _attention_kernel(q_ref, k_ref, v_ref, o_ref, s_scr, *, seq, ctx, ctx_tile):
    i = pl.program_id(2)
    n_blk = ATT_KV_CHUNK // TM
    n_chunks = seq // ATT_KV_CHUNK

    def scores(hh, slot, blk, nb):
        start = blk * TM if isinstance(blk, int) else pl.multiple_of(blk * TM, TM)
        s_scr[hh, slot, 0:nb * TM, :] = _dot_nt(k_ref[hh, pl.ds(start, nb * TM), :], q_ref[hh])

    def absorb(hh, slot, blk, nb, carry):
        m, acc = carry
        s = s_scr[hh, slot, 0:nb * TM, :]
        m_new = jnp.maximum(m, jnp.max(s, axis=0, keepdims=True))
        p = jnp.exp2(s - m_new).astype(BF16)
        acc = jnp.exp2(m - m_new) * acc
        for j in range(nb):
            acc = acc + _dot(v_ref[hh, blk + j], p[j * TM:(j + 1) * TM, :])
        return m_new, acc

    def init():
        return (jnp.full((1, TM), -jnp.inf, F32), jnp.zeros((MLA_V_EXT, TM), F32))

    def write(carries):
        outs = [acc[0:MLA_V, :] / acc[MLA_V:MLA_V + 1, :] for _, acc in carries]
        o_ref[...] = jnp.concatenate(outs, axis=0).T.astype(BF16)

    def step(carries, slot, blk, nb, next_blk, next_nb):
        out = []
        for hh in range(2):
            if next_blk is not None:
                scores(hh, 1 - slot, next_blk, next_nb)
            out.append(absorb(hh, slot, blk, nb, carries[hh]))
        return tuple(out)

    ctx_blk = seq // TM
    ctx_nb = ctx // TM

    @pl.when(i != ctx_tile)
    def _():
        for hh in range(2):
            scores(hh, 0, ctx_blk, ctx_nb)
        carries = step((init(), init()), 0, ctx_blk, ctx_nb, 0, n_blk)
        last_blk = (n_chunks - 1) * n_blk

        def body(c, carries):
            for u in range(ATT_UNROLL):
                blk = (c * ATT_UNROLL + u) * n_blk
                carries = step(carries, (1 + u) % 2, blk, n_blk, jnp.minimum(blk + n_blk, last_blk), n_blk)
            return carries

        write(lax.fori_loop(0, n_chunks // ATT_UNROLL, body, carries))

    @pl.when(i == ctx_tile)
    def _():
        for hh in range(2):
            scores(hh, 0, ctx_blk, ctx_nb)
        write(step((init(), init()), 0, ctx_blk, ctx_nb, None, None))


def _attention(q, k, v, seq, ctx, n_q_tiles):
    B, H, lt, _ = q.shape
    kern = functools.partial(_attention_kernel, seq=seq, ctx=ctx, ctx_tile=seq // TM)
    return pl.pallas_call(
        kern,
        grid=(B, H // 2, n_q_tiles),
        in_specs=[pl.BlockSpec((None, 2, TM, HEAD_PAD), lambda b, h, i: (b, h, i, 0)),
                  pl.BlockSpec((None, 2, lt, HEAD_PAD), lambda b, h, i: (b, h, 0, 0)),
                  pl.BlockSpec((None, 2, lt // TM, MLA_V_EXT, TM), lambda b, h, i: (b, h, 0, 0, 0))],
        out_specs=pl.BlockSpec((None, TM, HEAD_PAD), lambda b, h, i: (b, i, h)),
        out_shape=jax.ShapeDtypeStruct((B, lt, (H // 2) * HEAD_PAD), BF16),
        scratch_shapes=[pltpu.VMEM((2, 2, ATT_KV_CHUNK, TM), F32)],
        compiler_params=_cparams(3),
        name="attention",
    )(q, k, v)


def _merge_kernel(x_ref, mod_ref, mg_ref, uv_ref, yr_ref, om_ref, lng_ref, lnb_ref, ws_ref, bs_ref,
                  wr_ref, wm_ref, wg_ref, wo_ref, n2_ref, x1_ref, h2_ref):
    yr = _dot(yr_ref[...], wr_ref[...])
    ym = _dot(om_ref[...], wm_ref[...])
    z = jax.nn.gelu(uv_ref[...].astype(F32))
    u = z[:, :GMLP_W]
    v = z[:, GMLP_W:]
    mu = jnp.mean(v, axis=-1, keepdims=True)
    var = jnp.mean(jnp.square(v - mu), axis=-1, keepdims=True)
    vn = ((v - mu) * lax.rsqrt(var + EPS) * lng_ref[...] + lnb_ref[...]).astype(BF16)
    gw = GMLP_W // GMLP_GROUPS
    chunks = []
    for c in range(TM // GMLP_CHUNK):
        rows = slice(c * GMLP_CHUNK, (c + 1) * GMLP_CHUNK)
        groups = [_dot(ws_ref[g], vn[rows, g * gw:(g + 1) * gw]) + bs_ref[g] for g in range(GMLP_GROUPS)]
        chunks.append(jnp.concatenate(groups, axis=1))
    sv = jnp.concatenate(chunks, axis=0)
    yg = _dot((u * sv).astype(BF16), wg_ref[...])
    gate = jax.nn.sigmoid(mg_ref[...].astype(F32))
    y = gate[:, :D] * yr + gate[:, D:2 * D] * ym + gate[:, 2 * D:] * yg
    out = _dot(y.astype(BF16), wo_ref[...])
    x1 = x_ref[...] + mod_ref[:, 2 * D:3 * D] * out
    x1_ref[...] = x1
    h2_ref[...] = _modulated_rmsnorm(x1, n2_ref[...], mod_ref[:, 3 * D:4 * D], mod_ref[:, 4 * D:5 * D])


def _merge(xs, mod3, p, y_ret, o_mla, ln_g, ln_b, ws, bs_full, w_br_ret, w_br_mla, w_br_gmlp, w_out, n2_g,
           n_tiles, tile, mod_row):
    n_rows = xs.shape[0]
    const = lambda shape: pl.BlockSpec(shape, lambda t: (0,) * len(shape))
    row = lambda w, cb=0: pl.BlockSpec((TM, w), lambda t: (tile(t), cb))
    shp = jax.ShapeDtypeStruct((n_rows, D), F32)
    return pl.pallas_call(
        _merge_kernel,
        grid=(n_tiles,),
        in_specs=[row(D),
                  pl.BlockSpec((None, 1, 6 * D), lambda t: (mod_row(tile(t)), 0, 0)),
                  row(3 * D, C_MERGE // (3 * D)), row(D, C_UV // D),
                  row(RET_HEADS * RET_D), row(MLA_HEADS * MLA_V),
                  const((1, GMLP_W)), const((1, GMLP_W)),
                  const((GMLP_GROUPS, GMLP_CHUNK, GMLP_CHUNK)), const((GMLP_GROUPS, GMLP_CHUNK, GMLP_CHUNK)),
                  const((RET_HEADS * RET_D, D)), const((MLA_HEADS * MLA_V, D)), const((GMLP_W, D)),
                  const((D, D)), const((1, D))],
        out_specs=[row(D), row(D)],
        out_shape=[shp, shp],
        compiler_params=_cparams(1),
        name="merge",
    )(xs, mod3, p, p, y_ret, o_mla, ln_g, ln_b, ws, bs_full, w_br_ret, w_br_mla, w_br_gmlp, w_out, n2_g)


def _route_kernel(h_ref, r_ref, b_ref, sg_ref, su_ref, sd_ref, idx_ref, w_ref, rank_ref, cnt_ref, sh_ref, hp_ref,
                  cnt_scr):
    @pl.when(pl.program_id(0) == 0)
    def _():
        cnt_scr[...] = jnp.zeros_like(cnt_scr)

    h = h_ref[...]
    logits = jnp.dot(h, r_ref[...], preferred_element_type=F32, precision=lax.Precision.HIGHEST)
    scores = jax.nn.sigmoid(logits)
    sel = scores + b_ref[...]
    lane_e = lax.broadcasted_iota(jnp.int32, (TM, N_EXPERTS), 1).astype(F32)
    lane_o = lax.broadcasted_iota(jnp.int32, (TM, 128), 1)
    idx_out = jnp.zeros((TM, 128), F32)
    w_out = jnp.zeros((TM, 128), F32)
    hits = []
    for k in range(TOP_K):
        best = jnp.max(sel, axis=-1, keepdims=True)
        pick = jnp.min(jnp.where(sel == best, lane_e, float(N_EXPERTS)), axis=-1, keepdims=True)
        hit = lane_e == pick
        hits.append(hit)
        wk = jnp.sum(jnp.where(hit, scores, 0.0), axis=-1, keepdims=True)
        sel = jnp.where(hit, -jnp.inf, sel)
        idx_out = jnp.where(lane_o == k, pick, idx_out)
        w_out = jnp.where(lane_o == k, wk, w_out)
    w_out = w_out / jnp.sum(w_out, axis=-1, keepdims=True) * ROUTED_SCALE
    idx_ref[...] = idx_out.astype(jnp.int32)
    w_ref[...] = w_out
    chosen = jnp.zeros((TM, N_EXPERTS), F32)
    for hit in hits:
        chosen = jnp.where(hit, 1.0, chosen)
    earlier = (lax.broadcasted_iota(jnp.int32, (TM, TM), 0) > lax.broadcasted_iota(jnp.int32, (TM, TM), 1))
    before = _dot(jnp.where(earlier, 1.0, 0.0).astype(BF16), chosen.astype(BF16)) + cnt_scr[...]
    rank_out = jnp.zeros((TM, 128), F32)
    for k, hit in enumerate(hits):
        rank_out = jnp.where(lane_o == k, jnp.sum(jnp.where(hit, before, 0.0), axis=-1, keepdims=True), rank_out)
    rank_ref[...] = rank_out.astype(jnp.int32)
    cnt_scr[...] += jnp.sum(chosen, axis=0, keepdims=True)
    cnt_ref[...] = cnt_scr[...]
    hb = h.astype(BF16)
    a = _silu(_dot(hb, sg_ref[...])) * _dot(hb, su_ref[...])
    sh_ref[...] = _dot(a.astype(BF16), sd_ref[...])
    hp_ref[...] = _pack_bf16_pairs(h)


def _route(h2, router, bias, sg, su, sd, n_tiles, tile):
    const = lambda shape: pl.BlockSpec(shape, lambda t: (0,) * len(shape))
    n_act = n_tiles * TM
    return pl.pallas_call(
        _route_kernel,
        grid=(n_tiles,),
        in_specs=[pl.BlockSpec((TM, D), lambda t: (tile(t), 0)),
                  const((D, N_EXPERTS)), const((1, N_EXPERTS)),
                  const((D, D_EXPERT)), const((D, D_EXPERT)), const((D_EXPERT, D))],
        out_specs=[pl.BlockSpec((TM, 128), lambda t: (t, 0)),
                   pl.BlockSpec((TM, 128), lambda t: (t, 0)),
                   pl.BlockSpec((TM, 128), lambda t: (t, 0)),
                   pl.BlockSpec((1, N_EXPERTS), lambda t: (0, 0)),
                   pl.BlockSpec((TM, D), lambda t: (t, 0)),
                   pl.BlockSpec((TM, D // 2), lambda t: (t, 0))],
        out_shape=[jax.ShapeDtypeStruct((n_act, 128), jnp.int32),
                   jax.ShapeDtypeStruct((n_act, 128), F32),
                   jax.ShapeDtypeStruct((n_act, 128), jnp.int32),
                   jax.ShapeDtypeStruct((1, N_EXPERTS), F32),
                   jax.ShapeDtypeStruct((n_act, D), F32),
                   jax.ShapeDtypeStruct((n_act, D // 2), jnp.uint32)],
        scratch_shapes=[pltpu.VMEM((1, N_EXPERTS), F32)],
        compiler_params=_cparams(1),
        name="route_shared",
    )(h2, router, bias, sg, su, sd)


def _dispatch_kernel(pos_ref, h_ref, xs_hbm, sem):
    for k in range(TOP_K):
        def start(t, _, k=k):
            pltpu.make_async_copy(h_ref.at[pl.ds(t, 1)], xs_hbm.at[pl.ds(pos_ref[0, k * TM + t], 1)], sem).start()
            return 0

        lax.fori_loop(0, TM, start, 0, unroll=8)
    for k in range(TOP_K):
        pltpu.make_async_copy(h_ref, xs_hbm.at[pl.ds(0, TM)], sem).wait()


def _dispatch(pos_tiles, hp, n_tiles, n_rows):
    return pl.pallas_call(
        _dispatch_kernel,
        grid=(n_tiles,),
        in_specs=[pl.BlockSpec((None, 1, TOP_K * TM), lambda t: (t, 0, 0), memory_space=pltpu.SMEM),
                  pl.BlockSpec((TM, D // 2), lambda t: (t, 0))],
        out_specs=pl.BlockSpec(memory_space=pl.ANY),
        out_shape=jax.ShapeDtypeStruct((n_rows, D // 2), jnp.uint32),
        scratch_shapes=[pltpu.SemaphoreType.DMA(())],
        compiler_params=_cparams(1),
        name="moe_dispatch",
    )(pos_tiles, hp)


def _expert_kernel(blk_e_ref, valid_ref, x_ref, wg_ref, wu_ref, wd_ref, y_ref):
    n_valid = valid_ref[pl.program_id(0)]

    @pl.when(n_valid > 0)
    def _():
        row = lax.broadcasted_iota(jnp.int32, (MOE_ROWS, 1), 0)
        lo, hi = _unpack_bf16_pairs(jnp.where(row < n_valid, x_ref[...], jnp.uint32(0)))
        x = jnp.concatenate([lo, hi], axis=1).astype(BF16)
        hb = _silu(_dot(x, wg_ref[...].astype(BF16))) * _dot(x, wu_ref[...].astype(BF16))
        y_ref[...] = _pack_bf16_pairs(_dot(hb.astype(BF16), wd_ref[...].astype(BF16)))

    @pl.when(n_valid == 0)
    def _():
        y_ref[...] = jnp.zeros_like(y_ref)


def _experts(blk_e, valid, xs, wg, wu, wd, n_blocks):
    grid_spec = pltpu.PrefetchScalarGridSpec(
        num_scalar_prefetch=2,
        grid=(n_blocks,),
        in_specs=[pl.BlockSpec((MOE_ROWS, D // 2), lambda i, be, nv: (i, 0)),
                  pl.BlockSpec((None, D, D_EXPERT), lambda i, be, nv: (be[i], 0, 0)),
                  pl.BlockSpec((None, D, D_EXPERT), lambda i, be, nv: (be[i], 0, 0)),
                  pl.BlockSpec((None, D_EXPERT, D), lambda i, be, nv: (be[i], 0, 0))],
        out_specs=pl.BlockSpec((MOE_ROWS, D // 2), lambda i, be, nv: (i, 0)),
    )
    return pl.pallas_call(
        _expert_kernel,
        grid_spec=grid_spec,
        out_shape=jax.ShapeDtypeStruct((n_blocks * MOE_ROWS, D // 2), jnp.uint32),
        compiler_params=_cparams(1),
        name="routed_experts",
    )(blk_e, valid, xs, wg, wu, wd)


def _combine_kernel(pos_ref, y_hbm, w_ref, x1_ref, sh_ref, mod_ref, o_ref, buf, sem):
    for k in range(TOP_K):
        def start(t, _, k=k):
            pltpu.make_async_copy(y_hbm.at[pl.ds(pos_ref[0, k * TM + t], 1)], buf.at[pl.ds(k * TM + t, 1)], sem).start()
            return 0

        lax.fori_loop(0, TM, start, 0, unroll=8)
    for k in range(TOP_K):
        pltpu.make_async_copy(y_hbm.at[pl.ds(0, TM)], buf.at[pl.ds(k * TM, TM)], sem).wait()
    f_lo = sh_ref[:, :D // 2]
    f_hi = sh_ref[:, D // 2:]
    for k in range(TOP_K):
        lo, hi = _unpack_bf16_pairs(buf[k * TM:(k + 1) * TM, :])
        f_lo = f_lo + lo * w_ref[:, k:k + 1]
        f_hi = f_hi + hi * w_ref[:, k:k + 1]
    o_ref[...] = x1_ref[...] + mod_ref[:, 5 * D:6 * D] * jnp.concatenate([f_lo, f_hi], axis=1)


def _combine(pos_tiles, ys, w, x1, shared, mod3, n_tiles, tile, mod_row, out_rows, out_tile):
    return pl.pallas_call(
        _combine_kernel,
        grid=(n_tiles,),
        in_specs=[pl.BlockSpec((None, 1, TOP_K * TM), lambda t: (t, 0, 0), memory_space=pltpu.SMEM),
                  pl.BlockSpec(memory_space=pl.ANY),
                  pl.BlockSpec((TM, 128), lambda t: (t, 0)),
                  pl.BlockSpec((TM, D), lambda t: (tile(t), 0)),
                  pl.BlockSpec((TM, D), lambda t: (t, 0)),
                  pl.BlockSpec((None, 1, 6 * D), lambda t: (mod_row(tile(t)), 0, 0))],
        out_specs=pl.BlockSpec((TM, D), lambda t: (out_tile(t), 0)),
        out_shape=jax.ShapeDtypeStruct((out_rows, D), F32),
        scratch_shapes=[pltpu.VMEM((TOP_K * TM, D // 2), jnp.uint32), pltpu.SemaphoreType.DMA(())],
        compiler_params=_cparams(1),
        name="moe_combine",
    )(pos_tiles, ys, w, x1, shared, mod3)


def _moe_plan(idx, rank, counts, n_blocks):
    n = idx.shape[0]
    cnt = counts.reshape(N_EXPERTS).astype(jnp.int32)
    padded = (cnt + MOE_ROWS - 1) // MOE_ROWS * MOE_ROWS
    pad_end = jnp.cumsum(padded)
    pad_start = pad_end - padded
    experts = jnp.arange(N_EXPERTS, dtype=jnp.int32)
    pos = rank + jnp.sum(jnp.where(idx[:, :, None] == experts, pad_start, 0), axis=-1)
    blk_start = jnp.arange(n_blocks, dtype=jnp.int32) * MOE_ROWS
    blk_e = jnp.minimum(jnp.sum(blk_start[:, None] >= pad_end[None, :], axis=1), N_EXPERTS - 1).astype(jnp.int32)
    mine = blk_e[:, None] == experts
    in_expert = blk_start - jnp.sum(jnp.where(mine, pad_start, 0), axis=1)
    valid = jnp.clip(jnp.sum(jnp.where(mine, cnt, 0), axis=1) - in_expert, 0, MOE_ROWS).astype(jnp.int32)
    pos_tiles = pos.reshape(n // TM, TM, TOP_K).transpose(0, 2, 1).reshape(n // TM, 1, TOP_K * TM)
    return blk_e, valid, pos_tiles.astype(jnp.int32)


def _rope_tables(seq, ctx):
    half = MLA_ROPE // 2
    n_freq = half // 2
    inv = ROPE_THETA ** (-2.0 * jnp.arange(n_freq, dtype=F32) / half)
    t = jnp.arange(seq)
    ang_r = (t // GRID_W).astype(F32)[:, None] * inv
    ang_c = (t % GRID_W).astype(F32)[:, None] * inv
    cos = jnp.concatenate([jnp.cos(ang_r), jnp.cos(ang_r), jnp.cos(ang_c), jnp.cos(ang_c)], axis=1)
    sin = jnp.concatenate([-jnp.sin(ang_r), jnp.sin(ang_r), -jnp.sin(ang_c), jnp.sin(ang_c)], axis=1)
    pad_l = MLA_NOPE
    pad_r = HEAD_PAD - MLA_NOPE - MLA_ROPE
    cos = jnp.pad(cos, ((0, ctx), (pad_l, pad_r)), constant_values=1.0)
    cos = cos.at[seq:, :].set(1.0)
    sin = jnp.pad(sin, ((0, ctx), (pad_l, pad_r)))
    return cos, sin


def _pad_heads(w, n_heads, width, offset=0):
    k = w.shape[0]
    w = w.reshape(k, n_heads, width)
    w = jnp.pad(w, ((0, 0), (0, 0), (offset, HEAD_PAD - width - offset)))
    return w.reshape(k, n_heads * HEAD_PAD)


def _pad_vec(g, offset):
    return jnp.pad(g, (offset, HEAD_PAD - g.shape[0] - offset)).reshape(1, HEAD_PAD)


def _reorder_w_in(w):
    off_cq, off_ckv, off_kr, off_uv, off_merge = 2048, 2432, 2688, 2720, 3744
    kr = jnp.pad(w[:, off_kr:off_uv], ((0, 0), (MLA_NOPE, HEAD_PAD - MLA_NOPE - MLA_ROPE)))
    return jnp.concatenate([w[:, off_merge:], w[:, off_uv:off_merge], w[:, :off_cq],
                            w[:, off_ckv:off_kr], kr, w[:, off_cq:off_ckv]], axis=1).astype(BF16)


def kernel(x, c, ctx, c_ctx, ada_w, ada_b, norm1_g, norm2_g, w_in, ret_decay_fwd, ret_decay_bwd, ret_gn_g,
           ret_gn_b, w_br_ret, mla_qa_g, mla_w_uq, mla_kva_g, mla_w_ukv, mla_qn_g, mla_kn_g, mla_kr_g, w_br_mla,
           gmlp_ln_g, gmlp_ln_b, gmlp_ws, gmlp_bs, w_br_gmlp, w_out, moe_router, moe_bias, moe_w_gate, moe_w_up,
           moe_w_down, sh_w_gate, sh_w_up, sh_w_down):
    B, seq, _ = x.shape
    n_ctx = ctx.shape[1]
    depth = ada_w.shape[0]
    assert n_ctx == TM and seq % (ATT_UNROLL * ATT_KV_CHUNK) == 0 and seq % TM == 0
    lt = seq + n_ctx
    tiles_per_b = lt // TM
    lat_tiles_per_b = seq // TM
    ctx_tile = lat_tiles_per_b

    def mod_row(t):
        return jnp.where(t % tiles_per_b == ctx_tile, B, t // tiles_per_b)

    c_rows = jnp.concatenate([c, c_ctx[None, :], jnp.zeros((8 - B - 1, D), F32)], axis=0)
    mod = _ada(c_rows, ada_w, ada_b)
    cos_t, sin_t = _rope_tables(seq, n_ctx)
    xs = jnp.concatenate([x, ctx], axis=1).reshape(B * lt, D)

    for l in range(depth):
        last = l == depth - 1
        mod3 = mod[l].reshape(8, 1, 6 * D)
        p = _in_proj(xs, mod3, norm1_g[l].reshape(1, D), _reorder_w_in(w_in[l]), B * tiles_per_b, mod_row)

        lg = jnp.stack([jax.nn.log_sigmoid(ret_decay_fwd[l].astype(F32)),
                        jax.nn.log_sigmoid(ret_decay_bwd[l].astype(F32))])
        y_ret = _retention(p.reshape(B, lt, N_IN_PAD), lg, ret_gn_g[l].reshape(1, -1), ret_gn_b[l].reshape(1, -1),
                           seq, n_ctx)

        w_ukv = mla_w_ukv[l].reshape(MLA_KV_LORA, MLA_HEADS, MLA_NOPE + MLA_V)
        wk_p = _pad_heads(w_ukv[:, :, :MLA_NOPE].reshape(MLA_KV_LORA, -1), MLA_HEADS, MLA_NOPE).astype(BF16)
        wv = w_ukv[:, :, MLA_NOPE:].reshape(MLA_KV_LORA, MLA_HEADS * MLA_V).astype(BF16)
        wq_p = _pad_heads(mla_w_uq[l], MLA_HEADS, MLA_QK).astype(BF16)
        q, k, v = _mla_prep(p, cos_t, sin_t, mla_qa_g[l].reshape(1, -1), mla_kva_g[l].reshape(1, -1),
                            _pad_vec(mla_qn_g[l], 0), _pad_vec(mla_kn_g[l], 0), _pad_vec(mla_kr_g[l], MLA_NOPE),
                            wq_p, wk_p, wv, B, lt)
        o_mla = _attention(q, k, v, seq, n_ctx, lat_tiles_per_b if last else tiles_per_b)

        if last:
            n_tiles = B * lat_tiles_per_b
            tile = lambda t: (t // lat_tiles_per_b) * tiles_per_b + t % lat_tiles_per_b
        else:
            n_tiles = B * tiles_per_b
            tile = lambda t: t
        bs_full = jnp.broadcast_to(gmlp_bs[l][:, :, None], (GMLP_GROUPS, GMLP_CHUNK, GMLP_CHUNK))
        x1, h2 = _merge(xs, mod3, p, y_ret.reshape(B * lt, -1), o_mla.reshape(B * lt, -1),
                        gmlp_ln_g[l].reshape(1, -1), gmlp_ln_b[l].reshape(1, -1), gmlp_ws[l].astype(BF16), bs_full,
                        w_br_ret[l].astype(BF16), w_br_mla[l].astype(BF16), w_br_gmlp[l].astype(BF16),
                        w_out[l].astype(BF16), norm2_g[l].reshape(1, D), n_tiles, tile, mod_row)

        idx, w, rank, counts, shared, hp = _route(h2, moe_router[l], moe_bias[l].reshape(1, -1),
                                                  sh_w_gate[l].astype(BF16), sh_w_up[l].astype(BF16),
                                                  sh_w_down[l].astype(BF16), n_tiles, tile)
        n_act = n_tiles * TM
        n_blocks = -(-(n_act * TOP_K + N_EXPERTS * (MOE_ROWS - 1)) // MOE_ROWS)
        blk_e, valid, pos_tiles = _moe_plan(idx[:, :TOP_K], rank[:, :TOP_K], counts, n_blocks)
        xg = _dispatch(pos_tiles, hp, n_tiles, n_blocks * MOE_ROWS)
        ys = _experts(blk_e, valid, xg, moe_w_gate[l], moe_w_up[l], moe_w_down[l], n_blocks)
        if last:
            xs = _combine(pos_tiles, ys, w, x1, shared, mod3, n_tiles, tile, mod_row, B * seq, lambda t: t)
        else:
            xs = _combine(pos_tiles, ys, w, x1, shared, mod3, n_tiles, tile, mod_row, B * lt, tile)
    return xs.reshape(B, seq, D)
```

```python
import functools
import math

import jax
import jax.numpy as jnp
from jax import lax
from jax.experimental import pallas as pl
from jax.experimental.pallas import tpu as pltpu
from jax.experimental.pallas import tpu_sc as plsc

F32 = jnp.float32
BF16 = jnp.bfloat16

D = 1024
GRID_W = 64
RET_HEADS = 4
RET_D = 128
RET_CHUNK = 128
RET_OUT_ROWS = 256
MLA_HEADS = 8
MLA_Q_LORA = 384
MLA_KV_LORA = 256
MLA_NOPE = 64
MLA_ROPE = 32
MLA_V = 64
MLA_V_EXT = MLA_V + 16
MLA_QK = MLA_NOPE + MLA_ROPE
HEAD_PAD = 128
ROPE_THETA = 10000.0
GMLP_GROUPS = 4
GMLP_W = 512
GMLP_CHUNK = 128
N_EXPERTS = 64
TOP_K = 6
D_EXPERT = 256
ROUTED_SCALE = 2.5
EPS = 1e-6
LOG2_E = 1.4426950408889634

TM = 256
MOE_ROWS = 256
ATT_KV_CHUNK = 512
ATT_UNROLL = 16

C_MERGE = 0
C_UV = 3072
C_RET = 4096
C_CKV = 6144
C_KR = 6400
C_CQ = 6528
N_IN_PAD = 6912
IN_CHUNK = 768

VMEM_LIMIT = 56 * 1024 * 1024

SC_CORES = 2
SC_SUBCORES = 16
SC_ROWS = 128


def _cparams(n_axes, vmem=VMEM_LIMIT):
    return pltpu.CompilerParams(dimension_semantics=("arbitrary",) * n_axes, vmem_limit_bytes=vmem)


def _silu(x):
    return x * jax.nn.sigmoid(x)


def _dot(a, b):
    return jnp.dot(a, b, preferred_element_type=F32)


def _dot_nt(a, b):
    return lax.dot_general(a, b, (((1,), (1,)), ((), ())), preferred_element_type=F32)


def _dot_tn(a, b):
    return lax.dot_general(a, b, (((0,), (0,)), ((), ())), preferred_element_type=F32)


def _pack_bf16_pairs(x):
    n = x.shape[1] // 2
    lo = lax.bitcast_convert_type(x[:, :n].astype(BF16).astype(F32), jnp.uint32)
    hi = lax.bitcast_convert_type(x[:, n:].astype(BF16).astype(F32), jnp.uint32)
    return (lo >> 16) | hi


def _unpack_bf16_pairs(u):
    lo = lax.bitcast_convert_type(u << 16, F32)
    hi = lax.bitcast_convert_type(u & jnp.uint32(0xFFFF0000), F32)
    return lo, hi


def _ada_kernel(c_ref, w_ref, b_ref, o_ref):
    s = _silu(c_ref[...])
    o_ref[...] = _dot(s.astype(BF16), w_ref[...].astype(BF16)) + b_ref[...]


def _ada(c_rows, ada_w, ada_b):
    depth = ada_w.shape[0]
    n = ada_w.shape[2]
    cw = 1536
    return pl.pallas_call(
        _ada_kernel,
        grid=(depth, n // cw),
        in_specs=[pl.BlockSpec((8, D), lambda l, j: (0, 0)),
                  pl.BlockSpec((None, D, cw), lambda l, j: (l, 0, j)),
                  pl.BlockSpec((None, 1, cw), lambda l, j: (l, 0, j))],
        out_specs=pl.BlockSpec((None, 8, cw), lambda l, j: (l, 0, j)),
        out_shape=jax.ShapeDtypeStruct((depth, 8, n), F32),
        compiler_params=_cparams(2),
        name="ada_mod",
    )(c_rows, ada_w, ada_b.reshape(depth, 1, n))


def _modulated_rmsnorm(x, g, shift, scale):
    y = x * lax.rsqrt(jnp.mean(x * x, axis=-1, keepdims=True) + EPS) * g
    return y * (1.0 + scale) + shift


def _in_proj_kernel(x_ref, mod_ref, g_ref, w_ref, o_ref, h_scr):
    h = _modulated_rmsnorm(x_ref[...], g_ref[...], mod_ref[:, 0:D], mod_ref[:, D:2 * D])
    h_scr[...] = h.astype(BF16)
    for c in range(N_IN_PAD // IN_CHUNK):
        cols = slice(c * IN_CHUNK, (c + 1) * IN_CHUNK)
        o_ref[:, cols] = _dot(h_scr[...], w_ref[:, cols]).astype(BF16)


def _in_proj(xs, mod3, g, w_in_r, n_tiles, mod_row):
    n_rows = xs.shape[0]
    return pl.pallas_call(
        _in_proj_kernel,
        grid=(n_tiles,),
        in_specs=[pl.BlockSpec((TM, D), lambda t: (t, 0)),
                  pl.BlockSpec((None, 1, 6 * D), lambda t: (mod_row(t), 0, 0)),
                  pl.BlockSpec((1, D), lambda t: (0, 0)),
                  pl.BlockSpec((D, N_IN_PAD), lambda t: (0, 0))],
        out_specs=pl.BlockSpec((TM, N_IN_PAD), lambda t: (t, 0)),
        out_shape=jax.ShapeDtypeStruct((n_rows, N_IN_PAD), BF16),
        scratch_shapes=[pltpu.VMEM((TM, D), BF16)],
        compiler_params=_cparams(1),
        name="in_proj",
    )(xs, mod3, g, w_in_r)


def _retention_kernel(lg_ref, q_ref, k_ref, v_ref, g_ref, gng_ref, gnb_ref, y_ref, of_scr, ob_scr,
                      *, n_lat_chunks, n_ctx_chunks):
    h = pl.program_id(1)
    lg_f = lg_ref[0, h]
    lg_b = lg_ref[1, h]
    C = RET_CHUNK
    k_scale = RET_D ** -0.5
    ri = lax.broadcasted_iota(jnp.int32, (C, C), 0).astype(F32)
    ci = lax.broadcasted_iota(jnp.int32, (C, C), 1).astype(F32)
    pos = lax.broadcasted_iota(jnp.int32, (C, 1), 0).astype(F32)
    diff = ri - ci
    d_f = jnp.where(diff >= 0, jnp.exp(lg_f * jnp.maximum(diff, 0.0)), 0.0) * k_scale
    d_b = jnp.where(diff < 0, jnp.exp(lg_b * jnp.maximum(-diff, 0.0)), 0.0) * k_scale
    qdec_f = jnp.exp(lg_f * (pos + 1.0))
    kdec_f = jnp.exp(lg_f * (C - 1.0 - pos)) * k_scale
    cdec_f = jnp.exp(lg_f * C)
    qdec_b = jnp.exp(lg_b * (C - pos))
    kdec_b = jnp.exp(lg_b * pos) * k_scale
    cdec_b = jnp.exp(lg_b * C)

    def chunk(c, state, dmat, qdec, kdec, cdec):
        rows = pl.ds(pl.multiple_of(c * C, C), C)
        q = q_ref[rows, :]
        k = k_ref[rows, :]
        v = v_ref[rows, :]
        att = (_dot_nt(q, k) * dmat).astype(BF16)
        o = _dot(att, v) + _dot((q.astype(F32) * qdec).astype(BF16), state.astype(BF16))
        kd = (k.astype(F32) * kdec).astype(BF16)
        return rows, o, state * cdec + _dot_tn(kd, v)

    n_all = n_lat_chunks + n_ctx_chunks

    def scan_body(i, states):
        s_f, s_b = states
        c_f = jnp.where(i < n_ctx_chunks, n_lat_chunks + i, i - n_ctx_chunks)
        rows, o, s_f = chunk(c_f, s_f, d_f, qdec_f, kdec_f, cdec_f)
        of_scr[rows, :] = o
        rows, o, s_b = chunk(n_all - 1 - i, s_b, d_b, qdec_b, kdec_b, cdec_b)
        ob_scr[rows, :] = o
        return s_f, s_b

    zero = jnp.zeros((RET_D, RET_D), F32)
    lax.fori_loop(0, n_all, scan_body, (zero, zero), unroll=2)

    def out_body(c, _):
        rows = pl.ds(pl.multiple_of(c * RET_OUT_ROWS, RET_OUT_ROWS), RET_OUT_ROWS)
        o = of_scr[rows, :] + ob_scr[rows, :]
        mu = jnp.mean(o, axis=-1, keepdims=True)
        var = jnp.mean(jnp.square(o - mu), axis=-1, keepdims=True)
        on = (o - mu) * lax.rsqrt(var + EPS)
        y = _silu(g_ref[rows, :].astype(F32)) * (on * gng_ref[...] + gnb_ref[...])
        y_ref[rows, :] = y.astype(BF16)
        return 0

    lax.fori_loop(0, n_all * C // RET_OUT_ROWS, out_body, 0)


def _retention(p3, lg, gn_g, gn_b, seq, ctx):
    B, lt, _ = p3.shape
    base = C_RET // RET_D
    kern = functools.partial(_retention_kernel, n_lat_chunks=seq // RET_CHUNK, n_ctx_chunks=ctx // RET_CHUNK)

    def col(off):
        return pl.BlockSpec((None, lt, RET_D), lambda b, h: (b, 0, base + off * RET_HEADS + h))

    return pl.pallas_call(
        kern,
        grid=(B, RET_HEADS),
        in_specs=[pl.BlockSpec(memory_space=pltpu.SMEM),
                  col(0), col(1), col(2), col(3),
                  pl.BlockSpec((1, RET_D), lambda b, h: (0, h)),
                  pl.BlockSpec((1, RET_D), lambda b, h: (0, h))],
        out_specs=pl.BlockSpec((None, lt, RET_D), lambda b, h: (b, 0, h)),
        out_shape=jax.ShapeDtypeStruct((B, lt, RET_HEADS * RET_D), BF16),
        scratch_shapes=[pltpu.VMEM((lt, RET_D), F32), pltpu.VMEM((lt, RET_D), F32)],
        compiler_params=_cparams(2),
        name="retention",
    )(lg, p3, p3, p3, p3, gn_g, gn_b)


def _rope_rotate(x, first_half):
    return jnp.where(first_half, pltpu.roll(x, HEAD_PAD - 8, 1), pltpu.roll(x, 8, 1))


def _mla_prep_kernel(cq_ref, ckv_ref, kr_ref, cos_ref, sin_ref, qa_ref, kva_ref, qn_ref, kn_ref, krg_ref,
                     wq_ref, wk_ref, wv_ref, q_ref, k_ref, v_ref):
    lane = lax.broadcasted_iota(jnp.int32, (1, HEAD_PAD), 1)
    first_half = (lane % 16) < 8
    cos = cos_ref[...]
    sin = sin_ref[...]

    def rms(x, n):
        return x * lax.rsqrt(jnp.sum(x * x, axis=-1, keepdims=True) * (1.0 / n) + EPS)

    def rope(x):
        return x * cos + _rope_rotate(x, first_half) * sin

    cq = cq_ref[...].astype(F32)
    cqn = (rms(cq, MLA_Q_LORA) * qa_ref[...]).astype(BF16)
    q_all = _dot(cqn, wq_ref[...])
    ckv = ckv_ref[...].astype(F32)
    ckvn = (rms(ckv, MLA_KV_LORA) * kva_ref[...]).astype(BF16)
    k_all = _dot(ckvn, wk_ref[...])
    v_all = _dot(ckvn, wv_ref[...])
    k_rope = rope(rms(kr_ref[...].astype(F32), MLA_ROPE) * krg_ref[...])
    scale = MLA_QK ** -0.5 * LOG2_E
    v_t = v_all.T
    ones_row = jnp.where(lax.broadcasted_iota(jnp.int32, (MLA_V_EXT - MLA_V, TM), 0) == 0, 1.0, 0.0)
    for h in range(MLA_HEADS):
        cols = slice(h * HEAD_PAD, (h + 1) * HEAD_PAD)
        qh = rope(rms(q_all[:, cols], MLA_QK) * qn_ref[...]) * scale
        q_ref[h] = qh.astype(BF16)
        kh = rms(k_all[:, cols], MLA_NOPE) * kn_ref[...] + k_rope
        k_ref[h] = kh.astype(BF16)
        v_ref[h] = jnp.concatenate([v_t[h * MLA_V:(h + 1) * MLA_V, :], ones_row], axis=0).astype(BF16)


def _mla_prep(p, cos_t, sin_t, qa_g, kva_g, qn_p, kn_p, kr_p, wq_p, wk_p, wv, B, lt):
    tiles_per_b = lt // TM
    hw = MLA_HEADS * HEAD_PAD
    const = lambda shape: pl.BlockSpec(shape, lambda b, j: (0,) * len(shape))
    head_out = pl.BlockSpec((None, MLA_HEADS, TM, HEAD_PAD), lambda b, j: (b, 0, j, 0))
    shp = jax.ShapeDtypeStruct((B, MLA_HEADS, lt, HEAD_PAD), BF16)
    v_out = pl.BlockSpec((None, MLA_HEADS, None, MLA_V_EXT, TM), lambda b, j: (b, 0, j, 0, 0))
    v_shp = jax.ShapeDtypeStruct((B, MLA_HEADS, tiles_per_b, MLA_V_EXT, TM), BF16)
    return pl.pallas_call(
        _mla_prep_kernel,
        grid=(B, tiles_per_b),
        in_specs=[pl.BlockSpec((TM, MLA_Q_LORA), lambda b, j: (b * tiles_per_b + j, C_CQ // MLA_Q_LORA)),
                  pl.BlockSpec((TM, MLA_KV_LORA), lambda b, j: (b * tiles_per_b + j, C_CKV // MLA_KV_LORA)),
                  pl.BlockSpec((TM, HEAD_PAD), lambda b, j: (b * tiles_per_b + j, C_KR // HEAD_PAD)),
                  pl.BlockSpec((TM, HEAD_PAD), lambda b, j: (j, 0)),
                  pl.BlockSpec((TM, HEAD_PAD), lambda b, j: (j, 0)),
                  const((1, MLA_Q_LORA)), const((1, MLA_KV_LORA)),
                  const((1, HEAD_PAD)), const((1, HEAD_PAD)), const((1, HEAD_PAD)),
                  const((MLA_Q_LORA, hw)), const((MLA_KV_LORA, hw)), const((MLA_KV_LORA, MLA_HEADS * MLA_V))],
        out_specs=[head_out, head_out, v_out],
        out_shape=[shp, shp, v_shp],
        compiler_params=_cparams(2),
        name="mla_prep",
    )(p, p, p, cos_t, sin_t, qa_g, kva_g, qn_p, kn_p, kr_p, wq_p, wk_p, wv)


def _attention_kernel(q_ref, k_ref, v_ref, o_ref, s_scr, *, seq, ctx, ctx_tile):
    i = pl.program_id(2)
    n_blk = ATT_KV_CHUNK // TM
    n_chunks = seq // ATT_KV_CHUNK
    unroll = math.gcd(n_chunks, ATT_UNROLL)

    def scores(hh, slot, blk, nb):
        start = blk * TM if isinstance(blk, int) else pl.multiple_of(blk * TM, TM)
        s_scr[hh, slot, 0:nb * TM, :] = _dot_nt(k_ref[hh, pl.ds(start, nb * TM), :], q_ref[hh])

    def absorb(hh, slot, blk, nb, carry):
        m, acc = carry
        s = s_scr[hh, slot, 0:nb * TM, :]
        m_new = jnp.maximum(m, jnp.max(s, axis=0, keepdims=True))
        p = jnp.exp2(s - m_new).astype(BF16)
        acc = jnp.exp2(m - m_new) * acc
        for j in range(nb):
            acc = acc + _dot(v_ref[hh, blk + j], p[j * TM:(j + 1) * TM, :])
        return m_new, acc

    def init():
        return (jnp.full((1, TM), -jnp.inf, F32), jnp.zeros((MLA_V_EXT, TM), F32))

    def write(carries):
        outs = [acc[0:MLA_V, :] / acc[MLA_V:MLA_V + 1, :] for _, acc in carries]
        o_ref[...] = jnp.concatenate(outs, axis=0).T.astype(BF16)

    def step(carries, slot, blk, nb, next_blk, next_nb):
        out = []
        for hh in range(2):
            if next_blk is not None:
                scores(hh, 1 - slot, next_blk, next_nb)
            out.append(absorb(hh, slot, blk, nb, carries[hh]))
        return tuple(out)

    ctx_blk = seq // TM
    ctx_nb = ctx // TM

    @pl.when(i != ctx_tile)
    def _():
        for hh in range(2):
            scores(hh, 0, ctx_blk, ctx_nb)
        carries = step((init(), init()), 0, ctx_blk, ctx_nb, 0, n_blk)
        last_blk = (n_chunks - 1) * n_blk

        def body(c, carries):
            for u in range(unroll):
                blk = (c * unroll + u) * n_blk
                carries = step(carries, (1 + u) % 2, blk, n_blk, jnp.minimum(blk + n_blk, last_blk), n_blk)
            return carries

        if n_chunks == unroll:
            for u in range(n_chunks):
                nxt = (u + 1) * n_blk if u + 1 < n_chunks else None
                carries = step(carries, (1 + u) % 2, u * n_blk, n_blk, nxt, n_blk)
            write(carries)
        else:
            write(lax.fori_loop(0, n_chunks // unroll, body, carries))

    @pl.when(i == ctx_tile)
    def _():
        for hh in range(2):
            scores(hh, 0, ctx_blk, ctx_nb)
        write(step((init(), init()), 0, ctx_blk, ctx_nb, None, None))


def _attention(q, k, v, seq, ctx, n_q_tiles):
    B, H, lt, _ = q.shape
    kern = functools.partial(_attention_kernel, seq=seq, ctx=ctx, ctx_tile=seq // TM)
    return pl.pallas_call(
        kern,
        grid=(B, H // 2, n_q_tiles),
        in_specs=[pl.BlockSpec((None, 2, TM, HEAD_PAD), lambda b, h, i: (b, h, i, 0)),
                  pl.BlockSpec((None, 2, lt, HEAD_PAD), lambda b, h, i: (b, h, 0, 0)),
                  pl.BlockSpec((None, 2, lt // TM, MLA_V_EXT, TM), lambda b, h, i: (b, h, 0, 0, 0))],
        out_specs=pl.BlockSpec((None, TM, HEAD_PAD), lambda b, h, i: (b, i, h)),
        out_shape=jax.ShapeDtypeStruct((B, lt, (H // 2) * HEAD_PAD), BF16),
        scratch_shapes=[pltpu.VMEM((2, 2, ATT_KV_CHUNK, TM), F32)],
        compiler_params=_cparams(3),
        name="attention",
    )(q, k, v)


def _merge_kernel(x_ref, mod_ref, mg_ref, uv_ref, yr_ref, om_ref, lng_ref, lnb_ref, ws_ref, bs_ref,
                  wr_ref, wm_ref, wg_ref, wo_ref, n2_ref, x1_ref, h2_ref):
    yr = _dot(yr_ref[...], wr_ref[...])
    ym = _dot(om_ref[...], wm_ref[...])
    z = jax.nn.gelu(uv_ref[...].astype(F32))
    u = z[:, :GMLP_W]
    v = z[:, GMLP_W:]
    mu = jnp.mean(v, axis=-1, keepdims=True)
    var = jnp.mean(jnp.square(v - mu), axis=-1, keepdims=True)
    vn = ((v - mu) * lax.rsqrt(var + EPS) * lng_ref[...] + lnb_ref[...]).astype(BF16)
    gw = GMLP_W // GMLP_GROUPS
    chunks = []
    for c in range(TM // GMLP_CHUNK):
        rows = slice(c * GMLP_CHUNK, (c + 1) * GMLP_CHUNK)
        groups = [_dot(ws_ref[g], vn[rows, g * gw:(g + 1) * gw]) + bs_ref[g] for g in range(GMLP_GROUPS)]
        chunks.append(jnp.concatenate(groups, axis=1))
    sv = jnp.concatenate(chunks, axis=0)
    yg = _dot((u * sv).astype(BF16), wg_ref[...])
    gate = jax.nn.sigmoid(mg_ref[...].astype(F32))
    y = gate[:, :D] * yr + gate[:, D:2 * D] * ym + gate[:, 2 * D:] * yg
    out = _dot(y.astype(BF16), wo_ref[...])
    x1 = x_ref[...] + mod_ref[:, 2 * D:3 * D] * out
    x1_ref[...] = x1
    h2_ref[...] = _modulated_rmsnorm(x1, n2_ref[...], mod_ref[:, 3 * D:4 * D], mod_ref[:, 4 * D:5 * D])


def _merge(xs, mod3, p, y_ret, o_mla, ln_g, ln_b, ws, bs_full, w_br_ret, w_br_mla, w_br_gmlp, w_out, n2_g,
           n_tiles, tile, mod_row):
    n_rows = xs.shape[0]
    const = lambda shape: pl.BlockSpec(shape, lambda t: (0,) * len(shape))
    row = lambda w, cb=0: pl.BlockSpec((TM, w), lambda t: (tile(t), cb))
    shp = jax.ShapeDtypeStruct((n_rows, D), F32)
    return pl.pallas_call(
        _merge_kernel,
        grid=(n_tiles,),
        in_specs=[row(D),
                  pl.BlockSpec((None, 1, 6 * D), lambda t: (mod_row(tile(t)), 0, 0)),
                  row(3 * D, C_MERGE // (3 * D)), row(D, C_UV // D),
                  row(RET_HEADS * RET_D), row(MLA_HEADS * MLA_V),
                  const((1, GMLP_W)), const((1, GMLP_W)),
                  const((GMLP_GROUPS, GMLP_CHUNK, GMLP_CHUNK)), const((GMLP_GROUPS, GMLP_CHUNK, GMLP_CHUNK)),
                  const((RET_HEADS * RET_D, D)), const((MLA_HEADS * MLA_V, D)), const((GMLP_W, D)),
                  const((D, D)), const((1, D))],
        out_specs=[row(D), row(D)],
        out_shape=[shp, shp],
        compiler_params=_cparams(1),
        name="merge",
    )(xs, mod3, p, p, y_ret, o_mla, ln_g, ln_b, ws, bs_full, w_br_ret, w_br_mla, w_br_gmlp, w_out, n2_g)


def _route_kernel(h_ref, r_ref, b_ref, sg_ref, su_ref, sd_ref, idx_ref, w_ref, rank_ref, cnt_ref, sh_ref, hp_ref,
                  cnt_scr):
    @pl.when(pl.program_id(0) == 0)
    def _():
        cnt_scr[...] = jnp.zeros_like(cnt_scr)

    h = h_ref[...]
    logits = jnp.dot(h, r_ref[...], preferred_element_type=F32, precision=lax.Precision.HIGHEST)
    scores = jax.nn.sigmoid(logits)
    sel = scores + b_ref[...]
    lane_e = lax.broadcasted_iota(jnp.int32, (TM, N_EXPERTS), 1).astype(F32)
    lane_o = lax.broadcasted_iota(jnp.int32, (TM, 128), 1)
    idx_out = jnp.zeros((TM, 128), F32)
    w_out = jnp.zeros((TM, 128), F32)
    hits = []
    for k in range(TOP_K):
        best = jnp.max(sel, axis=-1, keepdims=True)
        pick = jnp.min(jnp.where(sel == best, lane_e, float(N_EXPERTS)), axis=-1, keepdims=True)
        hit = lane_e == pick
        hits.append(hit)
        wk = jnp.sum(jnp.where(hit, scores, 0.0), axis=-1, keepdims=True)
        sel = jnp.where(hit, -jnp.inf, sel)
        idx_out = jnp.where(lane_o == k, pick, idx_out)
        w_out = jnp.where(lane_o == k, wk, w_out)
    w_out = w_out / jnp.sum(w_out, axis=-1, keepdims=True) * ROUTED_SCALE
    idx_ref[...] = idx_out.astype(jnp.int32)
    w_ref[...] = w_out
    chosen = jnp.zeros((TM, N_EXPERTS), F32)
    for hit in hits:
        chosen = jnp.where(hit, 1.0, chosen)
    earlier = (lax.broadcasted_iota(jnp.int32, (TM, TM), 0) > lax.broadcasted_iota(jnp.int32, (TM, TM), 1))
    before = _dot(jnp.where(earlier, 1.0, 0.0).astype(BF16), chosen.astype(BF16)) + cnt_scr[...]
    rank_out = jnp.zeros((TM, 128), F32)
    for k, hit in enumerate(hits):
        rank_out = jnp.where(lane_o == k, jnp.sum(jnp.where(hit, before, 0.0), axis=-1, keepdims=True), rank_out)
    rank_ref[...] = rank_out.astype(jnp.int32)
    cnt_scr[...] += jnp.sum(chosen, axis=0, keepdims=True)
    cnt_ref[...] = cnt_scr[...]
    hb = h.astype(BF16)
    a = _silu(_dot(hb, sg_ref[...])) * _dot(hb, su_ref[...])
    sh_ref[...] = _dot(a.astype(BF16), sd_ref[...])
    hp_ref[...] = _pack_bf16_pairs(h)


def _route(h2, router, bias, sg, su, sd, n_tiles, tile):
    const = lambda shape: pl.BlockSpec(shape, lambda t: (0,) * len(shape))
    n_act = n_tiles * TM
    return pl.pallas_call(
        _route_kernel,
        grid=(n_tiles,),
        in_specs=[pl.BlockSpec((TM, D), lambda t: (tile(t), 0)),
                  const((D, N_EXPERTS)), const((1, N_EXPERTS)),
                  const((D, D_EXPERT)), const((D, D_EXPERT)), const((D_EXPERT, D))],
        out_specs=[pl.BlockSpec((TM, 128), lambda t: (t, 0)),
                   pl.BlockSpec((TM, 128), lambda t: (t, 0)),
                   pl.BlockSpec((TM, 128), lambda t: (t, 0)),
                   pl.BlockSpec((1, N_EXPERTS), lambda t: (0, 0)),
                   pl.BlockSpec((TM, D), lambda t: (t, 0)),
                   pl.BlockSpec((TM, D // 2), lambda t: (t, 0))],
        out_shape=[jax.ShapeDtypeStruct((n_act, 128), jnp.int32),
                   jax.ShapeDtypeStruct((n_act, 128), F32),
                   jax.ShapeDtypeStruct((n_act, 128), jnp.int32),
                   jax.ShapeDtypeStruct((1, N_EXPERTS), F32),
                   jax.ShapeDtypeStruct((n_act, D), F32),
                   jax.ShapeDtypeStruct((n_act, D // 2), jnp.uint32)],
        scratch_shapes=[pltpu.VMEM((1, N_EXPERTS), F32)],
        compiler_params=_cparams(1),
        name="route_shared",
    )(h2, router, bias, sg, su, sd)


def _dispatch(pos_sc, hp, n_rows):
    n_batches = pos_sc.shape[0]
    n_workers = SC_CORES * SC_SUBCORES
    mesh = plsc.VectorSubcoreMesh(core_axis_name="c", subcore_axis_name="s")

    @functools.partial(
        pl.kernel, mesh=mesh,
        out_type=jax.ShapeDtypeStruct((n_rows, D // 2), jnp.uint32),
        scratch_types=[pltpu.VMEM((TOP_K, SC_ROWS), jnp.int32),
                       pltpu.VMEM((SC_ROWS, D // 2), jnp.uint32),
                       pltpu.SemaphoreType.DMA],
        name="moe_dispatch")
    def scatter(pos_hbm, h_hbm, xs_hbm, idx_v, rows_v, sem):
        worker = lax.axis_index("s") * SC_CORES + lax.axis_index("c")

        @pl.loop(0, pl.cdiv(n_batches, n_workers))
        def _(j):
            b = j * n_workers + worker

            @pl.when(b < n_batches)
            def _():
                pltpu.sync_copy(pos_hbm.at[b], idx_v)
                pltpu.sync_copy(h_hbm.at[pl.ds(b * SC_ROWS, SC_ROWS)], rows_v)
                copies = [pltpu.async_copy(rows_v, xs_hbm.at[idx_v.at[k]], sem) for k in range(TOP_K)]
                for cp in copies:
                    cp.wait()

    return scatter(pos_sc, hp)


def _expert_kernel(blk_e_ref, valid_ref, x_ref, wg_ref, wu_ref, wd_ref, y_ref):
    n_valid = valid_ref[pl.program_id(0)]

    @pl.when(n_valid > 0)
    def _():
        row = lax.broadcasted_iota(jnp.int32, (MOE_ROWS, 1), 0)
        lo, hi = _unpack_bf16_pairs(jnp.where(row < n_valid, x_ref[...], jnp.uint32(0)))
        x = jnp.concatenate([lo, hi], axis=1).astype(BF16)
        hb = _silu(_dot(x, wg_ref[...].astype(BF16))) * _dot(x, wu_ref[...].astype(BF16))
        y_ref[...] = _pack_bf16_pairs(_dot(hb.astype(BF16), wd_ref[...].astype(BF16)))

    @pl.when(n_valid == 0)
    def _():
        y_ref[...] = jnp.zeros_like(y_ref)


def _experts(blk_e, valid, xs, wg, wu, wd, n_blocks):
    grid_spec = pltpu.PrefetchScalarGridSpec(
        num_scalar_prefetch=2,
        grid=(n_blocks,),
        in_specs=[pl.BlockSpec((MOE_ROWS, D // 2), lambda i, be, nv: (i, 0)),
                  pl.BlockSpec((None, D, D_EXPERT), lambda i, be, nv: (be[i], 0, 0)),
                  pl.BlockSpec((None, D, D_EXPERT), lambda i, be, nv: (be[i], 0, 0)),
                  pl.BlockSpec((None, D_EXPERT, D), lambda i, be, nv: (be[i], 0, 0))],
        out_specs=pl.BlockSpec((MOE_ROWS, D // 2), lambda i, be, nv: (i, 0)),
    )
    return pl.pallas_call(
        _expert_kernel,
        grid_spec=grid_spec,
        out_shape=jax.ShapeDtypeStruct((n_blocks * MOE_ROWS, D // 2), jnp.uint32),
        compiler_params=_cparams(1),
        name="routed_experts",
    )(blk_e, valid, xs, wg, wu, wd)


def _combine_kernel(pos_ref, y_hbm, w_ref, x1_ref, sh_ref, mod_ref, o_ref, buf, sem):
    for k in range(TOP_K):
        def start(t, _, k=k):
            pltpu.make_async_copy(y_hbm.at[pl.ds(pos_ref[0, k * TM + t], 1)], buf.at[pl.ds(k * TM + t, 1)], sem).start()
            return 0

        lax.fori_loop(0, TM, start, 0, unroll=8)
    for k in range(TOP_K):
        pltpu.make_async_copy(y_hbm.at[pl.ds(0, TM)], buf.at[pl.ds(k * TM, TM)], sem).wait()
    f_lo = sh_ref[:, :D // 2]
    f_hi = sh_ref[:, D // 2:]
    for k in range(TOP_K):
        lo, hi = _unpack_bf16_pairs(buf[k * TM:(k + 1) * TM, :])
        f_lo = f_lo + lo * w_ref[:, k:k + 1]
        f_hi = f_hi + hi * w_ref[:, k:k + 1]
    o_ref[...] = x1_ref[...] + mod_ref[:, 5 * D:6 * D] * jnp.concatenate([f_lo, f_hi], axis=1)


def _combine(pos_tiles, ys, w, x1, shared, mod3, n_tiles, tile, mod_row, out_rows, out_tile):
    return pl.pallas_call(
        _combine_kernel,
        grid=(n_tiles,),
        in_specs=[pl.BlockSpec((None, 1, TOP_K * TM), lambda t: (t, 0, 0), memory_space=pltpu.SMEM),
                  pl.BlockSpec(memory_space=pl.ANY),
                  pl.BlockSpec((TM, 128), lambda t: (t, 0)),
                  pl.BlockSpec((TM, D), lambda t: (tile(t), 0)),
                  pl.BlockSpec((TM, D), lambda t: (t, 0)),
                  pl.BlockSpec((None, 1, 6 * D), lambda t: (mod_row(tile(t)), 0, 0))],
        out_specs=pl.BlockSpec((TM, D), lambda t: (out_tile(t), 0)),
        out_shape=jax.ShapeDtypeStruct((out_rows, D), F32),
        scratch_shapes=[pltpu.VMEM((TOP_K * TM, D // 2), jnp.uint32), pltpu.SemaphoreType.DMA(())],
        compiler_params=_cparams(1),
        name="moe_combine",
    )(pos_tiles, ys, w, x1, shared, mod3)


def _moe_plan(idx, rank, counts, n_blocks):
    n = idx.shape[0]
    cnt = counts.reshape(N_EXPERTS).astype(jnp.int32)
    padded = (cnt + MOE_ROWS - 1) // MOE_ROWS * MOE_ROWS
    pad_end = jnp.cumsum(padded)
    pad_start = pad_end - padded
    experts = jnp.arange(N_EXPERTS, dtype=jnp.int32)
    pos = rank + jnp.sum(jnp.where(idx[:, :, None] == experts, pad_start, 0), axis=-1)
    blk_start = jnp.arange(n_blocks, dtype=jnp.int32) * MOE_ROWS
    blk_e = jnp.minimum(jnp.sum(blk_start[:, None] >= pad_end[None, :], axis=1), N_EXPERTS - 1).astype(jnp.int32)
    mine = blk_e[:, None] == experts
    in_expert = blk_start - jnp.sum(jnp.where(mine, pad_start, 0), axis=1)
    valid = jnp.clip(jnp.sum(jnp.where(mine, cnt, 0), axis=1) - in_expert, 0, MOE_ROWS).astype(jnp.int32)
    pos = pos.astype(jnp.int32)
    pos_tiles = pos.reshape(n // TM, TM, TOP_K).transpose(0, 2, 1).reshape(n // TM, 1, TOP_K * TM)
    pos_sc = pos.reshape(n // SC_ROWS, SC_ROWS, TOP_K).transpose(0, 2, 1)
    return blk_e, valid, pos_tiles, pos_sc


def _rope_tables(seq, ctx):
    half = MLA_ROPE // 2
    n_freq = half // 2
    inv = ROPE_THETA ** (-2.0 * jnp.arange(n_freq, dtype=F32) / half)
    t = jnp.arange(seq)
    ang_r = (t // GRID_W).astype(F32)[:, None] * inv
    ang_c = (t % GRID_W).astype(F32)[:, None] * inv
    cos = jnp.concatenate([jnp.cos(ang_r), jnp.cos(ang_r), jnp.cos(ang_c), jnp.cos(ang_c)], axis=1)
    sin = jnp.concatenate([-jnp.sin(ang_r), jnp.sin(ang_r), -jnp.sin(ang_c), jnp.sin(ang_c)], axis=1)
    pad_l = MLA_NOPE
    pad_r = HEAD_PAD - MLA_NOPE - MLA_ROPE
    cos = jnp.pad(cos, ((0, ctx), (pad_l, pad_r)), constant_values=1.0)
    cos = cos.at[seq:, :].set(1.0)
    sin = jnp.pad(sin, ((0, ctx), (pad_l, pad_r)))
    return cos, sin


def _pad_heads(w, n_heads, width, offset=0):
    k = w.shape[0]
    w = w.reshape(k, n_heads, width)
    w = jnp.pad(w, ((0, 0), (0, 0), (offset, HEAD_PAD - width - offset)))
    return w.reshape(k, n_heads * HEAD_PAD)


def _pad_vec(g, offset):
    return jnp.pad(g, (offset, HEAD_PAD - g.shape[0] - offset)).reshape(1, HEAD_PAD)


def _reorder_w_in(w):
    off_cq, off_ckv, off_kr, off_uv, off_merge = 2048, 2432, 2688, 2720, 3744
    kr = jnp.pad(w[:, off_kr:off_uv], ((0, 0), (MLA_NOPE, HEAD_PAD - MLA_NOPE - MLA_ROPE)))
    return jnp.concatenate([w[:, off_merge:], w[:, off_uv:off_merge], w[:, :off_cq],
                            w[:, off_ckv:off_kr], kr, w[:, off_cq:off_ckv]], axis=1).astype(BF16)


def kernel(x, c, ctx, c_ctx, ada_w, ada_b, norm1_g, norm2_g, w_in, ret_decay_fwd, ret_decay_bwd, ret_gn_g,
           ret_gn_b, w_br_ret, mla_qa_g, mla_w_uq, mla_kva_g, mla_w_ukv, mla_qn_g, mla_kn_g, mla_kr_g, w_br_mla,
           gmlp_ln_g, gmlp_ln_b, gmlp_ws, gmlp_bs, w_br_gmlp, w_out, moe_router, moe_bias, moe_w_gate, moe_w_up,
           moe_w_down, sh_w_gate, sh_w_up, sh_w_down):
    B, seq, _ = x.shape
    n_ctx = ctx.shape[1]
    depth = ada_w.shape[0]
    assert n_ctx == TM and seq % (2 * ATT_KV_CHUNK) == 0 and seq % TM == 0
    lt = seq + n_ctx
    tiles_per_b = lt // TM
    lat_tiles_per_b = seq // TM
    ctx_tile = lat_tiles_per_b

    def mod_row(t):
        return jnp.where(t % tiles_per_b == ctx_tile, B, t // tiles_per_b)

    c_rows = jnp.concatenate([c, c_ctx[None, :], jnp.zeros((8 - B - 1, D), F32)], axis=0)
    mod = _ada(c_rows, ada_w, ada_b)
    cos_t, sin_t = _rope_tables(seq, n_ctx)
    xs = jnp.concatenate([x, ctx], axis=1).reshape(B * lt, D)

    for l in range(depth):
        last = l == depth - 1
        mod3 = mod[l].reshape(8, 1, 6 * D)
        p = _in_proj(xs, mod3, norm1_g[l].reshape(1, D), _reorder_w_in(w_in[l]), B * tiles_per_b, mod_row)

        lg = jnp.stack([jax.nn.log_sigmoid(ret_decay_fwd[l].astype(F32)),
                        jax.nn.log_sigmoid(ret_decay_bwd[l].astype(F32))])
        y_ret = _retention(p.reshape(B, lt, N_IN_PAD), lg, ret_gn_g[l].reshape(1, -1), ret_gn_b[l].reshape(1, -1),
                           seq, n_ctx)

        w_ukv = mla_w_ukv[l].reshape(MLA_KV_LORA, MLA_HEADS, MLA_NOPE + MLA_V)
        wk_p = _pad_heads(w_ukv[:, :, :MLA_NOPE].reshape(MLA_KV_LORA, -1), MLA_HEADS, MLA_NOPE).astype(BF16)
        wv = w_ukv[:, :, MLA_NOPE:].reshape(MLA_KV_LORA, MLA_HEADS * MLA_V).astype(BF16)
        wq_p = _pad_heads(mla_w_uq[l], MLA_HEADS, MLA_QK).astype(BF16)
        q, k, v = _mla_prep(p, cos_t, sin_t, mla_qa_g[l].reshape(1, -1), mla_kva_g[l].reshape(1, -1),
                            _pad_vec(mla_qn_g[l], 0), _pad_vec(mla_kn_g[l], 0), _pad_vec(mla_kr_g[l], MLA_NOPE),
                            wq_p, wk_p, wv, B, lt)
        o_mla = _attention(q, k, v, seq, n_ctx, lat_tiles_per_b if last else tiles_per_b)

        if last:
            n_tiles = B * lat_tiles_per_b
            tile = lambda t: (t // lat_tiles_per_b) * tiles_per_b + t % lat_tiles_per_b
        else:
            n_tiles = B * tiles_per_b
            tile = lambda t: t
        bs_full = jnp.broadcast_to(gmlp_bs[l][:, :, None], (GMLP_GROUPS, GMLP_CHUNK, GMLP_CHUNK))
        x1, h2 = _merge(xs, mod3, p, y_ret.reshape(B * lt, -1), o_mla.reshape(B * lt, -1),
                        gmlp_ln_g[l].reshape(1, -1), gmlp_ln_b[l].reshape(1, -1), gmlp_ws[l].astype(BF16), bs_full,
                        w_br_ret[l].astype(BF16), w_br_mla[l].astype(BF16), w_br_gmlp[l].astype(BF16),
                        w_out[l].astype(BF16), norm2_g[l].reshape(1, D), n_tiles, tile, mod_row)

        idx, w, rank, counts, shared, hp = _route(h2, moe_router[l], moe_bias[l].reshape(1, -1),
                                                  sh_w_gate[l].astype(BF16), sh_w_up[l].astype(BF16),
                                                  sh_w_down[l].astype(BF16), n_tiles, tile)
        n_act = n_tiles * TM
        n_blocks = -(-(n_act * TOP_K + N_EXPERTS * (MOE_ROWS - 1)) // MOE_ROWS)
        blk_e, valid, pos_tiles, pos_sc = _moe_plan(idx[:, :TOP_K], rank[:, :TOP_K], counts, n_blocks)
        xg = _dispatch(pos_sc, hp, n_blocks * MOE_ROWS)
        ys = _experts(blk_e, valid, xg, moe_w_gate[l], moe_w_up[l], moe_w_down[l], n_blocks)
        if last:
            xs = _combine(pos_tiles, ys, w, x1, shared, mod3, n_tiles, tile, mod_row, B * seq, lambda t: t)
        else:
            xs = _combine(pos_tiles, ys, w, x1, shared, mod3, n_tiles, tile, mod_row, B * lt, tile)
    return xs.reshape(B, seq, D)
```

```python
import functools
import math

import jax
import jax.numpy as jnp
from jax import lax
from jax.experimental import pallas as pl
from jax.experimental.pallas import tpu as pltpu
from jax.experimental.pallas import tpu_sc as plsc

F32 = jnp.float32
BF16 = jnp.bfloat16

D = 1024
GRID_W = 64
RET_HEADS = 4
RET_D = 128
RET_CHUNK = 128
RET_OUT_ROWS = 256
MLA_HEADS = 8
MLA_Q_LORA = 384
MLA_KV_LORA = 256
MLA_NOPE = 64
MLA_ROPE = 32
MLA_V = 64
MLA_V_EXT = MLA_V + 16
MLA_QK = MLA_NOPE + MLA_ROPE
HEAD_PAD = 128
ROPE_THETA = 10000.0
GMLP_GROUPS = 4
GMLP_W = 512
GMLP_CHUNK = 128
N_EXPERTS = 64
TOP_K = 6
D_EXPERT = 256
ROUTED_SCALE = 2.5
EPS = 1e-6
LOG2_E = 1.4426950408889634

TM = 256
MOE_ROWS = 256
ATT_KV_CHUNK = 512
ATT_UNROLL = 16

C_MERGE = 0
C_UV = 3072
C_RET = 4096
C_CKV = 6144
C_KR = 6400
C_CQ = 6528
N_IN_PAD = 6912
IN_CHUNK = 768

VMEM_LIMIT = 56 * 1024 * 1024

SC_CORES = 2
SC_SUBCORES = 16
SC_ROWS = 128


def _cparams(n_axes, vmem=VMEM_LIMIT):
    return pltpu.CompilerParams(dimension_semantics=("arbitrary",) * n_axes, vmem_limit_bytes=vmem)


def _silu(x):
    return x * jax.nn.sigmoid(x)


def _dot(a, b):
    return jnp.dot(a, b, preferred_element_type=F32)


def _dot_nt(a, b):
    return lax.dot_general(a, b, (((1,), (1,)), ((), ())), preferred_element_type=F32)


def _dot_tn(a, b):
    return lax.dot_general(a, b, (((0,), (0,)), ((), ())), preferred_element_type=F32)


def _pack_bf16_pairs(x):
    n = x.shape[1] // 2
    lo = lax.bitcast_convert_type(x[:, :n].astype(BF16).astype(F32), jnp.uint32)
    hi = lax.bitcast_convert_type(x[:, n:].astype(BF16).astype(F32), jnp.uint32)
    return (lo >> 16) | hi


def _unpack_bf16_pairs(u):
    lo = lax.bitcast_convert_type(u << 16, F32)
    hi = lax.bitcast_convert_type(u & jnp.uint32(0xFFFF0000), F32)
    return lo, hi


def _ada_kernel(c_ref, w_ref, b_ref, o_ref):
    s = _silu(c_ref[...])
    o_ref[...] = _dot(s.astype(BF16), w_ref[...].astype(BF16)) + b_ref[...]


def _ada(c_rows, ada_w, ada_b):
    depth = ada_w.shape[0]
    n = ada_w.shape[2]
    cw = 1536
    return pl.pallas_call(
        _ada_kernel,
        grid=(depth, n // cw),
        in_specs=[pl.BlockSpec((8, D), lambda l, j: (0, 0)),
                  pl.BlockSpec((None, D, cw), lambda l, j: (l, 0, j)),
                  pl.BlockSpec((None, 1, cw), lambda l, j: (l, 0, j))],
        out_specs=pl.BlockSpec((None, 8, cw), lambda l, j: (l, 0, j)),
        out_shape=jax.ShapeDtypeStruct((depth, 8, n), F32),
        compiler_params=_cparams(2),
        name="ada_mod",
    )(c_rows, ada_w, ada_b.reshape(depth, 1, n))


def _modulated_rmsnorm(x, g, shift, scale):
    y = x * lax.rsqrt(jnp.mean(x * x, axis=-1, keepdims=True) + EPS) * g
    return y * (1.0 + scale) + shift


def _in_proj_kernel(x_ref, mod_ref, g_ref, w_ref, o_ref, h_scr):
    h = _modulated_rmsnorm(x_ref[...], g_ref[...], mod_ref[:, 0:D], mod_ref[:, D:2 * D])
    h_scr[...] = h.astype(BF16)
    for c in range(N_IN_PAD // IN_CHUNK):
        cols = slice(c * IN_CHUNK, (c + 1) * IN_CHUNK)
        o_ref[:, cols] = _dot(h_scr[...], w_ref[:, cols]).astype(BF16)


def _in_proj(xs, mod3, g, w_in_r, n_tiles, mod_row):
    n_rows = xs.shape[0]
    return pl.pallas_call(
        _in_proj_kernel,
        grid=(n_tiles,),
        in_specs=[pl.BlockSpec((TM, D), lambda t: (t, 0)),
                  pl.BlockSpec((None, 1, 6 * D), lambda t: (mod_row(t), 0, 0)),
                  pl.BlockSpec((1, D), lambda t: (0, 0)),
                  pl.BlockSpec((D, N_IN_PAD), lambda t: (0, 0))],
        out_specs=pl.BlockSpec((TM, N_IN_PAD), lambda t: (t, 0)),
        out_shape=jax.ShapeDtypeStruct((n_rows, N_IN_PAD), BF16),
        scratch_shapes=[pltpu.VMEM((TM, D), BF16)],
        compiler_params=_cparams(1),
        name="in_proj",
    )(xs, mod3, g, w_in_r)


def _retention_kernel(lg_ref, q_ref, k_ref, v_ref, g_ref, gng_ref, gnb_ref, y_ref, of_scr, ob_scr,
                      *, n_lat_chunks, n_ctx_chunks):
    h = pl.program_id(1)
    lg_f = lg_ref[0, h]
    lg_b = lg_ref[1, h]
    C = RET_CHUNK
    k_scale = RET_D ** -0.5
    ri = lax.broadcasted_iota(jnp.int32, (C, C), 0).astype(F32)
    ci = lax.broadcasted_iota(jnp.int32, (C, C), 1).astype(F32)
    pos = lax.broadcasted_iota(jnp.int32, (C, 1), 0).astype(F32)
    diff = ri - ci
    d_f = jnp.where(diff >= 0, jnp.exp(lg_f * jnp.maximum(diff, 0.0)), 0.0) * k_scale
    d_b = jnp.where(diff < 0, jnp.exp(lg_b * jnp.maximum(-diff, 0.0)), 0.0) * k_scale
    qdec_f = jnp.exp(lg_f * (pos + 1.0))
    kdec_f = jnp.exp(lg_f * (C - 1.0 - pos)) * k_scale
    cdec_f = jnp.exp(lg_f * C)
    qdec_b = jnp.exp(lg_b * (C - pos))
    kdec_b = jnp.exp(lg_b * pos) * k_scale
    cdec_b = jnp.exp(lg_b * C)

    def chunk(c, state, dmat, qdec, kdec, cdec):
        rows = pl.ds(pl.multiple_of(c * C, C), C)
        q = q_ref[rows, :]
        k = k_ref[rows, :]
        v = v_ref[rows, :]
        att = (_dot_nt(q, k) * dmat).astype(BF16)
        o = _dot(att, v) + _dot((q.astype(F32) * qdec).astype(BF16), state.astype(BF16))
        kd = (k.astype(F32) * kdec).astype(BF16)
        return rows, o, state * cdec + _dot_tn(kd, v)

    n_all = n_lat_chunks + n_ctx_chunks

    def scan_body(i, states):
        s_f, s_b = states
        c_f = jnp.where(i < n_ctx_chunks, n_lat_chunks + i, i - n_ctx_chunks)
        rows, o, s_f = chunk(c_f, s_f, d_f, qdec_f, kdec_f, cdec_f)
        of_scr[rows, :] = o
        rows, o, s_b = chunk(n_all - 1 - i, s_b, d_b, qdec_b, kdec_b, cdec_b)
        ob_scr[rows, :] = o
        return s_f, s_b

    zero = jnp.zeros((RET_D, RET_D), F32)
    lax.fori_loop(0, n_all, scan_body, (zero, zero), unroll=2)

    def out_body(c, _):
        rows = pl.ds(pl.multiple_of(c * RET_OUT_ROWS, RET_OUT_ROWS), RET_OUT_ROWS)
        o = of_scr[rows, :] + ob_scr[rows, :]
        mu = jnp.mean(o, axis=-1, keepdims=True)
        var = jnp.mean(jnp.square(o - mu), axis=-1, keepdims=True)
        on = (o - mu) * lax.rsqrt(var + EPS)
        y = _silu(g_ref[rows, :].astype(F32)) * (on * gng_ref[...] + gnb_ref[...])
        y_ref[rows, :] = y.astype(BF16)
        return 0

    lax.fori_loop(0, n_all * C // RET_OUT_ROWS, out_body, 0)


def _retention(p3, lg, gn_g, gn_b, seq, ctx):
    B, lt, _ = p3.shape
    base = C_RET // RET_D
    kern = functools.partial(_retention_kernel, n_lat_chunks=seq // RET_CHUNK, n_ctx_chunks=ctx // RET_CHUNK)

    def col(off):
        return pl.BlockSpec((None, lt, RET_D), lambda b, h: (b, 0, base + off * RET_HEADS + h))

    return pl.pallas_call(
        kern,
        grid=(B, RET_HEADS),
        in_specs=[pl.BlockSpec(memory_space=pltpu.SMEM),
                  col(0), col(1), col(2), col(3),
                  pl.BlockSpec((1, RET_D), lambda b, h: (0, h)),
                  pl.BlockSpec((1, RET_D), lambda b, h: (0, h))],
        out_specs=pl.BlockSpec((None, lt, RET_D), lambda b, h: (b, 0, h)),
        out_shape=jax.ShapeDtypeStruct((B, lt, RET_HEADS * RET_D), BF16),
        scratch_shapes=[pltpu.VMEM((lt, RET_D), F32), pltpu.VMEM((lt, RET_D), F32)],
        compiler_params=_cparams(2),
        name="retention",
    )(lg, p3, p3, p3, p3, gn_g, gn_b)


def _rope_rotate(x, first_half):
    return jnp.where(first_half, pltpu.roll(x, HEAD_PAD - 8, 1), pltpu.roll(x, 8, 1))


def _mla_prep_kernel(cq_ref, ckv_ref, kr_ref, cos_ref, sin_ref, qa_ref, kva_ref, qn_ref, kn_ref, krg_ref,
                     wq_ref, wk_ref, wv_ref, q_ref, k_ref, v_ref):
    lane = lax.broadcasted_iota(jnp.int32, (1, HEAD_PAD), 1)
    first_half = (lane % 16) < 8
    cos = cos_ref[...]
    sin = sin_ref[...]

    def rms(x, n):
        return x * lax.rsqrt(jnp.sum(x * x, axis=-1, keepdims=True) * (1.0 / n) + EPS)

    def rope(x):
        return x * cos + _rope_rotate(x, first_half) * sin

    cq = cq_ref[...].astype(F32)
    cqn = (rms(cq, MLA_Q_LORA) * qa_ref[...]).astype(BF16)
    q_all = _dot(cqn, wq_ref[...])
    ckv = ckv_ref[...].astype(F32)
    ckvn = (rms(ckv, MLA_KV_LORA) * kva_ref[...]).astype(BF16)
    k_all = _dot(ckvn, wk_ref[...])
    v_all = _dot(ckvn, wv_ref[...])
    k_rope = rope(rms(kr_ref[...].astype(F32), MLA_ROPE) * krg_ref[...])
    scale = MLA_QK ** -0.5 * LOG2_E
    v_t = v_all.T
    ones_row = jnp.where(lax.broadcasted_iota(jnp.int32, (MLA_V_EXT - MLA_V, TM), 0) == 0, 1.0, 0.0)
    for h in range(MLA_HEADS):
        cols = slice(h * HEAD_PAD, (h + 1) * HEAD_PAD)
        qh = rope(rms(q_all[:, cols], MLA_QK) * qn_ref[...]) * scale
        q_ref[h] = qh.astype(BF16)
        kh = rms(k_all[:, cols], MLA_NOPE) * kn_ref[...] + k_rope
        k_ref[h] = kh.astype(BF16)
        v_ref[h] = jnp.concatenate([v_t[h * MLA_V:(h + 1) * MLA_V, :], ones_row], axis=0).astype(BF16)


def _mla_prep(p, cos_t, sin_t, qa_g, kva_g, qn_p, kn_p, kr_p, wq_p, wk_p, wv, B, lt):
    tiles_per_b = lt // TM
    hw = MLA_HEADS * HEAD_PAD
    const = lambda shape: pl.BlockSpec(shape, lambda b, j: (0,) * len(shape))
    head_out = pl.BlockSpec((None, MLA_HEADS, TM, HEAD_PAD), lambda b, j: (b, 0, j, 0))
    shp = jax.ShapeDtypeStruct((B, MLA_HEADS, lt, HEAD_PAD), BF16)
    v_out = pl.BlockSpec((None, MLA_HEADS, None, MLA_V_EXT, TM), lambda b, j: (b, 0, j, 0, 0))
    v_shp = jax.ShapeDtypeStruct((B, MLA_HEADS, tiles_per_b, MLA_V_EXT, TM), BF16)
    return pl.pallas_call(
        _mla_prep_kernel,
        grid=(B, tiles_per_b),
        in_specs=[pl.BlockSpec((TM, MLA_Q_LORA), lambda b, j: (b * tiles_per_b + j, C_CQ // MLA_Q_LORA)),
                  pl.BlockSpec((TM, MLA_KV_LORA), lambda b, j: (b * tiles_per_b + j, C_CKV // MLA_KV_LORA)),
                  pl.BlockSpec((TM, HEAD_PAD), lambda b, j: (b * tiles_per_b + j, C_KR // HEAD_PAD)),
                  pl.BlockSpec((TM, HEAD_PAD), lambda b, j: (j, 0)),
                  pl.BlockSpec((TM, HEAD_PAD), lambda b, j: (j, 0)),
                  const((1, MLA_Q_LORA)), const((1, MLA_KV_LORA)),
                  const((1, HEAD_PAD)), const((1, HEAD_PAD)), const((1, HEAD_PAD)),
                  const((MLA_Q_LORA, hw)), const((MLA_KV_LORA, hw)), const((MLA_KV_LORA, MLA_HEADS * MLA_V))],
        out_specs=[head_out, head_out, v_out],
        out_shape=[shp, shp, v_shp],
        compiler_params=_cparams(2),
        name="mla_prep",
    )(p, p, p, cos_t, sin_t, qa_g, kva_g, qn_p, kn_p, kr_p, wq_p, wk_p, wv)


def _attention_kernel(q_ref, k_ref, v_ref, o_ref, s_scr, *, seq, ctx, ctx_tile):
    i = pl.program_id(2)
    n_blk = ATT_KV_CHUNK // TM
    n_chunks = seq // ATT_KV_CHUNK
    unroll = math.gcd(n_chunks, ATT_UNROLL)

    def scores(hh, slot, blk, nb):
        start = blk * TM if isinstance(blk, int) else pl.multiple_of(blk * TM, TM)
        s_scr[hh, slot, 0:nb * TM, :] = _dot_nt(k_ref[hh, pl.ds(start, nb * TM), :], q_ref[hh])

    def absorb(hh, slot, blk, nb, carry):
        m, acc = carry
        s = s_scr[hh, slot, 0:nb * TM, :]
        m_new = jnp.maximum(m, jnp.max(s, axis=0, keepdims=True))
        p = jnp.exp2(s - m_new).astype(BF16)
        acc = jnp.exp2(m - m_new) * acc
        for j in range(nb):
            acc = acc + _dot(v_ref[hh, blk + j], p[j * TM:(j + 1) * TM, :])
        return m_new, acc

    def init():
        return (jnp.full((1, TM), -jnp.inf, F32), jnp.zeros((MLA_V_EXT, TM), F32))

    def write(carries):
        outs = [acc[0:MLA_V, :] / acc[MLA_V:MLA_V + 1, :] for _, acc in carries]
        o_ref[...] = jnp.concatenate(outs, axis=0).T.astype(BF16)

    def step(carries, slot, blk, nb, next_blk, next_nb):
        out = []
        for hh in range(2):
            if next_blk is not None:
                scores(hh, 1 - slot, next_blk, next_nb)
            out.append(absorb(hh, slot, blk, nb, carries[hh]))
        return tuple(out)

    ctx_blk = seq // TM
    ctx_nb = ctx // TM

    @pl.when(i != ctx_tile)
    def _():
        for hh in range(2):
            scores(hh, 0, ctx_blk, ctx_nb)
        carries = step((init(), init()), 0, ctx_blk, ctx_nb, 0, n_blk)
        last_blk = (n_chunks - 1) * n_blk

        def body(c, carries):
            for u in range(unroll):
                blk = (c * unroll + u) * n_blk
                carries = step(carries, (1 + u) % 2, blk, n_blk, jnp.minimum(blk + n_blk, last_blk), n_blk)
            return carries

        if n_chunks == unroll:
            for u in range(n_chunks):
                nxt = (u + 1) * n_blk if u + 1 < n_chunks else None
                carries = step(carries, (1 + u) % 2, u * n_blk, n_blk, nxt, n_blk)
            write(carries)
        else:
            write(lax.fori_loop(0, n_chunks // unroll, body, carries))

    @pl.when(i == ctx_tile)
    def _():
        for hh in range(2):
            scores(hh, 0, ctx_blk, ctx_nb)
        write(step((init(), init()), 0, ctx_blk, ctx_nb, None, None))


def _attention(q, k, v, seq, ctx, n_q_tiles):
    B, H, lt, _ = q.shape
    kern = functools.partial(_attention_kernel, seq=seq, ctx=ctx, ctx_tile=seq // TM)
    return pl.pallas_call(
        kern,
        grid=(B, H // 2, n_q_tiles),
        in_specs=[pl.BlockSpec((None, 2, TM, HEAD_PAD), lambda b, h, i: (b, h, i, 0)),
                  pl.BlockSpec((None, 2, lt, HEAD_PAD), lambda b, h, i: (b, h, 0, 0)),
                  pl.BlockSpec((None, 2, lt // TM, MLA_V_EXT, TM), lambda b, h, i: (b, h, 0, 0, 0))],
        out_specs=pl.BlockSpec((None, TM, HEAD_PAD), lambda b, h, i: (b, i, h)),
        out_shape=jax.ShapeDtypeStruct((B, lt, (H // 2) * HEAD_PAD), BF16),
        scratch_shapes=[pltpu.VMEM((2, 2, ATT_KV_CHUNK, TM), F32)],
        compiler_params=_cparams(3),
        name="attention",
    )(q, k, v)


def _merge_kernel(x_ref, mod_ref, mg_ref, uv_ref, yr_ref, om_ref, lng_ref, lnb_ref, ws_ref, bs_ref,
                  wr_ref, wm_ref, wg_ref, wo_ref, n2_ref, x1_ref, h2_ref):
    yr = _dot(yr_ref[...], wr_ref[...])
    ym = _dot(om_ref[...], wm_ref[...])
    z = jax.nn.gelu(uv_ref[...].astype(F32))
    u = z[:, :GMLP_W]
    v = z[:, GMLP_W:]
    mu = jnp.mean(v, axis=-1, keepdims=True)
    var = jnp.mean(jnp.square(v - mu), axis=-1, keepdims=True)
    vn = ((v - mu) * lax.rsqrt(var + EPS) * lng_ref[...] + lnb_ref[...]).astype(BF16)
    gw = GMLP_W // GMLP_GROUPS
    chunks = []
    for c in range(TM // GMLP_CHUNK):
        rows = slice(c * GMLP_CHUNK, (c + 1) * GMLP_CHUNK)
        groups = [_dot(ws_ref[g], vn[rows, g * gw:(g + 1) * gw]) + bs_ref[g] for g in range(GMLP_GROUPS)]
        chunks.append(jnp.concatenate(groups, axis=1))
    sv = jnp.concatenate(chunks, axis=0)
    yg = _dot((u * sv).astype(BF16), wg_ref[...])
    gate = jax.nn.sigmoid(mg_ref[...].astype(F32))
    y = gate[:, :D] * yr + gate[:, D:2 * D] * ym + gate[:, 2 * D:] * yg
    out = _dot(y.astype(BF16), wo_ref[...])
    x1 = x_ref[...] + mod_ref[:, 2 * D:3 * D] * out
    x1_ref[...] = x1
    h2_ref[...] = _modulated_rmsnorm(x1, n2_ref[...], mod_ref[:, 3 * D:4 * D], mod_ref[:, 4 * D:5 * D])


def _merge(xs, mod3, p, y_ret, o_mla, ln_g, ln_b, ws, bs_full, w_br_ret, w_br_mla, w_br_gmlp, w_out, n2_g,
           n_tiles, tile, mod_row):
    n_rows = xs.shape[0]
    const = lambda shape: pl.BlockSpec(shape, lambda t: (0,) * len(shape))
    row = lambda w, cb=0: pl.BlockSpec((TM, w), lambda t: (tile(t), cb))
    shp = jax.ShapeDtypeStruct((n_rows, D), F32)
    return pl.pallas_call(
        _merge_kernel,
        grid=(n_tiles,),
        in_specs=[row(D),
                  pl.BlockSpec((None, 1, 6 * D), lambda t: (mod_row(tile(t)), 0, 0)),
                  row(3 * D, C_MERGE // (3 * D)), row(D, C_UV // D),
                  row(RET_HEADS * RET_D), row(MLA_HEADS * MLA_V),
                  const((1, GMLP_W)), const((1, GMLP_W)),
                  const((GMLP_GROUPS, GMLP_CHUNK, GMLP_CHUNK)), const((GMLP_GROUPS, GMLP_CHUNK, GMLP_CHUNK)),
                  const((RET_HEADS * RET_D, D)), const((MLA_HEADS * MLA_V, D)), const((GMLP_W, D)),
                  const((D, D)), const((1, D))],
        out_specs=[row(D), row(D)],
        out_shape=[shp, shp],
        compiler_params=_cparams(1),
        name="merge",
    )(xs, mod3, p, p, y_ret, o_mla, ln_g, ln_b, ws, bs_full, w_br_ret, w_br_mla, w_br_gmlp, w_out, n2_g)


def _route_kernel(h_ref, r_ref, b_ref, sg_ref, su_ref, sd_ref, idx_ref, w_ref, rank_ref, cnt_ref, sh_ref, hp_ref,
                  cnt_scr):
    @pl.when(pl.program_id(0) == 0)
    def _():
        cnt_scr[...] = jnp.zeros_like(cnt_scr)

    h = h_ref[...]
    logits = jnp.dot(h, r_ref[...], preferred_element_type=F32, precision=lax.Precision.HIGHEST)
    scores = jax.nn.sigmoid(logits)
    sel = scores + b_ref[...]
    lane_e = lax.broadcasted_iota(jnp.int32, (TM, N_EXPERTS), 1).astype(F32)
    lane_o = lax.broadcasted_iota(jnp.int32, (TM, 128), 1)
    idx_out = jnp.zeros((TM, 128), F32)
    w_out = jnp.zeros((TM, 128), F32)
    hits = []
    for k in range(TOP_K):
        best = jnp.max(sel, axis=-1, keepdims=True)
        pick = jnp.min(jnp.where(sel == best, lane_e, float(N_EXPERTS)), axis=-1, keepdims=True)
        hit = lane_e == pick
        hits.append(hit)
        wk = jnp.sum(jnp.where(hit, scores, 0.0), axis=-1, keepdims=True)
        sel = jnp.where(hit, -jnp.inf, sel)
        idx_out = jnp.where(lane_o == k, pick, idx_out)
        w_out = jnp.where(lane_o == k, wk, w_out)
    w_out = w_out / jnp.sum(w_out, axis=-1, keepdims=True) * ROUTED_SCALE
    idx_ref[...] = idx_out.astype(jnp.int32)
    w_ref[...] = w_out
    chosen = jnp.zeros((TM, N_EXPERTS), F32)
    for hit in hits:
        chosen = jnp.where(hit, 1.0, chosen)
    earlier = (lax.broadcasted_iota(jnp.int32, (TM, TM), 0) > lax.broadcasted_iota(jnp.int32, (TM, TM), 1))
    before = _dot(jnp.where(earlier, 1.0, 0.0).astype(BF16), chosen.astype(BF16)) + cnt_scr[...]
    rank_out = jnp.zeros((TM, 128), F32)
    for k, hit in enumerate(hits):
        rank_out = jnp.where(lane_o == k, jnp.sum(jnp.where(hit, before, 0.0), axis=-1, keepdims=True), rank_out)
    rank_ref[...] = rank_out.astype(jnp.int32)
    cnt_scr[...] += jnp.sum(chosen, axis=0, keepdims=True)
    cnt_ref[...] = cnt_scr[...]
    hb = h.astype(BF16)
    a = _silu(_dot(hb, sg_ref[...])) * _dot(hb, su_ref[...])
    sh_ref[...] = _dot(a.astype(BF16), sd_ref[...])
    hp_ref[...] = _pack_bf16_pairs(h)


def _route(h2, router, bias, sg, su, sd, n_tiles, tile):
    const = lambda shape: pl.BlockSpec(shape, lambda t: (0,) * len(shape))
    n_act = n_tiles * TM
    return pl.pallas_call(
        _route_kernel,
        grid=(n_tiles,),
        in_specs=[pl.BlockSpec((TM, D), lambda t: (tile(t), 0)),
                  const((D, N_EXPERTS)), const((1, N_EXPERTS)),
                  const((D, D_EXPERT)), const((D, D_EXPERT)), const((D_EXPERT, D))],
        out_specs=[pl.BlockSpec((TM, 128), lambda t: (t, 0)),
                   pl.BlockSpec((TM, 128), lambda t: (t, 0)),
                   pl.BlockSpec((TM, 128), lambda t: (t, 0)),
                   pl.BlockSpec((1, N_EXPERTS), lambda t: (0, 0)),
                   pl.BlockSpec((TM, D), lambda t: (t, 0)),
                   pl.BlockSpec((TM, D // 2), lambda t: (t, 0))],
        out_shape=[jax.ShapeDtypeStruct((n_act, 128), jnp.int32),
                   jax.ShapeDtypeStruct((n_act, 128), F32),
                   jax.ShapeDtypeStruct((n_act, 128), jnp.int32),
                   jax.ShapeDtypeStruct((1, N_EXPERTS), F32),
                   jax.ShapeDtypeStruct((n_act, D), F32),
                   jax.ShapeDtypeStruct((n_act, D // 2), jnp.uint32)],
        scratch_shapes=[pltpu.VMEM((1, N_EXPERTS), F32)],
        compiler_params=_cparams(1),
        name="route_shared",
    )(h2, router, bias, sg, su, sd)


def _dispatch(pos_sc, hp, n_rows):
    n_batches = pos_sc.shape[0]
    n_workers = SC_CORES * SC_SUBCORES
    mesh = plsc.VectorSubcoreMesh(core_axis_name="c", subcore_axis_name="s")

    @functools.partial(
        pl.kernel, mesh=mesh,
        out_type=jax.ShapeDtypeStruct((n_rows, D // 2), jnp.uint32),
        scratch_types=[pltpu.VMEM((TOP_K, SC_ROWS), jnp.int32),
                       pltpu.VMEM((SC_ROWS, D // 2), jnp.uint32),
                       pltpu.SemaphoreType.DMA],
        name="moe_dispatch")
    def scatter(pos_hbm, h_hbm, xs_hbm, idx_v, rows_v, sem):
        worker = lax.axis_index("s") * SC_CORES + lax.axis_index("c")

        @pl.loop(0, pl.cdiv(n_batches, n_workers))
        def _(j):
            b = j * n_workers + worker

            @pl.when(b < n_batches)
            def _():
                pltpu.sync_copy(pos_hbm.at[b], idx_v)
                pltpu.sync_copy(h_hbm.at[pl.ds(b * SC_ROWS, SC_ROWS)], rows_v)
                copies = [pltpu.async_copy(rows_v, xs_hbm.at[idx_v.at[k]], sem) for k in range(TOP_K)]
                for cp in copies:
                    cp.wait()

    return scatter(pos_sc, hp)


def _expert_kernel(blk_e_ref, valid_ref, x_ref, wg_ref, wu_ref, wd_ref, y_ref):
    n_valid = valid_ref[pl.program_id(0)]

    @pl.when(n_valid > 0)
    def _():
        row = lax.broadcasted_iota(jnp.int32, (MOE_ROWS, 1), 0)
        lo, hi = _unpack_bf16_pairs(jnp.where(row < n_valid, x_ref[...], jnp.uint32(0)))
        x = jnp.concatenate([lo, hi], axis=1).astype(BF16)
        hb = _silu(_dot(x, wg_ref[...].astype(BF16))) * _dot(x, wu_ref[...].astype(BF16))
        y_ref[...] = _pack_bf16_pairs(_dot(hb.astype(BF16), wd_ref[...].astype(BF16)))

    @pl.when(n_valid == 0)
    def _():
        y_ref[...] = jnp.zeros_like(y_ref)


def _experts(blk_e, valid, xs, wg, wu, wd, n_blocks):
    grid_spec = pltpu.PrefetchScalarGridSpec(
        num_scalar_prefetch=2,
        grid=(n_blocks,),
        in_specs=[pl.BlockSpec((MOE_ROWS, D // 2), lambda i, be, nv: (i, 0)),
                  pl.BlockSpec((None, D, D_EXPERT), lambda i, be, nv: (be[i], 0, 0)),
                  pl.BlockSpec((None, D, D_EXPERT), lambda i, be, nv: (be[i], 0, 0)),
                  pl.BlockSpec((None, D_EXPERT, D), lambda i, be, nv: (be[i], 0, 0))],
        out_specs=pl.BlockSpec((MOE_ROWS, D // 2), lambda i, be, nv: (i, 0)),
    )
    return pl.pallas_call(
        _expert_kernel,
        grid_spec=grid_spec,
        out_shape=jax.ShapeDtypeStruct((n_blocks * MOE_ROWS, D // 2), jnp.uint32),
        compiler_params=_cparams(1),
        name="routed_experts",
    )(blk_e, valid, xs, wg, wu, wd)


def _gather_rows(pos_sc, ys):
    n_batches = pos_sc.shape[0]
    n_workers = SC_CORES * SC_SUBCORES
    half = SC_ROWS // 2
    mesh = plsc.VectorSubcoreMesh(core_axis_name="c", subcore_axis_name="s")

    @functools.partial(
        pl.kernel, mesh=mesh,
        out_type=jax.ShapeDtypeStruct((TOP_K, n_batches * SC_ROWS, D // 2), jnp.uint32),
        scratch_types=[pltpu.VMEM((TOP_K, SC_ROWS), jnp.int32),
                       pltpu.VMEM((2, half, D // 2), jnp.uint32),
                       pltpu.SemaphoreType.DMA, pltpu.SemaphoreType.DMA],
        name="moe_gather")
    def gather(pos_hbm, y_hbm, out_hbm, idx_v, bufs, gsem, wsem):
        worker = lax.axis_index("s") * SC_CORES + lax.axis_index("c")

        @pl.loop(0, pl.cdiv(n_batches, n_workers))
        def _(j):
            b = j * n_workers + worker

            @pl.when(b < n_batches)
            def _():
                pltpu.sync_copy(pos_hbm.at[b], idx_v)
                items = [(k, h) for k in range(TOP_K) for h in range(2)]

                def fetch(i):
                    k, h = items[i]
                    return pltpu.async_copy(y_hbm.at[idx_v.at[k, pl.ds(h * half, half)]], bufs.at[i % 2], gsem)

                pending_gather = fetch(0)
                pending_write = None
                for i, (k, h) in enumerate(items):
                    pending_gather.wait()
                    if pending_write is not None:
                        pending_write.wait()
                    if i + 1 < len(items):
                        pending_gather = fetch(i + 1)
                    pending_write = pltpu.async_copy(
                        bufs.at[i % 2], out_hbm.at[k, pl.ds(b * SC_ROWS + h * half, half)], wsem)
                pending_write.wait()

    return gather(pos_sc, ys)


def _combine_kernel(*refs):
    y_refs = refs[:TOP_K]
    w_ref, x1_ref, sh_ref, mod_ref, o_ref = refs[TOP_K:]
    f_lo = sh_ref[:, :D // 2]
    f_hi = sh_ref[:, D // 2:]
    for k in range(TOP_K):
        lo, hi = _unpack_bf16_pairs(y_refs[k][...])
        f_lo = f_lo + lo * w_ref[:, k:k + 1]
        f_hi = f_hi + hi * w_ref[:, k:k + 1]
    o_ref[...] = x1_ref[...] + mod_ref[:, 5 * D:6 * D] * jnp.concatenate([f_lo, f_hi], axis=1)


def _combine(yg, w, x1, shared, mod3, n_tiles, tile, mod_row, out_rows, out_tile):
    planes = [pl.BlockSpec((None, TM, D // 2), lambda t, k=k: (k, t, 0)) for k in range(TOP_K)]
    return pl.pallas_call(
        _combine_kernel,
        grid=(n_tiles,),
        in_specs=planes + [pl.BlockSpec((TM, 128), lambda t: (t, 0)),
                           pl.BlockSpec((TM, D), lambda t: (tile(t), 0)),
                           pl.BlockSpec((TM, D), lambda t: (t, 0)),
                           pl.BlockSpec((None, 1, 6 * D), lambda t: (mod_row(tile(t)), 0, 0))],
        out_specs=pl.BlockSpec((TM, D), lambda t: (out_tile(t), 0)),
        out_shape=jax.ShapeDtypeStruct((out_rows, D), F32),
        compiler_params=_cparams(1),
        name="moe_combine",
    )(*([yg] * TOP_K), w, x1, shared, mod3)


def _moe_plan(idx, rank, counts, n_blocks):
    n = idx.shape[0]
    cnt = counts.reshape(N_EXPERTS).astype(jnp.int32)
    padded = (cnt + MOE_ROWS - 1) // MOE_ROWS * MOE_ROWS
    pad_end = jnp.cumsum(padded)
    pad_start = pad_end - padded
    experts = jnp.arange(N_EXPERTS, dtype=jnp.int32)
    pos = rank + jnp.sum(jnp.where(idx[:, :, None] == experts, pad_start, 0), axis=-1)
    blk_start = jnp.arange(n_blocks, dtype=jnp.int32) * MOE_ROWS
    blk_e = jnp.minimum(jnp.sum(blk_start[:, None] >= pad_end[None, :], axis=1), N_EXPERTS - 1).astype(jnp.int32)
    mine = blk_e[:, None] == experts
    in_expert = blk_start - jnp.sum(jnp.where(mine, pad_start, 0), axis=1)
    valid = jnp.clip(jnp.sum(jnp.where(mine, cnt, 0), axis=1) - in_expert, 0, MOE_ROWS).astype(jnp.int32)
    pos_sc = pos.astype(jnp.int32).reshape(n // SC_ROWS, SC_ROWS, TOP_K).transpose(0, 2, 1)
    return blk_e, valid, pos_sc


def _rope_tables(seq, ctx):
    half = MLA_ROPE // 2
    n_freq = half // 2
    inv = ROPE_THETA ** (-2.0 * jnp.arange(n_freq, dtype=F32) / half)
    t = jnp.arange(seq)
    ang_r = (t // GRID_W).astype(F32)[:, None] * inv
    ang_c = (t % GRID_W).astype(F32)[:, None] * inv
    cos = jnp.concatenate([jnp.cos(ang_r), jnp.cos(ang_r), jnp.cos(ang_c), jnp.cos(ang_c)], axis=1)
    sin = jnp.concatenate([-jnp.sin(ang_r), jnp.sin(ang_r), -jnp.sin(ang_c), jnp.sin(ang_c)], axis=1)
    pad_l = MLA_NOPE
    pad_r = HEAD_PAD - MLA_NOPE - MLA_ROPE
    cos = jnp.pad(cos, ((0, ctx), (pad_l, pad_r)), constant_values=1.0)
    cos = cos.at[seq:, :].set(1.0)
    sin = jnp.pad(sin, ((0, ctx), (pad_l, pad_r)))
    return cos, sin


def _pad_heads(w, n_heads, width, offset=0):
    k = w.shape[0]
    w = w.reshape(k, n_heads, width)
    w = jnp.pad(w, ((0, 0), (0, 0), (offset, HEAD_PAD - width - offset)))
    return w.reshape(k, n_heads * HEAD_PAD)


def _pad_vec(g, offset):
    return jnp.pad(g, (offset, HEAD_PAD - g.shape[0] - offset)).reshape(1, HEAD_PAD)


def _reorder_w_in(w):
    off_cq, off_ckv, off_kr, off_uv, off_merge = 2048, 2432, 2688, 2720, 3744
    kr = jnp.pad(w[:, off_kr:off_uv], ((0, 0), (MLA_NOPE, HEAD_PAD - MLA_NOPE - MLA_ROPE)))
    return jnp.concatenate([w[:, off_merge:], w[:, off_uv:off_merge], w[:, :off_cq],
                            w[:, off_ckv:off_kr], kr, w[:, off_cq:off_ckv]], axis=1).astype(BF16)


def kernel(x, c, ctx, c_ctx, ada_w, ada_b, norm1_g, norm2_g, w_in, ret_decay_fwd, ret_decay_bwd, ret_gn_g,
           ret_gn_b, w_br_ret, mla_qa_g, mla_w_uq, mla_kva_g, mla_w_ukv, mla_qn_g, mla_kn_g, mla_kr_g, w_br_mla,
           gmlp_ln_g, gmlp_ln_b, gmlp_ws, gmlp_bs, w_br_gmlp, w_out, moe_router, moe_bias, moe_w_gate, moe_w_up,
           moe_w_down, sh_w_gate, sh_w_up, sh_w_down):
    B, seq, _ = x.shape
    n_ctx = ctx.shape[1]
    depth = ada_w.shape[0]
    assert n_ctx == TM and seq % (2 * ATT_KV_CHUNK) == 0 and seq % TM == 0
    lt = seq + n_ctx
    tiles_per_b = lt // TM
    lat_tiles_per_b = seq // TM
    ctx_tile = lat_tiles_per_b

    def mod_row(t):
        return jnp.where(t % tiles_per_b == ctx_tile, B, t // tiles_per_b)

    c_rows = jnp.concatenate([c, c_ctx[None, :], jnp.zeros((8 - B - 1, D), F32)], axis=0)
    mod = _ada(c_rows, ada_w, ada_b)
    cos_t, sin_t = _rope_tables(seq, n_ctx)
    xs = jnp.concatenate([x, ctx], axis=1).reshape(B * lt, D)

    for l in range(depth):
        last = l == depth - 1
        mod3 = mod[l].reshape(8, 1, 6 * D)
        p = _in_proj(xs, mod3, norm1_g[l].reshape(1, D), _reorder_w_in(w_in[l]), B * tiles_per_b, mod_row)

        lg = jnp.stack([jax.nn.log_sigmoid(ret_decay_fwd[l].astype(F32)),
                        jax.nn.log_sigmoid(ret_decay_bwd[l].astype(F32))])
        y_ret = _retention(p.reshape(B, lt, N_IN_PAD), lg, ret_gn_g[l].reshape(1, -1), ret_gn_b[l].reshape(1, -1),
                           seq, n_ctx)

        w_ukv = mla_w_ukv[l].reshape(MLA_KV_LORA, MLA_HEADS, MLA_NOPE + MLA_V)
        wk_p = _pad_heads(w_ukv[:, :, :MLA_NOPE].reshape(MLA_KV_LORA, -1), MLA_HEADS, MLA_NOPE).astype(BF16)
        wv = w_ukv[:, :, MLA_NOPE:].reshape(MLA_KV_LORA, MLA_HEADS * MLA_V).astype(BF16)
        wq_p = _pad_heads(mla_w_uq[l], MLA_HEADS, MLA_QK).astype(BF16)
        q, k, v = _mla_prep(p, cos_t, sin_t, mla_qa_g[l].reshape(1, -1), mla_kva_g[l].reshape(1, -1),
                            _pad_vec(mla_qn_g[l], 0), _pad_vec(mla_kn_g[l], 0), _pad_vec(mla_kr_g[l], MLA_NOPE),
                            wq_p, wk_p, wv, B, lt)
        o_mla = _attention(q, k, v, seq, n_ctx, lat_tiles_per_b if last else tiles_per_b)

        if last:
            n_tiles = B * lat_tiles_per_b
            tile = lambda t: (t // lat_tiles_per_b) * tiles_per_b + t % lat_tiles_per_b
        else:
            n_tiles = B * tiles_per_b
            tile = lambda t: t
        bs_full = jnp.broadcast_to(gmlp_bs[l][:, :, None], (GMLP_GROUPS, GMLP_CHUNK, GMLP_CHUNK))
        x1, h2 = _merge(xs, mod3, p, y_ret.reshape(B * lt, -1), o_mla.reshape(B * lt, -1),
                        gmlp_ln_g[l].reshape(1, -1), gmlp_ln_b[l].reshape(1, -1), gmlp_ws[l].astype(BF16), bs_full,
                        w_br_ret[l].astype(BF16), w_br_mla[l].astype(BF16), w_br_gmlp[l].astype(BF16),
                        w_out[l].astype(BF16), norm2_g[l].reshape(1, D), n_tiles, tile, mod_row)

        idx, w, rank, counts, shared, hp = _route(h2, moe_router[l], moe_bias[l].reshape(1, -1),
                                                  sh_w_gate[l].astype(BF16), sh_w_up[l].astype(BF16),
                                                  sh_w_down[l].astype(BF16), n_tiles, tile)
        n_act = n_tiles * TM
        n_blocks = -(-(n_act * TOP_K + N_EXPERTS * (MOE_ROWS - 1)) // MOE_ROWS)
        blk_e, valid, pos_sc = _moe_plan(idx[:, :TOP_K], rank[:, :TOP_K], counts, n_blocks)
        xg = _dispatch(pos_sc, hp, n_blocks * MOE_ROWS)
        ys = _experts(blk_e, valid, xg, moe_w_gate[l], moe_w_up[l], moe_w_down[l], n_blocks)
        yg = _gather_rows(pos_sc, ys)
        if last:
            xs = _combine(yg, w, x1, shared, mod3, n_tiles, tile, mod_row, B * seq, lambda t: t)
        else:
            xs = _combine(yg, w, x1, shared, mod3, n_tiles, tile, mod_row, B * lt, tile)
    return xs.reshape(B, seq, D)
```

```python
import functools
import math

import jax
import jax.numpy as jnp
from jax import lax
from jax.experimental import pallas as pl
from jax.experimental.pallas import tpu as pltpu
from jax.experimental.pallas import tpu_sc as plsc

F32 = jnp.float32
BF16 = jnp.bfloat16

D = 1024
GRID_W = 64
RET_HEADS = 4
RET_D = 128
RET_CHUNK = 128
RET_OUT_ROWS = 256
MLA_HEADS = 8
MLA_Q_LORA = 384
MLA_KV_LORA = 256
MLA_NOPE = 64
MLA_ROPE = 32
MLA_V = 64
MLA_V_EXT = MLA_V + 16
MLA_QK = MLA_NOPE + MLA_ROPE
HEAD_PAD = 128
ROPE_THETA = 10000.0
GMLP_GROUPS = 4
GMLP_W = 512
GMLP_CHUNK = 128
N_EXPERTS = 64
TOP_K = 6
D_EXPERT = 256
ROUTED_SCALE = 2.5
EPS = 1e-6
LOG2_E = 1.4426950408889634

TM = 256
MOE_ROWS = 256
RT = 2 * TM
ATT_KV_CHUNK = 1024
ATT_UNROLL = 16

C_MERGE = 0
C_UV = 3072
C_RET = 4096
C_CKV = 6144
C_KR = 6400
C_CQ = 6528
N_IN_PAD = 6912
IN_CHUNK = 768

VMEM_LIMIT = 56 * 1024 * 1024

SC_CORES = 2
SC_SUBCORES = 16
SC_ROWS = 128


def _cparams(n_axes, vmem=VMEM_LIMIT):
    return pltpu.CompilerParams(dimension_semantics=("arbitrary",) * n_axes, vmem_limit_bytes=vmem)


def _silu(x):
    return x * jax.nn.sigmoid(x)


def _dot(a, b):
    return jnp.dot(a, b, preferred_element_type=F32)


def _dot_nt(a, b):
    return lax.dot_general(a, b, (((1,), (1,)), ((), ())), preferred_element_type=F32)


def _dot_tn(a, b):
    return lax.dot_general(a, b, (((0,), (0,)), ((), ())), preferred_element_type=F32)


def _pack_bf16_pairs(x):
    n = x.shape[1] // 2
    lo = lax.bitcast_convert_type(x[:, :n].astype(BF16).astype(F32), jnp.uint32)
    hi = lax.bitcast_convert_type(x[:, n:].astype(BF16).astype(F32), jnp.uint32)
    return (lo >> 16) | hi


def _unpack_bf16_pairs(u):
    lo = lax.bitcast_convert_type(u << 16, F32)
    hi = lax.bitcast_convert_type(u & jnp.uint32(0xFFFF0000), F32)
    return lo, hi


def _ada_kernel(c_ref, w_ref, b_ref, o_ref):
    s = _silu(c_ref[...])
    o_ref[...] = _dot(s.astype(BF16), w_ref[...].astype(BF16)) + b_ref[...]


def _ada(c_rows, ada_w, ada_b):
    depth = ada_w.shape[0]
    n = ada_w.shape[2]
    cw = 1536
    return pl.pallas_call(
        _ada_kernel,
        grid=(depth, n // cw),
        in_specs=[pl.BlockSpec((8, D), lambda l, j: (0, 0)),
                  pl.BlockSpec((None, D, cw), lambda l, j: (l, 0, j)),
                  pl.BlockSpec((None, 1, cw), lambda l, j: (l, 0, j))],
        out_specs=pl.BlockSpec((None, 8, cw), lambda l, j: (l, 0, j)),
        out_shape=jax.ShapeDtypeStruct((depth, 8, n), F32),
        compiler_params=_cparams(2),
        name="ada_mod",
    )(c_rows, ada_w, ada_b.reshape(depth, 1, n))


def _modulated_rmsnorm(x, g, shift, scale):
    y = x * lax.rsqrt(jnp.mean(x * x, axis=-1, keepdims=True) + EPS) * g
    return y * (1.0 + scale) + shift


def _in_proj_kernel(x_ref, mod_ref, g_ref, w_ref, o_ref, h_scr):
    h = _modulated_rmsnorm(x_ref[...], g_ref[...], mod_ref[:, 0:D], mod_ref[:, D:2 * D])
    h_scr[...] = h.astype(BF16)
    for c in range(N_IN_PAD // IN_CHUNK):
        cols = slice(c * IN_CHUNK, (c + 1) * IN_CHUNK)
        o_ref[:, cols] = _dot(h_scr[...], w_ref[:, cols]).astype(BF16)


def _in_proj(xs, mod3, g, w_in_r, n_tiles, mod_row):
    n_rows = xs.shape[0]
    return pl.pallas_call(
        _in_proj_kernel,
        grid=(n_tiles,),
        in_specs=[pl.BlockSpec((TM, D), lambda t: (t, 0)),
                  pl.BlockSpec((None, 1, 6 * D), lambda t: (mod_row(t), 0, 0)),
                  pl.BlockSpec((1, D), lambda t: (0, 0)),
                  pl.BlockSpec((D, N_IN_PAD), lambda t: (0, 0))],
        out_specs=pl.BlockSpec((TM, N_IN_PAD), lambda t: (t, 0)),
        out_shape=jax.ShapeDtypeStruct((n_rows, N_IN_PAD), BF16),
        scratch_shapes=[pltpu.VMEM((TM, D), BF16)],
        compiler_params=_cparams(1),
        name="in_proj",
    )(xs, mod3, g, w_in_r)


def _retention_kernel(lg_ref, q_ref, k_ref, v_ref, g_ref, gng_ref, gnb_ref, y_ref, of_scr, ob_scr,
                      *, n_lat_chunks, n_ctx_chunks):
    h = pl.program_id(1)
    lg_f = lg_ref[0, h]
    lg_b = lg_ref[1, h]
    C = RET_CHUNK
    k_scale = RET_D ** -0.5
    ri = lax.broadcasted_iota(jnp.int32, (C, C), 0).astype(F32)
    ci = lax.broadcasted_iota(jnp.int32, (C, C), 1).astype(F32)
    pos = lax.broadcasted_iota(jnp.int32, (C, 1), 0).astype(F32)
    diff = ri - ci
    d_f = jnp.where(diff >= 0, jnp.exp(lg_f * jnp.maximum(diff, 0.0)), 0.0) * k_scale
    d_b = jnp.where(diff < 0, jnp.exp(lg_b * jnp.maximum(-diff, 0.0)), 0.0) * k_scale
    qdec_f = jnp.exp(lg_f * (pos + 1.0))
    kdec_f = jnp.exp(lg_f * (C - 1.0 - pos)) * k_scale
    cdec_f = jnp.exp(lg_f * C)
    qdec_b = jnp.exp(lg_b * (C - pos))
    kdec_b = jnp.exp(lg_b * pos) * k_scale
    cdec_b = jnp.exp(lg_b * C)

    def chunk(c, state, dmat, qdec, kdec, cdec):
        rows = pl.ds(pl.multiple_of(c * C, C), C)
        q = q_ref[rows, :]
        k = k_ref[rows, :]
        v = v_ref[rows, :]
        att = (_dot_nt(q, k) * dmat).astype(BF16)
        o = _dot(att, v) + _dot((q.astype(F32) * qdec).astype(BF16), state.astype(BF16))
        kd = (k.astype(F32) * kdec).astype(BF16)
        return rows, o, state * cdec + _dot_tn(kd, v)

    n_all = n_lat_chunks + n_ctx_chunks

    def scan_body(i, states):
        s_f, s_b = states
        c_f = jnp.where(i < n_ctx_chunks, n_lat_chunks + i, i - n_ctx_chunks)
        rows, o, s_f = chunk(c_f, s_f, d_f, qdec_f, kdec_f, cdec_f)
        of_scr[rows, :] = o
        rows, o, s_b = chunk(n_all - 1 - i, s_b, d_b, qdec_b, kdec_b, cdec_b)
        ob_scr[rows, :] = o
        return s_f, s_b

    zero = jnp.zeros((RET_D, RET_D), F32)
    lax.fori_loop(0, n_all, scan_body, (zero, zero), unroll=2)

    def out_body(c, _):
        rows = pl.ds(pl.multiple_of(c * RET_OUT_ROWS, RET_OUT_ROWS), RET_OUT_ROWS)
        o = of_scr[rows, :] + ob_scr[rows, :]
        mu = jnp.mean(o, axis=-1, keepdims=True)
        var = jnp.mean(jnp.square(o - mu), axis=-1, keepdims=True)
        on = (o - mu) * lax.rsqrt(var + EPS)
        y = _silu(g_ref[rows, :].astype(F32)) * (on * gng_ref[...] + gnb_ref[...])
        y_ref[rows, :] = y.astype(BF16)
        return 0

    lax.fori_loop(0, n_all * C // RET_OUT_ROWS, out_body, 0)


def _retention(p3, lg, gn_g, gn_b, seq, ctx):
    B, lt, _ = p3.shape
    base = C_RET // RET_D
    kern = functools.partial(_retention_kernel, n_lat_chunks=seq // RET_CHUNK, n_ctx_chunks=ctx // RET_CHUNK)

    def col(off):
        return pl.BlockSpec((None, lt, RET_D), lambda b, h: (b, 0, base + off * RET_HEADS + h))

    return pl.pallas_call(
        kern,
        grid=(B, RET_HEADS),
        in_specs=[pl.BlockSpec(memory_space=pltpu.SMEM),
                  col(0), col(1), col(2), col(3),
                  pl.BlockSpec((1, RET_D), lambda b, h: (0, h)),
                  pl.BlockSpec((1, RET_D), lambda b, h: (0, h))],
        out_specs=pl.BlockSpec((None, lt, RET_D), lambda b, h: (b, 0, h)),
        out_shape=jax.ShapeDtypeStruct((B, lt, RET_HEADS * RET_D), BF16),
        scratch_shapes=[pltpu.VMEM((lt, RET_D), F32), pltpu.VMEM((lt, RET_D), F32)],
        compiler_params=_cparams(2),
        name="retention",
    )(lg, p3, p3, p3, p3, gn_g, gn_b)


def _rope_rotate(x, first_half):
    return jnp.where(first_half, pltpu.roll(x, HEAD_PAD - 8, 1), pltpu.roll(x, 8, 1))


def _mla_prep_kernel(cq_ref, ckv_ref, kr_ref, cos_ref, sin_ref, qa_ref, kva_ref, qn_ref, kn_ref, krg_ref,
                     wq_ref, wk_ref, wv_ref, q_ref, k_ref, v_ref):
    lane = lax.broadcasted_iota(jnp.int32, (1, HEAD_PAD), 1)
    first_half = (lane % 16) < 8
    cos = cos_ref[...]
    sin = sin_ref[...]

    def rms(x, n):
        return x * lax.rsqrt(jnp.sum(x * x, axis=-1, keepdims=True) * (1.0 / n) + EPS)

    def rope(x):
        return x * cos + _rope_rotate(x, first_half) * sin

    cq = cq_ref[...].astype(F32)
    cqn = (rms(cq, MLA_Q_LORA) * qa_ref[...]).astype(BF16)
    q_all = _dot(cqn, wq_ref[...])
    ckv = ckv_ref[...].astype(F32)
    ckvn = (rms(ckv, MLA_KV_LORA) * kva_ref[...]).astype(BF16)
    k_all = _dot(ckvn, wk_ref[...])
    v_all = _dot(ckvn, wv_ref[...])
    k_rope = rope(rms(kr_ref[...].astype(F32), MLA_ROPE) * krg_ref[...])
    scale = MLA_QK ** -0.5 * LOG2_E
    v_t = v_all.T
    ones_row = jnp.where(lax.broadcasted_iota(jnp.int32, (MLA_V_EXT - MLA_V, TM), 0) == 0, 1.0, 0.0)
    for h in range(MLA_HEADS):
        cols = slice(h * HEAD_PAD, (h + 1) * HEAD_PAD)
        qh = rope(rms(q_all[:, cols], MLA_QK) * qn_ref[...]) * scale
        q_ref[h] = qh.astype(BF16)
        kh = rms(k_all[:, cols], MLA_NOPE) * kn_ref[...] + k_rope
        k_ref[h] = kh.astype(BF16)
        v_ref[h] = jnp.concatenate([v_t[h * MLA_V:(h + 1) * MLA_V, :], ones_row], axis=0).astype(BF16)


def _mla_prep(p, cos_t, sin_t, qa_g, kva_g, qn_p, kn_p, kr_p, wq_p, wk_p, wv, B, lt):
    tiles_per_b = lt // TM
    hw = MLA_HEADS * HEAD_PAD
    const = lambda shape: pl.BlockSpec(shape, lambda b, j: (0,) * len(shape))
    head_out = pl.BlockSpec((None, MLA_HEADS, TM, HEAD_PAD), lambda b, j: (b, 0, j, 0))
    shp = jax.ShapeDtypeStruct((B, MLA_HEADS, lt, HEAD_PAD), BF16)
    v_out = pl.BlockSpec((None, MLA_HEADS, None, MLA_V_EXT, TM), lambda b, j: (b, 0, j, 0, 0))
    v_shp = jax.ShapeDtypeStruct((B, MLA_HEADS, tiles_per_b, MLA_V_EXT, TM), BF16)
    return pl.pallas_call(
        _mla_prep_kernel,
        grid=(B, tiles_per_b),
        in_specs=[pl.BlockSpec((TM, MLA_Q_LORA), lambda b, j: (b * tiles_per_b + j, C_CQ // MLA_Q_LORA)),
                  pl.BlockSpec((TM, MLA_KV_LORA), lambda b, j: (b * tiles_per_b + j, C_CKV // MLA_KV_LORA)),
                  pl.BlockSpec((TM, HEAD_PAD), lambda b, j: (b * tiles_per_b + j, C_KR // HEAD_PAD)),
                  pl.BlockSpec((TM, HEAD_PAD), lambda b, j: (j, 0)),
                  pl.BlockSpec((TM, HEAD_PAD), lambda b, j: (j, 0)),
                  const((1, MLA_Q_LORA)), const((1, MLA_KV_LORA)),
                  const((1, HEAD_PAD)), const((1, HEAD_PAD)), const((1, HEAD_PAD)),
                  const((MLA_Q_LORA, hw)), const((MLA_KV_LORA, hw)), const((MLA_KV_LORA, MLA_HEADS * MLA_V))],
        out_specs=[head_out, head_out, v_out],
        out_shape=[shp, shp, v_shp],
        compiler_params=_cparams(2),
        name="mla_prep",
    )(p, p, p, cos_t, sin_t, qa_g, kva_g, qn_p, kn_p, kr_p, wq_p, wk_p, wv)


def _attention_kernel(q_ref, k_ref, v_ref, o_ref, s_scr, *, seq, ctx, ctx_tile):
    i = pl.program_id(2)
    n_blk = ATT_KV_CHUNK // TM
    n_chunks = seq // ATT_KV_CHUNK
    unroll = math.gcd(n_chunks, ATT_UNROLL)

    def scores(hh, slot, blk, nb):
        start = blk * TM if isinstance(blk, int) else pl.multiple_of(blk * TM, TM)
        s_scr[hh, slot, 0:nb * TM, :] = _dot_nt(k_ref[hh, pl.ds(start, nb * TM), :], q_ref[hh])

    def absorb(hh, slot, blk, nb, carry):
        m, acc = carry
        s = s_scr[hh, slot, 0:nb * TM, :]
        m_new = jnp.maximum(m, jnp.max(s, axis=0, keepdims=True))
        p = jnp.exp2(s - m_new).astype(BF16)
        acc = jnp.exp2(m - m_new) * acc
        for j in range(nb):
            acc = acc + _dot(v_ref[hh, blk + j], p[j * TM:(j + 1) * TM, :])
        return m_new, acc

    def init():
        return (jnp.full((1, TM), -jnp.inf, F32), jnp.zeros((MLA_V_EXT, TM), F32))

    def write(carries):
        outs = [acc[0:MLA_V, :] / acc[MLA_V:MLA_V + 1, :] for _, acc in carries]
        o_ref[...] = jnp.concatenate(outs, axis=0).T.astype(BF16)

    def step(carries, slot, blk, nb, next_blk, next_nb):
        out = []
        for hh in range(2):
            if next_blk is not None:
                scores(hh, 1 - slot, next_blk, next_nb)
            out.append(absorb(hh, slot, blk, nb, carries[hh]))
        return tuple(out)

    ctx_blk = seq // TM
    ctx_nb = ctx // TM

    @pl.when(i != ctx_tile)
    def _():
        for hh in range(2):
            scores(hh, 0, ctx_blk, ctx_nb)
        carries = step((init(), init()), 0, ctx_blk, ctx_nb, 0, n_blk)
        last_blk = (n_chunks - 1) * n_blk

        def body(c, carries):
            for u in range(unroll):
                blk = (c * unroll + u) * n_blk
                carries = step(carries, (1 + u) % 2, blk, n_blk, jnp.minimum(blk + n_blk, last_blk), n_blk)
            return carries

        if n_chunks == unroll:
            for u in range(n_chunks):
                nxt = (u + 1) * n_blk if u + 1 < n_chunks else None
                carries = step(carries, (1 + u) % 2, u * n_blk, n_blk, nxt, n_blk)
            write(carries)
        else:
            write(lax.fori_loop(0, n_chunks // unroll, body, carries))

    @pl.when(i == ctx_tile)
    def _():
        for hh in range(2):
            scores(hh, 0, ctx_blk, ctx_nb)
        write(step((init(), init()), 0, ctx_blk, ctx_nb, None, None))


def _attention(q, k, v, seq, ctx, n_q_tiles):
    B, H, lt, _ = q.shape
    kern = functools.partial(_attention_kernel, seq=seq, ctx=ctx, ctx_tile=seq // TM)
    return pl.pallas_call(
        kern,
        grid=(B, H // 2, n_q_tiles),
        in_specs=[pl.BlockSpec((None, 2, TM, HEAD_PAD), lambda b, h, i: (b, h, i, 0)),
                  pl.BlockSpec((None, 2, lt, HEAD_PAD), lambda b, h, i: (b, h, 0, 0)),
                  pl.BlockSpec((None, 2, lt // TM, MLA_V_EXT, TM), lambda b, h, i: (b, h, 0, 0, 0))],
        out_specs=pl.BlockSpec((None, TM, HEAD_PAD), lambda b, h, i: (b, i, h)),
        out_shape=jax.ShapeDtypeStruct((B, lt, (H // 2) * HEAD_PAD), BF16),
        scratch_shapes=[pltpu.VMEM((2, 2, ATT_KV_CHUNK, TM), F32)],
        compiler_params=_cparams(3),
        name="attention",
    )(q, k, v)


def _merge_kernel(x_ref, mod_ref, mg_ref, uv_ref, yr_ref, om_ref, lng_ref, lnb_ref, ws_ref, bs_ref,
                  wr_ref, wm_ref, wg_ref, wo_ref, n2_ref, x1_ref, h2_ref):
    yr = _dot(yr_ref[...], wr_ref[...])
    ym = _dot(om_ref[...], wm_ref[...])
    z = jax.nn.gelu(uv_ref[...].astype(F32))
    u = z[:, :GMLP_W]
    v = z[:, GMLP_W:]
    mu = jnp.mean(v, axis=-1, keepdims=True)
    var = jnp.mean(jnp.square(v - mu), axis=-1, keepdims=True)
    vn = ((v - mu) * lax.rsqrt(var + EPS) * lng_ref[...] + lnb_ref[...]).astype(BF16)
    gw = GMLP_W // GMLP_GROUPS
    chunks = []
    for c in range(TM // GMLP_CHUNK):
        rows = slice(c * GMLP_CHUNK, (c + 1) * GMLP_CHUNK)
        groups = [_dot(ws_ref[g], vn[rows, g * gw:(g + 1) * gw]) + bs_ref[g] for g in range(GMLP_GROUPS)]
        chunks.append(jnp.concatenate(groups, axis=1))
    sv = jnp.concatenate(chunks, axis=0)
    yg = _dot((u * sv).astype(BF16), wg_ref[...])
    gate = jax.nn.sigmoid(mg_ref[...].astype(F32))
    y = gate[:, :D] * yr + gate[:, D:2 * D] * ym + gate[:, 2 * D:] * yg
    out = _dot(y.astype(BF16), wo_ref[...])
    x1 = x_ref[...] + mod_ref[:, 2 * D:3 * D] * out
    x1_ref[...] = x1
    h2_ref[...] = _modulated_rmsnorm(x1, n2_ref[...], mod_ref[:, 3 * D:4 * D], mod_ref[:, 4 * D:5 * D])


def _merge(xs, mod3, p, y_ret, o_mla, ln_g, ln_b, ws, bs_full, w_br_ret, w_br_mla, w_br_gmlp, w_out, n2_g,
           n_tiles, tile, mod_row):
    n_rows = xs.shape[0]
    const = lambda shape: pl.BlockSpec(shape, lambda t: (0,) * len(shape))
    row = lambda w, cb=0: pl.BlockSpec((TM, w), lambda t: (tile(t), cb))
    shp = jax.ShapeDtypeStruct((n_rows, D), F32)
    return pl.pallas_call(
        _merge_kernel,
        grid=(n_tiles,),
        in_specs=[row(D),
                  pl.BlockSpec((None, 1, 6 * D), lambda t: (mod_row(tile(t)), 0, 0)),
                  row(3 * D, C_MERGE // (3 * D)), row(D, C_UV // D),
                  row(RET_HEADS * RET_D), row(MLA_HEADS * MLA_V),
                  const((1, GMLP_W)), const((1, GMLP_W)),
                  const((GMLP_GROUPS, GMLP_CHUNK, GMLP_CHUNK)), const((GMLP_GROUPS, GMLP_CHUNK, GMLP_CHUNK)),
                  const((RET_HEADS * RET_D, D)), const((MLA_HEADS * MLA_V, D)), const((GMLP_W, D)),
                  const((D, D)), const((1, D))],
        out_specs=[row(D), row(D)],
        out_shape=[shp, shp],
        compiler_params=_cparams(1),
        name="merge",
    )(xs, mod3, p, p, y_ret, o_mla, ln_g, ln_b, ws, bs_full, w_br_ret, w_br_mla, w_br_gmlp, w_out, n2_g)


def _route_kernel(ha_ref, hb_ref, r_ref, b_ref, sg_ref, su_ref, sd_ref, idx_ref, w_ref, rank_ref, cnt_ref, sh_ref,
                  hp_ref, cnt_scr):
    @pl.when(pl.program_id(0) == 0)
    def _():
        cnt_scr[...] = jnp.zeros_like(cnt_scr)

    h = jnp.concatenate([ha_ref[...], hb_ref[...]], axis=0)
    logits = jnp.dot(h, r_ref[...], preferred_element_type=F32, precision=lax.Precision.HIGHEST)
    scores = jax.nn.sigmoid(logits)
    sel = scores + b_ref[...]
    lane_e = lax.broadcasted_iota(jnp.int32, (RT, N_EXPERTS), 1).astype(F32)
    lane_o = lax.broadcasted_iota(jnp.int32, (RT, 128), 1)
    idx_out = jnp.zeros((RT, 128), F32)
    w_out = jnp.zeros((RT, 128), F32)
    hits = []
    for k in range(TOP_K):
        best = jnp.max(sel, axis=-1, keepdims=True)
        pick = jnp.min(jnp.where(sel == best, lane_e, float(N_EXPERTS)), axis=-1, keepdims=True)
        hit = lane_e == pick
        hits.append(hit)
        wk = jnp.sum(jnp.where(hit, scores, 0.0), axis=-1, keepdims=True)
        sel = jnp.where(hit, -jnp.inf, sel)
        idx_out = jnp.where(lane_o == k, pick, idx_out)
        w_out = jnp.where(lane_o == k, wk, w_out)
    w_out = w_out / jnp.sum(w_out, axis=-1, keepdims=True) * ROUTED_SCALE
    idx_ref[...] = idx_out.astype(jnp.int32)
    w_ref[...] = w_out
    chosen = jnp.zeros((RT, N_EXPERTS), F32)
    for hit in hits:
        chosen = jnp.where(hit, 1.0, chosen)
    earlier = (lax.broadcasted_iota(jnp.int32, (RT, RT), 0) > lax.broadcasted_iota(jnp.int32, (RT, RT), 1))
    before = _dot(jnp.where(earlier, 1.0, 0.0).astype(BF16), chosen.astype(BF16)) + cnt_scr[...]
    rank_out = jnp.zeros((RT, 128), F32)
    for k, hit in enumerate(hits):
        rank_out = jnp.where(lane_o == k, jnp.sum(jnp.where(hit, before, 0.0), axis=-1, keepdims=True), rank_out)
    rank_ref[...] = rank_out.astype(jnp.int32)
    cnt_scr[...] += jnp.sum(chosen, axis=0, keepdims=True)
    cnt_ref[...] = cnt_scr[...]
    hb = h.astype(BF16)
    a = _silu(_dot(hb, sg_ref[...])) * _dot(hb, su_ref[...])
    sh_ref[...] = _dot(a.astype(BF16), sd_ref[...])
    hp_ref[...] = _pack_bf16_pairs(h)


def _route(h2, router, bias, sg, su, sd, n_tiles, tile):
    const = lambda shape: pl.BlockSpec(shape, lambda t: (0,) * len(shape))
    n_act = n_tiles * TM
    assert n_tiles % 2 == 0
    return pl.pallas_call(
        _route_kernel,
        grid=(n_tiles // 2,),
        in_specs=[pl.BlockSpec((TM, D), lambda t: (tile(2 * t), 0)),
                  pl.BlockSpec((TM, D), lambda t: (tile(2 * t + 1), 0)),
                  const((D, N_EXPERTS)), const((1, N_EXPERTS)),
                  const((D, D_EXPERT)), const((D, D_EXPERT)), const((D_EXPERT, D))],
        out_specs=[pl.BlockSpec((RT, 128), lambda t: (t, 0)),
                   pl.BlockSpec((RT, 128), lambda t: (t, 0)),
                   pl.BlockSpec((RT, 128), lambda t: (t, 0)),
                   pl.BlockSpec((1, N_EXPERTS), lambda t: (0, 0)),
                   pl.BlockSpec((RT, D), lambda t: (t, 0)),
                   pl.BlockSpec((RT, D // 2), lambda t: (t, 0))],
        out_shape=[jax.ShapeDtypeStruct((n_act, 128), jnp.int32),
                   jax.ShapeDtypeStruct((n_act, 128), F32),
                   jax.ShapeDtypeStruct((n_act, 128), jnp.int32),
                   jax.ShapeDtypeStruct((1, N_EXPERTS), F32),
                   jax.ShapeDtypeStruct((n_act, D), F32),
                   jax.ShapeDtypeStruct((n_act, D // 2), jnp.uint32)],
        scratch_shapes=[pltpu.VMEM((1, N_EXPERTS), F32)],
        compiler_params=_cparams(1),
        name="route_shared",
    )(h2, h2, router, bias, sg, su, sd)


def _dispatch(pos_sc, hp, n_rows):
    n_batches = pos_sc.shape[0]
    n_workers = SC_CORES * SC_SUBCORES
    mesh = plsc.VectorSubcoreMesh(core_axis_name="c", subcore_axis_name="s")

    @functools.partial(
        pl.kernel, mesh=mesh,
        out_type=jax.ShapeDtypeStruct((n_rows, D // 2), jnp.uint32),
        scratch_types=[pltpu.VMEM((TOP_K, SC_ROWS), jnp.int32),
                       pltpu.VMEM((SC_ROWS, D // 2), jnp.uint32),
                       pltpu.SemaphoreType.DMA],
        name="moe_dispatch")
    def scatter(pos_hbm, h_hbm, xs_hbm, idx_v, rows_v, sem):
        worker = lax.axis_index("s") * SC_CORES + lax.axis_index("c")

        @pl.loop(0, pl.cdiv(n_batches, n_workers))
        def _(j):
            b = j * n_workers + worker

            @pl.when(b < n_batches)
            def _():
                pltpu.sync_copy(pos_hbm.at[b], idx_v)
                pltpu.sync_copy(h_hbm.at[pl.ds(b * SC_ROWS, SC_ROWS)], rows_v)
                copies = [pltpu.async_copy(rows_v, xs_hbm.at[idx_v.at[k]], sem) for k in range(TOP_K)]
                for cp in copies:
                    cp.wait()

    return scatter(pos_sc, hp)


def _expert_kernel(blk_e_ref, valid_ref, x_ref, wg_ref, wu_ref, wd_ref, y_ref, wg_s, wu_s, wd_s):
    i = pl.program_id(0)
    n_valid = valid_ref[i]

    @pl.when(n_valid > 0)
    def _():
        @pl.when(jnp.logical_or(i == 0, blk_e_ref[i] != blk_e_ref[jnp.maximum(i - 1, 0)]))
        def _():
            wg_s[...] = wg_ref[...].astype(BF16)
            wu_s[...] = wu_ref[...].astype(BF16)
            wd_s[...] = wd_ref[...].astype(BF16)

        row = lax.broadcasted_iota(jnp.int32, (MOE_ROWS, 1), 0)
        lo, hi = _unpack_bf16_pairs(jnp.where(row < n_valid, x_ref[...], jnp.uint32(0)))
        x = jnp.concatenate([lo, hi], axis=1).astype(BF16)
        hb = _silu(_dot(x, wg_s[...])) * _dot(x, wu_s[...])
        y_ref[...] = _pack_bf16_pairs(_dot(hb.astype(BF16), wd_s[...]))

    @pl.when(n_valid == 0)
    def _():
        y_ref[...] = jnp.zeros_like(y_ref)


def _experts(blk_e, valid, xs, wg, wu, wd, n_blocks):
    grid_spec = pltpu.PrefetchScalarGridSpec(
        num_scalar_prefetch=2,
        grid=(n_blocks,),
        in_specs=[pl.BlockSpec((MOE_ROWS, D // 2), lambda i, be, nv: (i, 0)),
                  pl.BlockSpec((None, D, D_EXPERT), lambda i, be, nv: (be[i], 0, 0)),
                  pl.BlockSpec((None, D, D_EXPERT), lambda i, be, nv: (be[i], 0, 0)),
                  pl.BlockSpec((None, D_EXPERT, D), lambda i, be, nv: (be[i], 0, 0))],
        out_specs=pl.BlockSpec((MOE_ROWS, D // 2), lambda i, be, nv: (i, 0)),
        scratch_shapes=[pltpu.VMEM((D, D_EXPERT), BF16), pltpu.VMEM((D, D_EXPERT), BF16),
                        pltpu.VMEM((D_EXPERT, D), BF16)],
    )
    return pl.pallas_call(
        _expert_kernel,
        grid_spec=grid_spec,
        out_shape=jax.ShapeDtypeStruct((n_blocks * MOE_ROWS, D // 2), jnp.uint32),
        compiler_params=_cparams(1),
        name="routed_experts",
    )(blk_e, valid, xs, wg, wu, wd)


def _gather_rows(pos_sc, ys):
    n_batches = pos_sc.shape[0]
    n_workers = SC_CORES * SC_SUBCORES
    half = SC_ROWS // 2
    mesh = plsc.VectorSubcoreMesh(core_axis_name="c", subcore_axis_name="s")

    @functools.partial(
        pl.kernel, mesh=mesh,
        out_type=jax.ShapeDtypeStruct((TOP_K, n_batches * SC_ROWS, D // 2), jnp.uint32),
        scratch_types=[pltpu.VMEM((TOP_K, SC_ROWS), jnp.int32),
                       pltpu.VMEM((2, half, D // 2), jnp.uint32),
                       pltpu.SemaphoreType.DMA, pltpu.SemaphoreType.DMA],
        name="moe_gather")
    def gather(pos_hbm, y_hbm, out_hbm, idx_v, bufs, gsem, wsem):
        worker = lax.axis_index("s") * SC_CORES + lax.axis_index("c")

        @pl.loop(0, pl.cdiv(n_batches, n_workers))
        def _(j):
            b = j * n_workers + worker

            @pl.when(b < n_batches)
            def _():
                pltpu.sync_copy(pos_hbm.at[b], idx_v)
                items = [(k, h) for k in range(TOP_K) for h in range(2)]

                def fetch(i):
                    k, h = items[i]
                    return pltpu.async_copy(y_hbm.at[idx_v.at[k, pl.ds(h * half, half)]], bufs.at[i % 2], gsem)

                pending_gather = fetch(0)
                pending_write = None
                for i, (k, h) in enumerate(items):
                    pending_gather.wait()
                    if pending_write is not None:
                        pending_write.wait()
                    if i + 1 < len(items):
                        pending_gather = fetch(i + 1)
                    pending_write = pltpu.async_copy(
                        bufs.at[i % 2], out_hbm.at[k, pl.ds(b * SC_ROWS + h * half, half)], wsem)
                pending_write.wait()

    return gather(pos_sc, ys)


def _combine_kernel(*refs):
    y_refs = refs[:TOP_K]
    w_ref, x1_ref, sh_ref, mod_ref, o_ref = refs[TOP_K:]
    f_lo = sh_ref[:, :D // 2]
    f_hi = sh_ref[:, D // 2:]
    for k in range(TOP_K):
        lo, hi = _unpack_bf16_pairs(y_refs[k][...])
        f_lo = f_lo + lo * w_ref[:, k:k + 1]
        f_hi = f_hi + hi * w_ref[:, k:k + 1]
    o_ref[...] = x1_ref[...] + mod_ref[:, 5 * D:6 * D] * jnp.concatenate([f_lo, f_hi], axis=1)


def _combine(yg, w, x1, shared, mod3, n_tiles, tile, mod_row, out_rows, out_tile):
    planes = [pl.BlockSpec((None, TM, D // 2), lambda t, k=k: (k, t, 0)) for k in range(TOP_K)]
    return pl.pallas_call(
        _combine_kernel,
        grid=(n_tiles,),
        in_specs=planes + [pl.BlockSpec((TM, 128), lambda t: (t, 0)),
                           pl.BlockSpec((TM, D), lambda t: (tile(t), 0)),
                           pl.BlockSpec((TM, D), lambda t: (t, 0)),
                           pl.BlockSpec((None, 1, 6 * D), lambda t: (mod_row(tile(t)), 0, 0))],
        out_specs=pl.BlockSpec((TM, D), lambda t: (out_tile(t), 0)),
        out_shape=jax.ShapeDtypeStruct((out_rows, D), F32),
        compiler_params=_cparams(1),
        name="moe_combine",
    )(*([yg] * TOP_K), w, x1, shared, mod3)


def _moe_plan(idx, rank, counts, n_blocks):
    n = idx.shape[0]
    cnt = counts.reshape(N_EXPERTS).astype(jnp.int32)
    padded = (cnt + MOE_ROWS - 1) // MOE_ROWS * MOE_ROWS
    pad_end = jnp.cumsum(padded)
    pad_start = pad_end - padded
    experts = jnp.arange(N_EXPERTS, dtype=jnp.int32)
    pos = rank + jnp.sum(jnp.where(idx[:, :, None] == experts, pad_start, 0), axis=-1)
    blk_start = jnp.arange(n_blocks, dtype=jnp.int32) * MOE_ROWS
    blk_e = jnp.minimum(jnp.sum(blk_start[:, None] >= pad_end[None, :], axis=1), N_EXPERTS - 1).astype(jnp.int32)
    mine = blk_e[:, None] == experts
    in_expert = blk_start - jnp.sum(jnp.where(mine, pad_start, 0), axis=1)
    valid = jnp.clip(jnp.sum(jnp.where(mine, cnt, 0), axis=1) - in_expert, 0, MOE_ROWS).astype(jnp.int32)
    pos_sc = pos.astype(jnp.int32).reshape(n // SC_ROWS, SC_ROWS, TOP_K).transpose(0, 2, 1)
    return blk_e, valid, pos_sc


def _rope_tables(seq, ctx):
    half = MLA_ROPE // 2
    n_freq = half // 2
    inv = ROPE_THETA ** (-2.0 * jnp.arange(n_freq, dtype=F32) / half)
    t = jnp.arange(seq)
    ang_r = (t // GRID_W).astype(F32)[:, None] * inv
    ang_c = (t % GRID_W).astype(F32)[:, None] * inv
    cos = jnp.concatenate([jnp.cos(ang_r), jnp.cos(ang_r), jnp.cos(ang_c), jnp.cos(ang_c)], axis=1)
    sin = jnp.concatenate([-jnp.sin(ang_r), jnp.sin(ang_r), -jnp.sin(ang_c), jnp.sin(ang_c)], axis=1)
    pad_l = MLA_NOPE
    pad_r = HEAD_PAD - MLA_NOPE - MLA_ROPE
    cos = jnp.pad(cos, ((0, ctx), (pad_l, pad_r)), constant_values=1.0)
    cos = cos.at[seq:, :].set(1.0)
    sin = jnp.pad(sin, ((0, ctx), (pad_l, pad_r)))
    return cos, sin


def _pad_heads(w, n_heads, width, offset=0):
    k = w.shape[0]
    w = w.reshape(k, n_heads, width)
    w = jnp.pad(w, ((0, 0), (0, 0), (offset, HEAD_PAD - width - offset)))
    return w.reshape(k, n_heads * HEAD_PAD)


def _pad_vec(g, offset):
    return jnp.pad(g, (offset, HEAD_PAD - g.shape[0] - offset)).reshape(1, HEAD_PAD)


def _reorder_w_in(w):
    off_cq, off_ckv, off_kr, off_uv, off_merge = 2048, 2432, 2688, 2720, 3744
    kr = jnp.pad(w[:, off_kr:off_uv], ((0, 0), (MLA_NOPE, HEAD_PAD - MLA_NOPE - MLA_ROPE)))
    return jnp.concatenate([w[:, off_merge:], w[:, off_uv:off_merge], w[:, :off_cq],
                            w[:, off_ckv:off_kr], kr, w[:, off_cq:off_ckv]], axis=1).astype(BF16)


def kernel(x, c, ctx, c_ctx, ada_w, ada_b, norm1_g, norm2_g, w_in, ret_decay_fwd, ret_decay_bwd, ret_gn_g,
           ret_gn_b, w_br_ret, mla_qa_g, mla_w_uq, mla_kva_g, mla_w_ukv, mla_qn_g, mla_kn_g, mla_kr_g, w_br_mla,
           gmlp_ln_g, gmlp_ln_b, gmlp_ws, gmlp_bs, w_br_gmlp, w_out, moe_router, moe_bias, moe_w_gate, moe_w_up,
           moe_w_down, sh_w_gate, sh_w_up, sh_w_down):
    B, seq, _ = x.shape
    n_ctx = ctx.shape[1]
    depth = ada_w.shape[0]
    assert n_ctx == TM and seq % (2 * ATT_KV_CHUNK) == 0 and seq % TM == 0
    lt = seq + n_ctx
    tiles_per_b = lt // TM
    lat_tiles_per_b = seq // TM
    ctx_tile = lat_tiles_per_b

    def mod_row(t):
        return jnp.where(t % tiles_per_b == ctx_tile, B, t // tiles_per_b)

    c_rows = jnp.concatenate([c, c_ctx[None, :], jnp.zeros((8 - B - 1, D), F32)], axis=0)
    mod = _ada(c_rows, ada_w, ada_b)
    cos_t, sin_t = _rope_tables(seq, n_ctx)
    xs = jnp.concatenate([x, ctx], axis=1).reshape(B * lt, D)

    for l in range(depth):
        last = l == depth - 1
        mod3 = mod[l].reshape(8, 1, 6 * D)
        p = _in_proj(xs, mod3, norm1_g[l].reshape(1, D), _reorder_w_in(w_in[l]), B * tiles_per_b, mod_row)

        lg = jnp.stack([jax.nn.log_sigmoid(ret_decay_fwd[l].astype(F32)),
                        jax.nn.log_sigmoid(ret_decay_bwd[l].astype(F32))])
        y_ret = _retention(p.reshape(B, lt, N_IN_PAD), lg, ret_gn_g[l].reshape(1, -1), ret_gn_b[l].reshape(1, -1),
                           seq, n_ctx)

        w_ukv = mla_w_ukv[l].reshape(MLA_KV_LORA, MLA_HEADS, MLA_NOPE + MLA_V)
        wk_p = _pad_heads(w_ukv[:, :, :MLA_NOPE].reshape(MLA_KV_LORA, -1), MLA_HEADS, MLA_NOPE).astype(BF16)
        wv = w_ukv[:, :, MLA_NOPE:].reshape(MLA_KV_LORA, MLA_HEADS * MLA_V).astype(BF16)
        wq_p = _pad_heads(mla_w_uq[l], MLA_HEADS, MLA_QK).astype(BF16)
        q, k, v = _mla_prep(p, cos_t, sin_t, mla_qa_g[l].reshape(1, -1), mla_kva_g[l].reshape(1, -1),
                            _pad_vec(mla_qn_g[l], 0), _pad_vec(mla_kn_g[l], 0), _pad_vec(mla_kr_g[l], MLA_NOPE),
                            wq_p, wk_p, wv, B, lt)
        o_mla = _attention(q, k, v, seq, n_ctx, lat_tiles_per_b if last else tiles_per_b)

        if last:
            n_tiles = B * lat_tiles_per_b
            tile = lambda t: (t // lat_tiles_per_b) * tiles_per_b + t % lat_tiles_per_b
        else:
            n_tiles = B * tiles_per_b
            tile = lambda t: t
        bs_full = jnp.broadcast_to(gmlp_bs[l][:, :, None], (GMLP_GROUPS, GMLP_CHUNK, GMLP_CHUNK))
        x1, h2 = _merge(xs, mod3, p, y_ret.reshape(B * lt, -1), o_mla.reshape(B * lt, -1),
                        gmlp_ln_g[l].reshape(1, -1), gmlp_ln_b[l].reshape(1, -1), gmlp_ws[l].astype(BF16), bs_full,
                        w_br_ret[l].astype(BF16), w_br_mla[l].astype(BF16), w_br_gmlp[l].astype(BF16),
                        w_out[l].astype(BF16), norm2_g[l].reshape(1, D), n_tiles, tile, mod_row)

        idx, w, rank, counts, shared, hp = _route(h2, moe_router[l], moe_bias[l].reshape(1, -1),
                                                  sh_w_gate[l].astype(BF16), sh_w_up[l].astype(BF16),
                                                  sh_w_down[l].astype(BF16), n_tiles, tile)
        n_act = n_tiles * TM
        n_blocks = -(-(n_act * TOP_K + N_EXPERTS * (MOE_ROWS - 1)) // MOE_ROWS)
        blk_e, valid, pos_sc = _moe_plan(idx[:, :TOP_K], rank[:, :TOP_K], counts, n_blocks)
        xg = _dispatch(pos_sc, hp, n_blocks * MOE_ROWS)
        ys = _experts(blk_e, valid, xg, moe_w_gate[l], moe_w_up[l], moe_w_down[l], n_blocks)
        yg = _gather_rows(pos_sc, ys)
        if last:
            xs = _combine(yg, w, x1, shared, mod3, n_tiles, tile, mod_row, B * seq, lambda t: t)
        else:
            xs = _combine(yg, w, x1, shared, mod3, n_tiles, tile, mod_row, B * lt, tile)
    return xs.reshape(B, seq, D)
```

```python
import functools
import math

import jax
import jax.numpy as jnp
from jax import lax
from jax.experimental import pallas as pl
from jax.experimental.pallas import tpu as pltpu
from jax.experimental.pallas import tpu_sc as plsc

F32 = jnp.float32
BF16 = jnp.bfloat16

D = 1024
GRID_W = 64
RET_HEADS = 4
RET_D = 128
RET_CHUNK = 256
RET_OUT_ROWS = 256
MLA_HEADS = 8
MLA_Q_LORA = 384
MLA_KV_LORA = 256
MLA_NOPE = 64
MLA_ROPE = 32
MLA_V = 64
MLA_V_EXT = MLA_V + 16
MLA_QK = MLA_NOPE + MLA_ROPE
HEAD_PAD = 128
ROPE_THETA = 10000.0
GMLP_GROUPS = 4
GMLP_W = 512
GMLP_CHUNK = 128
N_EXPERTS = 64
TOP_K = 6
D_EXPERT = 256
ROUTED_SCALE = 2.5
EPS = 1e-6
LOG2_E = 1.4426950408889634

TM = 256
MOE_ROWS = 512
RT = 2 * TM
ATT_KV_CHUNK = 1024
ATT_UNROLL = 16

C_MERGE = 0
C_UV = 3072
C_RET = 4096
C_CKV = 6144
C_KR = 6400
C_CQ = 6528
N_IN_PAD = 6912
IN_CHUNK = 768

VMEM_LIMIT = 56 * 1024 * 1024

SC_CORES = 2
SC_SUBCORES = 16
SC_ROWS = 128


def _cparams(n_axes, vmem=VMEM_LIMIT):
    return pltpu.CompilerParams(dimension_semantics=("arbitrary",) * n_axes, vmem_limit_bytes=vmem)


def _silu(x):
    return x * jax.nn.sigmoid(x)


def _dot(a, b):
    return jnp.dot(a, b, preferred_element_type=F32)


def _dot_nt(a, b):
    return lax.dot_general(a, b, (((1,), (1,)), ((), ())), preferred_element_type=F32)


def _dot_tn(a, b):
    return lax.dot_general(a, b, (((0,), (0,)), ((), ())), preferred_element_type=F32)


def _pack_bf16_pairs(x):
    n = x.shape[1] // 2
    lo = lax.bitcast_convert_type(x[:, :n].astype(BF16).astype(F32), jnp.uint32)
    hi = lax.bitcast_convert_type(x[:, n:].astype(BF16).astype(F32), jnp.uint32)
    return (lo >> 16) | hi


def _unpack_bf16_pairs(u):
    lo = lax.bitcast_convert_type(u << 16, F32)
    hi = lax.bitcast_convert_type(u & jnp.uint32(0xFFFF0000), F32)
    return lo, hi


def _ada_kernel(c_ref, w_ref, b_ref, o_ref):
    s = _silu(c_ref[...])
    o_ref[...] = _dot(s.astype(BF16), w_ref[...].astype(BF16)) + b_ref[...]


def _ada(c_rows, ada_w, ada_b):
    depth = ada_w.shape[0]
    n = ada_w.shape[2]
    cw = 1536
    return pl.pallas_call(
        _ada_kernel,
        grid=(depth, n // cw),
        in_specs=[pl.BlockSpec((8, D), lambda l, j: (0, 0)),
                  pl.BlockSpec((None, D, cw), lambda l, j: (l, 0, j)),
                  pl.BlockSpec((None, 1, cw), lambda l, j: (l, 0, j))],
        out_specs=pl.BlockSpec((None, 8, cw), lambda l, j: (l, 0, j)),
        out_shape=jax.ShapeDtypeStruct((depth, 8, n), F32),
        compiler_params=_cparams(2),
        name="ada_mod",
    )(c_rows, ada_w, ada_b.reshape(depth, 1, n))


def _modulated_rmsnorm(x, g, shift, scale):
    y = x * lax.rsqrt(jnp.mean(x * x, axis=-1, keepdims=True) + EPS) * g
    return y * (1.0 + scale) + shift


def _in_proj_kernel(x_ref, mod_ref, g_ref, w_ref, o_ref, h_scr):
    h = _modulated_rmsnorm(x_ref[...], g_ref[...], mod_ref[:, 0:D], mod_ref[:, D:2 * D])
    h_scr[...] = h.astype(BF16)
    for c in range(N_IN_PAD // IN_CHUNK):
        cols = slice(c * IN_CHUNK, (c + 1) * IN_CHUNK)
        o_ref[:, cols] = _dot(h_scr[...], w_ref[:, cols]).astype(BF16)


def _in_proj(xs, mod3, g, w_in_r, n_tiles, mod_row):
    n_rows = xs.shape[0]
    return pl.pallas_call(
        _in_proj_kernel,
        grid=(n_tiles,),
        in_specs=[pl.BlockSpec((TM, D), lambda t: (t, 0)),
                  pl.BlockSpec((None, 1, 6 * D), lambda t: (mod_row(t), 0, 0)),
                  pl.BlockSpec((1, D), lambda t: (0, 0)),
                  pl.BlockSpec((D, N_IN_PAD), lambda t: (0, 0))],
        out_specs=pl.BlockSpec((TM, N_IN_PAD), lambda t: (t, 0)),
        out_shape=jax.ShapeDtypeStruct((n_rows, N_IN_PAD), BF16),
        scratch_shapes=[pltpu.VMEM((TM, D), BF16)],
        compiler_params=_cparams(1),
        name="in_proj",
    )(xs, mod3, g, w_in_r)


def _retention_kernel(lg_ref, q_ref, k_ref, v_ref, g_ref, gng_ref, gnb_ref, y_ref, of_scr, ob_scr,
                      *, n_lat_chunks, n_ctx_chunks):
    h = pl.program_id(1)
    lg_f = lg_ref[0, h]
    lg_b = lg_ref[1, h]
    C = RET_CHUNK
    k_scale = RET_D ** -0.5
    ri = lax.broadcasted_iota(jnp.int32, (C, C), 0).astype(F32)
    ci = lax.broadcasted_iota(jnp.int32, (C, C), 1).astype(F32)
    pos = lax.broadcasted_iota(jnp.int32, (C, 1), 0).astype(F32)
    diff = ri - ci
    d_f = jnp.where(diff >= 0, jnp.exp(lg_f * jnp.maximum(diff, 0.0)), 0.0) * k_scale
    d_b = jnp.where(diff < 0, jnp.exp(lg_b * jnp.maximum(-diff, 0.0)), 0.0) * k_scale
    qdec_f = jnp.exp(lg_f * (pos + 1.0))
    kdec_f = jnp.exp(lg_f * (C - 1.0 - pos)) * k_scale
    cdec_f = jnp.exp(lg_f * C)
    qdec_b = jnp.exp(lg_b * (C - pos))
    kdec_b = jnp.exp(lg_b * pos) * k_scale
    cdec_b = jnp.exp(lg_b * C)

    def chunk(c, state, dmat, qdec, kdec, cdec):
        rows = pl.ds(pl.multiple_of(c * C, C), C)
        q = q_ref[rows, :]
        k = k_ref[rows, :]
        v = v_ref[rows, :]
        att = (_dot_nt(q, k) * dmat).astype(BF16)
        o = _dot(att, v) + _dot((q.astype(F32) * qdec).astype(BF16), state.astype(BF16))
        kd = (k.astype(F32) * kdec).astype(BF16)
        return rows, o, state * cdec + _dot_tn(kd, v)

    n_all = n_lat_chunks + n_ctx_chunks

    def scan_body(i, states):
        s_f, s_b = states
        c_f = jnp.where(i < n_ctx_chunks, n_lat_chunks + i, i - n_ctx_chunks)
        rows, o, s_f = chunk(c_f, s_f, d_f, qdec_f, kdec_f, cdec_f)
        of_scr[rows, :] = o
        rows, o, s_b = chunk(n_all - 1 - i, s_b, d_b, qdec_b, kdec_b, cdec_b)
        ob_scr[rows, :] = o
        return s_f, s_b

    zero = jnp.zeros((RET_D, RET_D), F32)
    lax.fori_loop(0, n_all, scan_body, (zero, zero), unroll=2)

    def out_body(c, _):
        rows = pl.ds(pl.multiple_of(c * RET_OUT_ROWS, RET_OUT_ROWS), RET_OUT_ROWS)
        o = of_scr[rows, :] + ob_scr[rows, :]
        mu = jnp.mean(o, axis=-1, keepdims=True)
        var = jnp.mean(jnp.square(o - mu), axis=-1, keepdims=True)
        on = (o - mu) * lax.rsqrt(var + EPS)
        y = _silu(g_ref[rows, :].astype(F32)) * (on * gng_ref[...] + gnb_ref[...])
        y_ref[rows, :] = y.astype(BF16)
        return 0

    lax.fori_loop(0, n_all * C // RET_OUT_ROWS, out_body, 0)


def _retention(p3, lg, gn_g, gn_b, seq, ctx):
    B, lt, _ = p3.shape
    base = C_RET // RET_D
    kern = functools.partial(_retention_kernel, n_lat_chunks=seq // RET_CHUNK, n_ctx_chunks=ctx // RET_CHUNK)

    def col(off):
        return pl.BlockSpec((None, lt, RET_D), lambda b, h: (b, 0, base + off * RET_HEADS + h))

    return pl.pallas_call(
        kern,
        grid=(B, RET_HEADS),
        in_specs=[pl.BlockSpec(memory_space=pltpu.SMEM),
                  col(0), col(1), col(2), col(3),
                  pl.BlockSpec((1, RET_D), lambda b, h: (0, h)),
                  pl.BlockSpec((1, RET_D), lambda b, h: (0, h))],
        out_specs=pl.BlockSpec((None, lt, RET_D), lambda b, h: (b, 0, h)),
        out_shape=jax.ShapeDtypeStruct((B, lt, RET_HEADS * RET_D), BF16),
        scratch_shapes=[pltpu.VMEM((lt, RET_D), F32), pltpu.VMEM((lt, RET_D), F32)],
        compiler_params=_cparams(2),
        name="retention",
    )(lg, p3, p3, p3, p3, gn_g, gn_b)


def _rope_rotate(x, first_half):
    return jnp.where(first_half, pltpu.roll(x, HEAD_PAD - 8, 1), pltpu.roll(x, 8, 1))


def _mla_prep_kernel(cq_ref, ckv_ref, kr_ref, cos_ref, sin_ref, qa_ref, kva_ref, qn_ref, kn_ref, krg_ref,
                     wq_ref, wk_ref, wv_ref, q_ref, k_ref, v_ref):
    lane = lax.broadcasted_iota(jnp.int32, (1, HEAD_PAD), 1)
    first_half = (lane % 16) < 8
    cos = cos_ref[...]
    sin = sin_ref[...]

    def rms(x, n):
        return x * lax.rsqrt(jnp.sum(x * x, axis=-1, keepdims=True) * (1.0 / n) + EPS)

    def rope(x):
        return x * cos + _rope_rotate(x, first_half) * sin

    cq = cq_ref[...].astype(F32)
    cqn = (rms(cq, MLA_Q_LORA) * qa_ref[...]).astype(BF16)
    q_all = _dot(cqn, wq_ref[...])
    ckv = ckv_ref[...].astype(F32)
    ckvn = (rms(ckv, MLA_KV_LORA) * kva_ref[...]).astype(BF16)
    k_all = _dot(ckvn, wk_ref[...])
    v_all = _dot(ckvn, wv_ref[...])
    k_rope = rope(rms(kr_ref[...].astype(F32), MLA_ROPE) * krg_ref[...])
    scale = MLA_QK ** -0.5 * LOG2_E
    v_t = v_all.T
    ones_row = jnp.where(lax.broadcasted_iota(jnp.int32, (MLA_V_EXT - MLA_V, TM), 0) == 0, 1.0, 0.0)
    for h in range(MLA_HEADS):
        cols = slice(h * HEAD_PAD, (h + 1) * HEAD_PAD)
        qh = rope(rms(q_all[:, cols], MLA_QK) * qn_ref[...]) * scale
        q_ref[h] = qh.astype(BF16)
        kh = rms(k_all[:, cols], MLA_NOPE) * kn_ref[...] + k_rope
        k_ref[h] = kh.astype(BF16)
        v_ref[h] = jnp.concatenate([v_t[h * MLA_V:(h + 1) * MLA_V, :], ones_row], axis=0).astype(BF16)


def _mla_prep(p, cos_t, sin_t, qa_g, kva_g, qn_p, kn_p, kr_p, wq_p, wk_p, wv, B, lt):
    tiles_per_b = lt // TM
    hw = MLA_HEADS * HEAD_PAD
    const = lambda shape: pl.BlockSpec(shape, lambda b, j: (0,) * len(shape))
    head_out = pl.BlockSpec((None, MLA_HEADS, TM, HEAD_PAD), lambda b, j: (b, 0, j, 0))
    shp = jax.ShapeDtypeStruct((B, MLA_HEADS, lt, HEAD_PAD), BF16)
    v_out = pl.BlockSpec((None, MLA_HEADS, None, MLA_V_EXT, TM), lambda b, j: (b, 0, j, 0, 0))
    v_shp = jax.ShapeDtypeStruct((B, MLA_HEADS, tiles_per_b, MLA_V_EXT, TM), BF16)
    return pl.pallas_call(
        _mla_prep_kernel,
        grid=(B, tiles_per_b),
        in_specs=[pl.BlockSpec((TM, MLA_Q_LORA), lambda b, j: (b * tiles_per_b + j, C_CQ // MLA_Q_LORA)),
                  pl.BlockSpec((TM, MLA_KV_LORA), lambda b, j: (b * tiles_per_b + j, C_CKV // MLA_KV_LORA)),
                  pl.BlockSpec((TM, HEAD_PAD), lambda b, j: (b * tiles_per_b + j, C_KR // HEAD_PAD)),
                  pl.BlockSpec((TM, HEAD_PAD), lambda b, j: (j, 0)),
                  pl.BlockSpec((TM, HEAD_PAD), lambda b, j: (j, 0)),
                  const((1, MLA_Q_LORA)), const((1, MLA_KV_LORA)),
                  const((1, HEAD_PAD)), const((1, HEAD_PAD)), const((1, HEAD_PAD)),
                  const((MLA_Q_LORA, hw)), const((MLA_KV_LORA, hw)), const((MLA_KV_LORA, MLA_HEADS * MLA_V))],
        out_specs=[head_out, head_out, v_out],
        out_shape=[shp, shp, v_shp],
        compiler_params=_cparams(2),
        name="mla_prep",
    )(p, p, p, cos_t, sin_t, qa_g, kva_g, qn_p, kn_p, kr_p, wq_p, wk_p, wv)


def _attention_kernel(q_ref, k_ref, v_ref, o_ref, s_scr, *, seq, ctx, ctx_tile):
    i = pl.program_id(2)
    n_blk = ATT_KV_CHUNK // TM
    n_chunks = seq // ATT_KV_CHUNK
    unroll = math.gcd(n_chunks, ATT_UNROLL)

    def scores(hh, slot, blk, nb):
        start = blk * TM if isinstance(blk, int) else pl.multiple_of(blk * TM, TM)
        s_scr[hh, slot, 0:nb * TM, :] = _dot_nt(k_ref[hh, pl.ds(start, nb * TM), :], q_ref[hh])

    def absorb(hh, slot, blk, nb, carry):
        m, acc = carry
        s = s_scr[hh, slot, 0:nb * TM, :]
        m_new = jnp.maximum(m, jnp.max(s, axis=0, keepdims=True))
        p = jnp.exp2(s - m_new).astype(BF16)
        acc = jnp.exp2(m - m_new) * acc
        for j in range(nb):
            acc = acc + _dot(v_ref[hh, blk + j], p[j * TM:(j + 1) * TM, :])
        return m_new, acc

    def init():
        return (jnp.full((1, TM), -jnp.inf, F32), jnp.zeros((MLA_V_EXT, TM), F32))

    def write(carries):
        outs = [acc[0:MLA_V, :] / acc[MLA_V:MLA_V + 1, :] for _, acc in carries]
        o_ref[...] = jnp.concatenate(outs, axis=0).T.astype(BF16)

    def step(carries, slot, blk, nb, next_blk, next_nb):
        out = []
        for hh in range(2):
            if next_blk is not None:
                scores(hh, 1 - slot, next_blk, next_nb)
            out.append(absorb(hh, slot, blk, nb, carries[hh]))
        return tuple(out)

    ctx_blk = seq // TM
    ctx_nb = ctx // TM

    @pl.when(i != ctx_tile)
    def _():
        for hh in range(2):
            scores(hh, 0, ctx_blk, ctx_nb)
        carries = step((init(), init()), 0, ctx_blk, ctx_nb, 0, n_blk)
        last_blk = (n_chunks - 1) * n_blk

        def body(c, carries):
            for u in range(unroll):
                blk = (c * unroll + u) * n_blk
                carries = step(carries, (1 + u) % 2, blk, n_blk, jnp.minimum(blk + n_blk, last_blk), n_blk)
            return carries

        if n_chunks == unroll:
            for u in range(n_chunks):
                nxt = (u + 1) * n_blk if u + 1 < n_chunks else None
                carries = step(carries, (1 + u) % 2, u * n_blk, n_blk, nxt, n_blk)
            write(carries)
        else:
            write(lax.fori_loop(0, n_chunks // unroll, body, carries))

    @pl.when(i == ctx_tile)
    def _():
        for hh in range(2):
            scores(hh, 0, ctx_blk, ctx_nb)
        write(step((init(), init()), 0, ctx_blk, ctx_nb, None, None))


def _attention(q, k, v, seq, ctx, n_q_tiles):
    B, H, lt, _ = q.shape
    kern = functools.partial(_attention_kernel, seq=seq, ctx=ctx, ctx_tile=seq // TM)
    return pl.pallas_call(
        kern,
        grid=(B, H // 2, n_q_tiles),
        in_specs=[pl.BlockSpec((None, 2, TM, HEAD_PAD), lambda b, h, i: (b, h, i, 0)),
                  pl.BlockSpec((None, 2, lt, HEAD_PAD), lambda b, h, i: (b, h, 0, 0)),
                  pl.BlockSpec((None, 2, lt // TM, MLA_V_EXT, TM), lambda b, h, i: (b, h, 0, 0, 0))],
        out_specs=pl.BlockSpec((None, TM, HEAD_PAD), lambda b, h, i: (b, i, h)),
        out_shape=jax.ShapeDtypeStruct((B, lt, (H // 2) * HEAD_PAD), BF16),
        scratch_shapes=[pltpu.VMEM((2, 2, ATT_KV_CHUNK, TM), F32)],
        compiler_params=_cparams(3),
        name="attention",
    )(q, k, v)


def _merge_kernel(x_ref, mod_ref, mg_ref, uv_ref, yr_ref, om_ref, lng_ref, lnb_ref, ws_ref, bs_ref,
                  wr_ref, wm_ref, wg_ref, wo_ref, n2_ref, x1_ref, h2_ref):
    yr = _dot(yr_ref[...], wr_ref[...])
    ym = _dot(om_ref[...], wm_ref[...])
    z = jax.nn.gelu(uv_ref[...].astype(F32))
    u = z[:, :GMLP_W]
    v = z[:, GMLP_W:]
    mu = jnp.mean(v, axis=-1, keepdims=True)
    var = jnp.mean(jnp.square(v - mu), axis=-1, keepdims=True)
    vn = ((v - mu) * lax.rsqrt(var + EPS) * lng_ref[...] + lnb_ref[...]).astype(BF16)
    gw = GMLP_W // GMLP_GROUPS
    chunks = []
    for c in range(TM // GMLP_CHUNK):
        rows = slice(c * GMLP_CHUNK, (c + 1) * GMLP_CHUNK)
        groups = [_dot(ws_ref[g], vn[rows, g * gw:(g + 1) * gw]) + bs_ref[g] for g in range(GMLP_GROUPS)]
        chunks.append(jnp.concatenate(groups, axis=1))
    sv = jnp.concatenate(chunks, axis=0)
    yg = _dot((u * sv).astype(BF16), wg_ref[...])
    gate = jax.nn.sigmoid(mg_ref[...].astype(F32))
    y = gate[:, :D] * yr + gate[:, D:2 * D] * ym + gate[:, 2 * D:] * yg
    out = _dot(y.astype(BF16), wo_ref[...])
    x1 = x_ref[...] + mod_ref[:, 2 * D:3 * D] * out
    x1_ref[...] = x1
    h2_ref[...] = _modulated_rmsnorm(x1, n2_ref[...], mod_ref[:, 3 * D:4 * D], mod_ref[:, 4 * D:5 * D])


def _merge(xs, mod3, p, y_ret, o_mla, ln_g, ln_b, ws, bs_full, w_br_ret, w_br_mla, w_br_gmlp, w_out, n2_g,
           n_tiles, tile, mod_row):
    n_rows = xs.shape[0]
    const = lambda shape: pl.BlockSpec(shape, lambda t: (0,) * len(shape))
    row = lambda w, cb=0: pl.BlockSpec((TM, w), lambda t: (tile(t), cb))
    shp = jax.ShapeDtypeStruct((n_rows, D), F32)
    return pl.pallas_call(
        _merge_kernel,
        grid=(n_tiles,),
        in_specs=[row(D),
                  pl.BlockSpec((None, 1, 6 * D), lambda t: (mod_row(tile(t)), 0, 0)),
                  row(3 * D, C_MERGE // (3 * D)), row(D, C_UV // D),
                  row(RET_HEADS * RET_D), row(MLA_HEADS * MLA_V),
                  const((1, GMLP_W)), const((1, GMLP_W)),
                  const((GMLP_GROUPS, GMLP_CHUNK, GMLP_CHUNK)), const((GMLP_GROUPS, GMLP_CHUNK, GMLP_CHUNK)),
                  const((RET_HEADS * RET_D, D)), const((MLA_HEADS * MLA_V, D)), const((GMLP_W, D)),
                  const((D, D)), const((1, D))],
        out_specs=[row(D), row(D)],
        out_shape=[shp, shp],
        compiler_params=_cparams(1),
        name="merge",
    )(xs, mod3, p, p, y_ret, o_mla, ln_g, ln_b, ws, bs_full, w_br_ret, w_br_mla, w_br_gmlp, w_out, n2_g)


def _route_kernel(ha_ref, hb_ref, r_ref, b_ref, sg_ref, su_ref, sd_ref, idx_ref, w_ref, rank_ref, cnt_ref, sh_ref,
                  hp_ref, cnt_scr):
    @pl.when(pl.program_id(0) == 0)
    def _():
        cnt_scr[...] = jnp.zeros_like(cnt_scr)

    h = jnp.concatenate([ha_ref[...], hb_ref[...]], axis=0)
    logits = jnp.dot(h, r_ref[...], preferred_element_type=F32, precision=lax.Precision.HIGHEST)
    scores = jax.nn.sigmoid(logits)
    sel = scores + b_ref[...]
    lane_e = lax.broadcasted_iota(jnp.int32, (RT, N_EXPERTS), 1).astype(F32)
    lane_o = lax.broadcasted_iota(jnp.int32, (RT, 128), 1)
    idx_out = jnp.zeros((RT, 128), F32)
    w_out = jnp.zeros((RT, 128), F32)
    hits = []
    for k in range(TOP_K):
        best = jnp.max(sel, axis=-1, keepdims=True)
        pick = jnp.min(jnp.where(sel == best, lane_e, float(N_EXPERTS)), axis=-1, keepdims=True)
        hit = lane_e == pick
        hits.append(hit)
        wk = jnp.sum(jnp.where(hit, scores, 0.0), axis=-1, keepdims=True)
        sel = jnp.where(hit, -jnp.inf, sel)
        idx_out = jnp.where(lane_o == k, pick, idx_out)
        w_out = jnp.where(lane_o == k, wk, w_out)
    w_out = w_out / jnp.sum(w_out, axis=-1, keepdims=True) * ROUTED_SCALE
    idx_ref[...] = idx_out.astype(jnp.int32)
    w_ref[...] = w_out
    chosen = jnp.zeros((RT, N_EXPERTS), F32)
    for hit in hits:
        chosen = jnp.where(hit, 1.0, chosen)
    earlier = (lax.broadcasted_iota(jnp.int32, (RT, RT), 0) > lax.broadcasted_iota(jnp.int32, (RT, RT), 1))
    before = _dot(jnp.where(earlier, 1.0, 0.0).astype(BF16), chosen.astype(BF16)) + cnt_scr[...]
    rank_out = jnp.zeros((RT, 128), F32)
    for k, hit in enumerate(hits):
        rank_out = jnp.where(lane_o == k, jnp.sum(jnp.where(hit, before, 0.0), axis=-1, keepdims=True), rank_out)
    rank_ref[...] = rank_out.astype(jnp.int32)
    cnt_scr[...] += jnp.sum(chosen, axis=0, keepdims=True)
    cnt_ref[...] = cnt_scr[...]
    hb = h.astype(BF16)
    a = _silu(_dot(hb, sg_ref[...])) * _dot(hb, su_ref[...])
    sh_ref[...] = _dot(a.astype(BF16), sd_ref[...])
    hp_ref[...] = _pack_bf16_pairs(h)


def _route(h2, router, bias, sg, su, sd, n_tiles, tile):
    const = lambda shape: pl.BlockSpec(shape, lambda t: (0,) * len(shape))
    n_act = n_tiles * TM
    assert n_tiles % 2 == 0
    return pl.pallas_call(
        _route_kernel,
        grid=(n_tiles // 2,),
        in_specs=[pl.BlockSpec((TM, D), lambda t: (tile(2 * t), 0)),
                  pl.BlockSpec((TM, D), lambda t: (tile(2 * t + 1), 0)),
                  const((D, N_EXPERTS)), const((1, N_EXPERTS)),
                  const((D, D_EXPERT)), const((D, D_EXPERT)), const((D_EXPERT, D))],
        out_specs=[pl.BlockSpec((RT, 128), lambda t: (t, 0)),
                   pl.BlockSpec((RT, 128), lambda t: (t, 0)),
                   pl.BlockSpec((RT, 128), lambda t: (t, 0)),
                   pl.BlockSpec((1, N_EXPERTS), lambda t: (0, 0)),
                   pl.BlockSpec((RT, D), lambda t: (t, 0)),
                   pl.BlockSpec((RT, D // 2), lambda t: (t, 0))],
        out_shape=[jax.ShapeDtypeStruct((n_act, 128), jnp.int32),
                   jax.ShapeDtypeStruct((n_act, 128), F32),
                   jax.ShapeDtypeStruct((n_act, 128), jnp.int32),
                   jax.ShapeDtypeStruct((1, N_EXPERTS), F32),
                   jax.ShapeDtypeStruct((n_act, D), F32),
                   jax.ShapeDtypeStruct((n_act, D // 2), jnp.uint32)],
        scratch_shapes=[pltpu.VMEM((1, N_EXPERTS), F32)],
        compiler_params=_cparams(1),
        name="route_shared",
    )(h2, h2, router, bias, sg, su, sd)


def _dispatch(pos_sc, hp, n_rows):
    n_batches = pos_sc.shape[0]
    n_workers = SC_CORES * SC_SUBCORES
    mesh = plsc.VectorSubcoreMesh(core_axis_name="c", subcore_axis_name="s")

    @functools.partial(
        pl.kernel, mesh=mesh,
        out_type=jax.ShapeDtypeStruct((n_rows, D // 2), jnp.uint32),
        scratch_types=[pltpu.VMEM((TOP_K, SC_ROWS), jnp.int32),
                       pltpu.VMEM((SC_ROWS, D // 2), jnp.uint32),
                       pltpu.SemaphoreType.DMA],
        name="moe_dispatch")
    def scatter(pos_hbm, h_hbm, xs_hbm, idx_v, rows_v, sem):
        worker = lax.axis_index("s") * SC_CORES + lax.axis_index("c")

        @pl.loop(0, pl.cdiv(n_batches, n_workers))
        def _(j):
            b = j * n_workers + worker

            @pl.when(b < n_batches)
            def _():
                pltpu.sync_copy(pos_hbm.at[b], idx_v)
                pltpu.sync_copy(h_hbm.at[pl.ds(b * SC_ROWS, SC_ROWS)], rows_v)
                copies = [pltpu.async_copy(rows_v, xs_hbm.at[idx_v.at[k]], sem) for k in range(TOP_K)]
                for cp in copies:
                    cp.wait()

    return scatter(pos_sc, hp)


def _expert_kernel(blk_e_ref, valid_ref, x_ref, wg_ref, wu_ref, wd_ref, y_ref, wg_s, wu_s, wd_s):
    i = pl.program_id(0)
    n_valid = valid_ref[i]

    @pl.when(n_valid > 0)
    def _():
        @pl.when(jnp.logical_or(i == 0, blk_e_ref[i] != blk_e_ref[jnp.maximum(i - 1, 0)]))
        def _():
            wg_s[...] = wg_ref[...].astype(BF16)
            wu_s[...] = wu_ref[...].astype(BF16)
            wd_s[...] = wd_ref[...].astype(BF16)

        row = lax.broadcasted_iota(jnp.int32, (MOE_ROWS, 1), 0)
        lo, hi = _unpack_bf16_pairs(jnp.where(row < n_valid, x_ref[...], jnp.uint32(0)))
        x = jnp.concatenate([lo, hi], axis=1).astype(BF16)
        hb = _silu(_dot(x, wg_s[...])) * _dot(x, wu_s[...])
        y_ref[...] = _pack_bf16_pairs(_dot(hb.astype(BF16), wd_s[...]))

    @pl.when(n_valid == 0)
    def _():
        y_ref[...] = jnp.zeros_like(y_ref)


def _experts(blk_e, valid, xs, wg, wu, wd, layer, n_blocks):
    grid_spec = pltpu.PrefetchScalarGridSpec(
        num_scalar_prefetch=2,
        grid=(n_blocks,),
        in_specs=[pl.BlockSpec((MOE_ROWS, D // 2), lambda i, be, nv: (i, 0)),
                  pl.BlockSpec((None, None, D, D_EXPERT), lambda i, be, nv: (layer, be[i], 0, 0)),
                  pl.BlockSpec((None, None, D, D_EXPERT), lambda i, be, nv: (layer, be[i], 0, 0)),
                  pl.BlockSpec((None, None, D_EXPERT, D), lambda i, be, nv: (layer, be[i], 0, 0))],
        out_specs=pl.BlockSpec((MOE_ROWS, D // 2), lambda i, be, nv: (i, 0)),
        scratch_shapes=[pltpu.VMEM((D, D_EXPERT), BF16), pltpu.VMEM((D, D_EXPERT), BF16),
                        pltpu.VMEM((D_EXPERT, D), BF16)],
    )
    return pl.pallas_call(
        _expert_kernel,
        grid_spec=grid_spec,
        out_shape=jax.ShapeDtypeStruct((n_blocks * MOE_ROWS, D // 2), jnp.uint32),
        compiler_params=_cparams(1),
        name="routed_experts",
    )(blk_e, valid, xs, wg, wu, wd)


def _gather_rows(pos_sc, ys):
    n_batches = pos_sc.shape[0]
    n_workers = SC_CORES * SC_SUBCORES
    half = SC_ROWS // 2
    mesh = plsc.VectorSubcoreMesh(core_axis_name="c", subcore_axis_name="s")

    @functools.partial(
        pl.kernel, mesh=mesh,
        out_type=jax.ShapeDtypeStruct((TOP_K, n_batches * SC_ROWS, D // 2), jnp.uint32),
        scratch_types=[pltpu.VMEM((TOP_K, SC_ROWS), jnp.int32),
                       pltpu.VMEM((2, half, D // 2), jnp.uint32),
                       pltpu.SemaphoreType.DMA, pltpu.SemaphoreType.DMA],
        name="moe_gather")
    def gather(pos_hbm, y_hbm, out_hbm, idx_v, bufs, gsem, wsem):
        worker = lax.axis_index("s") * SC_CORES + lax.axis_index("c")

        @pl.loop(0, pl.cdiv(n_batches, n_workers))
        def _(j):
            b = j * n_workers + worker

            @pl.when(b < n_batches)
            def _():
                pltpu.sync_copy(pos_hbm.at[b], idx_v)
                items = [(k, h) for k in range(TOP_K) for h in range(2)]

                def fetch(i):
                    k, h = items[i]
                    return pltpu.async_copy(y_hbm.at[idx_v.at[k, pl.ds(h * half, half)]], bufs.at[i % 2], gsem)

                pending_gather = fetch(0)
                pending_write = None
                for i, (k, h) in enumerate(items):
                    pending_gather.wait()
                    if pending_write is not None:
                        pending_write.wait()
                    if i + 1 < len(items):
                        pending_gather = fetch(i + 1)
                    pending_write = pltpu.async_copy(
                        bufs.at[i % 2], out_hbm.at[k, pl.ds(b * SC_ROWS + h * half, half)], wsem)
                pending_write.wait()

    return gather(pos_sc, ys)


def _combine_kernel(*refs):
    y_refs = refs[:TOP_K]
    w_ref, x1_ref, sh_ref, mod_ref, o_ref = refs[TOP_K:]
    f_lo = sh_ref[:, :D // 2]
    f_hi = sh_ref[:, D // 2:]
    for k in range(TOP_K):
        lo, hi = _unpack_bf16_pairs(y_refs[k][...])
        f_lo = f_lo + lo * w_ref[:, k:k + 1]
        f_hi = f_hi + hi * w_ref[:, k:k + 1]
    o_ref[...] = x1_ref[...] + mod_ref[:, 5 * D:6 * D] * jnp.concatenate([f_lo, f_hi], axis=1)


def _combine(yg, w, x1, shared, mod3, n_tiles, tile, mod_row, out_rows, out_tile):
    planes = [pl.BlockSpec((None, TM, D // 2), lambda t, k=k: (k, t, 0)) for k in range(TOP_K)]
    return pl.pallas_call(
        _combine_kernel,
        grid=(n_tiles,),
        in_specs=planes + [pl.BlockSpec((TM, 128), lambda t: (t, 0)),
                           pl.BlockSpec((TM, D), lambda t: (tile(t), 0)),
                           pl.BlockSpec((TM, D), lambda t: (t, 0)),
                           pl.BlockSpec((None, 1, 6 * D), lambda t: (mod_row(tile(t)), 0, 0))],
        out_specs=pl.BlockSpec((TM, D), lambda t: (out_tile(t), 0)),
        out_shape=jax.ShapeDtypeStruct((out_rows, D), F32),
        compiler_params=_cparams(1),
        name="moe_combine",
    )(*([yg] * TOP_K), w, x1, shared, mod3)


def _moe_plan(idx, rank, counts, n_blocks):
    n = idx.shape[0]
    cnt = counts.reshape(N_EXPERTS).astype(jnp.int32)
    padded = (cnt + MOE_ROWS - 1) // MOE_ROWS * MOE_ROWS
    pad_end = jnp.cumsum(padded)
    pad_start = pad_end - padded
    experts = jnp.arange(N_EXPERTS, dtype=jnp.int32)
    pos = rank + jnp.sum(jnp.where(idx[:, :, None] == experts, pad_start, 0), axis=-1)
    blk_start = jnp.arange(n_blocks, dtype=jnp.int32) * MOE_ROWS
    blk_e = jnp.minimum(jnp.sum(blk_start[:, None] >= pad_end[None, :], axis=1), N_EXPERTS - 1).astype(jnp.int32)
    mine = blk_e[:, None] == experts
    in_expert = blk_start - jnp.sum(jnp.where(mine, pad_start, 0), axis=1)
    valid = jnp.clip(jnp.sum(jnp.where(mine, cnt, 0), axis=1) - in_expert, 0, MOE_ROWS).astype(jnp.int32)
    pos_sc = pos.astype(jnp.int32).reshape(n // SC_ROWS, SC_ROWS, TOP_K).transpose(0, 2, 1)
    return blk_e, valid, pos_sc


def _rope_tables(seq, ctx):
    half = MLA_ROPE // 2
    n_freq = half // 2
    inv = ROPE_THETA ** (-2.0 * jnp.arange(n_freq, dtype=F32) / half)
    t = jnp.arange(seq)
    ang_r = (t // GRID_W).astype(F32)[:, None] * inv
    ang_c = (t % GRID_W).astype(F32)[:, None] * inv
    cos = jnp.concatenate([jnp.cos(ang_r), jnp.cos(ang_r), jnp.cos(ang_c), jnp.cos(ang_c)], axis=1)
    sin = jnp.concatenate([-jnp.sin(ang_r), jnp.sin(ang_r), -jnp.sin(ang_c), jnp.sin(ang_c)], axis=1)
    pad_l = MLA_NOPE
    pad_r = HEAD_PAD - MLA_NOPE - MLA_ROPE
    cos = jnp.pad(cos, ((0, ctx), (pad_l, pad_r)), constant_values=1.0)
    cos = cos.at[seq:, :].set(1.0)
    sin = jnp.pad(sin, ((0, ctx), (pad_l, pad_r)))
    return cos, sin


def _pad_heads(w, n_heads, width, offset=0):
    k = w.shape[0]
    w = w.reshape(k, n_heads, width)
    w = jnp.pad(w, ((0, 0), (0, 0), (offset, HEAD_PAD - width - offset)))
    return w.reshape(k, n_heads * HEAD_PAD)


def _pad_vec(g, offset):
    return jnp.pad(g, (offset, HEAD_PAD - g.shape[0] - offset)).reshape(1, HEAD_PAD)


def _reorder_w_in(w):
    off_cq, off_ckv, off_kr, off_uv, off_merge = 2048, 2432, 2688, 2720, 3744
    kr = jnp.pad(w[:, off_kr:off_uv], ((0, 0), (MLA_NOPE, HEAD_PAD - MLA_NOPE - MLA_ROPE)))
    return jnp.concatenate([w[:, off_merge:], w[:, off_uv:off_merge], w[:, :off_cq],
                            w[:, off_ckv:off_kr], kr, w[:, off_cq:off_ckv]], axis=1).astype(BF16)


def kernel(x, c, ctx, c_ctx, ada_w, ada_b, norm1_g, norm2_g, w_in, ret_decay_fwd, ret_decay_bwd, ret_gn_g,
           ret_gn_b, w_br_ret, mla_qa_g, mla_w_uq, mla_kva_g, mla_w_ukv, mla_qn_g, mla_kn_g, mla_kr_g, w_br_mla,
           gmlp_ln_g, gmlp_ln_b, gmlp_ws, gmlp_bs, w_br_gmlp, w_out, moe_router, moe_bias, moe_w_gate, moe_w_up,
           moe_w_down, sh_w_gate, sh_w_up, sh_w_down):
    B, seq, _ = x.shape
    n_ctx = ctx.shape[1]
    depth = ada_w.shape[0]
    assert n_ctx == TM and seq % (2 * ATT_KV_CHUNK) == 0 and seq % TM == 0
    lt = seq + n_ctx
    tiles_per_b = lt // TM
    lat_tiles_per_b = seq // TM
    ctx_tile = lat_tiles_per_b

    def mod_row(t):
        return jnp.where(t % tiles_per_b == ctx_tile, B, t // tiles_per_b)

    c_rows = jnp.concatenate([c, c_ctx[None, :], jnp.zeros((8 - B - 1, D), F32)], axis=0)
    mod = _ada(c_rows, ada_w, ada_b)
    cos_t, sin_t = _rope_tables(seq, n_ctx)
    xs = jnp.concatenate([x, ctx], axis=1).reshape(B * lt, D)

    for l in range(depth):
        last = l == depth - 1
        mod3 = mod[l].reshape(8, 1, 6 * D)
        p = _in_proj(xs, mod3, norm1_g[l].reshape(1, D), _reorder_w_in(w_in[l]), B * tiles_per_b, mod_row)

        lg = jnp.stack([jax.nn.log_sigmoid(ret_decay_fwd[l].astype(F32)),
                        jax.nn.log_sigmoid(ret_decay_bwd[l].astype(F32))])
        y_ret = _retention(p.reshape(B, lt, N_IN_PAD), lg, ret_gn_g[l].reshape(1, -1), ret_gn_b[l].reshape(1, -1),
                           seq, n_ctx)

        w_ukv = mla_w_ukv[l].reshape(MLA_KV_LORA, MLA_HEADS, MLA_NOPE + MLA_V)
        wk_p = _pad_heads(w_ukv[:, :, :MLA_NOPE].reshape(MLA_KV_LORA, -1), MLA_HEADS, MLA_NOPE).astype(BF16)
        wv = w_ukv[:, :, MLA_NOPE:].reshape(MLA_KV_LORA, MLA_HEADS * MLA_V).astype(BF16)
        wq_p = _pad_heads(mla_w_uq[l], MLA_HEADS, MLA_QK).astype(BF16)
        q, k, v = _mla_prep(p, cos_t, sin_t, mla_qa_g[l].reshape(1, -1), mla_kva_g[l].reshape(1, -1),
                            _pad_vec(mla_qn_g[l], 0), _pad_vec(mla_kn_g[l], 0), _pad_vec(mla_kr_g[l], MLA_NOPE),
                            wq_p, wk_p, wv, B, lt)
        o_mla = _attention(q, k, v, seq, n_ctx, lat_tiles_per_b if last else tiles_per_b)

        if last:
            n_tiles = B * lat_tiles_per_b
            tile = lambda t: (t // lat_tiles_per_b) * tiles_per_b + t % lat_tiles_per_b
        else:
            n_tiles = B * tiles_per_b
            tile = lambda t: t
        bs_full = jnp.broadcast_to(gmlp_bs[l][:, :, None], (GMLP_GROUPS, GMLP_CHUNK, GMLP_CHUNK))
        x1, h2 = _merge(xs, mod3, p, y_ret.reshape(B * lt, -1), o_mla.reshape(B * lt, -1),
                        gmlp_ln_g[l].reshape(1, -1), gmlp_ln_b[l].reshape(1, -1), gmlp_ws[l].astype(BF16), bs_full,
                        w_br_ret[l].astype(BF16), w_br_mla[l].astype(BF16), w_br_gmlp[l].astype(BF16),
                        w_out[l].astype(BF16), norm2_g[l].reshape(1, D), n_tiles, tile, mod_row)

        idx, w, rank, counts, shared, hp = _route(h2, moe_router[l], moe_bias[l].reshape(1, -1),
                                                  sh_w_gate[l].astype(BF16), sh_w_up[l].astype(BF16),
                                                  sh_w_down[l].astype(BF16), n_tiles, tile)
        n_act = n_tiles * TM
        n_blocks = -(-(n_act * TOP_K + N_EXPERTS * (MOE_ROWS - 1)) // MOE_ROWS)
        blk_e, valid, pos_sc = _moe_plan(idx[:, :TOP_K], rank[:, :TOP_K], counts, n_blocks)
        xg = _dispatch(pos_sc, hp, n_blocks * MOE_ROWS)
        ys = _experts(blk_e, valid, xg, moe_w_gate, moe_w_up, moe_w_down, l, n_blocks)
        yg = _gather_rows(pos_sc, ys)
        if last:
            xs = _combine(yg, w, x1, shared, mod3, n_tiles, tile, mod_row, B * seq, lambda t: t)
        else:
            xs = _combine(yg, w, x1, shared, mod3, n_tiles, tile, mod_row, B * lt, tile)
    return xs.reshape(B, seq, D)
```

```python
import functools
import math

import jax
import jax.numpy as jnp
from jax import lax
from jax.experimental import pallas as pl
from jax.experimental.pallas import tpu as pltpu
from jax.experimental.pallas import tpu_sc as plsc

F32 = jnp.float32
BF16 = jnp.bfloat16

D = 1024
GRID_W = 64
RET_HEADS = 4
RET_D = 128
RET_CHUNK = 256
RET_OUT_ROWS = 256
MLA_HEADS = 8
MLA_Q_LORA = 384
MLA_KV_LORA = 256
MLA_NOPE = 64
MLA_ROPE = 32
MLA_V = 64
MLA_V_EXT = MLA_V + 16
MLA_QK = MLA_NOPE + MLA_ROPE
HEAD_PAD = 128
ROPE_THETA = 10000.0
GMLP_GROUPS = 4
GMLP_W = 512
GMLP_CHUNK = 128
N_EXPERTS = 64
TOP_K = 6
D_EXPERT = 256
ROUTED_SCALE = 2.5
EPS = 1e-6
LOG2_E = 1.4426950408889634

TM = 256
MOE_ROWS = 512
RT = 2 * TM
ATT_KV_CHUNK = 1024
ATT_UNROLL = 16

C_MERGE = 0
C_UV = 3072
C_RET = 4096
C_CKV = 6144
C_KR = 6400
C_CQ = 6528
N_IN_PAD = 6912
IN_CHUNK = 768

VMEM_LIMIT = 56 * 1024 * 1024

SC_CORES = 2
SC_SUBCORES = 16
SC_ROWS = 128


def _cparams(n_axes, vmem=VMEM_LIMIT):
    return pltpu.CompilerParams(dimension_semantics=("arbitrary",) * n_axes, vmem_limit_bytes=vmem)


def _silu(x):
    return x * jax.nn.sigmoid(x)


def _dot(a, b):
    return jnp.dot(a, b, preferred_element_type=F32)


def _dot_nt(a, b):
    return lax.dot_general(a, b, (((1,), (1,)), ((), ())), preferred_element_type=F32)


def _dot_tn(a, b):
    return lax.dot_general(a, b, (((0,), (0,)), ((), ())), preferred_element_type=F32)


def _pack_bf16_pairs(x):
    n = x.shape[1] // 2
    lo = lax.bitcast_convert_type(x[:, :n].astype(BF16).astype(F32), jnp.uint32)
    hi = lax.bitcast_convert_type(x[:, n:].astype(BF16).astype(F32), jnp.uint32)
    return (lo >> 16) | hi


def _unpack_bf16_pairs(u):
    lo = lax.bitcast_convert_type(u << 16, F32)
    hi = lax.bitcast_convert_type(u & jnp.uint32(0xFFFF0000), F32)
    return lo, hi


def _ada_kernel(c_ref, w_ref, b_ref, o_ref):
    s = _silu(c_ref[...])
    o_ref[...] = _dot(s.astype(BF16), w_ref[...].astype(BF16)) + b_ref[...]


def _ada(c_rows, ada_w, ada_b):
    depth = ada_w.shape[0]
    n = ada_w.shape[2]
    cw = 1536
    return pl.pallas_call(
        _ada_kernel,
        grid=(depth, n // cw),
        in_specs=[pl.BlockSpec((8, D), lambda l, j: (0, 0)),
                  pl.BlockSpec((None, D, cw), lambda l, j: (l, 0, j)),
                  pl.BlockSpec((None, 1, cw), lambda l, j: (l, 0, j))],
        out_specs=pl.BlockSpec((None, 8, cw), lambda l, j: (l, 0, j)),
        out_shape=jax.ShapeDtypeStruct((depth, 8, n), F32),
        compiler_params=_cparams(2),
        name="ada_mod",
    )(c_rows, ada_w, ada_b.reshape(depth, 1, n))


def _modulated_rmsnorm(x, g, shift, scale):
    y = x * lax.rsqrt(jnp.mean(x * x, axis=-1, keepdims=True) + EPS) * g
    return y * (1.0 + scale) + shift


def _in_proj_kernel(x_ref, mod_ref, g_ref, w_ref, o_ref, h_scr):
    h = _modulated_rmsnorm(x_ref[...], g_ref[...], mod_ref[:, 0:D], mod_ref[:, D:2 * D])
    h_scr[...] = h.astype(BF16)
    for c in range(N_IN_PAD // IN_CHUNK):
        cols = slice(c * IN_CHUNK, (c + 1) * IN_CHUNK)
        o_ref[:, cols] = _dot(h_scr[...], w_ref[:, cols]).astype(BF16)


def _in_proj(xs, mod3, g, w_in_r, n_tiles, mod_row):
    n_rows = xs.shape[0]
    return pl.pallas_call(
        _in_proj_kernel,
        grid=(n_tiles,),
        in_specs=[pl.BlockSpec((TM, D), lambda t: (t, 0)),
                  pl.BlockSpec((None, 1, 6 * D), lambda t: (mod_row(t), 0, 0)),
                  pl.BlockSpec((1, D), lambda t: (0, 0)),
                  pl.BlockSpec((D, N_IN_PAD), lambda t: (0, 0))],
        out_specs=pl.BlockSpec((TM, N_IN_PAD), lambda t: (t, 0)),
        out_shape=jax.ShapeDtypeStruct((n_rows, N_IN_PAD), BF16),
        scratch_shapes=[pltpu.VMEM((TM, D), BF16)],
        compiler_params=_cparams(1),
        name="in_proj",
    )(xs, mod3, g, w_in_r)


def _retention_kernel(lg_ref, q_ref, k_ref, v_ref, g_ref, gng_ref, gnb_ref, y_ref, of_scr, ob_scr,
                      *, n_lat_chunks, n_ctx_chunks):
    h = pl.program_id(1)
    lg_f = lg_ref[0, h]
    lg_b = lg_ref[1, h]
    C = RET_CHUNK
    k_scale = RET_D ** -0.5
    ri = lax.broadcasted_iota(jnp.int32, (C, C), 0).astype(F32)
    ci = lax.broadcasted_iota(jnp.int32, (C, C), 1).astype(F32)
    pos = lax.broadcasted_iota(jnp.int32, (C, 1), 0).astype(F32)
    diff = ri - ci
    d_f = jnp.where(diff >= 0, jnp.exp(lg_f * jnp.maximum(diff, 0.0)), 0.0) * k_scale
    d_b = jnp.where(diff < 0, jnp.exp(lg_b * jnp.maximum(-diff, 0.0)), 0.0) * k_scale
    qdec_f = jnp.exp(lg_f * (pos + 1.0))
    kdec_f = jnp.exp(lg_f * (C - 1.0 - pos)) * k_scale
    cdec_f = jnp.exp(lg_f * C)
    qdec_b = jnp.exp(lg_b * (C - pos))
    kdec_b = jnp.exp(lg_b * pos) * k_scale
    cdec_b = jnp.exp(lg_b * C)

    def chunk(c, state, dmat, qdec, kdec, cdec):
        rows = pl.ds(pl.multiple_of(c * C, C), C)
        q = q_ref[rows, :]
        k = k_ref[rows, :]
        v = v_ref[rows, :]
        att = (_dot_nt(q, k) * dmat).astype(BF16)
        o = _dot(att, v) + _dot((q.astype(F32) * qdec).astype(BF16), state.astype(BF16))
        kd = (k.astype(F32) * kdec).astype(BF16)
        return rows, o, state * cdec + _dot_tn(kd, v)

    n_all = n_lat_chunks + n_ctx_chunks

    def scan_body(i, states):
        s_f, s_b = states
        c_f = jnp.where(i < n_ctx_chunks, n_lat_chunks + i, i - n_ctx_chunks)
        rows, o, s_f = chunk(c_f, s_f, d_f, qdec_f, kdec_f, cdec_f)
        of_scr[rows, :] = o
        rows, o, s_b = chunk(n_all - 1 - i, s_b, d_b, qdec_b, kdec_b, cdec_b)
        ob_scr[rows, :] = o
        return s_f, s_b

    zero = jnp.zeros((RET_D, RET_D), F32)
    lax.fori_loop(0, n_all, scan_body, (zero, zero), unroll=2)

    def out_body(c, _):
        rows = pl.ds(pl.multiple_of(c * RET_OUT_ROWS, RET_OUT_ROWS), RET_OUT_ROWS)
        o = of_scr[rows, :] + ob_scr[rows, :]
        mu = jnp.mean(o, axis=-1, keepdims=True)
        var = jnp.mean(jnp.square(o - mu), axis=-1, keepdims=True)
        on = (o - mu) * lax.rsqrt(var + EPS)
        y = _silu(g_ref[rows, :].astype(F32)) * (on * gng_ref[...] + gnb_ref[...])
        y_ref[rows, :] = y.astype(BF16)
        return 0

    lax.fori_loop(0, n_all * C // RET_OUT_ROWS, out_body, 0)


def _retention(p3, lg, gn_g, gn_b, seq, ctx):
    B, lt, _ = p3.shape
    base = C_RET // RET_D
    kern = functools.partial(_retention_kernel, n_lat_chunks=seq // RET_CHUNK, n_ctx_chunks=ctx // RET_CHUNK)

    def col(off):
        return pl.BlockSpec((None, lt, RET_D), lambda b, h: (b, 0, base + off * RET_HEADS + h))

    return pl.pallas_call(
        kern,
        grid=(B, RET_HEADS),
        in_specs=[pl.BlockSpec(memory_space=pltpu.SMEM),
                  col(0), col(1), col(2), col(3),
                  pl.BlockSpec((1, RET_D), lambda b, h: (0, h)),
                  pl.BlockSpec((1, RET_D), lambda b, h: (0, h))],
        out_specs=pl.BlockSpec((None, lt, RET_D), lambda b, h: (b, 0, h)),
        out_shape=jax.ShapeDtypeStruct((B, lt, RET_HEADS * RET_D), BF16),
        scratch_shapes=[pltpu.VMEM((lt, RET_D), F32), pltpu.VMEM((lt, RET_D), F32)],
        compiler_params=_cparams(2),
        name="retention",
    )(lg, p3, p3, p3, p3, gn_g, gn_b)


def _rope_rotate(x, first_half):
    return jnp.where(first_half, pltpu.roll(x, HEAD_PAD - 8, 1), pltpu.roll(x, 8, 1))


def _mla_prep_kernel(cq_ref, ckv_ref, kr_ref, cos_ref, sin_ref, qa_ref, kva_ref, qn_ref, kn_ref, krg_ref,
                     wq_ref, wk_ref, wv_ref, q_ref, k_ref, v_ref):
    lane = lax.broadcasted_iota(jnp.int32, (1, HEAD_PAD), 1)
    first_half = (lane % 16) < 8
    cos = cos_ref[...]
    sin = sin_ref[...]

    def rms(x, n):
        return x * lax.rsqrt(jnp.sum(x * x, axis=-1, keepdims=True) * (1.0 / n) + EPS)

    def rope(x):
        return x * cos + _rope_rotate(x, first_half) * sin

    cq = cq_ref[...].astype(F32)
    cqn = (rms(cq, MLA_Q_LORA) * qa_ref[...]).astype(BF16)
    q_all = _dot(cqn, wq_ref[...])
    ckv = ckv_ref[...].astype(F32)
    ckvn = (rms(ckv, MLA_KV_LORA) * kva_ref[...]).astype(BF16)
    k_all = _dot(ckvn, wk_ref[...])
    v_all = _dot(ckvn, wv_ref[...])
    k_rope = rope(rms(kr_ref[...].astype(F32), MLA_ROPE) * krg_ref[...])
    scale = MLA_QK ** -0.5 * LOG2_E
    v_t = v_all.T
    ones_row = jnp.where(lax.broadcasted_iota(jnp.int32, (MLA_V_EXT - MLA_V, TM), 0) == 0, 1.0, 0.0)
    for h in range(MLA_HEADS):
        cols = slice(h * HEAD_PAD, (h + 1) * HEAD_PAD)
        qh = rope(rms(q_all[:, cols], MLA_QK) * qn_ref[...]) * scale
        q_ref[h] = qh.astype(BF16)
        kh = rms(k_all[:, cols], MLA_NOPE) * kn_ref[...] + k_rope
        k_ref[h] = kh.astype(BF16)
        v_ref[h] = jnp.concatenate([v_t[h * MLA_V:(h + 1) * MLA_V, :], ones_row], axis=0).astype(BF16)


def _mla_prep(p, cos_t, sin_t, qa_g, kva_g, qn_p, kn_p, kr_p, wq_p, wk_p, wv, B, lt):
    tiles_per_b = lt // TM
    hw = MLA_HEADS * HEAD_PAD
    const = lambda shape: pl.BlockSpec(shape, lambda b, j: (0,) * len(shape))
    head_out = pl.BlockSpec((None, MLA_HEADS, TM, HEAD_PAD), lambda b, j: (b, 0, j, 0))
    shp = jax.ShapeDtypeStruct((B, MLA_HEADS, lt, HEAD_PAD), BF16)
    v_out = pl.BlockSpec((None, MLA_HEADS, None, MLA_V_EXT, TM), lambda b, j: (b, 0, j, 0, 0))
    v_shp = jax.ShapeDtypeStruct((B, MLA_HEADS, tiles_per_b, MLA_V_EXT, TM), BF16)
    return pl.pallas_call(
        _mla_prep_kernel,
        grid=(B, tiles_per_b),
        in_specs=[pl.BlockSpec((TM, MLA_Q_LORA), lambda b, j: (b * tiles_per_b + j, C_CQ // MLA_Q_LORA)),
                  pl.BlockSpec((TM, MLA_KV_LORA), lambda b, j: (b * tiles_per_b + j, C_CKV // MLA_KV_LORA)),
                  pl.BlockSpec((TM, HEAD_PAD), lambda b, j: (b * tiles_per_b + j, C_KR // HEAD_PAD)),
                  pl.BlockSpec((TM, HEAD_PAD), lambda b, j: (j, 0)),
                  pl.BlockSpec((TM, HEAD_PAD), lambda b, j: (j, 0)),
                  const((1, MLA_Q_LORA)), const((1, MLA_KV_LORA)),
                  const((1, HEAD_PAD)), const((1, HEAD_PAD)), const((1, HEAD_PAD)),
                  const((MLA_Q_LORA, hw)), const((MLA_KV_LORA, hw)), const((MLA_KV_LORA, MLA_HEADS * MLA_V))],
        out_specs=[head_out, head_out, v_out],
        out_shape=[shp, shp, v_shp],
        compiler_params=_cparams(2),
        name="mla_prep",
    )(p, p, p, cos_t, sin_t, qa_g, kva_g, qn_p, kn_p, kr_p, wq_p, wk_p, wv)


def _attention_kernel(q_ref, k_ref, v_ref, o_ref, s_scr, *, seq, ctx, ctx_tile):
    i = pl.program_id(2)
    n_blk = ATT_KV_CHUNK // TM
    n_chunks = seq // ATT_KV_CHUNK
    unroll = math.gcd(n_chunks, ATT_UNROLL)

    def scores(hh, slot, blk, nb):
        start = blk * TM if isinstance(blk, int) else pl.multiple_of(blk * TM, TM)
        s_scr[hh, slot, 0:nb * TM, :] = _dot_nt(k_ref[hh, pl.ds(start, nb * TM), :], q_ref[hh])

    def absorb(hh, slot, blk, nb, carry):
        m, acc = carry
        s = s_scr[hh, slot, 0:nb * TM, :]
        m_new = jnp.maximum(m, jnp.max(s, axis=0, keepdims=True))
        p = jnp.exp2(s - m_new).astype(BF16)
        acc = jnp.exp2(m - m_new) * acc
        for j in range(nb):
            acc = acc + _dot(v_ref[hh, blk + j], p[j * TM:(j + 1) * TM, :])
        return m_new, acc

    def init():
        return (jnp.full((1, TM), -jnp.inf, F32), jnp.zeros((MLA_V_EXT, TM), F32))

    def write(carries):
        outs = [acc[0:MLA_V, :] / acc[MLA_V:MLA_V + 1, :] for _, acc in carries]
        o_ref[...] = jnp.concatenate(outs, axis=0).T.astype(BF16)

    def step(carries, slot, blk, nb, next_blk, next_nb):
        out = []
        for hh in range(2):
            if next_blk is not None:
                scores(hh, 1 - slot, next_blk, next_nb)
            out.append(absorb(hh, slot, blk, nb, carries[hh]))
        return tuple(out)

    ctx_blk = seq // TM
    ctx_nb = ctx // TM

    @pl.when(i != ctx_tile)
    def _():
        for hh in range(2):
            scores(hh, 0, ctx_blk, ctx_nb)
        carries = step((init(), init()), 0, ctx_blk, ctx_nb, 0, n_blk)
        last_blk = (n_chunks - 1) * n_blk

        def body(c, carries):
            for u in range(unroll):
                blk = (c * unroll + u) * n_blk
                carries = step(carries, (1 + u) % 2, blk, n_blk, jnp.minimum(blk + n_blk, last_blk), n_blk)
            return carries

        if n_chunks == unroll:
            for u in range(n_chunks):
                nxt = (u + 1) * n_blk if u + 1 < n_chunks else None
                carries = step(carries, (1 + u) % 2, u * n_blk, n_blk, nxt, n_blk)
            write(carries)
        else:
            write(lax.fori_loop(0, n_chunks // unroll, body, carries))

    @pl.when(i == ctx_tile)
    def _():
        for hh in range(2):
            scores(hh, 0, ctx_blk, ctx_nb)
        write(step((init(), init()), 0, ctx_blk, ctx_nb, None, None))


def _attention(q, k, v, seq, ctx, n_q_tiles):
    B, H, lt, _ = q.shape
    kern = functools.partial(_attention_kernel, seq=seq, ctx=ctx, ctx_tile=seq // TM)
    return pl.pallas_call(
        kern,
        grid=(B, H // 2, n_q_tiles),
        in_specs=[pl.BlockSpec((None, 2, TM, HEAD_PAD), lambda b, h, i: (b, h, i, 0)),
                  pl.BlockSpec((None, 2, lt, HEAD_PAD), lambda b, h, i: (b, h, 0, 0)),
                  pl.BlockSpec((None, 2, lt // TM, MLA_V_EXT, TM), lambda b, h, i: (b, h, 0, 0, 0))],
        out_specs=pl.BlockSpec((None, TM, HEAD_PAD), lambda b, h, i: (b, i, h)),
        out_shape=jax.ShapeDtypeStruct((B, lt, (H // 2) * HEAD_PAD), BF16),
        scratch_shapes=[pltpu.VMEM((2, 2, ATT_KV_CHUNK, TM), F32)],
        compiler_params=_cparams(3),
        name="attention",
    )(q, k, v)


def _merge_kernel(x_ref, mod_ref, mg_ref, uv_ref, yr_ref, om_ref, lng_ref, lnb_ref, ws_ref, bs_ref,
                  wr_ref, wm_ref, wg_ref, wo_ref, n2_ref, x1_ref, h2_ref):
    yr = _dot(yr_ref[...], wr_ref[...])
    ym = _dot(om_ref[...], wm_ref[...])
    z = jax.nn.gelu(uv_ref[...].astype(F32))
    u = z[:, :GMLP_W]
    v = z[:, GMLP_W:]
    mu = jnp.mean(v, axis=-1, keepdims=True)
    var = jnp.mean(jnp.square(v - mu), axis=-1, keepdims=True)
    vn = ((v - mu) * lax.rsqrt(var + EPS) * lng_ref[...] + lnb_ref[...]).astype(BF16)
    gw = GMLP_W // GMLP_GROUPS
    chunks = []
    for c in range(TM // GMLP_CHUNK):
        rows = slice(c * GMLP_CHUNK, (c + 1) * GMLP_CHUNK)
        groups = [_dot(ws_ref[g], vn[rows, g * gw:(g + 1) * gw]) + bs_ref[g] for g in range(GMLP_GROUPS)]
        chunks.append(jnp.concatenate(groups, axis=1))
    sv = jnp.concatenate(chunks, axis=0)
    yg = _dot((u * sv).astype(BF16), wg_ref[...])
    gate = jax.nn.sigmoid(mg_ref[...].astype(F32))
    y = gate[:, :D] * yr + gate[:, D:2 * D] * ym + gate[:, 2 * D:] * yg
    out = _dot(y.astype(BF16), wo_ref[...])
    x1 = x_ref[...] + mod_ref[:, 2 * D:3 * D] * out
    x1_ref[...] = x1
    h2_ref[...] = _modulated_rmsnorm(x1, n2_ref[...], mod_ref[:, 3 * D:4 * D], mod_ref[:, 4 * D:5 * D])


def _merge(xs, mod3, p, y_ret, o_mla, ln_g, ln_b, ws, bs_full, w_br_ret, w_br_mla, w_br_gmlp, w_out, n2_g,
           n_tiles, tile, mod_row):
    n_rows = xs.shape[0]
    const = lambda shape: pl.BlockSpec(shape, lambda t: (0,) * len(shape))
    row = lambda w, cb=0: pl.BlockSpec((TM, w), lambda t: (tile(t), cb))
    shp = jax.ShapeDtypeStruct((n_rows, D), F32)
    return pl.pallas_call(
        _merge_kernel,
        grid=(n_tiles,),
        in_specs=[row(D),
                  pl.BlockSpec((None, 1, 6 * D), lambda t: (mod_row(tile(t)), 0, 0)),
                  row(3 * D, C_MERGE // (3 * D)), row(D, C_UV // D),
                  row(RET_HEADS * RET_D), row(MLA_HEADS * MLA_V),
                  const((1, GMLP_W)), const((1, GMLP_W)),
                  const((GMLP_GROUPS, GMLP_CHUNK, GMLP_CHUNK)), const((GMLP_GROUPS, GMLP_CHUNK, GMLP_CHUNK)),
                  const((RET_HEADS * RET_D, D)), const((MLA_HEADS * MLA_V, D)), const((GMLP_W, D)),
                  const((D, D)), const((1, D))],
        out_specs=[row(D), row(D)],
        out_shape=[shp, shp],
        compiler_params=_cparams(1),
        name="merge",
    )(xs, mod3, p, p, y_ret, o_mla, ln_g, ln_b, ws, bs_full, w_br_ret, w_br_mla, w_br_gmlp, w_out, n2_g)


def _route_kernel(ha_ref, hb_ref, rt_ref, bt_ref, sg_ref, su_ref, sd_ref, idx_ref, w_ref, rank_ref, cnt_ref, sh_ref,
                  hp_ref, cnt_scr):
    @pl.when(pl.program_id(0) == 0)
    def _():
        cnt_scr[...] = jnp.zeros_like(cnt_scr)

    h = jnp.concatenate([ha_ref[...], hb_ref[...]], axis=0)
    logits = lax.dot_general(rt_ref[...], h, (((1,), (1,)), ((), ())), preferred_element_type=F32,
                             precision=lax.Precision.HIGHEST)
    scores = jax.nn.sigmoid(logits)
    sel = scores + bt_ref[:, 0:1]
    row_e = lax.broadcasted_iota(jnp.int32, (N_EXPERTS, RT), 0).astype(F32)
    row_o = lax.broadcasted_iota(jnp.int32, (8, RT), 0)
    idx_out = jnp.zeros((8, RT), F32)
    w_out = jnp.zeros((8, RT), F32)
    hits = []
    for k in range(TOP_K):
        best = jnp.max(sel, axis=0, keepdims=True)
        pick = jnp.min(jnp.where(sel == best, row_e, float(N_EXPERTS)), axis=0, keepdims=True)
        hit = row_e == pick
        hits.append(hit)
        wk = jnp.sum(jnp.where(hit, scores, 0.0), axis=0, keepdims=True)
        sel = jnp.where(hit, -jnp.inf, sel)
        idx_out = jnp.where(row_o == k, pick, idx_out)
        w_out = jnp.where(row_o == k, wk, w_out)
    w_out = w_out / jnp.sum(w_out, axis=0, keepdims=True) * ROUTED_SCALE
    idx_ref[...] = idx_out.astype(jnp.int32)
    w_ref[...] = w_out
    chosen = jnp.zeros((N_EXPERTS, RT), F32)
    for hit in hits:
        chosen = jnp.where(hit, 1.0, chosen)
    earlier = (lax.broadcasted_iota(jnp.int32, (RT, RT), 0) < lax.broadcasted_iota(jnp.int32, (RT, RT), 1))
    before = _dot(chosen.astype(BF16), jnp.where(earlier, 1.0, 0.0).astype(BF16)) + cnt_scr[:, 0:1]
    rank_out = jnp.zeros((8, RT), F32)
    for k, hit in enumerate(hits):
        rank_out = jnp.where(row_o == k, jnp.sum(jnp.where(hit, before, 0.0), axis=0, keepdims=True), rank_out)
    rank_ref[...] = rank_out.astype(jnp.int32)
    cnt_scr[...] += jnp.sum(chosen, axis=1, keepdims=True)
    cnt_ref[...] = cnt_scr[...]
    hb = h.astype(BF16)
    a = _silu(_dot(hb, sg_ref[...])) * _dot(hb, su_ref[...])
    sh_ref[...] = _dot(a.astype(BF16), sd_ref[...])
    hp_ref[...] = _pack_bf16_pairs(h)


def _route(h2, router_t, bias_t, sg, su, sd, n_tiles, tile):
    const = lambda shape: pl.BlockSpec(shape, lambda t: (0,) * len(shape))
    n_act = n_tiles * TM
    assert n_tiles % 2 == 0
    n_steps = n_tiles // 2
    per_tok = pl.BlockSpec((None, 8, RT), lambda t: (t, 0, 0))
    idx_t, w_t, rank_t, counts, shared, hp = pl.pallas_call(
        _route_kernel,
        grid=(n_steps,),
        in_specs=[pl.BlockSpec((TM, D), lambda t: (tile(2 * t), 0)),
                  pl.BlockSpec((TM, D), lambda t: (tile(2 * t + 1), 0)),
                  const((N_EXPERTS, D)), const((N_EXPERTS, 128)),
                  const((D, D_EXPERT)), const((D, D_EXPERT)), const((D_EXPERT, D))],
        out_specs=[per_tok, per_tok, per_tok,
                   pl.BlockSpec((N_EXPERTS, 128), lambda t: (0, 0)),
                   pl.BlockSpec((RT, D), lambda t: (t, 0)),
                   pl.BlockSpec((RT, D // 2), lambda t: (t, 0))],
        out_shape=[jax.ShapeDtypeStruct((n_steps, 8, RT), jnp.int32),
                   jax.ShapeDtypeStruct((n_steps, 8, RT), F32),
                   jax.ShapeDtypeStruct((n_steps, 8, RT), jnp.int32),
                   jax.ShapeDtypeStruct((N_EXPERTS, 128), F32),
                   jax.ShapeDtypeStruct((n_act, D), F32),
                   jax.ShapeDtypeStruct((n_act, D // 2), jnp.uint32)],
        scratch_shapes=[pltpu.VMEM((N_EXPERTS, 128), F32)],
        compiler_params=_cparams(1),
        name="route_shared",
    )(h2, h2, router_t, bias_t, sg, su, sd)
    token_major = lambda a: a.transpose(0, 2, 1).reshape(n_act, 8)[:, :TOP_K]
    return token_major(idx_t), token_major(w_t), token_major(rank_t), counts[:, 0], shared, hp


def _dispatch(pos_sc, hp, n_rows):
    n_batches = pos_sc.shape[0]
    n_workers = SC_CORES * SC_SUBCORES
    mesh = plsc.VectorSubcoreMesh(core_axis_name="c", subcore_axis_name="s")

    @functools.partial(
        pl.kernel, mesh=mesh,
        out_type=jax.ShapeDtypeStruct((n_rows, D // 2), jnp.uint32),
        scratch_types=[pltpu.VMEM((TOP_K, SC_ROWS), jnp.int32),
                       pltpu.VMEM((SC_ROWS, D // 2), jnp.uint32),
                       pltpu.SemaphoreType.DMA],
        name="moe_dispatch")
    def scatter(pos_hbm, h_hbm, xs_hbm, idx_v, rows_v, sem):
        worker = lax.axis_index("s") * SC_CORES + lax.axis_index("c")

        @pl.loop(0, pl.cdiv(n_batches, n_workers))
        def _(j):
            b = j * n_workers + worker

            @pl.when(b < n_batches)
            def _():
                pltpu.sync_copy(pos_hbm.at[b], idx_v)
                pltpu.sync_copy(h_hbm.at[pl.ds(b * SC_ROWS, SC_ROWS)], rows_v)
                copies = [pltpu.async_copy(rows_v, xs_hbm.at[idx_v.at[k]], sem) for k in range(TOP_K)]
                for cp in copies:
                    cp.wait()

    return scatter(pos_sc, hp)


def _expert_kernel(blk_e_ref, valid_ref, x_ref, wg_ref, wu_ref, wd_ref, y_ref, wg_s, wu_s, wd_s):
    i = pl.program_id(0)
    n_valid = valid_ref[i]

    @pl.when(n_valid > 0)
    def _():
        @pl.when(jnp.logical_or(i == 0, blk_e_ref[i] != blk_e_ref[jnp.maximum(i - 1, 0)]))
        def _():
            wg_s[...] = wg_ref[...].astype(BF16)
            wu_s[...] = wu_ref[...].astype(BF16)
            wd_s[...] = wd_ref[...].astype(BF16)

        row = lax.broadcasted_iota(jnp.int32, (MOE_ROWS, 1), 0)
        lo, hi = _unpack_bf16_pairs(jnp.where(row < n_valid, x_ref[...], jnp.uint32(0)))
        x = jnp.concatenate([lo, hi], axis=1).astype(BF16)
        hb = _silu(_dot(x, wg_s[...])) * _dot(x, wu_s[...])
        y_ref[...] = _pack_bf16_pairs(_dot(hb.astype(BF16), wd_s[...]))

    @pl.when(n_valid == 0)
    def _():
        y_ref[...] = jnp.zeros_like(y_ref)


def _experts(blk_e, valid, xs, wg, wu, wd, layer, n_blocks):
    grid_spec = pltpu.PrefetchScalarGridSpec(
        num_scalar_prefetch=2,
        grid=(n_blocks,),
        in_specs=[pl.BlockSpec((MOE_ROWS, D // 2), lambda i, be, nv: (i, 0)),
                  pl.BlockSpec((None, None, D, D_EXPERT), lambda i, be, nv: (layer, be[i], 0, 0)),
                  pl.BlockSpec((None, None, D, D_EXPERT), lambda i, be, nv: (layer, be[i], 0, 0)),
                  pl.BlockSpec((None, None, D_EXPERT, D), lambda i, be, nv: (layer, be[i], 0, 0))],
        out_specs=pl.BlockSpec((MOE_ROWS, D // 2), lambda i, be, nv: (i, 0)),
        scratch_shapes=[pltpu.VMEM((D, D_EXPERT), BF16), pltpu.VMEM((D, D_EXPERT), BF16),
                        pltpu.VMEM((D_EXPERT, D), BF16)],
    )
    return pl.pallas_call(
        _expert_kernel,
        grid_spec=grid_spec,
        out_shape=jax.ShapeDtypeStruct((n_blocks * MOE_ROWS, D // 2), jnp.uint32),
        compiler_params=_cparams(1),
        name="routed_experts",
    )(blk_e, valid, xs, wg, wu, wd)


def _gather_rows(pos_sc, ys):
    n_batches = pos_sc.shape[0]
    n_workers = SC_CORES * SC_SUBCORES
    half = SC_ROWS // 2
    mesh = plsc.VectorSubcoreMesh(core_axis_name="c", subcore_axis_name="s")

    @functools.partial(
        pl.kernel, mesh=mesh,
        out_type=jax.ShapeDtypeStruct((TOP_K, n_batches * SC_ROWS, D // 2), jnp.uint32),
        scratch_types=[pltpu.VMEM((TOP_K, SC_ROWS), jnp.int32),
                       pltpu.VMEM((2, half, D // 2), jnp.uint32),
                       pltpu.SemaphoreType.DMA, pltpu.SemaphoreType.DMA],
        name="moe_gather")
    def gather(pos_hbm, y_hbm, out_hbm, idx_v, bufs, gsem, wsem):
        worker = lax.axis_index("s") * SC_CORES + lax.axis_index("c")

        @pl.loop(0, pl.cdiv(n_batches, n_workers))
        def _(j):
            b = j * n_workers + worker

            @pl.when(b < n_batches)
            def _():
                pltpu.sync_copy(pos_hbm.at[b], idx_v)
                items = [(k, h) for k in range(TOP_K) for h in range(2)]

                def fetch(i):
                    k, h = items[i]
                    return pltpu.async_copy(y_hbm.at[idx_v.at[k, pl.ds(h * half, half)]], bufs.at[i % 2], gsem)

                pending_gather = fetch(0)
                pending_write = None
                for i, (k, h) in enumerate(items):
                    pending_gather.wait()
                    if pending_write is not None:
                        pending_write.wait()
                    if i + 1 < len(items):
                        pending_gather = fetch(i + 1)
                    pending_write = pltpu.async_copy(
                        bufs.at[i % 2], out_hbm.at[k, pl.ds(b * SC_ROWS + h * half, half)], wsem)
                pending_write.wait()

    return gather(pos_sc, ys)


def _combine_kernel(*refs):
    y_refs = refs[:TOP_K]
    w_ref, x1_ref, sh_ref, mod_ref, o_ref = refs[TOP_K:]
    f_lo = sh_ref[:, :D // 2]
    f_hi = sh_ref[:, D // 2:]
    for k in range(TOP_K):
        lo, hi = _unpack_bf16_pairs(y_refs[k][...])
        f_lo = f_lo + lo * w_ref[:, k:k + 1]
        f_hi = f_hi + hi * w_ref[:, k:k + 1]
    o_ref[...] = x1_ref[...] + mod_ref[:, 5 * D:6 * D] * jnp.concatenate([f_lo, f_hi], axis=1)


def _combine(yg, w, x1, shared, mod3, n_tiles, tile, mod_row, out_rows, out_tile):
    planes = [pl.BlockSpec((None, TM, D // 2), lambda t, k=k: (k, t, 0)) for k in range(TOP_K)]
    return pl.pallas_call(
        _combine_kernel,
        grid=(n_tiles,),
        in_specs=planes + [pl.BlockSpec((TM, 128), lambda t: (t, 0)),
                           pl.BlockSpec((TM, D), lambda t: (tile(t), 0)),
                           pl.BlockSpec((TM, D), lambda t: (t, 0)),
                           pl.BlockSpec((None, 1, 6 * D), lambda t: (mod_row(tile(t)), 0, 0))],
        out_specs=pl.BlockSpec((TM, D), lambda t: (out_tile(t), 0)),
        out_shape=jax.ShapeDtypeStruct((out_rows, D), F32),
        compiler_params=_cparams(1),
        name="moe_combine",
    )(*([yg] * TOP_K), w, x1, shared, mod3)


def _moe_plan(idx, rank, counts, n_blocks):
    n = idx.shape[0]
    cnt = counts.reshape(N_EXPERTS).astype(jnp.int32)
    padded = (cnt + MOE_ROWS - 1) // MOE_ROWS * MOE_ROWS
    pad_end = jnp.cumsum(padded)
    pad_start = pad_end - padded
    experts = jnp.arange(N_EXPERTS, dtype=jnp.int32)
    pos = rank + jnp.sum(jnp.where(idx[:, :, None] == experts, pad_start, 0), axis=-1)
    blk_start = jnp.arange(n_blocks, dtype=jnp.int32) * MOE_ROWS
    blk_e = jnp.minimum(jnp.sum(blk_start[:, None] >= pad_end[None, :], axis=1), N_EXPERTS - 1).astype(jnp.int32)
    mine = blk_e[:, None] == experts
    in_expert = blk_start - jnp.sum(jnp.where(mine, pad_start, 0), axis=1)
    valid = jnp.clip(jnp.sum(jnp.where(mine, cnt, 0), axis=1) - in_expert, 0, MOE_ROWS).astype(jnp.int32)
    pos_sc = pos.astype(jnp.int32).reshape(n // SC_ROWS, SC_ROWS, TOP_K).transpose(0, 2, 1)
    return blk_e, valid, pos_sc


def _rope_tables(seq, ctx):
    half = MLA_ROPE // 2
    n_freq = half // 2
    inv = ROPE_THETA ** (-2.0 * jnp.arange(n_freq, dtype=F32) / half)
    t = jnp.arange(seq)
    ang_r = (t // GRID_W).astype(F32)[:, None] * inv
    ang_c = (t % GRID_W).astype(F32)[:, None] * inv
    cos = jnp.concatenate([jnp.cos(ang_r), jnp.cos(ang_r), jnp.cos(ang_c), jnp.cos(ang_c)], axis=1)
    sin = jnp.concatenate([-jnp.sin(ang_r), jnp.sin(ang_r), -jnp.sin(ang_c), jnp.sin(ang_c)], axis=1)
    pad_l = MLA_NOPE
    pad_r = HEAD_PAD - MLA_NOPE - MLA_ROPE
    cos = jnp.pad(cos, ((0, ctx), (pad_l, pad_r)), constant_values=1.0)
    cos = cos.at[seq:, :].set(1.0)
    sin = jnp.pad(sin, ((0, ctx), (pad_l, pad_r)))
    return cos, sin


def _pad_heads(w, n_heads, width, offset=0):
    k = w.shape[0]
    w = w.reshape(k, n_heads, width)
    w = jnp.pad(w, ((0, 0), (0, 0), (offset, HEAD_PAD - width - offset)))
    return w.reshape(k, n_heads * HEAD_PAD)


def _pad_vec(g, offset):
    return jnp.pad(g, (offset, HEAD_PAD - g.shape[0] - offset)).reshape(1, HEAD_PAD)


def _reorder_w_in(w):
    off_cq, off_ckv, off_kr, off_uv, off_merge = 2048, 2432, 2688, 2720, 3744
    kr = jnp.pad(w[:, off_kr:off_uv], ((0, 0), (MLA_NOPE, HEAD_PAD - MLA_NOPE - MLA_ROPE)))
    return jnp.concatenate([w[:, off_merge:], w[:, off_uv:off_merge], w[:, :off_cq],
                            w[:, off_ckv:off_kr], kr, w[:, off_cq:off_ckv]], axis=1).astype(BF16)


def kernel(x, c, ctx, c_ctx, ada_w, ada_b, norm1_g, norm2_g, w_in, ret_decay_fwd, ret_decay_bwd, ret_gn_g,
           ret_gn_b, w_br_ret, mla_qa_g, mla_w_uq, mla_kva_g, mla_w_ukv, mla_qn_g, mla_kn_g, mla_kr_g, w_br_mla,
           gmlp_ln_g, gmlp_ln_b, gmlp_ws, gmlp_bs, w_br_gmlp, w_out, moe_router, moe_bias, moe_w_gate, moe_w_up,
           moe_w_down, sh_w_gate, sh_w_up, sh_w_down):
    B, seq, _ = x.shape
    n_ctx = ctx.shape[1]
    depth = ada_w.shape[0]
    assert n_ctx == TM and seq % (2 * ATT_KV_CHUNK) == 0 and seq % TM == 0
    lt = seq + n_ctx
    tiles_per_b = lt // TM
    lat_tiles_per_b = seq // TM
    ctx_tile = lat_tiles_per_b

    def mod_row(t):
        return jnp.where(t % tiles_per_b == ctx_tile, B, t // tiles_per_b)

    c_rows = jnp.concatenate([c, c_ctx[None, :], jnp.zeros((8 - B - 1, D), F32)], axis=0)
    mod = _ada(c_rows, ada_w, ada_b)
    cos_t, sin_t = _rope_tables(seq, n_ctx)
    xs = jnp.concatenate([x, ctx], axis=1).reshape(B * lt, D)

    for l in range(depth):
        last = l == depth - 1
        mod3 = mod[l].reshape(8, 1, 6 * D)
        p = _in_proj(xs, mod3, norm1_g[l].reshape(1, D), _reorder_w_in(w_in[l]), B * tiles_per_b, mod_row)

        lg = jnp.stack([jax.nn.log_sigmoid(ret_decay_fwd[l].astype(F32)),
                        jax.nn.log_sigmoid(ret_decay_bwd[l].astype(F32))])
        y_ret = _retention(p.reshape(B, lt, N_IN_PAD), lg, ret_gn_g[l].reshape(1, -1), ret_gn_b[l].reshape(1, -1),
                           seq, n_ctx)

        w_ukv = mla_w_ukv[l].reshape(MLA_KV_LORA, MLA_HEADS, MLA_NOPE + MLA_V)
        wk_p = _pad_heads(w_ukv[:, :, :MLA_NOPE].reshape(MLA_KV_LORA, -1), MLA_HEADS, MLA_NOPE).astype(BF16)
        wv = w_ukv[:, :, MLA_NOPE:].reshape(MLA_KV_LORA, MLA_HEADS * MLA_V).astype(BF16)
        wq_p = _pad_heads(mla_w_uq[l], MLA_HEADS, MLA_QK).astype(BF16)
        q, k, v = _mla_prep(p, cos_t, sin_t, mla_qa_g[l].reshape(1, -1), mla_kva_g[l].reshape(1, -1),
                            _pad_vec(mla_qn_g[l], 0), _pad_vec(mla_kn_g[l], 0), _pad_vec(mla_kr_g[l], MLA_NOPE),
                            wq_p, wk_p, wv, B, lt)
        o_mla = _attention(q, k, v, seq, n_ctx, lat_tiles_per_b if last else tiles_per_b)

        if last:
            n_tiles = B * lat_tiles_per_b
            tile = lambda t: (t // lat_tiles_per_b) * tiles_per_b + t % lat_tiles_per_b
        else:
            n_tiles = B * tiles_per_b
            tile = lambda t: t
        bs_full = jnp.broadcast_to(gmlp_bs[l][:, :, None], (GMLP_GROUPS, GMLP_CHUNK, GMLP_CHUNK))
        x1, h2 = _merge(xs, mod3, p, y_ret.reshape(B * lt, -1), o_mla.reshape(B * lt, -1),
                        gmlp_ln_g[l].reshape(1, -1), gmlp_ln_b[l].reshape(1, -1), gmlp_ws[l].astype(BF16), bs_full,
                        w_br_ret[l].astype(BF16), w_br_mla[l].astype(BF16), w_br_gmlp[l].astype(BF16),
                        w_out[l].astype(BF16), norm2_g[l].reshape(1, D), n_tiles, tile, mod_row)

        bias_t = jnp.broadcast_to(moe_bias[l][:, None], (N_EXPERTS, 128))
        idx, w, rank, counts, shared, hp = _route(h2, moe_router[l].T, bias_t,
                                                  sh_w_gate[l].astype(BF16), sh_w_up[l].astype(BF16),
                                                  sh_w_down[l].astype(BF16), n_tiles, tile)
        w = jnp.pad(w, ((0, 0), (0, 128 - TOP_K)))
        n_act = n_tiles * TM
        n_blocks = -(-(n_act * TOP_K + N_EXPERTS * (MOE_ROWS - 1)) // MOE_ROWS)
        blk_e, valid, pos_sc = _moe_plan(idx, rank, counts, n_blocks)
        xg = _dispatch(pos_sc, hp, n_blocks * MOE_ROWS)
        ys = _experts(blk_e, valid, xg, moe_w_gate, moe_w_up, moe_w_down, l, n_blocks)
        yg = _gather_rows(pos_sc, ys)
        if last:
            xs = _combine(yg, w, x1, shared, mod3, n_tiles, tile, mod_row, B * seq, lambda t: t)
        else:
            xs = _combine(yg, w, x1, shared, mod3, n_tiles, tile, mod_row, B * lt, tile)
    return xs.reshape(B, seq, D)
```

```python
import functools
import math

import jax
import jax.numpy as jnp
from jax import lax
from jax.experimental import pallas as pl
from jax.experimental.pallas import tpu as pltpu
from jax.experimental.pallas import tpu_sc as plsc

F32 = jnp.float32
BF16 = jnp.bfloat16

D = 1024
GRID_W = 64
RET_HEADS = 4
RET_D = 128
RET_CHUNK = 256
RET_OUT_ROWS = 256
MLA_HEADS = 8
MLA_Q_LORA = 384
MLA_KV_LORA = 256
MLA_NOPE = 64
MLA_ROPE = 32
MLA_V = 64
MLA_V_EXT = MLA_V + 16
MLA_QK = MLA_NOPE + MLA_ROPE
HEAD_PAD = 128
ROPE_THETA = 10000.0
GMLP_GROUPS = 4
GMLP_W = 512
GMLP_CHUNK = 128
N_EXPERTS = 64
TOP_K = 6
D_EXPERT = 256
ROUTED_SCALE = 2.5
EPS = 1e-6
LOG2_E = 1.4426950408889634

TM = 256
MOE_ROWS = 512
RT = 2 * TM
ATT_KV_CHUNK = 1024
ATT_UNROLL = 16
ATT_HEADS = 4

C_MERGE = 0
C_UV = 3072
C_RET = 4096
C_CKV = 6144
C_KR = 6400
C_CQ = 6528
N_IN_PAD = 6912
IN_CHUNK = 768

VMEM_LIMIT = 56 * 1024 * 1024

SC_CORES = 2
SC_SUBCORES = 16
SC_ROWS = 128


def _cparams(n_axes, vmem=VMEM_LIMIT):
    return pltpu.CompilerParams(dimension_semantics=("arbitrary",) * n_axes, vmem_limit_bytes=vmem)


def _silu(x):
    return x * jax.nn.sigmoid(x)


def _dot(a, b):
    return jnp.dot(a, b, preferred_element_type=F32)


def _dot_nt(a, b):
    return lax.dot_general(a, b, (((1,), (1,)), ((), ())), preferred_element_type=F32)


def _dot_tn(a, b):
    return lax.dot_general(a, b, (((0,), (0,)), ((), ())), preferred_element_type=F32)


def _pack_bf16_pairs(x):
    n = x.shape[1] // 2
    lo = lax.bitcast_convert_type(x[:, :n].astype(BF16).astype(F32), jnp.uint32)
    hi = lax.bitcast_convert_type(x[:, n:].astype(BF16).astype(F32), jnp.uint32)
    return (lo >> 16) | hi


def _unpack_bf16_pairs(u):
    lo = lax.bitcast_convert_type(u << 16, F32)
    hi = lax.bitcast_convert_type(u & jnp.uint32(0xFFFF0000), F32)
    return lo, hi


def _ada_kernel(c_ref, w_ref, b_ref, o_ref):
    s = _silu(c_ref[...])
    o_ref[...] = _dot(s.astype(BF16), w_ref[...].astype(BF16)) + b_ref[...]


def _ada(c_rows, ada_w, ada_b):
    depth = ada_w.shape[0]
    n = ada_w.shape[2]
    cw = 1536
    return pl.pallas_call(
        _ada_kernel,
        grid=(depth, n // cw),
        in_specs=[pl.BlockSpec((8, D), lambda l, j: (0, 0)),
                  pl.BlockSpec((None, D, cw), lambda l, j: (l, 0, j)),
                  pl.BlockSpec((None, 1, cw), lambda l, j: (l, 0, j))],
        out_specs=pl.BlockSpec((None, 8, cw), lambda l, j: (l, 0, j)),
        out_shape=jax.ShapeDtypeStruct((depth, 8, n), F32),
        compiler_params=_cparams(2),
        name="ada_mod",
    )(c_rows, ada_w, ada_b.reshape(depth, 1, n))


def _modulated_rmsnorm(x, g, shift, scale):
    y = x * lax.rsqrt(jnp.mean(x * x, axis=-1, keepdims=True) + EPS) * g
    return y * (1.0 + scale) + shift


def _in_proj_kernel(x_ref, mod_ref, g_ref, w_ref, o_ref, h_scr):
    h = _modulated_rmsnorm(x_ref[...], g_ref[...], mod_ref[:, 0:D], mod_ref[:, D:2 * D])
    h_scr[...] = h.astype(BF16)
    for c in range(N_IN_PAD // IN_CHUNK):
        cols = slice(c * IN_CHUNK, (c + 1) * IN_CHUNK)
        o_ref[:, cols] = _dot(h_scr[...], w_ref[:, cols]).astype(BF16)


def _in_proj(xs, mod3, g, w_in_r, n_tiles, mod_row):
    n_rows = xs.shape[0]
    return pl.pallas_call(
        _in_proj_kernel,
        grid=(n_tiles,),
        in_specs=[pl.BlockSpec((TM, D), lambda t: (t, 0)),
                  pl.BlockSpec((None, 1, 6 * D), lambda t: (mod_row(t), 0, 0)),
                  pl.BlockSpec((1, D), lambda t: (0, 0)),
                  pl.BlockSpec((D, N_IN_PAD), lambda t: (0, 0))],
        out_specs=pl.BlockSpec((TM, N_IN_PAD), lambda t: (t, 0)),
        out_shape=jax.ShapeDtypeStruct((n_rows, N_IN_PAD), BF16),
        scratch_shapes=[pltpu.VMEM((TM, D), BF16)],
        compiler_params=_cparams(1),
        name="in_proj",
    )(xs, mod3, g, w_in_r)


def _retention_kernel(lg_ref, q_ref, k_ref, v_ref, g_ref, gng_ref, gnb_ref, y_ref, of_scr, ob_scr,
                      *, n_lat_chunks, n_ctx_chunks):
    h = pl.program_id(1)
    lg_f = lg_ref[0, h]
    lg_b = lg_ref[1, h]
    C = RET_CHUNK
    k_scale = RET_D ** -0.5
    ri = lax.broadcasted_iota(jnp.int32, (C, C), 0).astype(F32)
    ci = lax.broadcasted_iota(jnp.int32, (C, C), 1).astype(F32)
    pos = lax.broadcasted_iota(jnp.int32, (C, 1), 0).astype(F32)
    diff = ri - ci
    d_f = jnp.where(diff >= 0, jnp.exp(lg_f * jnp.maximum(diff, 0.0)), 0.0) * k_scale
    d_b = jnp.where(diff < 0, jnp.exp(lg_b * jnp.maximum(-diff, 0.0)), 0.0) * k_scale
    qdec_f = jnp.exp(lg_f * (pos + 1.0))
    kdec_f = jnp.exp(lg_f * (C - 1.0 - pos)) * k_scale
    cdec_f = jnp.exp(lg_f * C)
    qdec_b = jnp.exp(lg_b * (C - pos))
    kdec_b = jnp.exp(lg_b * pos) * k_scale
    cdec_b = jnp.exp(lg_b * C)

    def chunk(c, state, dmat, qdec, kdec, cdec):
        rows = pl.ds(pl.multiple_of(c * C, C), C)
        q = q_ref[rows, :]
        k = k_ref[rows, :]
        v = v_ref[rows, :]
        att = (_dot_nt(q, k) * dmat).astype(BF16)
        o = _dot(att, v) + _dot((q.astype(F32) * qdec).astype(BF16), state.astype(BF16))
        kd = (k.astype(F32) * kdec).astype(BF16)
        return rows, o, state * cdec + _dot_tn(kd, v)

    n_all = n_lat_chunks + n_ctx_chunks

    def scan_body(i, states):
        s_f, s_b = states
        c_f = jnp.where(i < n_ctx_chunks, n_lat_chunks + i, i - n_ctx_chunks)
        rows, o, s_f = chunk(c_f, s_f, d_f, qdec_f, kdec_f, cdec_f)
        of_scr[rows, :] = o
        rows, o, s_b = chunk(n_all - 1 - i, s_b, d_b, qdec_b, kdec_b, cdec_b)
        ob_scr[rows, :] = o
        return s_f, s_b

    zero = jnp.zeros((RET_D, RET_D), F32)
    lax.fori_loop(0, n_all, scan_body, (zero, zero), unroll=2)

    def out_body(c, _):
        rows = pl.ds(pl.multiple_of(c * RET_OUT_ROWS, RET_OUT_ROWS), RET_OUT_ROWS)
        o = of_scr[rows, :] + ob_scr[rows, :]
        mu = jnp.mean(o, axis=-1, keepdims=True)
        var = jnp.mean(jnp.square(o - mu), axis=-1, keepdims=True)
        on = (o - mu) * lax.rsqrt(var + EPS)
        y = _silu(g_ref[rows, :].astype(F32)) * (on * gng_ref[...] + gnb_ref[...])
        y_ref[rows, :] = y.astype(BF16)
        return 0

    lax.fori_loop(0, n_all * C // RET_OUT_ROWS, out_body, 0)


def _retention(p3, lg, gn_g, gn_b, seq, ctx):
    B, lt, _ = p3.shape
    base = C_RET // RET_D
    kern = functools.partial(_retention_kernel, n_lat_chunks=seq // RET_CHUNK, n_ctx_chunks=ctx // RET_CHUNK)

    def col(off):
        return pl.BlockSpec((None, lt, RET_D), lambda b, h: (b, 0, base + off * RET_HEADS + h))

    return pl.pallas_call(
        kern,
        grid=(B, RET_HEADS),
        in_specs=[pl.BlockSpec(memory_space=pltpu.SMEM),
                  col(0), col(1), col(2), col(3),
                  pl.BlockSpec((1, RET_D), lambda b, h: (0, h)),
                  pl.BlockSpec((1, RET_D), lambda b, h: (0, h))],
        out_specs=pl.BlockSpec((None, lt, RET_D), lambda b, h: (b, 0, h)),
        out_shape=jax.ShapeDtypeStruct((B, lt, RET_HEADS * RET_D), BF16),
        scratch_shapes=[pltpu.VMEM((lt, RET_D), F32), pltpu.VMEM((lt, RET_D), F32)],
        compiler_params=_cparams(2),
        name="retention",
    )(lg, p3, p3, p3, p3, gn_g, gn_b)


def _rope_rotate(x, first_half):
    return jnp.where(first_half, pltpu.roll(x, HEAD_PAD - 8, 1), pltpu.roll(x, 8, 1))


def _mla_prep_kernel(cq_ref, ckv_ref, kr_ref, cos_ref, sin_ref, qa_ref, kva_ref, qn_ref, kn_ref, krg_ref,
                     wq_ref, wk_ref, wv_ref, q_ref, k_ref, v_ref):
    lane = lax.broadcasted_iota(jnp.int32, (1, HEAD_PAD), 1)
    first_half = (lane % 16) < 8
    cos = cos_ref[...]
    sin = sin_ref[...]

    def rms(x, n):
        return x * lax.rsqrt(jnp.sum(x * x, axis=-1, keepdims=True) * (1.0 / n) + EPS)

    def rope(x):
        return x * cos + _rope_rotate(x, first_half) * sin

    cq = cq_ref[...].astype(F32)
    cqn = (rms(cq, MLA_Q_LORA) * qa_ref[...]).astype(BF16)
    q_all = _dot(cqn, wq_ref[...])
    ckv = ckv_ref[...].astype(F32)
    ckvn = (rms(ckv, MLA_KV_LORA) * kva_ref[...]).astype(BF16)
    k_all = _dot(ckvn, wk_ref[...])
    v_all = _dot(ckvn, wv_ref[...])
    k_rope = rope(rms(kr_ref[...].astype(F32), MLA_ROPE) * krg_ref[...])
    scale = MLA_QK ** -0.5 * LOG2_E
    v_t = v_all.T
    ones_row = jnp.where(lax.broadcasted_iota(jnp.int32, (MLA_V_EXT - MLA_V, TM), 0) == 0, 1.0, 0.0)
    for h in range(MLA_HEADS):
        cols = slice(h * HEAD_PAD, (h + 1) * HEAD_PAD)
        qh = rope(rms(q_all[:, cols], MLA_QK) * qn_ref[...]) * scale
        q_ref[h] = qh.astype(BF16)
        kh = rms(k_all[:, cols], MLA_NOPE) * kn_ref[...] + k_rope
        k_ref[h] = kh.astype(BF16)
        v_ref[h] = jnp.concatenate([v_t[h * MLA_V:(h + 1) * MLA_V, :], ones_row], axis=0).astype(BF16)


def _mla_prep(p, cos_t, sin_t, qa_g, kva_g, qn_p, kn_p, kr_p, wq_p, wk_p, wv, B, lt):
    tiles_per_b = lt // TM
    hw = MLA_HEADS * HEAD_PAD
    const = lambda shape: pl.BlockSpec(shape, lambda b, j: (0,) * len(shape))
    head_out = pl.BlockSpec((None, MLA_HEADS, TM, HEAD_PAD), lambda b, j: (b, 0, j, 0))
    shp = jax.ShapeDtypeStruct((B, MLA_HEADS, lt, HEAD_PAD), BF16)
    v_out = pl.BlockSpec((None, MLA_HEADS, None, MLA_V_EXT, TM), lambda b, j: (b, 0, j, 0, 0))
    v_shp = jax.ShapeDtypeStruct((B, MLA_HEADS, tiles_per_b, MLA_V_EXT, TM), BF16)
    return pl.pallas_call(
        _mla_prep_kernel,
        grid=(B, tiles_per_b),
        in_specs=[pl.BlockSpec((TM, MLA_Q_LORA), lambda b, j: (b * tiles_per_b + j, C_CQ // MLA_Q_LORA)),
                  pl.BlockSpec((TM, MLA_KV_LORA), lambda b, j: (b * tiles_per_b + j, C_CKV // MLA_KV_LORA)),
                  pl.BlockSpec((TM, HEAD_PAD), lambda b, j: (b * tiles_per_b + j, C_KR // HEAD_PAD)),
                  pl.BlockSpec((TM, HEAD_PAD), lambda b, j: (j, 0)),
                  pl.BlockSpec((TM, HEAD_PAD), lambda b, j: (j, 0)),
                  const((1, MLA_Q_LORA)), const((1, MLA_KV_LORA)),
                  const((1, HEAD_PAD)), const((1, HEAD_PAD)), const((1, HEAD_PAD)),
                  const((MLA_Q_LORA, hw)), const((MLA_KV_LORA, hw)), const((MLA_KV_LORA, MLA_HEADS * MLA_V))],
        out_specs=[head_out, head_out, v_out],
        out_shape=[shp, shp, v_shp],
        compiler_params=_cparams(2),
        name="mla_prep",
    )(p, p, p, cos_t, sin_t, qa_g, kva_g, qn_p, kn_p, kr_p, wq_p, wk_p, wv)


def _attention_kernel(q_ref, k_ref, v_ref, o_ref, s_scr, *, seq, ctx, ctx_tile):
    i = pl.program_id(2)
    n_blk = ATT_KV_CHUNK // TM
    n_chunks = seq // ATT_KV_CHUNK
    unroll = math.gcd(n_chunks, ATT_UNROLL)

    def scores(hh, slot, blk, nb):
        start = blk * TM if isinstance(blk, int) else pl.multiple_of(blk * TM, TM)
        s_scr[hh, slot, 0:nb * TM, :] = _dot_nt(k_ref[hh, pl.ds(start, nb * TM), :], q_ref[hh])

    def absorb(hh, slot, blk, nb, carry):
        m, acc = carry
        s = s_scr[hh, slot, 0:nb * TM, :]
        m_new = jnp.maximum(m, jnp.max(s, axis=0, keepdims=True))
        p = jnp.exp2(s - m_new).astype(BF16)
        acc = jnp.exp2(m - m_new) * acc
        for j in range(nb):
            acc = acc + _dot(v_ref[hh, blk + j], p[j * TM:(j + 1) * TM, :])
        return m_new, acc

    def init():
        return (jnp.full((1, TM), -jnp.inf, F32), jnp.zeros((MLA_V_EXT, TM), F32))

    def write(carries):
        outs = [acc[0:MLA_V, :] / acc[MLA_V:MLA_V + 1, :] for _, acc in carries]
        o_ref[...] = jnp.concatenate(outs, axis=0).T.astype(BF16)

    def step(carries, slot, blk, nb, next_blk, next_nb):
        out = []
        for hh in range(ATT_HEADS):
            if next_blk is not None:
                scores(hh, 1 - slot, next_blk, next_nb)
            out.append(absorb(hh, slot, blk, nb, carries[hh]))
        return tuple(out)

    ctx_blk = seq // TM
    ctx_nb = ctx // TM

    @pl.when(i != ctx_tile)
    def _():
        for hh in range(ATT_HEADS):
            scores(hh, 0, ctx_blk, ctx_nb)
        carries = step(tuple(init() for _ in range(ATT_HEADS)), 0, ctx_blk, ctx_nb, 0, n_blk)
        last_blk = (n_chunks - 1) * n_blk

        def body(c, carries):
            for u in range(unroll):
                blk = (c * unroll + u) * n_blk
                carries = step(carries, (1 + u) % 2, blk, n_blk, jnp.minimum(blk + n_blk, last_blk), n_blk)
            return carries

        if n_chunks == unroll:
            for u in range(n_chunks):
                nxt = (u + 1) * n_blk if u + 1 < n_chunks else None
                carries = step(carries, (1 + u) % 2, u * n_blk, n_blk, nxt, n_blk)
            write(carries)
        else:
            write(lax.fori_loop(0, n_chunks // unroll, body, carries))

    @pl.when(i == ctx_tile)
    def _():
        for hh in range(ATT_HEADS):
            scores(hh, 0, ctx_blk, ctx_nb)
        write(step(tuple(init() for _ in range(ATT_HEADS)), 0, ctx_blk, ctx_nb, None, None))


def _attention(q, k, v, seq, ctx, n_q_tiles):
    B, H, lt, _ = q.shape
    kern = functools.partial(_attention_kernel, seq=seq, ctx=ctx, ctx_tile=seq // TM)
    return pl.pallas_call(
        kern,
        grid=(B, H // ATT_HEADS, n_q_tiles),
        in_specs=[pl.BlockSpec((None, ATT_HEADS, TM, HEAD_PAD), lambda b, h, i: (b, h, i, 0)),
                  pl.BlockSpec((None, ATT_HEADS, lt, HEAD_PAD), lambda b, h, i: (b, h, 0, 0)),
                  pl.BlockSpec((None, ATT_HEADS, lt // TM, MLA_V_EXT, TM), lambda b, h, i: (b, h, 0, 0, 0))],
        out_specs=pl.BlockSpec((None, TM, ATT_HEADS * MLA_V), lambda b, h, i: (b, i, h)),
        out_shape=jax.ShapeDtypeStruct((B, lt, H * MLA_V), BF16),
        scratch_shapes=[pltpu.VMEM((ATT_HEADS, 2, ATT_KV_CHUNK, TM), F32)],
        compiler_params=_cparams(3),
        name="attention",
    )(q, k, v)


def _merge_kernel(x_ref, mod_ref, mg_ref, uv_ref, yr_ref, om_ref, lng_ref, lnb_ref, ws_ref, bs_ref,
                  wr_ref, wm_ref, wg_ref, wo_ref, n2_ref, x1_ref, h2_ref):
    yr = _dot(yr_ref[...], wr_ref[...])
    ym = _dot(om_ref[...], wm_ref[...])
    z = jax.nn.gelu(uv_ref[...].astype(F32))
    u = z[:, :GMLP_W]
    v = z[:, GMLP_W:]
    mu = jnp.mean(v, axis=-1, keepdims=True)
    var = jnp.mean(jnp.square(v - mu), axis=-1, keepdims=True)
    vn = ((v - mu) * lax.rsqrt(var + EPS) * lng_ref[...] + lnb_ref[...]).astype(BF16)
    gw = GMLP_W // GMLP_GROUPS
    chunks = []
    for c in range(TM // GMLP_CHUNK):
        rows = slice(c * GMLP_CHUNK, (c + 1) * GMLP_CHUNK)
        groups = [_dot(ws_ref[g], vn[rows, g * gw:(g + 1) * gw]) + bs_ref[g] for g in range(GMLP_GROUPS)]
        chunks.append(jnp.concatenate(groups, axis=1))
    sv = jnp.concatenate(chunks, axis=0)
    yg = _dot((u * sv).astype(BF16), wg_ref[...])
    gate = jax.nn.sigmoid(mg_ref[...].astype(F32))
    y = gate[:, :D] * yr + gate[:, D:2 * D] * ym + gate[:, 2 * D:] * yg
    out = _dot(y.astype(BF16), wo_ref[...])
    x1 = x_ref[...] + mod_ref[:, 2 * D:3 * D] * out
    x1_ref[...] = x1
    h2_ref[...] = _modulated_rmsnorm(x1, n2_ref[...], mod_ref[:, 3 * D:4 * D], mod_ref[:, 4 * D:5 * D])


def _merge(xs, mod3, p, y_ret, o_mla, ln_g, ln_b, ws, bs_full, w_br_ret, w_br_mla, w_br_gmlp, w_out, n2_g,
           n_tiles, tile, mod_row):
    n_rows = xs.shape[0]
    const = lambda shape: pl.BlockSpec(shape, lambda t: (0,) * len(shape))
    row = lambda w, cb=0: pl.BlockSpec((TM, w), lambda t: (tile(t), cb))
    shp = jax.ShapeDtypeStruct((n_rows, D), F32)
    return pl.pallas_call(
        _merge_kernel,
        grid=(n_tiles,),
        in_specs=[row(D),
                  pl.BlockSpec((None, 1, 6 * D), lambda t: (mod_row(tile(t)), 0, 0)),
                  row(3 * D, C_MERGE // (3 * D)), row(D, C_UV // D),
                  row(RET_HEADS * RET_D), row(MLA_HEADS * MLA_V),
                  const((1, GMLP_W)), const((1, GMLP_W)),
                  const((GMLP_GROUPS, GMLP_CHUNK, GMLP_CHUNK)), const((GMLP_GROUPS, GMLP_CHUNK, GMLP_CHUNK)),
                  const((RET_HEADS * RET_D, D)), const((MLA_HEADS * MLA_V, D)), const((GMLP_W, D)),
                  const((D, D)), const((1, D))],
        out_specs=[row(D), row(D)],
        out_shape=[shp, shp],
        compiler_params=_cparams(1),
        name="merge",
    )(xs, mod3, p, p, y_ret, o_mla, ln_g, ln_b, ws, bs_full, w_br_ret, w_br_mla, w_br_gmlp, w_out, n2_g)


def _route_kernel(ha_ref, hb_ref, rt_ref, bt_ref, sg_ref, su_ref, sd_ref, idx_ref, w_ref, rank_ref, cnt_ref, sh_ref,
                  hp_ref, cnt_scr):
    @pl.when(pl.program_id(0) == 0)
    def _():
        cnt_scr[...] = jnp.zeros_like(cnt_scr)

    h = jnp.concatenate([ha_ref[...], hb_ref[...]], axis=0)
    logits = lax.dot_general(rt_ref[...], h, (((1,), (1,)), ((), ())), preferred_element_type=F32,
                             precision=lax.Precision.HIGHEST)
    scores = jax.nn.sigmoid(logits)
    sel = scores + bt_ref[:, 0:1]
    row_e = lax.broadcasted_iota(jnp.int32, (N_EXPERTS, RT), 0).astype(F32)
    row_o = lax.broadcasted_iota(jnp.int32, (8, RT), 0)
    idx_out = jnp.zeros((8, RT), F32)
    w_out = jnp.zeros((8, RT), F32)
    hits = []
    for k in range(TOP_K):
        best = jnp.max(sel, axis=0, keepdims=True)
        pick = jnp.min(jnp.where(sel == best, row_e, float(N_EXPERTS)), axis=0, keepdims=True)
        hit = row_e == pick
        hits.append(hit)
        wk = jnp.sum(jnp.where(hit, scores, 0.0), axis=0, keepdims=True)
        sel = jnp.where(hit, -jnp.inf, sel)
        idx_out = jnp.where(row_o == k, pick, idx_out)
        w_out = jnp.where(row_o == k, wk, w_out)
    w_out = w_out / jnp.sum(w_out, axis=0, keepdims=True) * ROUTED_SCALE
    idx_ref[...] = idx_out.astype(jnp.int32)
    w_ref[...] = w_out
    chosen = jnp.zeros((N_EXPERTS, RT), F32)
    for hit in hits:
        chosen = jnp.where(hit, 1.0, chosen)
    earlier = (lax.broadcasted_iota(jnp.int32, (RT, RT), 0) < lax.broadcasted_iota(jnp.int32, (RT, RT), 1))
    before = _dot(chosen.astype(BF16), jnp.where(earlier, 1.0, 0.0).astype(BF16)) + cnt_scr[:, 0:1]
    rank_out = jnp.zeros((8, RT), F32)
    for k, hit in enumerate(hits):
        rank_out = jnp.where(row_o == k, jnp.sum(jnp.where(hit, before, 0.0), axis=0, keepdims=True), rank_out)
    rank_ref[...] = rank_out.astype(jnp.int32)
    cnt_scr[...] += jnp.sum(chosen, axis=1, keepdims=True)
    cnt_ref[...] = cnt_scr[...]
    hb = h.astype(BF16)
    a = _silu(_dot(hb, sg_ref[...])) * _dot(hb, su_ref[...])
    sh_ref[...] = _dot(a.astype(BF16), sd_ref[...])
    hp_ref[...] = _pack_bf16_pairs(h)


def _route(h2, router_t, bias_t, sg, su, sd, n_tiles, tile):
    const = lambda shape: pl.BlockSpec(shape, lambda t: (0,) * len(shape))
    n_act = n_tiles * TM
    assert n_tiles % 2 == 0
    n_steps = n_tiles // 2
    per_tok = pl.BlockSpec((None, 8, RT), lambda t: (t, 0, 0))
    idx_t, w_t, rank_t, counts, shared, hp = pl.pallas_call(
        _route_kernel,
        grid=(n_steps,),
        in_specs=[pl.BlockSpec((TM, D), lambda t: (tile(2 * t), 0)),
                  pl.BlockSpec((TM, D), lambda t: (tile(2 * t + 1), 0)),
                  const((N_EXPERTS, D)), const((N_EXPERTS, 128)),
                  const((D, D_EXPERT)), const((D, D_EXPERT)), const((D_EXPERT, D))],
        out_specs=[per_tok, per_tok, per_tok,
                   pl.BlockSpec((N_EXPERTS, 128), lambda t: (0, 0)),
                   pl.BlockSpec((RT, D), lambda t: (t, 0)),
                   pl.BlockSpec((RT, D // 2), lambda t: (t, 0))],
        out_shape=[jax.ShapeDtypeStruct((n_steps, 8, RT), jnp.int32),
                   jax.ShapeDtypeStruct((n_steps, 8, RT), F32),
                   jax.ShapeDtypeStruct((n_steps, 8, RT), jnp.int32),
                   jax.ShapeDtypeStruct((N_EXPERTS, 128), F32),
                   jax.ShapeDtypeStruct((n_act, D), F32),
                   jax.ShapeDtypeStruct((n_act, D // 2), jnp.uint32)],
        scratch_shapes=[pltpu.VMEM((N_EXPERTS, 128), F32)],
        compiler_params=_cparams(1),
        name="route_shared",
    )(h2, h2, router_t, bias_t, sg, su, sd)
    token_major = lambda a: a.transpose(0, 2, 1).reshape(n_act, 8)[:, :TOP_K]
    return token_major(idx_t), token_major(w_t), token_major(rank_t), counts[:, 0], shared, hp


def _dispatch(pos_sc, hp, n_rows):
    n_batches = pos_sc.shape[0]
    n_workers = SC_CORES * SC_SUBCORES
    mesh = plsc.VectorSubcoreMesh(core_axis_name="c", subcore_axis_name="s")

    @functools.partial(
        pl.kernel, mesh=mesh,
        out_type=jax.ShapeDtypeStruct((n_rows, D // 2), jnp.uint32),
        scratch_types=[pltpu.VMEM((TOP_K, SC_ROWS), jnp.int32),
                       pltpu.VMEM((SC_ROWS, D // 2), jnp.uint32),
                       pltpu.SemaphoreType.DMA],
        name="moe_dispatch")
    def scatter(pos_hbm, h_hbm, xs_hbm, idx_v, rows_v, sem):
        worker = lax.axis_index("s") * SC_CORES + lax.axis_index("c")

        @pl.loop(0, pl.cdiv(n_batches, n_workers))
        def _(j):
            b = j * n_workers + worker

            @pl.when(b < n_batches)
            def _():
                pltpu.sync_copy(pos_hbm.at[b], idx_v)
                pltpu.sync_copy(h_hbm.at[pl.ds(b * SC_ROWS, SC_ROWS)], rows_v)
                copies = [pltpu.async_copy(rows_v, xs_hbm.at[idx_v.at[k]], sem) for k in range(TOP_K)]
                for cp in copies:
                    cp.wait()

    return scatter(pos_sc, hp)


def _expert_kernel(blk_e_ref, valid_ref, x_ref, wg_ref, wu_ref, wd_ref, y_ref, wg_s, wu_s, wd_s):
    i = pl.program_id(0)
    n_valid = valid_ref[i]

    @pl.when(n_valid > 0)
    def _():
        @pl.when(jnp.logical_or(i == 0, blk_e_ref[i] != blk_e_ref[jnp.maximum(i - 1, 0)]))
        def _():
            wg_s[...] = wg_ref[...].astype(BF16)
            wu_s[...] = wu_ref[...].astype(BF16)
            wd_s[...] = wd_ref[...].astype(BF16)

        row = lax.broadcasted_iota(jnp.int32, (MOE_ROWS, 1), 0)
        lo, hi = _unpack_bf16_pairs(jnp.where(row < n_valid, x_ref[...], jnp.uint32(0)))
        x = jnp.concatenate([lo, hi], axis=1).astype(BF16)
        hb = _silu(_dot(x, wg_s[...])) * _dot(x, wu_s[...])
        y_ref[...] = _pack_bf16_pairs(_dot(hb.astype(BF16), wd_s[...]))

    @pl.when(n_valid == 0)
    def _():
        y_ref[...] = jnp.zeros_like(y_ref)


def _experts(blk_e, valid, xs, wg, wu, wd, layer, n_blocks):
    grid_spec = pltpu.PrefetchScalarGridSpec(
        num_scalar_prefetch=2,
        grid=(n_blocks,),
        in_specs=[pl.BlockSpec((MOE_ROWS, D // 2), lambda i, be, nv: (i, 0)),
                  pl.BlockSpec((None, None, D, D_EXPERT), lambda i, be, nv: (layer, be[i], 0, 0)),
                  pl.BlockSpec((None, None, D, D_EXPERT), lambda i, be, nv: (layer, be[i], 0, 0)),
                  pl.BlockSpec((None, None, D_EXPERT, D), lambda i, be, nv: (layer, be[i], 0, 0))],
        out_specs=pl.BlockSpec((MOE_ROWS, D // 2), lambda i, be, nv: (i, 0)),
        scratch_shapes=[pltpu.VMEM((D, D_EXPERT), BF16), pltpu.VMEM((D, D_EXPERT), BF16),
                        pltpu.VMEM((D_EXPERT, D), BF16)],
    )
    return pl.pallas_call(
        _expert_kernel,
        grid_spec=grid_spec,
        out_shape=jax.ShapeDtypeStruct((n_blocks * MOE_ROWS, D // 2), jnp.uint32),
        compiler_params=_cparams(1),
        name="routed_experts",
    )(blk_e, valid, xs, wg, wu, wd)


def _gather_rows(pos_sc, ys):
    n_batches = pos_sc.shape[0]
    n_workers = SC_CORES * SC_SUBCORES
    half = SC_ROWS // 2
    mesh = plsc.VectorSubcoreMesh(core_axis_name="c", subcore_axis_name="s")

    @functools.partial(
        pl.kernel, mesh=mesh,
        out_type=jax.ShapeDtypeStruct((TOP_K, n_batches * SC_ROWS, D // 2), jnp.uint32),
        scratch_types=[pltpu.VMEM((TOP_K, SC_ROWS), jnp.int32),
                       pltpu.VMEM((2, half, D // 2), jnp.uint32),
                       pltpu.SemaphoreType.DMA, pltpu.SemaphoreType.DMA],
        name="moe_gather")
    def gather(pos_hbm, y_hbm, out_hbm, idx_v, bufs, gsem, wsem):
        worker = lax.axis_index("s") * SC_CORES + lax.axis_index("c")

        @pl.loop(0, pl.cdiv(n_batches, n_workers))
        def _(j):
            b = j * n_workers + worker

            @pl.when(b < n_batches)
            def _():
                pltpu.sync_copy(pos_hbm.at[b], idx_v)
                items = [(k, h) for k in range(TOP_K) for h in range(2)]

                def fetch(i):
                    k, h = items[i]
                    return pltpu.async_copy(y_hbm.at[idx_v.at[k, pl.ds(h * half, half)]], bufs.at[i % 2], gsem)

                pending_gather = fetch(0)
                pending_write = None
                for i, (k, h) in enumerate(items):
                    pending_gather.wait()
                    if pending_write is not None:
                        pending_write.wait()
                    if i + 1 < len(items):
                        pending_gather = fetch(i + 1)
                    pending_write = pltpu.async_copy(
                        bufs.at[i % 2], out_hbm.at[k, pl.ds(b * SC_ROWS + h * half, half)], wsem)
                pending_write.wait()

    return gather(pos_sc, ys)


def _combine_kernel(*refs):
    y_refs = refs[:TOP_K]
    w_ref, x1_ref, sh_ref, mod_ref, o_ref = refs[TOP_K:]
    f_lo = sh_ref[:, :D // 2]
    f_hi = sh_ref[:, D // 2:]
    for k in range(TOP_K):
        lo, hi = _unpack_bf16_pairs(y_refs[k][...])
        f_lo = f_lo + lo * w_ref[:, k:k + 1]
        f_hi = f_hi + hi * w_ref[:, k:k + 1]
    o_ref[...] = x1_ref[...] + mod_ref[:, 5 * D:6 * D] * jnp.concatenate([f_lo, f_hi], axis=1)


def _combine(yg, w, x1, shared, mod3, n_tiles, tile, mod_row, out_rows, out_tile):
    planes = [pl.BlockSpec((None, TM, D // 2), lambda t, k=k: (k, t, 0)) for k in range(TOP_K)]
    return pl.pallas_call(
        _combine_kernel,
        grid=(n_tiles,),
        in_specs=planes + [pl.BlockSpec((TM, 128), lambda t: (t, 0)),
                           pl.BlockSpec((TM, D), lambda t: (tile(t), 0)),
                           pl.BlockSpec((TM, D), lambda t: (t, 0)),
                           pl.BlockSpec((None, 1, 6 * D), lambda t: (mod_row(tile(t)), 0, 0))],
        out_specs=pl.BlockSpec((TM, D), lambda t: (out_tile(t), 0)),
        out_shape=jax.ShapeDtypeStruct((out_rows, D), F32),
        compiler_params=_cparams(1),
        name="moe_combine",
    )(*([yg] * TOP_K), w, x1, shared, mod3)


def _moe_plan(idx, rank, counts, n_blocks):
    n = idx.shape[0]
    cnt = counts.reshape(N_EXPERTS).astype(jnp.int32)
    padded = (cnt + MOE_ROWS - 1) // MOE_ROWS * MOE_ROWS
    pad_end = jnp.cumsum(padded)
    pad_start = pad_end - padded
    experts = jnp.arange(N_EXPERTS, dtype=jnp.int32)
    pos = rank + jnp.sum(jnp.where(idx[:, :, None] == experts, pad_start, 0), axis=-1)
    blk_start = jnp.arange(n_blocks, dtype=jnp.int32) * MOE_ROWS
    blk_e = jnp.minimum(jnp.sum(blk_start[:, None] >= pad_end[None, :], axis=1), N_EXPERTS - 1).astype(jnp.int32)
    mine = blk_e[:, None] == experts
    in_expert = blk_start - jnp.sum(jnp.where(mine, pad_start, 0), axis=1)
    valid = jnp.clip(jnp.sum(jnp.where(mine, cnt, 0), axis=1) - in_expert, 0, MOE_ROWS).astype(jnp.int32)
    pos_sc = pos.astype(jnp.int32).reshape(n // SC_ROWS, SC_ROWS, TOP_K).transpose(0, 2, 1)
    return blk_e, valid, pos_sc


def _rope_tables(seq, ctx):
    half = MLA_ROPE // 2
    n_freq = half // 2
    inv = ROPE_THETA ** (-2.0 * jnp.arange(n_freq, dtype=F32) / half)
    t = jnp.arange(seq)
    ang_r = (t // GRID_W).astype(F32)[:, None] * inv
    ang_c = (t % GRID_W).astype(F32)[:, None] * inv
    cos = jnp.concatenate([jnp.cos(ang_r), jnp.cos(ang_r), jnp.cos(ang_c), jnp.cos(ang_c)], axis=1)
    sin = jnp.concatenate([-jnp.sin(ang_r), jnp.sin(ang_r), -jnp.sin(ang_c), jnp.sin(ang_c)], axis=1)
    pad_l = MLA_NOPE
    pad_r = HEAD_PAD - MLA_NOPE - MLA_ROPE
    cos = jnp.pad(cos, ((0, ctx), (pad_l, pad_r)), constant_values=1.0)
    cos = cos.at[seq:, :].set(1.0)
    sin = jnp.pad(sin, ((0, ctx), (pad_l, pad_r)))
    return cos, sin


def _pad_heads(w, n_heads, width, offset=0):
    k = w.shape[0]
    w = w.reshape(k, n_heads, width)
    w = jnp.pad(w, ((0, 0), (0, 0), (offset, HEAD_PAD - width - offset)))
    return w.reshape(k, n_heads * HEAD_PAD)


def _pad_vec(g, offset):
    return jnp.pad(g, (offset, HEAD_PAD - g.shape[0] - offset)).reshape(1, HEAD_PAD)


def _reorder_w_in(w):
    off_cq, off_ckv, off_kr, off_uv, off_merge = 2048, 2432, 2688, 2720, 3744
    kr = jnp.pad(w[:, off_kr:off_uv], ((0, 0), (MLA_NOPE, HEAD_PAD - MLA_NOPE - MLA_ROPE)))
    return jnp.concatenate([w[:, off_merge:], w[:, off_uv:off_merge], w[:, :off_cq],
                            w[:, off_ckv:off_kr], kr, w[:, off_cq:off_ckv]], axis=1).astype(BF16)


def kernel(x, c, ctx, c_ctx, ada_w, ada_b, norm1_g, norm2_g, w_in, ret_decay_fwd, ret_decay_bwd, ret_gn_g,
           ret_gn_b, w_br_ret, mla_qa_g, mla_w_uq, mla_kva_g, mla_w_ukv, mla_qn_g, mla_kn_g, mla_kr_g, w_br_mla,
           gmlp_ln_g, gmlp_ln_b, gmlp_ws, gmlp_bs, w_br_gmlp, w_out, moe_router, moe_bias, moe_w_gate, moe_w_up,
           moe_w_down, sh_w_gate, sh_w_up, sh_w_down):
    B, seq, _ = x.shape
    n_ctx = ctx.shape[1]
    depth = ada_w.shape[0]
    assert n_ctx == TM and seq % (2 * ATT_KV_CHUNK) == 0 and seq % TM == 0
    lt = seq + n_ctx
    tiles_per_b = lt // TM
    lat_tiles_per_b = seq // TM
    ctx_tile = lat_tiles_per_b

    def mod_row(t):
        return jnp.where(t % tiles_per_b == ctx_tile, B, t // tiles_per_b)

    c_rows = jnp.concatenate([c, c_ctx[None, :], jnp.zeros((8 - B - 1, D), F32)], axis=0)
    mod = _ada(c_rows, ada_w, ada_b)
    cos_t, sin_t = _rope_tables(seq, n_ctx)
    xs = jnp.concatenate([x, ctx], axis=1).reshape(B * lt, D)

    for l in range(depth):
        last = l == depth - 1
        mod3 = mod[l].reshape(8, 1, 6 * D)
        p = _in_proj(xs, mod3, norm1_g[l].reshape(1, D), _reorder_w_in(w_in[l]), B * tiles_per_b, mod_row)

        lg = jnp.stack([jax.nn.log_sigmoid(ret_decay_fwd[l].astype(F32)),
                        jax.nn.log_sigmoid(ret_decay_bwd[l].astype(F32))])
        y_ret = _retention(p.reshape(B, lt, N_IN_PAD), lg, ret_gn_g[l].reshape(1, -1), ret_gn_b[l].reshape(1, -1),
                           seq, n_ctx)

        w_ukv = mla_w_ukv[l].reshape(MLA_KV_LORA, MLA_HEADS, MLA_NOPE + MLA_V)
        wk_p = _pad_heads(w_ukv[:, :, :MLA_NOPE].reshape(MLA_KV_LORA, -1), MLA_HEADS, MLA_NOPE).astype(BF16)
        wv = w_ukv[:, :, MLA_NOPE:].reshape(MLA_KV_LORA, MLA_HEADS * MLA_V).astype(BF16)
        wq_p = _pad_heads(mla_w_uq[l], MLA_HEADS, MLA_QK).astype(BF16)
        q, k, v = _mla_prep(p, cos_t, sin_t, mla_qa_g[l].reshape(1, -1), mla_kva_g[l].reshape(1, -1),
                            _pad_vec(mla_qn_g[l], 0), _pad_vec(mla_kn_g[l], 0), _pad_vec(mla_kr_g[l], MLA_NOPE),
                            wq_p, wk_p, wv, B, lt)
        o_mla = _attention(q, k, v, seq, n_ctx, lat_tiles_per_b if last else tiles_per_b)

        if last:
            n_tiles = B * lat_tiles_per_b
            tile = lambda t: (t // lat_tiles_per_b) * tiles_per_b + t % lat_tiles_per_b
        else:
            n_tiles = B * tiles_per_b
            tile = lambda t: t
        bs_full = jnp.broadcast_to(gmlp_bs[l][:, :, None], (GMLP_GROUPS, GMLP_CHUNK, GMLP_CHUNK))
        x1, h2 = _merge(xs, mod3, p, y_ret.reshape(B * lt, -1), o_mla.reshape(B * lt, -1),
                        gmlp_ln_g[l].reshape(1, -1), gmlp_ln_b[l].reshape(1, -1), gmlp_ws[l].astype(BF16), bs_full,
                        w_br_ret[l].astype(BF16), w_br_mla[l].astype(BF16), w_br_gmlp[l].astype(BF16),
                        w_out[l].astype(BF16), norm2_g[l].reshape(1, D), n_tiles, tile, mod_row)

        bias_t = jnp.broadcast_to(moe_bias[l][:, None], (N_EXPERTS, 128))
        idx, w, rank, counts, shared, hp = _route(h2, moe_router[l].T, bias_t,
                                                  sh_w_gate[l].astype(BF16), sh_w_up[l].astype(BF16),
                                                  sh_w_down[l].astype(BF16), n_tiles, tile)
        w = jnp.pad(w, ((0, 0), (0, 128 - TOP_K)))
        n_act = n_tiles * TM
        n_blocks = -(-(n_act * TOP_K + N_EXPERTS * (MOE_ROWS - 1)) // MOE_ROWS)
        blk_e, valid, pos_sc = _moe_plan(idx, rank, counts, n_blocks)
        xg = _dispatch(pos_sc, hp, n_blocks * MOE_ROWS)
        ys = _experts(blk_e, valid, xg, moe_w_gate, moe_w_up, moe_w_down, l, n_blocks)
        yg = _gather_rows(pos_sc, ys)
        if last:
            xs = _combine(yg, w, x1, shared, mod3, n_tiles, tile, mod_row, B * seq, lambda t: t)
        else:
            xs = _combine(yg, w, x1, shared, mod3, n_tiles, tile, mod_row, B * lt, tile)
    return xs.reshape(B, seq, D)
```

```python
import functools
import math

import jax
import jax.numpy as jnp
from jax import lax
from jax.experimental import pallas as pl
from jax.experimental.pallas import tpu as pltpu
from jax.experimental.pallas import tpu_sc as plsc

F32 = jnp.float32
BF16 = jnp.bfloat16

D = 1024
GRID_W = 64
RET_HEADS = 4
RET_D = 128
RET_CHUNK = 256
RET_OUT_ROWS = 256
MLA_HEADS = 8
MLA_Q_LORA = 384
MLA_KV_LORA = 256
MLA_NOPE = 64
MLA_ROPE = 32
MLA_V = 64
MLA_V_EXT = MLA_V + 16
MLA_QK = MLA_NOPE + MLA_ROPE
HEAD_PAD = 128
ROPE_THETA = 10000.0
GMLP_GROUPS = 4
GMLP_W = 512
GMLP_CHUNK = 128
N_EXPERTS = 64
TOP_K = 6
D_EXPERT = 256
ROUTED_SCALE = 2.5
EPS = 1e-6
LOG2_E = 1.4426950408889634

TM = 256
MOE_ROWS = 512
RT = 2 * TM
ATT_KV_CHUNK = 1024
ATT_UNROLL = 16
ATT_HEADS = 4

C_MERGE = 0
C_UV = 3072
C_RET = 4096
C_CKV = 6144
C_KR = 6400
C_CQ = 6528
N_IN_PAD = 6912
IN_CHUNK = 768

VMEM_LIMIT = 56 * 1024 * 1024

SC_CORES = 2
SC_SUBCORES = 16
SC_ROWS = 128


def _cparams(n_axes, vmem=VMEM_LIMIT):
    return pltpu.CompilerParams(dimension_semantics=("arbitrary",) * n_axes, vmem_limit_bytes=vmem)


def _silu(x):
    return x * jax.nn.sigmoid(x)


def _dot(a, b):
    return jnp.dot(a, b, preferred_element_type=F32)


def _dot_nt(a, b):
    return lax.dot_general(a, b, (((1,), (1,)), ((), ())), preferred_element_type=F32)


def _dot_tn(a, b):
    return lax.dot_general(a, b, (((0,), (0,)), ((), ())), preferred_element_type=F32)


def _pack_bf16_pairs(x):
    n = x.shape[1] // 2
    lo = lax.bitcast_convert_type(x[:, :n].astype(BF16).astype(F32), jnp.uint32)
    hi = lax.bitcast_convert_type(x[:, n:].astype(BF16).astype(F32), jnp.uint32)
    return (lo >> 16) | hi


def _unpack_bf16_pairs(u):
    lo = lax.bitcast_convert_type(u << 16, F32)
    hi = lax.bitcast_convert_type(u & jnp.uint32(0xFFFF0000), F32)
    return lo, hi


def _ada_kernel(c_ref, w_ref, b_ref, o_ref):
    s = _silu(c_ref[...])
    o_ref[...] = _dot(s.astype(BF16), w_ref[...].astype(BF16)) + b_ref[...]


def _ada(c_rows, ada_w, ada_b):
    depth = ada_w.shape[0]
    n = ada_w.shape[2]
    cw = 1536
    return pl.pallas_call(
        _ada_kernel,
        grid=(depth, n // cw),
        in_specs=[pl.BlockSpec((8, D), lambda l, j: (0, 0)),
                  pl.BlockSpec((None, D, cw), lambda l, j: (l, 0, j)),
                  pl.BlockSpec((None, 1, cw), lambda l, j: (l, 0, j))],
        out_specs=pl.BlockSpec((None, 8, cw), lambda l, j: (l, 0, j)),
        out_shape=jax.ShapeDtypeStruct((depth, 8, n), F32),
        compiler_params=_cparams(2),
        name="ada_mod",
    )(c_rows, ada_w, ada_b.reshape(depth, 1, n))


def _modulated_rmsnorm(x, g, shift, scale):
    y = x * lax.rsqrt(jnp.mean(x * x, axis=-1, keepdims=True) + EPS) * g
    return y * (1.0 + scale) + shift


def _project_in(x, mod_ref, g_ref, w_ref, o_ref, h_scr):
    h = _modulated_rmsnorm(x, g_ref[...], mod_ref[:, 0:D], mod_ref[:, D:2 * D])
    h_scr[...] = h.astype(BF16)
    for c in range(N_IN_PAD // IN_CHUNK):
        cols = slice(c * IN_CHUNK, (c + 1) * IN_CHUNK)
        o_ref[:, cols] = _dot(h_scr[...], w_ref[:, cols]).astype(BF16)


def _in_proj_kernel(x_ref, mod_ref, g_ref, w_ref, o_ref, h_scr):
    _project_in(x_ref[...], mod_ref, g_ref, w_ref, o_ref, h_scr)


def _in_proj(xs, mod3, g, w_in_r, n_tiles, mod_row):
    n_rows = xs.shape[0]
    return pl.pallas_call(
        _in_proj_kernel,
        grid=(n_tiles,),
        in_specs=[pl.BlockSpec((TM, D), lambda t: (t, 0)),
                  pl.BlockSpec((None, 1, 6 * D), lambda t: (mod_row(t), 0, 0)),
                  pl.BlockSpec((1, D), lambda t: (0, 0)),
                  pl.BlockSpec((D, N_IN_PAD), lambda t: (0, 0), pipeline_mode=pl.Buffered(1))],
        out_specs=pl.BlockSpec((TM, N_IN_PAD), lambda t: (t, 0)),
        out_shape=jax.ShapeDtypeStruct((n_rows, N_IN_PAD), BF16),
        scratch_shapes=[pltpu.VMEM((TM, D), BF16)],
        compiler_params=_cparams(1),
        name="in_proj",
    )(xs, mod3, g, w_in_r)


def _retention_kernel(lg_ref, q_ref, k_ref, v_ref, g_ref, gng_ref, gnb_ref, y_ref, of_scr, ob_scr,
                      *, n_lat_chunks, n_ctx_chunks):
    h = pl.program_id(1)
    lg_f = lg_ref[0, h]
    lg_b = lg_ref[1, h]
    C = RET_CHUNK
    k_scale = RET_D ** -0.5
    ri = lax.broadcasted_iota(jnp.int32, (C, C), 0).astype(F32)
    ci = lax.broadcasted_iota(jnp.int32, (C, C), 1).astype(F32)
    pos = lax.broadcasted_iota(jnp.int32, (C, 1), 0).astype(F32)
    diff = ri - ci
    d_f = jnp.where(diff >= 0, jnp.exp(lg_f * jnp.maximum(diff, 0.0)), 0.0) * k_scale
    d_b = jnp.where(diff < 0, jnp.exp(lg_b * jnp.maximum(-diff, 0.0)), 0.0) * k_scale
    qdec_f = jnp.exp(lg_f * (pos + 1.0))
    kdec_f = jnp.exp(lg_f * (C - 1.0 - pos)) * k_scale
    cdec_f = jnp.exp(lg_f * C)
    qdec_b = jnp.exp(lg_b * (C - pos))
    kdec_b = jnp.exp(lg_b * pos) * k_scale
    cdec_b = jnp.exp(lg_b * C)

    def chunk(c, state, dmat, qdec, kdec, cdec):
        rows = pl.ds(pl.multiple_of(c * C, C), C)
        q = q_ref[rows, :]
        k = k_ref[rows, :]
        v = v_ref[rows, :]
        att = (_dot_nt(q, k) * dmat).astype(BF16)
        o = _dot(att, v) + _dot((q.astype(F32) * qdec).astype(BF16), state.astype(BF16))
        kd = (k.astype(F32) * kdec).astype(BF16)
        return rows, o, state * cdec + _dot_tn(kd, v)

    n_all = n_lat_chunks + n_ctx_chunks

    def scan_body(i, states):
        s_f, s_b = states
        c_f = jnp.where(i < n_ctx_chunks, n_lat_chunks + i, i - n_ctx_chunks)
        rows, o, s_f = chunk(c_f, s_f, d_f, qdec_f, kdec_f, cdec_f)
        of_scr[rows, :] = o
        rows, o, s_b = chunk(n_all - 1 - i, s_b, d_b, qdec_b, kdec_b, cdec_b)
        ob_scr[rows, :] = o
        return s_f, s_b

    zero = jnp.zeros((RET_D, RET_D), F32)
    lax.fori_loop(0, n_all, scan_body, (zero, zero), unroll=2)

    def out_body(c, _):
        rows = pl.ds(pl.multiple_of(c * RET_OUT_ROWS, RET_OUT_ROWS), RET_OUT_ROWS)
        o = of_scr[rows, :] + ob_scr[rows, :]
        mu = jnp.mean(o, axis=-1, keepdims=True)
        var = jnp.mean(jnp.square(o - mu), axis=-1, keepdims=True)
        on = (o - mu) * lax.rsqrt(var + EPS)
        y = _silu(g_ref[rows, :].astype(F32)) * (on * gng_ref[...] + gnb_ref[...])
        y_ref[rows, :] = y.astype(BF16)
        return 0

    lax.fori_loop(0, n_all * C // RET_OUT_ROWS, out_body, 0, unroll=3)


def _retention(p3, lg, gn_g, gn_b, seq, ctx):
    B, lt, _ = p3.shape
    base = C_RET // RET_D
    kern = functools.partial(_retention_kernel, n_lat_chunks=seq // RET_CHUNK, n_ctx_chunks=ctx // RET_CHUNK)

    def col(off):
        return pl.BlockSpec((None, lt, RET_D), lambda b, h: (b, 0, base + off * RET_HEADS + h))

    return pl.pallas_call(
        kern,
        grid=(B, RET_HEADS),
        in_specs=[pl.BlockSpec(memory_space=pltpu.SMEM),
                  col(0), col(1), col(2), col(3),
                  pl.BlockSpec((1, RET_D), lambda b, h: (0, h)),
                  pl.BlockSpec((1, RET_D), lambda b, h: (0, h))],
        out_specs=pl.BlockSpec((None, lt, RET_D), lambda b, h: (b, 0, h)),
        out_shape=jax.ShapeDtypeStruct((B, lt, RET_HEADS * RET_D), BF16),
        scratch_shapes=[pltpu.VMEM((lt, RET_D), F32), pltpu.VMEM((lt, RET_D), F32)],
        compiler_params=_cparams(2),
        name="retention",
    )(lg, p3, p3, p3, p3, gn_g, gn_b)


def _rope_rotate(x, first_half):
    return jnp.where(first_half, pltpu.roll(x, HEAD_PAD - 8, 1), pltpu.roll(x, 8, 1))


def _mla_prep_kernel(cq_ref, ckv_ref, kr_ref, cos_ref, sin_ref, qa_ref, kva_ref, qn_ref, kn_ref, krg_ref,
                     wq_ref, wk_ref, wv_ref, q_ref, k_ref, v_ref):
    lane = lax.broadcasted_iota(jnp.int32, (1, HEAD_PAD), 1)
    first_half = (lane % 16) < 8
    cos = cos_ref[...]
    sin = sin_ref[...]

    def rms(x, n):
        return x * lax.rsqrt(jnp.sum(x * x, axis=-1, keepdims=True) * (1.0 / n) + EPS)

    def rope(x):
        return x * cos + _rope_rotate(x, first_half) * sin

    cq = cq_ref[...].astype(F32)
    cqn = (rms(cq, MLA_Q_LORA) * qa_ref[...]).astype(BF16)
    q_all = _dot(cqn, wq_ref[...])
    ckv = ckv_ref[...].astype(F32)
    ckvn = (rms(ckv, MLA_KV_LORA) * kva_ref[...]).astype(BF16)
    k_all = _dot(ckvn, wk_ref[...])
    v_all = _dot(ckvn, wv_ref[...])
    k_rope = rope(rms(kr_ref[...].astype(F32), MLA_ROPE) * krg_ref[...])
    scale = MLA_QK ** -0.5 * LOG2_E
    v_t = v_all.T
    ones_row = jnp.where(lax.broadcasted_iota(jnp.int32, (MLA_V_EXT - MLA_V, TM), 0) == 0, 1.0, 0.0)
    for h in range(MLA_HEADS):
        cols = slice(h * HEAD_PAD, (h + 1) * HEAD_PAD)
        qh = rope(rms(q_all[:, cols], MLA_QK) * qn_ref[...]) * scale
        q_ref[h] = qh.astype(BF16)
        kh = rms(k_all[:, cols], MLA_NOPE) * kn_ref[...] + k_rope
        k_ref[h] = kh.astype(BF16)
        v_ref[h] = jnp.concatenate([v_t[h * MLA_V:(h + 1) * MLA_V, :], ones_row], axis=0).astype(BF16)


def _mla_prep(p, cos_t, sin_t, qa_g, kva_g, qn_p, kn_p, kr_p, wq_p, wk_p, wv, B, lt):
    tiles_per_b = lt // TM
    hw = MLA_HEADS * HEAD_PAD
    const = lambda shape: pl.BlockSpec(shape, lambda b, j: (0,) * len(shape))
    head_out = pl.BlockSpec((None, MLA_HEADS, TM, HEAD_PAD), lambda b, j: (b, 0, j, 0))
    shp = jax.ShapeDtypeStruct((B, MLA_HEADS, lt, HEAD_PAD), BF16)
    v_out = pl.BlockSpec((None, MLA_HEADS, None, MLA_V_EXT, TM), lambda b, j: (b, 0, j, 0, 0))
    v_shp = jax.ShapeDtypeStruct((B, MLA_HEADS, tiles_per_b, MLA_V_EXT, TM), BF16)
    return pl.pallas_call(
        _mla_prep_kernel,
        grid=(B, tiles_per_b),
        in_specs=[pl.BlockSpec((TM, MLA_Q_LORA), lambda b, j: (b * tiles_per_b + j, C_CQ // MLA_Q_LORA)),
                  pl.BlockSpec((TM, MLA_KV_LORA), lambda b, j: (b * tiles_per_b + j, C_CKV // MLA_KV_LORA)),
                  pl.BlockSpec((TM, HEAD_PAD), lambda b, j: (b * tiles_per_b + j, C_KR // HEAD_PAD)),
                  pl.BlockSpec((TM, HEAD_PAD), lambda b, j: (j, 0)),
                  pl.BlockSpec((TM, HEAD_PAD), lambda b, j: (j, 0)),
                  const((1, MLA_Q_LORA)), const((1, MLA_KV_LORA)),
                  const((1, HEAD_PAD)), const((1, HEAD_PAD)), const((1, HEAD_PAD)),
                  const((MLA_Q_LORA, hw)), const((MLA_KV_LORA, hw)), const((MLA_KV_LORA, MLA_HEADS * MLA_V))],
        out_specs=[head_out, head_out, v_out],
        out_shape=[shp, shp, v_shp],
        compiler_params=_cparams(2),
        name="mla_prep",
    )(p, p, p, cos_t, sin_t, qa_g, kva_g, qn_p, kn_p, kr_p, wq_p, wk_p, wv)


def _attention_kernel(q_ref, k_ref, v_ref, o_ref, s_scr, *, seq, ctx, ctx_tile):
    i = pl.program_id(2)
    n_blk = ATT_KV_CHUNK // TM
    n_chunks = seq // ATT_KV_CHUNK
    unroll = math.gcd(n_chunks, ATT_UNROLL)

    def scores(hh, slot, blk, nb):
        start = blk * TM if isinstance(blk, int) else pl.multiple_of(blk * TM, TM)
        s_scr[hh, slot, 0:nb * TM, :] = _dot_nt(k_ref[hh, pl.ds(start, nb * TM), :], q_ref[hh])

    def absorb(hh, slot, blk, nb, carry):
        m, acc = carry
        s = s_scr[hh, slot, 0:nb * TM, :]
        m_new = jnp.maximum(m, jnp.max(s, axis=0, keepdims=True))
        p = jnp.exp2(s - m_new).astype(BF16)
        acc = jnp.exp2(m - m_new) * acc
        for j in range(nb):
            acc = acc + _dot(v_ref[hh, blk + j], p[j * TM:(j + 1) * TM, :])
        return m_new, acc

    def init():
        return (jnp.full((1, TM), -jnp.inf, F32), jnp.zeros((MLA_V_EXT, TM), F32))

    def write(carries):
        outs = [acc[0:MLA_V, :] / acc[MLA_V:MLA_V + 1, :] for _, acc in carries]
        o_ref[...] = jnp.concatenate(outs, axis=0).T.astype(BF16)

    def step(carries, slot, blk, nb, next_blk, next_nb):
        out = []
        for hh in range(ATT_HEADS):
            if next_blk is not None:
                scores(hh, 1 - slot, next_blk, next_nb)
            out.append(absorb(hh, slot, blk, nb, carries[hh]))
        return tuple(out)

    ctx_blk = seq // TM
    ctx_nb = ctx // TM

    @pl.when(i != ctx_tile)
    def _():
        for hh in range(ATT_HEADS):
            scores(hh, 0, ctx_blk, ctx_nb)
        carries = step(tuple(init() for _ in range(ATT_HEADS)), 0, ctx_blk, ctx_nb, 0, n_blk)
        last_blk = (n_chunks - 1) * n_blk

        def body(c, carries):
            for u in range(unroll):
                blk = (c * unroll + u) * n_blk
                carries = step(carries, (1 + u) % 2, blk, n_blk, jnp.minimum(blk + n_blk, last_blk), n_blk)
            return carries

        if n_chunks == unroll:
            for u in range(n_chunks):
                nxt = (u + 1) * n_blk if u + 1 < n_chunks else None
                carries = step(carries, (1 + u) % 2, u * n_blk, n_blk, nxt, n_blk)
            write(carries)
        else:
            write(lax.fori_loop(0, n_chunks // unroll, body, carries))

    @pl.when(i == ctx_tile)
    def _():
        for hh in range(ATT_HEADS):
            scores(hh, 0, ctx_blk, ctx_nb)
        write(step(tuple(init() for _ in range(ATT_HEADS)), 0, ctx_blk, ctx_nb, None, None))


def _attention(q, k, v, seq, ctx, n_q_tiles):
    B, H, lt, _ = q.shape
    kern = functools.partial(_attention_kernel, seq=seq, ctx=ctx, ctx_tile=seq // TM)
    return pl.pallas_call(
        kern,
        grid=(B, H // ATT_HEADS, n_q_tiles),
        in_specs=[pl.BlockSpec((None, ATT_HEADS, TM, HEAD_PAD), lambda b, h, i: (b, h, i, 0)),
                  pl.BlockSpec((None, ATT_HEADS, lt, HEAD_PAD), lambda b, h, i: (b, h, 0, 0)),
                  pl.BlockSpec((None, ATT_HEADS, lt // TM, MLA_V_EXT, TM), lambda b, h, i: (b, h, 0, 0, 0))],
        out_specs=pl.BlockSpec((None, TM, ATT_HEADS * MLA_V), lambda b, h, i: (b, i, h)),
        out_shape=jax.ShapeDtypeStruct((B, lt, H * MLA_V), BF16),
        scratch_shapes=[pltpu.VMEM((ATT_HEADS, 2, ATT_KV_CHUNK, TM), F32)],
        compiler_params=_cparams(3),
        name="attention",
    )(q, k, v)


def _merge_kernel(x_ref, mod_ref, mg_ref, uv_ref, yr_ref, om_ref, lng_ref, lnb_ref, ws_ref, bs_ref,
                  wr_ref, wm_ref, wg_ref, wo_ref, n2_ref, x1_ref, h2_ref):
    yr = _dot(yr_ref[...], wr_ref[...])
    ym = _dot(om_ref[...], wm_ref[...])
    z = jax.nn.gelu(uv_ref[...].astype(F32))
    u = z[:, :GMLP_W]
    v = z[:, GMLP_W:]
    mu = jnp.mean(v, axis=-1, keepdims=True)
    var = jnp.mean(jnp.square(v - mu), axis=-1, keepdims=True)
    vn = ((v - mu) * lax.rsqrt(var + EPS) * lng_ref[...] + lnb_ref[...]).astype(BF16)
    gw = GMLP_W // GMLP_GROUPS
    chunks = []
    for c in range(TM // GMLP_CHUNK):
        rows = slice(c * GMLP_CHUNK, (c + 1) * GMLP_CHUNK)
        groups = [_dot(ws_ref[g], vn[rows, g * gw:(g + 1) * gw]) + bs_ref[g] for g in range(GMLP_GROUPS)]
        chunks.append(jnp.concatenate(groups, axis=1))
    sv = jnp.concatenate(chunks, axis=0)
    yg = _dot((u * sv).astype(BF16), wg_ref[...])
    gate = jax.nn.sigmoid(mg_ref[...].astype(F32))
    y = gate[:, :D] * yr + gate[:, D:2 * D] * ym + gate[:, 2 * D:] * yg
    out = _dot(y.astype(BF16), wo_ref[...])
    x1 = x_ref[...] + mod_ref[:, 2 * D:3 * D] * out
    x1_ref[...] = x1
    h2_ref[...] = _modulated_rmsnorm(x1, n2_ref[...], mod_ref[:, 3 * D:4 * D], mod_ref[:, 4 * D:5 * D])


def _merge(xs, mod3, p, y_ret, o_mla, ln_g, ln_b, ws, bs_full, w_br_ret, w_br_mla, w_br_gmlp, w_out, n2_g,
           n_tiles, tile, mod_row):
    n_rows = xs.shape[0]
    const = lambda shape: pl.BlockSpec(shape, lambda t: (0,) * len(shape))
    row = lambda w, cb=0: pl.BlockSpec((TM, w), lambda t: (tile(t), cb))
    shp = jax.ShapeDtypeStruct((n_rows, D), F32)
    return pl.pallas_call(
        _merge_kernel,
        grid=(n_tiles,),
        in_specs=[row(D),
                  pl.BlockSpec((None, 1, 6 * D), lambda t: (mod_row(tile(t)), 0, 0)),
                  row(3 * D, C_MERGE // (3 * D)), row(D, C_UV // D),
                  row(RET_HEADS * RET_D), row(MLA_HEADS * MLA_V),
                  const((1, GMLP_W)), const((1, GMLP_W)),
                  const((GMLP_GROUPS, GMLP_CHUNK, GMLP_CHUNK)), const((GMLP_GROUPS, GMLP_CHUNK, GMLP_CHUNK)),
                  const((RET_HEADS * RET_D, D)), const((MLA_HEADS * MLA_V, D)), const((GMLP_W, D)),
                  const((D, D)), const((1, D))],
        out_specs=[row(D), row(D)],
        out_shape=[shp, shp],
        compiler_params=_cparams(1),
        name="merge",
    )(xs, mod3, p, p, y_ret, o_mla, ln_g, ln_b, ws, bs_full, w_br_ret, w_br_mla, w_br_gmlp, w_out, n2_g)


def _route_kernel(ha_ref, hb_ref, rt_ref, bt_ref, sg_ref, su_ref, sd_ref, idx_ref, w_ref, rank_ref, cnt_ref, sh_ref,
                  hp_ref, cnt_scr):
    @pl.when(pl.program_id(0) == 0)
    def _():
        cnt_scr[...] = jnp.zeros_like(cnt_scr)

    h = jnp.concatenate([ha_ref[...], hb_ref[...]], axis=0)
    logits = lax.dot_general(rt_ref[...], h, (((1,), (1,)), ((), ())), preferred_element_type=F32,
                             precision=lax.Precision.HIGHEST)
    scores = jax.nn.sigmoid(logits)
    sel = scores + bt_ref[:, 0:1]
    row_e = lax.broadcasted_iota(jnp.int32, (N_EXPERTS, RT), 0).astype(F32)
    row_o = lax.broadcasted_iota(jnp.int32, (8, RT), 0)
    idx_out = jnp.zeros((8, RT), F32)
    w_out = jnp.zeros((8, RT), F32)
    hits = []
    for k in range(TOP_K):
        best = jnp.max(sel, axis=0, keepdims=True)
        pick = jnp.min(jnp.where(sel == best, row_e, float(N_EXPERTS)), axis=0, keepdims=True)
        hit = row_e == pick
        hits.append(hit)
        wk = jnp.sum(jnp.where(hit, scores, 0.0), axis=0, keepdims=True)
        sel = jnp.where(hit, -jnp.inf, sel)
        idx_out = jnp.where(row_o == k, pick, idx_out)
        w_out = jnp.where(row_o == k, wk, w_out)
    w_out = w_out / jnp.sum(w_out, axis=0, keepdims=True) * ROUTED_SCALE
    idx_ref[...] = idx_out.astype(jnp.int32)
    w_ref[...] = w_out
    chosen = jnp.zeros((N_EXPERTS, RT), F32)
    for hit in hits:
        chosen = jnp.where(hit, 1.0, chosen)
    earlier = (lax.broadcasted_iota(jnp.int32, (RT, RT), 0) < lax.broadcasted_iota(jnp.int32, (RT, RT), 1))
    before = _dot(chosen.astype(BF16), jnp.where(earlier, 1.0, 0.0).astype(BF16)) + cnt_scr[:, 0:1]
    rank_out = jnp.zeros((8, RT), F32)
    for k, hit in enumerate(hits):
        rank_out = jnp.where(row_o == k, jnp.sum(jnp.where(hit, before, 0.0), axis=0, keepdims=True), rank_out)
    rank_ref[...] = rank_out.astype(jnp.int32)
    cnt_scr[...] += jnp.sum(chosen, axis=1, keepdims=True)
    cnt_ref[...] = cnt_scr[...]
    hb = h.astype(BF16)
    a = _silu(_dot(hb, sg_ref[...])) * _dot(hb, su_ref[...])
    sh_ref[...] = _dot(a.astype(BF16), sd_ref[...])
    hp_ref[...] = _pack_bf16_pairs(h)


def _route(h2, router_t, bias_t, sg, su, sd, n_tiles, tile):
    const = lambda shape: pl.BlockSpec(shape, lambda t: (0,) * len(shape))
    n_act = n_tiles * TM
    assert n_tiles % 2 == 0
    n_steps = n_tiles // 2
    per_tok = pl.BlockSpec((None, 8, RT), lambda t: (t, 0, 0))
    idx_t, w_t, rank_t, counts, shared, hp = pl.pallas_call(
        _route_kernel,
        grid=(n_steps,),
        in_specs=[pl.BlockSpec((TM, D), lambda t: (tile(2 * t), 0)),
                  pl.BlockSpec((TM, D), lambda t: (tile(2 * t + 1), 0)),
                  const((N_EXPERTS, D)), const((N_EXPERTS, 128)),
                  const((D, D_EXPERT)), const((D, D_EXPERT)), const((D_EXPERT, D))],
        out_specs=[per_tok, per_tok, per_tok,
                   pl.BlockSpec((N_EXPERTS, 128), lambda t: (0, 0)),
                   pl.BlockSpec((RT, D), lambda t: (t, 0)),
                   pl.BlockSpec((RT, D // 2), lambda t: (t, 0))],
        out_shape=[jax.ShapeDtypeStruct((n_steps, 8, RT), jnp.int32),
                   jax.ShapeDtypeStruct((n_steps, 8, RT), F32),
                   jax.ShapeDtypeStruct((n_steps, 8, RT), jnp.int32),
                   jax.ShapeDtypeStruct((N_EXPERTS, 128), F32),
                   jax.ShapeDtypeStruct((n_act, D), F32),
                   jax.ShapeDtypeStruct((n_act, D // 2), jnp.uint32)],
        scratch_shapes=[pltpu.VMEM((N_EXPERTS, 128), F32)],
        compiler_params=_cparams(1),
        name="route_shared",
    )(h2, h2, router_t, bias_t, sg, su, sd)
    token_major = lambda a: a.transpose(0, 2, 1).reshape(n_act, 8)[:, :TOP_K]
    return token_major(idx_t), token_major(w_t), token_major(rank_t), counts[:, 0], shared, hp


def _dispatch(pos_sc, hp, n_rows):
    n_batches = pos_sc.shape[0]
    n_workers = SC_CORES * SC_SUBCORES
    mesh = plsc.VectorSubcoreMesh(core_axis_name="c", subcore_axis_name="s")

    @functools.partial(
        pl.kernel, mesh=mesh,
        out_type=jax.ShapeDtypeStruct((n_rows, D // 2), jnp.uint32),
        scratch_types=[pltpu.VMEM((TOP_K, SC_ROWS), jnp.int32),
                       pltpu.VMEM((SC_ROWS, D // 2), jnp.uint32),
                       pltpu.SemaphoreType.DMA],
        name="moe_dispatch")
    def scatter(pos_hbm, h_hbm, xs_hbm, idx_v, rows_v, sem):
        worker = lax.axis_index("s") * SC_CORES + lax.axis_index("c")

        @pl.loop(0, pl.cdiv(n_batches, n_workers))
        def _(j):
            b = j * n_workers + worker

            @pl.when(b < n_batches)
            def _():
                pltpu.sync_copy(pos_hbm.at[b], idx_v)
                pltpu.sync_copy(h_hbm.at[pl.ds(b * SC_ROWS, SC_ROWS)], rows_v)
                copies = [pltpu.async_copy(rows_v, xs_hbm.at[idx_v.at[k]], sem) for k in range(TOP_K)]
                for cp in copies:
                    cp.wait()

    return scatter(pos_sc, hp)


def _expert_kernel(blk_e_ref, valid_ref, x_ref, wg_ref, wu_ref, wd_ref, y_ref, wg_s, wu_s, wd_s):
    i = pl.program_id(0)
    n_valid = valid_ref[i]

    @pl.when(n_valid > 0)
    def _():
        @pl.when(jnp.logical_or(i == 0, blk_e_ref[i] != blk_e_ref[jnp.maximum(i - 1, 0)]))
        def _():
            wg_s[...] = wg_ref[...].astype(BF16)
            wu_s[...] = wu_ref[...].astype(BF16)
            wd_s[...] = wd_ref[...].astype(BF16)

        row = lax.broadcasted_iota(jnp.int32, (MOE_ROWS, 1), 0)
        lo, hi = _unpack_bf16_pairs(jnp.where(row < n_valid, x_ref[...], jnp.uint32(0)))
        x = jnp.concatenate([lo, hi], axis=1).astype(BF16)
        hb = _silu(_dot(x, wg_s[...])) * _dot(x, wu_s[...])
        y_ref[...] = _pack_bf16_pairs(_dot(hb.astype(BF16), wd_s[...]))

    @pl.when(n_valid == 0)
    def _():
        y_ref[...] = jnp.zeros_like(y_ref)


def _experts(blk_e, valid, xs, wg, wu, wd, layer, n_blocks):
    grid_spec = pltpu.PrefetchScalarGridSpec(
        num_scalar_prefetch=2,
        grid=(n_blocks,),
        in_specs=[pl.BlockSpec((MOE_ROWS, D // 2), lambda i, be, nv: (i, 0)),
                  pl.BlockSpec((None, None, D, D_EXPERT), lambda i, be, nv: (layer, be[i], 0, 0)),
                  pl.BlockSpec((None, None, D, D_EXPERT), lambda i, be, nv: (layer, be[i], 0, 0)),
                  pl.BlockSpec((None, None, D_EXPERT, D), lambda i, be, nv: (layer, be[i], 0, 0))],
        out_specs=pl.BlockSpec((MOE_ROWS, D // 2), lambda i, be, nv: (i, 0)),
        scratch_shapes=[pltpu.VMEM((D, D_EXPERT), BF16), pltpu.VMEM((D, D_EXPERT), BF16),
                        pltpu.VMEM((D_EXPERT, D), BF16)],
    )
    return pl.pallas_call(
        _expert_kernel,
        grid_spec=grid_spec,
        out_shape=jax.ShapeDtypeStruct((n_blocks * MOE_ROWS, D // 2), jnp.uint32),
        compiler_params=_cparams(1),
        name="routed_experts",
    )(blk_e, valid, xs, wg, wu, wd)


def _gather_rows(pos_sc, ys):
    n_batches = pos_sc.shape[0]
    n_workers = SC_CORES * SC_SUBCORES
    half = SC_ROWS // 2
    mesh = plsc.VectorSubcoreMesh(core_axis_name="c", subcore_axis_name="s")

    @functools.partial(
        pl.kernel, mesh=mesh,
        out_type=jax.ShapeDtypeStruct((TOP_K, n_batches * SC_ROWS, D // 2), jnp.uint32),
        scratch_types=[pltpu.VMEM((TOP_K, SC_ROWS), jnp.int32),
                       pltpu.VMEM((2, half, D // 2), jnp.uint32),
                       pltpu.SemaphoreType.DMA, pltpu.SemaphoreType.DMA],
        name="moe_gather")
    def gather(pos_hbm, y_hbm, out_hbm, idx_v, bufs, gsem, wsem):
        worker = lax.axis_index("s") * SC_CORES + lax.axis_index("c")

        @pl.loop(0, pl.cdiv(n_batches, n_workers))
        def _(j):
            b = j * n_workers + worker

            @pl.when(b < n_batches)
            def _():
                pltpu.sync_copy(pos_hbm.at[b], idx_v)
                items = [(k, h) for k in range(TOP_K) for h in range(2)]

                def fetch(i):
                    k, h = items[i]
                    return pltpu.async_copy(y_hbm.at[idx_v.at[k, pl.ds(h * half, half)]], bufs.at[i % 2], gsem)

                pending_gather = fetch(0)
                pending_write = None
                for i, (k, h) in enumerate(items):
                    pending_gather.wait()
                    if pending_write is not None:
                        pending_write.wait()
                    if i + 1 < len(items):
                        pending_gather = fetch(i + 1)
                    pending_write = pltpu.async_copy(
                        bufs.at[i % 2], out_hbm.at[k, pl.ds(b * SC_ROWS + h * half, half)], wsem)
                pending_write.wait()

    return gather(pos_sc, ys)


def _combine_tile(y_refs, w_ref, x1_ref, sh_ref, mod_ref):
    f_lo = sh_ref[:, :D // 2]
    f_hi = sh_ref[:, D // 2:]
    for k in range(TOP_K):
        lo, hi = _unpack_bf16_pairs(y_refs[k][...])
        f_lo = f_lo + lo * w_ref[:, k:k + 1]
        f_hi = f_hi + hi * w_ref[:, k:k + 1]
    return x1_ref[...] + mod_ref[:, 5 * D:6 * D] * jnp.concatenate([f_lo, f_hi], axis=1)


def _combine_kernel(*refs):
    o_ref = refs[-1]
    o_ref[...] = _combine_tile(refs[:TOP_K], *refs[TOP_K:-1])


def _combine_specs(tile, mod_row):
    planes = [pl.BlockSpec((None, TM, D // 2), lambda t, k=k: (k, t, 0)) for k in range(TOP_K)]
    return planes + [pl.BlockSpec((TM, 128), lambda t: (t, 0)),
                     pl.BlockSpec((TM, D), lambda t: (tile(t), 0)),
                     pl.BlockSpec((TM, D), lambda t: (t, 0)),
                     pl.BlockSpec((None, 1, 6 * D), lambda t: (mod_row(tile(t)), 0, 0))]


def _combine(yg, w, x1, shared, mod3, n_tiles, tile, mod_row, out_rows, out_tile):
    return pl.pallas_call(
        _combine_kernel,
        grid=(n_tiles,),
        in_specs=_combine_specs(tile, mod_row),
        out_specs=pl.BlockSpec((TM, D), lambda t: (out_tile(t), 0)),
        out_shape=jax.ShapeDtypeStruct((out_rows, D), F32),
        compiler_params=_cparams(1),
        name="moe_combine",
    )(*([yg] * TOP_K), w, x1, shared, mod3)


def _combine_in_proj_kernel(*refs):
    xs_ref, o_ref, h_scr = refs[-3:]
    mod_ref, g_ref, w_ref = refs[TOP_K + 4:-3]
    x = _combine_tile(refs[:TOP_K], *refs[TOP_K:TOP_K + 4])
    xs_ref[...] = x
    _project_in(x, mod_ref, g_ref, w_ref, o_ref, h_scr)


def _combine_in_proj(yg, w, x1, shared, mod3_prev, mod3, g, w_in_r, n_tiles, mod_row):
    n_rows = x1.shape[0]
    ident = lambda t: t
    return pl.pallas_call(
        _combine_in_proj_kernel,
        grid=(n_tiles,),
        in_specs=_combine_specs(ident, mod_row) + [
            pl.BlockSpec((None, 1, 6 * D), lambda t: (mod_row(t), 0, 0)),
            pl.BlockSpec((1, D), lambda t: (0, 0)),
            pl.BlockSpec((D, N_IN_PAD), lambda t: (0, 0), pipeline_mode=pl.Buffered(1))],
        out_specs=[pl.BlockSpec((TM, D), lambda t: (t, 0)),
                   pl.BlockSpec((TM, N_IN_PAD), lambda t: (t, 0))],
        out_shape=[jax.ShapeDtypeStruct((n_rows, D), F32),
                   jax.ShapeDtypeStruct((n_rows, N_IN_PAD), BF16)],
        scratch_shapes=[pltpu.VMEM((TM, D), BF16)],
        compiler_params=_cparams(1),
        name="combine_in_proj",
    )(*([yg] * TOP_K), w, x1, shared, mod3_prev, mod3, g, w_in_r)


def _moe_plan(idx, rank, counts, n_blocks):
    n = idx.shape[0]
    cnt = counts.reshape(N_EXPERTS).astype(jnp.int32)
    padded = (cnt + MOE_ROWS - 1) // MOE_ROWS * MOE_ROWS
    pad_end = jnp.cumsum(padded)
    pad_start = pad_end - padded
    experts = jnp.arange(N_EXPERTS, dtype=jnp.int32)
    pos = rank + jnp.sum(jnp.where(idx[:, :, None] == experts, pad_start, 0), axis=-1)
    blk_start = jnp.arange(n_blocks, dtype=jnp.int32) * MOE_ROWS
    blk_e = jnp.minimum(jnp.sum(blk_start[:, None] >= pad_end[None, :], axis=1), N_EXPERTS - 1).astype(jnp.int32)
    mine = blk_e[:, None] == experts
    in_expert = blk_start - jnp.sum(jnp.where(mine, pad_start, 0), axis=1)
    valid = jnp.clip(jnp.sum(jnp.where(mine, cnt, 0), axis=1) - in_expert, 0, MOE_ROWS).astype(jnp.int32)
    pos_sc = pos.astype(jnp.int32).reshape(n // SC_ROWS, SC_ROWS, TOP_K).transpose(0, 2, 1)
    return blk_e, valid, pos_sc


def _rope_tables(seq, ctx):
    half = MLA_ROPE // 2
    n_freq = half // 2
    inv = ROPE_THETA ** (-2.0 * jnp.arange(n_freq, dtype=F32) / half)
    t = jnp.arange(seq)
    ang_r = (t // GRID_W).astype(F32)[:, None] * inv
    ang_c = (t % GRID_W).astype(F32)[:, None] * inv
    cos = jnp.concatenate([jnp.cos(ang_r), jnp.cos(ang_r), jnp.cos(ang_c), jnp.cos(ang_c)], axis=1)
    sin = jnp.concatenate([-jnp.sin(ang_r), jnp.sin(ang_r), -jnp.sin(ang_c), jnp.sin(ang_c)], axis=1)
    pad_l = MLA_NOPE
    pad_r = HEAD_PAD - MLA_NOPE - MLA_ROPE
    cos = jnp.pad(cos, ((0, ctx), (pad_l, pad_r)), constant_values=1.0)
    cos = cos.at[seq:, :].set(1.0)
    sin = jnp.pad(sin, ((0, ctx), (pad_l, pad_r)))
    return cos, sin


def _pad_heads(w, n_heads, width, offset=0):
    k = w.shape[0]
    w = w.reshape(k, n_heads, width)
    w = jnp.pad(w, ((0, 0), (0, 0), (offset, HEAD_PAD - width - offset)))
    return w.reshape(k, n_heads * HEAD_PAD)


def _pad_vec(g, offset):
    return jnp.pad(g, (offset, HEAD_PAD - g.shape[0] - offset)).reshape(1, HEAD_PAD)


def _reorder_w_in(w):
    off_cq, off_ckv, off_kr, off_uv, off_merge = 2048, 2432, 2688, 2720, 3744
    kr = jnp.pad(w[:, off_kr:off_uv], ((0, 0), (MLA_NOPE, HEAD_PAD - MLA_NOPE - MLA_ROPE)))
    return jnp.concatenate([w[:, off_merge:], w[:, off_uv:off_merge], w[:, :off_cq],
                            w[:, off_ckv:off_kr], kr, w[:, off_cq:off_ckv]], axis=1).astype(BF16)


def kernel(x, c, ctx, c_ctx, ada_w, ada_b, norm1_g, norm2_g, w_in, ret_decay_fwd, ret_decay_bwd, ret_gn_g,
           ret_gn_b, w_br_ret, mla_qa_g, mla_w_uq, mla_kva_g, mla_w_ukv, mla_qn_g, mla_kn_g, mla_kr_g, w_br_mla,
           gmlp_ln_g, gmlp_ln_b, gmlp_ws, gmlp_bs, w_br_gmlp, w_out, moe_router, moe_bias, moe_w_gate, moe_w_up,
           moe_w_down, sh_w_gate, sh_w_up, sh_w_down):
    B, seq, _ = x.shape
    n_ctx = ctx.shape[1]
    depth = ada_w.shape[0]
    assert n_ctx == TM and seq % (2 * ATT_KV_CHUNK) == 0 and seq % TM == 0
    lt = seq + n_ctx
    tiles_per_b = lt // TM
    lat_tiles_per_b = seq // TM
    ctx_tile = lat_tiles_per_b

    def mod_row(t):
        return jnp.where(t % tiles_per_b == ctx_tile, B, t // tiles_per_b)

    c_rows = jnp.concatenate([c, c_ctx[None, :], jnp.zeros((8 - B - 1, D), F32)], axis=0)
    mod = _ada(c_rows, ada_w, ada_b)
    cos_t, sin_t = _rope_tables(seq, n_ctx)
    xs = jnp.concatenate([x, ctx], axis=1).reshape(B * lt, D)

    pending = None
    for l in range(depth):
        last = l == depth - 1
        mod3 = mod[l].reshape(8, 1, 6 * D)
        if pending is None:
            p = _in_proj(xs, mod3, norm1_g[l].reshape(1, D), _reorder_w_in(w_in[l]), B * tiles_per_b, mod_row)
        else:
            xs, p = _combine_in_proj(*pending, mod3, norm1_g[l].reshape(1, D), _reorder_w_in(w_in[l]),
                                     B * tiles_per_b, mod_row)

        lg = jnp.stack([jax.nn.log_sigmoid(ret_decay_fwd[l].astype(F32)),
                        jax.nn.log_sigmoid(ret_decay_bwd[l].astype(F32))])
        y_ret = _retention(p.reshape(B, lt, N_IN_PAD), lg, ret_gn_g[l].reshape(1, -1), ret_gn_b[l].reshape(1, -1),
                           seq, n_ctx)

        w_ukv = mla_w_ukv[l].reshape(MLA_KV_LORA, MLA_HEADS, MLA_NOPE + MLA_V)
        wk_p = _pad_heads(w_ukv[:, :, :MLA_NOPE].reshape(MLA_KV_LORA, -1), MLA_HEADS, MLA_NOPE).astype(BF16)
        wv = w_ukv[:, :, MLA_NOPE:].reshape(MLA_KV_LORA, MLA_HEADS * MLA_V).astype(BF16)
        wq_p = _pad_heads(mla_w_uq[l], MLA_HEADS, MLA_QK).astype(BF16)
        q, k, v = _mla_prep(p, cos_t, sin_t, mla_qa_g[l].reshape(1, -1), mla_kva_g[l].reshape(1, -1),
                            _pad_vec(mla_qn_g[l], 0), _pad_vec(mla_kn_g[l], 0), _pad_vec(mla_kr_g[l], MLA_NOPE),
                            wq_p, wk_p, wv, B, lt)
        o_mla = _attention(q, k, v, seq, n_ctx, lat_tiles_per_b if last else tiles_per_b)

        if last:
            n_tiles = B * lat_tiles_per_b
            tile = lambda t: (t // lat_tiles_per_b) * tiles_per_b + t % lat_tiles_per_b
        else:
            n_tiles = B * tiles_per_b
            tile = lambda t: t
        bs_full = jnp.broadcast_to(gmlp_bs[l][:, :, None], (GMLP_GROUPS, GMLP_CHUNK, GMLP_CHUNK))
        x1, h2 = _merge(xs, mod3, p, y_ret.reshape(B * lt, -1), o_mla.reshape(B * lt, -1),
                        gmlp_ln_g[l].reshape(1, -1), gmlp_ln_b[l].reshape(1, -1), gmlp_ws[l].astype(BF16), bs_full,
                        w_br_ret[l].astype(BF16), w_br_mla[l].astype(BF16), w_br_gmlp[l].astype(BF16),
                        w_out[l].astype(BF16), norm2_g[l].reshape(1, D), n_tiles, tile, mod_row)

        bias_t = jnp.broadcast_to(moe_bias[l][:, None], (N_EXPERTS, 128))
        idx, w, rank, counts, shared, hp = _route(h2, moe_router[l].T, bias_t,
                                                  sh_w_gate[l].astype(BF16), sh_w_up[l].astype(BF16),
                                                  sh_w_down[l].astype(BF16), n_tiles, tile)
        w = jnp.pad(w, ((0, 0), (0, 128 - TOP_K)))
        n_act = n_tiles * TM
        n_blocks = -(-(n_act * TOP_K + N_EXPERTS * (MOE_ROWS - 1)) // MOE_ROWS)
        blk_e, valid, pos_sc = _moe_plan(idx, rank, counts, n_blocks)
        xg = _dispatch(pos_sc, hp, n_blocks * MOE_ROWS)
        ys = _experts(blk_e, valid, xg, moe_w_gate, moe_w_up, moe_w_down, l, n_blocks)
        yg = _gather_rows(pos_sc, ys)
        if last:
            xs = _combine(yg, w, x1, shared, mod3, n_tiles, tile, mod_row, B * seq, lambda t: t)
        else:
            pending = (yg, w, x1, shared, mod3)
    return xs.reshape(B, seq, D)
```

```python
import functools
import math

import jax
import jax.numpy as jnp
from jax import lax
from jax.experimental import pallas as pl
from jax.experimental.pallas import tpu as pltpu
from jax.experimental.pallas import tpu_sc as plsc

F32 = jnp.float32
BF16 = jnp.bfloat16

D = 1024
GRID_W = 64
RET_HEADS = 4
RET_D = 128
RET_CHUNK = 256
RET_OUT_ROWS = 256
MLA_HEADS = 8
MLA_Q_LORA = 384
MLA_KV_LORA = 256
MLA_NOPE = 64
MLA_ROPE = 32
MLA_V = 64
MLA_V_EXT = MLA_V + 16
MLA_QK = MLA_NOPE + MLA_ROPE
HEAD_PAD = 128
ROPE_THETA = 10000.0
GMLP_GROUPS = 4
GMLP_W = 512
GMLP_CHUNK = 128
N_EXPERTS = 64
TOP_K = 6
D_EXPERT = 256
ROUTED_SCALE = 2.5
EPS = 1e-6
LOG2_E = 1.4426950408889634

TM = 256
MOE_ROWS = 512
RT = 2 * TM
ATT_KV_CHUNK = 1024
ATT_UNROLL = 16
ATT_HEADS = 4

C_MERGE = 0
C_UV = 3072
C_RET = 4096
C_CKV = 6144
C_KR = 6400
C_CQ = 6528
N_IN_PAD = 6912
IN_CHUNK = 768

VMEM_LIMIT = 56 * 1024 * 1024

SC_CORES = 2
SC_SUBCORES = 16
SC_ROWS = 128


def _cparams(n_axes, vmem=VMEM_LIMIT):
    return pltpu.CompilerParams(dimension_semantics=("arbitrary",) * n_axes, vmem_limit_bytes=vmem)


def _silu(x):
    return x * jax.nn.sigmoid(x)


def _dot(a, b):
    return jnp.dot(a, b, preferred_element_type=F32)


def _dot_nt(a, b):
    return lax.dot_general(a, b, (((1,), (1,)), ((), ())), preferred_element_type=F32)


def _dot_tn(a, b):
    return lax.dot_general(a, b, (((0,), (0,)), ((), ())), preferred_element_type=F32)


def _pack_bf16_pairs(x):
    n = x.shape[1] // 2
    lo = lax.bitcast_convert_type(x[:, :n].astype(BF16).astype(F32), jnp.uint32)
    hi = lax.bitcast_convert_type(x[:, n:].astype(BF16).astype(F32), jnp.uint32)
    return (lo >> 16) | hi


def _unpack_bf16_pairs(u):
    lo = lax.bitcast_convert_type(u << 16, F32)
    hi = lax.bitcast_convert_type(u & jnp.uint32(0xFFFF0000), F32)
    return lo, hi


def _ada_kernel(c_ref, w_ref, b_ref, o_ref):
    s = _silu(c_ref[...])
    o_ref[...] = _dot(s.astype(BF16), w_ref[...].astype(BF16)) + b_ref[...]


def _ada(c_rows, ada_w, ada_b):
    depth = ada_w.shape[0]
    n = ada_w.shape[2]
    cw = 1536
    return pl.pallas_call(
        _ada_kernel,
        grid=(depth, n // cw),
        in_specs=[pl.BlockSpec((8, D), lambda l, j: (0, 0)),
                  pl.BlockSpec((None, D, cw), lambda l, j: (l, 0, j)),
                  pl.BlockSpec((None, 1, cw), lambda l, j: (l, 0, j))],
        out_specs=pl.BlockSpec((None, 8, cw), lambda l, j: (l, 0, j)),
        out_shape=jax.ShapeDtypeStruct((depth, 8, n), F32),
        compiler_params=_cparams(2),
        name="ada_mod",
    )(c_rows, ada_w, ada_b.reshape(depth, 1, n))


def _modulated_rmsnorm(x, g, shift, scale):
    y = x * lax.rsqrt(jnp.mean(x * x, axis=-1, keepdims=True) + EPS) * g
    return y * (1.0 + scale) + shift


def _project_in(x, mod_ref, g_ref, w_ref, o_ref, h_scr):
    h = _modulated_rmsnorm(x, g_ref[...], mod_ref[:, 0:D], mod_ref[:, D:2 * D])
    h_scr[...] = h.astype(BF16)
    for c in range(N_IN_PAD // IN_CHUNK):
        cols = slice(c * IN_CHUNK, (c + 1) * IN_CHUNK)
        o_ref[:, cols] = _dot(h_scr[...], w_ref[:, cols]).astype(BF16)


def _in_proj_kernel(x_ref, mod_ref, g_ref, w_ref, o_ref, h_scr):
    _project_in(x_ref[...], mod_ref, g_ref, w_ref, o_ref, h_scr)


def _in_proj(xs, mod3, g, w_in_r, n_tiles, mod_row):
    n_rows = xs.shape[0]
    return pl.pallas_call(
        _in_proj_kernel,
        grid=(n_tiles,),
        in_specs=[pl.BlockSpec((TM, D), lambda t: (t, 0)),
                  pl.BlockSpec((None, 1, 6 * D), lambda t: (mod_row(t), 0, 0)),
                  pl.BlockSpec((1, D), lambda t: (0, 0)),
                  pl.BlockSpec((D, N_IN_PAD), lambda t: (0, 0), pipeline_mode=pl.Buffered(1))],
        out_specs=pl.BlockSpec((TM, N_IN_PAD), lambda t: (t, 0)),
        out_shape=jax.ShapeDtypeStruct((n_rows, N_IN_PAD), BF16),
        scratch_shapes=[pltpu.VMEM((TM, D), BF16)],
        compiler_params=_cparams(1),
        name="in_proj",
    )(xs, mod3, g, w_in_r)


def _retention_kernel(lg_ref, q_ref, k_ref, v_ref, g_ref, gng_ref, gnb_ref, y_ref, of_scr, ob_scr,
                      *, n_lat_chunks, n_ctx_chunks):
    h = pl.program_id(1)
    lg_f = lg_ref[0, h]
    lg_b = lg_ref[1, h]
    C = RET_CHUNK
    k_scale = RET_D ** -0.5
    ri = lax.broadcasted_iota(jnp.int32, (C, C), 0).astype(F32)
    ci = lax.broadcasted_iota(jnp.int32, (C, C), 1).astype(F32)
    pos = lax.broadcasted_iota(jnp.int32, (C, 1), 0).astype(F32)
    diff = ri - ci
    d_f = jnp.where(diff >= 0, jnp.exp(lg_f * jnp.maximum(diff, 0.0)), 0.0) * k_scale
    d_b = jnp.where(diff < 0, jnp.exp(lg_b * jnp.maximum(-diff, 0.0)), 0.0) * k_scale
    qdec_f = jnp.exp(lg_f * (pos + 1.0))
    kdec_f = jnp.exp(lg_f * (C - 1.0 - pos)) * k_scale
    cdec_f = jnp.exp(lg_f * C)
    qdec_b = jnp.exp(lg_b * (C - pos))
    kdec_b = jnp.exp(lg_b * pos) * k_scale
    cdec_b = jnp.exp(lg_b * C)

    def chunk(c, state, dmat, qdec, kdec, cdec):
        rows = pl.ds(pl.multiple_of(c * C, C), C)
        q = q_ref[rows, :]
        k = k_ref[rows, :]
        v = v_ref[rows, :]
        att = (_dot_nt(q, k) * dmat).astype(BF16)
        o = _dot(att, v) + _dot((q.astype(F32) * qdec).astype(BF16), state.astype(BF16))
        kd = (k.astype(F32) * kdec).astype(BF16)
        return rows, o, state * cdec + _dot_tn(kd, v)

    n_all = n_lat_chunks + n_ctx_chunks

    def scan_body(i, states):
        s_f, s_b = states
        c_f = jnp.where(i < n_ctx_chunks, n_lat_chunks + i, i - n_ctx_chunks)
        rows, o, s_f = chunk(c_f, s_f, d_f, qdec_f, kdec_f, cdec_f)
        of_scr[rows, :] = o
        rows, o, s_b = chunk(n_all - 1 - i, s_b, d_b, qdec_b, kdec_b, cdec_b)
        ob_scr[rows, :] = o
        return s_f, s_b

    zero = jnp.zeros((RET_D, RET_D), F32)
    lax.fori_loop(0, n_all, scan_body, (zero, zero), unroll=2)

    def out_body(c, _):
        rows = pl.ds(pl.multiple_of(c * RET_OUT_ROWS, RET_OUT_ROWS), RET_OUT_ROWS)
        o = of_scr[rows, :] + ob_scr[rows, :]
        mu = jnp.mean(o, axis=-1, keepdims=True)
        var = jnp.mean(jnp.square(o - mu), axis=-1, keepdims=True)
        on = (o - mu) * lax.rsqrt(var + EPS)
        y = _silu(g_ref[rows, :].astype(F32)) * (on * gng_ref[...] + gnb_ref[...])
        y_ref[rows, :] = y.astype(BF16)
        return 0

    lax.fori_loop(0, n_all * C // RET_OUT_ROWS, out_body, 0, unroll=3)


def _retention(p3, lg, gn_g, gn_b, seq, ctx):
    B, lt, _ = p3.shape
    base = C_RET // RET_D
    kern = functools.partial(_retention_kernel, n_lat_chunks=seq // RET_CHUNK, n_ctx_chunks=ctx // RET_CHUNK)

    def col(off):
        return pl.BlockSpec((None, lt, RET_D), lambda b, h: (b, 0, base + off * RET_HEADS + h))

    return pl.pallas_call(
        kern,
        grid=(B, RET_HEADS),
        in_specs=[pl.BlockSpec(memory_space=pltpu.SMEM),
                  col(0), col(1), col(2), col(3),
                  pl.BlockSpec((1, RET_D), lambda b, h: (0, h)),
                  pl.BlockSpec((1, RET_D), lambda b, h: (0, h))],
        out_specs=pl.BlockSpec((None, lt, RET_D), lambda b, h: (b, 0, h)),
        out_shape=jax.ShapeDtypeStruct((B, lt, RET_HEADS * RET_D), BF16),
        scratch_shapes=[pltpu.VMEM((lt, RET_D), F32), pltpu.VMEM((lt, RET_D), F32)],
        compiler_params=_cparams(2),
        name="retention",
    )(lg, p3, p3, p3, p3, gn_g, gn_b)


def _rope_rotate(x, first_half):
    return jnp.where(first_half, pltpu.roll(x, HEAD_PAD - 8, 1), pltpu.roll(x, 8, 1))


def _mla_prep_kernel(cq_ref, ckv_ref, kr_ref, cos_ref, sin_ref, qa_ref, kva_ref, qn_ref, kn_ref, krg_ref,
                     wq_ref, wk_ref, wv_ref, q_ref, k_ref, v_ref):
    lane = lax.broadcasted_iota(jnp.int32, (1, HEAD_PAD), 1)
    first_half = (lane % 16) < 8
    cos = cos_ref[...]
    sin = sin_ref[...]

    def rms(x, n):
        return x * lax.rsqrt(jnp.sum(x * x, axis=-1, keepdims=True) * (1.0 / n) + EPS)

    def rope(x):
        return x * cos + _rope_rotate(x, first_half) * sin

    cq = cq_ref[...].astype(F32)
    cqn = (rms(cq, MLA_Q_LORA) * qa_ref[...]).astype(BF16)
    q_all = _dot(cqn, wq_ref[...])
    ckv = ckv_ref[...].astype(F32)
    ckvn = (rms(ckv, MLA_KV_LORA) * kva_ref[...]).astype(BF16)
    k_all = _dot(ckvn, wk_ref[...])
    v_all = _dot(ckvn, wv_ref[...])
    k_rope = rope(rms(kr_ref[...].astype(F32), MLA_ROPE) * krg_ref[...])
    scale = MLA_QK ** -0.5 * LOG2_E
    v_t = v_all.T
    ones_row = jnp.where(lax.broadcasted_iota(jnp.int32, (MLA_V_EXT - MLA_V, TM), 0) == 0, 1.0, 0.0)
    for h in range(MLA_HEADS):
        cols = slice(h * HEAD_PAD, (h + 1) * HEAD_PAD)
        qh = rope(rms(q_all[:, cols], MLA_QK) * qn_ref[...]) * scale
        q_ref[h] = qh.astype(BF16)
        kh = rms(k_all[:, cols], MLA_NOPE) * kn_ref[...] + k_rope
        k_ref[h] = kh.astype(BF16)
        v_ref[h] = jnp.concatenate([v_t[h * MLA_V:(h + 1) * MLA_V, :], ones_row], axis=0).astype(BF16)


def _mla_prep(p, cos_t, sin_t, qa_g, kva_g, qn_p, kn_p, kr_p, wq_p, wk_p, wv, B, lt):
    tiles_per_b = lt // TM
    hw = MLA_HEADS * HEAD_PAD
    const = lambda shape: pl.BlockSpec(shape, lambda b, j: (0,) * len(shape))
    head_out = pl.BlockSpec((None, MLA_HEADS, TM, HEAD_PAD), lambda b, j: (b, 0, j, 0))
    shp = jax.ShapeDtypeStruct((B, MLA_HEADS, lt, HEAD_PAD), BF16)
    v_out = pl.BlockSpec((None, MLA_HEADS, None, MLA_V_EXT, TM), lambda b, j: (b, 0, j, 0, 0))
    v_shp = jax.ShapeDtypeStruct((B, MLA_HEADS, tiles_per_b, MLA_V_EXT, TM), BF16)
    return pl.pallas_call(
        _mla_prep_kernel,
        grid=(B, tiles_per_b),
        in_specs=[pl.BlockSpec((TM, MLA_Q_LORA), lambda b, j: (b * tiles_per_b + j, C_CQ // MLA_Q_LORA)),
                  pl.BlockSpec((TM, MLA_KV_LORA), lambda b, j: (b * tiles_per_b + j, C_CKV // MLA_KV_LORA)),
                  pl.BlockSpec((TM, HEAD_PAD), lambda b, j: (b * tiles_per_b + j, C_KR // HEAD_PAD)),
                  pl.BlockSpec((TM, HEAD_PAD), lambda b, j: (j, 0)),
                  pl.BlockSpec((TM, HEAD_PAD), lambda b, j: (j, 0)),
                  const((1, MLA_Q_LORA)), const((1, MLA_KV_LORA)),
                  const((1, HEAD_PAD)), const((1, HEAD_PAD)), const((1, HEAD_PAD)),
                  const((MLA_Q_LORA, hw)), const((MLA_KV_LORA, hw)), const((MLA_KV_LORA, MLA_HEADS * MLA_V))],
        out_specs=[head_out, head_out, v_out],
        out_shape=[shp, shp, v_shp],
        compiler_params=_cparams(2),
        name="mla_prep",
    )(p, p, p, cos_t, sin_t, qa_g, kva_g, qn_p, kn_p, kr_p, wq_p, wk_p, wv)


def _attention_kernel(q_ref, k_ref, v_ref, o_ref, s_scr, *, seq, ctx, ctx_tile):
    i = pl.program_id(2)
    n_blk = ATT_KV_CHUNK // TM
    n_chunks = seq // ATT_KV_CHUNK
    unroll = math.gcd(n_chunks, ATT_UNROLL)

    def scores(hh, slot, blk, nb):
        start = blk * TM if isinstance(blk, int) else pl.multiple_of(blk * TM, TM)
        s_scr[hh, slot, 0:nb * TM, :] = _dot_nt(k_ref[hh, pl.ds(start, nb * TM), :], q_ref[hh])

    def absorb(hh, slot, blk, nb, carry):
        m, acc = carry
        s = s_scr[hh, slot, 0:nb * TM, :]
        m_new = jnp.maximum(m, jnp.max(s, axis=0, keepdims=True))
        p = jnp.exp2(s - m_new).astype(BF16)
        acc = jnp.exp2(m - m_new) * acc
        for j in range(nb):
            acc = acc + _dot(v_ref[hh, blk + j], p[j * TM:(j + 1) * TM, :])
        return m_new, acc

    def init():
        return (jnp.full((1, TM), -jnp.inf, F32), jnp.zeros((MLA_V_EXT, TM), F32))

    def write(carries):
        outs = [acc[0:MLA_V, :] / acc[MLA_V:MLA_V + 1, :] for _, acc in carries]
        o_ref[...] = jnp.concatenate(outs, axis=0).T.astype(BF16)

    def step(carries, slot, blk, nb, next_blk, next_nb):
        out = []
        for hh in range(ATT_HEADS):
            if next_blk is not None:
                scores(hh, 1 - slot, next_blk, next_nb)
            out.append(absorb(hh, slot, blk, nb, carries[hh]))
        return tuple(out)

    ctx_blk = seq // TM
    ctx_nb = ctx // TM

    @pl.when(i != ctx_tile)
    def _():
        for hh in range(ATT_HEADS):
            scores(hh, 0, ctx_blk, ctx_nb)
        carries = step(tuple(init() for _ in range(ATT_HEADS)), 0, ctx_blk, ctx_nb, 0, n_blk)
        last_blk = (n_chunks - 1) * n_blk

        def body(c, carries):
            for u in range(unroll):
                blk = (c * unroll + u) * n_blk
                carries = step(carries, (1 + u) % 2, blk, n_blk, jnp.minimum(blk + n_blk, last_blk), n_blk)
            return carries

        if n_chunks == unroll:
            for u in range(n_chunks):
                nxt = (u + 1) * n_blk if u + 1 < n_chunks else None
                carries = step(carries, (1 + u) % 2, u * n_blk, n_blk, nxt, n_blk)
            write(carries)
        else:
            write(lax.fori_loop(0, n_chunks // unroll, body, carries))

    @pl.when(i == ctx_tile)
    def _():
        for hh in range(ATT_HEADS):
            scores(hh, 0, ctx_blk, ctx_nb)
        write(step(tuple(init() for _ in range(ATT_HEADS)), 0, ctx_blk, ctx_nb, None, None))


def _attention(q, k, v, seq, ctx, n_q_tiles):
    B, H, lt, _ = q.shape
    kern = functools.partial(_attention_kernel, seq=seq, ctx=ctx, ctx_tile=seq // TM)
    return pl.pallas_call(
        kern,
        grid=(B, H // ATT_HEADS, n_q_tiles),
        in_specs=[pl.BlockSpec((None, ATT_HEADS, TM, HEAD_PAD), lambda b, h, i: (b, h, i, 0)),
                  pl.BlockSpec((None, ATT_HEADS, lt, HEAD_PAD), lambda b, h, i: (b, h, 0, 0)),
                  pl.BlockSpec((None, ATT_HEADS, lt // TM, MLA_V_EXT, TM), lambda b, h, i: (b, h, 0, 0, 0))],
        out_specs=pl.BlockSpec((None, TM, ATT_HEADS * MLA_V), lambda b, h, i: (b, i, h)),
        out_shape=jax.ShapeDtypeStruct((B, lt, H * MLA_V), BF16),
        scratch_shapes=[pltpu.VMEM((ATT_HEADS, 2, ATT_KV_CHUNK, TM), F32)],
        compiler_params=_cparams(3),
        name="attention",
    )(q, k, v)


def _merge_kernel(x_ref, mod_ref, mg_ref, uv_ref, yr_ref, om_ref, lng_ref, lnb_ref, ws_ref, bs_ref,
                  wr_ref, wm_ref, wg_ref, wo_ref, n2_ref, x1_ref, h2_ref):
    yr = _dot(yr_ref[...], wr_ref[...])
    ym = _dot(om_ref[...], wm_ref[...])
    z = jax.nn.gelu(uv_ref[...].astype(F32))
    u = z[:, :GMLP_W]
    v = z[:, GMLP_W:]
    mu = jnp.mean(v, axis=-1, keepdims=True)
    var = jnp.mean(jnp.square(v - mu), axis=-1, keepdims=True)
    vn = ((v - mu) * lax.rsqrt(var + EPS) * lng_ref[...] + lnb_ref[...]).astype(BF16)
    gw = GMLP_W // GMLP_GROUPS
    chunks = []
    for c in range(TM // GMLP_CHUNK):
        rows = slice(c * GMLP_CHUNK, (c + 1) * GMLP_CHUNK)
        groups = [_dot(ws_ref[g], vn[rows, g * gw:(g + 1) * gw]) + bs_ref[g] for g in range(GMLP_GROUPS)]
        chunks.append(jnp.concatenate(groups, axis=1))
    sv = jnp.concatenate(chunks, axis=0)
    yg = _dot((u * sv).astype(BF16), wg_ref[...])
    gate = jax.nn.sigmoid(mg_ref[...].astype(F32))
    y = gate[:, :D] * yr + gate[:, D:2 * D] * ym + gate[:, 2 * D:] * yg
    out = _dot(y.astype(BF16), wo_ref[...])
    x1 = x_ref[...] + mod_ref[:, 2 * D:3 * D] * out
    x1_ref[...] = x1
    h2_ref[...] = _modulated_rmsnorm(x1, n2_ref[...], mod_ref[:, 3 * D:4 * D], mod_ref[:, 4 * D:5 * D])


def _merge(xs, mod3, p, y_ret, o_mla, ln_g, ln_b, ws, bs_full, w_br_ret, w_br_mla, w_br_gmlp, w_out, n2_g,
           n_tiles, tile, mod_row):
    n_rows = xs.shape[0]
    const = lambda shape: pl.BlockSpec(shape, lambda t: (0,) * len(shape))
    row = lambda w, cb=0: pl.BlockSpec((TM, w), lambda t: (tile(t), cb))
    shp = jax.ShapeDtypeStruct((n_rows, D), F32)
    return pl.pallas_call(
        _merge_kernel,
        grid=(n_tiles,),
        in_specs=[row(D),
                  pl.BlockSpec((None, 1, 6 * D), lambda t: (mod_row(tile(t)), 0, 0)),
                  row(3 * D, C_MERGE // (3 * D)), row(D, C_UV // D),
                  row(RET_HEADS * RET_D), row(MLA_HEADS * MLA_V),
                  const((1, GMLP_W)), const((1, GMLP_W)),
                  const((GMLP_GROUPS, GMLP_CHUNK, GMLP_CHUNK)), const((GMLP_GROUPS, GMLP_CHUNK, GMLP_CHUNK)),
                  const((RET_HEADS * RET_D, D)), const((MLA_HEADS * MLA_V, D)), const((GMLP_W, D)),
                  const((D, D)), const((1, D))],
        out_specs=[row(D), row(D)],
        out_shape=[shp, shp],
        compiler_params=_cparams(1),
        name="merge",
    )(xs, mod3, p, p, y_ret, o_mla, ln_g, ln_b, ws, bs_full, w_br_ret, w_br_mla, w_br_gmlp, w_out, n2_g)


def _route_kernel(ha_ref, hb_ref, rt_ref, bt_ref, idx_ref, w_ref, rank_ref, cnt_ref, hp_ref, cnt_scr):
    @pl.when(pl.program_id(0) == 0)
    def _():
        cnt_scr[...] = jnp.zeros_like(cnt_scr)

    h = jnp.concatenate([ha_ref[...], hb_ref[...]], axis=0)
    logits = lax.dot_general(rt_ref[...], h, (((1,), (1,)), ((), ())), preferred_element_type=F32,
                             precision=lax.Precision.HIGHEST)
    scores = jax.nn.sigmoid(logits)
    sel = scores + bt_ref[:, 0:1]
    row_e = lax.broadcasted_iota(jnp.int32, (N_EXPERTS, RT), 0).astype(F32)
    row_o = lax.broadcasted_iota(jnp.int32, (8, RT), 0)
    idx_out = jnp.zeros((8, RT), F32)
    w_out = jnp.zeros((8, RT), F32)
    hits = []
    for k in range(TOP_K):
        best = jnp.max(sel, axis=0, keepdims=True)
        pick = jnp.min(jnp.where(sel == best, row_e, float(N_EXPERTS)), axis=0, keepdims=True)
        hit = row_e == pick
        hits.append(hit)
        wk = jnp.sum(jnp.where(hit, scores, 0.0), axis=0, keepdims=True)
        sel = jnp.where(hit, -jnp.inf, sel)
        idx_out = jnp.where(row_o == k, pick, idx_out)
        w_out = jnp.where(row_o == k, wk, w_out)
    w_out = w_out / jnp.sum(w_out, axis=0, keepdims=True) * ROUTED_SCALE
    idx_ref[...] = idx_out.astype(jnp.int32)
    w_ref[...] = w_out
    chosen = jnp.zeros((N_EXPERTS, RT), F32)
    for hit in hits:
        chosen = jnp.where(hit, 1.0, chosen)
    earlier = (lax.broadcasted_iota(jnp.int32, (RT, RT), 0) < lax.broadcasted_iota(jnp.int32, (RT, RT), 1))
    before = _dot(chosen.astype(BF16), jnp.where(earlier, 1.0, 0.0).astype(BF16)) + cnt_scr[:, 0:1]
    rank_out = jnp.zeros((8, RT), F32)
    for k, hit in enumerate(hits):
        rank_out = jnp.where(row_o == k, jnp.sum(jnp.where(hit, before, 0.0), axis=0, keepdims=True), rank_out)
    rank_ref[...] = rank_out.astype(jnp.int32)
    cnt_scr[...] += jnp.sum(chosen, axis=1, keepdims=True)
    cnt_ref[...] = cnt_scr[...]
    hp_ref[...] = _pack_bf16_pairs(h)


def _route(h2, router_t, bias_t, n_tiles, tile):
    const = lambda shape: pl.BlockSpec(shape, lambda t: (0,) * len(shape))
    n_act = n_tiles * TM
    assert n_tiles % 2 == 0
    n_steps = n_tiles // 2
    per_tok = pl.BlockSpec((None, 8, RT), lambda t: (t, 0, 0))
    idx_t, w_t, rank_t, counts, hp = pl.pallas_call(
        _route_kernel,
        grid=(n_steps,),
        in_specs=[pl.BlockSpec((TM, D), lambda t: (tile(2 * t), 0)),
                  pl.BlockSpec((TM, D), lambda t: (tile(2 * t + 1), 0)),
                  const((N_EXPERTS, D)), const((N_EXPERTS, 128))],
        out_specs=[per_tok, per_tok, per_tok,
                   pl.BlockSpec((N_EXPERTS, 128), lambda t: (0, 0)),
                   pl.BlockSpec((RT, D // 2), lambda t: (t, 0))],
        out_shape=[jax.ShapeDtypeStruct((n_steps, 8, RT), jnp.int32),
                   jax.ShapeDtypeStruct((n_steps, 8, RT), F32),
                   jax.ShapeDtypeStruct((n_steps, 8, RT), jnp.int32),
                   jax.ShapeDtypeStruct((N_EXPERTS, 128), F32),
                   jax.ShapeDtypeStruct((n_act, D // 2), jnp.uint32)],
        scratch_shapes=[pltpu.VMEM((N_EXPERTS, 128), F32)],
        compiler_params=_cparams(1),
        name="route",
    )(h2, h2, router_t, bias_t)
    token_major = lambda a: a.transpose(0, 2, 1).reshape(n_act, 8)[:, :TOP_K]
    return token_major(idx_t), token_major(w_t), token_major(rank_t), counts[:, 0], hp


def _shared_expert_kernel(h_ref, sg_ref, su_ref, sd_ref, o_ref):
    hb = h_ref[...].astype(BF16)
    a = _silu(_dot(hb, sg_ref[...])) * _dot(hb, su_ref[...])
    o_ref[...] = _dot(a.astype(BF16), sd_ref[...])


def _shared_expert(h2, sg, su, sd, n_tiles, tile):
    const = lambda shape: pl.BlockSpec(shape, lambda t: (0,) * len(shape))
    return pl.pallas_call(
        _shared_expert_kernel,
        grid=(n_tiles,),
        in_specs=[pl.BlockSpec((TM, D), lambda t: (tile(t), 0)),
                  const((D, D_EXPERT)), const((D, D_EXPERT)), const((D_EXPERT, D))],
        out_specs=pl.BlockSpec((TM, D), lambda t: (t, 0)),
        out_shape=jax.ShapeDtypeStruct((n_tiles * TM, D), F32),
        compiler_params=_cparams(1),
        name="shared_expert",
    )(h2, sg, su, sd)


def _dispatch(pos_sc, hp, n_rows):
    n_batches = pos_sc.shape[0]
    n_workers = SC_CORES * SC_SUBCORES
    mesh = plsc.VectorSubcoreMesh(core_axis_name="c", subcore_axis_name="s")

    @functools.partial(
        pl.kernel, mesh=mesh,
        out_type=jax.ShapeDtypeStruct((n_rows, D // 2), jnp.uint32),
        scratch_types=[pltpu.VMEM((TOP_K, SC_ROWS), jnp.int32),
                       pltpu.VMEM((SC_ROWS, D // 2), jnp.uint32),
                       pltpu.SemaphoreType.DMA],
        name="moe_dispatch")
    def scatter(pos_hbm, h_hbm, xs_hbm, idx_v, rows_v, sem):
        worker = lax.axis_index("s") * SC_CORES + lax.axis_index("c")

        @pl.loop(0, pl.cdiv(n_batches, n_workers))
        def _(j):
            b = j * n_workers + worker

            @pl.when(b < n_batches)
            def _():
                pltpu.sync_copy(pos_hbm.at[b], idx_v)
                pltpu.sync_copy(h_hbm.at[pl.ds(b * SC_ROWS, SC_ROWS)], rows_v)
                copies = [pltpu.async_copy(rows_v, xs_hbm.at[idx_v.at[k]], sem) for k in range(TOP_K)]
                for cp in copies:
                    cp.wait()

    return scatter(pos_sc, hp)


def _expert_kernel(blk_e_ref, valid_ref, x_ref, wg_ref, wu_ref, wd_ref, y_ref, wg_s, wu_s, wd_s):
    i = pl.program_id(0)
    n_valid = valid_ref[i]

    @pl.when(n_valid > 0)
    def _():
        @pl.when(jnp.logical_or(i == 0, blk_e_ref[i] != blk_e_ref[jnp.maximum(i - 1, 0)]))
        def _():
            wg_s[...] = wg_ref[...].astype(BF16)
            wu_s[...] = wu_ref[...].astype(BF16)
            wd_s[...] = wd_ref[...].astype(BF16)

        row = lax.broadcasted_iota(jnp.int32, (MOE_ROWS, 1), 0)
        lo, hi = _unpack_bf16_pairs(jnp.where(row < n_valid, x_ref[...], jnp.uint32(0)))
        x = jnp.concatenate([lo, hi], axis=1).astype(BF16)
        hb = _silu(_dot(x, wg_s[...])) * _dot(x, wu_s[...])
        y_ref[...] = _pack_bf16_pairs(_dot(hb.astype(BF16), wd_s[...]))

    @pl.when(n_valid == 0)
    def _():
        y_ref[...] = jnp.zeros_like(y_ref)


def _experts(blk_e, valid, xs, wg, wu, wd, layer, n_blocks):
    grid_spec = pltpu.PrefetchScalarGridSpec(
        num_scalar_prefetch=2,
        grid=(n_blocks,),
        in_specs=[pl.BlockSpec((MOE_ROWS, D // 2), lambda i, be, nv: (i, 0)),
                  pl.BlockSpec((None, None, D, D_EXPERT), lambda i, be, nv: (layer, be[i], 0, 0)),
                  pl.BlockSpec((None, None, D, D_EXPERT), lambda i, be, nv: (layer, be[i], 0, 0)),
                  pl.BlockSpec((None, None, D_EXPERT, D), lambda i, be, nv: (layer, be[i], 0, 0))],
        out_specs=pl.BlockSpec((MOE_ROWS, D // 2), lambda i, be, nv: (i, 0)),
        scratch_shapes=[pltpu.VMEM((D, D_EXPERT), BF16), pltpu.VMEM((D, D_EXPERT), BF16),
                        pltpu.VMEM((D_EXPERT, D), BF16)],
    )
    return pl.pallas_call(
        _expert_kernel,
        grid_spec=grid_spec,
        out_shape=jax.ShapeDtypeStruct((n_blocks * MOE_ROWS, D // 2), jnp.uint32),
        compiler_params=_cparams(1),
        name="routed_experts",
    )(blk_e, valid, xs, wg, wu, wd)


def _gather_rows(pos_sc, ys):
    n_batches = pos_sc.shape[0]
    n_workers = SC_CORES * SC_SUBCORES
    half = SC_ROWS // 2
    mesh = plsc.VectorSubcoreMesh(core_axis_name="c", subcore_axis_name="s")

    @functools.partial(
        pl.kernel, mesh=mesh,
        out_type=jax.ShapeDtypeStruct((TOP_K, n_batches * SC_ROWS, D // 2), jnp.uint32),
        scratch_types=[pltpu.VMEM((TOP_K, SC_ROWS), jnp.int32),
                       pltpu.VMEM((2, half, D // 2), jnp.uint32),
                       pltpu.SemaphoreType.DMA, pltpu.SemaphoreType.DMA],
        name="moe_gather")
    def gather(pos_hbm, y_hbm, out_hbm, idx_v, bufs, gsem, wsem):
        worker = lax.axis_index("s") * SC_CORES + lax.axis_index("c")

        @pl.loop(0, pl.cdiv(n_batches, n_workers))
        def _(j):
            b = j * n_workers + worker

            @pl.when(b < n_batches)
            def _():
                pltpu.sync_copy(pos_hbm.at[b], idx_v)
                items = [(k, h) for k in range(TOP_K) for h in range(2)]

                def fetch(i):
                    k, h = items[i]
                    return pltpu.async_copy(y_hbm.at[idx_v.at[k, pl.ds(h * half, half)]], bufs.at[i % 2], gsem)

                pending_gather = fetch(0)
                pending_write = None
                for i, (k, h) in enumerate(items):
                    pending_gather.wait()
                    if pending_write is not None:
                        pending_write.wait()
                    if i + 1 < len(items):
                        pending_gather = fetch(i + 1)
                    pending_write = pltpu.async_copy(
                        bufs.at[i % 2], out_hbm.at[k, pl.ds(b * SC_ROWS + h * half, half)], wsem)
                pending_write.wait()

    return gather(pos_sc, ys)


def _combine_tile(y_refs, w_ref, x1_ref, sh_ref, mod_ref):
    f_lo = sh_ref[:, :D // 2]
    f_hi = sh_ref[:, D // 2:]
    for k in range(TOP_K):
        lo, hi = _unpack_bf16_pairs(y_refs[k][...])
        f_lo = f_lo + lo * w_ref[:, k:k + 1]
        f_hi = f_hi + hi * w_ref[:, k:k + 1]
    return x1_ref[...] + mod_ref[:, 5 * D:6 * D] * jnp.concatenate([f_lo, f_hi], axis=1)


def _combine_kernel(*refs):
    o_ref = refs[-1]
    o_ref[...] = _combine_tile(refs[:TOP_K], *refs[TOP_K:-1])


def _combine_specs(tile, mod_row):
    planes = [pl.BlockSpec((None, TM, D // 2), lambda t, k=k: (k, t, 0)) for k in range(TOP_K)]
    return planes + [pl.BlockSpec((TM, 128), lambda t: (t, 0)),
                     pl.BlockSpec((TM, D), lambda t: (tile(t), 0)),
                     pl.BlockSpec((TM, D), lambda t: (t, 0)),
                     pl.BlockSpec((None, 1, 6 * D), lambda t: (mod_row(tile(t)), 0, 0))]


def _combine(yg, w, x1, shared, mod3, n_tiles, tile, mod_row, out_rows, out_tile):
    return pl.pallas_call(
        _combine_kernel,
        grid=(n_tiles,),
        in_specs=_combine_specs(tile, mod_row),
        out_specs=pl.BlockSpec((TM, D), lambda t: (out_tile(t), 0)),
        out_shape=jax.ShapeDtypeStruct((out_rows, D), F32),
        compiler_params=_cparams(1),
        name="moe_combine",
    )(*([yg] * TOP_K), w, x1, shared, mod3)


def _combine_in_proj_kernel(*refs):
    xs_ref, o_ref, h_scr = refs[-3:]
    mod_ref, g_ref, w_ref = refs[TOP_K + 4:-3]
    x = _combine_tile(refs[:TOP_K], *refs[TOP_K:TOP_K + 4])
    xs_ref[...] = x
    _project_in(x, mod_ref, g_ref, w_ref, o_ref, h_scr)


def _combine_in_proj(yg, w, x1, shared, mod3_prev, mod3, g, w_in_r, n_tiles, mod_row):
    n_rows = x1.shape[0]
    ident = lambda t: t
    return pl.pallas_call(
        _combine_in_proj_kernel,
        grid=(n_tiles,),
        in_specs=_combine_specs(ident, mod_row) + [
            pl.BlockSpec((None, 1, 6 * D), lambda t: (mod_row(t), 0, 0)),
            pl.BlockSpec((1, D), lambda t: (0, 0)),
            pl.BlockSpec((D, N_IN_PAD), lambda t: (0, 0), pipeline_mode=pl.Buffered(1))],
        out_specs=[pl.BlockSpec((TM, D), lambda t: (t, 0)),
                   pl.BlockSpec((TM, N_IN_PAD), lambda t: (t, 0))],
        out_shape=[jax.ShapeDtypeStruct((n_rows, D), F32),
                   jax.ShapeDtypeStruct((n_rows, N_IN_PAD), BF16)],
        scratch_shapes=[pltpu.VMEM((TM, D), BF16)],
        compiler_params=_cparams(1),
        name="combine_in_proj",
    )(*([yg] * TOP_K), w, x1, shared, mod3_prev, mod3, g, w_in_r)


def _moe_plan(idx, rank, counts, n_blocks):
    n = idx.shape[0]
    cnt = counts.reshape(N_EXPERTS).astype(jnp.int32)
    padded = (cnt + MOE_ROWS - 1) // MOE_ROWS * MOE_ROWS
    pad_end = jnp.cumsum(padded)
    pad_start = pad_end - padded
    experts = jnp.arange(N_EXPERTS, dtype=jnp.int32)
    pos = rank + jnp.sum(jnp.where(idx[:, :, None] == experts, pad_start, 0), axis=-1)
    blk_start = jnp.arange(n_blocks, dtype=jnp.int32) * MOE_ROWS
    blk_e = jnp.minimum(jnp.sum(blk_start[:, None] >= pad_end[None, :], axis=1), N_EXPERTS - 1).astype(jnp.int32)
    mine = blk_e[:, None] == experts
    in_expert = blk_start - jnp.sum(jnp.where(mine, pad_start, 0), axis=1)
    valid = jnp.clip(jnp.sum(jnp.where(mine, cnt, 0), axis=1) - in_expert, 0, MOE_ROWS).astype(jnp.int32)
    pos_sc = pos.astype(jnp.int32).reshape(n // SC_ROWS, SC_ROWS, TOP_K).transpose(0, 2, 1)
    return blk_e, valid, pos_sc


def _rope_tables(seq, ctx):
    half = MLA_ROPE // 2
    n_freq = half // 2
    inv = ROPE_THETA ** (-2.0 * jnp.arange(n_freq, dtype=F32) / half)
    t = jnp.arange(seq)
    ang_r = (t // GRID_W).astype(F32)[:, None] * inv
    ang_c = (t % GRID_W).astype(F32)[:, None] * inv
    cos = jnp.concatenate([jnp.cos(ang_r), jnp.cos(ang_r), jnp.cos(ang_c), jnp.cos(ang_c)], axis=1)
    sin = jnp.concatenate([-jnp.sin(ang_r), jnp.sin(ang_r), -jnp.sin(ang_c), jnp.sin(ang_c)], axis=1)
    pad_l = MLA_NOPE
    pad_r = HEAD_PAD - MLA_NOPE - MLA_ROPE
    cos = jnp.pad(cos, ((0, ctx), (pad_l, pad_r)), constant_values=1.0)
    cos = cos.at[seq:, :].set(1.0)
    sin = jnp.pad(sin, ((0, ctx), (pad_l, pad_r)))
    return cos, sin


def _pad_heads(w, n_heads, width, offset=0):
    k = w.shape[0]
    w = w.reshape(k, n_heads, width)
    w = jnp.pad(w, ((0, 0), (0, 0), (offset, HEAD_PAD - width - offset)))
    return w.reshape(k, n_heads * HEAD_PAD)


def _pad_vec(g, offset):
    return jnp.pad(g, (offset, HEAD_PAD - g.shape[0] - offset)).reshape(1, HEAD_PAD)


def _reorder_w_in(w):
    off_cq, off_ckv, off_kr, off_uv, off_merge = 2048, 2432, 2688, 2720, 3744
    kr = jnp.pad(w[:, off_kr:off_uv], ((0, 0), (MLA_NOPE, HEAD_PAD - MLA_NOPE - MLA_ROPE)))
    return jnp.concatenate([w[:, off_merge:], w[:, off_uv:off_merge], w[:, :off_cq],
                            w[:, off_ckv:off_kr], kr, w[:, off_cq:off_ckv]], axis=1).astype(BF16)


def kernel(x, c, ctx, c_ctx, ada_w, ada_b, norm1_g, norm2_g, w_in, ret_decay_fwd, ret_decay_bwd, ret_gn_g,
           ret_gn_b, w_br_ret, mla_qa_g, mla_w_uq, mla_kva_g, mla_w_ukv, mla_qn_g, mla_kn_g, mla_kr_g, w_br_mla,
           gmlp_ln_g, gmlp_ln_b, gmlp_ws, gmlp_bs, w_br_gmlp, w_out, moe_router, moe_bias, moe_w_gate, moe_w_up,
           moe_w_down, sh_w_gate, sh_w_up, sh_w_down):
    B, seq, _ = x.shape
    n_ctx = ctx.shape[1]
    depth = ada_w.shape[0]
    assert n_ctx == TM and seq % (2 * ATT_KV_CHUNK) == 0 and seq % TM == 0
    lt = seq + n_ctx
    tiles_per_b = lt // TM
    lat_tiles_per_b = seq // TM
    ctx_tile = lat_tiles_per_b

    def mod_row(t):
        return jnp.where(t % tiles_per_b == ctx_tile, B, t // tiles_per_b)

    c_rows = jnp.concatenate([c, c_ctx[None, :], jnp.zeros((8 - B - 1, D), F32)], axis=0)
    mod = _ada(c_rows, ada_w, ada_b)
    cos_t, sin_t = _rope_tables(seq, n_ctx)
    xs = jnp.concatenate([x, ctx], axis=1).reshape(B * lt, D)

    pending = None
    for l in range(depth):
        last = l == depth - 1
        mod3 = mod[l].reshape(8, 1, 6 * D)
        if pending is None:
            p = _in_proj(xs, mod3, norm1_g[l].reshape(1, D), _reorder_w_in(w_in[l]), B * tiles_per_b, mod_row)
        else:
            xs, p = _combine_in_proj(*pending, mod3, norm1_g[l].reshape(1, D), _reorder_w_in(w_in[l]),
                                     B * tiles_per_b, mod_row)

        lg = jnp.stack([jax.nn.log_sigmoid(ret_decay_fwd[l].astype(F32)),
                        jax.nn.log_sigmoid(ret_decay_bwd[l].astype(F32))])
        y_ret = _retention(p.reshape(B, lt, N_IN_PAD), lg, ret_gn_g[l].reshape(1, -1), ret_gn_b[l].reshape(1, -1),
                           seq, n_ctx)

        w_ukv = mla_w_ukv[l].reshape(MLA_KV_LORA, MLA_HEADS, MLA_NOPE + MLA_V)
        wk_p = _pad_heads(w_ukv[:, :, :MLA_NOPE].reshape(MLA_KV_LORA, -1), MLA_HEADS, MLA_NOPE).astype(BF16)
        wv = w_ukv[:, :, MLA_NOPE:].reshape(MLA_KV_LORA, MLA_HEADS * MLA_V).astype(BF16)
        wq_p = _pad_heads(mla_w_uq[l], MLA_HEADS, MLA_QK).astype(BF16)
        q, k, v = _mla_prep(p, cos_t, sin_t, mla_qa_g[l].reshape(1, -1), mla_kva_g[l].reshape(1, -1),
                            _pad_vec(mla_qn_g[l], 0), _pad_vec(mla_kn_g[l], 0), _pad_vec(mla_kr_g[l], MLA_NOPE),
                            wq_p, wk_p, wv, B, lt)
        o_mla = _attention(q, k, v, seq, n_ctx, lat_tiles_per_b if last else tiles_per_b)

        if last:
            n_tiles = B * lat_tiles_per_b
            tile = lambda t: (t // lat_tiles_per_b) * tiles_per_b + t % lat_tiles_per_b
        else:
            n_tiles = B * tiles_per_b
            tile = lambda t: t
        bs_full = jnp.broadcast_to(gmlp_bs[l][:, :, None], (GMLP_GROUPS, GMLP_CHUNK, GMLP_CHUNK))
        x1, h2 = _merge(xs, mod3, p, y_ret.reshape(B * lt, -1), o_mla.reshape(B * lt, -1),
                        gmlp_ln_g[l].reshape(1, -1), gmlp_ln_b[l].reshape(1, -1), gmlp_ws[l].astype(BF16), bs_full,
                        w_br_ret[l].astype(BF16), w_br_mla[l].astype(BF16), w_br_gmlp[l].astype(BF16),
                        w_out[l].astype(BF16), norm2_g[l].reshape(1, D), n_tiles, tile, mod_row)

        bias_t = jnp.broadcast_to(moe_bias[l][:, None], (N_EXPERTS, 128))
        idx, w, rank, counts, hp = _route(h2, moe_router[l].T, bias_t, n_tiles, tile)
        w = jnp.pad(w, ((0, 0), (0, 128 - TOP_K)))
        n_act = n_tiles * TM
        n_blocks = -(-(n_act * TOP_K + N_EXPERTS * (MOE_ROWS - 1)) // MOE_ROWS)
        blk_e, valid, pos_sc = _moe_plan(idx, rank, counts, n_blocks)
        xg = _dispatch(pos_sc, hp, n_blocks * MOE_ROWS)
        shared = _shared_expert(h2, sh_w_gate[l].astype(BF16), sh_w_up[l].astype(BF16), sh_w_down[l].astype(BF16),
                                n_tiles, tile)
        ys = _experts(blk_e, valid, xg, moe_w_gate, moe_w_up, moe_w_down, l, n_blocks)
        yg = _gather_rows(pos_sc, ys)
        if last:
            xs = _combine(yg, w, x1, shared, mod3, n_tiles, tile, mod_row, B * seq, lambda t: t)
        else:
            pending = (yg, w, x1, shared, mod3)
    return xs.reshape(B, seq, D)
```

```python
import functools
import math

import jax
import jax.numpy as jnp
from jax import lax
from jax.experimental import pallas as pl
from jax.experimental.pallas import tpu as pltpu
from jax.experimental.pallas import tpu_sc as plsc

F32 = jnp.float32
BF16 = jnp.bfloat16

D = 1024
GRID_W = 64
RET_HEADS = 4
RET_D = 128
RET_CHUNK = 256
RET_OUT_ROWS = 256
MLA_HEADS = 8
MLA_Q_LORA = 384
MLA_KV_LORA = 256
MLA_NOPE = 64
MLA_ROPE = 32
MLA_V = 64
MLA_V_EXT = MLA_V + 16
MLA_QK = MLA_NOPE + MLA_ROPE
HEAD_PAD = 128
ROPE_THETA = 10000.0
GMLP_GROUPS = 4
GMLP_W = 512
GMLP_CHUNK = 128
N_EXPERTS = 64
TOP_K = 6
D_EXPERT = 256
ROUTED_SCALE = 2.5
EPS = 1e-6
LOG2_E = 1.4426950408889634

TM = 256
MOE_ROWS = 512
RT = 2 * TM
SHARED_ROWS = 1024
ATT_KV_CHUNK = 1024
ATT_UNROLL = 16
ATT_HEADS = 4

C_MERGE = 0
C_UV = 3072
C_RET = 4096
C_CKV = 6144
C_KR = 6400
C_CQ = 6528
N_IN_PAD = 6912
IN_CHUNK = 768

VMEM_LIMIT = 56 * 1024 * 1024

SC_CORES = 2
SC_SUBCORES = 16
SC_ROWS = 128


def _cparams(n_axes, vmem=VMEM_LIMIT):
    return pltpu.CompilerParams(dimension_semantics=("arbitrary",) * n_axes, vmem_limit_bytes=vmem)


def _silu(x):
    return x * jax.nn.sigmoid(x)


def _dot(a, b):
    return jnp.dot(a, b, preferred_element_type=F32)


def _dot_nt(a, b):
    return lax.dot_general(a, b, (((1,), (1,)), ((), ())), preferred_element_type=F32)


def _dot_tn(a, b):
    return lax.dot_general(a, b, (((0,), (0,)), ((), ())), preferred_element_type=F32)


def _pack_bf16_pairs(x):
    n = x.shape[1] // 2
    lo = lax.bitcast_convert_type(x[:, :n].astype(BF16).astype(F32), jnp.uint32)
    hi = lax.bitcast_convert_type(x[:, n:].astype(BF16).astype(F32), jnp.uint32)
    return (lo >> 16) | hi


def _unpack_bf16_pairs(u):
    lo = lax.bitcast_convert_type(u << 16, F32)
    hi = lax.bitcast_convert_type(u & jnp.uint32(0xFFFF0000), F32)
    return lo, hi


def _ada_kernel(c_ref, w_ref, b_ref, o_ref):
    s = _silu(c_ref[...])
    o_ref[...] = _dot(s.astype(BF16), w_ref[...].astype(BF16)) + b_ref[...]


def _ada(c_rows, ada_w, ada_b):
    depth = ada_w.shape[0]
    n = ada_w.shape[2]
    cw = 1536
    return pl.pallas_call(
        _ada_kernel,
        grid=(depth, n // cw),
        in_specs=[pl.BlockSpec((8, D), lambda l, j: (0, 0)),
                  pl.BlockSpec((None, D, cw), lambda l, j: (l, 0, j)),
                  pl.BlockSpec((None, 1, cw), lambda l, j: (l, 0, j))],
        out_specs=pl.BlockSpec((None, 8, cw), lambda l, j: (l, 0, j)),
        out_shape=jax.ShapeDtypeStruct((depth, 8, n), F32),
        compiler_params=_cparams(2),
        name="ada_mod",
    )(c_rows, ada_w, ada_b.reshape(depth, 1, n))


def _modulated_rmsnorm(x, g, shift, scale):
    y = x * lax.rsqrt(jnp.mean(x * x, axis=-1, keepdims=True) + EPS) * g
    return y * (1.0 + scale) + shift


def _project_in(x, mod_ref, g_ref, w_ref, o_ref, h_scr):
    h = _modulated_rmsnorm(x, g_ref[...], mod_ref[:, 0:D], mod_ref[:, D:2 * D])
    h_scr[...] = h.astype(BF16)
    for c in range(N_IN_PAD // IN_CHUNK):
        cols = slice(c * IN_CHUNK, (c + 1) * IN_CHUNK)
        o_ref[:, cols] = _dot(h_scr[...], w_ref[:, cols]).astype(BF16)


def _in_proj_kernel(x_ref, mod_ref, g_ref, w_ref, o_ref, h_scr):
    _project_in(x_ref[...], mod_ref, g_ref, w_ref, o_ref, h_scr)


def _in_proj(xs, mod3, g, w_in_r, n_tiles, mod_row):
    n_rows = xs.shape[0]
    return pl.pallas_call(
        _in_proj_kernel,
        grid=(n_tiles,),
        in_specs=[pl.BlockSpec((TM, D), lambda t: (t, 0)),
                  pl.BlockSpec((None, 1, 6 * D), lambda t: (mod_row(t), 0, 0)),
                  pl.BlockSpec((1, D), lambda t: (0, 0)),
                  pl.BlockSpec((D, N_IN_PAD), lambda t: (0, 0), pipeline_mode=pl.Buffered(1))],
        out_specs=pl.BlockSpec((TM, N_IN_PAD), lambda t: (t, 0)),
        out_shape=jax.ShapeDtypeStruct((n_rows, N_IN_PAD), BF16),
        scratch_shapes=[pltpu.VMEM((TM, D), BF16)],
        compiler_params=_cparams(1),
        name="in_proj",
    )(xs, mod3, g, w_in_r)


def _retention_kernel(lg_ref, q_ref, k_ref, v_ref, g_ref, gng_ref, gnb_ref, y_ref, of_scr, ob_scr,
                      *, n_lat_chunks, n_ctx_chunks):
    h = pl.program_id(1)
    lg_f = lg_ref[0, h]
    lg_b = lg_ref[1, h]
    C = RET_CHUNK
    k_scale = RET_D ** -0.5
    ri = lax.broadcasted_iota(jnp.int32, (C, C), 0).astype(F32)
    ci = lax.broadcasted_iota(jnp.int32, (C, C), 1).astype(F32)
    pos = lax.broadcasted_iota(jnp.int32, (C, 1), 0).astype(F32)
    diff = ri - ci
    d_f = jnp.where(diff >= 0, jnp.exp(lg_f * jnp.maximum(diff, 0.0)), 0.0) * k_scale
    d_b = jnp.where(diff < 0, jnp.exp(lg_b * jnp.maximum(-diff, 0.0)), 0.0) * k_scale
    qdec_f = jnp.exp(lg_f * (pos + 1.0))
    kdec_f = jnp.exp(lg_f * (C - 1.0 - pos)) * k_scale
    cdec_f = jnp.exp(lg_f * C)
    qdec_b = jnp.exp(lg_b * (C - pos))
    kdec_b = jnp.exp(lg_b * pos) * k_scale
    cdec_b = jnp.exp(lg_b * C)

    def chunk(c, state, dmat, qdec, kdec, cdec):
        rows = pl.ds(pl.multiple_of(c * C, C), C)
        q = q_ref[rows, :]
        k = k_ref[rows, :]
        v = v_ref[rows, :]
        att = (_dot_nt(q, k) * dmat).astype(BF16)
        o = _dot(att, v) + _dot((q.astype(F32) * qdec).astype(BF16), state.astype(BF16))
        kd = (k.astype(F32) * kdec).astype(BF16)
        return rows, o, state * cdec + _dot_tn(kd, v)

    n_all = n_lat_chunks + n_ctx_chunks

    def scan_body(i, states):
        s_f, s_b = states
        c_f = jnp.where(i < n_ctx_chunks, n_lat_chunks + i, i - n_ctx_chunks)
        rows, o, s_f = chunk(c_f, s_f, d_f, qdec_f, kdec_f, cdec_f)
        of_scr[rows, :] = o
        rows, o, s_b = chunk(n_all - 1 - i, s_b, d_b, qdec_b, kdec_b, cdec_b)
        ob_scr[rows, :] = o
        return s_f, s_b

    zero = jnp.zeros((RET_D, RET_D), F32)
    lax.fori_loop(0, n_all, scan_body, (zero, zero), unroll=2)

    def out_body(c, _):
        rows = pl.ds(pl.multiple_of(c * RET_OUT_ROWS, RET_OUT_ROWS), RET_OUT_ROWS)
        o = of_scr[rows, :] + ob_scr[rows, :]
        mu = jnp.mean(o, axis=-1, keepdims=True)
        var = jnp.mean(jnp.square(o - mu), axis=-1, keepdims=True)
        on = (o - mu) * lax.rsqrt(var + EPS)
        y = _silu(g_ref[rows, :].astype(F32)) * (on * gng_ref[...] + gnb_ref[...])
        y_ref[rows, :] = y.astype(BF16)
        return 0

    lax.fori_loop(0, n_all * C // RET_OUT_ROWS, out_body, 0, unroll=3)


def _retention(p3, lg, gn_g, gn_b, seq, ctx):
    B, lt, _ = p3.shape
    base = C_RET // RET_D
    kern = functools.partial(_retention_kernel, n_lat_chunks=seq // RET_CHUNK, n_ctx_chunks=ctx // RET_CHUNK)

    def col(off):
        return pl.BlockSpec((None, lt, RET_D), lambda b, h: (b, 0, base + off * RET_HEADS + h))

    return pl.pallas_call(
        kern,
        grid=(B, RET_HEADS),
        in_specs=[pl.BlockSpec(memory_space=pltpu.SMEM),
                  col(0), col(1), col(2), col(3),
                  pl.BlockSpec((1, RET_D), lambda b, h: (0, h)),
                  pl.BlockSpec((1, RET_D), lambda b, h: (0, h))],
        out_specs=pl.BlockSpec((None, lt, RET_D), lambda b, h: (b, 0, h)),
        out_shape=jax.ShapeDtypeStruct((B, lt, RET_HEADS * RET_D), BF16),
        scratch_shapes=[pltpu.VMEM((lt, RET_D), F32), pltpu.VMEM((lt, RET_D), F32)],
        compiler_params=_cparams(2),
        name="retention",
    )(lg, p3, p3, p3, p3, gn_g, gn_b)


def _rope_rotate(x, first_half):
    return jnp.where(first_half, pltpu.roll(x, HEAD_PAD - 8, 1), pltpu.roll(x, 8, 1))


def _mla_prep_kernel(cq_ref, ckv_ref, kr_ref, cos_ref, sin_ref, qa_ref, kva_ref, qn_ref, kn_ref, krg_ref,
                     wq_ref, wk_ref, wv_ref, q_ref, k_ref, v_ref):
    lane = lax.broadcasted_iota(jnp.int32, (1, HEAD_PAD), 1)
    first_half = (lane % 16) < 8
    cos = cos_ref[...]
    sin = sin_ref[...]

    def rms(x, n):
        return x * lax.rsqrt(jnp.sum(x * x, axis=-1, keepdims=True) * (1.0 / n) + EPS)

    def rope(x):
        return x * cos + _rope_rotate(x, first_half) * sin

    cq = cq_ref[...].astype(F32)
    cqn = (rms(cq, MLA_Q_LORA) * qa_ref[...]).astype(BF16)
    q_all = _dot(cqn, wq_ref[...])
    ckv = ckv_ref[...].astype(F32)
    ckvn = (rms(ckv, MLA_KV_LORA) * kva_ref[...]).astype(BF16)
    k_all = _dot(ckvn, wk_ref[...])
    v_all = _dot(ckvn, wv_ref[...])
    k_rope = rope(rms(kr_ref[...].astype(F32), MLA_ROPE) * krg_ref[...])
    scale = MLA_QK ** -0.5 * LOG2_E
    v_t = v_all.T
    ones_row = jnp.where(lax.broadcasted_iota(jnp.int32, (MLA_V_EXT - MLA_V, TM), 0) == 0, 1.0, 0.0)
    for h in range(MLA_HEADS):
        cols = slice(h * HEAD_PAD, (h + 1) * HEAD_PAD)
        qh = rope(rms(q_all[:, cols], MLA_QK) * qn_ref[...]) * scale
        q_ref[h] = qh.astype(BF16)
        kh = rms(k_all[:, cols], MLA_NOPE) * kn_ref[...] + k_rope
        k_ref[h] = kh.astype(BF16)
        v_ref[h] = jnp.concatenate([v_t[h * MLA_V:(h + 1) * MLA_V, :], ones_row], axis=0).astype(BF16)


def _mla_prep(p, cos_t, sin_t, qa_g, kva_g, qn_p, kn_p, kr_p, wq_p, wk_p, wv, B, lt):
    tiles_per_b = lt // TM
    hw = MLA_HEADS * HEAD_PAD
    const = lambda shape: pl.BlockSpec(shape, lambda b, j: (0,) * len(shape))
    head_out = pl.BlockSpec((None, MLA_HEADS, TM, HEAD_PAD), lambda b, j: (b, 0, j, 0))
    shp = jax.ShapeDtypeStruct((B, MLA_HEADS, lt, HEAD_PAD), BF16)
    v_out = pl.BlockSpec((None, MLA_HEADS, None, MLA_V_EXT, TM), lambda b, j: (b, 0, j, 0, 0))
    v_shp = jax.ShapeDtypeStruct((B, MLA_HEADS, tiles_per_b, MLA_V_EXT, TM), BF16)
    return pl.pallas_call(
        _mla_prep_kernel,
        grid=(B, tiles_per_b),
        in_specs=[pl.BlockSpec((TM, MLA_Q_LORA), lambda b, j: (b * tiles_per_b + j, C_CQ // MLA_Q_LORA)),
                  pl.BlockSpec((TM, MLA_KV_LORA), lambda b, j: (b * tiles_per_b + j, C_CKV // MLA_KV_LORA)),
                  pl.BlockSpec((TM, HEAD_PAD), lambda b, j: (b * tiles_per_b + j, C_KR // HEAD_PAD)),
                  pl.BlockSpec((TM, HEAD_PAD), lambda b, j: (j, 0)),
                  pl.BlockSpec((TM, HEAD_PAD), lambda b, j: (j, 0)),
                  const((1, MLA_Q_LORA)), const((1, MLA_KV_LORA)),
                  const((1, HEAD_PAD)), const((1, HEAD_PAD)), const((1, HEAD_PAD)),
                  const((MLA_Q_LORA, hw)), const((MLA_KV_LORA, hw)), const((MLA_KV_LORA, MLA_HEADS * MLA_V))],
        out_specs=[head_out, head_out, v_out],
        out_shape=[shp, shp, v_shp],
        compiler_params=_cparams(2),
        name="mla_prep",
    )(p, p, p, cos_t, sin_t, qa_g, kva_g, qn_p, kn_p, kr_p, wq_p, wk_p, wv)


def _attention_kernel(q_ref, k_ref, v_ref, o_ref, s_scr, *, seq, ctx, ctx_tile):
    i = pl.program_id(2)
    n_blk = ATT_KV_CHUNK // TM
    n_chunks = seq // ATT_KV_CHUNK
    unroll = math.gcd(n_chunks, ATT_UNROLL)

    def scores(hh, slot, blk, nb):
        start = blk * TM if isinstance(blk, int) else pl.multiple_of(blk * TM, TM)
        s_scr[hh, slot, 0:nb * TM, :] = _dot_nt(k_ref[hh, pl.ds(start, nb * TM), :], q_ref[hh])

    def absorb(hh, slot, blk, nb, carry):
        m, acc = carry
        s = s_scr[hh, slot, 0:nb * TM, :]
        m_new = jnp.maximum(m, jnp.max(s, axis=0, keepdims=True))
        p = jnp.exp2(s - m_new).astype(BF16)
        acc = jnp.exp2(m - m_new) * acc
        for j in range(nb):
            acc = acc + _dot(v_ref[hh, blk + j], p[j * TM:(j + 1) * TM, :])
        return m_new, acc

    def init():
        return (jnp.full((1, TM), -jnp.inf, F32), jnp.zeros((MLA_V_EXT, TM), F32))

    def write(carries):
        outs = [acc[0:MLA_V, :] / acc[MLA_V:MLA_V + 1, :] for _, acc in carries]
        o_ref[...] = jnp.concatenate(outs, axis=0).T.astype(BF16)

    def step(carries, slot, blk, nb, next_blk, next_nb):
        out = []
        for hh in range(ATT_HEADS):
            if next_blk is not None:
                scores(hh, 1 - slot, next_blk, next_nb)
            out.append(absorb(hh, slot, blk, nb, carries[hh]))
        return tuple(out)

    ctx_blk = seq // TM
    ctx_nb = ctx // TM

    @pl.when(i != ctx_tile)
    def _():
        for hh in range(ATT_HEADS):
            scores(hh, 0, ctx_blk, ctx_nb)
        carries = step(tuple(init() for _ in range(ATT_HEADS)), 0, ctx_blk, ctx_nb, 0, n_blk)
        last_blk = (n_chunks - 1) * n_blk

        def body(c, carries):
            for u in range(unroll):
                blk = (c * unroll + u) * n_blk
                carries = step(carries, (1 + u) % 2, blk, n_blk, jnp.minimum(blk + n_blk, last_blk), n_blk)
            return carries

        if n_chunks == unroll:
            for u in range(n_chunks):
                nxt = (u + 1) * n_blk if u + 1 < n_chunks else None
                carries = step(carries, (1 + u) % 2, u * n_blk, n_blk, nxt, n_blk)
            write(carries)
        else:
            write(lax.fori_loop(0, n_chunks // unroll, body, carries))

    @pl.when(i == ctx_tile)
    def _():
        for hh in range(ATT_HEADS):
            scores(hh, 0, ctx_blk, ctx_nb)
        write(step(tuple(init() for _ in range(ATT_HEADS)), 0, ctx_blk, ctx_nb, None, None))


def _attention(q, k, v, seq, ctx, n_q_tiles):
    B, H, lt, _ = q.shape
    kern = functools.partial(_attention_kernel, seq=seq, ctx=ctx, ctx_tile=seq // TM)
    return pl.pallas_call(
        kern,
        grid=(B, H // ATT_HEADS, n_q_tiles),
        in_specs=[pl.BlockSpec((None, ATT_HEADS, TM, HEAD_PAD), lambda b, h, i: (b, h, i, 0)),
                  pl.BlockSpec((None, ATT_HEADS, lt, HEAD_PAD), lambda b, h, i: (b, h, 0, 0)),
                  pl.BlockSpec((None, ATT_HEADS, lt // TM, MLA_V_EXT, TM), lambda b, h, i: (b, h, 0, 0, 0))],
        out_specs=pl.BlockSpec((None, TM, ATT_HEADS * MLA_V), lambda b, h, i: (b, i, h)),
        out_shape=jax.ShapeDtypeStruct((B, lt, H * MLA_V), BF16),
        scratch_shapes=[pltpu.VMEM((ATT_HEADS, 2, ATT_KV_CHUNK, TM), F32)],
        compiler_params=_cparams(3),
        name="attention",
    )(q, k, v)


def _merge_kernel(x_ref, mod_ref, mg_ref, uv_ref, yr_ref, om_ref, lng_ref, lnb_ref, ws_ref, bs_ref,
                  wr_ref, wm_ref, wg_ref, wo_ref, n2_ref, x1_ref, h2_ref):
    yr = _dot(yr_ref[...], wr_ref[...])
    ym = _dot(om_ref[...], wm_ref[...])
    z = jax.nn.gelu(uv_ref[...].astype(F32))
    u = z[:, :GMLP_W]
    v = z[:, GMLP_W:]
    mu = jnp.mean(v, axis=-1, keepdims=True)
    var = jnp.mean(jnp.square(v - mu), axis=-1, keepdims=True)
    vn = ((v - mu) * lax.rsqrt(var + EPS) * lng_ref[...] + lnb_ref[...]).astype(BF16)
    gw = GMLP_W // GMLP_GROUPS
    chunks = []
    for c in range(TM // GMLP_CHUNK):
        rows = slice(c * GMLP_CHUNK, (c + 1) * GMLP_CHUNK)
        groups = [_dot(ws_ref[g], vn[rows, g * gw:(g + 1) * gw]) + bs_ref[g] for g in range(GMLP_GROUPS)]
        chunks.append(jnp.concatenate(groups, axis=1))
    sv = jnp.concatenate(chunks, axis=0)
    yg = _dot((u * sv).astype(BF16), wg_ref[...])
    gate = jax.nn.sigmoid(mg_ref[...].astype(F32))
    y = gate[:, :D] * yr + gate[:, D:2 * D] * ym + gate[:, 2 * D:] * yg
    out = _dot(y.astype(BF16), wo_ref[...])
    x1 = x_ref[...] + mod_ref[:, 2 * D:3 * D] * out
    x1_ref[...] = x1
    h2_ref[...] = _modulated_rmsnorm(x1, n2_ref[...], mod_ref[:, 3 * D:4 * D], mod_ref[:, 4 * D:5 * D])


def _merge(xs, mod3, p, y_ret, o_mla, ln_g, ln_b, ws, bs_full, w_br_ret, w_br_mla, w_br_gmlp, w_out, n2_g,
           n_tiles, tile, mod_row):
    n_rows = xs.shape[0]
    const = lambda shape: pl.BlockSpec(shape, lambda t: (0,) * len(shape))
    row = lambda w, cb=0: pl.BlockSpec((TM, w), lambda t: (tile(t), cb))
    shp = jax.ShapeDtypeStruct((n_rows, D), F32)
    return pl.pallas_call(
        _merge_kernel,
        grid=(n_tiles,),
        in_specs=[row(D),
                  pl.BlockSpec((None, 1, 6 * D), lambda t: (mod_row(tile(t)), 0, 0)),
                  row(3 * D, C_MERGE // (3 * D)), row(D, C_UV // D),
                  row(RET_HEADS * RET_D), row(MLA_HEADS * MLA_V),
                  const((1, GMLP_W)), const((1, GMLP_W)),
                  const((GMLP_GROUPS, GMLP_CHUNK, GMLP_CHUNK)), const((GMLP_GROUPS, GMLP_CHUNK, GMLP_CHUNK)),
                  const((RET_HEADS * RET_D, D)), const((MLA_HEADS * MLA_V, D)), const((GMLP_W, D)),
                  const((D, D)), const((1, D))],
        out_specs=[row(D), row(D)],
        out_shape=[shp, shp],
        compiler_params=_cparams(1),
        name="merge",
    )(xs, mod3, p, p, y_ret, o_mla, ln_g, ln_b, ws, bs_full, w_br_ret, w_br_mla, w_br_gmlp, w_out, n2_g)


def _route_kernel(ha_ref, hb_ref, rt_ref, bt_ref, idx_ref, w_ref, rank_ref, cnt_ref, hp_ref, cnt_scr):
    @pl.when(pl.program_id(0) == 0)
    def _():
        cnt_scr[...] = jnp.zeros_like(cnt_scr)

    h = jnp.concatenate([ha_ref[...], hb_ref[...]], axis=0)
    logits = lax.dot_general(rt_ref[...], h, (((1,), (1,)), ((), ())), preferred_element_type=F32,
                             precision=lax.Precision.HIGHEST)
    scores = jax.nn.sigmoid(logits)
    sel = scores + bt_ref[:, 0:1]
    row_e = lax.broadcasted_iota(jnp.int32, (N_EXPERTS, RT), 0).astype(F32)
    row_o = lax.broadcasted_iota(jnp.int32, (8, RT), 0)
    idx_out = jnp.zeros((8, RT), F32)
    w_out = jnp.zeros((8, RT), F32)
    hits = []
    for k in range(TOP_K):
        best = jnp.max(sel, axis=0, keepdims=True)
        pick = jnp.min(jnp.where(sel == best, row_e, float(N_EXPERTS)), axis=0, keepdims=True)
        hit = row_e == pick
        hits.append(hit)
        wk = jnp.sum(jnp.where(hit, scores, 0.0), axis=0, keepdims=True)
        sel = jnp.where(hit, -jnp.inf, sel)
        idx_out = jnp.where(row_o == k, pick, idx_out)
        w_out = jnp.where(row_o == k, wk, w_out)
    w_out = w_out / jnp.sum(w_out, axis=0, keepdims=True) * ROUTED_SCALE
    idx_ref[...] = idx_out.astype(jnp.int32)
    w_ref[...] = w_out
    chosen = jnp.zeros((N_EXPERTS, RT), F32)
    for hit in hits:
        chosen = jnp.where(hit, 1.0, chosen)
    earlier = (lax.broadcasted_iota(jnp.int32, (RT, RT), 0) < lax.broadcasted_iota(jnp.int32, (RT, RT), 1))
    before = _dot(chosen.astype(BF16), jnp.where(earlier, 1.0, 0.0).astype(BF16)) + cnt_scr[:, 0:1]
    rank_out = jnp.zeros((8, RT), F32)
    for k, hit in enumerate(hits):
        rank_out = jnp.where(row_o == k, jnp.sum(jnp.where(hit, before, 0.0), axis=0, keepdims=True), rank_out)
    rank_ref[...] = rank_out.astype(jnp.int32)
    cnt_scr[...] += jnp.sum(chosen, axis=1, keepdims=True)
    cnt_ref[...] = cnt_scr[...]
    hp_ref[...] = _pack_bf16_pairs(h)


def _route(h2, router_t, bias_t, n_tiles, tile):
    const = lambda shape: pl.BlockSpec(shape, lambda t: (0,) * len(shape))
    n_act = n_tiles * TM
    assert n_tiles % 2 == 0
    n_steps = n_tiles // 2
    per_tok = pl.BlockSpec((None, 8, RT), lambda t: (t, 0, 0))
    idx_t, w_t, rank_t, counts, hp = pl.pallas_call(
        _route_kernel,
        grid=(n_steps,),
        in_specs=[pl.BlockSpec((TM, D), lambda t: (tile(2 * t), 0)),
                  pl.BlockSpec((TM, D), lambda t: (tile(2 * t + 1), 0)),
                  const((N_EXPERTS, D)), const((N_EXPERTS, 128))],
        out_specs=[per_tok, per_tok, per_tok,
                   pl.BlockSpec((N_EXPERTS, 128), lambda t: (0, 0)),
                   pl.BlockSpec((RT, D // 2), lambda t: (t, 0))],
        out_shape=[jax.ShapeDtypeStruct((n_steps, 8, RT), jnp.int32),
                   jax.ShapeDtypeStruct((n_steps, 8, RT), F32),
                   jax.ShapeDtypeStruct((n_steps, 8, RT), jnp.int32),
                   jax.ShapeDtypeStruct((N_EXPERTS, 128), F32),
                   jax.ShapeDtypeStruct((n_act, D // 2), jnp.uint32)],
        scratch_shapes=[pltpu.VMEM((N_EXPERTS, 128), F32)],
        compiler_params=_cparams(1),
        name="route",
    )(h2, h2, router_t, bias_t)
    token_major = lambda a: a.transpose(0, 2, 1).reshape(n_act, 8)[:, :TOP_K]
    return token_major(idx_t), token_major(w_t), token_major(rank_t), counts[:, 0], hp


def _shared_expert_kernel(h_ref, sg_ref, su_ref, sd_ref, o_ref):
    lo, hi = _unpack_bf16_pairs(h_ref[...])
    hb = jnp.concatenate([lo, hi], axis=1).astype(BF16)
    a = _silu(_dot(hb, sg_ref[...])) * _dot(hb, su_ref[...])
    o_ref[...] = _dot(a.astype(BF16), sd_ref[...]).astype(BF16)


def _shared_expert(hp, sg, su, sd):
    const = lambda shape: pl.BlockSpec(shape, lambda t: (0,) * len(shape))
    n_act = hp.shape[0]
    rows = math.gcd(n_act, SHARED_ROWS)
    return pl.pallas_call(
        _shared_expert_kernel,
        grid=(n_act // rows,),
        in_specs=[pl.BlockSpec((rows, D // 2), lambda t: (t, 0)),
                  const((D, D_EXPERT)), const((D, D_EXPERT)), const((D_EXPERT, D))],
        out_specs=pl.BlockSpec((rows, D), lambda t: (t, 0)),
        out_shape=jax.ShapeDtypeStruct((n_act, D), BF16),
        compiler_params=_cparams(1),
        name="shared_expert",
    )(hp, sg, su, sd)


def _dispatch(pos_sc, hp, n_rows):
    n_batches = pos_sc.shape[0]
    n_workers = SC_CORES * SC_SUBCORES
    mesh = plsc.VectorSubcoreMesh(core_axis_name="c", subcore_axis_name="s")

    @functools.partial(
        pl.kernel, mesh=mesh,
        out_type=jax.ShapeDtypeStruct((n_rows, D // 2), jnp.uint32),
        scratch_types=[pltpu.VMEM((TOP_K, SC_ROWS), jnp.int32),
                       pltpu.VMEM((SC_ROWS, D // 2), jnp.uint32),
                       pltpu.SemaphoreType.DMA],
        name="moe_dispatch")
    def scatter(pos_hbm, h_hbm, xs_hbm, idx_v, rows_v, sem):
        worker = lax.axis_index("s") * SC_CORES + lax.axis_index("c")

        @pl.loop(0, pl.cdiv(n_batches, n_workers))
        def _(j):
            b = j * n_workers + worker

            @pl.when(b < n_batches)
            def _():
                pltpu.sync_copy(pos_hbm.at[b], idx_v)
                pltpu.sync_copy(h_hbm.at[pl.ds(b * SC_ROWS, SC_ROWS)], rows_v)
                copies = [pltpu.async_copy(rows_v, xs_hbm.at[idx_v.at[k]], sem) for k in range(TOP_K)]
                for cp in copies:
                    cp.wait()

    return scatter(pos_sc, hp)


def _expert_kernel(blk_e_ref, valid_ref, x_ref, wg_ref, wu_ref, wd_ref, y_ref, wg_s, wu_s, wd_s):
    i = pl.program_id(0)
    n_valid = valid_ref[i]

    @pl.when(n_valid > 0)
    def _():
        @pl.when(jnp.logical_or(i == 0, blk_e_ref[i] != blk_e_ref[jnp.maximum(i - 1, 0)]))
        def _():
            wg_s[...] = wg_ref[...].astype(BF16)
            wu_s[...] = wu_ref[...].astype(BF16)
            wd_s[...] = wd_ref[...].astype(BF16)

        row = lax.broadcasted_iota(jnp.int32, (MOE_ROWS, 1), 0)
        lo, hi = _unpack_bf16_pairs(jnp.where(row < n_valid, x_ref[...], jnp.uint32(0)))
        x = jnp.concatenate([lo, hi], axis=1).astype(BF16)
        hb = _silu(_dot(x, wg_s[...])) * _dot(x, wu_s[...])
        y_ref[...] = _pack_bf16_pairs(_dot(hb.astype(BF16), wd_s[...]))

    @pl.when(n_valid == 0)
    def _():
        y_ref[...] = jnp.zeros_like(y_ref)


def _experts(blk_e, valid, xs, wg, wu, wd, layer, n_blocks):
    grid_spec = pltpu.PrefetchScalarGridSpec(
        num_scalar_prefetch=2,
        grid=(n_blocks,),
        in_specs=[pl.BlockSpec((MOE_ROWS, D // 2), lambda i, be, nv: (i, 0)),
                  pl.BlockSpec((None, None, D, D_EXPERT), lambda i, be, nv: (layer, be[i], 0, 0)),
                  pl.BlockSpec((None, None, D, D_EXPERT), lambda i, be, nv: (layer, be[i], 0, 0)),
                  pl.BlockSpec((None, None, D_EXPERT, D), lambda i, be, nv: (layer, be[i], 0, 0))],
        out_specs=pl.BlockSpec((MOE_ROWS, D // 2), lambda i, be, nv: (i, 0)),
        scratch_shapes=[pltpu.VMEM((D, D_EXPERT), BF16), pltpu.VMEM((D, D_EXPERT), BF16),
                        pltpu.VMEM((D_EXPERT, D), BF16)],
    )
    return pl.pallas_call(
        _expert_kernel,
        grid_spec=grid_spec,
        out_shape=jax.ShapeDtypeStruct((n_blocks * MOE_ROWS, D // 2), jnp.uint32),
        compiler_params=_cparams(1),
        name="routed_experts",
    )(blk_e, valid, xs, wg, wu, wd)


def _gather_rows(pos_sc, ys):
    n_batches = pos_sc.shape[0]
    n_workers = SC_CORES * SC_SUBCORES
    half = SC_ROWS // 2
    mesh = plsc.VectorSubcoreMesh(core_axis_name="c", subcore_axis_name="s")

    @functools.partial(
        pl.kernel, mesh=mesh,
        out_type=jax.ShapeDtypeStruct((TOP_K, n_batches * SC_ROWS, D // 2), jnp.uint32),
        scratch_types=[pltpu.VMEM((TOP_K, SC_ROWS), jnp.int32),
                       pltpu.VMEM((2, half, D // 2), jnp.uint32),
                       pltpu.SemaphoreType.DMA, pltpu.SemaphoreType.DMA],
        name="moe_gather")
    def gather(pos_hbm, y_hbm, out_hbm, idx_v, bufs, gsem, wsem):
        worker = lax.axis_index("s") * SC_CORES + lax.axis_index("c")

        @pl.loop(0, pl.cdiv(n_batches, n_workers))
        def _(j):
            b = j * n_workers + worker

            @pl.when(b < n_batches)
            def _():
                pltpu.sync_copy(pos_hbm.at[b], idx_v)
                items = [(k, h) for k in range(TOP_K) for h in range(2)]

                def fetch(i):
                    k, h = items[i]
                    return pltpu.async_copy(y_hbm.at[idx_v.at[k, pl.ds(h * half, half)]], bufs.at[i % 2], gsem)

                pending_gather = fetch(0)
                pending_write = None
                for i, (k, h) in enumerate(items):
                    pending_gather.wait()
                    if pending_write is not None:
                        pending_write.wait()
                    if i + 1 < len(items):
                        pending_gather = fetch(i + 1)
                    pending_write = pltpu.async_copy(
                        bufs.at[i % 2], out_hbm.at[k, pl.ds(b * SC_ROWS + h * half, half)], wsem)
                pending_write.wait()

    return gather(pos_sc, ys)


def _combine_tile(y_refs, w_ref, x1_ref, sh_ref, mod_ref):
    f_lo = sh_ref[:, :D // 2].astype(F32)
    f_hi = sh_ref[:, D // 2:].astype(F32)
    for k in range(TOP_K):
        lo, hi = _unpack_bf16_pairs(y_refs[k][...])
        f_lo = f_lo + lo * w_ref[:, k:k + 1]
        f_hi = f_hi + hi * w_ref[:, k:k + 1]
    return x1_ref[...] + mod_ref[:, 5 * D:6 * D] * jnp.concatenate([f_lo, f_hi], axis=1)


def _combine_kernel(*refs):
    o_ref = refs[-1]
    o_ref[...] = _combine_tile(refs[:TOP_K], *refs[TOP_K:-1])


def _combine_specs(tile, mod_row):
    planes = [pl.BlockSpec((None, TM, D // 2), lambda t, k=k: (k, t, 0)) for k in range(TOP_K)]
    return planes + [pl.BlockSpec((TM, 128), lambda t: (t, 0)),
                     pl.BlockSpec((TM, D), lambda t: (tile(t), 0)),
                     pl.BlockSpec((TM, D), lambda t: (t, 0)),
                     pl.BlockSpec((None, 1, 6 * D), lambda t: (mod_row(tile(t)), 0, 0))]


def _combine(yg, w, x1, shared, mod3, n_tiles, tile, mod_row, out_rows, out_tile):
    return pl.pallas_call(
        _combine_kernel,
        grid=(n_tiles,),
        in_specs=_combine_specs(tile, mod_row),
        out_specs=pl.BlockSpec((TM, D), lambda t: (out_tile(t), 0)),
        out_shape=jax.ShapeDtypeStruct((out_rows, D), F32),
        compiler_params=_cparams(1),
        name="moe_combine",
    )(*([yg] * TOP_K), w, x1, shared, mod3)


def _combine_in_proj_kernel(*refs):
    xs_ref, o_ref, h_scr = refs[-3:]
    mod_ref, g_ref, w_ref = refs[TOP_K + 4:-3]
    x = _combine_tile(refs[:TOP_K], *refs[TOP_K:TOP_K + 4])
    xs_ref[...] = x
    _project_in(x, mod_ref, g_ref, w_ref, o_ref, h_scr)


def _combine_in_proj(yg, w, x1, shared, mod3_prev, mod3, g, w_in_r, n_tiles, mod_row):
    n_rows = x1.shape[0]
    ident = lambda t: t
    return pl.pallas_call(
        _combine_in_proj_kernel,
        grid=(n_tiles,),
        in_specs=_combine_specs(ident, mod_row) + [
            pl.BlockSpec((None, 1, 6 * D), lambda t: (mod_row(t), 0, 0)),
            pl.BlockSpec((1, D), lambda t: (0, 0)),
            pl.BlockSpec((D, N_IN_PAD), lambda t: (0, 0), pipeline_mode=pl.Buffered(1))],
        out_specs=[pl.BlockSpec((TM, D), lambda t: (t, 0)),
                   pl.BlockSpec((TM, N_IN_PAD), lambda t: (t, 0))],
        out_shape=[jax.ShapeDtypeStruct((n_rows, D), F32),
                   jax.ShapeDtypeStruct((n_rows, N_IN_PAD), BF16)],
        scratch_shapes=[pltpu.VMEM((TM, D), BF16)],
        compiler_params=_cparams(1),
        name="combine_in_proj",
    )(*([yg] * TOP_K), w, x1, shared, mod3_prev, mod3, g, w_in_r)


def _moe_plan(idx, rank, counts, n_blocks):
    n = idx.shape[0]
    cnt = counts.reshape(N_EXPERTS).astype(jnp.int32)
    padded = (cnt + MOE_ROWS - 1) // MOE_ROWS * MOE_ROWS
    pad_end = jnp.cumsum(padded)
    pad_start = pad_end - padded
    experts = jnp.arange(N_EXPERTS, dtype=jnp.int32)
    pos = rank + jnp.sum(jnp.where(idx[:, :, None] == experts, pad_start, 0), axis=-1)
    blk_start = jnp.arange(n_blocks, dtype=jnp.int32) * MOE_ROWS
    blk_e = jnp.minimum(jnp.sum(blk_start[:, None] >= pad_end[None, :], axis=1), N_EXPERTS - 1).astype(jnp.int32)
    mine = blk_e[:, None] == experts
    in_expert = blk_start - jnp.sum(jnp.where(mine, pad_start, 0), axis=1)
    valid = jnp.clip(jnp.sum(jnp.where(mine, cnt, 0), axis=1) - in_expert, 0, MOE_ROWS).astype(jnp.int32)
    pos_sc = pos.astype(jnp.int32).reshape(n // SC_ROWS, SC_ROWS, TOP_K).transpose(0, 2, 1)
    return blk_e, valid, pos_sc


def _rope_tables(seq, ctx):
    half = MLA_ROPE // 2
    n_freq = half // 2
    inv = ROPE_THETA ** (-2.0 * jnp.arange(n_freq, dtype=F32) / half)
    t = jnp.arange(seq)
    ang_r = (t // GRID_W).astype(F32)[:, None] * inv
    ang_c = (t % GRID_W).astype(F32)[:, None] * inv
    cos = jnp.concatenate([jnp.cos(ang_r), jnp.cos(ang_r), jnp.cos(ang_c), jnp.cos(ang_c)], axis=1)
    sin = jnp.concatenate([-jnp.sin(ang_r), jnp.sin(ang_r), -jnp.sin(ang_c), jnp.sin(ang_c)], axis=1)
    pad_l = MLA_NOPE
    pad_r = HEAD_PAD - MLA_NOPE - MLA_ROPE
    cos = jnp.pad(cos, ((0, ctx), (pad_l, pad_r)), constant_values=1.0)
    cos = cos.at[seq:, :].set(1.0)
    sin = jnp.pad(sin, ((0, ctx), (pad_l, pad_r)))
    return cos, sin


def _pad_heads(w, n_heads, width, offset=0):
    k = w.shape[0]
    w = w.reshape(k, n_heads, width)
    w = jnp.pad(w, ((0, 0), (0, 0), (offset, HEAD_PAD - width - offset)))
    return w.reshape(k, n_heads * HEAD_PAD)


def _pad_vec(g, offset):
    return jnp.pad(g, (offset, HEAD_PAD - g.shape[0] - offset)).reshape(1, HEAD_PAD)


def _reorder_w_in(w):
    off_cq, off_ckv, off_kr, off_uv, off_merge = 2048, 2432, 2688, 2720, 3744
    kr = jnp.pad(w[:, off_kr:off_uv], ((0, 0), (MLA_NOPE, HEAD_PAD - MLA_NOPE - MLA_ROPE)))
    return jnp.concatenate([w[:, off_merge:], w[:, off_uv:off_merge], w[:, :off_cq],
                            w[:, off_ckv:off_kr], kr, w[:, off_cq:off_ckv]], axis=1).astype(BF16)


def kernel(x, c, ctx, c_ctx, ada_w, ada_b, norm1_g, norm2_g, w_in, ret_decay_fwd, ret_decay_bwd, ret_gn_g,
           ret_gn_b, w_br_ret, mla_qa_g, mla_w_uq, mla_kva_g, mla_w_ukv, mla_qn_g, mla_kn_g, mla_kr_g, w_br_mla,
           gmlp_ln_g, gmlp_ln_b, gmlp_ws, gmlp_bs, w_br_gmlp, w_out, moe_router, moe_bias, moe_w_gate, moe_w_up,
           moe_w_down, sh_w_gate, sh_w_up, sh_w_down):
    B, seq, _ = x.shape
    n_ctx = ctx.shape[1]
    depth = ada_w.shape[0]
    assert n_ctx == TM and seq % (2 * ATT_KV_CHUNK) == 0 and seq % TM == 0
    lt = seq + n_ctx
    tiles_per_b = lt // TM
    lat_tiles_per_b = seq // TM
    ctx_tile = lat_tiles_per_b

    def mod_row(t):
        return jnp.where(t % tiles_per_b == ctx_tile, B, t // tiles_per_b)

    c_rows = jnp.concatenate([c, c_ctx[None, :], jnp.zeros((8 - B - 1, D), F32)], axis=0)
    mod = _ada(c_rows, ada_w, ada_b)
    cos_t, sin_t = _rope_tables(seq, n_ctx)
    xs = jnp.concatenate([x, ctx], axis=1).reshape(B * lt, D)

    pending = None
    for l in range(depth):
        last = l == depth - 1
        mod3 = mod[l].reshape(8, 1, 6 * D)
        if pending is None:
            p = _in_proj(xs, mod3, norm1_g[l].reshape(1, D), _reorder_w_in(w_in[l]), B * tiles_per_b, mod_row)
        else:
            xs, p = _combine_in_proj(*pending, mod3, norm1_g[l].reshape(1, D), _reorder_w_in(w_in[l]),
                                     B * tiles_per_b, mod_row)

        lg = jnp.stack([jax.nn.log_sigmoid(ret_decay_fwd[l].astype(F32)),
                        jax.nn.log_sigmoid(ret_decay_bwd[l].astype(F32))])
        y_ret = _retention(p.reshape(B, lt, N_IN_PAD), lg, ret_gn_g[l].reshape(1, -1), ret_gn_b[l].reshape(1, -1),
                           seq, n_ctx)

        w_ukv = mla_w_ukv[l].reshape(MLA_KV_LORA, MLA_HEADS, MLA_NOPE + MLA_V)
        wk_p = _pad_heads(w_ukv[:, :, :MLA_NOPE].reshape(MLA_KV_LORA, -1), MLA_HEADS, MLA_NOPE).astype(BF16)
        wv = w_ukv[:, :, MLA_NOPE:].reshape(MLA_KV_LORA, MLA_HEADS * MLA_V).astype(BF16)
        wq_p = _pad_heads(mla_w_uq[l], MLA_HEADS, MLA_QK).astype(BF16)
        q, k, v = _mla_prep(p, cos_t, sin_t, mla_qa_g[l].reshape(1, -1), mla_kva_g[l].reshape(1, -1),
                            _pad_vec(mla_qn_g[l], 0), _pad_vec(mla_kn_g[l], 0), _pad_vec(mla_kr_g[l], MLA_NOPE),
                            wq_p, wk_p, wv, B, lt)
        o_mla = _attention(q, k, v, seq, n_ctx, lat_tiles_per_b if last else tiles_per_b)

        if last:
            n_tiles = B * lat_tiles_per_b
            tile = lambda t: (t // lat_tiles_per_b) * tiles_per_b + t % lat_tiles_per_b
        else:
            n_tiles = B * tiles_per_b
            tile = lambda t: t
        bs_full = jnp.broadcast_to(gmlp_bs[l][:, :, None], (GMLP_GROUPS, GMLP_CHUNK, GMLP_CHUNK))
        x1, h2 = _merge(xs, mod3, p, y_ret.reshape(B * lt, -1), o_mla.reshape(B * lt, -1),
                        gmlp_ln_g[l].reshape(1, -1), gmlp_ln_b[l].reshape(1, -1), gmlp_ws[l].astype(BF16), bs_full,
                        w_br_ret[l].astype(BF16), w_br_mla[l].astype(BF16), w_br_gmlp[l].astype(BF16),
                        w_out[l].astype(BF16), norm2_g[l].reshape(1, D), n_tiles, tile, mod_row)

        bias_t = jnp.broadcast_to(moe_bias[l][:, None], (N_EXPERTS, 128))
        idx, w, rank, counts, hp = _route(h2, moe_router[l].T, bias_t, n_tiles, tile)
        w = jnp.pad(w, ((0, 0), (0, 128 - TOP_K)))
        n_act = n_tiles * TM
        n_blocks = -(-(n_act * TOP_K + N_EXPERTS * (MOE_ROWS - 1)) // MOE_ROWS)
        blk_e, valid, pos_sc = _moe_plan(idx, rank, counts, n_blocks)
        xg = _dispatch(pos_sc, hp, n_blocks * MOE_ROWS)
        shared = _shared_expert(hp, sh_w_gate[l].astype(BF16), sh_w_up[l].astype(BF16), sh_w_down[l].astype(BF16))
        ys = _experts(blk_e, valid, xg, moe_w_gate, moe_w_up, moe_w_down, l, n_blocks)
        yg = _gather_rows(pos_sc, ys)
        if last:
            xs = _combine(yg, w, x1, shared, mod3, n_tiles, tile, mod_row, B * seq, lambda t: t)
        else:
            pending = (yg, w, x1, shared, mod3)
    return xs.reshape(B, seq, D)
```

```python
import functools
import math

import jax
import jax.numpy as jnp
from jax import lax
from jax.experimental import pallas as pl
from jax.experimental.pallas import tpu as pltpu
from jax.experimental.pallas import tpu_sc as plsc

F32 = jnp.float32
BF16 = jnp.bfloat16

D = 1024
GRID_W = 64
RET_HEADS = 4
RET_D = 128
RET_CHUNK = 256
RET_OUT_ROWS = 256
MLA_HEADS = 8
MLA_Q_LORA = 384
MLA_KV_LORA = 256
MLA_NOPE = 64
MLA_ROPE = 32
MLA_V = 64
MLA_V_EXT = MLA_V + 16
MLA_QK = MLA_NOPE + MLA_ROPE
HEAD_PAD = 128
ROPE_THETA = 10000.0
GMLP_GROUPS = 4
GMLP_W = 512
GMLP_CHUNK = 128
N_EXPERTS = 64
TOP_K = 6
D_EXPERT = 256
ROUTED_SCALE = 2.5
EPS = 1e-6
LOG2_E = 1.4426950408889634

TM = 256
MOE_ROWS = 512
RT = 2 * TM
SHARED_ROWS = 1024
ATT_KV_CHUNK = 1024
ATT_UNROLL = 16
ATT_HEADS = 4

C_MERGE = 0
C_UV = 3072
C_RET = 4096
C_CKV = 6144
C_KR = 6400
C_CQ = 6528
N_IN_PAD = 6912
IN_CHUNK = 768

VMEM_LIMIT = 56 * 1024 * 1024

SC_CORES = 2
SC_SUBCORES = 16
SC_ROWS = 128


def _cparams(n_axes, vmem=VMEM_LIMIT):
    return pltpu.CompilerParams(dimension_semantics=("arbitrary",) * n_axes, vmem_limit_bytes=vmem)


def _silu(x):
    return x * jax.nn.sigmoid(x)


def _dot(a, b):
    return jnp.dot(a, b, preferred_element_type=F32)


def _dot_nt(a, b):
    return lax.dot_general(a, b, (((1,), (1,)), ((), ())), preferred_element_type=F32)


def _dot_tn(a, b):
    return lax.dot_general(a, b, (((0,), (0,)), ((), ())), preferred_element_type=F32)


def _pack_bf16_pairs(x):
    n = x.shape[1] // 2
    lo = lax.bitcast_convert_type(x[:, :n].astype(BF16).astype(F32), jnp.uint32)
    hi = lax.bitcast_convert_type(x[:, n:].astype(BF16).astype(F32), jnp.uint32)
    return (lo >> 16) | hi


def _unpack_bf16_pairs(u):
    lo = lax.bitcast_convert_type(u << 16, F32)
    hi = lax.bitcast_convert_type(u & jnp.uint32(0xFFFF0000), F32)
    return lo, hi


def _ada_kernel(c_ref, w_ref, b_ref, o_ref):
    s = _silu(c_ref[...])
    o_ref[...] = _dot(s.astype(BF16), w_ref[...].astype(BF16)) + b_ref[...]


def _ada(c_rows, ada_w, ada_b):
    depth = ada_w.shape[0]
    n = ada_w.shape[2]
    cw = 1536
    return pl.pallas_call(
        _ada_kernel,
        grid=(depth, n // cw),
        in_specs=[pl.BlockSpec((8, D), lambda l, j: (0, 0)),
                  pl.BlockSpec((None, D, cw), lambda l, j: (l, 0, j)),
                  pl.BlockSpec((None, 1, cw), lambda l, j: (l, 0, j))],
        out_specs=pl.BlockSpec((None, 8, cw), lambda l, j: (l, 0, j)),
        out_shape=jax.ShapeDtypeStruct((depth, 8, n), F32),
        compiler_params=_cparams(2),
        name="ada_mod",
    )(c_rows, ada_w, ada_b.reshape(depth, 1, n))


def _modulated_rmsnorm(x, g, shift, scale):
    y = x * lax.rsqrt(jnp.mean(x * x, axis=-1, keepdims=True) + EPS) * g
    return y * (1.0 + scale) + shift


def _stream_specs(x_parts, tile, tiles_per_b):
    if len(x_parts) == 1:
        return [pl.BlockSpec((TM, D), lambda t: (tile(t), 0))]
    lat_tiles = tiles_per_b - 1

    def latent(t):
        s = tile(t)
        return ((s // tiles_per_b) * lat_tiles + jnp.minimum(s % tiles_per_b, lat_tiles - 1), 0)

    return [pl.BlockSpec((TM, D), latent), pl.BlockSpec((TM, D), lambda t: (tile(t) // tiles_per_b, 0))]


def _read_stream(x_refs, tile, tiles_per_b):
    if len(x_refs) == 1:
        return x_refs[0][...]
    is_ctx = tile(pl.program_id(0)) % tiles_per_b == tiles_per_b - 1
    return jnp.where(is_ctx, x_refs[1][...], x_refs[0][...])


def _project_in(x, mod_ref, g_ref, w_ref, o_ref, h_scr):
    h = _modulated_rmsnorm(x, g_ref[...], mod_ref[:, 0:D], mod_ref[:, D:2 * D])
    h_scr[...] = h.astype(BF16)
    for c in range(N_IN_PAD // IN_CHUNK):
        cols = slice(c * IN_CHUNK, (c + 1) * IN_CHUNK)
        o_ref[:, cols] = _dot(h_scr[...], w_ref[:, cols]).astype(BF16)


def _in_proj_kernel(*refs, n_x, tiles_per_b):
    mod_ref, g_ref, w_ref, o_ref, h_scr = refs[n_x:]
    _project_in(_read_stream(refs[:n_x], lambda t: t, tiles_per_b), mod_ref, g_ref, w_ref, o_ref, h_scr)


def _in_proj(x_parts, mod3, g, w_in_r, n_tiles, tiles_per_b, mod_row):
    n_rows = n_tiles * TM
    return pl.pallas_call(
        functools.partial(_in_proj_kernel, n_x=len(x_parts), tiles_per_b=tiles_per_b),
        grid=(n_tiles,),
        in_specs=_stream_specs(x_parts, lambda t: t, tiles_per_b) + [
                  pl.BlockSpec((None, 1, 6 * D), lambda t: (mod_row(t), 0, 0)),
                  pl.BlockSpec((1, D), lambda t: (0, 0)),
                  pl.BlockSpec((D, N_IN_PAD), lambda t: (0, 0), pipeline_mode=pl.Buffered(1))],
        out_specs=pl.BlockSpec((TM, N_IN_PAD), lambda t: (t, 0)),
        out_shape=jax.ShapeDtypeStruct((n_rows, N_IN_PAD), BF16),
        scratch_shapes=[pltpu.VMEM((TM, D), BF16)],
        compiler_params=_cparams(1),
        name="in_proj",
    )(*x_parts, mod3, g, w_in_r)


def _retention_kernel(lg_ref, q_ref, k_ref, v_ref, g_ref, gng_ref, gnb_ref, y_ref, of_scr, ob_scr,
                      *, n_lat_chunks, n_ctx_chunks):
    h = pl.program_id(1)
    lg_f = lg_ref[0, h]
    lg_b = lg_ref[1, h]
    C = RET_CHUNK
    k_scale = RET_D ** -0.5
    ri = lax.broadcasted_iota(jnp.int32, (C, C), 0).astype(F32)
    ci = lax.broadcasted_iota(jnp.int32, (C, C), 1).astype(F32)
    pos = lax.broadcasted_iota(jnp.int32, (C, 1), 0).astype(F32)
    diff = ri - ci
    d_f = jnp.where(diff >= 0, jnp.exp(lg_f * jnp.maximum(diff, 0.0)), 0.0) * k_scale
    d_b = jnp.where(diff < 0, jnp.exp(lg_b * jnp.maximum(-diff, 0.0)), 0.0) * k_scale
    qdec_f = jnp.exp(lg_f * (pos + 1.0))
    kdec_f = jnp.exp(lg_f * (C - 1.0 - pos)) * k_scale
    cdec_f = jnp.exp(lg_f * C)
    qdec_b = jnp.exp(lg_b * (C - pos))
    kdec_b = jnp.exp(lg_b * pos) * k_scale
    cdec_b = jnp.exp(lg_b * C)

    def chunk(c, state, dmat, qdec, kdec, cdec):
        rows = pl.ds(pl.multiple_of(c * C, C), C)
        q = q_ref[rows, :]
        k = k_ref[rows, :]
        v = v_ref[rows, :]
        att = (_dot_nt(q, k) * dmat).astype(BF16)
        o = _dot(att, v) + _dot((q.astype(F32) * qdec).astype(BF16), state.astype(BF16))
        kd = (k.astype(F32) * kdec).astype(BF16)
        return rows, o, state * cdec + _dot_tn(kd, v)

    n_all = n_lat_chunks + n_ctx_chunks

    def scan_body(i, states):
        s_f, s_b = states
        c_f = jnp.where(i < n_ctx_chunks, n_lat_chunks + i, i - n_ctx_chunks)
        rows, o, s_f = chunk(c_f, s_f, d_f, qdec_f, kdec_f, cdec_f)
        of_scr[rows, :] = o
        rows, o, s_b = chunk(n_all - 1 - i, s_b, d_b, qdec_b, kdec_b, cdec_b)
        ob_scr[rows, :] = o
        return s_f, s_b

    zero = jnp.zeros((RET_D, RET_D), F32)
    lax.fori_loop(0, n_all, scan_body, (zero, zero), unroll=2)

    def out_body(c, _):
        rows = pl.ds(pl.multiple_of(c * RET_OUT_ROWS, RET_OUT_ROWS), RET_OUT_ROWS)
        o = of_scr[rows, :] + ob_scr[rows, :]
        mu = jnp.mean(o, axis=-1, keepdims=True)
        var = jnp.mean(jnp.square(o - mu), axis=-1, keepdims=True)
        on = (o - mu) * lax.rsqrt(var + EPS)
        y = _silu(g_ref[rows, :].astype(F32)) * (on * gng_ref[...] + gnb_ref[...])
        y_ref[rows, :] = y.astype(BF16)
        return 0

    lax.fori_loop(0, n_all * C // RET_OUT_ROWS, out_body, 0, unroll=3)


def _retention(p3, lg, gn_g, gn_b, seq, ctx):
    B, lt, _ = p3.shape
    base = C_RET // RET_D
    kern = functools.partial(_retention_kernel, n_lat_chunks=seq // RET_CHUNK, n_ctx_chunks=ctx // RET_CHUNK)

    def col(off):
        return pl.BlockSpec((None, lt, RET_D), lambda b, h: (b, 0, base + off * RET_HEADS + h))

    return pl.pallas_call(
        kern,
        grid=(B, RET_HEADS),
        in_specs=[pl.BlockSpec(memory_space=pltpu.SMEM),
                  col(0), col(1), col(2), col(3),
                  pl.BlockSpec((1, RET_D), lambda b, h: (0, h)),
                  pl.BlockSpec((1, RET_D), lambda b, h: (0, h))],
        out_specs=pl.BlockSpec((None, lt, RET_D), lambda b, h: (b, 0, h)),
        out_shape=jax.ShapeDtypeStruct((B, lt, RET_HEADS * RET_D), BF16),
        scratch_shapes=[pltpu.VMEM((lt, RET_D), F32), pltpu.VMEM((lt, RET_D), F32)],
        compiler_params=_cparams(2),
        name="retention",
    )(lg, p3, p3, p3, p3, gn_g, gn_b)


def _rope_rotate(x, first_half):
    return jnp.where(first_half, pltpu.roll(x, HEAD_PAD - 8, 1), pltpu.roll(x, 8, 1))


def _mla_prep_kernel(cq_ref, ckv_ref, kr_ref, cos_ref, sin_ref, qa_ref, kva_ref, qn_ref, kn_ref, krg_ref,
                     wq_ref, wk_ref, wv_ref, q_ref, k_ref, v_ref):
    lane = lax.broadcasted_iota(jnp.int32, (1, HEAD_PAD), 1)
    first_half = (lane % 16) < 8
    cos = cos_ref[...]
    sin = sin_ref[...]

    def rms(x, n):
        return x * lax.rsqrt(jnp.sum(x * x, axis=-1, keepdims=True) * (1.0 / n) + EPS)

    def rope(x):
        return x * cos + _rope_rotate(x, first_half) * sin

    cq = cq_ref[...].astype(F32)
    cqn = (rms(cq, MLA_Q_LORA) * qa_ref[...]).astype(BF16)
    q_all = _dot(cqn, wq_ref[...])
    ckv = ckv_ref[...].astype(F32)
    ckvn = (rms(ckv, MLA_KV_LORA) * kva_ref[...]).astype(BF16)
    k_all = _dot(ckvn, wk_ref[...])
    v_all = _dot(ckvn, wv_ref[...])
    k_rope = rope(rms(kr_ref[...].astype(F32), MLA_ROPE) * krg_ref[...])
    scale = MLA_QK ** -0.5 * LOG2_E
    v_t = v_all.T
    ones_row = jnp.where(lax.broadcasted_iota(jnp.int32, (MLA_V_EXT - MLA_V, TM), 0) == 0, 1.0, 0.0)
    for h in range(MLA_HEADS):
        cols = slice(h * HEAD_PAD, (h + 1) * HEAD_PAD)
        qh = rope(rms(q_all[:, cols], MLA_QK) * qn_ref[...]) * scale
        q_ref[h] = qh.astype(BF16)
        kh = rms(k_all[:, cols], MLA_NOPE) * kn_ref[...] + k_rope
        k_ref[h] = kh.astype(BF16)
        v_ref[h] = jnp.concatenate([v_t[h * MLA_V:(h + 1) * MLA_V, :], ones_row], axis=0).astype(BF16)


def _mla_prep(p, cos_t, sin_t, qa_g, kva_g, qn_p, kn_p, kr_p, wq_p, wk_p, wv, B, lt):
    tiles_per_b = lt // TM
    hw = MLA_HEADS * HEAD_PAD
    const = lambda shape: pl.BlockSpec(shape, lambda b, j: (0,) * len(shape))
    head_out = pl.BlockSpec((None, MLA_HEADS, TM, HEAD_PAD), lambda b, j: (b, 0, j, 0))
    shp = jax.ShapeDtypeStruct((B, MLA_HEADS, lt, HEAD_PAD), BF16)
    v_out = pl.BlockSpec((None, MLA_HEADS, None, MLA_V_EXT, TM), lambda b, j: (b, 0, j, 0, 0))
    v_shp = jax.ShapeDtypeStruct((B, MLA_HEADS, tiles_per_b, MLA_V_EXT, TM), BF16)
    return pl.pallas_call(
        _mla_prep_kernel,
        grid=(B, tiles_per_b),
        in_specs=[pl.BlockSpec((TM, MLA_Q_LORA), lambda b, j: (b * tiles_per_b + j, C_CQ // MLA_Q_LORA)),
                  pl.BlockSpec((TM, MLA_KV_LORA), lambda b, j: (b * tiles_per_b + j, C_CKV // MLA_KV_LORA)),
                  pl.BlockSpec((TM, HEAD_PAD), lambda b, j: (b * tiles_per_b + j, C_KR // HEAD_PAD)),
                  pl.BlockSpec((TM, HEAD_PAD), lambda b, j: (j, 0)),
                  pl.BlockSpec((TM, HEAD_PAD), lambda b, j: (j, 0)),
                  const((1, MLA_Q_LORA)), const((1, MLA_KV_LORA)),
                  const((1, HEAD_PAD)), const((1, HEAD_PAD)), const((1, HEAD_PAD)),
                  const((MLA_Q_LORA, hw)), const((MLA_KV_LORA, hw)), const((MLA_KV_LORA, MLA_HEADS * MLA_V))],
        out_specs=[head_out, head_out, v_out],
        out_shape=[shp, shp, v_shp],
        compiler_params=_cparams(2),
        name="mla_prep",
    )(p, p, p, cos_t, sin_t, qa_g, kva_g, qn_p, kn_p, kr_p, wq_p, wk_p, wv)


def _attention_kernel(q_ref, k_ref, v_ref, o_ref, s_scr, *, seq, ctx, ctx_tile):
    i = pl.program_id(2)
    n_blk = ATT_KV_CHUNK // TM
    n_chunks = seq // ATT_KV_CHUNK
    unroll = math.gcd(n_chunks, ATT_UNROLL)

    def scores(hh, slot, blk, nb):
        start = blk * TM if isinstance(blk, int) else pl.multiple_of(blk * TM, TM)
        s_scr[hh, slot, 0:nb * TM, :] = _dot_nt(k_ref[hh, pl.ds(start, nb * TM), :], q_ref[hh])

    def absorb(hh, slot, blk, nb, carry):
        m, acc = carry
        s = s_scr[hh, slot, 0:nb * TM, :]
        m_new = jnp.maximum(m, jnp.max(s, axis=0, keepdims=True))
        p = jnp.exp2(s - m_new).astype(BF16)
        acc = jnp.exp2(m - m_new) * acc
        for j in range(nb):
            acc = acc + _dot(v_ref[hh, blk + j], p[j * TM:(j + 1) * TM, :])
        return m_new, acc

    def init():
        return (jnp.full((1, TM), -jnp.inf, F32), jnp.zeros((MLA_V_EXT, TM), F32))

    def write(carries):
        outs = [acc[0:MLA_V, :] / acc[MLA_V:MLA_V + 1, :] for _, acc in carries]
        o_ref[...] = jnp.concatenate(outs, axis=0).T.astype(BF16)

    def step(carries, slot, blk, nb, next_blk, next_nb):
        out = []
        for hh in range(ATT_HEADS):
            if next_blk is not None:
                scores(hh, 1 - slot, next_blk, next_nb)
            out.append(absorb(hh, slot, blk, nb, carries[hh]))
        return tuple(out)

    ctx_blk = seq // TM
    ctx_nb = ctx // TM

    @pl.when(i != ctx_tile)
    def _():
        for hh in range(ATT_HEADS):
            scores(hh, 0, ctx_blk, ctx_nb)
        carries = step(tuple(init() for _ in range(ATT_HEADS)), 0, ctx_blk, ctx_nb, 0, n_blk)
        last_blk = (n_chunks - 1) * n_blk

        def body(c, carries):
            for u in range(unroll):
                blk = (c * unroll + u) * n_blk
                carries = step(carries, (1 + u) % 2, blk, n_blk, jnp.minimum(blk + n_blk, last_blk), n_blk)
            return carries

        if n_chunks == unroll:
            for u in range(n_chunks):
                nxt = (u + 1) * n_blk if u + 1 < n_chunks else None
                carries = step(carries, (1 + u) % 2, u * n_blk, n_blk, nxt, n_blk)
            write(carries)
        else:
            write(lax.fori_loop(0, n_chunks // unroll, body, carries))

    @pl.when(i == ctx_tile)
    def _():
        for hh in range(ATT_HEADS):
            scores(hh, 0, ctx_blk, ctx_nb)
        write(step(tuple(init() for _ in range(ATT_HEADS)), 0, ctx_blk, ctx_nb, None, None))


def _attention(q, k, v, seq, ctx, n_q_tiles):
    B, H, lt, _ = q.shape
    kern = functools.partial(_attention_kernel, seq=seq, ctx=ctx, ctx_tile=seq // TM)
    return pl.pallas_call(
        kern,
        grid=(B, H // ATT_HEADS, n_q_tiles),
        in_specs=[pl.BlockSpec((None, ATT_HEADS, TM, HEAD_PAD), lambda b, h, i: (b, h, i, 0)),
                  pl.BlockSpec((None, ATT_HEADS, lt, HEAD_PAD), lambda b, h, i: (b, h, 0, 0)),
                  pl.BlockSpec((None, ATT_HEADS, lt // TM, MLA_V_EXT, TM), lambda b, h, i: (b, h, 0, 0, 0))],
        out_specs=pl.BlockSpec((None, TM, ATT_HEADS * MLA_V), lambda b, h, i: (b, i, h)),
        out_shape=jax.ShapeDtypeStruct((B, lt, H * MLA_V), BF16),
        scratch_shapes=[pltpu.VMEM((ATT_HEADS, 2, ATT_KV_CHUNK, TM), F32)],
        compiler_params=_cparams(3),
        name="attention",
    )(q, k, v)


def _merge_kernel(*refs, n_x, tile, tiles_per_b):
    (mod_ref, mg_ref, uv_ref, yr_ref, om_ref, lng_ref, lnb_ref, ws_ref, bs_ref,
     wr_ref, wm_ref, wg_ref, wo_ref, n2_ref, x1_ref, h2_ref) = refs[n_x:]
    yr = _dot(yr_ref[...], wr_ref[...])
    ym = _dot(om_ref[...], wm_ref[...])
    z = jax.nn.gelu(uv_ref[...].astype(F32))
    u = z[:, :GMLP_W]
    v = z[:, GMLP_W:]
    mu = jnp.mean(v, axis=-1, keepdims=True)
    var = jnp.mean(jnp.square(v - mu), axis=-1, keepdims=True)
    vn = ((v - mu) * lax.rsqrt(var + EPS) * lng_ref[...] + lnb_ref[...]).astype(BF16)
    gw = GMLP_W // GMLP_GROUPS
    chunks = []
    for c in range(TM // GMLP_CHUNK):
        rows = slice(c * GMLP_CHUNK, (c + 1) * GMLP_CHUNK)
        groups = [_dot(ws_ref[g], vn[rows, g * gw:(g + 1) * gw]) + bs_ref[g] for g in range(GMLP_GROUPS)]
        chunks.append(jnp.concatenate(groups, axis=1))
    sv = jnp.concatenate(chunks, axis=0)
    yg = _dot((u * sv).astype(BF16), wg_ref[...])
    gate = jax.nn.sigmoid(mg_ref[...].astype(F32))
    y = gate[:, :D] * yr + gate[:, D:2 * D] * ym + gate[:, 2 * D:] * yg
    out = _dot(y.astype(BF16), wo_ref[...])
    x1 = _read_stream(refs[:n_x], tile, tiles_per_b) + mod_ref[:, 2 * D:3 * D] * out
    x1_ref[...] = x1
    h2_ref[...] = _modulated_rmsnorm(x1, n2_ref[...], mod_ref[:, 3 * D:4 * D], mod_ref[:, 4 * D:5 * D])


def _merge(x_parts, mod3, p, y_ret, o_mla, ln_g, ln_b, ws, bs_full, w_br_ret, w_br_mla, w_br_gmlp, w_out, n2_g,
           n_tiles, tile, tiles_per_b, mod_row):
    n_rows = p.shape[0]
    const = lambda shape: pl.BlockSpec(shape, lambda t: (0,) * len(shape))
    row = lambda w, cb=0: pl.BlockSpec((TM, w), lambda t: (tile(t), cb))
    shp = jax.ShapeDtypeStruct((n_rows, D), F32)
    return pl.pallas_call(
        functools.partial(_merge_kernel, n_x=len(x_parts), tile=tile, tiles_per_b=tiles_per_b),
        grid=(n_tiles,),
        in_specs=_stream_specs(x_parts, tile, tiles_per_b) + [
                  pl.BlockSpec((None, 1, 6 * D), lambda t: (mod_row(tile(t)), 0, 0)),
                  row(3 * D, C_MERGE // (3 * D)), row(D, C_UV // D),
                  row(RET_HEADS * RET_D), row(MLA_HEADS * MLA_V),
                  const((1, GMLP_W)), const((1, GMLP_W)),
                  const((GMLP_GROUPS, GMLP_CHUNK, GMLP_CHUNK)), const((GMLP_GROUPS, GMLP_CHUNK, GMLP_CHUNK)),
                  const((RET_HEADS * RET_D, D)), const((MLA_HEADS * MLA_V, D)), const((GMLP_W, D)),
                  const((D, D)), const((1, D))],
        out_specs=[row(D), row(D)],
        out_shape=[shp, shp],
        compiler_params=_cparams(1),
        name="merge",
    )(*x_parts, mod3, p, p, y_ret, o_mla, ln_g, ln_b, ws, bs_full, w_br_ret, w_br_mla, w_br_gmlp, w_out, n2_g)


def _route_kernel(ha_ref, hb_ref, rt_ref, bt_ref, idx_ref, w_ref, rank_ref, cnt_ref, hp_ref, cnt_scr):
    @pl.when(pl.program_id(0) == 0)
    def _():
        cnt_scr[...] = jnp.zeros_like(cnt_scr)

    h = jnp.concatenate([ha_ref[...], hb_ref[...]], axis=0)
    logits = lax.dot_general(rt_ref[...], h, (((1,), (1,)), ((), ())), preferred_element_type=F32,
                             precision=lax.Precision.HIGHEST)
    scores = jax.nn.sigmoid(logits)
    sel = scores + bt_ref[:, 0:1]
    row_e = lax.broadcasted_iota(jnp.int32, (N_EXPERTS, RT), 0).astype(F32)
    row_o = lax.broadcasted_iota(jnp.int32, (8, RT), 0)
    idx_out = jnp.zeros((8, RT), F32)
    w_out = jnp.zeros((8, RT), F32)
    hits = []
    for k in range(TOP_K):
        best = jnp.max(sel, axis=0, keepdims=True)
        pick = jnp.min(jnp.where(sel == best, row_e, float(N_EXPERTS)), axis=0, keepdims=True)
        hit = row_e == pick
        hits.append(hit)
        wk = jnp.sum(jnp.where(hit, scores, 0.0), axis=0, keepdims=True)
        sel = jnp.where(hit, -jnp.inf, sel)
        idx_out = jnp.where(row_o == k, pick, idx_out)
        w_out = jnp.where(row_o == k, wk, w_out)
    w_out = w_out / jnp.sum(w_out, axis=0, keepdims=True) * ROUTED_SCALE
    idx_ref[...] = idx_out.astype(jnp.int32)
    w_ref[...] = w_out
    chosen = jnp.zeros((N_EXPERTS, RT), F32)
    for hit in hits:
        chosen = jnp.where(hit, 1.0, chosen)
    earlier = (lax.broadcasted_iota(jnp.int32, (RT, RT), 0) < lax.broadcasted_iota(jnp.int32, (RT, RT), 1))
    before = _dot(chosen.astype(BF16), jnp.where(earlier, 1.0, 0.0).astype(BF16)) + cnt_scr[:, 0:1]
    rank_out = jnp.zeros((8, RT), F32)
    for k, hit in enumerate(hits):
        rank_out = jnp.where(row_o == k, jnp.sum(jnp.where(hit, before, 0.0), axis=0, keepdims=True), rank_out)
    rank_ref[...] = rank_out.astype(jnp.int32)
    cnt_scr[...] += jnp.sum(chosen, axis=1, keepdims=True)
    cnt_ref[...] = cnt_scr[...]
    hp_ref[...] = _pack_bf16_pairs(h)


def _route(h2, router_t, bias_t, n_tiles, tile):
    const = lambda shape: pl.BlockSpec(shape, lambda t: (0,) * len(shape))
    n_act = n_tiles * TM
    assert n_tiles % 2 == 0
    n_steps = n_tiles // 2
    per_tok = pl.BlockSpec((None, 8, RT), lambda t: (t, 0, 0))
    idx_t, w_t, rank_t, counts, hp = pl.pallas_call(
        _route_kernel,
        grid=(n_steps,),
        in_specs=[pl.BlockSpec((TM, D), lambda t: (tile(2 * t), 0)),
                  pl.BlockSpec((TM, D), lambda t: (tile(2 * t + 1), 0)),
                  const((N_EXPERTS, D)), const((N_EXPERTS, 128))],
        out_specs=[per_tok, per_tok, per_tok,
                   pl.BlockSpec((N_EXPERTS, 128), lambda t: (0, 0)),
                   pl.BlockSpec((RT, D // 2), lambda t: (t, 0))],
        out_shape=[jax.ShapeDtypeStruct((n_steps, 8, RT), jnp.int32),
                   jax.ShapeDtypeStruct((n_steps, 8, RT), F32),
                   jax.ShapeDtypeStruct((n_steps, 8, RT), jnp.int32),
                   jax.ShapeDtypeStruct((N_EXPERTS, 128), F32),
                   jax.ShapeDtypeStruct((n_act, D // 2), jnp.uint32)],
        scratch_shapes=[pltpu.VMEM((N_EXPERTS, 128), F32)],
        compiler_params=_cparams(1),
        name="route",
    )(h2, h2, router_t, bias_t)
    token_major = lambda a: a.transpose(0, 2, 1).reshape(n_act, 8)
    return (token_major(idx_t)[:, :TOP_K], token_major(w_t), token_major(rank_t)[:, :TOP_K], counts[:, 0], hp)


def _shared_expert_kernel(h_ref, sg_ref, su_ref, sd_ref, o_ref):
    lo, hi = _unpack_bf16_pairs(h_ref[...])
    hb = jnp.concatenate([lo, hi], axis=1).astype(BF16)
    a = _silu(_dot(hb, sg_ref[...])) * _dot(hb, su_ref[...])
    o_ref[...] = _dot(a.astype(BF16), sd_ref[...]).astype(BF16)


def _shared_expert(hp, sg, su, sd):
    const = lambda shape: pl.BlockSpec(shape, lambda t: (0,) * len(shape))
    n_act = hp.shape[0]
    rows = math.gcd(n_act, SHARED_ROWS)
    return pl.pallas_call(
        _shared_expert_kernel,
        grid=(n_act // rows,),
        in_specs=[pl.BlockSpec((rows, D // 2), lambda t: (t, 0)),
                  const((D, D_EXPERT)), const((D, D_EXPERT)), const((D_EXPERT, D))],
        out_specs=pl.BlockSpec((rows, D), lambda t: (t, 0)),
        out_shape=jax.ShapeDtypeStruct((n_act, D), BF16),
        compiler_params=_cparams(1),
        name="shared_expert",
    )(hp, sg, su, sd)


def _dispatch(pos_sc, hp, n_rows):
    n_batches = pos_sc.shape[0]
    n_workers = SC_CORES * SC_SUBCORES
    mesh = plsc.VectorSubcoreMesh(core_axis_name="c", subcore_axis_name="s")

    @functools.partial(
        pl.kernel, mesh=mesh,
        out_type=jax.ShapeDtypeStruct((n_rows, D // 2), jnp.uint32),
        scratch_types=[pltpu.VMEM((TOP_K, SC_ROWS), jnp.int32),
                       pltpu.VMEM((SC_ROWS, D // 2), jnp.uint32),
                       pltpu.SemaphoreType.DMA],
        name="moe_dispatch")
    def scatter(pos_hbm, h_hbm, xs_hbm, idx_v, rows_v, sem):
        worker = lax.axis_index("s") * SC_CORES + lax.axis_index("c")

        @pl.loop(0, pl.cdiv(n_batches, n_workers))
        def _(j):
            b = j * n_workers + worker

            @pl.when(b < n_batches)
            def _():
                pltpu.sync_copy(pos_hbm.at[b], idx_v)
                pltpu.sync_copy(h_hbm.at[pl.ds(b * SC_ROWS, SC_ROWS)], rows_v)
                copies = [pltpu.async_copy(rows_v, xs_hbm.at[idx_v.at[k]], sem) for k in range(TOP_K)]
                for cp in copies:
                    cp.wait()

    return scatter(pos_sc, hp)


def _expert_kernel(blk_e_ref, valid_ref, x_ref, wg_ref, wu_ref, wd_ref, y_ref, wg_s, wu_s, wd_s):
    i = pl.program_id(0)
    n_valid = valid_ref[i]

    @pl.when(n_valid > 0)
    def _():
        @pl.when(jnp.logical_or(i == 0, blk_e_ref[i] != blk_e_ref[jnp.maximum(i - 1, 0)]))
        def _():
            wg_s[...] = wg_ref[...].astype(BF16)
            wu_s[...] = wu_ref[...].astype(BF16)
            wd_s[...] = wd_ref[...].astype(BF16)

        row = lax.broadcasted_iota(jnp.int32, (MOE_ROWS, 1), 0)
        lo, hi = _unpack_bf16_pairs(jnp.where(row < n_valid, x_ref[...], jnp.uint32(0)))
        x = jnp.concatenate([lo, hi], axis=1).astype(BF16)
        hb = _silu(_dot(x, wg_s[...])) * _dot(x, wu_s[...])
        y_ref[...] = _pack_bf16_pairs(_dot(hb.astype(BF16), wd_s[...]))

    @pl.when(n_valid == 0)
    def _():
        y_ref[...] = jnp.zeros_like(y_ref)


def _experts(blk_e, valid, xs, wg, wu, wd, layer, n_blocks):
    grid_spec = pltpu.PrefetchScalarGridSpec(
        num_scalar_prefetch=2,
        grid=(n_blocks,),
        in_specs=[pl.BlockSpec((MOE_ROWS, D // 2), lambda i, be, nv: (i, 0)),
                  pl.BlockSpec((None, None, D, D_EXPERT), lambda i, be, nv: (layer, be[i], 0, 0)),
                  pl.BlockSpec((None, None, D, D_EXPERT), lambda i, be, nv: (layer, be[i], 0, 0)),
                  pl.BlockSpec((None, None, D_EXPERT, D), lambda i, be, nv: (layer, be[i], 0, 0))],
        out_specs=pl.BlockSpec((MOE_ROWS, D // 2), lambda i, be, nv: (i, 0)),
        scratch_shapes=[pltpu.VMEM((D, D_EXPERT), BF16), pltpu.VMEM((D, D_EXPERT), BF16),
                        pltpu.VMEM((D_EXPERT, D), BF16)],
    )
    return pl.pallas_call(
        _expert_kernel,
        grid_spec=grid_spec,
        out_shape=jax.ShapeDtypeStruct((n_blocks * MOE_ROWS, D // 2), jnp.uint32),
        compiler_params=_cparams(1),
        name="routed_experts",
    )(blk_e, valid, xs, wg, wu, wd)


def _gather_rows(pos_sc, ys):
    n_batches = pos_sc.shape[0]
    n_workers = SC_CORES * SC_SUBCORES
    half = SC_ROWS // 2
    mesh = plsc.VectorSubcoreMesh(core_axis_name="c", subcore_axis_name="s")

    @functools.partial(
        pl.kernel, mesh=mesh,
        out_type=jax.ShapeDtypeStruct((TOP_K, n_batches * SC_ROWS, D // 2), jnp.uint32),
        scratch_types=[pltpu.VMEM((TOP_K, SC_ROWS), jnp.int32),
                       pltpu.VMEM((2, half, D // 2), jnp.uint32),
                       pltpu.SemaphoreType.DMA, pltpu.SemaphoreType.DMA],
        name="moe_gather")
    def gather(pos_hbm, y_hbm, out_hbm, idx_v, bufs, gsem, wsem):
        worker = lax.axis_index("s") * SC_CORES + lax.axis_index("c")

        @pl.loop(0, pl.cdiv(n_batches, n_workers))
        def _(j):
            b = j * n_workers + worker

            @pl.when(b < n_batches)
            def _():
                pltpu.sync_copy(pos_hbm.at[b], idx_v)
                items = [(k, h) for k in range(TOP_K) for h in range(2)]

                def fetch(i):
                    k, h = items[i]
                    return pltpu.async_copy(y_hbm.at[idx_v.at[k, pl.ds(h * half, half)]], bufs.at[i % 2], gsem)

                pending_gather = fetch(0)
                pending_write = None
                for i, (k, h) in enumerate(items):
                    pending_gather.wait()
                    if pending_write is not None:
                        pending_write.wait()
                    if i + 1 < len(items):
                        pending_gather = fetch(i + 1)
                    pending_write = pltpu.async_copy(
                        bufs.at[i % 2], out_hbm.at[k, pl.ds(b * SC_ROWS + h * half, half)], wsem)
                pending_write.wait()

    return gather(pos_sc, ys)


def _combine_tile(y_refs, w_ref, x1_ref, sh_ref, mod_ref):
    f_lo = sh_ref[:, :D // 2].astype(F32)
    f_hi = sh_ref[:, D // 2:].astype(F32)
    for k in range(TOP_K):
        lo, hi = _unpack_bf16_pairs(y_refs[k][...])
        f_lo = f_lo + lo * w_ref[:, k:k + 1]
        f_hi = f_hi + hi * w_ref[:, k:k + 1]
    return x1_ref[...] + mod_ref[:, 5 * D:6 * D] * jnp.concatenate([f_lo, f_hi], axis=1)


def _combine_kernel(*refs):
    o_ref = refs[-1]
    o_ref[...] = _combine_tile(refs[:TOP_K], *refs[TOP_K:-1])


def _combine_specs(tile, mod_row):
    planes = [pl.BlockSpec((None, TM, D // 2), lambda t, k=k: (k, t, 0)) for k in range(TOP_K)]
    return planes + [pl.BlockSpec((TM, 8), lambda t: (t, 0)),
                     pl.BlockSpec((TM, D), lambda t: (tile(t), 0)),
                     pl.BlockSpec((TM, D), lambda t: (t, 0)),
                     pl.BlockSpec((None, 1, 6 * D), lambda t: (mod_row(tile(t)), 0, 0))]


def _combine(yg, w, x1, shared, mod3, n_tiles, tile, mod_row, out_rows, out_tile):
    return pl.pallas_call(
        _combine_kernel,
        grid=(n_tiles,),
        in_specs=_combine_specs(tile, mod_row),
        out_specs=pl.BlockSpec((TM, D), lambda t: (out_tile(t), 0)),
        out_shape=jax.ShapeDtypeStruct((out_rows, D), F32),
        compiler_params=_cparams(1),
        name="moe_combine",
    )(*([yg] * TOP_K), w, x1, shared, mod3)


def _combine_in_proj_kernel(*refs):
    xs_ref, o_ref, h_scr = refs[-3:]
    mod_ref, g_ref, w_ref = refs[TOP_K + 4:-3]
    x = _combine_tile(refs[:TOP_K], *refs[TOP_K:TOP_K + 4])
    xs_ref[...] = x
    _project_in(x, mod_ref, g_ref, w_ref, o_ref, h_scr)


def _combine_in_proj(yg, w, x1, shared, mod3_prev, mod3, g, w_in_r, n_tiles, mod_row):
    n_rows = x1.shape[0]
    ident = lambda t: t
    return pl.pallas_call(
        _combine_in_proj_kernel,
        grid=(n_tiles,),
        in_specs=_combine_specs(ident, mod_row) + [
            pl.BlockSpec((None, 1, 6 * D), lambda t: (mod_row(t), 0, 0)),
            pl.BlockSpec((1, D), lambda t: (0, 0)),
            pl.BlockSpec((D, N_IN_PAD), lambda t: (0, 0), pipeline_mode=pl.Buffered(1))],
        out_specs=[pl.BlockSpec((TM, D), lambda t: (t, 0)),
                   pl.BlockSpec((TM, N_IN_PAD), lambda t: (t, 0))],
        out_shape=[jax.ShapeDtypeStruct((n_rows, D), F32),
                   jax.ShapeDtypeStruct((n_rows, N_IN_PAD), BF16)],
        scratch_shapes=[pltpu.VMEM((TM, D), BF16)],
        compiler_params=_cparams(1),
        name="combine_in_proj",
    )(*([yg] * TOP_K), w, x1, shared, mod3_prev, mod3, g, w_in_r)


def _moe_plan(idx, rank, counts, n_blocks):
    n = idx.shape[0]
    cnt = counts.reshape(N_EXPERTS).astype(jnp.int32)
    padded = (cnt + MOE_ROWS - 1) // MOE_ROWS * MOE_ROWS
    pad_end = jnp.cumsum(padded)
    pad_start = pad_end - padded
    experts = jnp.arange(N_EXPERTS, dtype=jnp.int32)
    pos = rank + jnp.sum(jnp.where(idx[:, :, None] == experts, pad_start, 0), axis=-1)
    blk_start = jnp.arange(n_blocks, dtype=jnp.int32) * MOE_ROWS
    blk_e = jnp.minimum(jnp.sum(blk_start[:, None] >= pad_end[None, :], axis=1), N_EXPERTS - 1).astype(jnp.int32)
    mine = blk_e[:, None] == experts
    in_expert = blk_start - jnp.sum(jnp.where(mine, pad_start, 0), axis=1)
    valid = jnp.clip(jnp.sum(jnp.where(mine, cnt, 0), axis=1) - in_expert, 0, MOE_ROWS).astype(jnp.int32)
    pos_sc = pos.astype(jnp.int32).reshape(n // SC_ROWS, SC_ROWS, TOP_K).transpose(0, 2, 1)
    return blk_e, valid, pos_sc


def _rope_tables(seq, ctx):
    half = MLA_ROPE // 2
    n_freq = half // 2
    inv = ROPE_THETA ** (-2.0 * jnp.arange(n_freq, dtype=F32) / half)
    t = jnp.arange(seq)
    ang_r = (t // GRID_W).astype(F32)[:, None] * inv
    ang_c = (t % GRID_W).astype(F32)[:, None] * inv
    cos = jnp.concatenate([jnp.cos(ang_r), jnp.cos(ang_r), jnp.cos(ang_c), jnp.cos(ang_c)], axis=1)
    sin = jnp.concatenate([-jnp.sin(ang_r), jnp.sin(ang_r), -jnp.sin(ang_c), jnp.sin(ang_c)], axis=1)
    pad_l = MLA_NOPE
    pad_r = HEAD_PAD - MLA_NOPE - MLA_ROPE
    cos = jnp.pad(cos, ((0, ctx), (pad_l, pad_r)), constant_values=1.0)
    cos = cos.at[seq:, :].set(1.0)
    sin = jnp.pad(sin, ((0, ctx), (pad_l, pad_r)))
    return cos, sin


def _pad_heads(w, n_heads, width, offset=0):
    k = w.shape[0]
    w = w.reshape(k, n_heads, width)
    w = jnp.pad(w, ((0, 0), (0, 0), (offset, HEAD_PAD - width - offset)))
    return w.reshape(k, n_heads * HEAD_PAD)


def _pad_vec(g, offset):
    return jnp.pad(g, (offset, HEAD_PAD - g.shape[0] - offset)).reshape(1, HEAD_PAD)


def _reorder_w_in(w):
    off_cq, off_ckv, off_kr, off_uv, off_merge = 2048, 2432, 2688, 2720, 3744
    kr = jnp.pad(w[:, off_kr:off_uv], ((0, 0), (MLA_NOPE, HEAD_PAD - MLA_NOPE - MLA_ROPE)))
    return jnp.concatenate([w[:, off_merge:], w[:, off_uv:off_merge], w[:, :off_cq],
                            w[:, off_ckv:off_kr], kr, w[:, off_cq:off_ckv]], axis=1).astype(BF16)


def kernel(x, c, ctx, c_ctx, ada_w, ada_b, norm1_g, norm2_g, w_in, ret_decay_fwd, ret_decay_bwd, ret_gn_g,
           ret_gn_b, w_br_ret, mla_qa_g, mla_w_uq, mla_kva_g, mla_w_ukv, mla_qn_g, mla_kn_g, mla_kr_g, w_br_mla,
           gmlp_ln_g, gmlp_ln_b, gmlp_ws, gmlp_bs, w_br_gmlp, w_out, moe_router, moe_bias, moe_w_gate, moe_w_up,
           moe_w_down, sh_w_gate, sh_w_up, sh_w_down):
    B, seq, _ = x.shape
    n_ctx = ctx.shape[1]
    depth = ada_w.shape[0]
    assert n_ctx == TM and seq % (2 * ATT_KV_CHUNK) == 0 and seq % TM == 0
    lt = seq + n_ctx
    tiles_per_b = lt // TM
    lat_tiles_per_b = seq // TM
    ctx_tile = lat_tiles_per_b

    def mod_row(t):
        return jnp.where(t % tiles_per_b == ctx_tile, B, t // tiles_per_b)

    c_rows = jnp.concatenate([c, c_ctx[None, :], jnp.zeros((8 - B - 1, D), F32)], axis=0)
    mod = _ada(c_rows, ada_w, ada_b)
    cos_t, sin_t = _rope_tables(seq, n_ctx)
    x_parts = (x.reshape(B * seq, D), ctx.reshape(B * n_ctx, D))

    pending = None
    for l in range(depth):
        last = l == depth - 1
        mod3 = mod[l].reshape(8, 1, 6 * D)
        if pending is None:
            p = _in_proj(x_parts, mod3, norm1_g[l].reshape(1, D), _reorder_w_in(w_in[l]), B * tiles_per_b,
                         tiles_per_b, mod_row)
        else:
            xs, p = _combine_in_proj(*pending, mod3, norm1_g[l].reshape(1, D), _reorder_w_in(w_in[l]),
                                     B * tiles_per_b, mod_row)
            x_parts = (xs,)

        lg = jnp.stack([jax.nn.log_sigmoid(ret_decay_fwd[l].astype(F32)),
                        jax.nn.log_sigmoid(ret_decay_bwd[l].astype(F32))])
        y_ret = _retention(p.reshape(B, lt, N_IN_PAD), lg, ret_gn_g[l].reshape(1, -1), ret_gn_b[l].reshape(1, -1),
                           seq, n_ctx)

        w_ukv = mla_w_ukv[l].reshape(MLA_KV_LORA, MLA_HEADS, MLA_NOPE + MLA_V)
        wk_p = _pad_heads(w_ukv[:, :, :MLA_NOPE].reshape(MLA_KV_LORA, -1), MLA_HEADS, MLA_NOPE).astype(BF16)
        wv = w_ukv[:, :, MLA_NOPE:].reshape(MLA_KV_LORA, MLA_HEADS * MLA_V).astype(BF16)
        wq_p = _pad_heads(mla_w_uq[l], MLA_HEADS, MLA_QK).astype(BF16)
        q, k, v = _mla_prep(p, cos_t, sin_t, mla_qa_g[l].reshape(1, -1), mla_kva_g[l].reshape(1, -1),
                            _pad_vec(mla_qn_g[l], 0), _pad_vec(mla_kn_g[l], 0), _pad_vec(mla_kr_g[l], MLA_NOPE),
                            wq_p, wk_p, wv, B, lt)
        o_mla = _attention(q, k, v, seq, n_ctx, lat_tiles_per_b if last else tiles_per_b)

        if last:
            n_tiles = B * lat_tiles_per_b
            tile = lambda t: (t // lat_tiles_per_b) * tiles_per_b + t % lat_tiles_per_b
        else:
            n_tiles = B * tiles_per_b
            tile = lambda t: t
        bs_full = jnp.broadcast_to(gmlp_bs[l][:, :, None], (GMLP_GROUPS, GMLP_CHUNK, GMLP_CHUNK))
        x1, h2 = _merge(x_parts, mod3, p, y_ret.reshape(B * lt, -1), o_mla.reshape(B * lt, -1),
                        gmlp_ln_g[l].reshape(1, -1), gmlp_ln_b[l].reshape(1, -1), gmlp_ws[l].astype(BF16), bs_full,
                        w_br_ret[l].astype(BF16), w_br_mla[l].astype(BF16), w_br_gmlp[l].astype(BF16),
                        w_out[l].astype(BF16), norm2_g[l].reshape(1, D), n_tiles, tile, tiles_per_b, mod_row)

        bias_t = jnp.broadcast_to(moe_bias[l][:, None], (N_EXPERTS, 128))
        idx, w, rank, counts, hp = _route(h2, moe_router[l].T, bias_t, n_tiles, tile)
        n_act = n_tiles * TM
        n_blocks = -(-(n_act * TOP_K + N_EXPERTS * (MOE_ROWS - 1)) // MOE_ROWS)
        blk_e, valid, pos_sc = _moe_plan(idx, rank, counts, n_blocks)
        xg = _dispatch(pos_sc, hp, n_blocks * MOE_ROWS)
        shared = _shared_expert(hp, sh_w_gate[l].astype(BF16), sh_w_up[l].astype(BF16), sh_w_down[l].astype(BF16))
        ys = _experts(blk_e, valid, xg, moe_w_gate, moe_w_up, moe_w_down, l, n_blocks)
        yg = _gather_rows(pos_sc, ys)
        if last:
            xs = _combine(yg, w, x1, shared, mod3, n_tiles, tile, mod_row, B * seq, lambda t: t)
        else:
            pending = (yg, w, x1, shared, mod3)
    return xs.reshape(B, seq, D)
```

```python
import functools
import math

import jax
import jax.numpy as jnp
from jax import lax
from jax.experimental import pallas as pl
from jax.experimental.pallas import tpu as pltpu
from jax.experimental.pallas import tpu_sc as plsc

F32 = jnp.float32
BF16 = jnp.bfloat16

D = 1024
GRID_W = 64
RET_HEADS = 4
RET_D = 128
RET_CHUNK = 256
RET_OUT_ROWS = 256
MLA_HEADS = 8
MLA_Q_LORA = 384
MLA_KV_LORA = 256
MLA_NOPE = 64
MLA_ROPE = 32
MLA_V = 64
MLA_V_EXT = MLA_V + 16
MLA_QK = MLA_NOPE + MLA_ROPE
HEAD_PAD = 128
ROPE_THETA = 10000.0
GMLP_GROUPS = 4
GMLP_W = 512
GMLP_CHUNK = 128
N_EXPERTS = 64
TOP_K = 6
D_EXPERT = 256
ROUTED_SCALE = 2.5
EPS = 1e-6
LOG2_E = 1.4426950408889634

TM = 256
MOE_ROWS = 512
RT = 2 * TM
SHARED_ROWS = 1024
ATT_KV_CHUNK = 1024
ATT_UNROLL = 16
ATT_HEADS = 4

C_MERGE = 0
C_UV = 3072
C_RET = 4096
C_CKV = 6144
C_KR = 6400
C_CQ = 6528
N_IN_PAD = 6912
IN_CHUNK = 768

VMEM_LIMIT = 56 * 1024 * 1024

SC_CORES = 2
SC_SUBCORES = 16
SC_ROWS = 128


def _cparams(n_axes, vmem=VMEM_LIMIT):
    return pltpu.CompilerParams(dimension_semantics=("arbitrary",) * n_axes, vmem_limit_bytes=vmem)


def _silu(x):
    return x * jax.nn.sigmoid(x)


def _dot(a, b):
    return jnp.dot(a, b, preferred_element_type=F32)


def _dot_nt(a, b):
    return lax.dot_general(a, b, (((1,), (1,)), ((), ())), preferred_element_type=F32)


def _dot_tn(a, b):
    return lax.dot_general(a, b, (((0,), (0,)), ((), ())), preferred_element_type=F32)


def _pack_bf16_pairs(x):
    n = x.shape[1] // 2
    lo = lax.bitcast_convert_type(x[:, :n].astype(BF16).astype(F32), jnp.uint32)
    hi = lax.bitcast_convert_type(x[:, n:].astype(BF16).astype(F32), jnp.uint32)
    return (lo >> 16) | hi


def _unpack_bf16_pairs(u):
    lo = lax.bitcast_convert_type(u << 16, F32)
    hi = lax.bitcast_convert_type(u & jnp.uint32(0xFFFF0000), F32)
    return lo, hi


def _ada_kernel(c_ref, w_ref, b_ref, o_ref):
    s = _silu(c_ref[...])
    o_ref[...] = _dot(s.astype(BF16), w_ref[...].astype(BF16)) + b_ref[...]


def _ada(c_rows, ada_w, ada_b):
    depth = ada_w.shape[0]
    n = ada_w.shape[2]
    cw = 1536
    return pl.pallas_call(
        _ada_kernel,
        grid=(depth, n // cw),
        in_specs=[pl.BlockSpec((8, D), lambda l, j: (0, 0)),
                  pl.BlockSpec((None, D, cw), lambda l, j: (l, 0, j)),
                  pl.BlockSpec((None, 1, cw), lambda l, j: (l, 0, j))],
        out_specs=pl.BlockSpec((None, 8, cw), lambda l, j: (l, 0, j)),
        out_shape=jax.ShapeDtypeStruct((depth, 8, n), F32),
        compiler_params=_cparams(2),
        name="ada_mod",
    )(c_rows, ada_w, ada_b.reshape(depth, 1, n))


def _modulated_rmsnorm(x, g, shift, scale):
    y = x * lax.rsqrt(jnp.mean(x * x, axis=-1, keepdims=True) + EPS) * g
    return y * (1.0 + scale) + shift


def _stream_specs(x_parts, tile, tiles_per_b):
    if len(x_parts) == 1:
        return [pl.BlockSpec((TM, D), lambda t: (tile(t), 0))]
    lat_tiles = tiles_per_b - 1

    def latent(t):
        s = tile(t)
        return ((s // tiles_per_b) * lat_tiles + jnp.minimum(s % tiles_per_b, lat_tiles - 1), 0)

    return [pl.BlockSpec((TM, D), latent), pl.BlockSpec((TM, D), lambda t: (tile(t) // tiles_per_b, 0))]


def _read_stream(x_refs, tile, tiles_per_b):
    if len(x_refs) == 1:
        return x_refs[0][...]
    is_ctx = tile(pl.program_id(0)) % tiles_per_b == tiles_per_b - 1
    return jnp.where(is_ctx, x_refs[1][...], x_refs[0][...])


def _project_in(x, mod_ref, g_ref, w_ref, o_ref, h_scr):
    h = _modulated_rmsnorm(x, g_ref[...], mod_ref[:, 0:D], mod_ref[:, D:2 * D])
    h_scr[...] = h.astype(BF16)
    for c in range(N_IN_PAD // IN_CHUNK):
        cols = slice(c * IN_CHUNK, (c + 1) * IN_CHUNK)
        o_ref[:, cols] = _dot(h_scr[...], w_ref[:, cols]).astype(BF16)


def _in_proj_kernel(*refs, n_x, tiles_per_b):
    mod_ref, g_ref, w_ref, o_ref, h_scr = refs[n_x:]
    _project_in(_read_stream(refs[:n_x], lambda t: t, tiles_per_b), mod_ref, g_ref, w_ref, o_ref, h_scr)


def _in_proj(x_parts, mod3, g, w_in_r, n_tiles, tiles_per_b, mod_row):
    n_rows = n_tiles * TM
    return pl.pallas_call(
        functools.partial(_in_proj_kernel, n_x=len(x_parts), tiles_per_b=tiles_per_b),
        grid=(n_tiles,),
        in_specs=_stream_specs(x_parts, lambda t: t, tiles_per_b) + [
                  pl.BlockSpec((None, 1, 6 * D), lambda t: (mod_row(t), 0, 0)),
                  pl.BlockSpec((1, D), lambda t: (0, 0)),
                  pl.BlockSpec((D, N_IN_PAD), lambda t: (0, 0), pipeline_mode=pl.Buffered(1))],
        out_specs=pl.BlockSpec((TM, N_IN_PAD), lambda t: (t, 0)),
        out_shape=jax.ShapeDtypeStruct((n_rows, N_IN_PAD), BF16),
        scratch_shapes=[pltpu.VMEM((TM, D), BF16)],
        compiler_params=_cparams(1),
        name="in_proj",
    )(*x_parts, mod3, g, w_in_r)


def _retention_kernel(lg_ref, q_ref, k_ref, v_ref, g_ref, gng_ref, gnb_ref, y_ref, of_scr, ob_scr,
                      *, n_lat_chunks, n_ctx_chunks):
    h = pl.program_id(1)
    lg_f = lg_ref[0, h]
    lg_b = lg_ref[1, h]
    C = RET_CHUNK
    k_scale = RET_D ** -0.5
    ri = lax.broadcasted_iota(jnp.int32, (C, C), 0).astype(F32)
    ci = lax.broadcasted_iota(jnp.int32, (C, C), 1).astype(F32)
    pos = lax.broadcasted_iota(jnp.int32, (C, 1), 0).astype(F32)
    diff = ri - ci
    d_f = jnp.where(diff >= 0, jnp.exp(lg_f * jnp.maximum(diff, 0.0)), 0.0) * k_scale
    d_b = jnp.where(diff < 0, jnp.exp(lg_b * jnp.maximum(-diff, 0.0)), 0.0) * k_scale
    qdec_f = jnp.exp(lg_f * (pos + 1.0))
    kdec_f = jnp.exp(lg_f * (C - 1.0 - pos)) * k_scale
    cdec_f = jnp.exp(lg_f * C)
    qdec_b = jnp.exp(lg_b * (C - pos))
    kdec_b = jnp.exp(lg_b * pos) * k_scale
    cdec_b = jnp.exp(lg_b * C)

    d_both = d_f + d_b

    def chunk(c, state, qdec, kdec, cdec, with_intra):
        rows = pl.ds(pl.multiple_of(c * C, C), C)
        q = q_ref[rows, :]
        k = k_ref[rows, :]
        v = v_ref[rows, :]
        o = _dot((q.astype(F32) * qdec).astype(BF16), state.astype(BF16))
        if with_intra:
            o = o + _dot((_dot_nt(q, k) * d_both).astype(BF16), v)
        kd = (k.astype(F32) * kdec).astype(BF16)
        return rows, o, state * cdec + _dot_tn(kd, v)

    n_all = n_lat_chunks + n_ctx_chunks

    def scan_body(i, states):
        s_f, s_b = states
        c_f = jnp.where(i < n_ctx_chunks, n_lat_chunks + i, i - n_ctx_chunks)
        rows, o, s_f = chunk(c_f, s_f, qdec_f, kdec_f, cdec_f, True)
        of_scr[rows, :] = o
        rows, o, s_b = chunk(n_all - 1 - i, s_b, qdec_b, kdec_b, cdec_b, False)
        ob_scr[rows, :] = o
        return s_f, s_b

    zero = jnp.zeros((RET_D, RET_D), F32)
    lax.fori_loop(0, n_all, scan_body, (zero, zero), unroll=2)

    def out_body(c, _):
        rows = pl.ds(pl.multiple_of(c * RET_OUT_ROWS, RET_OUT_ROWS), RET_OUT_ROWS)
        o = of_scr[rows, :] + ob_scr[rows, :]
        mu = jnp.mean(o, axis=-1, keepdims=True)
        var = jnp.mean(jnp.square(o - mu), axis=-1, keepdims=True)
        on = (o - mu) * lax.rsqrt(var + EPS)
        y = _silu(g_ref[rows, :].astype(F32)) * (on * gng_ref[...] + gnb_ref[...])
        y_ref[rows, :] = y.astype(BF16)
        return 0

    lax.fori_loop(0, n_all * C // RET_OUT_ROWS, out_body, 0, unroll=3)


def _retention(p3, lg, gn_g, gn_b, seq, ctx):
    B, lt, _ = p3.shape
    base = C_RET // RET_D
    kern = functools.partial(_retention_kernel, n_lat_chunks=seq // RET_CHUNK, n_ctx_chunks=ctx // RET_CHUNK)

    def col(off):
        return pl.BlockSpec((None, lt, RET_D), lambda b, h: (b, 0, base + off * RET_HEADS + h))

    return pl.pallas_call(
        kern,
        grid=(B, RET_HEADS),
        in_specs=[pl.BlockSpec(memory_space=pltpu.SMEM),
                  col(0), col(1), col(2), col(3),
                  pl.BlockSpec((1, RET_D), lambda b, h: (0, h)),
                  pl.BlockSpec((1, RET_D), lambda b, h: (0, h))],
        out_specs=pl.BlockSpec((None, lt, RET_D), lambda b, h: (b, 0, h)),
        out_shape=jax.ShapeDtypeStruct((B, lt, RET_HEADS * RET_D), BF16),
        scratch_shapes=[pltpu.VMEM((lt, RET_D), F32), pltpu.VMEM((lt, RET_D), F32)],
        compiler_params=_cparams(2),
        name="retention",
    )(lg, p3, p3, p3, p3, gn_g, gn_b)


def _rope_rotate(x, first_half):
    return jnp.where(first_half, pltpu.roll(x, HEAD_PAD - 8, 1), pltpu.roll(x, 8, 1))


def _mla_prep_kernel(cq_ref, ckv_ref, kr_ref, cos_ref, sin_ref, qa_ref, kva_ref, qn_ref, kn_ref, krg_ref,
                     wq_ref, wk_ref, wv_ref, q_ref, k_ref, v_ref):
    lane = lax.broadcasted_iota(jnp.int32, (1, HEAD_PAD), 1)
    first_half = (lane % 16) < 8
    cos = cos_ref[...]
    sin = sin_ref[...]

    def rms(x, n):
        return x * lax.rsqrt(jnp.sum(x * x, axis=-1, keepdims=True) * (1.0 / n) + EPS)

    def rope(x):
        return x * cos + _rope_rotate(x, first_half) * sin

    cq = cq_ref[...].astype(F32)
    cqn = (rms(cq, MLA_Q_LORA) * qa_ref[...]).astype(BF16)
    q_all = _dot(cqn, wq_ref[...])
    ckv = ckv_ref[...].astype(F32)
    ckvn = (rms(ckv, MLA_KV_LORA) * kva_ref[...]).astype(BF16)
    k_all = _dot(ckvn, wk_ref[...])
    v_all = _dot(ckvn, wv_ref[...])
    k_rope = rope(rms(kr_ref[...].astype(F32), MLA_ROPE) * krg_ref[...])
    scale = MLA_QK ** -0.5 * LOG2_E
    v_t = v_all.T
    ones_row = jnp.where(lax.broadcasted_iota(jnp.int32, (MLA_V_EXT - MLA_V, TM), 0) == 0, 1.0, 0.0)
    for h in range(MLA_HEADS):
        cols = slice(h * HEAD_PAD, (h + 1) * HEAD_PAD)
        qh = rope(rms(q_all[:, cols], MLA_QK) * qn_ref[...]) * scale
        q_ref[h] = qh.astype(BF16)
        kh = rms(k_all[:, cols], MLA_NOPE) * kn_ref[...] + k_rope
        k_ref[h] = kh.astype(BF16)
        v_ref[h] = jnp.concatenate([v_t[h * MLA_V:(h + 1) * MLA_V, :], ones_row], axis=0).astype(BF16)


def _mla_prep(p, cos_t, sin_t, qa_g, kva_g, qn_p, kn_p, kr_p, wq_p, wk_p, wv, B, lt):
    tiles_per_b = lt // TM
    hw = MLA_HEADS * HEAD_PAD
    const = lambda shape: pl.BlockSpec(shape, lambda b, j: (0,) * len(shape))
    head_out = pl.BlockSpec((None, MLA_HEADS, TM, HEAD_PAD), lambda b, j: (b, 0, j, 0))
    shp = jax.ShapeDtypeStruct((B, MLA_HEADS, lt, HEAD_PAD), BF16)
    v_out = pl.BlockSpec((None, MLA_HEADS, None, MLA_V_EXT, TM), lambda b, j: (b, 0, j, 0, 0))
    v_shp = jax.ShapeDtypeStruct((B, MLA_HEADS, tiles_per_b, MLA_V_EXT, TM), BF16)
    return pl.pallas_call(
        _mla_prep_kernel,
        grid=(B, tiles_per_b),
        in_specs=[pl.BlockSpec((TM, MLA_Q_LORA), lambda b, j: (b * tiles_per_b + j, C_CQ // MLA_Q_LORA)),
                  pl.BlockSpec((TM, MLA_KV_LORA), lambda b, j: (b * tiles_per_b + j, C_CKV // MLA_KV_LORA)),
                  pl.BlockSpec((TM, HEAD_PAD), lambda b, j: (b * tiles_per_b + j, C_KR // HEAD_PAD)),
                  pl.BlockSpec((TM, HEAD_PAD), lambda b, j: (j, 0)),
                  pl.BlockSpec((TM, HEAD_PAD), lambda b, j: (j, 0)),
                  const((1, MLA_Q_LORA)), const((1, MLA_KV_LORA)),
                  const((1, HEAD_PAD)), const((1, HEAD_PAD)), const((1, HEAD_PAD)),
                  const((MLA_Q_LORA, hw)), const((MLA_KV_LORA, hw)), const((MLA_KV_LORA, MLA_HEADS * MLA_V))],
        out_specs=[head_out, head_out, v_out],
        out_shape=[shp, shp, v_shp],
        compiler_params=_cparams(2),
        name="mla_prep",
    )(p, p, p, cos_t, sin_t, qa_g, kva_g, qn_p, kn_p, kr_p, wq_p, wk_p, wv)


def _attention_kernel(q_ref, k_ref, v_ref, o_ref, s_scr, *, seq, ctx, ctx_tile):
    i = pl.program_id(2)
    n_blk = ATT_KV_CHUNK // TM
    n_chunks = seq // ATT_KV_CHUNK
    unroll = math.gcd(n_chunks, ATT_UNROLL)

    def scores(hh, slot, blk, nb):
        start = blk * TM if isinstance(blk, int) else pl.multiple_of(blk * TM, TM)
        s_scr[hh, slot, 0:nb * TM, :] = _dot_nt(k_ref[hh, pl.ds(start, nb * TM), :], q_ref[hh])

    def absorb(hh, slot, blk, nb, carry):
        m, acc = carry
        s = s_scr[hh, slot, 0:nb * TM, :]
        m_new = jnp.maximum(m, jnp.max(s, axis=0, keepdims=True))
        p = jnp.exp2(s - m_new).astype(BF16)
        acc = jnp.exp2(m - m_new) * acc
        for j in range(nb):
            acc = acc + _dot(v_ref[hh, blk + j], p[j * TM:(j + 1) * TM, :])
        return m_new, acc

    def init():
        return (jnp.full((1, TM), -jnp.inf, F32), jnp.zeros((MLA_V_EXT, TM), F32))

    def write(carries):
        outs = [acc[0:MLA_V, :] / acc[MLA_V:MLA_V + 1, :] for _, acc in carries]
        o_ref[...] = jnp.concatenate(outs, axis=0).T.astype(BF16)

    def step(carries, slot, blk, nb, next_blk, next_nb):
        out = []
        for hh in range(ATT_HEADS):
            if next_blk is not None:
                scores(hh, 1 - slot, next_blk, next_nb)
            out.append(absorb(hh, slot, blk, nb, carries[hh]))
        return tuple(out)

    ctx_blk = seq // TM
    ctx_nb = ctx // TM

    @pl.when(i != ctx_tile)
    def _():
        for hh in range(ATT_HEADS):
            scores(hh, 0, ctx_blk, ctx_nb)
        carries = step(tuple(init() for _ in range(ATT_HEADS)), 0, ctx_blk, ctx_nb, 0, n_blk)
        last_blk = (n_chunks - 1) * n_blk

        def body(c, carries):
            for u in range(unroll):
                blk = (c * unroll + u) * n_blk
                carries = step(carries, (1 + u) % 2, blk, n_blk, jnp.minimum(blk + n_blk, last_blk), n_blk)
            return carries

        if n_chunks == unroll:
            for u in range(n_chunks):
                nxt = (u + 1) * n_blk if u + 1 < n_chunks else None
                carries = step(carries, (1 + u) % 2, u * n_blk, n_blk, nxt, n_blk)
            write(carries)
        else:
            write(lax.fori_loop(0, n_chunks // unroll, body, carries))

    @pl.when(i == ctx_tile)
    def _():
        for hh in range(ATT_HEADS):
            scores(hh, 0, ctx_blk, ctx_nb)
        write(step(tuple(init() for _ in range(ATT_HEADS)), 0, ctx_blk, ctx_nb, None, None))


def _attention(q, k, v, seq, ctx, n_q_tiles):
    B, H, lt, _ = q.shape
    kern = functools.partial(_attention_kernel, seq=seq, ctx=ctx, ctx_tile=seq // TM)
    return pl.pallas_call(
        kern,
        grid=(B, H // ATT_HEADS, n_q_tiles),
        in_specs=[pl.BlockSpec((None, ATT_HEADS, TM, HEAD_PAD), lambda b, h, i: (b, h, i, 0)),
                  pl.BlockSpec((None, ATT_HEADS, lt, HEAD_PAD), lambda b, h, i: (b, h, 0, 0)),
                  pl.BlockSpec((None, ATT_HEADS, lt // TM, MLA_V_EXT, TM), lambda b, h, i: (b, h, 0, 0, 0))],
        out_specs=pl.BlockSpec((None, TM, ATT_HEADS * MLA_V), lambda b, h, i: (b, i, h)),
        out_shape=jax.ShapeDtypeStruct((B, lt, H * MLA_V), BF16),
        scratch_shapes=[pltpu.VMEM((ATT_HEADS, 2, ATT_KV_CHUNK, TM), F32)],
        compiler_params=_cparams(3),
        name="attention",
    )(q, k, v)


def _merge_kernel(*refs, n_x, tile, tiles_per_b):
    (mod_ref, mg_ref, uv_ref, yr_ref, om_ref, lng_ref, lnb_ref, ws_ref, bs_ref,
     wr_ref, wm_ref, wg_ref, wo_ref, n2_ref, x1_ref, h2_ref) = refs[n_x:]
    yr = _dot(yr_ref[...], wr_ref[...])
    ym = _dot(om_ref[...], wm_ref[...])
    z = jax.nn.gelu(uv_ref[...].astype(F32))
    u = z[:, :GMLP_W]
    v = z[:, GMLP_W:]
    mu = jnp.mean(v, axis=-1, keepdims=True)
    var = jnp.mean(jnp.square(v - mu), axis=-1, keepdims=True)
    vn = ((v - mu) * lax.rsqrt(var + EPS) * lng_ref[...] + lnb_ref[...]).astype(BF16)
    gw = GMLP_W // GMLP_GROUPS
    chunks = []
    for c in range(TM // GMLP_CHUNK):
        rows = slice(c * GMLP_CHUNK, (c + 1) * GMLP_CHUNK)
        groups = [_dot(ws_ref[g], vn[rows, g * gw:(g + 1) * gw]) + bs_ref[g] for g in range(GMLP_GROUPS)]
        chunks.append(jnp.concatenate(groups, axis=1))
    sv = jnp.concatenate(chunks, axis=0)
    yg = _dot((u * sv).astype(BF16), wg_ref[...])
    gate = jax.nn.sigmoid(mg_ref[...].astype(F32))
    y = gate[:, :D] * yr + gate[:, D:2 * D] * ym + gate[:, 2 * D:] * yg
    out = _dot(y.astype(BF16), wo_ref[...])
    x1 = _read_stream(refs[:n_x], tile, tiles_per_b) + mod_ref[:, 2 * D:3 * D] * out
    x1_ref[...] = x1
    h2_ref[...] = _modulated_rmsnorm(x1, n2_ref[...], mod_ref[:, 3 * D:4 * D], mod_ref[:, 4 * D:5 * D])


def _merge(x_parts, mod3, p, y_ret, o_mla, ln_g, ln_b, ws, bs_full, w_br_ret, w_br_mla, w_br_gmlp, w_out, n2_g,
           n_tiles, tile, tiles_per_b, mod_row):
    n_rows = p.shape[0]
    const = lambda shape: pl.BlockSpec(shape, lambda t: (0,) * len(shape))
    row = lambda w, cb=0: pl.BlockSpec((TM, w), lambda t: (tile(t), cb))
    shp = jax.ShapeDtypeStruct((n_rows, D), F32)
    return pl.pallas_call(
        functools.partial(_merge_kernel, n_x=len(x_parts), tile=tile, tiles_per_b=tiles_per_b),
        grid=(n_tiles,),
        in_specs=_stream_specs(x_parts, tile, tiles_per_b) + [
                  pl.BlockSpec((None, 1, 6 * D), lambda t: (mod_row(tile(t)), 0, 0)),
                  row(3 * D, C_MERGE // (3 * D)), row(D, C_UV // D),
                  row(RET_HEADS * RET_D), row(MLA_HEADS * MLA_V),
                  const((1, GMLP_W)), const((1, GMLP_W)),
                  const((GMLP_GROUPS, GMLP_CHUNK, GMLP_CHUNK)), const((GMLP_GROUPS, GMLP_CHUNK, GMLP_CHUNK)),
                  const((RET_HEADS * RET_D, D)), const((MLA_HEADS * MLA_V, D)), const((GMLP_W, D)),
                  const((D, D)), const((1, D))],
        out_specs=[row(D), row(D)],
        out_shape=[shp, shp],
        compiler_params=_cparams(1),
        name="merge",
    )(*x_parts, mod3, p, p, y_ret, o_mla, ln_g, ln_b, ws, bs_full, w_br_ret, w_br_mla, w_br_gmlp, w_out, n2_g)


def _route_kernel(ha_ref, hb_ref, rt_ref, bt_ref, idx_ref, w_ref, rank_ref, cnt_ref, hp_ref, cnt_scr):
    @pl.when(pl.program_id(0) == 0)
    def _():
        cnt_scr[...] = jnp.zeros_like(cnt_scr)

    h = jnp.concatenate([ha_ref[...], hb_ref[...]], axis=0)
    logits = lax.dot_general(rt_ref[...], h, (((1,), (1,)), ((), ())), preferred_element_type=F32,
                             precision=lax.Precision.HIGHEST)
    scores = jax.nn.sigmoid(logits)
    sel = scores + bt_ref[:, 0:1]
    row_e = lax.broadcasted_iota(jnp.int32, (N_EXPERTS, RT), 0).astype(F32)
    row_o = lax.broadcasted_iota(jnp.int32, (8, RT), 0)
    idx_out = jnp.zeros((8, RT), F32)
    w_out = jnp.zeros((8, RT), F32)
    hits = []
    for k in range(TOP_K):
        best = jnp.max(sel, axis=0, keepdims=True)
        pick = jnp.min(jnp.where(sel == best, row_e, float(N_EXPERTS)), axis=0, keepdims=True)
        hit = row_e == pick
        hits.append(hit)
        wk = jnp.sum(jnp.where(hit, scores, 0.0), axis=0, keepdims=True)
        sel = jnp.where(hit, -jnp.inf, sel)
        idx_out = jnp.where(row_o == k, pick, idx_out)
        w_out = jnp.where(row_o == k, wk, w_out)
    w_out = w_out / jnp.sum(w_out, axis=0, keepdims=True) * ROUTED_SCALE
    idx_ref[...] = idx_out.astype(jnp.int32)
    w_ref[...] = w_out
    chosen = jnp.zeros((N_EXPERTS, RT), F32)
    for hit in hits:
        chosen = jnp.where(hit, 1.0, chosen)
    earlier = (lax.broadcasted_iota(jnp.int32, (RT, RT), 0) < lax.broadcasted_iota(jnp.int32, (RT, RT), 1))
    before = _dot(chosen.astype(BF16), jnp.where(earlier, 1.0, 0.0).astype(BF16)) + cnt_scr[:, 0:1]
    rank_out = jnp.zeros((8, RT), F32)
    for k, hit in enumerate(hits):
        rank_out = jnp.where(row_o == k, jnp.sum(jnp.where(hit, before, 0.0), axis=0, keepdims=True), rank_out)
    rank_ref[...] = rank_out.astype(jnp.int32)
    cnt_scr[...] += jnp.sum(chosen, axis=1, keepdims=True)
    cnt_ref[...] = cnt_scr[...]
    hp_ref[...] = _pack_bf16_pairs(h)


def _route(h2, router_t, bias_t, n_tiles, tile):
    const = lambda shape: pl.BlockSpec(shape, lambda t: (0,) * len(shape))
    n_act = n_tiles * TM
    assert n_tiles % 2 == 0
    n_steps = n_tiles // 2
    per_tok = pl.BlockSpec((None, 8, RT), lambda t: (t, 0, 0))
    idx_t, w_t, rank_t, counts, hp = pl.pallas_call(
        _route_kernel,
        grid=(n_steps,),
        in_specs=[pl.BlockSpec((TM, D), lambda t: (tile(2 * t), 0)),
                  pl.BlockSpec((TM, D), lambda t: (tile(2 * t + 1), 0)),
                  const((N_EXPERTS, D)), const((N_EXPERTS, 128))],
        out_specs=[per_tok, per_tok, per_tok,
                   pl.BlockSpec((N_EXPERTS, 128), lambda t: (0, 0)),
                   pl.BlockSpec((RT, D // 2), lambda t: (t, 0))],
        out_shape=[jax.ShapeDtypeStruct((n_steps, 8, RT), jnp.int32),
                   jax.ShapeDtypeStruct((n_steps, 8, RT), F32),
                   jax.ShapeDtypeStruct((n_steps, 8, RT), jnp.int32),
                   jax.ShapeDtypeStruct((N_EXPERTS, 128), F32),
                   jax.ShapeDtypeStruct((n_act, D // 2), jnp.uint32)],
        scratch_shapes=[pltpu.VMEM((N_EXPERTS, 128), F32)],
        compiler_params=_cparams(1),
        name="route",
    )(h2, h2, router_t, bias_t)
    token_major = lambda a: a.transpose(0, 2, 1).reshape(n_act, 8)
    return (token_major(idx_t)[:, :TOP_K], token_major(w_t), token_major(rank_t)[:, :TOP_K], counts[:, 0], hp)


def _shared_expert_kernel(h_ref, sg_ref, su_ref, sd_ref, o_ref):
    lo, hi = _unpack_bf16_pairs(h_ref[...])
    hb = jnp.concatenate([lo, hi], axis=1).astype(BF16)
    a = _silu(_dot(hb, sg_ref[...])) * _dot(hb, su_ref[...])
    o_ref[...] = _dot(a.astype(BF16), sd_ref[...]).astype(BF16)


def _shared_expert(hp, sg, su, sd):
    const = lambda shape: pl.BlockSpec(shape, lambda t: (0,) * len(shape))
    n_act = hp.shape[0]
    rows = math.gcd(n_act, SHARED_ROWS)
    return pl.pallas_call(
        _shared_expert_kernel,
        grid=(n_act // rows,),
        in_specs=[pl.BlockSpec((rows, D // 2), lambda t: (t, 0)),
                  const((D, D_EXPERT)), const((D, D_EXPERT)), const((D_EXPERT, D))],
        out_specs=pl.BlockSpec((rows, D), lambda t: (t, 0)),
        out_shape=jax.ShapeDtypeStruct((n_act, D), BF16),
        compiler_params=_cparams(1),
        name="shared_expert",
    )(hp, sg, su, sd)


def _dispatch(pos_sc, hp, n_rows):
    n_batches = pos_sc.shape[0]
    n_workers = SC_CORES * SC_SUBCORES
    mesh = plsc.VectorSubcoreMesh(core_axis_name="c", subcore_axis_name="s")

    @functools.partial(
        pl.kernel, mesh=mesh,
        out_type=jax.ShapeDtypeStruct((n_rows, D // 2), jnp.uint32),
        scratch_types=[pltpu.VMEM((TOP_K, SC_ROWS), jnp.int32),
                       pltpu.VMEM((SC_ROWS, D // 2), jnp.uint32),
                       pltpu.SemaphoreType.DMA],
        name="moe_dispatch")
    def scatter(pos_hbm, h_hbm, xs_hbm, idx_v, rows_v, sem):
        worker = lax.axis_index("s") * SC_CORES + lax.axis_index("c")

        @pl.loop(0, pl.cdiv(n_batches, n_workers))
        def _(j):
            b = j * n_workers + worker

            @pl.when(b < n_batches)
            def _():
                pltpu.sync_copy(pos_hbm.at[b], idx_v)
                pltpu.sync_copy(h_hbm.at[pl.ds(b * SC_ROWS, SC_ROWS)], rows_v)
                copies = [pltpu.async_copy(rows_v, xs_hbm.at[idx_v.at[k]], sem) for k in range(TOP_K)]
                for cp in copies:
                    cp.wait()

    return scatter(pos_sc, hp)


def _expert_kernel(blk_e_ref, valid_ref, x_ref, wg_ref, wu_ref, wd_ref, y_ref, wg_s, wu_s, wd_s):
    i = pl.program_id(0)
    n_valid = valid_ref[i]

    @pl.when(n_valid > 0)
    def _():
        @pl.when(jnp.logical_or(i == 0, blk_e_ref[i] != blk_e_ref[jnp.maximum(i - 1, 0)]))
        def _():
            wg_s[...] = wg_ref[...].astype(BF16)
            wu_s[...] = wu_ref[...].astype(BF16)
            wd_s[...] = wd_ref[...].astype(BF16)

        row = lax.broadcasted_iota(jnp.int32, (MOE_ROWS, 1), 0)
        lo, hi = _unpack_bf16_pairs(jnp.where(row < n_valid, x_ref[...], jnp.uint32(0)))
        x = jnp.concatenate([lo, hi], axis=1).astype(BF16)
        hb = _silu(_dot(x, wg_s[...])) * _dot(x, wu_s[...])
        y_ref[...] = _pack_bf16_pairs(_dot(hb.astype(BF16), wd_s[...]))

    @pl.when(n_valid == 0)
    def _():
        y_ref[...] = jnp.zeros_like(y_ref)


def _experts(blk_e, valid, xs, wg, wu, wd, layer, n_blocks):
    grid_spec = pltpu.PrefetchScalarGridSpec(
        num_scalar_prefetch=2,
        grid=(n_blocks,),
        in_specs=[pl.BlockSpec((MOE_ROWS, D // 2), lambda i, be, nv: (i, 0)),
                  pl.BlockSpec((None, None, D, D_EXPERT), lambda i, be, nv: (layer, be[i], 0, 0)),
                  pl.BlockSpec((None, None, D, D_EXPERT), lambda i, be, nv: (layer, be[i], 0, 0)),
                  pl.BlockSpec((None, None, D_EXPERT, D), lambda i, be, nv: (layer, be[i], 0, 0))],
        out_specs=pl.BlockSpec((MOE_ROWS, D // 2), lambda i, be, nv: (i, 0)),
        scratch_shapes=[pltpu.VMEM((D, D_EXPERT), BF16), pltpu.VMEM((D, D_EXPERT), BF16),
                        pltpu.VMEM((D_EXPERT, D), BF16)],
    )
    return pl.pallas_call(
        _expert_kernel,
        grid_spec=grid_spec,
        out_shape=jax.ShapeDtypeStruct((n_blocks * MOE_ROWS, D // 2), jnp.uint32),
        compiler_params=_cparams(1),
        name="routed_experts",
    )(blk_e, valid, xs, wg, wu, wd)


def _gather_rows(pos_sc, ys):
    n_batches = pos_sc.shape[0]
    n_workers = SC_CORES * SC_SUBCORES
    half = SC_ROWS // 2
    mesh = plsc.VectorSubcoreMesh(core_axis_name="c", subcore_axis_name="s")

    @functools.partial(
        pl.kernel, mesh=mesh,
        out_type=jax.ShapeDtypeStruct((TOP_K, n_batches * SC_ROWS, D // 2), jnp.uint32),
        scratch_types=[pltpu.VMEM((TOP_K, SC_ROWS), jnp.int32),
                       pltpu.VMEM((2, half, D // 2), jnp.uint32),
                       pltpu.SemaphoreType.DMA, pltpu.SemaphoreType.DMA],
        name="moe_gather")
    def gather(pos_hbm, y_hbm, out_hbm, idx_v, bufs, gsem, wsem):
        worker = lax.axis_index("s") * SC_CORES + lax.axis_index("c")

        @pl.loop(0, pl.cdiv(n_batches, n_workers))
        def _(j):
            b = j * n_workers + worker

            @pl.when(b < n_batches)
            def _():
                pltpu.sync_copy(pos_hbm.at[b], idx_v)
                items = [(k, h) for k in range(TOP_K) for h in range(2)]

                def fetch(i):
                    k, h = items[i]
                    return pltpu.async_copy(y_hbm.at[idx_v.at[k, pl.ds(h * half, half)]], bufs.at[i % 2], gsem)

                pending_gather = fetch(0)
                pending_write = None
                for i, (k, h) in enumerate(items):
                    pending_gather.wait()
                    if pending_write is not None:
                        pending_write.wait()
                    if i + 1 < len(items):
                        pending_gather = fetch(i + 1)
                    pending_write = pltpu.async_copy(
                        bufs.at[i % 2], out_hbm.at[k, pl.ds(b * SC_ROWS + h * half, half)], wsem)
                pending_write.wait()

    return gather(pos_sc, ys)


def _combine_tile(y_refs, w_ref, x1_ref, sh_ref, mod_ref):
    f_lo = sh_ref[:, :D // 2].astype(F32)
    f_hi = sh_ref[:, D // 2:].astype(F32)
    for k in range(TOP_K):
        lo, hi = _unpack_bf16_pairs(y_refs[k][...])
        f_lo = f_lo + lo * w_ref[:, k:k + 1]
        f_hi = f_hi + hi * w_ref[:, k:k + 1]
    return x1_ref[...] + mod_ref[:, 5 * D:6 * D] * jnp.concatenate([f_lo, f_hi], axis=1)


def _combine_kernel(*refs):
    o_ref = refs[-1]
    o_ref[...] = _combine_tile(refs[:TOP_K], *refs[TOP_K:-1])


def _combine_specs(tile, mod_row):
    planes = [pl.BlockSpec((None, TM, D // 2), lambda t, k=k: (k, t, 0)) for k in range(TOP_K)]
    return planes + [pl.BlockSpec((TM, 8), lambda t: (t, 0)),
                     pl.BlockSpec((TM, D), lambda t: (tile(t), 0)),
                     pl.BlockSpec((TM, D), lambda t: (t, 0)),
                     pl.BlockSpec((None, 1, 6 * D), lambda t: (mod_row(tile(t)), 0, 0))]


def _combine(yg, w, x1, shared, mod3, n_tiles, tile, mod_row, out_rows, out_tile):
    return pl.pallas_call(
        _combine_kernel,
        grid=(n_tiles,),
        in_specs=_combine_specs(tile, mod_row),
        out_specs=pl.BlockSpec((TM, D), lambda t: (out_tile(t), 0)),
        out_shape=jax.ShapeDtypeStruct((out_rows, D), F32),
        compiler_params=_cparams(1),
        name="moe_combine",
    )(*([yg] * TOP_K), w, x1, shared, mod3)


def _combine_in_proj_kernel(*refs):
    xs_ref, o_ref, h_scr = refs[-3:]
    mod_ref, g_ref, w_ref = refs[TOP_K + 4:-3]
    x = _combine_tile(refs[:TOP_K], *refs[TOP_K:TOP_K + 4])
    xs_ref[...] = x
    _project_in(x, mod_ref, g_ref, w_ref, o_ref, h_scr)


def _combine_in_proj(yg, w, x1, shared, mod3_prev, mod3, g, w_in_r, n_tiles, mod_row):
    n_rows = x1.shape[0]
    ident = lambda t: t
    return pl.pallas_call(
        _combine_in_proj_kernel,
        grid=(n_tiles,),
        in_specs=_combine_specs(ident, mod_row) + [
            pl.BlockSpec((None, 1, 6 * D), lambda t: (mod_row(t), 0, 0)),
            pl.BlockSpec((1, D), lambda t: (0, 0)),
            pl.BlockSpec((D, N_IN_PAD), lambda t: (0, 0), pipeline_mode=pl.Buffered(1))],
        out_specs=[pl.BlockSpec((TM, D), lambda t: (t, 0)),
                   pl.BlockSpec((TM, N_IN_PAD), lambda t: (t, 0))],
        out_shape=[jax.ShapeDtypeStruct((n_rows, D), F32),
                   jax.ShapeDtypeStruct((n_rows, N_IN_PAD), BF16)],
        scratch_shapes=[pltpu.VMEM((TM, D), BF16)],
        compiler_params=_cparams(1),
        name="combine_in_proj",
    )(*([yg] * TOP_K), w, x1, shared, mod3_prev, mod3, g, w_in_r)


def _moe_plan(idx, rank, counts, n_blocks):
    n = idx.shape[0]
    cnt = counts.reshape(N_EXPERTS).astype(jnp.int32)
    padded = (cnt + MOE_ROWS - 1) // MOE_ROWS * MOE_ROWS
    pad_end = jnp.cumsum(padded)
    pad_start = pad_end - padded
    experts = jnp.arange(N_EXPERTS, dtype=jnp.int32)
    pos = rank + jnp.sum(jnp.where(idx[:, :, None] == experts, pad_start, 0), axis=-1)
    blk_start = jnp.arange(n_blocks, dtype=jnp.int32) * MOE_ROWS
    blk_e = jnp.minimum(jnp.sum(blk_start[:, None] >= pad_end[None, :], axis=1), N_EXPERTS - 1).astype(jnp.int32)
    mine = blk_e[:, None] == experts
    in_expert = blk_start - jnp.sum(jnp.where(mine, pad_start, 0), axis=1)
    valid = jnp.clip(jnp.sum(jnp.where(mine, cnt, 0), axis=1) - in_expert, 0, MOE_ROWS).astype(jnp.int32)
    pos_sc = pos.astype(jnp.int32).reshape(n // SC_ROWS, SC_ROWS, TOP_K).transpose(0, 2, 1)
    return blk_e, valid, pos_sc


def _rope_tables(seq, ctx):
    half = MLA_ROPE // 2
    n_freq = half // 2
    inv = ROPE_THETA ** (-2.0 * jnp.arange(n_freq, dtype=F32) / half)
    t = jnp.arange(seq)
    ang_r = (t // GRID_W).astype(F32)[:, None] * inv
    ang_c = (t % GRID_W).astype(F32)[:, None] * inv
    cos = jnp.concatenate([jnp.cos(ang_r), jnp.cos(ang_r), jnp.cos(ang_c), jnp.cos(ang_c)], axis=1)
    sin = jnp.concatenate([-jnp.sin(ang_r), jnp.sin(ang_r), -jnp.sin(ang_c), jnp.sin(ang_c)], axis=1)
    pad_l = MLA_NOPE
    pad_r = HEAD_PAD - MLA_NOPE - MLA_ROPE
    cos = jnp.pad(cos, ((0, ctx), (pad_l, pad_r)), constant_values=1.0)
    cos = cos.at[seq:, :].set(1.0)
    sin = jnp.pad(sin, ((0, ctx), (pad_l, pad_r)))
    return cos, sin


def _pad_heads(w, n_heads, width, offset=0):
    k = w.shape[0]
    w = w.reshape(k, n_heads, width)
    w = jnp.pad(w, ((0, 0), (0, 0), (offset, HEAD_PAD - width - offset)))
    return w.reshape(k, n_heads * HEAD_PAD)


def _pad_vec(g, offset):
    return jnp.pad(g, (offset, HEAD_PAD - g.shape[0] - offset)).reshape(1, HEAD_PAD)


def _reorder_w_in(w):
    off_cq, off_ckv, off_kr, off_uv, off_merge = 2048, 2432, 2688, 2720, 3744
    kr = jnp.pad(w[:, off_kr:off_uv], ((0, 0), (MLA_NOPE, HEAD_PAD - MLA_NOPE - MLA_ROPE)))
    return jnp.concatenate([w[:, off_merge:], w[:, off_uv:off_merge], w[:, :off_cq],
                            w[:, off_ckv:off_kr], kr, w[:, off_cq:off_ckv]], axis=1).astype(BF16)


def kernel(x, c, ctx, c_ctx, ada_w, ada_b, norm1_g, norm2_g, w_in, ret_decay_fwd, ret_decay_bwd, ret_gn_g,
           ret_gn_b, w_br_ret, mla_qa_g, mla_w_uq, mla_kva_g, mla_w_ukv, mla_qn_g, mla_kn_g, mla_kr_g, w_br_mla,
           gmlp_ln_g, gmlp_ln_b, gmlp_ws, gmlp_bs, w_br_gmlp, w_out, moe_router, moe_bias, moe_w_gate, moe_w_up,
           moe_w_down, sh_w_gate, sh_w_up, sh_w_down):
    B, seq, _ = x.shape
    n_ctx = ctx.shape[1]
    depth = ada_w.shape[0]
    assert n_ctx == TM and seq % (2 * ATT_KV_CHUNK) == 0 and seq % TM == 0
    lt = seq + n_ctx
    tiles_per_b = lt // TM
    lat_tiles_per_b = seq // TM
    ctx_tile = lat_tiles_per_b

    def mod_row(t):
        return jnp.where(t % tiles_per_b == ctx_tile, B, t // tiles_per_b)

    c_rows = jnp.concatenate([c, c_ctx[None, :], jnp.zeros((8 - B - 1, D), F32)], axis=0)
    mod = _ada(c_rows, ada_w, ada_b)
    cos_t, sin_t = _rope_tables(seq, n_ctx)
    x_parts = (x.reshape(B * seq, D), ctx.reshape(B * n_ctx, D))

    pending = None
    for l in range(depth):
        last = l == depth - 1
        mod3 = mod[l].reshape(8, 1, 6 * D)
        if pending is None:
            p = _in_proj(x_parts, mod3, norm1_g[l].reshape(1, D), _reorder_w_in(w_in[l]), B * tiles_per_b,
                         tiles_per_b, mod_row)
        else:
            xs, p = _combine_in_proj(*pending, mod3, norm1_g[l].reshape(1, D), _reorder_w_in(w_in[l]),
                                     B * tiles_per_b, mod_row)
            x_parts = (xs,)

        lg = jnp.stack([jax.nn.log_sigmoid(ret_decay_fwd[l].astype(F32)),
                        jax.nn.log_sigmoid(ret_decay_bwd[l].astype(F32))])
        y_ret = _retention(p.reshape(B, lt, N_IN_PAD), lg, ret_gn_g[l].reshape(1, -1), ret_gn_b[l].reshape(1, -1),
                           seq, n_ctx)

        w_ukv = mla_w_ukv[l].reshape(MLA_KV_LORA, MLA_HEADS, MLA_NOPE + MLA_V)
        wk_p = _pad_heads(w_ukv[:, :, :MLA_NOPE].reshape(MLA_KV_LORA, -1), MLA_HEADS, MLA_NOPE).astype(BF16)
        wv = w_ukv[:, :, MLA_NOPE:].reshape(MLA_KV_LORA, MLA_HEADS * MLA_V).astype(BF16)
        wq_p = _pad_heads(mla_w_uq[l], MLA_HEADS, MLA_QK).astype(BF16)
        q, k, v = _mla_prep(p, cos_t, sin_t, mla_qa_g[l].reshape(1, -1), mla_kva_g[l].reshape(1, -1),
                            _pad_vec(mla_qn_g[l], 0), _pad_vec(mla_kn_g[l], 0), _pad_vec(mla_kr_g[l], MLA_NOPE),
                            wq_p, wk_p, wv, B, lt)
        o_mla = _attention(q, k, v, seq, n_ctx, lat_tiles_per_b if last else tiles_per_b)

        if last:
            n_tiles = B * lat_tiles_per_b
            tile = lambda t: (t // lat_tiles_per_b) * tiles_per_b + t % lat_tiles_per_b
        else:
            n_tiles = B * tiles_per_b
            tile = lambda t: t
        bs_full = jnp.broadcast_to(gmlp_bs[l][:, :, None], (GMLP_GROUPS, GMLP_CHUNK, GMLP_CHUNK))
        x1, h2 = _merge(x_parts, mod3, p, y_ret.reshape(B * lt, -1), o_mla.reshape(B * lt, -1),
                        gmlp_ln_g[l].reshape(1, -1), gmlp_ln_b[l].reshape(1, -1), gmlp_ws[l].astype(BF16), bs_full,
                        w_br_ret[l].astype(BF16), w_br_mla[l].astype(BF16), w_br_gmlp[l].astype(BF16),
                        w_out[l].astype(BF16), norm2_g[l].reshape(1, D), n_tiles, tile, tiles_per_b, mod_row)

        bias_t = jnp.broadcast_to(moe_bias[l][:, None], (N_EXPERTS, 128))
        idx, w, rank, counts, hp = _route(h2, moe_router[l].T, bias_t, n_tiles, tile)
        n_act = n_tiles * TM
        n_blocks = -(-(n_act * TOP_K + N_EXPERTS * (MOE_ROWS - 1)) // MOE_ROWS)
        blk_e, valid, pos_sc = _moe_plan(idx, rank, counts, n_blocks)
        xg = _dispatch(pos_sc, hp, n_blocks * MOE_ROWS)
        shared = _shared_expert(hp, sh_w_gate[l].astype(BF16), sh_w_up[l].astype(BF16), sh_w_down[l].astype(BF16))
        ys = _experts(blk_e, valid, xg, moe_w_gate, moe_w_up, moe_w_down, l, n_blocks)
        yg = _gather_rows(pos_sc, ys)
        if last:
            xs = _combine(yg, w, x1, shared, mod3, n_tiles, tile, mod_row, B * seq, lambda t: t)
        else:
            pending = (yg, w, x1, shared, mod3)
    return xs.reshape(B, seq, D)
```

```python
import functools
import math

import jax
import jax.numpy as jnp
from jax import lax
from jax.experimental import pallas as pl
from jax.experimental.pallas import tpu as pltpu
from jax.experimental.pallas import tpu_sc as plsc

F32 = jnp.float32
BF16 = jnp.bfloat16

D = 1024
GRID_W = 64
RET_HEADS = 4
RET_D = 128
RET_CHUNK = 256
RET_OUT_ROWS = 256
MLA_HEADS = 8
MLA_Q_LORA = 384
MLA_KV_LORA = 256
MLA_NOPE = 64
MLA_ROPE = 32
MLA_V = 64
MLA_V_EXT = MLA_V + 16
MLA_QK = MLA_NOPE + MLA_ROPE
HEAD_PAD = 128
ROPE_THETA = 10000.0
GMLP_GROUPS = 4
GMLP_W = 512
GMLP_CHUNK = 128
N_EXPERTS = 64
TOP_K = 6
D_EXPERT = 256
ROUTED_SCALE = 2.5
EPS = 1e-6
LOG2_E = 1.4426950408889634

LANES = 128
SUBLANES = 8
TM = 256
MOE_ROWS = 512
RT = 2 * TM
SHARED_ROWS = 1024
ATT_KV_CHUNK = 1024
ATT_HEADS = 8

C_MERGE = 0
C_UV = 3072
C_RET = 4096
C_CKV = 6144
C_KR = 6400
C_CQ = 6528
N_IN_PAD = 6912
IN_CHUNK = 768
ADA_CHUNK = 1536

VMEM_LIMIT = 56 * 1024 * 1024

SC_CORES = 2
SC_SUBCORES = 16
SC_ROWS = 128


def _cparams(n_axes, vmem=VMEM_LIMIT):
    return pltpu.CompilerParams(dimension_semantics=("arbitrary",) * n_axes, vmem_limit_bytes=vmem)


def _silu(x):
    return x * jax.nn.sigmoid(x)


def _dot(a, b):
    return jnp.dot(a, b, preferred_element_type=F32)


def _dot_nt(a, b):
    return lax.dot_general(a, b, (((1,), (1,)), ((), ())), preferred_element_type=F32)


def _dot_tn(a, b):
    return lax.dot_general(a, b, (((0,), (0,)), ((), ())), preferred_element_type=F32)


def _pack_bf16_pairs(x):
    n = x.shape[1] // 2
    lo = lax.bitcast_convert_type(x[:, :n].astype(BF16).astype(F32), jnp.uint32)
    hi = lax.bitcast_convert_type(x[:, n:].astype(BF16).astype(F32), jnp.uint32)
    return (lo >> 16) | hi


def _unpack_bf16_pairs(u):
    lo = lax.bitcast_convert_type(u << 16, F32)
    hi = lax.bitcast_convert_type(u & jnp.uint32(0xFFFF0000), F32)
    return lo, hi


def _ada_kernel(c_ref, w_ref, b_ref, o_ref):
    s = _silu(c_ref[...])
    o_ref[...] = _dot(s.astype(BF16), w_ref[...].astype(BF16)) + b_ref[...]


def _ada(c_rows, ada_w, ada_b):
    depth = ada_w.shape[0]
    n = ada_w.shape[2]
    cw = ADA_CHUNK
    return pl.pallas_call(
        _ada_kernel,
        grid=(depth, n // cw),
        in_specs=[pl.BlockSpec((SUBLANES, D), lambda l, j: (0, 0)),
                  pl.BlockSpec((None, D, cw), lambda l, j: (l, 0, j)),
                  pl.BlockSpec((None, 1, cw), lambda l, j: (l, 0, j))],
        out_specs=pl.BlockSpec((None, SUBLANES, cw), lambda l, j: (l, 0, j)),
        out_shape=jax.ShapeDtypeStruct((depth, SUBLANES, n), F32),
        compiler_params=_cparams(2),
        name="ada_mod",
    )(c_rows, ada_w, ada_b.reshape(depth, 1, n))


def _modulated_rmsnorm(x, g, shift, scale):
    y = x * lax.rsqrt(jnp.mean(x * x, axis=-1, keepdims=True) + EPS) * g
    return y * (1.0 + scale) + shift


def _stream_specs(x_parts, tile, tiles_per_b):
    if len(x_parts) == 1:
        return [pl.BlockSpec((TM, D), lambda t: (tile(t), 0))]
    lat_tiles = tiles_per_b - 1

    def latent(t):
        s = tile(t)
        return ((s // tiles_per_b) * lat_tiles + jnp.minimum(s % tiles_per_b, lat_tiles - 1), 0)

    return [pl.BlockSpec((TM, D), latent), pl.BlockSpec((TM, D), lambda t: (tile(t) // tiles_per_b, 0))]


def _read_stream(x_refs, tile, tiles_per_b):
    if len(x_refs) == 1:
        return x_refs[0][...]
    is_ctx = tile(pl.program_id(0)) % tiles_per_b == tiles_per_b - 1
    return jnp.where(is_ctx, x_refs[1][...], x_refs[0][...])


def _project_in(x, mod_ref, g_ref, w_ref, o_ref, h_scr):
    h = _modulated_rmsnorm(x, g_ref[...], mod_ref[:, 0:D], mod_ref[:, D:2 * D])
    h_scr[...] = h.astype(BF16)
    for c in range(N_IN_PAD // IN_CHUNK):
        cols = slice(c * IN_CHUNK, (c + 1) * IN_CHUNK)
        o_ref[:, cols] = _dot(h_scr[...], w_ref[:, cols]).astype(BF16)


def _in_proj_kernel(*refs, n_x, tiles_per_b):
    mod_ref, g_ref, w_ref, o_ref, h_scr = refs[n_x:]
    _project_in(_read_stream(refs[:n_x], lambda t: t, tiles_per_b), mod_ref, g_ref, w_ref, o_ref, h_scr)


def _in_proj(x_parts, mod3, g, w_in_r, n_tiles, tiles_per_b, mod_row):
    n_rows = n_tiles * TM
    return pl.pallas_call(
        functools.partial(_in_proj_kernel, n_x=len(x_parts), tiles_per_b=tiles_per_b),
        grid=(n_tiles,),
        in_specs=_stream_specs(x_parts, lambda t: t, tiles_per_b) + [
                  pl.BlockSpec((None, 1, 6 * D), lambda t: (mod_row(t), 0, 0)),
                  pl.BlockSpec((1, D), lambda t: (0, 0)),
                  pl.BlockSpec((D, N_IN_PAD), lambda t: (0, 0), pipeline_mode=pl.Buffered(1))],
        out_specs=pl.BlockSpec((TM, N_IN_PAD), lambda t: (t, 0)),
        out_shape=jax.ShapeDtypeStruct((n_rows, N_IN_PAD), BF16),
        scratch_shapes=[pltpu.VMEM((TM, D), BF16)],
        compiler_params=_cparams(1),
        name="in_proj",
    )(*x_parts, mod3, g, w_in_r)


def _retention_kernel(lg_ref, q_ref, k_ref, v_ref, g_ref, gng_ref, gnb_ref, y_ref, of_scr, ob_scr,
                      *, n_lat_chunks, n_ctx_chunks):
    h = pl.program_id(1)
    lg_f = lg_ref[0, h]
    lg_b = lg_ref[1, h]
    C = RET_CHUNK
    k_scale = RET_D ** -0.5
    ri = lax.broadcasted_iota(jnp.int32, (C, C), 0).astype(F32)
    ci = lax.broadcasted_iota(jnp.int32, (C, C), 1).astype(F32)
    pos = lax.broadcasted_iota(jnp.int32, (C, 1), 0).astype(F32)
    diff = ri - ci
    d_f = jnp.where(diff >= 0, jnp.exp(lg_f * jnp.maximum(diff, 0.0)), 0.0) * k_scale
    d_b = jnp.where(diff < 0, jnp.exp(lg_b * jnp.maximum(-diff, 0.0)), 0.0) * k_scale
    qdec_f = jnp.exp(lg_f * (pos + 1.0))
    kdec_f = jnp.exp(lg_f * (C - 1.0 - pos)) * k_scale
    cdec_f = jnp.exp(lg_f * C)
    qdec_b = jnp.exp(lg_b * (C - pos))
    kdec_b = jnp.exp(lg_b * pos) * k_scale
    cdec_b = jnp.exp(lg_b * C)

    d_both = d_f + d_b

    def chunk(c, state, qdec, kdec, cdec, with_intra):
        rows = pl.ds(pl.multiple_of(c * C, C), C)
        q = q_ref[rows, :]
        k = k_ref[rows, :]
        v = v_ref[rows, :]
        o = _dot((q.astype(F32) * qdec).astype(BF16), state.astype(BF16))
        if with_intra:
            o = o + _dot((_dot_nt(q, k) * d_both).astype(BF16), v)
        kd = (k.astype(F32) * kdec).astype(BF16)
        return rows, o, state * cdec + _dot_tn(kd, v)

    n_all = n_lat_chunks + n_ctx_chunks

    def scan_body(i, states):
        s_f, s_b = states
        c_f = jnp.where(i < n_ctx_chunks, n_lat_chunks + i, i - n_ctx_chunks)
        rows, o, s_f = chunk(c_f, s_f, qdec_f, kdec_f, cdec_f, True)
        of_scr[rows, :] = o
        rows, o, s_b = chunk(n_all - 1 - i, s_b, qdec_b, kdec_b, cdec_b, False)
        ob_scr[rows, :] = o
        return s_f, s_b

    zero = jnp.zeros((RET_D, RET_D), F32)
    lax.fori_loop(0, n_all, scan_body, (zero, zero), unroll=2)

    def out_body(c, _):
        rows = pl.ds(pl.multiple_of(c * RET_OUT_ROWS, RET_OUT_ROWS), RET_OUT_ROWS)
        o = of_scr[rows, :] + ob_scr[rows, :]
        mu = jnp.mean(o, axis=-1, keepdims=True)
        var = jnp.mean(jnp.square(o - mu), axis=-1, keepdims=True)
        on = (o - mu) * lax.rsqrt(var + EPS)
        y = _silu(g_ref[rows, :].astype(F32)) * (on * gng_ref[...] + gnb_ref[...])
        y_ref[rows, :] = y.astype(BF16)
        return 0

    lax.fori_loop(0, n_all * C // RET_OUT_ROWS, out_body, 0, unroll=3)


def _retention(p3, lg, gn_g, gn_b, seq, ctx):
    B, lt, _ = p3.shape
    base = C_RET // RET_D
    kern = functools.partial(_retention_kernel, n_lat_chunks=seq // RET_CHUNK, n_ctx_chunks=ctx // RET_CHUNK)

    def col(off):
        return pl.BlockSpec((None, lt, RET_D), lambda b, h: (b, 0, base + off * RET_HEADS + h))

    return pl.pallas_call(
        kern,
        grid=(B, RET_HEADS),
        in_specs=[pl.BlockSpec(memory_space=pltpu.SMEM),
                  col(0), col(1), col(2), col(3),
                  pl.BlockSpec((1, RET_D), lambda b, h: (0, h)),
                  pl.BlockSpec((1, RET_D), lambda b, h: (0, h))],
        out_specs=pl.BlockSpec((None, lt, RET_D), lambda b, h: (b, 0, h)),
        out_shape=jax.ShapeDtypeStruct((B, lt, RET_HEADS * RET_D), BF16),
        scratch_shapes=[pltpu.VMEM((lt, RET_D), F32), pltpu.VMEM((lt, RET_D), F32)],
        compiler_params=_cparams(2),
        name="retention",
    )(lg, p3, p3, p3, p3, gn_g, gn_b)


def _mla_prep_kernel(cq_ref, ckv_ref, kr_ref, cos_ref, sin_ref, swap_ref, qa_ref, kva_ref, qn_ref, kn_ref, krg_ref,
                     wq_ref, wk_ref, wv_ref, q_ref, k_ref, v_ref):
    cos = cos_ref[...]
    sin = sin_ref[...]

    def rms(x, n):
        return x * lax.rsqrt(jnp.sum(x * x, axis=-1, keepdims=True) * (1.0 / n) + EPS)

    def rope(x):
        return x * cos + _dot(x.astype(BF16), swap_ref[...]) * sin

    cq = cq_ref[...].astype(F32)
    cqn = (rms(cq, MLA_Q_LORA) * qa_ref[...]).astype(BF16)
    q_all = _dot(cqn, wq_ref[...])
    ckv = ckv_ref[...].astype(F32)
    ckvn = (rms(ckv, MLA_KV_LORA) * kva_ref[...]).astype(BF16)
    k_all = _dot(ckvn, wk_ref[...])
    k_rope = rope(rms(kr_ref[...].astype(F32), MLA_ROPE) * krg_ref[...])
    scale = MLA_QK ** -0.5 * LOG2_E
    v_t = _dot_nt(wv_ref[...], ckvn)
    ones_row = jnp.where(lax.broadcasted_iota(jnp.int32, (MLA_V_EXT - MLA_V, TM), 0) == 0, 1.0, 0.0)
    for h in range(MLA_HEADS):
        cols = slice(h * HEAD_PAD, (h + 1) * HEAD_PAD)
        qh = rope(rms(q_all[:, cols], MLA_QK) * qn_ref[...]) * scale
        q_ref[h] = qh.astype(BF16)
        kh = rms(k_all[:, cols], MLA_NOPE) * kn_ref[...] + k_rope
        k_ref[h] = kh.astype(BF16)
        v_ref[h] = jnp.concatenate([v_t[h * MLA_V:(h + 1) * MLA_V, :], ones_row], axis=0).astype(BF16)


def _mla_prep(p, cos_t, sin_t, swap, qa_g, kva_g, qn_p, kn_p, kr_p, wq_p, wk_p, wv, B, lt):
    tiles_per_b = lt // TM
    hw = MLA_HEADS * HEAD_PAD
    const = lambda shape: pl.BlockSpec(shape, lambda b, j: (0,) * len(shape))
    head_out = pl.BlockSpec((None, MLA_HEADS, TM, HEAD_PAD), lambda b, j: (b, 0, j, 0))
    shp = jax.ShapeDtypeStruct((B, MLA_HEADS, lt, HEAD_PAD), BF16)
    v_out = pl.BlockSpec((None, MLA_HEADS, None, MLA_V_EXT, TM), lambda b, j: (b, 0, j, 0, 0))
    v_shp = jax.ShapeDtypeStruct((B, MLA_HEADS, tiles_per_b, MLA_V_EXT, TM), BF16)
    return pl.pallas_call(
        _mla_prep_kernel,
        grid=(B, tiles_per_b),
        in_specs=[pl.BlockSpec((TM, MLA_Q_LORA), lambda b, j: (b * tiles_per_b + j, C_CQ // MLA_Q_LORA)),
                  pl.BlockSpec((TM, MLA_KV_LORA), lambda b, j: (b * tiles_per_b + j, C_CKV // MLA_KV_LORA)),
                  pl.BlockSpec((TM, HEAD_PAD), lambda b, j: (b * tiles_per_b + j, C_KR // HEAD_PAD)),
                  pl.BlockSpec((TM, HEAD_PAD), lambda b, j: (j, 0)),
                  pl.BlockSpec((TM, HEAD_PAD), lambda b, j: (j, 0)),
                  const((HEAD_PAD, HEAD_PAD)),
                  const((1, MLA_Q_LORA)), const((1, MLA_KV_LORA)),
                  const((1, HEAD_PAD)), const((1, HEAD_PAD)), const((1, HEAD_PAD)),
                  const((MLA_Q_LORA, hw)), const((MLA_KV_LORA, hw)), const((MLA_HEADS * MLA_V, MLA_KV_LORA))],
        out_specs=[head_out, head_out, v_out],
        out_shape=[shp, shp, v_shp],
        compiler_params=_cparams(2),
        name="mla_prep",
    )(p, p, p, cos_t, sin_t, swap, qa_g, kva_g, qn_p, kn_p, kr_p, wq_p, wk_p, wv)


def _attention_stages(seq, ctx):
    n_blk = ATT_KV_CHUNK // TM
    total = (seq + ctx) // TM
    first = n_blk + total % n_blk if total >= n_blk else total
    return [(total - first, first)] + [(c * n_blk, n_blk) for c in range((total - first) // n_blk)]


def _attention_kernel(q_ref, k_ref, v_ref, o_ref, s_scr, *, seq, ctx, ctx_tile):
    i = pl.program_id(2)

    def scores(hh, slot, blk, nb):
        s_scr[hh, slot, 0:nb * TM, :] = _dot_nt(k_ref[hh, blk * TM:(blk + nb) * TM, :], q_ref[hh])

    def absorb(hh, slot, blk, nb, carry):
        m, acc = carry
        s = s_scr[hh, slot, 0:nb * TM, :]
        m_new = jnp.maximum(m, jnp.max(s, axis=0, keepdims=True))
        p = jnp.exp2(s - m_new).astype(BF16)
        acc = jnp.exp2(m - m_new) * acc
        for j in range(nb):
            acc = acc + _dot(v_ref[hh, blk + j], p[j * TM:(j + 1) * TM, :])
        return m_new, acc

    def attend(stages):
        for hh in range(ATT_HEADS):
            scores(hh, 0, *stages[0])
        carries = [(jnp.full((1, TM), -jnp.inf, F32), jnp.zeros((MLA_V_EXT, TM), F32))] * ATT_HEADS
        for n, stage in enumerate(stages):
            for hh in range(ATT_HEADS):
                if n + 1 < len(stages):
                    scores(hh, (n + 1) % 2, *stages[n + 1])
                carries[hh] = absorb(hh, n % 2, *stage, carries[hh])
        outs = [acc[0:MLA_V, :] / acc[MLA_V:MLA_V + 1, :] for _, acc in carries]
        o_ref[...] = jnp.concatenate(outs, axis=0).T.astype(BF16)

    @pl.when(i != ctx_tile)
    def _():
        attend(_attention_stages(seq, ctx))

    @pl.when(i == ctx_tile)
    def _():
        attend([(seq // TM, ctx // TM)])


def _attention(q, k, v, seq, ctx, n_q_tiles):
    B, H, lt, _ = q.shape
    kern = functools.partial(_attention_kernel, seq=seq, ctx=ctx, ctx_tile=seq // TM)
    slot_rows = max(nb for _, nb in _attention_stages(seq, ctx)) * TM
    return pl.pallas_call(
        kern,
        grid=(B, H // ATT_HEADS, n_q_tiles),
        in_specs=[pl.BlockSpec((None, ATT_HEADS, TM, HEAD_PAD), lambda b, h, i: (b, h, i, 0)),
                  pl.BlockSpec((None, ATT_HEADS, lt, HEAD_PAD), lambda b, h, i: (b, h, 0, 0),
                               pipeline_mode=pl.Buffered(1)),
                  pl.BlockSpec((None, ATT_HEADS, lt // TM, MLA_V_EXT, TM), lambda b, h, i: (b, h, 0, 0, 0),
                               pipeline_mode=pl.Buffered(1))],
        out_specs=pl.BlockSpec((None, TM, ATT_HEADS * MLA_V), lambda b, h, i: (b, i, h)),
        out_shape=jax.ShapeDtypeStruct((B, lt, H * MLA_V), BF16),
        scratch_shapes=[pltpu.VMEM((ATT_HEADS, 2, slot_rows, TM), F32)],
        compiler_params=_cparams(3),
        name="attention",
    )(q, k, v)


def _merge_kernel(*refs, n_x, tile, tiles_per_b):
    (mod_ref, mg_ref, uv_ref, yr_ref, om_ref, lng_ref, lnb_ref, ws_ref, bs_ref,
     wr_ref, wm_ref, wg_ref, wo_ref, n2_ref, x1_ref, h2_ref) = refs[n_x:]
    yr = _dot(yr_ref[...], wr_ref[...])
    ym = _dot(om_ref[...], wm_ref[...])
    z = jax.nn.gelu(uv_ref[...].astype(F32))
    u = z[:, :GMLP_W]
    v = z[:, GMLP_W:]
    mu = jnp.mean(v, axis=-1, keepdims=True)
    var = jnp.mean(jnp.square(v - mu), axis=-1, keepdims=True)
    vn = ((v - mu) * lax.rsqrt(var + EPS) * lng_ref[...] + lnb_ref[...]).astype(BF16)
    gw = GMLP_W // GMLP_GROUPS
    chunks = []
    for c in range(TM // GMLP_CHUNK):
        rows = slice(c * GMLP_CHUNK, (c + 1) * GMLP_CHUNK)
        groups = [_dot(ws_ref[g], vn[rows, g * gw:(g + 1) * gw]) + bs_ref[g] for g in range(GMLP_GROUPS)]
        chunks.append(jnp.concatenate(groups, axis=1))
    sv = jnp.concatenate(chunks, axis=0)
    yg = _dot((u * sv).astype(BF16), wg_ref[...])
    gate = jax.nn.sigmoid(mg_ref[...].astype(F32))
    y = gate[:, :D] * yr + gate[:, D:2 * D] * ym + gate[:, 2 * D:] * yg
    out = _dot(y.astype(BF16), wo_ref[...])
    x1 = _read_stream(refs[:n_x], tile, tiles_per_b) + mod_ref[:, 2 * D:3 * D] * out
    x1_ref[...] = x1
    h2_ref[...] = _modulated_rmsnorm(x1, n2_ref[...], mod_ref[:, 3 * D:4 * D], mod_ref[:, 4 * D:5 * D])


def _merge(x_parts, mod3, p, y_ret, o_mla, ln_g, ln_b, ws, bs_full, w_br_ret, w_br_mla, w_br_gmlp, w_out, n2_g,
           n_tiles, tile, tiles_per_b, mod_row):
    n_rows = p.shape[0]
    const = lambda shape: pl.BlockSpec(shape, lambda t: (0,) * len(shape))
    row = lambda w, cb=0: pl.BlockSpec((TM, w), lambda t: (tile(t), cb))
    shp = jax.ShapeDtypeStruct((n_rows, D), F32)
    return pl.pallas_call(
        functools.partial(_merge_kernel, n_x=len(x_parts), tile=tile, tiles_per_b=tiles_per_b),
        grid=(n_tiles,),
        in_specs=_stream_specs(x_parts, tile, tiles_per_b) + [
                  pl.BlockSpec((None, 1, 6 * D), lambda t: (mod_row(tile(t)), 0, 0)),
                  row(3 * D, C_MERGE // (3 * D)), row(D, C_UV // D),
                  row(RET_HEADS * RET_D), row(MLA_HEADS * MLA_V),
                  const((1, GMLP_W)), const((1, GMLP_W)),
                  const((GMLP_GROUPS, GMLP_CHUNK, GMLP_CHUNK)), const((GMLP_GROUPS, GMLP_CHUNK, GMLP_CHUNK)),
                  const((RET_HEADS * RET_D, D)), const((MLA_HEADS * MLA_V, D)), const((GMLP_W, D)),
                  const((D, D)), const((1, D))],
        out_specs=[row(D), row(D)],
        out_shape=[shp, shp],
        compiler_params=_cparams(1),
        name="merge",
    )(*x_parts, mod3, p, p, y_ret, o_mla, ln_g, ln_b, ws, bs_full, w_br_ret, w_br_mla, w_br_gmlp, w_out, n2_g)


def _route_kernel(ha_ref, hb_ref, rt_ref, bt_ref, idx_ref, w_ref, rank_ref, cnt_ref, hp_ref, cnt_scr):
    @pl.when(pl.program_id(0) == 0)
    def _():
        cnt_scr[...] = jnp.zeros_like(cnt_scr)

    h = jnp.concatenate([ha_ref[...], hb_ref[...]], axis=0)
    logits = lax.dot_general(rt_ref[...], h, (((1,), (1,)), ((), ())), preferred_element_type=F32,
                             precision=lax.Precision.HIGHEST)
    scores = jax.nn.sigmoid(logits)
    sel = scores + bt_ref[:, 0:1]
    row_e = lax.broadcasted_iota(jnp.int32, (N_EXPERTS, RT), 0).astype(F32)
    row_o = lax.broadcasted_iota(jnp.int32, (SUBLANES, RT), 0)
    idx_out = jnp.zeros((SUBLANES, RT), F32)
    w_out = jnp.zeros((SUBLANES, RT), F32)
    hits = []
    for k in range(TOP_K):
        best = jnp.max(sel, axis=0, keepdims=True)
        pick = jnp.min(jnp.where(sel == best, row_e, float(N_EXPERTS)), axis=0, keepdims=True)
        hit = row_e == pick
        hits.append(hit)
        wk = jnp.sum(jnp.where(hit, scores, 0.0), axis=0, keepdims=True)
        sel = jnp.where(hit, -jnp.inf, sel)
        idx_out = jnp.where(row_o == k, pick, idx_out)
        w_out = jnp.where(row_o == k, wk, w_out)
    w_out = w_out / jnp.sum(w_out, axis=0, keepdims=True) * ROUTED_SCALE
    idx_ref[...] = idx_out.astype(jnp.int32)
    w_ref[...] = w_out
    chosen = jnp.zeros((N_EXPERTS, RT), F32)
    for hit in hits:
        chosen = jnp.where(hit, 1.0, chosen)
    earlier = (lax.broadcasted_iota(jnp.int32, (RT, RT), 0) < lax.broadcasted_iota(jnp.int32, (RT, RT), 1))
    before = _dot(chosen.astype(BF16), jnp.where(earlier, 1.0, 0.0).astype(BF16)) + cnt_scr[:, 0:1]
    rank_out = jnp.zeros((SUBLANES, RT), F32)
    for k, hit in enumerate(hits):
        rank_out = jnp.where(row_o == k, jnp.sum(jnp.where(hit, before, 0.0), axis=0, keepdims=True), rank_out)
    rank_ref[...] = rank_out.astype(jnp.int32)
    cnt_scr[...] += jnp.sum(chosen, axis=1, keepdims=True)
    cnt_ref[...] = cnt_scr[...]
    hp_ref[...] = _pack_bf16_pairs(h)


def _route(h2, router_t, bias_t, n_tiles, tile):
    const = lambda shape: pl.BlockSpec(shape, lambda t: (0,) * len(shape))
    n_act = n_tiles * TM
    assert n_tiles % 2 == 0
    n_steps = n_tiles // 2
    per_tok = pl.BlockSpec((None, SUBLANES, RT), lambda t: (t, 0, 0))
    idx_t, w_t, rank_t, counts, hp = pl.pallas_call(
        _route_kernel,
        grid=(n_steps,),
        in_specs=[pl.BlockSpec((TM, D), lambda t: (tile(2 * t), 0)),
                  pl.BlockSpec((TM, D), lambda t: (tile(2 * t + 1), 0)),
                  const((N_EXPERTS, D)), const((N_EXPERTS, LANES))],
        out_specs=[per_tok, per_tok, per_tok,
                   pl.BlockSpec((N_EXPERTS, LANES), lambda t: (0, 0)),
                   pl.BlockSpec((RT, D // 2), lambda t: (t, 0))],
        out_shape=[jax.ShapeDtypeStruct((n_steps, SUBLANES, RT), jnp.int32),
                   jax.ShapeDtypeStruct((n_steps, SUBLANES, RT), F32),
                   jax.ShapeDtypeStruct((n_steps, SUBLANES, RT), jnp.int32),
                   jax.ShapeDtypeStruct((N_EXPERTS, LANES), F32),
                   jax.ShapeDtypeStruct((n_act, D // 2), jnp.uint32)],
        scratch_shapes=[pltpu.VMEM((N_EXPERTS, LANES), F32)],
        compiler_params=_cparams(1),
        name="route",
    )(h2, h2, router_t, bias_t)
    token_major = lambda a: a.transpose(0, 2, 1).reshape(n_act, SUBLANES)
    return (token_major(idx_t)[:, :TOP_K], token_major(w_t), token_major(rank_t)[:, :TOP_K], counts[:, 0], hp)


def _shared_expert_kernel(h_ref, sg_ref, su_ref, sd_ref, o_ref):
    lo, hi = _unpack_bf16_pairs(h_ref[...])
    hb = jnp.concatenate([lo, hi], axis=1).astype(BF16)
    a = _silu(_dot(hb, sg_ref[...])) * _dot(hb, su_ref[...])
    o_ref[...] = _dot(a.astype(BF16), sd_ref[...]).astype(BF16)


def _shared_expert(hp, sg, su, sd):
    const = lambda shape: pl.BlockSpec(shape, lambda t: (0,) * len(shape))
    n_act = hp.shape[0]
    rows = math.gcd(n_act, SHARED_ROWS)
    return pl.pallas_call(
        _shared_expert_kernel,
        grid=(n_act // rows,),
        in_specs=[pl.BlockSpec((rows, D // 2), lambda t: (t, 0)),
                  const((D, D_EXPERT)), const((D, D_EXPERT)), const((D_EXPERT, D))],
        out_specs=pl.BlockSpec((rows, D), lambda t: (t, 0)),
        out_shape=jax.ShapeDtypeStruct((n_act, D), BF16),
        compiler_params=_cparams(1),
        name="shared_expert",
    )(hp, sg, su, sd)


def _dispatch(pos_sc, hp, n_rows):
    n_batches = pos_sc.shape[0]
    n_workers = SC_CORES * SC_SUBCORES
    mesh = plsc.VectorSubcoreMesh(core_axis_name="c", subcore_axis_name="s")

    @functools.partial(
        pl.kernel, mesh=mesh,
        out_type=jax.ShapeDtypeStruct((n_rows, D // 2), jnp.uint32),
        scratch_types=[pltpu.VMEM((TOP_K, SC_ROWS), jnp.int32),
                       pltpu.VMEM((SC_ROWS, D // 2), jnp.uint32),
                       pltpu.SemaphoreType.DMA],
        name="moe_dispatch")
    def scatter(pos_hbm, h_hbm, xs_hbm, idx_v, rows_v, sem):
        worker = lax.axis_index("s") * SC_CORES + lax.axis_index("c")

        @pl.loop(0, pl.cdiv(n_batches, n_workers))
        def _(j):
            b = j * n_workers + worker

            @pl.when(b < n_batches)
            def _():
                pltpu.sync_copy(pos_hbm.at[b], idx_v)
                pltpu.sync_copy(h_hbm.at[pl.ds(b * SC_ROWS, SC_ROWS)], rows_v)
                copies = [pltpu.async_copy(rows_v, xs_hbm.at[idx_v.at[k]], sem) for k in range(TOP_K)]
                for cp in copies:
                    cp.wait()

    return scatter(pos_sc, hp)


def _expert_kernel(blk_e_ref, valid_ref, x_ref, wg_ref, wu_ref, wd_ref, y_ref, wg_s, wu_s, wd_s):
    i = pl.program_id(0)
    n_valid = valid_ref[i]

    @pl.when(n_valid > 0)
    def _():
        @pl.when(jnp.logical_or(i == 0, blk_e_ref[i] != blk_e_ref[jnp.maximum(i - 1, 0)]))
        def _():
            wg_s[...] = wg_ref[...].astype(BF16)
            wu_s[...] = wu_ref[...].astype(BF16)
            wd_s[...] = wd_ref[...].astype(BF16)

        row = lax.broadcasted_iota(jnp.int32, (MOE_ROWS, 1), 0)
        lo, hi = _unpack_bf16_pairs(jnp.where(row < n_valid, x_ref[...], jnp.uint32(0)))
        x = jnp.concatenate([lo, hi], axis=1).astype(BF16)
        hb = _silu(_dot(x, wg_s[...])) * _dot(x, wu_s[...])
        y_ref[...] = _pack_bf16_pairs(_dot(hb.astype(BF16), wd_s[...]))

    @pl.when(n_valid == 0)
    def _():
        y_ref[...] = jnp.zeros_like(y_ref)


def _experts(blk_e, valid, xs, wg, wu, wd, layer, n_blocks):
    grid_spec = pltpu.PrefetchScalarGridSpec(
        num_scalar_prefetch=2,
        grid=(n_blocks,),
        in_specs=[pl.BlockSpec((MOE_ROWS, D // 2), lambda i, be, nv: (i, 0)),
                  pl.BlockSpec((None, None, D, D_EXPERT), lambda i, be, nv: (layer, be[i], 0, 0)),
                  pl.BlockSpec((None, None, D, D_EXPERT), lambda i, be, nv: (layer, be[i], 0, 0)),
                  pl.BlockSpec((None, None, D_EXPERT, D), lambda i, be, nv: (layer, be[i], 0, 0))],
        out_specs=pl.BlockSpec((MOE_ROWS, D // 2), lambda i, be, nv: (i, 0)),
        scratch_shapes=[pltpu.VMEM((D, D_EXPERT), BF16), pltpu.VMEM((D, D_EXPERT), BF16),
                        pltpu.VMEM((D_EXPERT, D), BF16)],
    )
    return pl.pallas_call(
        _expert_kernel,
        grid_spec=grid_spec,
        out_shape=jax.ShapeDtypeStruct((n_blocks * MOE_ROWS, D // 2), jnp.uint32),
        compiler_params=_cparams(1),
        name="routed_experts",
    )(blk_e, valid, xs, wg, wu, wd)


def _gather_rows(pos_sc, ys):
    n_batches = pos_sc.shape[0]
    n_workers = SC_CORES * SC_SUBCORES
    half = SC_ROWS // 2
    mesh = plsc.VectorSubcoreMesh(core_axis_name="c", subcore_axis_name="s")

    @functools.partial(
        pl.kernel, mesh=mesh,
        out_type=jax.ShapeDtypeStruct((TOP_K, n_batches * SC_ROWS, D // 2), jnp.uint32),
        scratch_types=[pltpu.VMEM((TOP_K, SC_ROWS), jnp.int32),
                       pltpu.VMEM((2, half, D // 2), jnp.uint32),
                       pltpu.SemaphoreType.DMA, pltpu.SemaphoreType.DMA],
        name="moe_gather")
    def gather(pos_hbm, y_hbm, out_hbm, idx_v, bufs, gsem, wsem):
        worker = lax.axis_index("s") * SC_CORES + lax.axis_index("c")

        @pl.loop(0, pl.cdiv(n_batches, n_workers))
        def _(j):
            b = j * n_workers + worker

            @pl.when(b < n_batches)
            def _():
                pltpu.sync_copy(pos_hbm.at[b], idx_v)
                items = [(k, h) for k in range(TOP_K) for h in range(2)]

                def fetch(i):
                    k, h = items[i]
                    return pltpu.async_copy(y_hbm.at[idx_v.at[k, pl.ds(h * half, half)]], bufs.at[i % 2], gsem)

                pending_gather = fetch(0)
                pending_write = None
                for i, (k, h) in enumerate(items):
                    pending_gather.wait()
                    if pending_write is not None:
                        pending_write.wait()
                    if i + 1 < len(items):
                        pending_gather = fetch(i + 1)
                    pending_write = pltpu.async_copy(
                        bufs.at[i % 2], out_hbm.at[k, pl.ds(b * SC_ROWS + h * half, half)], wsem)
                pending_write.wait()

    return gather(pos_sc, ys)


def _combine_tile(y_refs, w_ref, x1_ref, sh_ref, mod_ref):
    f_lo = sh_ref[:, :D // 2].astype(F32)
    f_hi = sh_ref[:, D // 2:].astype(F32)
    for k in range(TOP_K):
        lo, hi = _unpack_bf16_pairs(y_refs[k][...])
        f_lo = f_lo + lo * w_ref[:, k:k + 1]
        f_hi = f_hi + hi * w_ref[:, k:k + 1]
    return x1_ref[...] + mod_ref[:, 5 * D:6 * D] * jnp.concatenate([f_lo, f_hi], axis=1)


def _combine_kernel(*refs):
    o_ref = refs[-1]
    o_ref[...] = _combine_tile(refs[:TOP_K], *refs[TOP_K:-1])


def _combine_specs(tile, mod_row):
    planes = [pl.BlockSpec((None, TM, D // 2), lambda t, k=k: (k, t, 0)) for k in range(TOP_K)]
    return planes + [pl.BlockSpec((TM, SUBLANES), lambda t: (t, 0)),
                     pl.BlockSpec((TM, D), lambda t: (tile(t), 0)),
                     pl.BlockSpec((TM, D), lambda t: (t, 0)),
                     pl.BlockSpec((None, 1, 6 * D), lambda t: (mod_row(tile(t)), 0, 0))]


def _combine(yg, w, x1, shared, mod3, n_tiles, tile, mod_row, out_rows, out_tile):
    return pl.pallas_call(
        _combine_kernel,
        grid=(n_tiles,),
        in_specs=_combine_specs(tile, mod_row),
        out_specs=pl.BlockSpec((TM, D), lambda t: (out_tile(t), 0)),
        out_shape=jax.ShapeDtypeStruct((out_rows, D), F32),
        compiler_params=_cparams(1),
        name="moe_combine",
    )(*([yg] * TOP_K), w, x1, shared, mod3)


def _combine_in_proj_kernel(*refs):
    xs_ref, o_ref, h_scr = refs[-3:]
    mod_ref, g_ref, w_ref = refs[TOP_K + 4:-3]
    x = _combine_tile(refs[:TOP_K], *refs[TOP_K:TOP_K + 4])
    xs_ref[...] = x
    _project_in(x, mod_ref, g_ref, w_ref, o_ref, h_scr)


def _combine_in_proj(yg, w, x1, shared, mod3_prev, mod3, g, w_in_r, n_tiles, mod_row):
    n_rows = x1.shape[0]
    ident = lambda t: t
    return pl.pallas_call(
        _combine_in_proj_kernel,
        grid=(n_tiles,),
        in_specs=_combine_specs(ident, mod_row) + [
            pl.BlockSpec((None, 1, 6 * D), lambda t: (mod_row(t), 0, 0)),
            pl.BlockSpec((1, D), lambda t: (0, 0)),
            pl.BlockSpec((D, N_IN_PAD), lambda t: (0, 0), pipeline_mode=pl.Buffered(1))],
        out_specs=[pl.BlockSpec((TM, D), lambda t: (t, 0)),
                   pl.BlockSpec((TM, N_IN_PAD), lambda t: (t, 0))],
        out_shape=[jax.ShapeDtypeStruct((n_rows, D), F32),
                   jax.ShapeDtypeStruct((n_rows, N_IN_PAD), BF16)],
        scratch_shapes=[pltpu.VMEM((TM, D), BF16)],
        compiler_params=_cparams(1),
        name="combine_in_proj",
    )(*([yg] * TOP_K), w, x1, shared, mod3_prev, mod3, g, w_in_r)


def _moe_plan(idx, rank, counts, n_blocks):
    n = idx.shape[0]
    cnt = counts.reshape(N_EXPERTS).astype(jnp.int32)
    padded = (cnt + MOE_ROWS - 1) // MOE_ROWS * MOE_ROWS
    pad_end = jnp.cumsum(padded)
    pad_start = pad_end - padded
    experts = jnp.arange(N_EXPERTS, dtype=jnp.int32)
    pos = rank + jnp.sum(jnp.where(idx[:, :, None] == experts, pad_start, 0), axis=-1)
    blk_start = jnp.arange(n_blocks, dtype=jnp.int32) * MOE_ROWS
    blk_e = jnp.minimum(jnp.sum(blk_start[:, None] >= pad_end[None, :], axis=1), N_EXPERTS - 1).astype(jnp.int32)
    mine = blk_e[:, None] == experts
    in_expert = blk_start - jnp.sum(jnp.where(mine, pad_start, 0), axis=1)
    valid = jnp.clip(jnp.sum(jnp.where(mine, cnt, 0), axis=1) - in_expert, 0, MOE_ROWS).astype(jnp.int32)
    pos_sc = pos.astype(jnp.int32).reshape(n // SC_ROWS, SC_ROWS, TOP_K).transpose(0, 2, 1)
    return blk_e, valid, pos_sc


def _rope_tables(seq, ctx):
    half = MLA_ROPE // 2
    n_freq = half // 2
    inv = ROPE_THETA ** (-2.0 * jnp.arange(n_freq, dtype=F32) / half)
    t = jnp.arange(seq)
    ang_r = (t // GRID_W).astype(F32)[:, None] * inv
    ang_c = (t % GRID_W).astype(F32)[:, None] * inv
    cos = jnp.concatenate([jnp.cos(ang_r), jnp.cos(ang_r), jnp.cos(ang_c), jnp.cos(ang_c)], axis=1)
    sin = jnp.concatenate([-jnp.sin(ang_r), jnp.sin(ang_r), -jnp.sin(ang_c), jnp.sin(ang_c)], axis=1)
    pad_l = MLA_NOPE
    pad_r = HEAD_PAD - MLA_NOPE - MLA_ROPE
    cos = jnp.pad(cos, ((0, ctx), (pad_l, pad_r)), constant_values=1.0)
    cos = cos.at[seq:, :].set(1.0)
    sin = jnp.pad(sin, ((0, ctx), (pad_l, pad_r)))
    lane = jnp.arange(HEAD_PAD)
    partner = jnp.where(lane % 16 < 8, lane + 8, lane - 8)
    swap = (lane[:, None] == partner[None, :]).astype(BF16)
    return cos, sin, swap


def _pad_heads(w, n_heads, width, offset=0):
    k = w.shape[0]
    w = w.reshape(k, n_heads, width)
    w = jnp.pad(w, ((0, 0), (0, 0), (offset, HEAD_PAD - width - offset)))
    return w.reshape(k, n_heads * HEAD_PAD)


def _pad_vec(g, offset):
    return jnp.pad(g, (offset, HEAD_PAD - g.shape[0] - offset)).reshape(1, HEAD_PAD)


def _reorder_w_in(w):
    off_cq, off_ckv, off_kr, off_uv, off_merge = 2048, 2432, 2688, 2720, 3744
    kr = jnp.pad(w[:, off_kr:off_uv], ((0, 0), (MLA_NOPE, HEAD_PAD - MLA_NOPE - MLA_ROPE)))
    return jnp.concatenate([w[:, off_merge:], w[:, off_uv:off_merge], w[:, :off_cq],
                            w[:, off_ckv:off_kr], kr, w[:, off_cq:off_ckv]], axis=1).astype(BF16)


def kernel(x, c, ctx, c_ctx, ada_w, ada_b, norm1_g, norm2_g, w_in, ret_decay_fwd, ret_decay_bwd, ret_gn_g,
           ret_gn_b, w_br_ret, mla_qa_g, mla_w_uq, mla_kva_g, mla_w_ukv, mla_qn_g, mla_kn_g, mla_kr_g, w_br_mla,
           gmlp_ln_g, gmlp_ln_b, gmlp_ws, gmlp_bs, w_br_gmlp, w_out, moe_router, moe_bias, moe_w_gate, moe_w_up,
           moe_w_down, sh_w_gate, sh_w_up, sh_w_down):
    B, seq, _ = x.shape
    n_ctx = ctx.shape[1]
    depth = ada_w.shape[0]
    assert n_ctx == TM and seq % TM == 0 and B + 1 <= SUBLANES and TOP_K <= SUBLANES
    lt = seq + n_ctx
    tiles_per_b = lt // TM
    lat_tiles_per_b = seq // TM
    ctx_tile = lat_tiles_per_b

    def mod_row(t):
        return jnp.where(t % tiles_per_b == ctx_tile, B, t // tiles_per_b)

    c_rows = jnp.concatenate([c, c_ctx[None, :], jnp.zeros((SUBLANES - B - 1, D), F32)], axis=0)
    mod = _ada(c_rows, ada_w, ada_b)
    cos_t, sin_t, swap = _rope_tables(seq, n_ctx)
    x_parts = (x.reshape(B * seq, D), ctx.reshape(B * n_ctx, D))

    pending = None
    for l in range(depth):
        last = l == depth - 1
        mod3 = mod[l].reshape(SUBLANES, 1, 6 * D)
        if pending is None:
            p = _in_proj(x_parts, mod3, norm1_g[l].reshape(1, D), _reorder_w_in(w_in[l]), B * tiles_per_b,
                         tiles_per_b, mod_row)
        else:
            xs, p = _combine_in_proj(*pending, mod3, norm1_g[l].reshape(1, D), _reorder_w_in(w_in[l]),
                                     B * tiles_per_b, mod_row)
            x_parts = (xs,)

        lg = jnp.stack([jax.nn.log_sigmoid(ret_decay_fwd[l].astype(F32)),
                        jax.nn.log_sigmoid(ret_decay_bwd[l].astype(F32))])
        y_ret = _retention(p.reshape(B, lt, N_IN_PAD), lg, ret_gn_g[l].reshape(1, -1), ret_gn_b[l].reshape(1, -1),
                           seq, n_ctx)

        w_ukv = mla_w_ukv[l].reshape(MLA_KV_LORA, MLA_HEADS, MLA_NOPE + MLA_V)
        wk_p = _pad_heads(w_ukv[:, :, :MLA_NOPE].reshape(MLA_KV_LORA, -1), MLA_HEADS, MLA_NOPE).astype(BF16)
        wv = w_ukv[:, :, MLA_NOPE:].reshape(MLA_KV_LORA, MLA_HEADS * MLA_V).T.astype(BF16)
        wq_p = _pad_heads(mla_w_uq[l], MLA_HEADS, MLA_QK).astype(BF16)
        q, k, v = _mla_prep(p, cos_t, sin_t, swap, mla_qa_g[l].reshape(1, -1), mla_kva_g[l].reshape(1, -1),
                            _pad_vec(mla_qn_g[l], 0), _pad_vec(mla_kn_g[l], 0), _pad_vec(mla_kr_g[l], MLA_NOPE),
                            wq_p, wk_p, wv, B, lt)
        o_mla = _attention(q, k, v, seq, n_ctx, lat_tiles_per_b if last else tiles_per_b)

        if last:
            n_tiles = B * lat_tiles_per_b
            tile = lambda t: (t // lat_tiles_per_b) * tiles_per_b + t % lat_tiles_per_b
        else:
            n_tiles = B * tiles_per_b
            tile = lambda t: t
        bs_full = jnp.broadcast_to(gmlp_bs[l][:, :, None], (GMLP_GROUPS, GMLP_CHUNK, GMLP_CHUNK))
        x1, h2 = _merge(x_parts, mod3, p, y_ret.reshape(B * lt, -1), o_mla.reshape(B * lt, -1),
                        gmlp_ln_g[l].reshape(1, -1), gmlp_ln_b[l].reshape(1, -1), gmlp_ws[l].astype(BF16), bs_full,
                        w_br_ret[l].astype(BF16), w_br_mla[l].astype(BF16), w_br_gmlp[l].astype(BF16),
                        w_out[l].astype(BF16), norm2_g[l].reshape(1, D), n_tiles, tile, tiles_per_b, mod_row)

        bias_t = jnp.broadcast_to(moe_bias[l][:, None], (N_EXPERTS, LANES))
        idx, w, rank, counts, hp = _route(h2, moe_router[l].T, bias_t, n_tiles, tile)
        n_act = n_tiles * TM
        n_blocks = -(-(n_act * TOP_K + N_EXPERTS * (MOE_ROWS - 1)) // MOE_ROWS)
        blk_e, valid, pos_sc = _moe_plan(idx, rank, counts, n_blocks)
        xg = _dispatch(pos_sc, hp, n_blocks * MOE_ROWS)
        shared = _shared_expert(hp, sh_w_gate[l].astype(BF16), sh_w_up[l].astype(BF16), sh_w_down[l].astype(BF16))
        ys = _experts(blk_e, valid, xg, moe_w_gate, moe_w_up, moe_w_down, l, n_blocks)
        yg = _gather_rows(pos_sc, ys)
        if last:
            xs = _combine(yg, w, x1, shared, mod3, n_tiles, tile, mod_row, B * seq, lambda t: t)
        else:
            pending = (yg, w, x1, shared, mod3)
    return xs.reshape(B, seq, D)
```

```python
import functools
import math

import jax
import jax.numpy as jnp
from jax import lax
from jax.experimental import pallas as pl
from jax.experimental.pallas import tpu as pltpu
from jax.experimental.pallas import tpu_sc as plsc

F32 = jnp.float32
BF16 = jnp.bfloat16

D = 1024
GRID_W = 64
RET_HEADS = 4
RET_D = 128
RET_CHUNK = 256
RET_OUT_ROWS = 256
MLA_HEADS = 8
MLA_Q_LORA = 384
MLA_KV_LORA = 256
MLA_NOPE = 64
MLA_ROPE = 32
MLA_V = 64
MLA_V_EXT = MLA_V + 16
MLA_QK = MLA_NOPE + MLA_ROPE
HEAD_PAD = 128
ROPE_THETA = 10000.0
GMLP_GROUPS = 4
GMLP_W = 512
GMLP_CHUNK = 128
N_EXPERTS = 64
TOP_K = 6
D_EXPERT = 256
ROUTED_SCALE = 2.5
EPS = 1e-6
LOG2_E = 1.4426950408889634

LANES = 128
SUBLANES = 8
TM = 256
MOE_ROWS = 512
RT = 2 * TM
SHARED_ROWS = 1024
ATT_KV_CHUNK = 1024
ATT_HEADS = 4

C_MERGE = 0
C_UV = 3072
C_RET = 4096
C_CKV = 6144
C_KR = 6400
C_CQ = 6528
N_IN_PAD = 6912
IN_CHUNK = 768
ADA_CHUNK = 1536

VMEM_LIMIT = 56 * 1024 * 1024

SC_CORES = 2
SC_SUBCORES = 16
SC_ROWS = 128


def _cparams(n_axes, vmem=VMEM_LIMIT):
    return pltpu.CompilerParams(dimension_semantics=("arbitrary",) * n_axes, vmem_limit_bytes=vmem)


def _silu(x):
    return x * jax.nn.sigmoid(x)


def _dot(a, b):
    return jnp.dot(a, b, preferred_element_type=F32)


def _dot_nt(a, b):
    return lax.dot_general(a, b, (((1,), (1,)), ((), ())), preferred_element_type=F32)


def _dot_tn(a, b):
    return lax.dot_general(a, b, (((0,), (0,)), ((), ())), preferred_element_type=F32)


def _pack_bf16_pairs(x):
    n = x.shape[1] // 2
    lo = lax.bitcast_convert_type(x[:, :n].astype(BF16).astype(F32), jnp.uint32)
    hi = lax.bitcast_convert_type(x[:, n:].astype(BF16).astype(F32), jnp.uint32)
    return (lo >> 16) | hi


def _unpack_bf16_pairs(u):
    lo = lax.bitcast_convert_type(u << 16, F32)
    hi = lax.bitcast_convert_type(u & jnp.uint32(0xFFFF0000), F32)
    return lo, hi


def _ada_kernel(c_ref, w_ref, b_ref, o_ref):
    s = _silu(c_ref[...])
    o_ref[...] = _dot(s.astype(BF16), w_ref[...].astype(BF16)) + b_ref[...]


def _ada(c_rows, ada_w, ada_b):
    depth = ada_w.shape[0]
    n = ada_w.shape[2]
    cw = ADA_CHUNK
    return pl.pallas_call(
        _ada_kernel,
        grid=(depth, n // cw),
        in_specs=[pl.BlockSpec((SUBLANES, D), lambda l, j: (0, 0)),
                  pl.BlockSpec((None, D, cw), lambda l, j: (l, 0, j)),
                  pl.BlockSpec((None, 1, cw), lambda l, j: (l, 0, j))],
        out_specs=pl.BlockSpec((None, SUBLANES, cw), lambda l, j: (l, 0, j)),
        out_shape=jax.ShapeDtypeStruct((depth, SUBLANES, n), F32),
        compiler_params=_cparams(2),
        name="ada_mod",
    )(c_rows, ada_w, ada_b.reshape(depth, 1, n))


def _modulated_rmsnorm(x, g, shift, scale):
    y = x * lax.rsqrt(jnp.mean(x * x, axis=-1, keepdims=True) + EPS) * g
    return y * (1.0 + scale) + shift


def _stream_specs(x_parts, tile, tiles_per_b):
    if len(x_parts) == 1:
        return [pl.BlockSpec((TM, D), lambda t: (tile(t), 0))]
    lat_tiles = tiles_per_b - 1

    def latent(t):
        s = tile(t)
        return ((s // tiles_per_b) * lat_tiles + jnp.minimum(s % tiles_per_b, lat_tiles - 1), 0)

    return [pl.BlockSpec((TM, D), latent), pl.BlockSpec((TM, D), lambda t: (tile(t) // tiles_per_b, 0))]


def _read_stream(x_refs, tile, tiles_per_b):
    if len(x_refs) == 1:
        return x_refs[0][...]
    is_ctx = tile(pl.program_id(0)) % tiles_per_b == tiles_per_b - 1
    return jnp.where(is_ctx, x_refs[1][...], x_refs[0][...])


def _project_in(x, mod_ref, g_ref, w_ref, o_ref, h_scr):
    h = _modulated_rmsnorm(x, g_ref[...], mod_ref[:, 0:D], mod_ref[:, D:2 * D])
    h_scr[...] = h.astype(BF16)
    for c in range(N_IN_PAD // IN_CHUNK):
        cols = slice(c * IN_CHUNK, (c + 1) * IN_CHUNK)
        o_ref[:, cols] = _dot(h_scr[...], w_ref[:, cols]).astype(BF16)


def _in_proj_kernel(*refs, n_x, tiles_per_b):
    mod_ref, g_ref, w_ref, o_ref, h_scr = refs[n_x:]
    _project_in(_read_stream(refs[:n_x], lambda t: t, tiles_per_b), mod_ref, g_ref, w_ref, o_ref, h_scr)


def _in_proj(x_parts, mod3, g, w_in_r, n_tiles, tiles_per_b, mod_row):
    n_rows = n_tiles * TM
    return pl.pallas_call(
        functools.partial(_in_proj_kernel, n_x=len(x_parts), tiles_per_b=tiles_per_b),
        grid=(n_tiles,),
        in_specs=_stream_specs(x_parts, lambda t: t, tiles_per_b) + [
                  pl.BlockSpec((None, 1, 6 * D), lambda t: (mod_row(t), 0, 0)),
                  pl.BlockSpec((1, D), lambda t: (0, 0)),
                  pl.BlockSpec((D, N_IN_PAD), lambda t: (0, 0), pipeline_mode=pl.Buffered(1))],
        out_specs=pl.BlockSpec((TM, N_IN_PAD), lambda t: (t, 0)),
        out_shape=jax.ShapeDtypeStruct((n_rows, N_IN_PAD), BF16),
        scratch_shapes=[pltpu.VMEM((TM, D), BF16)],
        compiler_params=_cparams(1),
        name="in_proj",
    )(*x_parts, mod3, g, w_in_r)


def _retention_kernel(lg_ref, q_ref, k_ref, v_ref, g_ref, gng_ref, gnb_ref, y_ref, of_scr, ob_scr,
                      *, n_lat_chunks, n_ctx_chunks):
    h = pl.program_id(1)
    lg_f = lg_ref[0, h]
    lg_b = lg_ref[1, h]
    C = RET_CHUNK
    k_scale = RET_D ** -0.5
    ri = lax.broadcasted_iota(jnp.int32, (C, C), 0).astype(F32)
    ci = lax.broadcasted_iota(jnp.int32, (C, C), 1).astype(F32)
    pos = lax.broadcasted_iota(jnp.int32, (C, 1), 0).astype(F32)
    diff = ri - ci
    d_f = jnp.where(diff >= 0, jnp.exp(lg_f * jnp.maximum(diff, 0.0)), 0.0) * k_scale
    d_b = jnp.where(diff < 0, jnp.exp(lg_b * jnp.maximum(-diff, 0.0)), 0.0) * k_scale
    qdec_f = jnp.exp(lg_f * (pos + 1.0))
    kdec_f = jnp.exp(lg_f * (C - 1.0 - pos)) * k_scale
    cdec_f = jnp.exp(lg_f * C)
    qdec_b = jnp.exp(lg_b * (C - pos))
    kdec_b = jnp.exp(lg_b * pos) * k_scale
    cdec_b = jnp.exp(lg_b * C)

    d_both = d_f + d_b

    def chunk(c, state, qdec, kdec, cdec, with_intra):
        rows = pl.ds(pl.multiple_of(c * C, C), C)
        q = q_ref[rows, :]
        k = k_ref[rows, :]
        v = v_ref[rows, :]
        o = _dot((q.astype(F32) * qdec).astype(BF16), state.astype(BF16))
        if with_intra:
            o = o + _dot((_dot_nt(q, k) * d_both).astype(BF16), v)
        kd = (k.astype(F32) * kdec).astype(BF16)
        return rows, o, state * cdec + _dot_tn(kd, v)

    n_all = n_lat_chunks + n_ctx_chunks

    def scan_body(i, states):
        s_f, s_b = states
        c_f = jnp.where(i < n_ctx_chunks, n_lat_chunks + i, i - n_ctx_chunks)
        rows, o, s_f = chunk(c_f, s_f, qdec_f, kdec_f, cdec_f, True)
        of_scr[rows, :] = o
        rows, o, s_b = chunk(n_all - 1 - i, s_b, qdec_b, kdec_b, cdec_b, False)
        ob_scr[rows, :] = o
        return s_f, s_b

    zero = jnp.zeros((RET_D, RET_D), F32)
    lax.fori_loop(0, n_all, scan_body, (zero, zero), unroll=2)

    def out_body(c, _):
        rows = pl.ds(pl.multiple_of(c * RET_OUT_ROWS, RET_OUT_ROWS), RET_OUT_ROWS)
        o = of_scr[rows, :] + ob_scr[rows, :]
        mu = jnp.mean(o, axis=-1, keepdims=True)
        var = jnp.mean(jnp.square(o - mu), axis=-1, keepdims=True)
        on = (o - mu) * lax.rsqrt(var + EPS)
        y = _silu(g_ref[rows, :].astype(F32)) * (on * gng_ref[...] + gnb_ref[...])
        y_ref[rows, :] = y.astype(BF16)
        return 0

    lax.fori_loop(0, n_all * C // RET_OUT_ROWS, out_body, 0, unroll=3)


def _retention(p3, lg, gn_g, gn_b, seq, ctx):
    B, lt, _ = p3.shape
    base = C_RET // RET_D
    kern = functools.partial(_retention_kernel, n_lat_chunks=seq // RET_CHUNK, n_ctx_chunks=ctx // RET_CHUNK)

    def col(off):
        return pl.BlockSpec((None, lt, RET_D), lambda b, h: (b, 0, base + off * RET_HEADS + h))

    return pl.pallas_call(
        kern,
        grid=(B, RET_HEADS),
        in_specs=[pl.BlockSpec(memory_space=pltpu.SMEM),
                  col(0), col(1), col(2), col(3),
                  pl.BlockSpec((1, RET_D), lambda b, h: (0, h)),
                  pl.BlockSpec((1, RET_D), lambda b, h: (0, h))],
        out_specs=pl.BlockSpec((None, lt, RET_D), lambda b, h: (b, 0, h)),
        out_shape=jax.ShapeDtypeStruct((B, lt, RET_HEADS * RET_D), BF16),
        scratch_shapes=[pltpu.VMEM((lt, RET_D), F32), pltpu.VMEM((lt, RET_D), F32)],
        compiler_params=_cparams(2),
        name="retention",
    )(lg, p3, p3, p3, p3, gn_g, gn_b)


def _mla_prep_kernel(cq_ref, ckv_ref, kr_ref, cos_ref, sin_ref, swap_ref, qa_ref, kva_ref, qn_ref, kn_ref, krg_ref,
                     wq_ref, wk_ref, wv_ref, q_ref, k_ref, v_ref):
    cos = cos_ref[...]
    sin = sin_ref[...]

    def rms(x, n):
        return x * lax.rsqrt(jnp.sum(x * x, axis=-1, keepdims=True) * (1.0 / n) + EPS)

    def rope(x):
        return x * cos + _dot(x.astype(BF16), swap_ref[...]) * sin

    cq = cq_ref[...].astype(F32)
    cqn = (rms(cq, MLA_Q_LORA) * qa_ref[...]).astype(BF16)
    q_all = _dot(cqn, wq_ref[...])
    ckv = ckv_ref[...].astype(F32)
    ckvn = (rms(ckv, MLA_KV_LORA) * kva_ref[...]).astype(BF16)
    k_all = _dot(ckvn, wk_ref[...])
    k_rope = rope(rms(kr_ref[...].astype(F32), MLA_ROPE) * krg_ref[...])
    scale = MLA_QK ** -0.5 * LOG2_E
    v_t = _dot_nt(wv_ref[...], ckvn)
    ones_row = jnp.where(lax.broadcasted_iota(jnp.int32, (MLA_V_EXT - MLA_V, TM), 0) == 0, 1.0, 0.0)
    for h in range(MLA_HEADS):
        cols = slice(h * HEAD_PAD, (h + 1) * HEAD_PAD)
        qh = rope(rms(q_all[:, cols], MLA_QK) * qn_ref[...]) * scale
        q_ref[h] = qh.astype(BF16)
        kh = rms(k_all[:, cols], MLA_NOPE) * kn_ref[...] + k_rope
        k_ref[h] = kh.astype(BF16)
        v_ref[h] = jnp.concatenate([v_t[h * MLA_V:(h + 1) * MLA_V, :], ones_row], axis=0).astype(BF16)


def _mla_prep(p, cos_t, sin_t, swap, qa_g, kva_g, qn_p, kn_p, kr_p, wq_p, wk_p, wv, B, lt):
    tiles_per_b = lt // TM
    hw = MLA_HEADS * HEAD_PAD
    const = lambda shape: pl.BlockSpec(shape, lambda b, j: (0,) * len(shape))
    head_out = pl.BlockSpec((None, MLA_HEADS, TM, HEAD_PAD), lambda b, j: (b, 0, j, 0))
    shp = jax.ShapeDtypeStruct((B, MLA_HEADS, lt, HEAD_PAD), BF16)
    v_out = pl.BlockSpec((None, MLA_HEADS, None, MLA_V_EXT, TM), lambda b, j: (b, 0, j, 0, 0))
    v_shp = jax.ShapeDtypeStruct((B, MLA_HEADS, tiles_per_b, MLA_V_EXT, TM), BF16)
    return pl.pallas_call(
        _mla_prep_kernel,
        grid=(B, tiles_per_b),
        in_specs=[pl.BlockSpec((TM, MLA_Q_LORA), lambda b, j: (b * tiles_per_b + j, C_CQ // MLA_Q_LORA)),
                  pl.BlockSpec((TM, MLA_KV_LORA), lambda b, j: (b * tiles_per_b + j, C_CKV // MLA_KV_LORA)),
                  pl.BlockSpec((TM, HEAD_PAD), lambda b, j: (b * tiles_per_b + j, C_KR // HEAD_PAD)),
                  pl.BlockSpec((TM, HEAD_PAD), lambda b, j: (j, 0)),
                  pl.BlockSpec((TM, HEAD_PAD), lambda b, j: (j, 0)),
                  const((HEAD_PAD, HEAD_PAD)),
                  const((1, MLA_Q_LORA)), const((1, MLA_KV_LORA)),
                  const((1, HEAD_PAD)), const((1, HEAD_PAD)), const((1, HEAD_PAD)),
                  const((MLA_Q_LORA, hw)), const((MLA_KV_LORA, hw)), const((MLA_HEADS * MLA_V, MLA_KV_LORA))],
        out_specs=[head_out, head_out, v_out],
        out_shape=[shp, shp, v_shp],
        compiler_params=_cparams(2),
        name="mla_prep",
    )(p, p, p, cos_t, sin_t, swap, qa_g, kva_g, qn_p, kn_p, kr_p, wq_p, wk_p, wv)


def _attention_stages(seq, ctx):
    n_blk = ATT_KV_CHUNK // TM
    total = (seq + ctx) // TM
    first = n_blk + total % n_blk if total >= n_blk else total
    return [(total - first, first)] + [(c * n_blk, n_blk) for c in range((total - first) // n_blk)]


def _attention_kernel(q_ref, k_ref, v_ref, o_ref, s_scr, *, seq, ctx, ctx_tile):
    i = pl.program_id(2)

    def scores(hh, slot, blk, nb):
        s_scr[hh, slot, 0:nb * TM, :] = _dot_nt(k_ref[hh, blk * TM:(blk + nb) * TM, :], q_ref[hh])

    def absorb(hh, slot, blk, nb, carry):
        m, acc = carry
        s = s_scr[hh, slot, 0:nb * TM, :]
        m_new = jnp.maximum(m, jnp.max(s, axis=0, keepdims=True))
        p = jnp.exp2(s - m_new).astype(BF16)
        acc = jnp.exp2(m - m_new) * acc
        for j in range(nb):
            acc = acc + _dot(v_ref[hh, blk + j], p[j * TM:(j + 1) * TM, :])
        return m_new, acc

    def attend(stages):
        for hh in range(ATT_HEADS):
            scores(hh, 0, *stages[0])
        carries = [(jnp.full((1, TM), -jnp.inf, F32), jnp.zeros((MLA_V_EXT, TM), F32))] * ATT_HEADS
        for n, stage in enumerate(stages):
            for hh in range(ATT_HEADS):
                if n + 1 < len(stages):
                    scores(hh, (n + 1) % 2, *stages[n + 1])
                carries[hh] = absorb(hh, n % 2, *stage, carries[hh])
        outs = [acc[0:MLA_V, :] / acc[MLA_V:MLA_V + 1, :] for _, acc in carries]
        o_ref[...] = jnp.concatenate(outs, axis=0).T.astype(BF16)

    @pl.when(i != ctx_tile)
    def _():
        attend(_attention_stages(seq, ctx))

    @pl.when(i == ctx_tile)
    def _():
        attend([(seq // TM, ctx // TM)])


def _attention(q, k, v, seq, ctx, n_q_tiles):
    B, H, lt, _ = q.shape
    kern = functools.partial(_attention_kernel, seq=seq, ctx=ctx, ctx_tile=seq // TM)
    slot_rows = max(nb for _, nb in _attention_stages(seq, ctx)) * TM
    return pl.pallas_call(
        kern,
        grid=(B, H // ATT_HEADS, n_q_tiles),
        in_specs=[pl.BlockSpec((None, ATT_HEADS, TM, HEAD_PAD), lambda b, h, i: (b, h, i, 0)),
                  pl.BlockSpec((None, ATT_HEADS, lt, HEAD_PAD), lambda b, h, i: (b, h, 0, 0)),
                  pl.BlockSpec((None, ATT_HEADS, lt // TM, MLA_V_EXT, TM), lambda b, h, i: (b, h, 0, 0, 0))],
        out_specs=pl.BlockSpec((None, TM, ATT_HEADS * MLA_V), lambda b, h, i: (b, i, h)),
        out_shape=jax.ShapeDtypeStruct((B, lt, H * MLA_V), BF16),
        scratch_shapes=[pltpu.VMEM((ATT_HEADS, 2, slot_rows, TM), F32)],
        compiler_params=_cparams(3),
        name="attention",
    )(q, k, v)


def _merge_kernel(*refs, n_x, tile, tiles_per_b):
    (mod_ref, mg_ref, uv_ref, yr_ref, om_ref, lng_ref, lnb_ref, ws_ref, bs_ref,
     wr_ref, wm_ref, wg_ref, wo_ref, n2_ref, x1_ref, h2_ref) = refs[n_x:]
    yr = _dot(yr_ref[...], wr_ref[...])
    ym = _dot(om_ref[...], wm_ref[...])
    z = jax.nn.gelu(uv_ref[...].astype(F32))
    u = z[:, :GMLP_W]
    v = z[:, GMLP_W:]
    mu = jnp.mean(v, axis=-1, keepdims=True)
    var = jnp.mean(jnp.square(v - mu), axis=-1, keepdims=True)
    vn = ((v - mu) * lax.rsqrt(var + EPS) * lng_ref[...] + lnb_ref[...]).astype(BF16)
    gw = GMLP_W // GMLP_GROUPS
    chunks = []
    for c in range(TM // GMLP_CHUNK):
        rows = slice(c * GMLP_CHUNK, (c + 1) * GMLP_CHUNK)
        groups = [_dot(ws_ref[g], vn[rows, g * gw:(g + 1) * gw]) + bs_ref[g] for g in range(GMLP_GROUPS)]
        chunks.append(jnp.concatenate(groups, axis=1))
    sv = jnp.concatenate(chunks, axis=0)
    yg = _dot((u * sv).astype(BF16), wg_ref[...])
    gate = jax.nn.sigmoid(mg_ref[...].astype(F32))
    y = gate[:, :D] * yr + gate[:, D:2 * D] * ym + gate[:, 2 * D:] * yg
    out = _dot(y.astype(BF16), wo_ref[...])
    x1 = _read_stream(refs[:n_x], tile, tiles_per_b) + mod_ref[:, 2 * D:3 * D] * out
    x1_ref[...] = x1
    h2_ref[...] = _modulated_rmsnorm(x1, n2_ref[...], mod_ref[:, 3 * D:4 * D], mod_ref[:, 4 * D:5 * D])


def _merge(x_parts, mod3, p, y_ret, o_mla, ln_g, ln_b, ws, bs_full, w_br_ret, w_br_mla, w_br_gmlp, w_out, n2_g,
           n_tiles, tile, tiles_per_b, mod_row):
    n_rows = p.shape[0]
    const = lambda shape: pl.BlockSpec(shape, lambda t: (0,) * len(shape))
    row = lambda w, cb=0: pl.BlockSpec((TM, w), lambda t: (tile(t), cb))
    shp = jax.ShapeDtypeStruct((n_rows, D), F32)
    return pl.pallas_call(
        functools.partial(_merge_kernel, n_x=len(x_parts), tile=tile, tiles_per_b=tiles_per_b),
        grid=(n_tiles,),
        in_specs=_stream_specs(x_parts, tile, tiles_per_b) + [
                  pl.BlockSpec((None, 1, 6 * D), lambda t: (mod_row(tile(t)), 0, 0)),
                  row(3 * D, C_MERGE // (3 * D)), row(D, C_UV // D),
                  row(RET_HEADS * RET_D), row(MLA_HEADS * MLA_V),
                  const((1, GMLP_W)), const((1, GMLP_W)),
                  const((GMLP_GROUPS, GMLP_CHUNK, GMLP_CHUNK)), const((GMLP_GROUPS, GMLP_CHUNK, GMLP_CHUNK)),
                  const((RET_HEADS * RET_D, D)), const((MLA_HEADS * MLA_V, D)), const((GMLP_W, D)),
                  const((D, D)), const((1, D))],
        out_specs=[row(D), row(D)],
        out_shape=[shp, shp],
        compiler_params=_cparams(1),
        name="merge",
    )(*x_parts, mod3, p, p, y_ret, o_mla, ln_g, ln_b, ws, bs_full, w_br_ret, w_br_mla, w_br_gmlp, w_out, n2_g)


def _route_kernel(ha_ref, hb_ref, rt_ref, bt_ref, idx_ref, w_ref, rank_ref, cnt_ref, hp_ref, cnt_scr):
    @pl.when(pl.program_id(0) == 0)
    def _():
        cnt_scr[...] = jnp.zeros_like(cnt_scr)

    h = jnp.concatenate([ha_ref[...], hb_ref[...]], axis=0)
    logits = lax.dot_general(rt_ref[...], h, (((1,), (1,)), ((), ())), preferred_element_type=F32,
                             precision=lax.Precision.HIGHEST)
    scores = jax.nn.sigmoid(logits)
    sel = scores + bt_ref[:, 0:1]
    row_e = lax.broadcasted_iota(jnp.int32, (N_EXPERTS, RT), 0).astype(F32)
    row_o = lax.broadcasted_iota(jnp.int32, (SUBLANES, RT), 0)
    idx_out = jnp.zeros((SUBLANES, RT), F32)
    w_out = jnp.zeros((SUBLANES, RT), F32)
    hits = []
    for k in range(TOP_K):
        best = jnp.max(sel, axis=0, keepdims=True)
        pick = jnp.min(jnp.where(sel == best, row_e, float(N_EXPERTS)), axis=0, keepdims=True)
        hit = row_e == pick
        hits.append(hit)
        wk = jnp.sum(jnp.where(hit, scores, 0.0), axis=0, keepdims=True)
        sel = jnp.where(hit, -jnp.inf, sel)
        idx_out = jnp.where(row_o == k, pick, idx_out)
        w_out = jnp.where(row_o == k, wk, w_out)
    w_out = w_out / jnp.sum(w_out, axis=0, keepdims=True) * ROUTED_SCALE
    idx_ref[...] = idx_out.astype(jnp.int32)
    w_ref[...] = w_out
    chosen = jnp.zeros((N_EXPERTS, RT), F32)
    for hit in hits:
        chosen = jnp.where(hit, 1.0, chosen)
    earlier = (lax.broadcasted_iota(jnp.int32, (RT, RT), 0) < lax.broadcasted_iota(jnp.int32, (RT, RT), 1))
    before = _dot(chosen.astype(BF16), jnp.where(earlier, 1.0, 0.0).astype(BF16)) + cnt_scr[:, 0:1]
    rank_out = jnp.zeros((SUBLANES, RT), F32)
    for k, hit in enumerate(hits):
        rank_out = jnp.where(row_o == k, jnp.sum(jnp.where(hit, before, 0.0), axis=0, keepdims=True), rank_out)
    rank_ref[...] = rank_out.astype(jnp.int32)
    cnt_scr[...] += jnp.sum(chosen, axis=1, keepdims=True)
    cnt_ref[...] = cnt_scr[...]
    hp_ref[...] = _pack_bf16_pairs(h)


def _route(h2, router_t, bias_t, n_tiles, tile):
    const = lambda shape: pl.BlockSpec(shape, lambda t: (0,) * len(shape))
    n_act = n_tiles * TM
    assert n_tiles % 2 == 0
    n_steps = n_tiles // 2
    per_tok = pl.BlockSpec((None, SUBLANES, RT), lambda t: (t, 0, 0))
    idx_t, w_t, rank_t, counts, hp = pl.pallas_call(
        _route_kernel,
        grid=(n_steps,),
        in_specs=[pl.BlockSpec((TM, D), lambda t: (tile(2 * t), 0)),
                  pl.BlockSpec((TM, D), lambda t: (tile(2 * t + 1), 0)),
                  const((N_EXPERTS, D)), const((N_EXPERTS, LANES))],
        out_specs=[per_tok, per_tok, per_tok,
                   pl.BlockSpec((N_EXPERTS, LANES), lambda t: (0, 0)),
                   pl.BlockSpec((RT, D // 2), lambda t: (t, 0))],
        out_shape=[jax.ShapeDtypeStruct((n_steps, SUBLANES, RT), jnp.int32),
                   jax.ShapeDtypeStruct((n_steps, SUBLANES, RT), F32),
                   jax.ShapeDtypeStruct((n_steps, SUBLANES, RT), jnp.int32),
                   jax.ShapeDtypeStruct((N_EXPERTS, LANES), F32),
                   jax.ShapeDtypeStruct((n_act, D // 2), jnp.uint32)],
        scratch_shapes=[pltpu.VMEM((N_EXPERTS, LANES), F32)],
        compiler_params=_cparams(1),
        name="route",
    )(h2, h2, router_t, bias_t)
    token_major = lambda a: a.transpose(0, 2, 1).reshape(n_act, SUBLANES)
    return (token_major(idx_t)[:, :TOP_K], token_major(w_t), token_major(rank_t)[:, :TOP_K], counts[:, 0], hp)


def _shared_expert_kernel(h_ref, sg_ref, su_ref, sd_ref, o_ref):
    lo, hi = _unpack_bf16_pairs(h_ref[...])
    hb = jnp.concatenate([lo, hi], axis=1).astype(BF16)
    a = _silu(_dot(hb, sg_ref[...])) * _dot(hb, su_ref[...])
    o_ref[...] = _dot(a.astype(BF16), sd_ref[...]).astype(BF16)


def _shared_expert(hp, sg, su, sd):
    const = lambda shape: pl.BlockSpec(shape, lambda t: (0,) * len(shape))
    n_act = hp.shape[0]
    rows = math.gcd(n_act, SHARED_ROWS)
    return pl.pallas_call(
        _shared_expert_kernel,
        grid=(n_act // rows,),
        in_specs=[pl.BlockSpec((rows, D // 2), lambda t: (t, 0)),
                  const((D, D_EXPERT)), const((D, D_EXPERT)), const((D_EXPERT, D))],
        out_specs=pl.BlockSpec((rows, D), lambda t: (t, 0)),
        out_shape=jax.ShapeDtypeStruct((n_act, D), BF16),
        compiler_params=_cparams(1),
        name="shared_expert",
    )(hp, sg, su, sd)


def _dispatch(pos_sc, hp, n_rows):
    n_batches = pos_sc.shape[0]
    n_workers = SC_CORES * SC_SUBCORES
    mesh = plsc.VectorSubcoreMesh(core_axis_name="c", subcore_axis_name="s")

    @functools.partial(
        pl.kernel, mesh=mesh,
        out_type=jax.ShapeDtypeStruct((n_rows, D // 2), jnp.uint32),
        scratch_types=[pltpu.VMEM((TOP_K, SC_ROWS), jnp.int32),
                       pltpu.VMEM((SC_ROWS, D // 2), jnp.uint32),
                       pltpu.SemaphoreType.DMA],
        name="moe_dispatch")
    def scatter(pos_hbm, h_hbm, xs_hbm, idx_v, rows_v, sem):
        worker = lax.axis_index("s") * SC_CORES + lax.axis_index("c")

        @pl.loop(0, pl.cdiv(n_batches, n_workers))
        def _(j):
            b = j * n_workers + worker

            @pl.when(b < n_batches)
            def _():
                pltpu.sync_copy(pos_hbm.at[b], idx_v)
                pltpu.sync_copy(h_hbm.at[pl.ds(b * SC_ROWS, SC_ROWS)], rows_v)
                copies = [pltpu.async_copy(rows_v, xs_hbm.at[idx_v.at[k]], sem) for k in range(TOP_K)]
                for cp in copies:
                    cp.wait()

    return scatter(pos_sc, hp)


def _expert_kernel(blk_e_ref, valid_ref, x_ref, wg_ref, wu_ref, wd_ref, y_ref, wg_s, wu_s, wd_s):
    i = pl.program_id(0)
    n_valid = valid_ref[i]

    @pl.when(n_valid > 0)
    def _():
        @pl.when(jnp.logical_or(i == 0, blk_e_ref[i] != blk_e_ref[jnp.maximum(i - 1, 0)]))
        def _():
            wg_s[...] = wg_ref[...].astype(BF16)
            wu_s[...] = wu_ref[...].astype(BF16)
            wd_s[...] = wd_ref[...].astype(BF16)

        row = lax.broadcasted_iota(jnp.int32, (MOE_ROWS, 1), 0)
        lo, hi = _unpack_bf16_pairs(jnp.where(row < n_valid, x_ref[...], jnp.uint32(0)))
        x = jnp.concatenate([lo, hi], axis=1).astype(BF16)
        hb = _silu(_dot(x, wg_s[...])) * _dot(x, wu_s[...])
        y_ref[...] = _pack_bf16_pairs(_dot(hb.astype(BF16), wd_s[...]))

    @pl.when(n_valid == 0)
    def _():
        y_ref[...] = jnp.zeros_like(y_ref)


def _experts(blk_e, valid, xs, wg, wu, wd, layer, n_blocks):
    grid_spec = pltpu.PrefetchScalarGridSpec(
        num_scalar_prefetch=2,
        grid=(n_blocks,),
        in_specs=[pl.BlockSpec((MOE_ROWS, D // 2), lambda i, be, nv: (i, 0)),
                  pl.BlockSpec((None, None, D, D_EXPERT), lambda i, be, nv: (layer, be[i], 0, 0)),
                  pl.BlockSpec((None, None, D, D_EXPERT), lambda i, be, nv: (layer, be[i], 0, 0)),
                  pl.BlockSpec((None, None, D_EXPERT, D), lambda i, be, nv: (layer, be[i], 0, 0))],
        out_specs=pl.BlockSpec((MOE_ROWS, D // 2), lambda i, be, nv: (i, 0)),
        scratch_shapes=[pltpu.VMEM((D, D_EXPERT), BF16), pltpu.VMEM((D, D_EXPERT), BF16),
                        pltpu.VMEM((D_EXPERT, D), BF16)],
    )
    return pl.pallas_call(
        _expert_kernel,
        grid_spec=grid_spec,
        out_shape=jax.ShapeDtypeStruct((n_blocks * MOE_ROWS, D // 2), jnp.uint32),
        compiler_params=_cparams(1),
        name="routed_experts",
    )(blk_e, valid, xs, wg, wu, wd)


def _gather_rows(pos_sc, ys):
    n_batches = pos_sc.shape[0]
    n_workers = SC_CORES * SC_SUBCORES
    half = SC_ROWS // 2
    mesh = plsc.VectorSubcoreMesh(core_axis_name="c", subcore_axis_name="s")

    @functools.partial(
        pl.kernel, mesh=mesh,
        out_type=jax.ShapeDtypeStruct((TOP_K, n_batches * SC_ROWS, D // 2), jnp.uint32),
        scratch_types=[pltpu.VMEM((TOP_K, SC_ROWS), jnp.int32),
                       pltpu.VMEM((2, half, D // 2), jnp.uint32),
                       pltpu.SemaphoreType.DMA, pltpu.SemaphoreType.DMA],
        name="moe_gather")
    def gather(pos_hbm, y_hbm, out_hbm, idx_v, bufs, gsem, wsem):
        worker = lax.axis_index("s") * SC_CORES + lax.axis_index("c")

        @pl.loop(0, pl.cdiv(n_batches, n_workers))
        def _(j):
            b = j * n_workers + worker

            @pl.when(b < n_batches)
            def _():
                pltpu.sync_copy(pos_hbm.at[b], idx_v)
                items = [(k, h) for k in range(TOP_K) for h in range(2)]

                def fetch(i):
                    k, h = items[i]
                    return pltpu.async_copy(y_hbm.at[idx_v.at[k, pl.ds(h * half, half)]], bufs.at[i % 2], gsem)

                pending_gather = fetch(0)
                pending_write = None
                for i, (k, h) in enumerate(items):
                    pending_gather.wait()
                    if pending_write is not None:
                        pending_write.wait()
                    if i + 1 < len(items):
                        pending_gather = fetch(i + 1)
                    pending_write = pltpu.async_copy(
                        bufs.at[i % 2], out_hbm.at[k, pl.ds(b * SC_ROWS + h * half, half)], wsem)
                pending_write.wait()

    return gather(pos_sc, ys)


def _combine_tile(y_refs, w_ref, x1_ref, sh_ref, mod_ref):
    f_lo = sh_ref[:, :D // 2].astype(F32)
    f_hi = sh_ref[:, D // 2:].astype(F32)
    for k in range(TOP_K):
        lo, hi = _unpack_bf16_pairs(y_refs[k][...])
        f_lo = f_lo + lo * w_ref[:, k:k + 1]
        f_hi = f_hi + hi * w_ref[:, k:k + 1]
    return x1_ref[...] + mod_ref[:, 5 * D:6 * D] * jnp.concatenate([f_lo, f_hi], axis=1)


def _combine_kernel(*refs):
    o_ref = refs[-1]
    o_ref[...] = _combine_tile(refs[:TOP_K], *refs[TOP_K:-1])


def _combine_specs(tile, mod_row):
    planes = [pl.BlockSpec((None, TM, D // 2), lambda t, k=k: (k, t, 0)) for k in range(TOP_K)]
    return planes + [pl.BlockSpec((TM, SUBLANES), lambda t: (t, 0)),
                     pl.BlockSpec((TM, D), lambda t: (tile(t), 0)),
                     pl.BlockSpec((TM, D), lambda t: (t, 0)),
                     pl.BlockSpec((None, 1, 6 * D), lambda t: (mod_row(tile(t)), 0, 0))]


def _combine(yg, w, x1, shared, mod3, n_tiles, tile, mod_row, out_rows, out_tile):
    return pl.pallas_call(
        _combine_kernel,
        grid=(n_tiles,),
        in_specs=_combine_specs(tile, mod_row),
        out_specs=pl.BlockSpec((TM, D), lambda t: (out_tile(t), 0)),
        out_shape=jax.ShapeDtypeStruct((out_rows, D), F32),
        compiler_params=_cparams(1),
        name="moe_combine",
    )(*([yg] * TOP_K), w, x1, shared, mod3)


def _combine_in_proj_kernel(*refs):
    xs_ref, o_ref, h_scr = refs[-3:]
    mod_ref, g_ref, w_ref = refs[TOP_K + 4:-3]
    x = _combine_tile(refs[:TOP_K], *refs[TOP_K:TOP_K + 4])
    xs_ref[...] = x
    _project_in(x, mod_ref, g_ref, w_ref, o_ref, h_scr)


def _combine_in_proj(yg, w, x1, shared, mod3_prev, mod3, g, w_in_r, n_tiles, mod_row):
    n_rows = x1.shape[0]
    ident = lambda t: t
    return pl.pallas_call(
        _combine_in_proj_kernel,
        grid=(n_tiles,),
        in_specs=_combine_specs(ident, mod_row) + [
            pl.BlockSpec((None, 1, 6 * D), lambda t: (mod_row(t), 0, 0)),
            pl.BlockSpec((1, D), lambda t: (0, 0)),
            pl.BlockSpec((D, N_IN_PAD), lambda t: (0, 0), pipeline_mode=pl.Buffered(1))],
        out_specs=[pl.BlockSpec((TM, D), lambda t: (t, 0)),
                   pl.BlockSpec((TM, N_IN_PAD), lambda t: (t, 0))],
        out_shape=[jax.ShapeDtypeStruct((n_rows, D), F32),
                   jax.ShapeDtypeStruct((n_rows, N_IN_PAD), BF16)],
        scratch_shapes=[pltpu.VMEM((TM, D), BF16)],
        compiler_params=_cparams(1),
        name="combine_in_proj",
    )(*([yg] * TOP_K), w, x1, shared, mod3_prev, mod3, g, w_in_r)


def _moe_plan(idx, rank, counts, n_blocks):
    n = idx.shape[0]
    cnt = counts.reshape(N_EXPERTS).astype(jnp.int32)
    padded = (cnt + MOE_ROWS - 1) // MOE_ROWS * MOE_ROWS
    pad_end = jnp.cumsum(padded)
    pad_start = pad_end - padded
    experts = jnp.arange(N_EXPERTS, dtype=jnp.int32)
    pos = rank + jnp.sum(jnp.where(idx[:, :, None] == experts, pad_start, 0), axis=-1)
    blk_start = jnp.arange(n_blocks, dtype=jnp.int32) * MOE_ROWS
    blk_e = jnp.minimum(jnp.sum(blk_start[:, None] >= pad_end[None, :], axis=1), N_EXPERTS - 1).astype(jnp.int32)
    mine = blk_e[:, None] == experts
    in_expert = blk_start - jnp.sum(jnp.where(mine, pad_start, 0), axis=1)
    valid = jnp.clip(jnp.sum(jnp.where(mine, cnt, 0), axis=1) - in_expert, 0, MOE_ROWS).astype(jnp.int32)
    pos_sc = pos.astype(jnp.int32).reshape(n // SC_ROWS, SC_ROWS, TOP_K).transpose(0, 2, 1)
    return blk_e, valid, pos_sc


def _rope_tables(seq, ctx):
    half = MLA_ROPE // 2
    n_freq = half // 2
    inv = ROPE_THETA ** (-2.0 * jnp.arange(n_freq, dtype=F32) / half)
    t = jnp.arange(seq)
    ang_r = (t // GRID_W).astype(F32)[:, None] * inv
    ang_c = (t % GRID_W).astype(F32)[:, None] * inv
    cos = jnp.concatenate([jnp.cos(ang_r), jnp.cos(ang_r), jnp.cos(ang_c), jnp.cos(ang_c)], axis=1)
    sin = jnp.concatenate([-jnp.sin(ang_r), jnp.sin(ang_r), -jnp.sin(ang_c), jnp.sin(ang_c)], axis=1)
    pad_l = MLA_NOPE
    pad_r = HEAD_PAD - MLA_NOPE - MLA_ROPE
    cos = jnp.pad(cos, ((0, ctx), (pad_l, pad_r)), constant_values=1.0)
    cos = cos.at[seq:, :].set(1.0)
    sin = jnp.pad(sin, ((0, ctx), (pad_l, pad_r)))
    lane = jnp.arange(HEAD_PAD)
    partner = jnp.where(lane % 16 < 8, lane + 8, lane - 8)
    swap = (lane[:, None] == partner[None, :]).astype(BF16)
    return cos, sin, swap


def _pad_heads(w, n_heads, width, offset=0):
    k = w.shape[0]
    w = w.reshape(k, n_heads, width)
    w = jnp.pad(w, ((0, 0), (0, 0), (offset, HEAD_PAD - width - offset)))
    return w.reshape(k, n_heads * HEAD_PAD)


def _pad_vec(g, offset):
    return jnp.pad(g, (offset, HEAD_PAD - g.shape[0] - offset)).reshape(1, HEAD_PAD)


_W_IN_MOVES = ((3744, C_MERGE, 3072), (2720, C_UV, 1024), (0, C_RET, 2048), (2432, C_CKV, 256),
               (2688, C_KR + MLA_NOPE, MLA_ROPE), (2048, C_CQ, 384))
W_IN_ROWS = 128


def _reorder_w_in_kernel(w_ref, o_ref):
    o_ref[:, C_KR:C_KR + HEAD_PAD] = jnp.zeros((W_IN_ROWS, HEAD_PAD), BF16)
    for src, dst, width in _W_IN_MOVES:
        o_ref[:, dst:dst + width] = w_ref[:, src:src + width].astype(BF16)


def _reorder_w_in(w_in, layer):
    _, d_in, n_in = w_in.shape
    return pl.pallas_call(
        _reorder_w_in_kernel,
        grid=(d_in // W_IN_ROWS,),
        in_specs=[pl.BlockSpec((None, W_IN_ROWS, n_in), lambda r: (layer, r, 0))],
        out_specs=pl.BlockSpec((W_IN_ROWS, N_IN_PAD), lambda r: (r, 0)),
        out_shape=jax.ShapeDtypeStruct((d_in, N_IN_PAD), BF16),
        compiler_params=_cparams(1),
        name="reorder_w_in",
    )(w_in)


def kernel(x, c, ctx, c_ctx, ada_w, ada_b, norm1_g, norm2_g, w_in, ret_decay_fwd, ret_decay_bwd, ret_gn_g,
           ret_gn_b, w_br_ret, mla_qa_g, mla_w_uq, mla_kva_g, mla_w_ukv, mla_qn_g, mla_kn_g, mla_kr_g, w_br_mla,
           gmlp_ln_g, gmlp_ln_b, gmlp_ws, gmlp_bs, w_br_gmlp, w_out, moe_router, moe_bias, moe_w_gate, moe_w_up,
           moe_w_down, sh_w_gate, sh_w_up, sh_w_down):
    B, seq, _ = x.shape
    n_ctx = ctx.shape[1]
    depth = ada_w.shape[0]
    assert n_ctx == TM and seq % TM == 0 and B + 1 <= SUBLANES and TOP_K <= SUBLANES
    lt = seq + n_ctx
    tiles_per_b = lt // TM
    lat_tiles_per_b = seq // TM
    ctx_tile = lat_tiles_per_b

    def mod_row(t):
        return jnp.where(t % tiles_per_b == ctx_tile, B, t // tiles_per_b)

    c_rows = jnp.concatenate([c, c_ctx[None, :], jnp.zeros((SUBLANES - B - 1, D), F32)], axis=0)
    mod = _ada(c_rows, ada_w, ada_b)
    cos_t, sin_t, swap = _rope_tables(seq, n_ctx)
    x_parts = (x.reshape(B * seq, D), ctx.reshape(B * n_ctx, D))

    pending = None
    for l in range(depth):
        last = l == depth - 1
        mod3 = mod[l].reshape(SUBLANES, 1, 6 * D)
        if pending is None:
            p = _in_proj(x_parts, mod3, norm1_g[l].reshape(1, D), _reorder_w_in(w_in, l), B * tiles_per_b,
                         tiles_per_b, mod_row)
        else:
            xs, p = _combine_in_proj(*pending, mod3, norm1_g[l].reshape(1, D), _reorder_w_in(w_in, l),
                                     B * tiles_per_b, mod_row)
            x_parts = (xs,)

        lg = jnp.stack([jax.nn.log_sigmoid(ret_decay_fwd[l].astype(F32)),
                        jax.nn.log_sigmoid(ret_decay_bwd[l].astype(F32))])
        y_ret = _retention(p.reshape(B, lt, N_IN_PAD), lg, ret_gn_g[l].reshape(1, -1), ret_gn_b[l].reshape(1, -1),
                           seq, n_ctx)

        w_ukv = mla_w_ukv[l].reshape(MLA_KV_LORA, MLA_HEADS, MLA_NOPE + MLA_V)
        wk_p = _pad_heads(w_ukv[:, :, :MLA_NOPE].reshape(MLA_KV_LORA, -1), MLA_HEADS, MLA_NOPE).astype(BF16)
        wv = w_ukv[:, :, MLA_NOPE:].reshape(MLA_KV_LORA, MLA_HEADS * MLA_V).T.astype(BF16)
        wq_p = _pad_heads(mla_w_uq[l], MLA_HEADS, MLA_QK).astype(BF16)
        q, k, v = _mla_prep(p, cos_t, sin_t, swap, mla_qa_g[l].reshape(1, -1), mla_kva_g[l].reshape(1, -1),
                            _pad_vec(mla_qn_g[l], 0), _pad_vec(mla_kn_g[l], 0), _pad_vec(mla_kr_g[l], MLA_NOPE),
                            wq_p, wk_p, wv, B, lt)
        o_mla = _attention(q, k, v, seq, n_ctx, lat_tiles_per_b if last else tiles_per_b)

        if last:
            n_tiles = B * lat_tiles_per_b
            tile = lambda t: (t // lat_tiles_per_b) * tiles_per_b + t % lat_tiles_per_b
        else:
            n_tiles = B * tiles_per_b
            tile = lambda t: t
        bs_full = jnp.broadcast_to(gmlp_bs[l][:, :, None], (GMLP_GROUPS, GMLP_CHUNK, GMLP_CHUNK))
        x1, h2 = _merge(x_parts, mod3, p, y_ret.reshape(B * lt, -1), o_mla.reshape(B * lt, -1),
                        gmlp_ln_g[l].reshape(1, -1), gmlp_ln_b[l].reshape(1, -1), gmlp_ws[l].astype(BF16), bs_full,
                        w_br_ret[l].astype(BF16), w_br_mla[l].astype(BF16), w_br_gmlp[l].astype(BF16),
                        w_out[l].astype(BF16), norm2_g[l].reshape(1, D), n_tiles, tile, tiles_per_b, mod_row)

        bias_t = jnp.broadcast_to(moe_bias[l][:, None], (N_EXPERTS, LANES))
        idx, w, rank, counts, hp = _route(h2, moe_router[l].T, bias_t, n_tiles, tile)
        n_act = n_tiles * TM
        n_blocks = -(-(n_act * TOP_K + N_EXPERTS * (MOE_ROWS - 1)) // MOE_ROWS)
        blk_e, valid, pos_sc = _moe_plan(idx, rank, counts, n_blocks)
        xg = _dispatch(pos_sc, hp, n_blocks * MOE_ROWS)
        shared = _shared_expert(hp, sh_w_gate[l].astype(BF16), sh_w_up[l].astype(BF16), sh_w_down[l].astype(BF16))
        ys = _experts(blk_e, valid, xg, moe_w_gate, moe_w_up, moe_w_down, l, n_blocks)
        yg = _gather_rows(pos_sc, ys)
        if last:
            xs = _combine(yg, w, x1, shared, mod3, n_tiles, tile, mod_row, B * seq, lambda t: t)
        else:
            pending = (yg, w, x1, shared, mod3)
    return xs.reshape(B, seq, D)
```

```python
import functools
import math

import jax
import jax.numpy as jnp
from jax import lax
from jax.experimental import pallas as pl
from jax.experimental.pallas import tpu as pltpu
from jax.experimental.pallas import tpu_sc as plsc

F32 = jnp.float32
BF16 = jnp.bfloat16

D = 1024
GRID_W = 64
RET_HEADS = 4
RET_D = 128
RET_CHUNK = 256
RET_OUT_ROWS = 256
MLA_HEADS = 8
MLA_Q_LORA = 384
MLA_KV_LORA = 256
MLA_NOPE = 64
MLA_ROPE = 32
MLA_V = 64
MLA_V_EXT = MLA_V + 16
MLA_QK = MLA_NOPE + MLA_ROPE
HEAD_PAD = 128
ROPE_THETA = 10000.0
GMLP_GROUPS = 4
GMLP_W = 512
GMLP_CHUNK = 128
N_EXPERTS = 64
TOP_K = 6
D_EXPERT = 256
ROUTED_SCALE = 2.5
EPS = 1e-6
LOG2_E = 1.4426950408889634

LANES = 128
SUBLANES = 8
TM = 256
MOE_ROWS = 512
RT = 2 * TM
SHARED_ROWS = 1024
ATT_KV_CHUNK = 1024
ATT_HEADS = 4

C_MERGE = 0
C_UV = 3072
C_RET = 4096
C_CKV = 6144
C_KR = 6400
C_CQ = 6528
N_IN_PAD = 6912
IN_CHUNK = 768
ADA_CHUNK = 1536

VMEM_LIMIT = 56 * 1024 * 1024

SC_CORES = 2
SC_SUBCORES = 16
SC_ROWS = 128


def _cparams(n_axes, vmem=VMEM_LIMIT):
    return pltpu.CompilerParams(dimension_semantics=("arbitrary",) * n_axes, vmem_limit_bytes=vmem)


def _silu(x):
    return x * jax.nn.sigmoid(x)


def _dot(a, b):
    return jnp.dot(a, b, preferred_element_type=F32)


def _dot_nt(a, b):
    return lax.dot_general(a, b, (((1,), (1,)), ((), ())), preferred_element_type=F32)


def _dot_tn(a, b):
    return lax.dot_general(a, b, (((0,), (0,)), ((), ())), preferred_element_type=F32)


def _pack_bf16_pairs(x):
    n = x.shape[1] // 2
    lo = lax.bitcast_convert_type(x[:, :n].astype(BF16).astype(F32), jnp.uint32)
    hi = lax.bitcast_convert_type(x[:, n:].astype(BF16).astype(F32), jnp.uint32)
    return (lo >> 16) | hi


def _unpack_bf16_pairs(u):
    lo = lax.bitcast_convert_type(u << 16, F32)
    hi = lax.bitcast_convert_type(u & jnp.uint32(0xFFFF0000), F32)
    return lo, hi


def _ada_kernel(c_ref, w_ref, b_ref, o_ref):
    s = _silu(c_ref[...])
    o_ref[...] = _dot(s.astype(BF16), w_ref[...].astype(BF16)) + b_ref[...]


def _ada(c_rows, ada_w, ada_b):
    depth = ada_w.shape[0]
    n = ada_w.shape[2]
    cw = ADA_CHUNK
    return pl.pallas_call(
        _ada_kernel,
        grid=(depth, n // cw),
        in_specs=[pl.BlockSpec((SUBLANES, D), lambda l, j: (0, 0)),
                  pl.BlockSpec((None, D, cw), lambda l, j: (l, 0, j)),
                  pl.BlockSpec((None, 1, cw), lambda l, j: (l, 0, j))],
        out_specs=pl.BlockSpec((None, SUBLANES, cw), lambda l, j: (l, 0, j)),
        out_shape=jax.ShapeDtypeStruct((depth, SUBLANES, n), F32),
        compiler_params=_cparams(2),
        name="ada_mod",
    )(c_rows, ada_w, ada_b.reshape(depth, 1, n))


def _modulated_rmsnorm(x, g, shift, scale):
    y = x * lax.rsqrt(jnp.mean(x * x, axis=-1, keepdims=True) + EPS) * g
    return y * (1.0 + scale) + shift


def _stream_specs(x_parts, tile, tiles_per_b):
    if len(x_parts) == 1:
        return [pl.BlockSpec((TM, D), lambda t: (tile(t), 0))]
    lat_tiles = tiles_per_b - 1

    def latent(t):
        s = tile(t)
        return ((s // tiles_per_b) * lat_tiles + jnp.minimum(s % tiles_per_b, lat_tiles - 1), 0)

    return [pl.BlockSpec((TM, D), latent), pl.BlockSpec((TM, D), lambda t: (tile(t) // tiles_per_b, 0))]


def _read_stream(x_refs, tile, tiles_per_b):
    if len(x_refs) == 1:
        return x_refs[0][...]
    is_ctx = tile(pl.program_id(0)) % tiles_per_b == tiles_per_b - 1
    return jnp.where(is_ctx, x_refs[1][...], x_refs[0][...])


def _project_in(x, mod_ref, g_ref, w_ref, o_ref, h_scr):
    h = _modulated_rmsnorm(x, g_ref[...], mod_ref[:, 0:D], mod_ref[:, D:2 * D])
    h_scr[...] = h.astype(BF16)
    for c in range(N_IN_PAD // IN_CHUNK):
        cols = slice(c * IN_CHUNK, (c + 1) * IN_CHUNK)
        o_ref[:, cols] = _dot(h_scr[...], w_ref[:, cols]).astype(BF16)


def _in_proj_kernel(*refs, n_x, tiles_per_b):
    mod_ref, g_ref, w_ref, o_ref, h_scr = refs[n_x:]
    _project_in(_read_stream(refs[:n_x], lambda t: t, tiles_per_b), mod_ref, g_ref, w_ref, o_ref, h_scr)


def _in_proj(x_parts, mod3, g, w_in_r, n_tiles, tiles_per_b, mod_row):
    n_rows = n_tiles * TM
    return pl.pallas_call(
        functools.partial(_in_proj_kernel, n_x=len(x_parts), tiles_per_b=tiles_per_b),
        grid=(n_tiles,),
        in_specs=_stream_specs(x_parts, lambda t: t, tiles_per_b) + [
                  pl.BlockSpec((None, 1, 6 * D), lambda t: (mod_row(t), 0, 0)),
                  pl.BlockSpec((1, D), lambda t: (0, 0)),
                  pl.BlockSpec((D, N_IN_PAD), lambda t: (0, 0), pipeline_mode=pl.Buffered(1))],
        out_specs=pl.BlockSpec((TM, N_IN_PAD), lambda t: (t, 0)),
        out_shape=jax.ShapeDtypeStruct((n_rows, N_IN_PAD), BF16),
        scratch_shapes=[pltpu.VMEM((TM, D), BF16)],
        compiler_params=_cparams(1),
        name="in_proj",
    )(*x_parts, mod3, g, w_in_r)


def _retention_kernel(lg_ref, q_ref, k_ref, v_ref, g_ref, gng_ref, gnb_ref, y_ref, of_scr, ob_scr,
                      *, n_lat_chunks, n_ctx_chunks):
    h = pl.program_id(1)
    lg_f = lg_ref[0, h]
    lg_b = lg_ref[1, h]
    C = RET_CHUNK
    k_scale = RET_D ** -0.5
    ri = lax.broadcasted_iota(jnp.int32, (C, C), 0).astype(F32)
    ci = lax.broadcasted_iota(jnp.int32, (C, C), 1).astype(F32)
    pos = lax.broadcasted_iota(jnp.int32, (C, 1), 0).astype(F32)
    diff = ri - ci
    d_f = jnp.where(diff >= 0, jnp.exp(lg_f * jnp.maximum(diff, 0.0)), 0.0) * k_scale
    d_b = jnp.where(diff < 0, jnp.exp(lg_b * jnp.maximum(-diff, 0.0)), 0.0) * k_scale
    qdec_f = jnp.exp(lg_f * (pos + 1.0))
    kdec_f = jnp.exp(lg_f * (C - 1.0 - pos)) * k_scale
    cdec_f = jnp.exp(lg_f * C)
    qdec_b = jnp.exp(lg_b * (C - pos))
    kdec_b = jnp.exp(lg_b * pos) * k_scale
    cdec_b = jnp.exp(lg_b * C)

    d_both = d_f + d_b

    def chunk(c, state, qdec, kdec, cdec, with_intra):
        rows = pl.ds(pl.multiple_of(c * C, C), C)
        q = q_ref[rows, :]
        k = k_ref[rows, :]
        v = v_ref[rows, :]
        o = _dot((q.astype(F32) * qdec).astype(BF16), state.astype(BF16))
        if with_intra:
            o = o + _dot((_dot_nt(q, k) * d_both).astype(BF16), v)
        kd = (k.astype(F32) * kdec).astype(BF16)
        return rows, o, state * cdec + _dot_tn(kd, v)

    n_all = n_lat_chunks + n_ctx_chunks

    def scan_body(i, states):
        s_f, s_b = states
        c_f = jnp.where(i < n_ctx_chunks, n_lat_chunks + i, i - n_ctx_chunks)
        rows, o, s_f = chunk(c_f, s_f, qdec_f, kdec_f, cdec_f, True)
        of_scr[rows, :] = o
        rows, o, s_b = chunk(n_all - 1 - i, s_b, qdec_b, kdec_b, cdec_b, False)
        ob_scr[rows, :] = o
        return s_f, s_b

    zero = jnp.zeros((RET_D, RET_D), F32)
    lax.fori_loop(0, n_all, scan_body, (zero, zero), unroll=2)

    def out_body(c, _):
        rows = pl.ds(pl.multiple_of(c * RET_OUT_ROWS, RET_OUT_ROWS), RET_OUT_ROWS)
        o = of_scr[rows, :] + ob_scr[rows, :]
        mu = jnp.mean(o, axis=-1, keepdims=True)
        var = jnp.mean(jnp.square(o - mu), axis=-1, keepdims=True)
        on = (o - mu) * lax.rsqrt(var + EPS)
        y = _silu(g_ref[rows, :].astype(F32)) * (on * gng_ref[...] + gnb_ref[...])
        y_ref[rows, :] = y.astype(BF16)
        return 0

    lax.fori_loop(0, n_all * C // RET_OUT_ROWS, out_body, 0, unroll=3)


def _retention(p3, lg, gn_g, gn_b, seq, ctx):
    B, lt, _ = p3.shape
    base = C_RET // RET_D
    kern = functools.partial(_retention_kernel, n_lat_chunks=seq // RET_CHUNK, n_ctx_chunks=ctx // RET_CHUNK)

    def col(off):
        return pl.BlockSpec((None, lt, RET_D), lambda b, h: (b, 0, base + off * RET_HEADS + h))

    return pl.pallas_call(
        kern,
        grid=(B, RET_HEADS),
        in_specs=[pl.BlockSpec(memory_space=pltpu.SMEM),
                  col(0), col(1), col(2), col(3),
                  pl.BlockSpec((1, RET_D), lambda b, h: (0, h)),
                  pl.BlockSpec((1, RET_D), lambda b, h: (0, h))],
        out_specs=pl.BlockSpec((None, lt, RET_D), lambda b, h: (b, 0, h)),
        out_shape=jax.ShapeDtypeStruct((B, lt, RET_HEADS * RET_D), BF16),
        scratch_shapes=[pltpu.VMEM((lt, RET_D), F32), pltpu.VMEM((lt, RET_D), F32)],
        compiler_params=_cparams(2),
        name="retention",
    )(lg, p3, p3, p3, p3, gn_g, gn_b)


def _mla_prep_kernel(cq_ref, ckv_ref, kr_ref, cos_ref, sin_ref, swap_ref, qa_ref, kva_ref, qn_ref, kn_ref, krg_ref,
                     wq_ref, wk_ref, wv_ref, q_ref, k_ref, v_ref):
    cos = cos_ref[...]
    sin = sin_ref[...]

    def rms(x, n):
        return x * lax.rsqrt(jnp.sum(x * x, axis=-1, keepdims=True) * (1.0 / n) + EPS)

    def rope(x):
        return x * cos + _dot(x.astype(BF16), swap_ref[...]) * sin

    cq = cq_ref[...].astype(F32)
    cqn = (rms(cq, MLA_Q_LORA) * qa_ref[...]).astype(BF16)
    q_all = _dot(cqn, wq_ref[...])
    ckv = ckv_ref[...].astype(F32)
    ckvn = (rms(ckv, MLA_KV_LORA) * kva_ref[...]).astype(BF16)
    k_all = _dot(ckvn, wk_ref[...])
    k_rope = rope(rms(kr_ref[...].astype(F32), MLA_ROPE) * krg_ref[...])
    scale = MLA_QK ** -0.5 * LOG2_E
    v_t = _dot_nt(wv_ref[...], ckvn)
    ones_row = jnp.where(lax.broadcasted_iota(jnp.int32, (MLA_V_EXT - MLA_V, TM), 0) == 0, 1.0, 0.0)
    for h in range(MLA_HEADS):
        cols = slice(h * HEAD_PAD, (h + 1) * HEAD_PAD)
        qh = rope(rms(q_all[:, cols], MLA_QK) * qn_ref[...]) * scale
        q_ref[h] = qh.astype(BF16)
        kh = rms(k_all[:, cols], MLA_NOPE) * kn_ref[...] + k_rope
        k_ref[h] = kh.astype(BF16)
        v_ref[h] = jnp.concatenate([v_t[h * MLA_V:(h + 1) * MLA_V, :], ones_row], axis=0).astype(BF16)


def _mla_prep(p, cos_t, sin_t, swap, qa_g, kva_g, qn_p, kn_p, kr_p, wq_p, wk_p, wv, B, lt):
    tiles_per_b = lt // TM
    hw = MLA_HEADS * HEAD_PAD
    const = lambda shape: pl.BlockSpec(shape, lambda b, j: (0,) * len(shape))
    head_out = pl.BlockSpec((None, MLA_HEADS, TM, HEAD_PAD), lambda b, j: (b, 0, j, 0))
    shp = jax.ShapeDtypeStruct((B, MLA_HEADS, lt, HEAD_PAD), BF16)
    v_out = pl.BlockSpec((None, MLA_HEADS, None, MLA_V_EXT, TM), lambda b, j: (b, 0, j, 0, 0))
    v_shp = jax.ShapeDtypeStruct((B, MLA_HEADS, tiles_per_b, MLA_V_EXT, TM), BF16)
    return pl.pallas_call(
        _mla_prep_kernel,
        grid=(B, tiles_per_b),
        in_specs=[pl.BlockSpec((TM, MLA_Q_LORA), lambda b, j: (b * tiles_per_b + j, C_CQ // MLA_Q_LORA)),
                  pl.BlockSpec((TM, MLA_KV_LORA), lambda b, j: (b * tiles_per_b + j, C_CKV // MLA_KV_LORA)),
                  pl.BlockSpec((TM, HEAD_PAD), lambda b, j: (b * tiles_per_b + j, C_KR // HEAD_PAD)),
                  pl.BlockSpec((TM, HEAD_PAD), lambda b, j: (j, 0)),
                  pl.BlockSpec((TM, HEAD_PAD), lambda b, j: (j, 0)),
                  const((HEAD_PAD, HEAD_PAD)),
                  const((1, MLA_Q_LORA)), const((1, MLA_KV_LORA)),
                  const((1, HEAD_PAD)), const((1, HEAD_PAD)), const((1, HEAD_PAD)),
                  const((MLA_Q_LORA, hw)), const((MLA_KV_LORA, hw)), const((MLA_HEADS * MLA_V, MLA_KV_LORA))],
        out_specs=[head_out, head_out, v_out],
        out_shape=[shp, shp, v_shp],
        compiler_params=_cparams(2),
        name="mla_prep",
    )(p, p, p, cos_t, sin_t, swap, qa_g, kva_g, qn_p, kn_p, kr_p, wq_p, wk_p, wv)


def _attention_stages(seq, ctx):
    n_blk = ATT_KV_CHUNK // TM
    total = (seq + ctx) // TM
    first = n_blk + total % n_blk if total >= n_blk else total
    return [(total - first, first)] + [(c * n_blk, n_blk) for c in range((total - first) // n_blk)]


def _attention_kernel(q_ref, k_ref, v_ref, o_ref, s_scr, *, seq, ctx, ctx_tile):
    i = pl.program_id(2)

    def scores(hh, slot, blk, nb):
        s_scr[hh, slot, 0:nb * TM, :] = _dot_nt(k_ref[hh, blk * TM:(blk + nb) * TM, :], q_ref[hh])

    def absorb(hh, slot, blk, nb, carry):
        m, acc = carry
        s = s_scr[hh, slot, 0:nb * TM, :]
        m_new = jnp.maximum(m, jnp.max(s, axis=0, keepdims=True))
        p = jnp.exp2(s - m_new).astype(BF16)
        acc = jnp.exp2(m - m_new) * acc
        for j in range(nb):
            acc = acc + _dot(v_ref[hh, blk + j], p[j * TM:(j + 1) * TM, :])
        return m_new, acc

    def attend(stages):
        for hh in range(ATT_HEADS):
            scores(hh, 0, *stages[0])
        carries = [(jnp.full((1, TM), -jnp.inf, F32), jnp.zeros((MLA_V_EXT, TM), F32))] * ATT_HEADS
        for n, stage in enumerate(stages):
            for hh in range(ATT_HEADS):
                if n + 1 < len(stages):
                    scores(hh, (n + 1) % 2, *stages[n + 1])
                carries[hh] = absorb(hh, n % 2, *stage, carries[hh])
        outs = [acc[0:MLA_V, :] / acc[MLA_V:MLA_V + 1, :] for _, acc in carries]
        o_ref[...] = jnp.concatenate(outs, axis=0).T.astype(BF16)

    @pl.when(i != ctx_tile)
    def _():
        attend(_attention_stages(seq, ctx))

    @pl.when(i == ctx_tile)
    def _():
        attend([(seq // TM, ctx // TM)])


def _attention(q, k, v, seq, ctx, n_q_tiles):
    B, H, lt, _ = q.shape
    kern = functools.partial(_attention_kernel, seq=seq, ctx=ctx, ctx_tile=seq // TM)
    slot_rows = max(nb for _, nb in _attention_stages(seq, ctx)) * TM
    return pl.pallas_call(
        kern,
        grid=(B, H // ATT_HEADS, n_q_tiles),
        in_specs=[pl.BlockSpec((None, ATT_HEADS, TM, HEAD_PAD), lambda b, h, i: (b, h, i, 0)),
                  pl.BlockSpec((None, ATT_HEADS, lt, HEAD_PAD), lambda b, h, i: (b, h, 0, 0)),
                  pl.BlockSpec((None, ATT_HEADS, lt // TM, MLA_V_EXT, TM), lambda b, h, i: (b, h, 0, 0, 0))],
        out_specs=pl.BlockSpec((None, TM, ATT_HEADS * MLA_V), lambda b, h, i: (b, i, h)),
        out_shape=jax.ShapeDtypeStruct((B, lt, H * MLA_V), BF16),
        scratch_shapes=[pltpu.VMEM((ATT_HEADS, 2, slot_rows, TM), F32)],
        compiler_params=_cparams(3),
        name="attention",
    )(q, k, v)


def _merge_kernel(*refs, n_x, tile, tiles_per_b):
    (mod_ref, mg_ref, uv_ref, yr_ref, om_ref, lng_ref, lnb_ref, ws_ref, bs_ref,
     wr_ref, wm_ref, wg_ref, wo_ref, x1_ref) = refs[n_x:]
    yr = _dot(yr_ref[...], wr_ref[...])
    ym = _dot(om_ref[...], wm_ref[...])
    z = jax.nn.gelu(uv_ref[...].astype(F32))
    u = z[:, :GMLP_W]
    v = z[:, GMLP_W:]
    mu = jnp.mean(v, axis=-1, keepdims=True)
    var = jnp.mean(jnp.square(v - mu), axis=-1, keepdims=True)
    vn = ((v - mu) * lax.rsqrt(var + EPS) * lng_ref[...] + lnb_ref[...]).astype(BF16)
    gw = GMLP_W // GMLP_GROUPS
    chunks = []
    for c in range(TM // GMLP_CHUNK):
        rows = slice(c * GMLP_CHUNK, (c + 1) * GMLP_CHUNK)
        groups = [_dot(ws_ref[g], vn[rows, g * gw:(g + 1) * gw]) + bs_ref[g] for g in range(GMLP_GROUPS)]
        chunks.append(jnp.concatenate(groups, axis=1))
    sv = jnp.concatenate(chunks, axis=0)
    yg = _dot((u * sv).astype(BF16), wg_ref[...])
    gate = jax.nn.sigmoid(mg_ref[...].astype(F32))
    y = gate[:, :D] * yr + gate[:, D:2 * D] * ym + gate[:, 2 * D:] * yg
    out = _dot(y.astype(BF16), wo_ref[...])
    x1_ref[...] = _read_stream(refs[:n_x], tile, tiles_per_b) + mod_ref[:, 2 * D:3 * D] * out


def _merge(x_parts, mod3, p, y_ret, o_mla, ln_g, ln_b, ws, bs_full, w_br_ret, w_br_mla, w_br_gmlp, w_out,
           n_tiles, tile, tiles_per_b, mod_row):
    n_rows = p.shape[0]
    const = lambda shape: pl.BlockSpec(shape, lambda t: (0,) * len(shape))
    row = lambda w, cb=0: pl.BlockSpec((TM, w), lambda t: (tile(t), cb))
    return pl.pallas_call(
        functools.partial(_merge_kernel, n_x=len(x_parts), tile=tile, tiles_per_b=tiles_per_b),
        grid=(n_tiles,),
        in_specs=_stream_specs(x_parts, tile, tiles_per_b) + [
                  pl.BlockSpec((None, 1, 6 * D), lambda t: (mod_row(tile(t)), 0, 0)),
                  row(3 * D, C_MERGE // (3 * D)), row(D, C_UV // D),
                  row(RET_HEADS * RET_D), row(MLA_HEADS * MLA_V),
                  const((1, GMLP_W)), const((1, GMLP_W)),
                  const((GMLP_GROUPS, GMLP_CHUNK, GMLP_CHUNK)), const((GMLP_GROUPS, GMLP_CHUNK, GMLP_CHUNK)),
                  const((RET_HEADS * RET_D, D)), const((MLA_HEADS * MLA_V, D)), const((GMLP_W, D)),
                  const((D, D))],
        out_specs=row(D),
        out_shape=jax.ShapeDtypeStruct((n_rows, D), F32),
        compiler_params=_cparams(1),
        name="merge",
    )(*x_parts, mod3, p, p, y_ret, o_mla, ln_g, ln_b, ws, bs_full, w_br_ret, w_br_mla, w_br_gmlp, w_out)


def _route_kernel(xa_ref, xb_ref, moda_ref, modb_ref, n2_ref, rt_ref, bt_ref, idx_ref, w_ref, rank_ref, cnt_ref,
                  hp_ref, cnt_scr):
    @pl.when(pl.program_id(0) == 0)
    def _():
        cnt_scr[...] = jnp.zeros_like(cnt_scr)

    h = jnp.concatenate([_modulated_rmsnorm(x_ref[...], n2_ref[...], m_ref[:, 3 * D:4 * D], m_ref[:, 4 * D:5 * D])
                         for x_ref, m_ref in ((xa_ref, moda_ref), (xb_ref, modb_ref))], axis=0)
    logits = lax.dot_general(rt_ref[...], h, (((1,), (1,)), ((), ())), preferred_element_type=F32,
                             precision=lax.Precision.HIGHEST)
    scores = jax.nn.sigmoid(logits)
    sel = scores + bt_ref[:, 0:1]
    row_e = lax.broadcasted_iota(jnp.int32, (N_EXPERTS, RT), 0).astype(F32)
    row_o = lax.broadcasted_iota(jnp.int32, (SUBLANES, RT), 0)
    idx_out = jnp.zeros((SUBLANES, RT), F32)
    w_out = jnp.zeros((SUBLANES, RT), F32)
    hits = []
    for k in range(TOP_K):
        best = jnp.max(sel, axis=0, keepdims=True)
        pick = jnp.min(jnp.where(sel == best, row_e, float(N_EXPERTS)), axis=0, keepdims=True)
        hit = row_e == pick
        hits.append(hit)
        wk = jnp.sum(jnp.where(hit, scores, 0.0), axis=0, keepdims=True)
        sel = jnp.where(hit, -jnp.inf, sel)
        idx_out = jnp.where(row_o == k, pick, idx_out)
        w_out = jnp.where(row_o == k, wk, w_out)
    w_out = w_out / jnp.sum(w_out, axis=0, keepdims=True) * ROUTED_SCALE
    idx_ref[...] = idx_out.astype(jnp.int32)
    w_ref[...] = w_out
    chosen = jnp.zeros((N_EXPERTS, RT), F32)
    for hit in hits:
        chosen = jnp.where(hit, 1.0, chosen)
    earlier = (lax.broadcasted_iota(jnp.int32, (RT, RT), 0) < lax.broadcasted_iota(jnp.int32, (RT, RT), 1))
    before = _dot(chosen.astype(BF16), jnp.where(earlier, 1.0, 0.0).astype(BF16)) + cnt_scr[:, 0:1]
    rank_out = jnp.zeros((SUBLANES, RT), F32)
    for k, hit in enumerate(hits):
        rank_out = jnp.where(row_o == k, jnp.sum(jnp.where(hit, before, 0.0), axis=0, keepdims=True), rank_out)
    rank_ref[...] = rank_out.astype(jnp.int32)
    cnt_scr[...] += jnp.sum(chosen, axis=1, keepdims=True)
    cnt_ref[...] = cnt_scr[...]
    hp_ref[...] = _pack_bf16_pairs(h)


def _route(x1, mod3, n2_g, router_t, bias_t, n_tiles, tile, mod_row):
    const = lambda shape: pl.BlockSpec(shape, lambda t: (0,) * len(shape))
    n_act = n_tiles * TM
    assert n_tiles % 2 == 0
    n_steps = n_tiles // 2
    per_tok = pl.BlockSpec((None, SUBLANES, RT), lambda t: (t, 0, 0))
    idx_t, w_t, rank_t, counts, hp = pl.pallas_call(
        _route_kernel,
        grid=(n_steps,),
        in_specs=[pl.BlockSpec((TM, D), lambda t: (tile(2 * t), 0)),
                  pl.BlockSpec((TM, D), lambda t: (tile(2 * t + 1), 0)),
                  pl.BlockSpec((None, 1, 6 * D), lambda t: (mod_row(tile(2 * t)), 0, 0)),
                  pl.BlockSpec((None, 1, 6 * D), lambda t: (mod_row(tile(2 * t + 1)), 0, 0)),
                  const((1, D)), const((N_EXPERTS, D)), const((N_EXPERTS, LANES))],
        out_specs=[per_tok, per_tok, per_tok,
                   pl.BlockSpec((N_EXPERTS, LANES), lambda t: (0, 0)),
                   pl.BlockSpec((RT, D // 2), lambda t: (t, 0))],
        out_shape=[jax.ShapeDtypeStruct((n_steps, SUBLANES, RT), jnp.int32),
                   jax.ShapeDtypeStruct((n_steps, SUBLANES, RT), F32),
                   jax.ShapeDtypeStruct((n_steps, SUBLANES, RT), jnp.int32),
                   jax.ShapeDtypeStruct((N_EXPERTS, LANES), F32),
                   jax.ShapeDtypeStruct((n_act, D // 2), jnp.uint32)],
        scratch_shapes=[pltpu.VMEM((N_EXPERTS, LANES), F32)],
        compiler_params=_cparams(1),
        name="route",
    )(x1, x1, mod3, mod3, n2_g, router_t, bias_t)
    token_major = lambda a: a.transpose(0, 2, 1).reshape(n_act, SUBLANES)
    return (token_major(idx_t)[:, :TOP_K], token_major(w_t), token_major(rank_t)[:, :TOP_K], counts[:, 0], hp)


def _shared_expert_kernel(h_ref, sg_ref, su_ref, sd_ref, o_ref):
    lo, hi = _unpack_bf16_pairs(h_ref[...])
    hb = jnp.concatenate([lo, hi], axis=1).astype(BF16)
    a = _silu(_dot(hb, sg_ref[...])) * _dot(hb, su_ref[...])
    o_ref[...] = _dot(a.astype(BF16), sd_ref[...]).astype(BF16)


def _shared_expert(hp, sg, su, sd):
    const = lambda shape: pl.BlockSpec(shape, lambda t: (0,) * len(shape))
    n_act = hp.shape[0]
    rows = math.gcd(n_act, SHARED_ROWS)
    return pl.pallas_call(
        _shared_expert_kernel,
        grid=(n_act // rows,),
        in_specs=[pl.BlockSpec((rows, D // 2), lambda t: (t, 0)),
                  const((D, D_EXPERT)), const((D, D_EXPERT)), const((D_EXPERT, D))],
        out_specs=pl.BlockSpec((rows, D), lambda t: (t, 0)),
        out_shape=jax.ShapeDtypeStruct((n_act, D), BF16),
        compiler_params=_cparams(1),
        name="shared_expert",
    )(hp, sg, su, sd)


def _dispatch(pos_sc, hp, n_rows):
    n_batches = pos_sc.shape[0]
    n_workers = SC_CORES * SC_SUBCORES
    mesh = plsc.VectorSubcoreMesh(core_axis_name="c", subcore_axis_name="s")

    @functools.partial(
        pl.kernel, mesh=mesh,
        out_type=jax.ShapeDtypeStruct((n_rows, D // 2), jnp.uint32),
        scratch_types=[pltpu.VMEM((TOP_K, SC_ROWS), jnp.int32),
                       pltpu.VMEM((SC_ROWS, D // 2), jnp.uint32),
                       pltpu.SemaphoreType.DMA],
        name="moe_dispatch")
    def scatter(pos_hbm, h_hbm, xs_hbm, idx_v, rows_v, sem):
        worker = lax.axis_index("s") * SC_CORES + lax.axis_index("c")

        @pl.loop(0, pl.cdiv(n_batches, n_workers))
        def _(j):
            b = j * n_workers + worker

            @pl.when(b < n_batches)
            def _():
                pltpu.sync_copy(pos_hbm.at[b], idx_v)
                pltpu.sync_copy(h_hbm.at[pl.ds(b * SC_ROWS, SC_ROWS)], rows_v)
                copies = [pltpu.async_copy(rows_v, xs_hbm.at[idx_v.at[k]], sem) for k in range(TOP_K)]
                for cp in copies:
                    cp.wait()

    return scatter(pos_sc, hp)


def _expert_kernel(blk_e_ref, valid_ref, x_ref, wg_ref, wu_ref, wd_ref, y_ref, wg_s, wu_s, wd_s):
    i = pl.program_id(0)
    n_valid = valid_ref[i]

    @pl.when(n_valid > 0)
    def _():
        @pl.when(jnp.logical_or(i == 0, blk_e_ref[i] != blk_e_ref[jnp.maximum(i - 1, 0)]))
        def _():
            wg_s[...] = wg_ref[...].astype(BF16)
            wu_s[...] = wu_ref[...].astype(BF16)
            wd_s[...] = wd_ref[...].astype(BF16)

        row = lax.broadcasted_iota(jnp.int32, (MOE_ROWS, 1), 0)
        lo, hi = _unpack_bf16_pairs(jnp.where(row < n_valid, x_ref[...], jnp.uint32(0)))
        x = jnp.concatenate([lo, hi], axis=1).astype(BF16)
        hb = _silu(_dot(x, wg_s[...])) * _dot(x, wu_s[...])
        y_ref[...] = _pack_bf16_pairs(_dot(hb.astype(BF16), wd_s[...]))

    @pl.when(n_valid == 0)
    def _():
        y_ref[...] = jnp.zeros_like(y_ref)


def _experts(blk_e, valid, xs, wg, wu, wd, layer, n_blocks):
    grid_spec = pltpu.PrefetchScalarGridSpec(
        num_scalar_prefetch=2,
        grid=(n_blocks,),
        in_specs=[pl.BlockSpec((MOE_ROWS, D // 2), lambda i, be, nv: (i, 0)),
                  pl.BlockSpec((None, None, D, D_EXPERT), lambda i, be, nv: (layer, be[i], 0, 0)),
                  pl.BlockSpec((None, None, D, D_EXPERT), lambda i, be, nv: (layer, be[i], 0, 0)),
                  pl.BlockSpec((None, None, D_EXPERT, D), lambda i, be, nv: (layer, be[i], 0, 0))],
        out_specs=pl.BlockSpec((MOE_ROWS, D // 2), lambda i, be, nv: (i, 0)),
        scratch_shapes=[pltpu.VMEM((D, D_EXPERT), BF16), pltpu.VMEM((D, D_EXPERT), BF16),
                        pltpu.VMEM((D_EXPERT, D), BF16)],
    )
    return pl.pallas_call(
        _expert_kernel,
        grid_spec=grid_spec,
        out_shape=jax.ShapeDtypeStruct((n_blocks * MOE_ROWS, D // 2), jnp.uint32),
        compiler_params=_cparams(1),
        name="routed_experts",
    )(blk_e, valid, xs, wg, wu, wd)


def _gather_rows(pos_sc, ys):
    n_batches = pos_sc.shape[0]
    n_workers = SC_CORES * SC_SUBCORES
    half = SC_ROWS // 2
    mesh = plsc.VectorSubcoreMesh(core_axis_name="c", subcore_axis_name="s")

    @functools.partial(
        pl.kernel, mesh=mesh,
        out_type=jax.ShapeDtypeStruct((TOP_K, n_batches * SC_ROWS, D // 2), jnp.uint32),
        scratch_types=[pltpu.VMEM((TOP_K, SC_ROWS), jnp.int32),
                       pltpu.VMEM((2, half, D // 2), jnp.uint32),
                       pltpu.SemaphoreType.DMA, pltpu.SemaphoreType.DMA],
        name="moe_gather")
    def gather(pos_hbm, y_hbm, out_hbm, idx_v, bufs, gsem, wsem):
        worker = lax.axis_index("s") * SC_CORES + lax.axis_index("c")

        @pl.loop(0, pl.cdiv(n_batches, n_workers))
        def _(j):
            b = j * n_workers + worker

            @pl.when(b < n_batches)
            def _():
                pltpu.sync_copy(pos_hbm.at[b], idx_v)
                items = [(k, h) for k in range(TOP_K) for h in range(2)]

                def fetch(i):
                    k, h = items[i]
                    return pltpu.async_copy(y_hbm.at[idx_v.at[k, pl.ds(h * half, half)]], bufs.at[i % 2], gsem)

                pending_gather = fetch(0)
                pending_write = None
                for i, (k, h) in enumerate(items):
                    pending_gather.wait()
                    if pending_write is not None:
                        pending_write.wait()
                    if i + 1 < len(items):
                        pending_gather = fetch(i + 1)
                    pending_write = pltpu.async_copy(
                        bufs.at[i % 2], out_hbm.at[k, pl.ds(b * SC_ROWS + h * half, half)], wsem)
                pending_write.wait()

    return gather(pos_sc, ys)


def _combine_tile(y_refs, w_ref, x1_ref, sh_ref, mod_ref):
    f_lo = sh_ref[:, :D // 2].astype(F32)
    f_hi = sh_ref[:, D // 2:].astype(F32)
    for k in range(TOP_K):
        lo, hi = _unpack_bf16_pairs(y_refs[k][...])
        f_lo = f_lo + lo * w_ref[:, k:k + 1]
        f_hi = f_hi + hi * w_ref[:, k:k + 1]
    return x1_ref[...] + mod_ref[:, 5 * D:6 * D] * jnp.concatenate([f_lo, f_hi], axis=1)


def _combine_kernel(*refs):
    o_ref = refs[-1]
    o_ref[...] = _combine_tile(refs[:TOP_K], *refs[TOP_K:-1])


def _combine_specs(tile, mod_row):
    planes = [pl.BlockSpec((None, TM, D // 2), lambda t, k=k: (k, t, 0)) for k in range(TOP_K)]
    return planes + [pl.BlockSpec((TM, SUBLANES), lambda t: (t, 0)),
                     pl.BlockSpec((TM, D), lambda t: (tile(t), 0)),
                     pl.BlockSpec((TM, D), lambda t: (t, 0)),
                     pl.BlockSpec((None, 1, 6 * D), lambda t: (mod_row(tile(t)), 0, 0))]


def _combine(yg, w, x1, shared, mod3, n_tiles, tile, mod_row, out_rows, out_tile):
    return pl.pallas_call(
        _combine_kernel,
        grid=(n_tiles,),
        in_specs=_combine_specs(tile, mod_row),
        out_specs=pl.BlockSpec((TM, D), lambda t: (out_tile(t), 0)),
        out_shape=jax.ShapeDtypeStruct((out_rows, D), F32),
        compiler_params=_cparams(1),
        name="moe_combine",
    )(*([yg] * TOP_K), w, x1, shared, mod3)


def _combine_in_proj_kernel(*refs):
    xs_ref, o_ref, h_scr = refs[-3:]
    mod_ref, g_ref, w_ref = refs[TOP_K + 4:-3]
    x = _combine_tile(refs[:TOP_K], *refs[TOP_K:TOP_K + 4])
    xs_ref[...] = x
    _project_in(x, mod_ref, g_ref, w_ref, o_ref, h_scr)


def _combine_in_proj(yg, w, x1, shared, mod3_prev, mod3, g, w_in_r, n_tiles, mod_row):
    n_rows = x1.shape[0]
    ident = lambda t: t
    return pl.pallas_call(
        _combine_in_proj_kernel,
        grid=(n_tiles,),
        in_specs=_combine_specs(ident, mod_row) + [
            pl.BlockSpec((None, 1, 6 * D), lambda t: (mod_row(t), 0, 0)),
            pl.BlockSpec((1, D), lambda t: (0, 0)),
            pl.BlockSpec((D, N_IN_PAD), lambda t: (0, 0), pipeline_mode=pl.Buffered(1))],
        out_specs=[pl.BlockSpec((TM, D), lambda t: (t, 0)),
                   pl.BlockSpec((TM, N_IN_PAD), lambda t: (t, 0))],
        out_shape=[jax.ShapeDtypeStruct((n_rows, D), F32),
                   jax.ShapeDtypeStruct((n_rows, N_IN_PAD), BF16)],
        scratch_shapes=[pltpu.VMEM((TM, D), BF16)],
        compiler_params=_cparams(1),
        name="combine_in_proj",
    )(*([yg] * TOP_K), w, x1, shared, mod3_prev, mod3, g, w_in_r)


def _moe_plan(idx, rank, counts, n_blocks):
    n = idx.shape[0]
    cnt = counts.reshape(N_EXPERTS).astype(jnp.int32)
    padded = (cnt + MOE_ROWS - 1) // MOE_ROWS * MOE_ROWS
    pad_end = jnp.cumsum(padded)
    pad_start = pad_end - padded
    experts = jnp.arange(N_EXPERTS, dtype=jnp.int32)
    pos = rank + jnp.sum(jnp.where(idx[:, :, None] == experts, pad_start, 0), axis=-1)
    blk_start = jnp.arange(n_blocks, dtype=jnp.int32) * MOE_ROWS
    blk_e = jnp.minimum(jnp.sum(blk_start[:, None] >= pad_end[None, :], axis=1), N_EXPERTS - 1).astype(jnp.int32)
    mine = blk_e[:, None] == experts
    in_expert = blk_start - jnp.sum(jnp.where(mine, pad_start, 0), axis=1)
    valid = jnp.clip(jnp.sum(jnp.where(mine, cnt, 0), axis=1) - in_expert, 0, MOE_ROWS).astype(jnp.int32)
    pos_sc = pos.astype(jnp.int32).reshape(n // SC_ROWS, SC_ROWS, TOP_K).transpose(0, 2, 1)
    return blk_e, valid, pos_sc


def _rope_tables(seq, ctx):
    half = MLA_ROPE // 2
    n_freq = half // 2
    inv = ROPE_THETA ** (-2.0 * jnp.arange(n_freq, dtype=F32) / half)
    t = jnp.arange(seq)
    ang_r = (t // GRID_W).astype(F32)[:, None] * inv
    ang_c = (t % GRID_W).astype(F32)[:, None] * inv
    cos = jnp.concatenate([jnp.cos(ang_r), jnp.cos(ang_r), jnp.cos(ang_c), jnp.cos(ang_c)], axis=1)
    sin = jnp.concatenate([-jnp.sin(ang_r), jnp.sin(ang_r), -jnp.sin(ang_c), jnp.sin(ang_c)], axis=1)
    pad_l = MLA_NOPE
    pad_r = HEAD_PAD - MLA_NOPE - MLA_ROPE
    cos = jnp.pad(cos, ((0, ctx), (pad_l, pad_r)), constant_values=1.0)
    cos = cos.at[seq:, :].set(1.0)
    sin = jnp.pad(sin, ((0, ctx), (pad_l, pad_r)))
    lane = jnp.arange(HEAD_PAD)
    partner = jnp.where(lane % 16 < 8, lane + 8, lane - 8)
    swap = (lane[:, None] == partner[None, :]).astype(BF16)
    return cos, sin, swap


def _pad_heads(w, n_heads, width, offset=0):
    k = w.shape[0]
    w = w.reshape(k, n_heads, width)
    w = jnp.pad(w, ((0, 0), (0, 0), (offset, HEAD_PAD - width - offset)))
    return w.reshape(k, n_heads * HEAD_PAD)


def _pad_vec(g, offset):
    return jnp.pad(g, (offset, HEAD_PAD - g.shape[0] - offset)).reshape(1, HEAD_PAD)


def _reorder_w_in(w):
    off_cq, off_ckv, off_kr, off_uv, off_merge = 2048, 2432, 2688, 2720, 3744
    kr = jnp.pad(w[:, off_kr:off_uv], ((0, 0), (MLA_NOPE, HEAD_PAD - MLA_NOPE - MLA_ROPE)))
    return jnp.concatenate([w[:, off_merge:], w[:, off_uv:off_merge], w[:, :off_cq],
                            w[:, off_ckv:off_kr], kr, w[:, off_cq:off_ckv]], axis=1).astype(BF16)


def kernel(x, c, ctx, c_ctx, ada_w, ada_b, norm1_g, norm2_g, w_in, ret_decay_fwd, ret_decay_bwd, ret_gn_g,
           ret_gn_b, w_br_ret, mla_qa_g, mla_w_uq, mla_kva_g, mla_w_ukv, mla_qn_g, mla_kn_g, mla_kr_g, w_br_mla,
           gmlp_ln_g, gmlp_ln_b, gmlp_ws, gmlp_bs, w_br_gmlp, w_out, moe_router, moe_bias, moe_w_gate, moe_w_up,
           moe_w_down, sh_w_gate, sh_w_up, sh_w_down):
    B, seq, _ = x.shape
    n_ctx = ctx.shape[1]
    depth = ada_w.shape[0]
    assert n_ctx == TM and seq % TM == 0 and B + 1 <= SUBLANES and TOP_K <= SUBLANES
    lt = seq + n_ctx
    tiles_per_b = lt // TM
    lat_tiles_per_b = seq // TM
    ctx_tile = lat_tiles_per_b

    def mod_row(t):
        return jnp.where(t % tiles_per_b == ctx_tile, B, t // tiles_per_b)

    c_rows = jnp.concatenate([c, c_ctx[None, :], jnp.zeros((SUBLANES - B - 1, D), F32)], axis=0)
    mod = _ada(c_rows, ada_w, ada_b)
    cos_t, sin_t, swap = _rope_tables(seq, n_ctx)
    x_parts = (x.reshape(B * seq, D), ctx.reshape(B * n_ctx, D))

    pending = None
    for l in range(depth):
        last = l == depth - 1
        mod3 = mod[l].reshape(SUBLANES, 1, 6 * D)
        if pending is None:
            p = _in_proj(x_parts, mod3, norm1_g[l].reshape(1, D), _reorder_w_in(w_in[l]), B * tiles_per_b,
                         tiles_per_b, mod_row)
        else:
            xs, p = _combine_in_proj(*pending, mod3, norm1_g[l].reshape(1, D), _reorder_w_in(w_in[l]),
                                     B * tiles_per_b, mod_row)
            x_parts = (xs,)

        lg = jnp.stack([jax.nn.log_sigmoid(ret_decay_fwd[l].astype(F32)),
                        jax.nn.log_sigmoid(ret_decay_bwd[l].astype(F32))])
        y_ret = _retention(p.reshape(B, lt, N_IN_PAD), lg, ret_gn_g[l].reshape(1, -1), ret_gn_b[l].reshape(1, -1),
                           seq, n_ctx)

        w_ukv = mla_w_ukv[l].reshape(MLA_KV_LORA, MLA_HEADS, MLA_NOPE + MLA_V)
        wk_p = _pad_heads(w_ukv[:, :, :MLA_NOPE].reshape(MLA_KV_LORA, -1), MLA_HEADS, MLA_NOPE).astype(BF16)
        wv = w_ukv[:, :, MLA_NOPE:].reshape(MLA_KV_LORA, MLA_HEADS * MLA_V).T.astype(BF16)
        wq_p = _pad_heads(mla_w_uq[l], MLA_HEADS, MLA_QK).astype(BF16)
        q, k, v = _mla_prep(p, cos_t, sin_t, swap, mla_qa_g[l].reshape(1, -1), mla_kva_g[l].reshape(1, -1),
                            _pad_vec(mla_qn_g[l], 0), _pad_vec(mla_kn_g[l], 0), _pad_vec(mla_kr_g[l], MLA_NOPE),
                            wq_p, wk_p, wv, B, lt)
        o_mla = _attention(q, k, v, seq, n_ctx, lat_tiles_per_b if last else tiles_per_b)

        if last:
            n_tiles = B * lat_tiles_per_b
            tile = lambda t: (t // lat_tiles_per_b) * tiles_per_b + t % lat_tiles_per_b
        else:
            n_tiles = B * tiles_per_b
            tile = lambda t: t
        bs_full = jnp.broadcast_to(gmlp_bs[l][:, :, None], (GMLP_GROUPS, GMLP_CHUNK, GMLP_CHUNK))
        x1 = _merge(x_parts, mod3, p, y_ret.reshape(B * lt, -1), o_mla.reshape(B * lt, -1),
                    gmlp_ln_g[l].reshape(1, -1), gmlp_ln_b[l].reshape(1, -1), gmlp_ws[l].astype(BF16), bs_full,
                    w_br_ret[l].astype(BF16), w_br_mla[l].astype(BF16), w_br_gmlp[l].astype(BF16),
                    w_out[l].astype(BF16), n_tiles, tile, tiles_per_b, mod_row)

        bias_t = jnp.broadcast_to(moe_bias[l][:, None], (N_EXPERTS, LANES))
        idx, w, rank, counts, hp = _route(x1, mod3, norm2_g[l].reshape(1, D), moe_router[l].T, bias_t,
                                          n_tiles, tile, mod_row)
        n_act = n_tiles * TM
        n_blocks = -(-(n_act * TOP_K + N_EXPERTS * (MOE_ROWS - 1)) // MOE_ROWS)
        blk_e, valid, pos_sc = _moe_plan(idx, rank, counts, n_blocks)
        xg = _dispatch(pos_sc, hp, n_blocks * MOE_ROWS)
        shared = _shared_expert(hp, sh_w_gate[l].astype(BF16), sh_w_up[l].astype(BF16), sh_w_down[l].astype(BF16))
        ys = _experts(blk_e, valid, xg, moe_w_gate, moe_w_up, moe_w_down, l, n_blocks)
        yg = _gather_rows(pos_sc, ys)
        if last:
            xs = _combine(yg, w, x1, shared, mod3, n_tiles, tile, mod_row, B * seq, lambda t: t)
        else:
            pending = (yg, w, x1, shared, mod3)
    return xs.reshape(B, seq, D)
```

```python
import functools
import math

import jax
import jax.numpy as jnp
from jax import lax
from jax.experimental import pallas as pl
from jax.experimental.pallas import tpu as pltpu
from jax.experimental.pallas import tpu_sc as plsc

F32 = jnp.float32
BF16 = jnp.bfloat16

D = 1024
GRID_W = 64
RET_HEADS = 4
RET_D = 128
RET_CHUNK = 256
RET_OUT_ROWS = 256
MLA_HEADS = 8
MLA_Q_LORA = 384
MLA_KV_LORA = 256
MLA_NOPE = 64
MLA_ROPE = 32
MLA_V = 64
MLA_V_EXT = MLA_V + 16
MLA_QK = MLA_NOPE + MLA_ROPE
HEAD_PAD = 128
ROPE_THETA = 10000.0
GMLP_GROUPS = 4
GMLP_W = 512
GMLP_CHUNK = 128
N_EXPERTS = 64
TOP_K = 6
D_EXPERT = 256
ROUTED_SCALE = 2.5
EPS = 1e-6
LOG2_E = 1.4426950408889634

LANES = 128
SUBLANES = 8
TM = 256
MOE_ROWS = 512
RT = 2 * TM
SHARED_ROWS = 1024
ATT_KV_CHUNK = 1024
ATT_HEADS = 4

C_MERGE = 0
C_UV = 3072
C_RET = 4096
C_CKV = 6144
C_KR = 6400
C_CQ = 6528
N_IN_PAD = 6912
IN_CHUNK = 768
ADA_CHUNK = 1536

VMEM_LIMIT = 56 * 1024 * 1024

SC_CORES = 2
SC_SUBCORES = 16
SC_ROWS = 128


def _cparams(n_axes, vmem=VMEM_LIMIT):
    return pltpu.CompilerParams(dimension_semantics=("arbitrary",) * n_axes, vmem_limit_bytes=vmem)


def _silu(x):
    return x * jax.nn.sigmoid(x)


def _dot(a, b):
    return jnp.dot(a, b, preferred_element_type=F32)


def _dot_nt(a, b):
    return lax.dot_general(a, b, (((1,), (1,)), ((), ())), preferred_element_type=F32)


def _dot_tn(a, b):
    return lax.dot_general(a, b, (((0,), (0,)), ((), ())), preferred_element_type=F32)


def _pack_bf16_pairs(x):
    n = x.shape[1] // 2
    lo = lax.bitcast_convert_type(x[:, :n].astype(BF16).astype(F32), jnp.uint32)
    hi = lax.bitcast_convert_type(x[:, n:].astype(BF16).astype(F32), jnp.uint32)
    return (lo >> 16) | hi


def _unpack_bf16_pairs(u):
    lo = lax.bitcast_convert_type(u << 16, F32)
    hi = lax.bitcast_convert_type(u & jnp.uint32(0xFFFF0000), F32)
    return lo, hi


def _ada_kernel(c_ref, w_ref, b_ref, o_ref):
    s = _silu(c_ref[...])
    o_ref[...] = _dot(s.astype(BF16), w_ref[...].astype(BF16)) + b_ref[...]


def _ada(c_rows, ada_w, ada_b):
    depth = ada_w.shape[0]
    n = ada_w.shape[2]
    cw = ADA_CHUNK
    return pl.pallas_call(
        _ada_kernel,
        grid=(depth, n // cw),
        in_specs=[pl.BlockSpec((SUBLANES, D), lambda l, j: (0, 0)),
                  pl.BlockSpec((None, D, cw), lambda l, j: (l, 0, j)),
                  pl.BlockSpec((None, 1, cw), lambda l, j: (l, 0, j))],
        out_specs=pl.BlockSpec((None, SUBLANES, cw), lambda l, j: (l, 0, j)),
        out_shape=jax.ShapeDtypeStruct((depth, SUBLANES, n), F32),
        compiler_params=_cparams(2),
        name="ada_mod",
    )(c_rows, ada_w, ada_b.reshape(depth, 1, n))


def _modulated_rmsnorm(x, g, shift, scale):
    y = x * lax.rsqrt(jnp.mean(x * x, axis=-1, keepdims=True) + EPS) * g
    return y * (1.0 + scale) + shift


def _stream_specs(x_parts, tile, tiles_per_b):
    if len(x_parts) == 1:
        return [pl.BlockSpec((TM, D), lambda t: (tile(t), 0))]
    lat_tiles = tiles_per_b - 1

    def latent(t):
        s = tile(t)
        return ((s // tiles_per_b) * lat_tiles + jnp.minimum(s % tiles_per_b, lat_tiles - 1), 0)

    return [pl.BlockSpec((TM, D), latent), pl.BlockSpec((TM, D), lambda t: (tile(t) // tiles_per_b, 0))]


def _read_stream(x_refs, tile, tiles_per_b):
    if len(x_refs) == 1:
        return x_refs[0][...]
    is_ctx = tile(pl.program_id(0)) % tiles_per_b == tiles_per_b - 1
    return jnp.where(is_ctx, x_refs[1][...], x_refs[0][...])


def _project_in(x, mod_ref, g_ref, w_ref, o_ref, h_scr):
    h = _modulated_rmsnorm(x, g_ref[...], mod_ref[:, 0:D], mod_ref[:, D:2 * D])
    h_scr[...] = h.astype(BF16)
    for c in range(N_IN_PAD // IN_CHUNK):
        cols = slice(c * IN_CHUNK, (c + 1) * IN_CHUNK)
        o_ref[:, cols] = _dot(h_scr[...], w_ref[:, cols]).astype(BF16)


def _in_proj_kernel(*refs, n_x, tiles_per_b):
    mod_ref, g_ref, w_ref, o_ref, h_scr = refs[n_x:]
    _project_in(_read_stream(refs[:n_x], lambda t: t, tiles_per_b), mod_ref, g_ref, w_ref, o_ref, h_scr)


def _in_proj(x_parts, mod3, g, w_in_r, n_tiles, tiles_per_b, mod_row):
    n_rows = n_tiles * TM
    return pl.pallas_call(
        functools.partial(_in_proj_kernel, n_x=len(x_parts), tiles_per_b=tiles_per_b),
        grid=(n_tiles,),
        in_specs=_stream_specs(x_parts, lambda t: t, tiles_per_b) + [
                  pl.BlockSpec((None, 1, 6 * D), lambda t: (mod_row(t), 0, 0)),
                  pl.BlockSpec((1, D), lambda t: (0, 0)),
                  pl.BlockSpec((D, N_IN_PAD), lambda t: (0, 0), pipeline_mode=pl.Buffered(1))],
        out_specs=pl.BlockSpec((TM, N_IN_PAD), lambda t: (t, 0)),
        out_shape=jax.ShapeDtypeStruct((n_rows, N_IN_PAD), BF16),
        scratch_shapes=[pltpu.VMEM((TM, D), BF16)],
        compiler_params=_cparams(1),
        name="in_proj",
    )(*x_parts, mod3, g, w_in_r)


def _retention_kernel(lg_ref, q_ref, k_ref, v_ref, g_ref, gng_ref, gnb_ref, y_ref, of_scr, ob_scr,
                      *, n_lat_chunks, n_ctx_chunks):
    h = pl.program_id(1)
    lg_f = lg_ref[0, h]
    lg_b = lg_ref[1, h]
    C = RET_CHUNK
    k_scale = RET_D ** -0.5
    ri = lax.broadcasted_iota(jnp.int32, (C, C), 0).astype(F32)
    ci = lax.broadcasted_iota(jnp.int32, (C, C), 1).astype(F32)
    pos = lax.broadcasted_iota(jnp.int32, (C, 1), 0).astype(F32)
    diff = ri - ci
    d_f = jnp.where(diff >= 0, jnp.exp(lg_f * jnp.maximum(diff, 0.0)), 0.0) * k_scale
    d_b = jnp.where(diff < 0, jnp.exp(lg_b * jnp.maximum(-diff, 0.0)), 0.0) * k_scale
    qdec_f = jnp.exp(lg_f * (pos + 1.0))
    kdec_f = jnp.exp(lg_f * (C - 1.0 - pos)) * k_scale
    cdec_f = jnp.exp(lg_f * C)
    qdec_b = jnp.exp(lg_b * (C - pos))
    kdec_b = jnp.exp(lg_b * pos) * k_scale
    cdec_b = jnp.exp(lg_b * C)

    d_both = d_f + d_b

    def chunk(c, state, qdec, kdec, cdec, with_intra):
        rows = pl.ds(pl.multiple_of(c * C, C), C)
        q = q_ref[rows, :]
        k = k_ref[rows, :]
        v = v_ref[rows, :]
        o = _dot((q.astype(F32) * qdec).astype(BF16), state.astype(BF16))
        if with_intra:
            o = o + _dot((_dot_nt(q, k) * d_both).astype(BF16), v)
        kd = (k.astype(F32) * kdec).astype(BF16)
        return rows, o, state * cdec + _dot_tn(kd, v)

    n_all = n_lat_chunks + n_ctx_chunks

    def scan_body(i, states):
        s_f, s_b = states
        c_f = jnp.where(i < n_ctx_chunks, n_lat_chunks + i, i - n_ctx_chunks)
        rows, o, s_f = chunk(c_f, s_f, qdec_f, kdec_f, cdec_f, True)
        of_scr[rows, :] = o
        rows, o, s_b = chunk(n_all - 1 - i, s_b, qdec_b, kdec_b, cdec_b, False)
        ob_scr[rows, :] = o
        return s_f, s_b

    zero = jnp.zeros((RET_D, RET_D), F32)
    lax.fori_loop(0, n_all, scan_body, (zero, zero), unroll=2)

    def out_body(c, _):
        rows = pl.ds(pl.multiple_of(c * RET_OUT_ROWS, RET_OUT_ROWS), RET_OUT_ROWS)
        o = of_scr[rows, :] + ob_scr[rows, :]
        mu = jnp.mean(o, axis=-1, keepdims=True)
        var = jnp.mean(jnp.square(o - mu), axis=-1, keepdims=True)
        on = (o - mu) * lax.rsqrt(var + EPS)
        y = _silu(g_ref[rows, :].astype(F32)) * (on * gng_ref[...] + gnb_ref[...])
        y_ref[rows, :] = y.astype(BF16)
        return 0

    lax.fori_loop(0, n_all * C // RET_OUT_ROWS, out_body, 0, unroll=3)


def _retention(p3, lg, gn_g, gn_b, seq, ctx):
    B, lt, _ = p3.shape
    base = C_RET // RET_D
    kern = functools.partial(_retention_kernel, n_lat_chunks=seq // RET_CHUNK, n_ctx_chunks=ctx // RET_CHUNK)

    def col(off):
        return pl.BlockSpec((None, lt, RET_D), lambda b, h: (b, 0, base + off * RET_HEADS + h))

    return pl.pallas_call(
        kern,
        grid=(B, RET_HEADS),
        in_specs=[pl.BlockSpec(memory_space=pltpu.SMEM),
                  col(0), col(1), col(2), col(3),
                  pl.BlockSpec((1, RET_D), lambda b, h: (0, h)),
                  pl.BlockSpec((1, RET_D), lambda b, h: (0, h))],
        out_specs=pl.BlockSpec((None, lt, RET_D), lambda b, h: (b, 0, h)),
        out_shape=jax.ShapeDtypeStruct((B, lt, RET_HEADS * RET_D), BF16),
        scratch_shapes=[pltpu.VMEM((lt, RET_D), F32), pltpu.VMEM((lt, RET_D), F32)],
        compiler_params=_cparams(2),
        name="retention",
    )(lg, p3, p3, p3, p3, gn_g, gn_b)


def _mla_prep_kernel(cq_ref, ckv_ref, kr_ref, cos_ref, sin_ref, swap_ref, qa_ref, kva_ref, qn_ref, kn_ref, krg_ref,
                     wq_ref, wk_ref, wv_ref, q_ref, k_ref, v_ref):
    cos = cos_ref[...]
    sin = sin_ref[...]

    def rms(x, n):
        return x * lax.rsqrt(jnp.sum(x * x, axis=-1, keepdims=True) * (1.0 / n) + EPS)

    def rope(x):
        return x * cos + _dot(x.astype(BF16), swap_ref[...]) * sin

    cq = cq_ref[...].astype(F32)
    cqn = (rms(cq, MLA_Q_LORA) * qa_ref[...]).astype(BF16)
    q_all = _dot(cqn, wq_ref[...])
    ckv = ckv_ref[...].astype(F32)
    ckvn = (rms(ckv, MLA_KV_LORA) * kva_ref[...]).astype(BF16)
    k_all = _dot(ckvn, wk_ref[...])
    k_rope = rope(rms(kr_ref[...].astype(F32), MLA_ROPE) * krg_ref[...])
    scale = MLA_QK ** -0.5 * LOG2_E
    v_t = _dot_nt(wv_ref[...], ckvn)
    ones_row = jnp.where(lax.broadcasted_iota(jnp.int32, (MLA_V_EXT - MLA_V, TM), 0) == 0, 1.0, 0.0)
    for h in range(MLA_HEADS):
        cols = slice(h * HEAD_PAD, (h + 1) * HEAD_PAD)
        qh = rope(rms(q_all[:, cols], MLA_QK) * qn_ref[...]) * scale
        q_ref[h] = qh.astype(BF16)
        kh = rms(k_all[:, cols], MLA_NOPE) * kn_ref[...] + k_rope
        k_ref[h] = kh.astype(BF16)
        v_ref[h] = jnp.concatenate([v_t[h * MLA_V:(h + 1) * MLA_V, :], ones_row], axis=0).astype(BF16)


def _mla_prep(p, cos_t, sin_t, swap, qa_g, kva_g, qn_p, kn_p, kr_p, wq_p, wk_p, wv, B, lt):
    tiles_per_b = lt // TM
    hw = MLA_HEADS * HEAD_PAD
    const = lambda shape: pl.BlockSpec(shape, lambda b, j: (0,) * len(shape))
    head_out = pl.BlockSpec((None, MLA_HEADS, TM, HEAD_PAD), lambda b, j: (b, 0, j, 0))
    shp = jax.ShapeDtypeStruct((B, MLA_HEADS, lt, HEAD_PAD), BF16)
    v_out = pl.BlockSpec((None, MLA_HEADS, None, MLA_V_EXT, TM), lambda b, j: (b, 0, j, 0, 0))
    v_shp = jax.ShapeDtypeStruct((B, MLA_HEADS, tiles_per_b, MLA_V_EXT, TM), BF16)
    return pl.pallas_call(
        _mla_prep_kernel,
        grid=(B, tiles_per_b),
        in_specs=[pl.BlockSpec((TM, MLA_Q_LORA), lambda b, j: (b * tiles_per_b + j, C_CQ // MLA_Q_LORA)),
                  pl.BlockSpec((TM, MLA_KV_LORA), lambda b, j: (b * tiles_per_b + j, C_CKV // MLA_KV_LORA)),
                  pl.BlockSpec((TM, HEAD_PAD), lambda b, j: (b * tiles_per_b + j, C_KR // HEAD_PAD)),
                  pl.BlockSpec((TM, HEAD_PAD), lambda b, j: (j, 0)),
                  pl.BlockSpec((TM, HEAD_PAD), lambda b, j: (j, 0)),
                  const((HEAD_PAD, HEAD_PAD)),
                  const((1, MLA_Q_LORA)), const((1, MLA_KV_LORA)),
                  const((1, HEAD_PAD)), const((1, HEAD_PAD)), const((1, HEAD_PAD)),
                  const((MLA_Q_LORA, hw)), const((MLA_KV_LORA, hw)), const((MLA_HEADS * MLA_V, MLA_KV_LORA))],
        out_specs=[head_out, head_out, v_out],
        out_shape=[shp, shp, v_shp],
        compiler_params=_cparams(2),
        name="mla_prep",
    )(p, p, p, cos_t, sin_t, swap, qa_g, kva_g, qn_p, kn_p, kr_p, wq_p, wk_p, wv)


def _attention_stages(seq, ctx):
    n_blk = ATT_KV_CHUNK // TM
    total = (seq + ctx) // TM
    first = n_blk + total % n_blk if total >= n_blk else total
    return [(total - first, first)] + [(c * n_blk, n_blk) for c in range((total - first) // n_blk)]


def _attention_kernel(q_ref, k_ref, v_ref, o_ref, s_scr, *, seq, ctx, ctx_tile):
    i = pl.program_id(2)

    def scores(hh, slot, blk, nb):
        s_scr[hh, slot, 0:nb * TM, :] = _dot_nt(k_ref[hh, blk * TM:(blk + nb) * TM, :], q_ref[hh])

    def absorb(hh, slot, blk, nb, carry):
        m, acc = carry
        s = s_scr[hh, slot, 0:nb * TM, :]
        m_new = jnp.maximum(m, jnp.max(s, axis=0, keepdims=True))
        p = jnp.exp2(s - m_new).astype(BF16)
        acc = jnp.exp2(m - m_new) * acc
        for j in range(nb):
            acc = acc + _dot(v_ref[hh, blk + j], p[j * TM:(j + 1) * TM, :])
        return m_new, acc

    def attend(stages):
        for hh in range(ATT_HEADS):
            scores(hh, 0, *stages[0])
        carries = [(jnp.full((1, TM), -jnp.inf, F32), jnp.zeros((MLA_V_EXT, TM), F32))] * ATT_HEADS
        for n, stage in enumerate(stages):
            for hh in range(ATT_HEADS):
                if n + 1 < len(stages):
                    scores(hh, (n + 1) % 2, *stages[n + 1])
                carries[hh] = absorb(hh, n % 2, *stage, carries[hh])
        outs = [acc[0:MLA_V, :] / acc[MLA_V:MLA_V + 1, :] for _, acc in carries]
        o_ref[...] = jnp.concatenate(outs, axis=0).T.astype(BF16)

    @pl.when(i != ctx_tile)
    def _():
        attend(_attention_stages(seq, ctx))

    @pl.when(i == ctx_tile)
    def _():
        attend([(seq // TM, ctx // TM)])


def _attention(q, k, v, seq, ctx, n_q_tiles):
    B, H, lt, _ = q.shape
    kern = functools.partial(_attention_kernel, seq=seq, ctx=ctx, ctx_tile=seq // TM)
    slot_rows = max(nb for _, nb in _attention_stages(seq, ctx)) * TM
    return pl.pallas_call(
        kern,
        grid=(B, H // ATT_HEADS, n_q_tiles),
        in_specs=[pl.BlockSpec((None, ATT_HEADS, TM, HEAD_PAD), lambda b, h, i: (b, h, i, 0)),
                  pl.BlockSpec((None, ATT_HEADS, lt, HEAD_PAD), lambda b, h, i: (b, h, 0, 0)),
                  pl.BlockSpec((None, ATT_HEADS, lt // TM, MLA_V_EXT, TM), lambda b, h, i: (b, h, 0, 0, 0))],
        out_specs=pl.BlockSpec((None, TM, ATT_HEADS * MLA_V), lambda b, h, i: (b, i, h)),
        out_shape=jax.ShapeDtypeStruct((B, lt, H * MLA_V), BF16),
        scratch_shapes=[pltpu.VMEM((ATT_HEADS, 2, slot_rows, TM), F32)],
        compiler_params=_cparams(3),
        name="attention",
    )(q, k, v)


def _merge_kernel(*refs, n_x, tile, tiles_per_b):
    (mod_ref, mg_ref, uv_ref, yr_ref, om_ref, lng_ref, lnb_ref, ws_ref, bs_ref,
     wr_ref, wm_ref, wg_ref, wo_ref, n2_ref, x1_ref, h2_ref) = refs[n_x:]
    yr = _dot(yr_ref[...], wr_ref[...])
    ym = _dot(om_ref[...], wm_ref[...])
    z = jax.nn.gelu(uv_ref[...].astype(F32))
    u = z[:, :GMLP_W]
    v = z[:, GMLP_W:]
    mu = jnp.mean(v, axis=-1, keepdims=True)
    var = jnp.mean(jnp.square(v - mu), axis=-1, keepdims=True)
    vn = ((v - mu) * lax.rsqrt(var + EPS) * lng_ref[...] + lnb_ref[...]).astype(BF16)
    gw = GMLP_W // GMLP_GROUPS
    chunks = []
    for c in range(TM // GMLP_CHUNK):
        rows = slice(c * GMLP_CHUNK, (c + 1) * GMLP_CHUNK)
        groups = [_dot(ws_ref[g], vn[rows, g * gw:(g + 1) * gw]) + bs_ref[g] for g in range(GMLP_GROUPS)]
        chunks.append(jnp.concatenate(groups, axis=1))
    sv = jnp.concatenate(chunks, axis=0)
    yg = _dot((u * sv).astype(BF16), wg_ref[...])
    gate = jax.nn.sigmoid(mg_ref[...].astype(F32))
    y = gate[:, :D] * yr + gate[:, D:2 * D] * ym + gate[:, 2 * D:] * yg
    out = _dot(y.astype(BF16), wo_ref[...])
    x1 = _read_stream(refs[:n_x], tile, tiles_per_b) + mod_ref[:, 2 * D:3 * D] * out
    x1_ref[...] = x1
    h2_ref[...] = _modulated_rmsnorm(x1, n2_ref[...], mod_ref[:, 3 * D:4 * D], mod_ref[:, 4 * D:5 * D])


def _merge(x_parts, mod3, p, y_ret, o_mla, ln_g, ln_b, ws, bs_full, w_br_ret, w_br_mla, w_br_gmlp, w_out, n2_g,
           n_tiles, tile, tiles_per_b, mod_row):
    n_rows = p.shape[0]
    const = lambda shape: pl.BlockSpec(shape, lambda t: (0,) * len(shape))
    row = lambda w, cb=0: pl.BlockSpec((TM, w), lambda t: (tile(t), cb))
    shp = jax.ShapeDtypeStruct((n_rows, D), F32)
    return pl.pallas_call(
        functools.partial(_merge_kernel, n_x=len(x_parts), tile=tile, tiles_per_b=tiles_per_b),
        grid=(n_tiles,),
        in_specs=_stream_specs(x_parts, tile, tiles_per_b) + [
                  pl.BlockSpec((None, 1, 6 * D), lambda t: (mod_row(tile(t)), 0, 0)),
                  row(3 * D, C_MERGE // (3 * D)), row(D, C_UV // D),
                  row(RET_HEADS * RET_D), row(MLA_HEADS * MLA_V),
                  const((1, GMLP_W)), const((1, GMLP_W)),
                  const((GMLP_GROUPS, GMLP_CHUNK, GMLP_CHUNK)), const((GMLP_GROUPS, GMLP_CHUNK, GMLP_CHUNK)),
                  const((RET_HEADS * RET_D, D)), const((MLA_HEADS * MLA_V, D)), const((GMLP_W, D)),
                  const((D, D)), const((1, D))],
        out_specs=[row(D), row(D)],
        out_shape=[shp, shp],
        compiler_params=_cparams(1),
        name="merge",
    )(*x_parts, mod3, p, p, y_ret, o_mla, ln_g, ln_b, ws, bs_full, w_br_ret, w_br_mla, w_br_gmlp, w_out, n2_g)


def _route_kernel(ha_ref, hb_ref, rt_ref, bt_ref, idx_ref, w_ref, rank_ref, cnt_ref, hp_ref, cnt_scr):
    @pl.when(pl.program_id(0) == 0)
    def _():
        cnt_scr[...] = jnp.zeros_like(cnt_scr)

    h = jnp.concatenate([ha_ref[...], hb_ref[...]], axis=0)
    logits = lax.dot_general(rt_ref[...], h, (((1,), (1,)), ((), ())), preferred_element_type=F32,
                             precision=lax.Precision.HIGHEST)
    scores = jax.nn.sigmoid(logits)
    sel = scores + bt_ref[:, 0:1]
    row_e = lax.broadcasted_iota(jnp.int32, (N_EXPERTS, RT), 0).astype(F32)
    row_o = lax.broadcasted_iota(jnp.int32, (SUBLANES, RT), 0)
    idx_out = jnp.zeros((SUBLANES, RT), F32)
    w_out = jnp.zeros((SUBLANES, RT), F32)
    hits = []
    for k in range(TOP_K):
        best = jnp.max(sel, axis=0, keepdims=True)
        pick = jnp.min(jnp.where(sel == best, row_e, float(N_EXPERTS)), axis=0, keepdims=True)
        hit = row_e == pick
        hits.append(hit)
        wk = jnp.sum(jnp.where(hit, scores, 0.0), axis=0, keepdims=True)
        sel = jnp.where(hit, -jnp.inf, sel)
        idx_out = jnp.where(row_o == k, pick, idx_out)
        w_out = jnp.where(row_o == k, wk, w_out)
    w_out = w_out / jnp.sum(w_out, axis=0, keepdims=True) * ROUTED_SCALE
    idx_ref[...] = idx_out.astype(jnp.int32)
    w_ref[...] = w_out
    chosen = jnp.zeros((N_EXPERTS, RT), F32)
    for hit in hits:
        chosen = jnp.where(hit, 1.0, chosen)
    earlier = (lax.broadcasted_iota(jnp.int32, (RT, RT), 0) < lax.broadcasted_iota(jnp.int32, (RT, RT), 1))
    before = _dot(chosen.astype(BF16), jnp.where(earlier, 1.0, 0.0).astype(BF16)) + cnt_scr[:, 0:1]
    rank_out = jnp.zeros((SUBLANES, RT), F32)
    for k, hit in enumerate(hits):
        rank_out = jnp.where(row_o == k, jnp.sum(jnp.where(hit, before, 0.0), axis=0, keepdims=True), rank_out)
    rank_ref[...] = rank_out.astype(jnp.int32)
    cnt_scr[...] += jnp.sum(chosen, axis=1, keepdims=True)
    cnt_ref[...] = cnt_scr[...]
    hp_ref[...] = _pack_bf16_pairs(h)


def _route(h2, router_t, bias_t, n_tiles, tile):
    const = lambda shape: pl.BlockSpec(shape, lambda t: (0,) * len(shape))
    n_act = n_tiles * TM
    assert n_tiles % 2 == 0
    n_steps = n_tiles // 2
    per_tok = pl.BlockSpec((None, SUBLANES, RT), lambda t: (t, 0, 0))
    idx_t, w_t, rank_t, counts, hp = pl.pallas_call(
        _route_kernel,
        grid=(n_steps,),
        in_specs=[pl.BlockSpec((TM, D), lambda t: (tile(2 * t), 0)),
                  pl.BlockSpec((TM, D), lambda t: (tile(2 * t + 1), 0)),
                  const((N_EXPERTS, D)), const((N_EXPERTS, LANES))],
        out_specs=[per_tok, per_tok, per_tok,
                   pl.BlockSpec((N_EXPERTS, LANES), lambda t: (0, 0)),
                   pl.BlockSpec((RT, D // 2), lambda t: (t, 0))],
        out_shape=[jax.ShapeDtypeStruct((n_steps, SUBLANES, RT), jnp.int32),
                   jax.ShapeDtypeStruct((n_steps, SUBLANES, RT), F32),
                   jax.ShapeDtypeStruct((n_steps, SUBLANES, RT), jnp.int32),
                   jax.ShapeDtypeStruct((N_EXPERTS, LANES), F32),
                   jax.ShapeDtypeStruct((n_act, D // 2), jnp.uint32)],
        scratch_shapes=[pltpu.VMEM((N_EXPERTS, LANES), F32)],
        compiler_params=_cparams(1),
        name="route",
    )(h2, h2, router_t, bias_t)
    token_major = lambda a: a.transpose(0, 2, 1).reshape(n_act, SUBLANES)
    return (token_major(idx_t)[:, :TOP_K], token_major(w_t), token_major(rank_t)[:, :TOP_K], counts[:, 0], hp)


def _shared_expert_kernel(h_ref, sg_ref, su_ref, sd_ref, o_ref):
    lo, hi = _unpack_bf16_pairs(h_ref[...])
    hb = jnp.concatenate([lo, hi], axis=1).astype(BF16)
    a = _silu(_dot(hb, sg_ref[...])) * _dot(hb, su_ref[...])
    o_ref[...] = _dot(a.astype(BF16), sd_ref[...]).astype(BF16)


def _shared_expert(hp, sg, su, sd):
    const = lambda shape: pl.BlockSpec(shape, lambda t: (0,) * len(shape))
    n_act = hp.shape[0]
    rows = math.gcd(n_act, SHARED_ROWS)
    return pl.pallas_call(
        _shared_expert_kernel,
        grid=(n_act // rows,),
        in_specs=[pl.BlockSpec((rows, D // 2), lambda t: (t, 0)),
                  const((D, D_EXPERT)), const((D, D_EXPERT)), const((D_EXPERT, D))],
        out_specs=pl.BlockSpec((rows, D), lambda t: (t, 0)),
        out_shape=jax.ShapeDtypeStruct((n_act, D), BF16),
        compiler_params=_cparams(1),
        name="shared_expert",
    )(hp, sg, su, sd)


def _dispatch(pos_sc, hp, n_rows):
    n_batches = pos_sc.shape[0]
    n_workers = SC_CORES * SC_SUBCORES
    mesh = plsc.VectorSubcoreMesh(core_axis_name="c", subcore_axis_name="s")

    @functools.partial(
        pl.kernel, mesh=mesh,
        out_type=jax.ShapeDtypeStruct((n_rows, D // 2), jnp.uint32),
        scratch_types=[pltpu.VMEM((TOP_K, SC_ROWS), jnp.int32),
                       pltpu.VMEM((SC_ROWS, D // 2), jnp.uint32),
                       pltpu.SemaphoreType.DMA],
        name="moe_dispatch")
    def scatter(pos_hbm, h_hbm, xs_hbm, idx_v, rows_v, sem):
        worker = lax.axis_index("s") * SC_CORES + lax.axis_index("c")

        @pl.loop(0, pl.cdiv(n_batches, n_workers))
        def _(j):
            b = j * n_workers + worker

            @pl.when(b < n_batches)
            def _():
                pltpu.sync_copy(pos_hbm.at[b], idx_v)
                pltpu.sync_copy(h_hbm.at[pl.ds(b * SC_ROWS, SC_ROWS)], rows_v)
                copies = [pltpu.async_copy(rows_v, xs_hbm.at[idx_v.at[k]], sem) for k in range(TOP_K)]
                for cp in copies:
                    cp.wait()

    return scatter(pos_sc, hp)


def _expert_kernel(blk_e_ref, valid_ref, x_ref, wg_ref, wu_ref, wd_ref, y_ref, wg_s, wu_s, wd_s):
    i = pl.program_id(0)
    n_valid = valid_ref[i]

    @pl.when(n_valid > 0)
    def _():
        @pl.when(jnp.logical_or(i == 0, blk_e_ref[i] != blk_e_ref[jnp.maximum(i - 1, 0)]))
        def _():
            wg_s[...] = wg_ref[...].astype(BF16)
            wu_s[...] = wu_ref[...].astype(BF16)
            wd_s[...] = wd_ref[...].astype(BF16)

        row = lax.broadcasted_iota(jnp.int32, (MOE_ROWS, 1), 0)
        lo, hi = _unpack_bf16_pairs(jnp.where(row < n_valid, x_ref[...], jnp.uint32(0)))
        x = jnp.concatenate([lo, hi], axis=1).astype(BF16)
        hb = _silu(_dot(x, wg_s[...])) * _dot(x, wu_s[...])
        y_ref[...] = _pack_bf16_pairs(_dot(hb.astype(BF16), wd_s[...]))

    @pl.when(n_valid == 0)
    def _():
        y_ref[...] = jnp.zeros_like(y_ref)


def _experts(blk_e, valid, xs, wg, wu, wd, layer, n_blocks):
    grid_spec = pltpu.PrefetchScalarGridSpec(
        num_scalar_prefetch=2,
        grid=(n_blocks,),
        in_specs=[pl.BlockSpec((MOE_ROWS, D // 2), lambda i, be, nv: (i, 0)),
                  pl.BlockSpec((None, None, D, D_EXPERT), lambda i, be, nv: (layer, be[i], 0, 0)),
                  pl.BlockSpec((None, None, D, D_EXPERT), lambda i, be, nv: (layer, be[i], 0, 0)),
                  pl.BlockSpec((None, None, D_EXPERT, D), lambda i, be, nv: (layer, be[i], 0, 0))],
        out_specs=pl.BlockSpec((MOE_ROWS, D // 2), lambda i, be, nv: (i, 0)),
        scratch_shapes=[pltpu.VMEM((D, D_EXPERT), BF16), pltpu.VMEM((D, D_EXPERT), BF16),
                        pltpu.VMEM((D_EXPERT, D), BF16)],
    )
    return pl.pallas_call(
        _expert_kernel,
        grid_spec=grid_spec,
        out_shape=jax.ShapeDtypeStruct((n_blocks * MOE_ROWS, D // 2), jnp.uint32),
        compiler_params=_cparams(1),
        name="routed_experts",
    )(blk_e, valid, xs, wg, wu, wd)


def _gather_rows(pos_sc, ys):
    n_batches = pos_sc.shape[0]
    n_workers = SC_CORES * SC_SUBCORES
    half = SC_ROWS // 2
    mesh = plsc.VectorSubcoreMesh(core_axis_name="c", subcore_axis_name="s")

    @functools.partial(
        pl.kernel, mesh=mesh,
        out_type=jax.ShapeDtypeStruct((TOP_K, n_batches * SC_ROWS, D // 2), jnp.uint32),
        scratch_types=[pltpu.VMEM((TOP_K, SC_ROWS), jnp.int32),
                       pltpu.VMEM((2, half, D // 2), jnp.uint32),
                       pltpu.SemaphoreType.DMA, pltpu.SemaphoreType.DMA],
        name="moe_gather")
    def gather(pos_hbm, y_hbm, out_hbm, idx_v, bufs, gsem, wsem):
        worker = lax.axis_index("s") * SC_CORES + lax.axis_index("c")

        @pl.loop(0, pl.cdiv(n_batches, n_workers))
        def _(j):
            b = j * n_workers + worker

            @pl.when(b < n_batches)
            def _():
                pltpu.sync_copy(pos_hbm.at[b], idx_v)
                items = [(k, h) for k in range(TOP_K) for h in range(2)]

                def fetch(i):
                    k, h = items[i]
                    return pltpu.async_copy(y_hbm.at[idx_v.at[k, pl.ds(h * half, half)]], bufs.at[i % 2], gsem)

                pending_gather = fetch(0)
                pending_write = None
                for i, (k, h) in enumerate(items):
                    pending_gather.wait()
                    if pending_write is not None:
                        pending_write.wait()
                    if i + 1 < len(items):
                        pending_gather = fetch(i + 1)
                    pending_write = pltpu.async_copy(
                        bufs.at[i % 2], out_hbm.at[k, pl.ds(b * SC_ROWS + h * half, half)], wsem)
                pending_write.wait()

    return gather(pos_sc, ys)


def _combine_tile(y_refs, w_ref, x1_ref, sh_ref, mod_ref):
    f_lo = sh_ref[:, :D // 2].astype(F32)
    f_hi = sh_ref[:, D // 2:].astype(F32)
    for k in range(TOP_K):
        lo, hi = _unpack_bf16_pairs(y_refs[k][...])
        f_lo = f_lo + lo * w_ref[:, k:k + 1]
        f_hi = f_hi + hi * w_ref[:, k:k + 1]
    return x1_ref[...] + mod_ref[:, 5 * D:6 * D] * jnp.concatenate([f_lo, f_hi], axis=1)


def _combine_kernel(*refs):
    o_ref = refs[-1]
    o_ref[...] = _combine_tile(refs[:TOP_K], *refs[TOP_K:-1])


def _combine_specs(tile, mod_row, tok=lambda t: t):
    planes = [pl.BlockSpec((None, TM, D // 2), lambda t, k=k: (k, tok(t), 0)) for k in range(TOP_K)]
    return planes + [pl.BlockSpec((TM, SUBLANES), lambda t: (tok(t), 0)),
                     pl.BlockSpec((TM, D), lambda t: (tile(t), 0)),
                     pl.BlockSpec((TM, D), lambda t: (tok(t), 0)),
                     pl.BlockSpec((None, 1, 6 * D), lambda t: (mod_row(tile(t)), 0, 0))]


def _combine(yg, w, x1, shared, mod3, n_tiles, tile, mod_row, out_rows, out_tile):
    return pl.pallas_call(
        _combine_kernel,
        grid=(n_tiles,),
        in_specs=_combine_specs(tile, mod_row),
        out_specs=pl.BlockSpec((TM, D), lambda t: (out_tile(t), 0)),
        out_shape=jax.ShapeDtypeStruct((out_rows, D), F32),
        compiler_params=_cparams(1),
        name="moe_combine",
    )(*([yg] * TOP_K), w, x1, shared, mod3)


def _combine_in_proj_kernel(*refs):
    xs_ref, o_ref, h_scr = refs[-3:]
    mod_ref, g_ref, w_ref = refs[TOP_K + 4:-3]
    step = pl.program_id(0)

    @pl.when(step == 0)
    def _():
        h_scr[...] = jnp.zeros_like(h_scr)

    def body(prev_slot, new_slot):
        for c in range(N_IN_PAD // IN_CHUNK):
            cols = slice(c * IN_CHUNK, (c + 1) * IN_CHUNK)
            o_ref[:, cols] = _dot(h_scr[prev_slot], w_ref[:, cols]).astype(BF16)
        x = _combine_tile(refs[:TOP_K], *refs[TOP_K:TOP_K + 4])
        xs_ref[...] = x
        h = _modulated_rmsnorm(x, g_ref[...], mod_ref[:, 0:D], mod_ref[:, D:2 * D])
        h_scr[new_slot] = h.astype(BF16)

    @pl.when(step % 2 == 0)
    def _():
        body(1, 0)

    @pl.when(step % 2 == 1)
    def _():
        body(0, 1)


def _combine_in_proj(yg, w, x1, shared, mod3_prev, mod3, g, w_in_r, n_tiles, mod_row):
    n_rows = x1.shape[0]
    cur = lambda t: jnp.minimum(t, n_tiles - 1)
    return pl.pallas_call(
        _combine_in_proj_kernel,
        grid=(n_tiles + 1,),
        in_specs=_combine_specs(cur, mod_row, cur) + [
            pl.BlockSpec((None, 1, 6 * D), lambda t: (mod_row(cur(t)), 0, 0)),
            pl.BlockSpec((1, D), lambda t: (0, 0)),
            pl.BlockSpec((D, N_IN_PAD), lambda t: (0, 0), pipeline_mode=pl.Buffered(1))],
        out_specs=[pl.BlockSpec((TM, D), lambda t: (cur(t), 0)),
                   pl.BlockSpec((TM, N_IN_PAD), lambda t: (jnp.maximum(t - 1, 0), 0))],
        out_shape=[jax.ShapeDtypeStruct((n_rows, D), F32),
                   jax.ShapeDtypeStruct((n_rows, N_IN_PAD), BF16)],
        scratch_shapes=[pltpu.VMEM((2, TM, D), BF16)],
        compiler_params=_cparams(1),
        name="combine_in_proj",
    )(*([yg] * TOP_K), w, x1, shared, mod3_prev, mod3, g, w_in_r)


def _moe_plan(idx, rank, counts, n_blocks):
    n = idx.shape[0]
    cnt = counts.reshape(N_EXPERTS).astype(jnp.int32)
    padded = (cnt + MOE_ROWS - 1) // MOE_ROWS * MOE_ROWS
    pad_end = jnp.cumsum(padded)
    pad_start = pad_end - padded
    experts = jnp.arange(N_EXPERTS, dtype=jnp.int32)
    pos = rank + jnp.sum(jnp.where(idx[:, :, None] == experts, pad_start, 0), axis=-1)
    blk_start = jnp.arange(n_blocks, dtype=jnp.int32) * MOE_ROWS
    blk_e = jnp.minimum(jnp.sum(blk_start[:, None] >= pad_end[None, :], axis=1), N_EXPERTS - 1).astype(jnp.int32)
    mine = blk_e[:, None] == experts
    in_expert = blk_start - jnp.sum(jnp.where(mine, pad_start, 0), axis=1)
    valid = jnp.clip(jnp.sum(jnp.where(mine, cnt, 0), axis=1) - in_expert, 0, MOE_ROWS).astype(jnp.int32)
    pos_sc = pos.astype(jnp.int32).reshape(n // SC_ROWS, SC_ROWS, TOP_K).transpose(0, 2, 1)
    return blk_e, valid, pos_sc


def _rope_tables(seq, ctx):
    half = MLA_ROPE // 2
    n_freq = half // 2
    inv = ROPE_THETA ** (-2.0 * jnp.arange(n_freq, dtype=F32) / half)
    t = jnp.arange(seq)
    ang_r = (t // GRID_W).astype(F32)[:, None] * inv
    ang_c = (t % GRID_W).astype(F32)[:, None] * inv
    cos = jnp.concatenate([jnp.cos(ang_r), jnp.cos(ang_r), jnp.cos(ang_c), jnp.cos(ang_c)], axis=1)
    sin = jnp.concatenate([-jnp.sin(ang_r), jnp.sin(ang_r), -jnp.sin(ang_c), jnp.sin(ang_c)], axis=1)
    pad_l = MLA_NOPE
    pad_r = HEAD_PAD - MLA_NOPE - MLA_ROPE
    cos = jnp.pad(cos, ((0, ctx), (pad_l, pad_r)), constant_values=1.0)
    cos = cos.at[seq:, :].set(1.0)
    sin = jnp.pad(sin, ((0, ctx), (pad_l, pad_r)))
    lane = jnp.arange(HEAD_PAD)
    partner = jnp.where(lane % 16 < 8, lane + 8, lane - 8)
    swap = (lane[:, None] == partner[None, :]).astype(BF16)
    return cos, sin, swap


def _pad_heads(w, n_heads, width, offset=0):
    k = w.shape[0]
    w = w.reshape(k, n_heads, width)
    w = jnp.pad(w, ((0, 0), (0, 0), (offset, HEAD_PAD - width - offset)))
    return w.reshape(k, n_heads * HEAD_PAD)


def _pad_vec(g, offset):
    return jnp.pad(g, (offset, HEAD_PAD - g.shape[0] - offset)).reshape(1, HEAD_PAD)


def _reorder_w_in(w):
    off_cq, off_ckv, off_kr, off_uv, off_merge = 2048, 2432, 2688, 2720, 3744
    kr = jnp.pad(w[:, off_kr:off_uv], ((0, 0), (MLA_NOPE, HEAD_PAD - MLA_NOPE - MLA_ROPE)))
    return jnp.concatenate([w[:, off_merge:], w[:, off_uv:off_merge], w[:, :off_cq],
                            w[:, off_ckv:off_kr], kr, w[:, off_cq:off_ckv]], axis=1).astype(BF16)


def kernel(x, c, ctx, c_ctx, ada_w, ada_b, norm1_g, norm2_g, w_in, ret_decay_fwd, ret_decay_bwd, ret_gn_g,
           ret_gn_b, w_br_ret, mla_qa_g, mla_w_uq, mla_kva_g, mla_w_ukv, mla_qn_g, mla_kn_g, mla_kr_g, w_br_mla,
           gmlp_ln_g, gmlp_ln_b, gmlp_ws, gmlp_bs, w_br_gmlp, w_out, moe_router, moe_bias, moe_w_gate, moe_w_up,
           moe_w_down, sh_w_gate, sh_w_up, sh_w_down):
    B, seq, _ = x.shape
    n_ctx = ctx.shape[1]
    depth = ada_w.shape[0]
    assert n_ctx == TM and seq % TM == 0 and B + 1 <= SUBLANES and TOP_K <= SUBLANES
    lt = seq + n_ctx
    tiles_per_b = lt // TM
    lat_tiles_per_b = seq // TM
    ctx_tile = lat_tiles_per_b

    def mod_row(t):
        return jnp.where(t % tiles_per_b == ctx_tile, B, t // tiles_per_b)

    c_rows = jnp.concatenate([c, c_ctx[None, :], jnp.zeros((SUBLANES - B - 1, D), F32)], axis=0)
    mod = _ada(c_rows, ada_w, ada_b)
    cos_t, sin_t, swap = _rope_tables(seq, n_ctx)
    x_parts = (x.reshape(B * seq, D), ctx.reshape(B * n_ctx, D))

    pending = None
    for l in range(depth):
        last = l == depth - 1
        mod3 = mod[l].reshape(SUBLANES, 1, 6 * D)
        if pending is None:
            p = _in_proj(x_parts, mod3, norm1_g[l].reshape(1, D), _reorder_w_in(w_in[l]), B * tiles_per_b,
                         tiles_per_b, mod_row)
        else:
            xs, p = _combine_in_proj(*pending, mod3, norm1_g[l].reshape(1, D), _reorder_w_in(w_in[l]),
                                     B * tiles_per_b, mod_row)
            x_parts = (xs,)

        lg = jnp.stack([jax.nn.log_sigmoid(ret_decay_fwd[l].astype(F32)),
                        jax.nn.log_sigmoid(ret_decay_bwd[l].astype(F32))])
        y_ret = _retention(p.reshape(B, lt, N_IN_PAD), lg, ret_gn_g[l].reshape(1, -1), ret_gn_b[l].reshape(1, -1),
                           seq, n_ctx)

        w_ukv = mla_w_ukv[l].reshape(MLA_KV_LORA, MLA_HEADS, MLA_NOPE + MLA_V)
        wk_p = _pad_heads(w_ukv[:, :, :MLA_NOPE].reshape(MLA_KV_LORA, -1), MLA_HEADS, MLA_NOPE).astype(BF16)
        wv = w_ukv[:, :, MLA_NOPE:].reshape(MLA_KV_LORA, MLA_HEADS * MLA_V).T.astype(BF16)
        wq_p = _pad_heads(mla_w_uq[l], MLA_HEADS, MLA_QK).astype(BF16)
        q, k, v = _mla_prep(p, cos_t, sin_t, swap, mla_qa_g[l].reshape(1, -1), mla_kva_g[l].reshape(1, -1),
                            _pad_vec(mla_qn_g[l], 0), _pad_vec(mla_kn_g[l], 0), _pad_vec(mla_kr_g[l], MLA_NOPE),
                            wq_p, wk_p, wv, B, lt)
        o_mla = _attention(q, k, v, seq, n_ctx, lat_tiles_per_b if last else tiles_per_b)

        if last:
            n_tiles = B * lat_tiles_per_b
            tile = lambda t: (t // lat_tiles_per_b) * tiles_per_b + t % lat_tiles_per_b
        else:
            n_tiles = B * tiles_per_b
            tile = lambda t: t
        bs_full = jnp.broadcast_to(gmlp_bs[l][:, :, None], (GMLP_GROUPS, GMLP_CHUNK, GMLP_CHUNK))
        x1, h2 = _merge(x_parts, mod3, p, y_ret.reshape(B * lt, -1), o_mla.reshape(B * lt, -1),
                        gmlp_ln_g[l].reshape(1, -1), gmlp_ln_b[l].reshape(1, -1), gmlp_ws[l].astype(BF16), bs_full,
                        w_br_ret[l].astype(BF16), w_br_mla[l].astype(BF16), w_br_gmlp[l].astype(BF16),
                        w_out[l].astype(BF16), norm2_g[l].reshape(1, D), n_tiles, tile, tiles_per_b, mod_row)

        bias_t = jnp.broadcast_to(moe_bias[l][:, None], (N_EXPERTS, LANES))
        idx, w, rank, counts, hp = _route(h2, moe_router[l].T, bias_t, n_tiles, tile)
        n_act = n_tiles * TM
        n_blocks = -(-(n_act * TOP_K + N_EXPERTS * (MOE_ROWS - 1)) // MOE_ROWS)
        blk_e, valid, pos_sc = _moe_plan(idx, rank, counts, n_blocks)
        xg = _dispatch(pos_sc, hp, n_blocks * MOE_ROWS)
        shared = _shared_expert(hp, sh_w_gate[l].astype(BF16), sh_w_up[l].astype(BF16), sh_w_down[l].astype(BF16))
        ys = _experts(blk_e, valid, xg, moe_w_gate, moe_w_up, moe_w_down, l, n_blocks)
        yg = _gather_rows(pos_sc, ys)
        if last:
            xs = _combine(yg, w, x1, shared, mod3, n_tiles, tile, mod_row, B * seq, lambda t: t)
        else:
            pending = (yg, w, x1, shared, mod3)
    return xs.reshape(B, seq, D)
```

```python
import functools
import math

import jax
import jax.numpy as jnp
from jax import lax
from jax.experimental import pallas as pl
from jax.experimental.pallas import tpu as pltpu
from jax.experimental.pallas import tpu_sc as plsc

F32 = jnp.float32
BF16 = jnp.bfloat16

D = 1024
GRID_W = 64
RET_HEADS = 4
RET_D = 128
RET_CHUNK = 256
RET_OUT_ROWS = 256
MLA_HEADS = 8
MLA_Q_LORA = 384
MLA_KV_LORA = 256
MLA_NOPE = 64
MLA_ROPE = 32
MLA_V = 64
MLA_V_EXT = MLA_V + 16
MLA_QK = MLA_NOPE + MLA_ROPE
HEAD_PAD = 128
ROPE_THETA = 10000.0
GMLP_GROUPS = 4
GMLP_W = 512
GMLP_CHUNK = 128
N_EXPERTS = 64
TOP_K = 6
D_EXPERT = 256
ROUTED_SCALE = 2.5
EPS = 1e-6
LOG2_E = 1.4426950408889634

LANES = 128
SUBLANES = 8
TM = 256
MOE_ROWS = 512
RT = 2 * TM
SHARED_ROWS = 1024
ATT_KV_CHUNK = 1280
ATT_HEADS = 4

C_MERGE = 0
C_UV = 3072
C_RET = 4096
C_CKV = 6144
C_KR = 6400
C_CQ = 6528
N_IN_PAD = 6912
IN_CHUNK = 768
ADA_CHUNK = 1536

VMEM_LIMIT = 56 * 1024 * 1024

SC_CORES = 2
SC_SUBCORES = 16
SC_ROWS = 128


def _cparams(n_axes, vmem=VMEM_LIMIT):
    return pltpu.CompilerParams(dimension_semantics=("arbitrary",) * n_axes, vmem_limit_bytes=vmem)


def _silu(x):
    return x * jax.nn.sigmoid(x)


def _dot(a, b):
    return jnp.dot(a, b, preferred_element_type=F32)


def _dot_nt(a, b):
    return lax.dot_general(a, b, (((1,), (1,)), ((), ())), preferred_element_type=F32)


def _dot_tn(a, b):
    return lax.dot_general(a, b, (((0,), (0,)), ((), ())), preferred_element_type=F32)


def _pack_bf16_pairs(x):
    n = x.shape[1] // 2
    lo = lax.bitcast_convert_type(x[:, :n].astype(BF16).astype(F32), jnp.uint32)
    hi = lax.bitcast_convert_type(x[:, n:].astype(BF16).astype(F32), jnp.uint32)
    return (lo >> 16) | hi


def _unpack_bf16_pairs(u):
    lo = lax.bitcast_convert_type(u << 16, F32)
    hi = lax.bitcast_convert_type(u & jnp.uint32(0xFFFF0000), F32)
    return lo, hi


def _ada_kernel(c_ref, w_ref, b_ref, o_ref):
    s = _silu(c_ref[...])
    o_ref[...] = _dot(s.astype(BF16), w_ref[...].astype(BF16)) + b_ref[...]


def _ada(c_rows, ada_w, ada_b):
    depth = ada_w.shape[0]
    n = ada_w.shape[2]
    cw = ADA_CHUNK
    return pl.pallas_call(
        _ada_kernel,
        grid=(depth, n // cw),
        in_specs=[pl.BlockSpec((SUBLANES, D), lambda l, j: (0, 0)),
                  pl.BlockSpec((None, D, cw), lambda l, j: (l, 0, j)),
                  pl.BlockSpec((None, 1, cw), lambda l, j: (l, 0, j))],
        out_specs=pl.BlockSpec((None, SUBLANES, cw), lambda l, j: (l, 0, j)),
        out_shape=jax.ShapeDtypeStruct((depth, SUBLANES, n), F32),
        compiler_params=_cparams(2),
        name="ada_mod",
    )(c_rows, ada_w, ada_b.reshape(depth, 1, n))


def _modulated_rmsnorm(x, g, shift, scale):
    y = x * lax.rsqrt(jnp.mean(x * x, axis=-1, keepdims=True) + EPS) * g
    return y * (1.0 + scale) + shift


def _stream_specs(x_parts, tile, tiles_per_b):
    if len(x_parts) == 1:
        return [pl.BlockSpec((TM, D), lambda t: (tile(t), 0))]
    lat_tiles = tiles_per_b - 1

    def latent(t):
        s = tile(t)
        return ((s // tiles_per_b) * lat_tiles + jnp.minimum(s % tiles_per_b, lat_tiles - 1), 0)

    return [pl.BlockSpec((TM, D), latent), pl.BlockSpec((TM, D), lambda t: (tile(t) // tiles_per_b, 0))]


def _read_stream(x_refs, tile, tiles_per_b):
    if len(x_refs) == 1:
        return x_refs[0][...]
    is_ctx = tile(pl.program_id(0)) % tiles_per_b == tiles_per_b - 1
    return jnp.where(is_ctx, x_refs[1][...], x_refs[0][...])


def _project_in(x, mod_ref, g_ref, w_ref, o_ref, h_scr):
    h = _modulated_rmsnorm(x, g_ref[...], mod_ref[:, 0:D], mod_ref[:, D:2 * D])
    h_scr[...] = h.astype(BF16)
    for c in range(N_IN_PAD // IN_CHUNK):
        cols = slice(c * IN_CHUNK, (c + 1) * IN_CHUNK)
        o_ref[:, cols] = _dot(h_scr[...], w_ref[:, cols]).astype(BF16)


def _in_proj_kernel(*refs, n_x, tiles_per_b):
    mod_ref, g_ref, w_ref, o_ref, h_scr = refs[n_x:]
    _project_in(_read_stream(refs[:n_x], lambda t: t, tiles_per_b), mod_ref, g_ref, w_ref, o_ref, h_scr)


def _in_proj(x_parts, mod3, g, w_in_r, n_tiles, tiles_per_b, mod_row):
    n_rows = n_tiles * TM
    return pl.pallas_call(
        functools.partial(_in_proj_kernel, n_x=len(x_parts), tiles_per_b=tiles_per_b),
        grid=(n_tiles,),
        in_specs=_stream_specs(x_parts, lambda t: t, tiles_per_b) + [
                  pl.BlockSpec((None, 1, 6 * D), lambda t: (mod_row(t), 0, 0)),
                  pl.BlockSpec((1, D), lambda t: (0, 0)),
                  pl.BlockSpec((D, N_IN_PAD), lambda t: (0, 0), pipeline_mode=pl.Buffered(1))],
        out_specs=pl.BlockSpec((TM, N_IN_PAD), lambda t: (t, 0)),
        out_shape=jax.ShapeDtypeStruct((n_rows, N_IN_PAD), BF16),
        scratch_shapes=[pltpu.VMEM((TM, D), BF16)],
        compiler_params=_cparams(1),
        name="in_proj",
    )(*x_parts, mod3, g, w_in_r)


def _retention_kernel(lg_ref, q_ref, k_ref, v_ref, g_ref, gng_ref, gnb_ref, y_ref, of_scr, ob_scr,
                      *, n_lat_chunks, n_ctx_chunks):
    h = pl.program_id(1)
    lg_f = lg_ref[0, h]
    lg_b = lg_ref[1, h]
    C = RET_CHUNK
    k_scale = RET_D ** -0.5
    ri = lax.broadcasted_iota(jnp.int32, (C, C), 0).astype(F32)
    ci = lax.broadcasted_iota(jnp.int32, (C, C), 1).astype(F32)
    pos = lax.broadcasted_iota(jnp.int32, (C, 1), 0).astype(F32)
    diff = ri - ci
    d_f = jnp.where(diff >= 0, jnp.exp(lg_f * jnp.maximum(diff, 0.0)), 0.0) * k_scale
    d_b = jnp.where(diff < 0, jnp.exp(lg_b * jnp.maximum(-diff, 0.0)), 0.0) * k_scale
    qdec_f = jnp.exp(lg_f * (pos + 1.0))
    kdec_f = jnp.exp(lg_f * (C - 1.0 - pos)) * k_scale
    cdec_f = jnp.exp(lg_f * C)
    qdec_b = jnp.exp(lg_b * (C - pos))
    kdec_b = jnp.exp(lg_b * pos) * k_scale
    cdec_b = jnp.exp(lg_b * C)

    d_both = d_f + d_b

    def chunk(c, state, qdec, kdec, cdec, with_intra):
        rows = pl.ds(pl.multiple_of(c * C, C), C)
        q = q_ref[rows, :]
        k = k_ref[rows, :]
        v = v_ref[rows, :]
        o = _dot((q.astype(F32) * qdec).astype(BF16), state.astype(BF16))
        if with_intra:
            o = o + _dot((_dot_nt(q, k) * d_both).astype(BF16), v)
        kd = (k.astype(F32) * kdec).astype(BF16)
        return rows, o, state * cdec + _dot_tn(kd, v)

    n_all = n_lat_chunks + n_ctx_chunks

    def scan_body(i, states):
        s_f, s_b = states
        c_f = jnp.where(i < n_ctx_chunks, n_lat_chunks + i, i - n_ctx_chunks)
        rows, o, s_f = chunk(c_f, s_f, qdec_f, kdec_f, cdec_f, True)
        of_scr[rows, :] = o
        rows, o, s_b = chunk(n_all - 1 - i, s_b, qdec_b, kdec_b, cdec_b, False)
        ob_scr[rows, :] = o
        return s_f, s_b

    zero = jnp.zeros((RET_D, RET_D), F32)
    lax.fori_loop(0, n_all, scan_body, (zero, zero), unroll=2)

    def out_body(c, _):
        rows = pl.ds(pl.multiple_of(c * RET_OUT_ROWS, RET_OUT_ROWS), RET_OUT_ROWS)
        o = of_scr[rows, :] + ob_scr[rows, :]
        mu = jnp.mean(o, axis=-1, keepdims=True)
        var = jnp.mean(jnp.square(o - mu), axis=-1, keepdims=True)
        on = (o - mu) * lax.rsqrt(var + EPS)
        y = _silu(g_ref[rows, :].astype(F32)) * (on * gng_ref[...] + gnb_ref[...])
        y_ref[rows, :] = y.astype(BF16)
        return 0

    lax.fori_loop(0, n_all * C // RET_OUT_ROWS, out_body, 0, unroll=3)


def _retention(p3, lg, gn_g, gn_b, seq, ctx):
    B, lt, _ = p3.shape
    base = C_RET // RET_D
    kern = functools.partial(_retention_kernel, n_lat_chunks=seq // RET_CHUNK, n_ctx_chunks=ctx // RET_CHUNK)

    def col(off):
        return pl.BlockSpec((None, lt, RET_D), lambda b, h: (b, 0, base + off * RET_HEADS + h))

    return pl.pallas_call(
        kern,
        grid=(B, RET_HEADS),
        in_specs=[pl.BlockSpec(memory_space=pltpu.SMEM),
                  col(0), col(1), col(2), col(3),
                  pl.BlockSpec((1, RET_D), lambda b, h: (0, h)),
                  pl.BlockSpec((1, RET_D), lambda b, h: (0, h))],
        out_specs=pl.BlockSpec((None, lt, RET_D), lambda b, h: (b, 0, h)),
        out_shape=jax.ShapeDtypeStruct((B, lt, RET_HEADS * RET_D), BF16),
        scratch_shapes=[pltpu.VMEM((lt, RET_D), F32), pltpu.VMEM((lt, RET_D), F32)],
        compiler_params=_cparams(2),
        name="retention",
    )(lg, p3, p3, p3, p3, gn_g, gn_b)


def _mla_prep_kernel(cq_ref, ckv_ref, kr_ref, cos_ref, sin_ref, swap_ref, qa_ref, kva_ref, qn_ref, kn_ref, krg_ref,
                     wq_ref, wk_ref, wv_ref, q_ref, k_ref, v_ref):
    cos = cos_ref[...]
    sin = sin_ref[...]

    def rms(x, n):
        return x * lax.rsqrt(jnp.sum(x * x, axis=-1, keepdims=True) * (1.0 / n) + EPS)

    def rope(x):
        return x * cos + _dot(x.astype(BF16), swap_ref[...]) * sin

    cq = cq_ref[...].astype(F32)
    cqn = (rms(cq, MLA_Q_LORA) * qa_ref[...]).astype(BF16)
    q_all = _dot(cqn, wq_ref[...])
    ckv = ckv_ref[...].astype(F32)
    ckvn = (rms(ckv, MLA_KV_LORA) * kva_ref[...]).astype(BF16)
    k_all = _dot(ckvn, wk_ref[...])
    k_rope = rope(rms(kr_ref[...].astype(F32), MLA_ROPE) * krg_ref[...])
    scale = MLA_QK ** -0.5 * LOG2_E
    v_t = _dot_nt(wv_ref[...], ckvn)
    ones_row = jnp.where(lax.broadcasted_iota(jnp.int32, (MLA_V_EXT - MLA_V, TM), 0) == 0, 1.0, 0.0)
    for h in range(MLA_HEADS):
        cols = slice(h * HEAD_PAD, (h + 1) * HEAD_PAD)
        qh = rope(rms(q_all[:, cols], MLA_QK) * qn_ref[...]) * scale
        q_ref[h] = qh.astype(BF16)
        kh = rms(k_all[:, cols], MLA_NOPE) * kn_ref[...] + k_rope
        k_ref[h] = kh.astype(BF16)
        v_ref[h] = jnp.concatenate([v_t[h * MLA_V:(h + 1) * MLA_V, :], ones_row], axis=0).astype(BF16)


def _mla_prep(p, cos_t, sin_t, swap, qa_g, kva_g, qn_p, kn_p, kr_p, wq_p, wk_p, wv, B, lt):
    tiles_per_b = lt // TM
    hw = MLA_HEADS * HEAD_PAD
    const = lambda shape: pl.BlockSpec(shape, lambda b, j: (0,) * len(shape))
    head_out = pl.BlockSpec((None, MLA_HEADS, TM, HEAD_PAD), lambda b, j: (b, 0, j, 0))
    shp = jax.ShapeDtypeStruct((B, MLA_HEADS, lt, HEAD_PAD), BF16)
    v_out = pl.BlockSpec((None, MLA_HEADS, None, MLA_V_EXT, TM), lambda b, j: (b, 0, j, 0, 0))
    v_shp = jax.ShapeDtypeStruct((B, MLA_HEADS, tiles_per_b, MLA_V_EXT, TM), BF16)
    return pl.pallas_call(
        _mla_prep_kernel,
        grid=(B, tiles_per_b),
        in_specs=[pl.BlockSpec((TM, MLA_Q_LORA), lambda b, j: (b * tiles_per_b + j, C_CQ // MLA_Q_LORA)),
                  pl.BlockSpec((TM, MLA_KV_LORA), lambda b, j: (b * tiles_per_b + j, C_CKV // MLA_KV_LORA)),
                  pl.BlockSpec((TM, HEAD_PAD), lambda b, j: (b * tiles_per_b + j, C_KR // HEAD_PAD)),
                  pl.BlockSpec((TM, HEAD_PAD), lambda b, j: (j, 0)),
                  pl.BlockSpec((TM, HEAD_PAD), lambda b, j: (j, 0)),
                  const((HEAD_PAD, HEAD_PAD)),
                  const((1, MLA_Q_LORA)), const((1, MLA_KV_LORA)),
                  const((1, HEAD_PAD)), const((1, HEAD_PAD)), const((1, HEAD_PAD)),
                  const((MLA_Q_LORA, hw)), const((MLA_KV_LORA, hw)), const((MLA_HEADS * MLA_V, MLA_KV_LORA))],
        out_specs=[head_out, head_out, v_out],
        out_shape=[shp, shp, v_shp],
        compiler_params=_cparams(2),
        name="mla_prep",
    )(p, p, p, cos_t, sin_t, swap, qa_g, kva_g, qn_p, kn_p, kr_p, wq_p, wk_p, wv)


def _attention_stages(seq, ctx):
    n_blk = ATT_KV_CHUNK // TM
    total = (seq + ctx) // TM
    first = n_blk + total % n_blk if total >= n_blk else total
    return [(total - first, first)] + [(c * n_blk, n_blk) for c in range((total - first) // n_blk)]


def _attention_kernel(q_ref, k_ref, v_ref, o_ref, s_scr, *, seq, ctx, ctx_tile):
    i = pl.program_id(2)

    def scores(hh, slot, blk, nb):
        s_scr[hh, slot, 0:nb * TM, :] = _dot_nt(k_ref[hh, blk * TM:(blk + nb) * TM, :], q_ref[hh])

    def absorb(hh, slot, blk, nb, carry):
        m, acc = carry
        s = s_scr[hh, slot, 0:nb * TM, :]
        m_new = jnp.maximum(m, jnp.max(s, axis=0, keepdims=True))
        p = jnp.exp2(s - m_new).astype(BF16)
        acc = jnp.exp2(m - m_new) * acc
        for j in range(nb):
            acc = acc + _dot(v_ref[hh, blk + j], p[j * TM:(j + 1) * TM, :])
        return m_new, acc

    def attend(stages):
        for hh in range(ATT_HEADS):
            scores(hh, 0, *stages[0])
        carries = [(jnp.full((1, TM), -jnp.inf, F32), jnp.zeros((MLA_V_EXT, TM), F32))] * ATT_HEADS
        for n, stage in enumerate(stages):
            for hh in range(ATT_HEADS):
                if n + 1 < len(stages):
                    scores(hh, (n + 1) % 2, *stages[n + 1])
                carries[hh] = absorb(hh, n % 2, *stage, carries[hh])
        outs = [acc[0:MLA_V, :] / acc[MLA_V:MLA_V + 1, :] for _, acc in carries]
        o_ref[...] = jnp.concatenate(outs, axis=0).T.astype(BF16)

    @pl.when(i != ctx_tile)
    def _():
        attend(_attention_stages(seq, ctx))

    @pl.when(i == ctx_tile)
    def _():
        attend([(seq // TM, ctx // TM)])


def _attention(q, k, v, seq, ctx, n_q_tiles):
    B, H, lt, _ = q.shape
    kern = functools.partial(_attention_kernel, seq=seq, ctx=ctx, ctx_tile=seq // TM)
    slot_rows = max(nb for _, nb in _attention_stages(seq, ctx)) * TM
    return pl.pallas_call(
        kern,
        grid=(B, H // ATT_HEADS, n_q_tiles),
        in_specs=[pl.BlockSpec((None, ATT_HEADS, TM, HEAD_PAD), lambda b, h, i: (b, h, i, 0)),
                  pl.BlockSpec((None, ATT_HEADS, lt, HEAD_PAD), lambda b, h, i: (b, h, 0, 0)),
                  pl.BlockSpec((None, ATT_HEADS, lt // TM, MLA_V_EXT, TM), lambda b, h, i: (b, h, 0, 0, 0))],
        out_specs=pl.BlockSpec((None, TM, ATT_HEADS * MLA_V), lambda b, h, i: (b, i, h)),
        out_shape=jax.ShapeDtypeStruct((B, lt, H * MLA_V), BF16),
        scratch_shapes=[pltpu.VMEM((ATT_HEADS, 2, slot_rows, TM), F32)],
        compiler_params=_cparams(3),
        name="attention",
    )(q, k, v)


def _merge_kernel(*refs, n_x, tile, tiles_per_b):
    (mod_ref, mg_ref, uv_ref, yr_ref, om_ref, lng_ref, lnb_ref, ws_ref, bs_ref,
     wr_ref, wm_ref, wg_ref, wo_ref, n2_ref, x1_ref, h2_ref) = refs[n_x:]
    yr = _dot(yr_ref[...], wr_ref[...])
    ym = _dot(om_ref[...], wm_ref[...])
    z = jax.nn.gelu(uv_ref[...].astype(F32))
    u = z[:, :GMLP_W]
    v = z[:, GMLP_W:]
    mu = jnp.mean(v, axis=-1, keepdims=True)
    var = jnp.mean(jnp.square(v - mu), axis=-1, keepdims=True)
    vn = ((v - mu) * lax.rsqrt(var + EPS) * lng_ref[...] + lnb_ref[...]).astype(BF16)
    gw = GMLP_W // GMLP_GROUPS
    chunks = []
    for c in range(TM // GMLP_CHUNK):
        rows = slice(c * GMLP_CHUNK, (c + 1) * GMLP_CHUNK)
        groups = [_dot(ws_ref[g], vn[rows, g * gw:(g + 1) * gw]) + bs_ref[g] for g in range(GMLP_GROUPS)]
        chunks.append(jnp.concatenate(groups, axis=1))
    sv = jnp.concatenate(chunks, axis=0)
    yg = _dot((u * sv).astype(BF16), wg_ref[...])
    gate = jax.nn.sigmoid(mg_ref[...].astype(F32))
    y = gate[:, :D] * yr + gate[:, D:2 * D] * ym + gate[:, 2 * D:] * yg
    out = _dot(y.astype(BF16), wo_ref[...])
    x1 = _read_stream(refs[:n_x], tile, tiles_per_b) + mod_ref[:, 2 * D:3 * D] * out
    x1_ref[...] = x1
    h2_ref[...] = _modulated_rmsnorm(x1, n2_ref[...], mod_ref[:, 3 * D:4 * D], mod_ref[:, 4 * D:5 * D])


def _merge(x_parts, mod3, p, y_ret, o_mla, ln_g, ln_b, ws, bs_full, w_br_ret, w_br_mla, w_br_gmlp, w_out, n2_g,
           n_tiles, tile, tiles_per_b, mod_row):
    n_rows = p.shape[0]
    const = lambda shape: pl.BlockSpec(shape, lambda t: (0,) * len(shape))
    row = lambda w, cb=0: pl.BlockSpec((TM, w), lambda t: (tile(t), cb))
    shp = jax.ShapeDtypeStruct((n_rows, D), F32)
    return pl.pallas_call(
        functools.partial(_merge_kernel, n_x=len(x_parts), tile=tile, tiles_per_b=tiles_per_b),
        grid=(n_tiles,),
        in_specs=_stream_specs(x_parts, tile, tiles_per_b) + [
                  pl.BlockSpec((None, 1, 6 * D), lambda t: (mod_row(tile(t)), 0, 0)),
                  row(3 * D, C_MERGE // (3 * D)), row(D, C_UV // D),
                  row(RET_HEADS * RET_D), row(MLA_HEADS * MLA_V),
                  const((1, GMLP_W)), const((1, GMLP_W)),
                  const((GMLP_GROUPS, GMLP_CHUNK, GMLP_CHUNK)), const((GMLP_GROUPS, GMLP_CHUNK, GMLP_CHUNK)),
                  const((RET_HEADS * RET_D, D)), const((MLA_HEADS * MLA_V, D)), const((GMLP_W, D)),
                  const((D, D)), const((1, D))],
        out_specs=[row(D), row(D)],
        out_shape=[shp, shp],
        compiler_params=_cparams(1),
        name="merge",
    )(*x_parts, mod3, p, p, y_ret, o_mla, ln_g, ln_b, ws, bs_full, w_br_ret, w_br_mla, w_br_gmlp, w_out, n2_g)


def _route_kernel(ha_ref, hb_ref, rt_ref, bt_ref, idx_ref, w_ref, rank_ref, cnt_ref, hp_ref, cnt_scr):
    @pl.when(pl.program_id(0) == 0)
    def _():
        cnt_scr[...] = jnp.zeros_like(cnt_scr)

    h = jnp.concatenate([ha_ref[...], hb_ref[...]], axis=0)
    logits = lax.dot_general(rt_ref[...], h, (((1,), (1,)), ((), ())), preferred_element_type=F32,
                             precision=lax.Precision.HIGHEST)
    scores = jax.nn.sigmoid(logits)
    sel = scores + bt_ref[:, 0:1]
    row_e = lax.broadcasted_iota(jnp.int32, (N_EXPERTS, RT), 0).astype(F32)
    row_o = lax.broadcasted_iota(jnp.int32, (SUBLANES, RT), 0)
    idx_out = jnp.zeros((SUBLANES, RT), F32)
    w_out = jnp.zeros((SUBLANES, RT), F32)
    hits = []
    for k in range(TOP_K):
        best = jnp.max(sel, axis=0, keepdims=True)
        pick = jnp.min(jnp.where(sel == best, row_e, float(N_EXPERTS)), axis=0, keepdims=True)
        hit = row_e == pick
        hits.append(hit)
        wk = jnp.sum(jnp.where(hit, scores, 0.0), axis=0, keepdims=True)
        sel = jnp.where(hit, -jnp.inf, sel)
        idx_out = jnp.where(row_o == k, pick, idx_out)
        w_out = jnp.where(row_o == k, wk, w_out)
    w_out = w_out / jnp.sum(w_out, axis=0, keepdims=True) * ROUTED_SCALE
    idx_ref[...] = idx_out.astype(jnp.int32)
    w_ref[...] = w_out
    chosen = jnp.zeros((N_EXPERTS, RT), F32)
    for hit in hits:
        chosen = jnp.where(hit, 1.0, chosen)
    earlier = (lax.broadcasted_iota(jnp.int32, (RT, RT), 0) < lax.broadcasted_iota(jnp.int32, (RT, RT), 1))
    before = _dot(chosen.astype(BF16), jnp.where(earlier, 1.0, 0.0).astype(BF16)) + cnt_scr[:, 0:1]
    rank_out = jnp.zeros((SUBLANES, RT), F32)
    for k, hit in enumerate(hits):
        rank_out = jnp.where(row_o == k, jnp.sum(jnp.where(hit, before, 0.0), axis=0, keepdims=True), rank_out)
    rank_ref[...] = rank_out.astype(jnp.int32)
    cnt_scr[...] += jnp.sum(chosen, axis=1, keepdims=True)
    cnt_ref[...] = cnt_scr[...]
    hp_ref[...] = _pack_bf16_pairs(h)


def _route(h2, router_t, bias_t, n_tiles, tile):
    const = lambda shape: pl.BlockSpec(shape, lambda t: (0,) * len(shape))
    n_act = n_tiles * TM
    assert n_tiles % 2 == 0
    n_steps = n_tiles // 2
    per_tok = pl.BlockSpec((None, SUBLANES, RT), lambda t: (t, 0, 0))
    idx_t, w_t, rank_t, counts, hp = pl.pallas_call(
        _route_kernel,
        grid=(n_steps,),
        in_specs=[pl.BlockSpec((TM, D), lambda t: (tile(2 * t), 0)),
                  pl.BlockSpec((TM, D), lambda t: (tile(2 * t + 1), 0)),
                  const((N_EXPERTS, D)), const((N_EXPERTS, LANES))],
        out_specs=[per_tok, per_tok, per_tok,
                   pl.BlockSpec((N_EXPERTS, LANES), lambda t: (0, 0)),
                   pl.BlockSpec((RT, D // 2), lambda t: (t, 0))],
        out_shape=[jax.ShapeDtypeStruct((n_steps, SUBLANES, RT), jnp.int32),
                   jax.ShapeDtypeStruct((n_steps, SUBLANES, RT), F32),
                   jax.ShapeDtypeStruct((n_steps, SUBLANES, RT), jnp.int32),
                   jax.ShapeDtypeStruct((N_EXPERTS, LANES), F32),
                   jax.ShapeDtypeStruct((n_act, D // 2), jnp.uint32)],
        scratch_shapes=[pltpu.VMEM((N_EXPERTS, LANES), F32)],
        compiler_params=_cparams(1),
        name="route",
    )(h2, h2, router_t, bias_t)
    token_major = lambda a: a.transpose(0, 2, 1).reshape(n_act, SUBLANES)
    return (token_major(idx_t)[:, :TOP_K], token_major(w_t), token_major(rank_t)[:, :TOP_K], counts[:, 0], hp)


def _shared_expert_kernel(h_ref, sg_ref, su_ref, sd_ref, o_ref):
    lo, hi = _unpack_bf16_pairs(h_ref[...])
    hb = jnp.concatenate([lo, hi], axis=1).astype(BF16)
    a = _silu(_dot(hb, sg_ref[...])) * _dot(hb, su_ref[...])
    o_ref[...] = _dot(a.astype(BF16), sd_ref[...]).astype(BF16)


def _shared_expert(hp, sg, su, sd):
    const = lambda shape: pl.BlockSpec(shape, lambda t: (0,) * len(shape))
    n_act = hp.shape[0]
    rows = math.gcd(n_act, SHARED_ROWS)
    return pl.pallas_call(
        _shared_expert_kernel,
        grid=(n_act // rows,),
        in_specs=[pl.BlockSpec((rows, D // 2), lambda t: (t, 0)),
                  const((D, D_EXPERT)), const((D, D_EXPERT)), const((D_EXPERT, D))],
        out_specs=pl.BlockSpec((rows, D), lambda t: (t, 0)),
        out_shape=jax.ShapeDtypeStruct((n_act, D), BF16),
        compiler_params=_cparams(1),
        name="shared_expert",
    )(hp, sg, su, sd)


def _dispatch(pos_sc, hp, n_rows):
    n_batches = pos_sc.shape[0]
    n_workers = SC_CORES * SC_SUBCORES
    mesh = plsc.VectorSubcoreMesh(core_axis_name="c", subcore_axis_name="s")

    @functools.partial(
        pl.kernel, mesh=mesh,
        out_type=jax.ShapeDtypeStruct((n_rows, D // 2), jnp.uint32),
        scratch_types=[pltpu.VMEM((TOP_K, SC_ROWS), jnp.int32),
                       pltpu.VMEM((SC_ROWS, D // 2), jnp.uint32),
                       pltpu.SemaphoreType.DMA],
        name="moe_dispatch")
    def scatter(pos_hbm, h_hbm, xs_hbm, idx_v, rows_v, sem):
        worker = lax.axis_index("s") * SC_CORES + lax.axis_index("c")

        @pl.loop(0, pl.cdiv(n_batches, n_workers))
        def _(j):
            b = j * n_workers + worker

            @pl.when(b < n_batches)
            def _():
                pltpu.sync_copy(pos_hbm.at[b], idx_v)
                pltpu.sync_copy(h_hbm.at[pl.ds(b * SC_ROWS, SC_ROWS)], rows_v)
                copies = [pltpu.async_copy(rows_v, xs_hbm.at[idx_v.at[k]], sem) for k in range(TOP_K)]
                for cp in copies:
                    cp.wait()

    return scatter(pos_sc, hp)


def _expert_kernel(blk_e_ref, valid_ref, x_ref, wg_ref, wu_ref, wd_ref, y_ref, wg_s, wu_s, wd_s):
    i = pl.program_id(0)
    n_valid = valid_ref[i]

    @pl.when(n_valid > 0)
    def _():
        @pl.when(jnp.logical_or(i == 0, blk_e_ref[i] != blk_e_ref[jnp.maximum(i - 1, 0)]))
        def _():
            wg_s[...] = wg_ref[...].astype(BF16)
            wu_s[...] = wu_ref[...].astype(BF16)
            wd_s[...] = wd_ref[...].astype(BF16)

        row = lax.broadcasted_iota(jnp.int32, (MOE_ROWS, 1), 0)
        lo, hi = _unpack_bf16_pairs(jnp.where(row < n_valid, x_ref[...], jnp.uint32(0)))
        x = jnp.concatenate([lo, hi], axis=1).astype(BF16)
        hb = _silu(_dot(x, wg_s[...])) * _dot(x, wu_s[...])
        y_ref[...] = _pack_bf16_pairs(_dot(hb.astype(BF16), wd_s[...]))

    @pl.when(n_valid == 0)
    def _():
        y_ref[...] = jnp.zeros_like(y_ref)


def _experts(blk_e, valid, xs, wg, wu, wd, layer, n_blocks):
    grid_spec = pltpu.PrefetchScalarGridSpec(
        num_scalar_prefetch=2,
        grid=(n_blocks,),
        in_specs=[pl.BlockSpec((MOE_ROWS, D // 2), lambda i, be, nv: (i, 0)),
                  pl.BlockSpec((None, None, D, D_EXPERT), lambda i, be, nv: (layer, be[i], 0, 0)),
                  pl.BlockSpec((None, None, D, D_EXPERT), lambda i, be, nv: (layer, be[i], 0, 0)),
                  pl.BlockSpec((None, None, D_EXPERT, D), lambda i, be, nv: (layer, be[i], 0, 0))],
        out_specs=pl.BlockSpec((MOE_ROWS, D // 2), lambda i, be, nv: (i, 0)),
        scratch_shapes=[pltpu.VMEM((D, D_EXPERT), BF16), pltpu.VMEM((D, D_EXPERT), BF16),
                        pltpu.VMEM((D_EXPERT, D), BF16)],
    )
    return pl.pallas_call(
        _expert_kernel,
        grid_spec=grid_spec,
        out_shape=jax.ShapeDtypeStruct((n_blocks * MOE_ROWS, D // 2), jnp.uint32),
        compiler_params=_cparams(1),
        name="routed_experts",
    )(blk_e, valid, xs, wg, wu, wd)


def _gather_rows(pos_sc, ys):
    n_batches = pos_sc.shape[0]
    n_workers = SC_CORES * SC_SUBCORES
    half = SC_ROWS // 2
    mesh = plsc.VectorSubcoreMesh(core_axis_name="c", subcore_axis_name="s")

    @functools.partial(
        pl.kernel, mesh=mesh,
        out_type=jax.ShapeDtypeStruct((TOP_K, n_batches * SC_ROWS, D // 2), jnp.uint32),
        scratch_types=[pltpu.VMEM((TOP_K, SC_ROWS), jnp.int32),
                       pltpu.VMEM((2, half, D // 2), jnp.uint32),
                       pltpu.SemaphoreType.DMA, pltpu.SemaphoreType.DMA],
        name="moe_gather")
    def gather(pos_hbm, y_hbm, out_hbm, idx_v, bufs, gsem, wsem):
        worker = lax.axis_index("s") * SC_CORES + lax.axis_index("c")

        @pl.loop(0, pl.cdiv(n_batches, n_workers))
        def _(j):
            b = j * n_workers + worker

            @pl.when(b < n_batches)
            def _():
                pltpu.sync_copy(pos_hbm.at[b], idx_v)
                items = [(k, h) for k in range(TOP_K) for h in range(2)]

                def fetch(i):
                    k, h = items[i]
                    return pltpu.async_copy(y_hbm.at[idx_v.at[k, pl.ds(h * half, half)]], bufs.at[i % 2], gsem)

                pending_gather = fetch(0)
                pending_write = None
                for i, (k, h) in enumerate(items):
                    pending_gather.wait()
                    if pending_write is not None:
                        pending_write.wait()
                    if i + 1 < len(items):
                        pending_gather = fetch(i + 1)
                    pending_write = pltpu.async_copy(
                        bufs.at[i % 2], out_hbm.at[k, pl.ds(b * SC_ROWS + h * half, half)], wsem)
                pending_write.wait()

    return gather(pos_sc, ys)


def _combine_tile(y_refs, w_ref, x1_ref, sh_ref, mod_ref):
    f_lo = sh_ref[:, :D // 2].astype(F32)
    f_hi = sh_ref[:, D // 2:].astype(F32)
    for k in range(TOP_K):
        lo, hi = _unpack_bf16_pairs(y_refs[k][...])
        f_lo = f_lo + lo * w_ref[:, k:k + 1]
        f_hi = f_hi + hi * w_ref[:, k:k + 1]
    return x1_ref[...] + mod_ref[:, 5 * D:6 * D] * jnp.concatenate([f_lo, f_hi], axis=1)


def _combine_kernel(*refs):
    o_ref = refs[-1]
    o_ref[...] = _combine_tile(refs[:TOP_K], *refs[TOP_K:-1])


def _combine_specs(tile, mod_row):
    planes = [pl.BlockSpec((None, TM, D // 2), lambda t, k=k: (k, t, 0)) for k in range(TOP_K)]
    return planes + [pl.BlockSpec((TM, SUBLANES), lambda t: (t, 0)),
                     pl.BlockSpec((TM, D), lambda t: (tile(t), 0)),
                     pl.BlockSpec((TM, D), lambda t: (t, 0)),
                     pl.BlockSpec((None, 1, 6 * D), lambda t: (mod_row(tile(t)), 0, 0))]


def _combine(yg, w, x1, shared, mod3, n_tiles, tile, mod_row, out_rows, out_tile):
    return pl.pallas_call(
        _combine_kernel,
        grid=(n_tiles,),
        in_specs=_combine_specs(tile, mod_row),
        out_specs=pl.BlockSpec((TM, D), lambda t: (out_tile(t), 0)),
        out_shape=jax.ShapeDtypeStruct((out_rows, D), F32),
        compiler_params=_cparams(1),
        name="moe_combine",
    )(*([yg] * TOP_K), w, x1, shared, mod3)


def _combine_in_proj_kernel(*refs):
    xs_ref, o_ref, h_scr = refs[-3:]
    mod_ref, g_ref, w_ref = refs[TOP_K + 4:-3]
    x = _combine_tile(refs[:TOP_K], *refs[TOP_K:TOP_K + 4])
    xs_ref[...] = x
    _project_in(x, mod_ref, g_ref, w_ref, o_ref, h_scr)


def _combine_in_proj(yg, w, x1, shared, mod3_prev, mod3, g, w_in_r, n_tiles, mod_row):
    n_rows = x1.shape[0]
    ident = lambda t: t
    return pl.pallas_call(
        _combine_in_proj_kernel,
        grid=(n_tiles,),
        in_specs=_combine_specs(ident, mod_row) + [
            pl.BlockSpec((None, 1, 6 * D), lambda t: (mod_row(t), 0, 0)),
            pl.BlockSpec((1, D), lambda t: (0, 0)),
            pl.BlockSpec((D, N_IN_PAD), lambda t: (0, 0), pipeline_mode=pl.Buffered(1))],
        out_specs=[pl.BlockSpec((TM, D), lambda t: (t, 0)),
                   pl.BlockSpec((TM, N_IN_PAD), lambda t: (t, 0))],
        out_shape=[jax.ShapeDtypeStruct((n_rows, D), F32),
                   jax.ShapeDtypeStruct((n_rows, N_IN_PAD), BF16)],
        scratch_shapes=[pltpu.VMEM((TM, D), BF16)],
        compiler_params=_cparams(1),
        name="combine_in_proj",
    )(*([yg] * TOP_K), w, x1, shared, mod3_prev, mod3, g, w_in_r)


def _moe_plan(idx, rank, counts, n_blocks):
    n = idx.shape[0]
    cnt = counts.reshape(N_EXPERTS).astype(jnp.int32)
    padded = (cnt + MOE_ROWS - 1) // MOE_ROWS * MOE_ROWS
    pad_end = jnp.cumsum(padded)
    pad_start = pad_end - padded
    experts = jnp.arange(N_EXPERTS, dtype=jnp.int32)
    pos = rank + jnp.sum(jnp.where(idx[:, :, None] == experts, pad_start, 0), axis=-1)
    blk_start = jnp.arange(n_blocks, dtype=jnp.int32) * MOE_ROWS
    blk_e = jnp.minimum(jnp.sum(blk_start[:, None] >= pad_end[None, :], axis=1), N_EXPERTS - 1).astype(jnp.int32)
    mine = blk_e[:, None] == experts
    in_expert = blk_start - jnp.sum(jnp.where(mine, pad_start, 0), axis=1)
    valid = jnp.clip(jnp.sum(jnp.where(mine, cnt, 0), axis=1) - in_expert, 0, MOE_ROWS).astype(jnp.int32)
    pos_sc = pos.astype(jnp.int32).reshape(n // SC_ROWS, SC_ROWS, TOP_K).transpose(0, 2, 1)
    return blk_e, valid, pos_sc


def _rope_tables(seq, ctx):
    half = MLA_ROPE // 2
    n_freq = half // 2
    inv = ROPE_THETA ** (-2.0 * jnp.arange(n_freq, dtype=F32) / half)
    t = jnp.arange(seq)
    ang_r = (t // GRID_W).astype(F32)[:, None] * inv
    ang_c = (t % GRID_W).astype(F32)[:, None] * inv
    cos = jnp.concatenate([jnp.cos(ang_r), jnp.cos(ang_r), jnp.cos(ang_c), jnp.cos(ang_c)], axis=1)
    sin = jnp.concatenate([-jnp.sin(ang_r), jnp.sin(ang_r), -jnp.sin(ang_c), jnp.sin(ang_c)], axis=1)
    pad_l = MLA_NOPE
    pad_r = HEAD_PAD - MLA_NOPE - MLA_ROPE
    cos = jnp.pad(cos, ((0, ctx), (pad_l, pad_r)), constant_values=1.0)
    cos = cos.at[seq:, :].set(1.0)
    sin = jnp.pad(sin, ((0, ctx), (pad_l, pad_r)))
    lane = jnp.arange(HEAD_PAD)
    partner = jnp.where(lane % 16 < 8, lane + 8, lane - 8)
    swap = (lane[:, None] == partner[None, :]).astype(BF16)
    return cos, sin, swap


def _pad_heads(w, n_heads, width, offset=0):
    k = w.shape[0]
    w = w.reshape(k, n_heads, width)
    w = jnp.pad(w, ((0, 0), (0, 0), (offset, HEAD_PAD - width - offset)))
    return w.reshape(k, n_heads * HEAD_PAD)


def _pad_vec(g, offset):
    return jnp.pad(g, (offset, HEAD_PAD - g.shape[0] - offset)).reshape(1, HEAD_PAD)


def _reorder_w_in(w):
    off_cq, off_ckv, off_kr, off_uv, off_merge = 2048, 2432, 2688, 2720, 3744
    kr = jnp.pad(w[:, off_kr:off_uv], ((0, 0), (MLA_NOPE, HEAD_PAD - MLA_NOPE - MLA_ROPE)))
    return jnp.concatenate([w[:, off_merge:], w[:, off_uv:off_merge], w[:, :off_cq],
                            w[:, off_ckv:off_kr], kr, w[:, off_cq:off_ckv]], axis=1).astype(BF16)


def kernel(x, c, ctx, c_ctx, ada_w, ada_b, norm1_g, norm2_g, w_in, ret_decay_fwd, ret_decay_bwd, ret_gn_g,
           ret_gn_b, w_br_ret, mla_qa_g, mla_w_uq, mla_kva_g, mla_w_ukv, mla_qn_g, mla_kn_g, mla_kr_g, w_br_mla,
           gmlp_ln_g, gmlp_ln_b, gmlp_ws, gmlp_bs, w_br_gmlp, w_out, moe_router, moe_bias, moe_w_gate, moe_w_up,
           moe_w_down, sh_w_gate, sh_w_up, sh_w_down):
    B, seq, _ = x.shape
    n_ctx = ctx.shape[1]
    depth = ada_w.shape[0]
    assert n_ctx == TM and seq % TM == 0 and B + 1 <= SUBLANES and TOP_K <= SUBLANES
    lt = seq + n_ctx
    tiles_per_b = lt // TM
    lat_tiles_per_b = seq // TM
    ctx_tile = lat_tiles_per_b

    def mod_row(t):
        return jnp.where(t % tiles_per_b == ctx_tile, B, t // tiles_per_b)

    c_rows = jnp.concatenate([c, c_ctx[None, :], jnp.zeros((SUBLANES - B - 1, D), F32)], axis=0)
    mod = _ada(c_rows, ada_w, ada_b)
    cos_t, sin_t, swap = _rope_tables(seq, n_ctx)
    x_parts = (x.reshape(B * seq, D), ctx.reshape(B * n_ctx, D))

    pending = None
    for l in range(depth):
        last = l == depth - 1
        mod3 = mod[l].reshape(SUBLANES, 1, 6 * D)
        if pending is None:
            p = _in_proj(x_parts, mod3, norm1_g[l].reshape(1, D), _reorder_w_in(w_in[l]), B * tiles_per_b,
                         tiles_per_b, mod_row)
        else:
            xs, p = _combine_in_proj(*pending, mod3, norm1_g[l].reshape(1, D), _reorder_w_in(w_in[l]),
                                     B * tiles_per_b, mod_row)
            x_parts = (xs,)

        lg = jnp.stack([jax.nn.log_sigmoid(ret_decay_fwd[l].astype(F32)),
                        jax.nn.log_sigmoid(ret_decay_bwd[l].astype(F32))])
        y_ret = _retention(p.reshape(B, lt, N_IN_PAD), lg, ret_gn_g[l].reshape(1, -1), ret_gn_b[l].reshape(1, -1),
                           seq, n_ctx)

        w_ukv = mla_w_ukv[l].reshape(MLA_KV_LORA, MLA_HEADS, MLA_NOPE + MLA_V)
        wk_p = _pad_heads(w_ukv[:, :, :MLA_NOPE].reshape(MLA_KV_LORA, -1), MLA_HEADS, MLA_NOPE).astype(BF16)
        wv = w_ukv[:, :, MLA_NOPE:].reshape(MLA_KV_LORA, MLA_HEADS * MLA_V).T.astype(BF16)
        wq_p = _pad_heads(mla_w_uq[l], MLA_HEADS, MLA_QK).astype(BF16)
        q, k, v = _mla_prep(p, cos_t, sin_t, swap, mla_qa_g[l].reshape(1, -1), mla_kva_g[l].reshape(1, -1),
                            _pad_vec(mla_qn_g[l], 0), _pad_vec(mla_kn_g[l], 0), _pad_vec(mla_kr_g[l], MLA_NOPE),
                            wq_p, wk_p, wv, B, lt)
        o_mla = _attention(q, k, v, seq, n_ctx, lat_tiles_per_b if last else tiles_per_b)

        if last:
            n_tiles = B * lat_tiles_per_b
            tile = lambda t: (t // lat_tiles_per_b) * tiles_per_b + t % lat_tiles_per_b
        else:
            n_tiles = B * tiles_per_b
            tile = lambda t: t
        bs_full = jnp.broadcast_to(gmlp_bs[l][:, :, None], (GMLP_GROUPS, GMLP_CHUNK, GMLP_CHUNK))
        x1, h2 = _merge(x_parts, mod3, p, y_ret.reshape(B * lt, -1), o_mla.reshape(B * lt, -1),
                        gmlp_ln_g[l].reshape(1, -1), gmlp_ln_b[l].reshape(1, -1), gmlp_ws[l].astype(BF16), bs_full,
                        w_br_ret[l].astype(BF16), w_br_mla[l].astype(BF16), w_br_gmlp[l].astype(BF16),
                        w_out[l].astype(BF16), norm2_g[l].reshape(1, D), n_tiles, tile, tiles_per_b, mod_row)

        bias_t = jnp.broadcast_to(moe_bias[l][:, None], (N_EXPERTS, LANES))
        idx, w, rank, counts, hp = _route(h2, moe_router[l].T, bias_t, n_tiles, tile)
        n_act = n_tiles * TM
        n_blocks = -(-(n_act * TOP_K + N_EXPERTS * (MOE_ROWS - 1)) // MOE_ROWS)
        blk_e, valid, pos_sc = _moe_plan(idx, rank, counts, n_blocks)
        xg = _dispatch(pos_sc, hp, n_blocks * MOE_ROWS)
        shared = _shared_expert(hp, sh_w_gate[l].astype(BF16), sh_w_up[l].astype(BF16), sh_w_down[l].astype(BF16))
        ys = _experts(blk_e, valid, xg, moe_w_gate, moe_w_up, moe_w_down, l, n_blocks)
        yg = _gather_rows(pos_sc, ys)
        if last:
            xs = _combine(yg, w, x1, shared, mod3, n_tiles, tile, mod_row, B * seq, lambda t: t)
        else:
            pending = (yg, w, x1, shared, mod3)
    return xs.reshape(B, seq, D)
```

```python
import functools
import math

import jax
import jax.numpy as jnp
from jax import lax
from jax.experimental import pallas as pl
from jax.experimental.pallas import tpu as pltpu
from jax.experimental.pallas import tpu_sc as plsc

F32 = jnp.float32
BF16 = jnp.bfloat16

D = 1024
GRID_W = 64
RET_HEADS = 4
RET_D = 128
RET_CHUNK = 256
RET_OUT_ROWS = 256
MLA_HEADS = 8
MLA_Q_LORA = 384
MLA_KV_LORA = 256
MLA_NOPE = 64
MLA_ROPE = 32
MLA_V = 64
MLA_V_EXT = MLA_V + 16
MLA_QK = MLA_NOPE + MLA_ROPE
HEAD_PAD = 128
ROPE_THETA = 10000.0
GMLP_GROUPS = 4
GMLP_W = 512
GMLP_CHUNK = 128
N_EXPERTS = 64
TOP_K = 6
D_EXPERT = 256
ROUTED_SCALE = 2.5
EPS = 1e-6
LOG2_E = 1.4426950408889634

LANES = 128
SUBLANES = 8
TM = 256
MOE_ROWS = 512
RT = 2 * TM
SHARED_ROWS = 1024
ATT_KV_CHUNK = 1024
ATT_HEADS = 4

C_MERGE = 0
C_UV = 3072
C_RET = 4096
C_CKV = 6144
C_KR = 6400
C_CQ = 6528
N_IN_PAD = 6912
IN_CHUNK = 768
ADA_CHUNK = 1536

VMEM_LIMIT = 56 * 1024 * 1024

SC_CORES = 2
SC_SUBCORES = 16
SC_ROWS = 128


def _cparams(n_axes, vmem=VMEM_LIMIT):
    return pltpu.CompilerParams(dimension_semantics=("arbitrary",) * n_axes, vmem_limit_bytes=vmem)


def _silu(x):
    return x * jax.nn.sigmoid(x)


def _dot(a, b):
    return jnp.dot(a, b, preferred_element_type=F32)


def _dot_nt(a, b):
    return lax.dot_general(a, b, (((1,), (1,)), ((), ())), preferred_element_type=F32)


def _dot_tn(a, b):
    return lax.dot_general(a, b, (((0,), (0,)), ((), ())), preferred_element_type=F32)


def _pack_bf16_pairs(x):
    n = x.shape[1] // 2
    lo = lax.bitcast_convert_type(x[:, :n].astype(BF16).astype(F32), jnp.uint32)
    hi = lax.bitcast_convert_type(x[:, n:].astype(BF16).astype(F32), jnp.uint32)
    return (lo >> 16) | hi


def _unpack_bf16_pairs(u):
    lo = lax.bitcast_convert_type(u << 16, F32)
    hi = lax.bitcast_convert_type(u & jnp.uint32(0xFFFF0000), F32)
    return lo, hi


def _ada_kernel(c_ref, w_ref, b_ref, o_ref):
    s = _silu(c_ref[...])
    o_ref[...] = _dot(s.astype(BF16), w_ref[...].astype(BF16)) + b_ref[...]


def _ada(c_rows, ada_w, ada_b):
    depth = ada_w.shape[0]
    n = ada_w.shape[2]
    cw = ADA_CHUNK
    return pl.pallas_call(
        _ada_kernel,
        grid=(depth, n // cw),
        in_specs=[pl.BlockSpec((SUBLANES, D), lambda l, j: (0, 0)),
                  pl.BlockSpec((None, D, cw), lambda l, j: (l, 0, j)),
                  pl.BlockSpec((None, 1, cw), lambda l, j: (l, 0, j))],
        out_specs=pl.BlockSpec((None, SUBLANES, cw), lambda l, j: (l, 0, j)),
        out_shape=jax.ShapeDtypeStruct((depth, SUBLANES, n), F32),
        compiler_params=_cparams(2),
        name="ada_mod",
    )(c_rows, ada_w, ada_b.reshape(depth, 1, n))


def _modulated_rmsnorm(x, g, shift, scale):
    y = x * lax.rsqrt(jnp.mean(x * x, axis=-1, keepdims=True) + EPS) * g
    return y * (1.0 + scale) + shift


def _stream_specs(x_parts, tile, tiles_per_b):
    if len(x_parts) == 1:
        return [pl.BlockSpec((TM, D), lambda t: (tile(t), 0))]
    lat_tiles = tiles_per_b - 1

    def latent(t):
        s = tile(t)
        return ((s // tiles_per_b) * lat_tiles + jnp.minimum(s % tiles_per_b, lat_tiles - 1), 0)

    return [pl.BlockSpec((TM, D), latent), pl.BlockSpec((TM, D), lambda t: (tile(t) // tiles_per_b, 0))]


def _read_stream(x_refs, tile, tiles_per_b):
    if len(x_refs) == 1:
        return x_refs[0][...]
    is_ctx = tile(pl.program_id(0)) % tiles_per_b == tiles_per_b - 1
    return jnp.where(is_ctx, x_refs[1][...], x_refs[0][...])


def _project_in(x, mod_ref, g_ref, w_ref, o_ref, h_scr):
    h = _modulated_rmsnorm(x, g_ref[...], mod_ref[:, 0:D], mod_ref[:, D:2 * D])
    h_scr[...] = h.astype(BF16)
    for c in range(N_IN_PAD // IN_CHUNK):
        cols = slice(c * IN_CHUNK, (c + 1) * IN_CHUNK)
        o_ref[:, cols] = _dot(h_scr[...], w_ref[:, cols]).astype(BF16)


def _in_proj_kernel(*refs, n_x, tiles_per_b):
    mod_ref, g_ref, w_ref, o_ref, h_scr = refs[n_x:]
    _project_in(_read_stream(refs[:n_x], lambda t: t, tiles_per_b), mod_ref, g_ref, w_ref, o_ref, h_scr)


def _in_proj(x_parts, mod3, g, w_in_r, n_tiles, tiles_per_b, mod_row):
    n_rows = n_tiles * TM
    return pl.pallas_call(
        functools.partial(_in_proj_kernel, n_x=len(x_parts), tiles_per_b=tiles_per_b),
        grid=(n_tiles,),
        in_specs=_stream_specs(x_parts, lambda t: t, tiles_per_b) + [
                  pl.BlockSpec((None, 1, 6 * D), lambda t: (mod_row(t), 0, 0)),
                  pl.BlockSpec((1, D), lambda t: (0, 0)),
                  pl.BlockSpec((D, N_IN_PAD), lambda t: (0, 0), pipeline_mode=pl.Buffered(1))],
        out_specs=pl.BlockSpec((TM, N_IN_PAD), lambda t: (t, 0)),
        out_shape=jax.ShapeDtypeStruct((n_rows, N_IN_PAD), BF16),
        scratch_shapes=[pltpu.VMEM((TM, D), BF16)],
        compiler_params=_cparams(1),
        name="in_proj",
    )(*x_parts, mod3, g, w_in_r)


def _retention_kernel(lg_ref, q_ref, k_ref, v_ref, g_ref, gng_ref, gnb_ref, y_ref, of_scr, ob_scr,
                      *, n_lat_chunks, n_ctx_chunks):
    h = pl.program_id(1)
    lg_f = lg_ref[0, h]
    lg_b = lg_ref[1, h]
    C = RET_CHUNK
    k_scale = RET_D ** -0.5
    ri = lax.broadcasted_iota(jnp.int32, (C, C), 0).astype(F32)
    ci = lax.broadcasted_iota(jnp.int32, (C, C), 1).astype(F32)
    pos = lax.broadcasted_iota(jnp.int32, (C, 1), 0).astype(F32)
    diff = ri - ci
    d_f = jnp.where(diff >= 0, jnp.exp(lg_f * jnp.maximum(diff, 0.0)), 0.0) * k_scale
    d_b = jnp.where(diff < 0, jnp.exp(lg_b * jnp.maximum(-diff, 0.0)), 0.0) * k_scale
    qdec_f = jnp.exp(lg_f * (pos + 1.0))
    kdec_f = jnp.exp(lg_f * (C - 1.0 - pos)) * k_scale
    cdec_f = jnp.exp(lg_f * C)
    qdec_b = jnp.exp(lg_b * (C - pos))
    kdec_b = jnp.exp(lg_b * pos) * k_scale
    cdec_b = jnp.exp(lg_b * C)

    d_both = d_f + d_b

    def chunk(c, state, qdec, kdec, cdec, with_intra):
        rows = pl.ds(pl.multiple_of(c * C, C), C)
        q = q_ref[rows, :]
        k = k_ref[rows, :]
        v = v_ref[rows, :]
        o = _dot((q.astype(F32) * qdec).astype(BF16), state.astype(BF16))
        if with_intra:
            o = o + _dot((_dot_nt(q, k) * d_both).astype(BF16), v)
        kd = (k.astype(F32) * kdec).astype(BF16)
        return rows, o, state * cdec + _dot_tn(kd, v)

    n_all = n_lat_chunks + n_ctx_chunks

    def scan_body(i, states):
        s_f, s_b = states
        c_f = jnp.where(i < n_ctx_chunks, n_lat_chunks + i, i - n_ctx_chunks)
        rows, o, s_f = chunk(c_f, s_f, qdec_f, kdec_f, cdec_f, True)
        of_scr[rows, :] = o
        rows, o, s_b = chunk(n_all - 1 - i, s_b, qdec_b, kdec_b, cdec_b, False)
        ob_scr[rows, :] = o
        return s_f, s_b

    zero = jnp.zeros((RET_D, RET_D), F32)
    lax.fori_loop(0, n_all, scan_body, (zero, zero), unroll=3)

    def out_body(c, _):
        rows = pl.ds(pl.multiple_of(c * RET_OUT_ROWS, RET_OUT_ROWS), RET_OUT_ROWS)
        o = of_scr[rows, :] + ob_scr[rows, :]
        mu = jnp.mean(o, axis=-1, keepdims=True)
        var = jnp.mean(jnp.square(o - mu), axis=-1, keepdims=True)
        on = (o - mu) * lax.rsqrt(var + EPS)
        y = _silu(g_ref[rows, :].astype(F32)) * (on * gng_ref[...] + gnb_ref[...])
        y_ref[rows, :] = y.astype(BF16)
        return 0

    lax.fori_loop(0, n_all * C // RET_OUT_ROWS, out_body, 0, unroll=3)


def _retention(p3, lg, gn_g, gn_b, seq, ctx):
    B, lt, _ = p3.shape
    base = C_RET // RET_D
    kern = functools.partial(_retention_kernel, n_lat_chunks=seq // RET_CHUNK, n_ctx_chunks=ctx // RET_CHUNK)

    def col(off):
        return pl.BlockSpec((None, lt, RET_D), lambda b, h: (b, 0, base + off * RET_HEADS + h))

    return pl.pallas_call(
        kern,
        grid=(B, RET_HEADS),
        in_specs=[pl.BlockSpec(memory_space=pltpu.SMEM),
                  col(0), col(1), col(2), col(3),
                  pl.BlockSpec((1, RET_D), lambda b, h: (0, h)),
                  pl.BlockSpec((1, RET_D), lambda b, h: (0, h))],
        out_specs=pl.BlockSpec((None, lt, RET_D), lambda b, h: (b, 0, h)),
        out_shape=jax.ShapeDtypeStruct((B, lt, RET_HEADS * RET_D), BF16),
        scratch_shapes=[pltpu.VMEM((lt, RET_D), F32), pltpu.VMEM((lt, RET_D), F32)],
        compiler_params=_cparams(2),
        name="retention",
    )(lg, p3, p3, p3, p3, gn_g, gn_b)


def _mla_prep_kernel(cq_ref, ckv_ref, kr_ref, cos_ref, sin_ref, swap_ref, qa_ref, kva_ref, qn_ref, kn_ref, krg_ref,
                     wq_ref, wk_ref, wv_ref, q_ref, k_ref, v_ref):
    cos = cos_ref[...]
    sin = sin_ref[...]

    def rms(x, n):
        return x * lax.rsqrt(jnp.sum(x * x, axis=-1, keepdims=True) * (1.0 / n) + EPS)

    def rope(x):
        return x * cos + _dot(x.astype(BF16), swap_ref[...]) * sin

    cq = cq_ref[...].astype(F32)
    cqn = (rms(cq, MLA_Q_LORA) * qa_ref[...]).astype(BF16)
    q_all = _dot(cqn, wq_ref[...])
    ckv = ckv_ref[...].astype(F32)
    ckvn = (rms(ckv, MLA_KV_LORA) * kva_ref[...]).astype(BF16)
    k_all = _dot(ckvn, wk_ref[...])
    k_rope = rope(rms(kr_ref[...].astype(F32), MLA_ROPE) * krg_ref[...])
    scale = MLA_QK ** -0.5 * LOG2_E
    v_t = _dot_nt(wv_ref[...], ckvn)
    ones_row = jnp.where(lax.broadcasted_iota(jnp.int32, (MLA_V_EXT - MLA_V, TM), 0) == 0, 1.0, 0.0)
    for h in range(MLA_HEADS):
        cols = slice(h * HEAD_PAD, (h + 1) * HEAD_PAD)
        qh = rope(rms(q_all[:, cols], MLA_QK) * qn_ref[...]) * scale
        q_ref[h] = qh.astype(BF16)
        kh = rms(k_all[:, cols], MLA_NOPE) * kn_ref[...] + k_rope
        k_ref[h] = kh.astype(BF16)
        v_ref[h] = jnp.concatenate([v_t[h * MLA_V:(h + 1) * MLA_V, :], ones_row], axis=0).astype(BF16)


def _mla_prep(p, cos_t, sin_t, swap, qa_g, kva_g, qn_p, kn_p, kr_p, wq_p, wk_p, wv, B, lt):
    tiles_per_b = lt // TM
    hw = MLA_HEADS * HEAD_PAD
    const = lambda shape: pl.BlockSpec(shape, lambda b, j: (0,) * len(shape))
    head_out = pl.BlockSpec((None, MLA_HEADS, TM, HEAD_PAD), lambda b, j: (b, 0, j, 0))
    shp = jax.ShapeDtypeStruct((B, MLA_HEADS, lt, HEAD_PAD), BF16)
    v_out = pl.BlockSpec((None, MLA_HEADS, None, MLA_V_EXT, TM), lambda b, j: (b, 0, j, 0, 0))
    v_shp = jax.ShapeDtypeStruct((B, MLA_HEADS, tiles_per_b, MLA_V_EXT, TM), BF16)
    return pl.pallas_call(
        _mla_prep_kernel,
        grid=(B, tiles_per_b),
        in_specs=[pl.BlockSpec((TM, MLA_Q_LORA), lambda b, j: (b * tiles_per_b + j, C_CQ // MLA_Q_LORA)),
                  pl.BlockSpec((TM, MLA_KV_LORA), lambda b, j: (b * tiles_per_b + j, C_CKV // MLA_KV_LORA)),
                  pl.BlockSpec((TM, HEAD_PAD), lambda b, j: (b * tiles_per_b + j, C_KR // HEAD_PAD)),
                  pl.BlockSpec((TM, HEAD_PAD), lambda b, j: (j, 0)),
                  pl.BlockSpec((TM, HEAD_PAD), lambda b, j: (j, 0)),
                  const((HEAD_PAD, HEAD_PAD)),
                  const((1, MLA_Q_LORA)), const((1, MLA_KV_LORA)),
                  const((1, HEAD_PAD)), const((1, HEAD_PAD)), const((1, HEAD_PAD)),
                  const((MLA_Q_LORA, hw)), const((MLA_KV_LORA, hw)), const((MLA_HEADS * MLA_V, MLA_KV_LORA))],
        out_specs=[head_out, head_out, v_out],
        out_shape=[shp, shp, v_shp],
        compiler_params=_cparams(2),
        name="mla_prep",
    )(p, p, p, cos_t, sin_t, swap, qa_g, kva_g, qn_p, kn_p, kr_p, wq_p, wk_p, wv)


def _attention_stages(seq, ctx):
    n_blk = ATT_KV_CHUNK // TM
    total = (seq + ctx) // TM
    first = n_blk + total % n_blk if total >= n_blk else total
    return [(total - first, first)] + [(c * n_blk, n_blk) for c in range((total - first) // n_blk)]


def _attention_kernel(q_ref, k_ref, v_ref, o_ref, s_scr, *, seq, ctx, ctx_tile):
    i = pl.program_id(2)

    def scores(hh, slot, blk, nb):
        s_scr[hh, slot, 0:nb * TM, :] = _dot_nt(k_ref[hh, blk * TM:(blk + nb) * TM, :], q_ref[hh])

    def absorb(hh, slot, blk, nb, carry):
        m, acc = carry
        s = s_scr[hh, slot, 0:nb * TM, :]
        m_new = jnp.maximum(m, jnp.max(s, axis=0, keepdims=True))
        p = jnp.exp2(s - m_new).astype(BF16)
        acc = jnp.exp2(m - m_new) * acc
        for j in range(nb):
            acc = acc + _dot(v_ref[hh, blk + j], p[j * TM:(j + 1) * TM, :])
        return m_new, acc

    def attend(stages):
        for hh in range(ATT_HEADS):
            scores(hh, 0, *stages[0])
        carries = [(jnp.full((1, TM), -jnp.inf, F32), jnp.zeros((MLA_V_EXT, TM), F32))] * ATT_HEADS
        for n, stage in enumerate(stages):
            for hh in range(ATT_HEADS):
                if n + 1 < len(stages):
                    scores(hh, (n + 1) % 2, *stages[n + 1])
                carries[hh] = absorb(hh, n % 2, *stage, carries[hh])
        outs = [acc[0:MLA_V, :] / acc[MLA_V:MLA_V + 1, :] for _, acc in carries]
        o_ref[...] = jnp.concatenate(outs, axis=0).T.astype(BF16)

    @pl.when(i != ctx_tile)
    def _():
        attend(_attention_stages(seq, ctx))

    @pl.when(i == ctx_tile)
    def _():
        attend([(seq // TM, ctx // TM)])


def _attention(q, k, v, seq, ctx, n_q_tiles):
    B, H, lt, _ = q.shape
    kern = functools.partial(_attention_kernel, seq=seq, ctx=ctx, ctx_tile=seq // TM)
    slot_rows = max(nb for _, nb in _attention_stages(seq, ctx)) * TM
    return pl.pallas_call(
        kern,
        grid=(B, H // ATT_HEADS, n_q_tiles),
        in_specs=[pl.BlockSpec((None, ATT_HEADS, TM, HEAD_PAD), lambda b, h, i: (b, h, i, 0)),
                  pl.BlockSpec((None, ATT_HEADS, lt, HEAD_PAD), lambda b, h, i: (b, h, 0, 0)),
                  pl.BlockSpec((None, ATT_HEADS, lt // TM, MLA_V_EXT, TM), lambda b, h, i: (b, h, 0, 0, 0))],
        out_specs=pl.BlockSpec((None, TM, ATT_HEADS * MLA_V), lambda b, h, i: (b, i, h)),
        out_shape=jax.ShapeDtypeStruct((B, lt, H * MLA_V), BF16),
        scratch_shapes=[pltpu.VMEM((ATT_HEADS, 2, slot_rows, TM), F32)],
        compiler_params=_cparams(3),
        name="attention",
    )(q, k, v)


def _merge_kernel(*refs, n_x, tile, tiles_per_b):
    (mod_ref, mg_ref, uv_ref, yr_ref, om_ref, lng_ref, lnb_ref, ws_ref, bs_ref,
     wr_ref, wm_ref, wg_ref, wo_ref, n2_ref, x1_ref, h2_ref) = refs[n_x:]
    yr = _dot(yr_ref[...], wr_ref[...])
    ym = _dot(om_ref[...], wm_ref[...])
    z = jax.nn.gelu(uv_ref[...].astype(F32))
    u = z[:, :GMLP_W]
    v = z[:, GMLP_W:]
    mu = jnp.mean(v, axis=-1, keepdims=True)
    var = jnp.mean(jnp.square(v - mu), axis=-1, keepdims=True)
    vn = ((v - mu) * lax.rsqrt(var + EPS) * lng_ref[...] + lnb_ref[...]).astype(BF16)
    gw = GMLP_W // GMLP_GROUPS
    chunks = []
    for c in range(TM // GMLP_CHUNK):
        rows = slice(c * GMLP_CHUNK, (c + 1) * GMLP_CHUNK)
        groups = [_dot(ws_ref[g], vn[rows, g * gw:(g + 1) * gw]) + bs_ref[g] for g in range(GMLP_GROUPS)]
        chunks.append(jnp.concatenate(groups, axis=1))
    sv = jnp.concatenate(chunks, axis=0)
    yg = _dot((u * sv).astype(BF16), wg_ref[...])
    gate = jax.nn.sigmoid(mg_ref[...].astype(F32))
    y = gate[:, :D] * yr + gate[:, D:2 * D] * ym + gate[:, 2 * D:] * yg
    out = _dot(y.astype(BF16), wo_ref[...])
    x1 = _read_stream(refs[:n_x], tile, tiles_per_b) + mod_ref[:, 2 * D:3 * D] * out
    x1_ref[...] = x1
    h2_ref[...] = _modulated_rmsnorm(x1, n2_ref[...], mod_ref[:, 3 * D:4 * D], mod_ref[:, 4 * D:5 * D])


def _merge(x_parts, mod3, p, y_ret, o_mla, ln_g, ln_b, ws, bs_full, w_br_ret, w_br_mla, w_br_gmlp, w_out, n2_g,
           n_tiles, tile, tiles_per_b, mod_row):
    n_rows = p.shape[0]
    const = lambda shape: pl.BlockSpec(shape, lambda t: (0,) * len(shape))
    row = lambda w, cb=0: pl.BlockSpec((TM, w), lambda t: (tile(t), cb))
    shp = jax.ShapeDtypeStruct((n_rows, D), F32)
    return pl.pallas_call(
        functools.partial(_merge_kernel, n_x=len(x_parts), tile=tile, tiles_per_b=tiles_per_b),
        grid=(n_tiles,),
        in_specs=_stream_specs(x_parts, tile, tiles_per_b) + [
                  pl.BlockSpec((None, 1, 6 * D), lambda t: (mod_row(tile(t)), 0, 0)),
                  row(3 * D, C_MERGE // (3 * D)), row(D, C_UV // D),
                  row(RET_HEADS * RET_D), row(MLA_HEADS * MLA_V),
                  const((1, GMLP_W)), const((1, GMLP_W)),
                  const((GMLP_GROUPS, GMLP_CHUNK, GMLP_CHUNK)), const((GMLP_GROUPS, GMLP_CHUNK, GMLP_CHUNK)),
                  const((RET_HEADS * RET_D, D)), const((MLA_HEADS * MLA_V, D)), const((GMLP_W, D)),
                  const((D, D)), const((1, D))],
        out_specs=[row(D), row(D)],
        out_shape=[shp, shp],
        compiler_params=_cparams(1),
        name="merge",
    )(*x_parts, mod3, p, p, y_ret, o_mla, ln_g, ln_b, ws, bs_full, w_br_ret, w_br_mla, w_br_gmlp, w_out, n2_g)


def _route_kernel(ha_ref, hb_ref, rt_ref, bt_ref, idx_ref, w_ref, rank_ref, cnt_ref, hp_ref, cnt_scr):
    @pl.when(pl.program_id(0) == 0)
    def _():
        cnt_scr[...] = jnp.zeros_like(cnt_scr)

    h = jnp.concatenate([ha_ref[...], hb_ref[...]], axis=0)
    logits = lax.dot_general(rt_ref[...], h, (((1,), (1,)), ((), ())), preferred_element_type=F32,
                             precision=lax.Precision.HIGHEST)
    scores = jax.nn.sigmoid(logits)
    sel = scores + bt_ref[:, 0:1]
    row_e = lax.broadcasted_iota(jnp.int32, (N_EXPERTS, RT), 0).astype(F32)
    row_o = lax.broadcasted_iota(jnp.int32, (SUBLANES, RT), 0)
    idx_out = jnp.zeros((SUBLANES, RT), F32)
    w_out = jnp.zeros((SUBLANES, RT), F32)
    hits = []
    for k in range(TOP_K):
        best = jnp.max(sel, axis=0, keepdims=True)
        pick = jnp.min(jnp.where(sel == best, row_e, float(N_EXPERTS)), axis=0, keepdims=True)
        hit = row_e == pick
        hits.append(hit)
        wk = jnp.sum(jnp.where(hit, scores, 0.0), axis=0, keepdims=True)
        sel = jnp.where(hit, -jnp.inf, sel)
        idx_out = jnp.where(row_o == k, pick, idx_out)
        w_out = jnp.where(row_o == k, wk, w_out)
    w_out = w_out / jnp.sum(w_out, axis=0, keepdims=True) * ROUTED_SCALE
    idx_ref[...] = idx_out.astype(jnp.int32)
    w_ref[...] = w_out
    chosen = jnp.zeros((N_EXPERTS, RT), F32)
    for hit in hits:
        chosen = jnp.where(hit, 1.0, chosen)
    earlier = (lax.broadcasted_iota(jnp.int32, (RT, RT), 0) < lax.broadcasted_iota(jnp.int32, (RT, RT), 1))
    before = _dot(chosen.astype(BF16), jnp.where(earlier, 1.0, 0.0).astype(BF16)) + cnt_scr[:, 0:1]
    rank_out = jnp.zeros((SUBLANES, RT), F32)
    for k, hit in enumerate(hits):
        rank_out = jnp.where(row_o == k, jnp.sum(jnp.where(hit, before, 0.0), axis=0, keepdims=True), rank_out)
    rank_ref[...] = rank_out.astype(jnp.int32)
    cnt_scr[...] += jnp.sum(chosen, axis=1, keepdims=True)
    cnt_ref[...] = cnt_scr[...]
    hp_ref[...] = _pack_bf16_pairs(h)


def _route(h2, router_t, bias_t, n_tiles, tile):
    const = lambda shape: pl.BlockSpec(shape, lambda t: (0,) * len(shape))
    n_act = n_tiles * TM
    assert n_tiles % 2 == 0
    n_steps = n_tiles // 2
    per_tok = pl.BlockSpec((None, SUBLANES, RT), lambda t: (t, 0, 0))
    idx_t, w_t, rank_t, counts, hp = pl.pallas_call(
        _route_kernel,
        grid=(n_steps,),
        in_specs=[pl.BlockSpec((TM, D), lambda t: (tile(2 * t), 0)),
                  pl.BlockSpec((TM, D), lambda t: (tile(2 * t + 1), 0)),
                  const((N_EXPERTS, D)), const((N_EXPERTS, LANES))],
        out_specs=[per_tok, per_tok, per_tok,
                   pl.BlockSpec((N_EXPERTS, LANES), lambda t: (0, 0)),
                   pl.BlockSpec((RT, D // 2), lambda t: (t, 0))],
        out_shape=[jax.ShapeDtypeStruct((n_steps, SUBLANES, RT), jnp.int32),
                   jax.ShapeDtypeStruct((n_steps, SUBLANES, RT), F32),
                   jax.ShapeDtypeStruct((n_steps, SUBLANES, RT), jnp.int32),
                   jax.ShapeDtypeStruct((N_EXPERTS, LANES), F32),
                   jax.ShapeDtypeStruct((n_act, D // 2), jnp.uint32)],
        scratch_shapes=[pltpu.VMEM((N_EXPERTS, LANES), F32)],
        compiler_params=_cparams(1),
        name="route",
    )(h2, h2, router_t, bias_t)
    token_major = lambda a: a.transpose(0, 2, 1).reshape(n_act, SUBLANES)
    return (token_major(idx_t)[:, :TOP_K], token_major(w_t), token_major(rank_t)[:, :TOP_K], counts[:, 0], hp)


def _shared_expert_kernel(h_ref, sg_ref, su_ref, sd_ref, o_ref):
    lo, hi = _unpack_bf16_pairs(h_ref[...])
    hb = jnp.concatenate([lo, hi], axis=1).astype(BF16)
    a = _silu(_dot(hb, sg_ref[...])) * _dot(hb, su_ref[...])
    o_ref[...] = _dot(a.astype(BF16), sd_ref[...]).astype(BF16)


def _shared_expert(hp, sg, su, sd):
    const = lambda shape: pl.BlockSpec(shape, lambda t: (0,) * len(shape))
    n_act = hp.shape[0]
    rows = math.gcd(n_act, SHARED_ROWS)
    return pl.pallas_call(
        _shared_expert_kernel,
        grid=(n_act // rows,),
        in_specs=[pl.BlockSpec((rows, D // 2), lambda t: (t, 0)),
                  const((D, D_EXPERT)), const((D, D_EXPERT)), const((D_EXPERT, D))],
        out_specs=pl.BlockSpec((rows, D), lambda t: (t, 0)),
        out_shape=jax.ShapeDtypeStruct((n_act, D), BF16),
        compiler_params=_cparams(1),
        name="shared_expert",
    )(hp, sg, su, sd)


def _dispatch(pos_sc, hp, n_rows):
    n_batches = pos_sc.shape[0]
    n_workers = SC_CORES * SC_SUBCORES
    mesh = plsc.VectorSubcoreMesh(core_axis_name="c", subcore_axis_name="s")

    @functools.partial(
        pl.kernel, mesh=mesh,
        out_type=jax.ShapeDtypeStruct((n_rows, D // 2), jnp.uint32),
        scratch_types=[pltpu.VMEM((TOP_K, SC_ROWS), jnp.int32),
                       pltpu.VMEM((SC_ROWS, D // 2), jnp.uint32),
                       pltpu.SemaphoreType.DMA],
        name="moe_dispatch")
    def scatter(pos_hbm, h_hbm, xs_hbm, idx_v, rows_v, sem):
        worker = lax.axis_index("s") * SC_CORES + lax.axis_index("c")

        @pl.loop(0, pl.cdiv(n_batches, n_workers))
        def _(j):
            b = j * n_workers + worker

            @pl.when(b < n_batches)
            def _():
                pltpu.sync_copy(pos_hbm.at[b], idx_v)
                pltpu.sync_copy(h_hbm.at[pl.ds(b * SC_ROWS, SC_ROWS)], rows_v)
                copies = [pltpu.async_copy(rows_v, xs_hbm.at[idx_v.at[k]], sem) for k in range(TOP_K)]
                for cp in copies:
                    cp.wait()

    return scatter(pos_sc, hp)


def _expert_kernel(blk_e_ref, valid_ref, x_ref, wg_ref, wu_ref, wd_ref, y_ref, wg_s, wu_s, wd_s):
    i = pl.program_id(0)
    n_valid = valid_ref[i]

    @pl.when(n_valid > 0)
    def _():
        @pl.when(jnp.logical_or(i == 0, blk_e_ref[i] != blk_e_ref[jnp.maximum(i - 1, 0)]))
        def _():
            wg_s[...] = wg_ref[...].astype(BF16)
            wu_s[...] = wu_ref[...].astype(BF16)
            wd_s[...] = wd_ref[...].astype(BF16)

        row = lax.broadcasted_iota(jnp.int32, (MOE_ROWS, 1), 0)
        lo, hi = _unpack_bf16_pairs(jnp.where(row < n_valid, x_ref[...], jnp.uint32(0)))
        x = jnp.concatenate([lo, hi], axis=1).astype(BF16)
        hb = _silu(_dot(x, wg_s[...])) * _dot(x, wu_s[...])
        y_ref[...] = _pack_bf16_pairs(_dot(hb.astype(BF16), wd_s[...]))

    @pl.when(n_valid == 0)
    def _():
        y_ref[...] = jnp.zeros_like(y_ref)


def _experts(blk_e, valid, xs, wg, wu, wd, layer, n_blocks):
    grid_spec = pltpu.PrefetchScalarGridSpec(
        num_scalar_prefetch=2,
        grid=(n_blocks,),
        in_specs=[pl.BlockSpec((MOE_ROWS, D // 2), lambda i, be, nv: (i, 0)),
                  pl.BlockSpec((None, None, D, D_EXPERT), lambda i, be, nv: (layer, be[i], 0, 0)),
                  pl.BlockSpec((None, None, D, D_EXPERT), lambda i, be, nv: (layer, be[i], 0, 0)),
                  pl.BlockSpec((None, None, D_EXPERT, D), lambda i, be, nv: (layer, be[i], 0, 0))],
        out_specs=pl.BlockSpec((MOE_ROWS, D // 2), lambda i, be, nv: (i, 0)),
        scratch_shapes=[pltpu.VMEM((D, D_EXPERT), BF16), pltpu.VMEM((D, D_EXPERT), BF16),
                        pltpu.VMEM((D_EXPERT, D), BF16)],
    )
    return pl.pallas_call(
        _expert_kernel,
        grid_spec=grid_spec,
        out_shape=jax.ShapeDtypeStruct((n_blocks * MOE_ROWS, D // 2), jnp.uint32),
        compiler_params=_cparams(1),
        name="routed_experts",
    )(blk_e, valid, xs, wg, wu, wd)


def _gather_rows(pos_sc, ys):
    n_batches = pos_sc.shape[0]
    n_workers = SC_CORES * SC_SUBCORES
    half = SC_ROWS // 2
    mesh = plsc.VectorSubcoreMesh(core_axis_name="c", subcore_axis_name="s")

    @functools.partial(
        pl.kernel, mesh=mesh,
        out_type=jax.ShapeDtypeStruct((TOP_K, n_batches * SC_ROWS, D // 2), jnp.uint32),
        scratch_types=[pltpu.VMEM((TOP_K, SC_ROWS), jnp.int32),
                       pltpu.VMEM((2, half, D // 2), jnp.uint32),
                       pltpu.SemaphoreType.DMA, pltpu.SemaphoreType.DMA],
        name="moe_gather")
    def gather(pos_hbm, y_hbm, out_hbm, idx_v, bufs, gsem, wsem):
        worker = lax.axis_index("s") * SC_CORES + lax.axis_index("c")

        @pl.loop(0, pl.cdiv(n_batches, n_workers))
        def _(j):
            b = j * n_workers + worker

            @pl.when(b < n_batches)
            def _():
                pltpu.sync_copy(pos_hbm.at[b], idx_v)
                items = [(k, h) for k in range(TOP_K) for h in range(2)]

                def fetch(i):
                    k, h = items[i]
                    return pltpu.async_copy(y_hbm.at[idx_v.at[k, pl.ds(h * half, half)]], bufs.at[i % 2], gsem)

                pending_gather = fetch(0)
                pending_write = None
                for i, (k, h) in enumerate(items):
                    pending_gather.wait()
                    if pending_write is not None:
                        pending_write.wait()
                    if i + 1 < len(items):
                        pending_gather = fetch(i + 1)
                    pending_write = pltpu.async_copy(
                        bufs.at[i % 2], out_hbm.at[k, pl.ds(b * SC_ROWS + h * half, half)], wsem)
                pending_write.wait()

    return gather(pos_sc, ys)


def _combine_tile(y_refs, w_ref, x1_ref, sh_ref, mod_ref):
    f_lo = sh_ref[:, :D // 2].astype(F32)
    f_hi = sh_ref[:, D // 2:].astype(F32)
    for k in range(TOP_K):
        lo, hi = _unpack_bf16_pairs(y_refs[k][...])
        f_lo = f_lo + lo * w_ref[:, k:k + 1]
        f_hi = f_hi + hi * w_ref[:, k:k + 1]
    return x1_ref[...] + mod_ref[:, 5 * D:6 * D] * jnp.concatenate([f_lo, f_hi], axis=1)


def _combine_kernel(*refs):
    o_ref = refs[-1]
    o_ref[...] = _combine_tile(refs[:TOP_K], *refs[TOP_K:-1])


def _combine_specs(tile, mod_row):
    planes = [pl.BlockSpec((None, TM, D // 2), lambda t, k=k: (k, t, 0)) for k in range(TOP_K)]
    return planes + [pl.BlockSpec((TM, SUBLANES), lambda t: (t, 0)),
                     pl.BlockSpec((TM, D), lambda t: (tile(t), 0)),
                     pl.BlockSpec((TM, D), lambda t: (t, 0)),
                     pl.BlockSpec((None, 1, 6 * D), lambda t: (mod_row(tile(t)), 0, 0))]


def _combine(yg, w, x1, shared, mod3, n_tiles, tile, mod_row, out_rows, out_tile):
    return pl.pallas_call(
        _combine_kernel,
        grid=(n_tiles,),
        in_specs=_combine_specs(tile, mod_row),
        out_specs=pl.BlockSpec((TM, D), lambda t: (out_tile(t), 0)),
        out_shape=jax.ShapeDtypeStruct((out_rows, D), F32),
        compiler_params=_cparams(1),
        name="moe_combine",
    )(*([yg] * TOP_K), w, x1, shared, mod3)


def _combine_in_proj_kernel(*refs):
    xs_ref, o_ref, h_scr = refs[-3:]
    mod_ref, g_ref, w_ref = refs[TOP_K + 4:-3]
    x = _combine_tile(refs[:TOP_K], *refs[TOP_K:TOP_K + 4])
    xs_ref[...] = x
    _project_in(x, mod_ref, g_ref, w_ref, o_ref, h_scr)


def _combine_in_proj(yg, w, x1, shared, mod3_prev, mod3, g, w_in_r, n_tiles, mod_row):
    n_rows = x1.shape[0]
    ident = lambda t: t
    return pl.pallas_call(
        _combine_in_proj_kernel,
        grid=(n_tiles,),
        in_specs=_combine_specs(ident, mod_row) + [
            pl.BlockSpec((None, 1, 6 * D), lambda t: (mod_row(t), 0, 0)),
            pl.BlockSpec((1, D), lambda t: (0, 0)),
            pl.BlockSpec((D, N_IN_PAD), lambda t: (0, 0), pipeline_mode=pl.Buffered(1))],
        out_specs=[pl.BlockSpec((TM, D), lambda t: (t, 0)),
                   pl.BlockSpec((TM, N_IN_PAD), lambda t: (t, 0))],
        out_shape=[jax.ShapeDtypeStruct((n_rows, D), F32),
                   jax.ShapeDtypeStruct((n_rows, N_IN_PAD), BF16)],
        scratch_shapes=[pltpu.VMEM((TM, D), BF16)],
        compiler_params=_cparams(1),
        name="combine_in_proj",
    )(*([yg] * TOP_K), w, x1, shared, mod3_prev, mod3, g, w_in_r)


def _moe_plan(idx, rank, counts, n_blocks):
    n = idx.shape[0]
    cnt = counts.reshape(N_EXPERTS).astype(jnp.int32)
    padded = (cnt + MOE_ROWS - 1) // MOE_ROWS * MOE_ROWS
    pad_end = jnp.cumsum(padded)
    pad_start = pad_end - padded
    experts = jnp.arange(N_EXPERTS, dtype=jnp.int32)
    pos = rank + jnp.sum(jnp.where(idx[:, :, None] == experts, pad_start, 0), axis=-1)
    blk_start = jnp.arange(n_blocks, dtype=jnp.int32) * MOE_ROWS
    blk_e = jnp.minimum(jnp.sum(blk_start[:, None] >= pad_end[None, :], axis=1), N_EXPERTS - 1).astype(jnp.int32)
    mine = blk_e[:, None] == experts
    in_expert = blk_start - jnp.sum(jnp.where(mine, pad_start, 0), axis=1)
    valid = jnp.clip(jnp.sum(jnp.where(mine, cnt, 0), axis=1) - in_expert, 0, MOE_ROWS).astype(jnp.int32)
    pos_sc = pos.astype(jnp.int32).reshape(n // SC_ROWS, SC_ROWS, TOP_K).transpose(0, 2, 1)
    return blk_e, valid, pos_sc


def _rope_tables(seq, ctx):
    half = MLA_ROPE // 2
    n_freq = half // 2
    inv = ROPE_THETA ** (-2.0 * jnp.arange(n_freq, dtype=F32) / half)
    t = jnp.arange(seq)
    ang_r = (t // GRID_W).astype(F32)[:, None] * inv
    ang_c = (t % GRID_W).astype(F32)[:, None] * inv
    cos = jnp.concatenate([jnp.cos(ang_r), jnp.cos(ang_r), jnp.cos(ang_c), jnp.cos(ang_c)], axis=1)
    sin = jnp.concatenate([-jnp.sin(ang_r), jnp.sin(ang_r), -jnp.sin(ang_c), jnp.sin(ang_c)], axis=1)
    pad_l = MLA_NOPE
    pad_r = HEAD_PAD - MLA_NOPE - MLA_ROPE
    cos = jnp.pad(cos, ((0, ctx), (pad_l, pad_r)), constant_values=1.0)
    cos = cos.at[seq:, :].set(1.0)
    sin = jnp.pad(sin, ((0, ctx), (pad_l, pad_r)))
    lane = jnp.arange(HEAD_PAD)
    partner = jnp.where(lane % 16 < 8, lane + 8, lane - 8)
    swap = (lane[:, None] == partner[None, :]).astype(BF16)
    return cos, sin, swap


def _pad_heads(w, n_heads, width, offset=0):
    k = w.shape[0]
    w = w.reshape(k, n_heads, width)
    w = jnp.pad(w, ((0, 0), (0, 0), (offset, HEAD_PAD - width - offset)))
    return w.reshape(k, n_heads * HEAD_PAD)


def _pad_vec(g, offset):
    return jnp.pad(g, (offset, HEAD_PAD - g.shape[0] - offset)).reshape(1, HEAD_PAD)


def _reorder_w_in(w):
    off_cq, off_ckv, off_kr, off_uv, off_merge = 2048, 2432, 2688, 2720, 3744
    kr = jnp.pad(w[:, off_kr:off_uv], ((0, 0), (MLA_NOPE, HEAD_PAD - MLA_NOPE - MLA_ROPE)))
    return jnp.concatenate([w[:, off_merge:], w[:, off_uv:off_merge], w[:, :off_cq],
                            w[:, off_ckv:off_kr], kr, w[:, off_cq:off_ckv]], axis=1).astype(BF16)


def kernel(x, c, ctx, c_ctx, ada_w, ada_b, norm1_g, norm2_g, w_in, ret_decay_fwd, ret_decay_bwd, ret_gn_g,
           ret_gn_b, w_br_ret, mla_qa_g, mla_w_uq, mla_kva_g, mla_w_ukv, mla_qn_g, mla_kn_g, mla_kr_g, w_br_mla,
           gmlp_ln_g, gmlp_ln_b, gmlp_ws, gmlp_bs, w_br_gmlp, w_out, moe_router, moe_bias, moe_w_gate, moe_w_up,
           moe_w_down, sh_w_gate, sh_w_up, sh_w_down):
    B, seq, _ = x.shape
    n_ctx = ctx.shape[1]
    depth = ada_w.shape[0]
    assert n_ctx == TM and seq % TM == 0 and B + 1 <= SUBLANES and TOP_K <= SUBLANES
    lt = seq + n_ctx
    tiles_per_b = lt // TM
    lat_tiles_per_b = seq // TM
    ctx_tile = lat_tiles_per_b

    def mod_row(t):
        return jnp.where(t % tiles_per_b == ctx_tile, B, t // tiles_per_b)

    c_rows = jnp.concatenate([c, c_ctx[None, :], jnp.zeros((SUBLANES - B - 1, D), F32)], axis=0)
    mod = _ada(c_rows, ada_w, ada_b)
    cos_t, sin_t, swap = _rope_tables(seq, n_ctx)
    x_parts = (x.reshape(B * seq, D), ctx.reshape(B * n_ctx, D))

    pending = None
    for l in range(depth):
        last = l == depth - 1
        mod3 = mod[l].reshape(SUBLANES, 1, 6 * D)
        if pending is None:
            p = _in_proj(x_parts, mod3, norm1_g[l].reshape(1, D), _reorder_w_in(w_in[l]), B * tiles_per_b,
                         tiles_per_b, mod_row)
        else:
            xs, p = _combine_in_proj(*pending, mod3, norm1_g[l].reshape(1, D), _reorder_w_in(w_in[l]),
                                     B * tiles_per_b, mod_row)
            x_parts = (xs,)

        lg = jnp.stack([jax.nn.log_sigmoid(ret_decay_fwd[l].astype(F32)),
                        jax.nn.log_sigmoid(ret_decay_bwd[l].astype(F32))])
        y_ret = _retention(p.reshape(B, lt, N_IN_PAD), lg, ret_gn_g[l].reshape(1, -1), ret_gn_b[l].reshape(1, -1),
                           seq, n_ctx)

        w_ukv = mla_w_ukv[l].reshape(MLA_KV_LORA, MLA_HEADS, MLA_NOPE + MLA_V)
        wk_p = _pad_heads(w_ukv[:, :, :MLA_NOPE].reshape(MLA_KV_LORA, -1), MLA_HEADS, MLA_NOPE).astype(BF16)
        wv = w_ukv[:, :, MLA_NOPE:].reshape(MLA_KV_LORA, MLA_HEADS * MLA_V).T.astype(BF16)
        wq_p = _pad_heads(mla_w_uq[l], MLA_HEADS, MLA_QK).astype(BF16)
        q, k, v = _mla_prep(p, cos_t, sin_t, swap, mla_qa_g[l].reshape(1, -1), mla_kva_g[l].reshape(1, -1),
                            _pad_vec(mla_qn_g[l], 0), _pad_vec(mla_kn_g[l], 0), _pad_vec(mla_kr_g[l], MLA_NOPE),
                            wq_p, wk_p, wv, B, lt)
        o_mla = _attention(q, k, v, seq, n_ctx, lat_tiles_per_b if last else tiles_per_b)

        if last:
            n_tiles = B * lat_tiles_per_b
            tile = lambda t: (t // lat_tiles_per_b) * tiles_per_b + t % lat_tiles_per_b
        else:
            n_tiles = B * tiles_per_b
            tile = lambda t: t
        bs_full = jnp.broadcast_to(gmlp_bs[l][:, :, None], (GMLP_GROUPS, GMLP_CHUNK, GMLP_CHUNK))
        x1, h2 = _merge(x_parts, mod3, p, y_ret.reshape(B * lt, -1), o_mla.reshape(B * lt, -1),
                        gmlp_ln_g[l].reshape(1, -1), gmlp_ln_b[l].reshape(1, -1), gmlp_ws[l].astype(BF16), bs_full,
                        w_br_ret[l].astype(BF16), w_br_mla[l].astype(BF16), w_br_gmlp[l].astype(BF16),
                        w_out[l].astype(BF16), norm2_g[l].reshape(1, D), n_tiles, tile, tiles_per_b, mod_row)

        bias_t = jnp.broadcast_to(moe_bias[l][:, None], (N_EXPERTS, LANES))
        idx, w, rank, counts, hp = _route(h2, moe_router[l].T, bias_t, n_tiles, tile)
        n_act = n_tiles * TM
        n_blocks = -(-(n_act * TOP_K + N_EXPERTS * (MOE_ROWS - 1)) // MOE_ROWS)
        blk_e, valid, pos_sc = _moe_plan(idx, rank, counts, n_blocks)
        xg = _dispatch(pos_sc, hp, n_blocks * MOE_ROWS)
        shared = _shared_expert(hp, sh_w_gate[l].astype(BF16), sh_w_up[l].astype(BF16), sh_w_down[l].astype(BF16))
        ys = _experts(blk_e, valid, xg, moe_w_gate, moe_w_up, moe_w_down, l, n_blocks)
        yg = _gather_rows(pos_sc, ys)
        if last:
            xs = _combine(yg, w, x1, shared, mod3, n_tiles, tile, mod_row, B * seq, lambda t: t)
        else:
            pending = (yg, w, x1, shared, mod3)
    return xs.reshape(B, seq, D)
```

```python
import functools
import math

import jax
import jax.numpy as jnp
from jax import lax
from jax.experimental import pallas as pl
from jax.experimental.pallas import tpu as pltpu
from jax.experimental.pallas import tpu_sc as plsc

F32 = jnp.float32
BF16 = jnp.bfloat16

D = 1024
GRID_W = 64
RET_HEADS = 4
RET_D = 128
RET_CHUNK = 256
RET_OUT_ROWS = 256
MLA_HEADS = 8
MLA_Q_LORA = 384
MLA_KV_LORA = 256
MLA_NOPE = 64
MLA_ROPE = 32
MLA_V = 64
MLA_V_EXT = MLA_V + 16
MLA_QK = MLA_NOPE + MLA_ROPE
HEAD_PAD = 128
ROPE_THETA = 10000.0
GMLP_GROUPS = 4
GMLP_W = 512
GMLP_CHUNK = 128
N_EXPERTS = 64
TOP_K = 6
D_EXPERT = 256
ROUTED_SCALE = 2.5
EPS = 1e-6
LOG2_E = 1.4426950408889634

LANES = 128
SUBLANES = 8
TM = 256
MOE_ROWS = 512
RT = 2 * TM
SHARED_ROWS = 1024
ATT_KV_CHUNK = 1024
ATT_HEADS = 4

C_MERGE = 0
C_UV = 3072
C_RET = 4096
C_CKV = 6144
C_KR = 6400
C_CQ = 6528
N_IN_PAD = 6912
IN_CHUNK = 768
ADA_CHUNK = 1536

VMEM_LIMIT = 56 * 1024 * 1024

SC_CORES = 2
SC_SUBCORES = 16
SC_ROWS = 128


def _cparams(n_axes, vmem=VMEM_LIMIT):
    return pltpu.CompilerParams(dimension_semantics=("arbitrary",) * n_axes, vmem_limit_bytes=vmem)


def _silu(x):
    return x * jax.nn.sigmoid(x)


def _dot(a, b):
    return jnp.dot(a, b, preferred_element_type=F32)


def _dot_nt(a, b):
    return lax.dot_general(a, b, (((1,), (1,)), ((), ())), preferred_element_type=F32)


def _dot_tn(a, b):
    return lax.dot_general(a, b, (((0,), (0,)), ((), ())), preferred_element_type=F32)


def _pack_bf16_pairs(x):
    n = x.shape[1] // 2
    lo = lax.bitcast_convert_type(x[:, :n].astype(BF16).astype(F32), jnp.uint32)
    hi = lax.bitcast_convert_type(x[:, n:].astype(BF16).astype(F32), jnp.uint32)
    return (lo >> 16) | hi


def _unpack_bf16_pairs(u):
    lo = lax.bitcast_convert_type(u << 16, F32)
    hi = lax.bitcast_convert_type(u & jnp.uint32(0xFFFF0000), F32)
    return lo, hi


def _ada_kernel(c_ref, w_ref, b_ref, o_ref):
    s = _silu(c_ref[...])
    o_ref[...] = _dot(s.astype(BF16), w_ref[...].astype(BF16)) + b_ref[...]


def _ada(c_rows, ada_w, ada_b):
    depth = ada_w.shape[0]
    n = ada_w.shape[2]
    cw = ADA_CHUNK
    return pl.pallas_call(
        _ada_kernel,
        grid=(depth, n // cw),
        in_specs=[pl.BlockSpec((SUBLANES, D), lambda l, j: (0, 0)),
                  pl.BlockSpec((None, D, cw), lambda l, j: (l, 0, j)),
                  pl.BlockSpec((None, 1, cw), lambda l, j: (l, 0, j))],
        out_specs=pl.BlockSpec((None, SUBLANES, cw), lambda l, j: (l, 0, j)),
        out_shape=jax.ShapeDtypeStruct((depth, SUBLANES, n), F32),
        compiler_params=_cparams(2),
        name="ada_mod",
    )(c_rows, ada_w, ada_b.reshape(depth, 1, n))


def _modulated_rmsnorm(x, g, shift, scale):
    y = x * lax.rsqrt(jnp.mean(x * x, axis=-1, keepdims=True) + EPS) * g
    return y * (1.0 + scale) + shift


def _stream_specs(x_parts, tile, tiles_per_b):
    if len(x_parts) == 1:
        return [pl.BlockSpec((TM, D), lambda t: (tile(t), 0))]
    lat_tiles = tiles_per_b - 1

    def latent(t):
        s = tile(t)
        return ((s // tiles_per_b) * lat_tiles + jnp.minimum(s % tiles_per_b, lat_tiles - 1), 0)

    return [pl.BlockSpec((TM, D), latent), pl.BlockSpec((TM, D), lambda t: (tile(t) // tiles_per_b, 0))]


def _read_stream(x_refs, tile, tiles_per_b):
    if len(x_refs) == 1:
        return x_refs[0][...]
    is_ctx = tile(pl.program_id(0)) % tiles_per_b == tiles_per_b - 1
    return jnp.where(is_ctx, x_refs[1][...], x_refs[0][...])


def _project_in(x, mod_ref, g_ref, w_ref, o_ref, h_scr):
    h = _modulated_rmsnorm(x, g_ref[...], mod_ref[:, 0:D], mod_ref[:, D:2 * D])
    h_scr[...] = h.astype(BF16)
    for c in range(N_IN_PAD // IN_CHUNK):
        cols = slice(c * IN_CHUNK, (c + 1) * IN_CHUNK)
        o_ref[:, cols] = _dot(h_scr[...], w_ref[:, cols]).astype(BF16)


def _in_proj_kernel(*refs, n_x, tiles_per_b):
    mod_ref, g_ref, w_ref, o_ref, h_scr = refs[n_x:]
    _project_in(_read_stream(refs[:n_x], lambda t: t, tiles_per_b), mod_ref, g_ref, w_ref, o_ref, h_scr)


def _in_proj(x_parts, mod3, g, w_in_r, n_tiles, tiles_per_b, mod_row):
    n_rows = n_tiles * TM
    return pl.pallas_call(
        functools.partial(_in_proj_kernel, n_x=len(x_parts), tiles_per_b=tiles_per_b),
        grid=(n_tiles,),
        in_specs=_stream_specs(x_parts, lambda t: t, tiles_per_b) + [
                  pl.BlockSpec((None, 1, 6 * D), lambda t: (mod_row(t), 0, 0)),
                  pl.BlockSpec((1, D), lambda t: (0, 0)),
                  pl.BlockSpec((D, N_IN_PAD), lambda t: (0, 0), pipeline_mode=pl.Buffered(1))],
        out_specs=pl.BlockSpec((TM, N_IN_PAD), lambda t: (t, 0)),
        out_shape=jax.ShapeDtypeStruct((n_rows, N_IN_PAD), BF16),
        scratch_shapes=[pltpu.VMEM((TM, D), BF16)],
        compiler_params=_cparams(1),
        name="in_proj",
    )(*x_parts, mod3, g, w_in_r)


def _retention_kernel(lg_ref, q_ref, k_ref, v_ref, g_ref, gng_ref, gnb_ref, y_ref, of_scr, ob_scr,
                      *, n_lat_chunks, n_ctx_chunks):
    h = pl.program_id(1)
    lg_f = lg_ref[0, h]
    lg_b = lg_ref[1, h]
    C = RET_CHUNK
    k_scale = RET_D ** -0.5
    ri = lax.broadcasted_iota(jnp.int32, (C, C), 0).astype(F32)
    ci = lax.broadcasted_iota(jnp.int32, (C, C), 1).astype(F32)
    pos = lax.broadcasted_iota(jnp.int32, (C, 1), 0).astype(F32)
    diff = ri - ci
    d_f = jnp.where(diff >= 0, jnp.exp(lg_f * jnp.maximum(diff, 0.0)), 0.0) * k_scale
    d_b = jnp.where(diff < 0, jnp.exp(lg_b * jnp.maximum(-diff, 0.0)), 0.0) * k_scale
    qdec_f = jnp.exp(lg_f * (pos + 1.0))
    kdec_f = jnp.exp(lg_f * (C - 1.0 - pos)) * k_scale
    cdec_f = jnp.exp(lg_f * C)
    qdec_b = jnp.exp(lg_b * (C - pos))
    kdec_b = jnp.exp(lg_b * pos) * k_scale
    cdec_b = jnp.exp(lg_b * C)

    d_both = d_f + d_b

    def chunk(c, state, qdec, kdec, cdec, with_intra):
        rows = pl.ds(pl.multiple_of(c * C, C), C)
        q = q_ref[rows, :]
        k = k_ref[rows, :]
        v = v_ref[rows, :]
        o = _dot((q.astype(F32) * qdec).astype(BF16), state.astype(BF16))
        if with_intra:
            o = o + _dot((_dot_nt(q, k) * d_both).astype(BF16), v)
        kd = (k.astype(F32) * kdec).astype(BF16)
        return rows, o, state * cdec + _dot_tn(kd, v)

    n_all = n_lat_chunks + n_ctx_chunks

    def scan_body(i, states):
        s_f, s_b = states
        c_f = jnp.where(i < n_ctx_chunks, n_lat_chunks + i, i - n_ctx_chunks)
        rows, o, s_f = chunk(c_f, s_f, qdec_f, kdec_f, cdec_f, True)
        of_scr[rows, :] = o
        rows, o, s_b = chunk(n_all - 1 - i, s_b, qdec_b, kdec_b, cdec_b, False)
        ob_scr[rows, :] = o
        return s_f, s_b

    zero = jnp.zeros((RET_D, RET_D), F32)
    lax.fori_loop(0, n_all, scan_body, (zero, zero), unroll=11)

    def out_body(c, _):
        rows = pl.ds(pl.multiple_of(c * RET_OUT_ROWS, RET_OUT_ROWS), RET_OUT_ROWS)
        o = of_scr[rows, :] + ob_scr[rows, :]
        mu = jnp.mean(o, axis=-1, keepdims=True)
        var = jnp.mean(jnp.square(o - mu), axis=-1, keepdims=True)
        on = (o - mu) * lax.rsqrt(var + EPS)
        y = _silu(g_ref[rows, :].astype(F32)) * (on * gng_ref[...] + gnb_ref[...])
        y_ref[rows, :] = y.astype(BF16)
        return 0

    lax.fori_loop(0, n_all * C // RET_OUT_ROWS, out_body, 0, unroll=3)


def _retention(p3, lg, gn_g, gn_b, seq, ctx):
    B, lt, _ = p3.shape
    base = C_RET // RET_D
    kern = functools.partial(_retention_kernel, n_lat_chunks=seq // RET_CHUNK, n_ctx_chunks=ctx // RET_CHUNK)

    def col(off):
        return pl.BlockSpec((None, lt, RET_D), lambda b, h: (b, 0, base + off * RET_HEADS + h))

    return pl.pallas_call(
        kern,
        grid=(B, RET_HEADS),
        in_specs=[pl.BlockSpec(memory_space=pltpu.SMEM),
                  col(0), col(1), col(2), col(3),
                  pl.BlockSpec((1, RET_D), lambda b, h: (0, h)),
                  pl.BlockSpec((1, RET_D), lambda b, h: (0, h))],
        out_specs=pl.BlockSpec((None, lt, RET_D), lambda b, h: (b, 0, h)),
        out_shape=jax.ShapeDtypeStruct((B, lt, RET_HEADS * RET_D), BF16),
        scratch_shapes=[pltpu.VMEM((lt, RET_D), F32), pltpu.VMEM((lt, RET_D), F32)],
        compiler_params=_cparams(2),
        name="retention",
    )(lg, p3, p3, p3, p3, gn_g, gn_b)


def _mla_prep_kernel(cq_ref, ckv_ref, kr_ref, cos_ref, sin_ref, swap_ref, qa_ref, kva_ref, qn_ref, kn_ref, krg_ref,
                     wq_ref, wk_ref, wv_ref, q_ref, k_ref, v_ref):
    cos = cos_ref[...]
    sin = sin_ref[...]

    def rms(x, n):
        return x * lax.rsqrt(jnp.sum(x * x, axis=-1, keepdims=True) * (1.0 / n) + EPS)

    def rope(x):
        return x * cos + _dot(x.astype(BF16), swap_ref[...]) * sin

    cq = cq_ref[...].astype(F32)
    cqn = (rms(cq, MLA_Q_LORA) * qa_ref[...]).astype(BF16)
    q_all = _dot(cqn, wq_ref[...])
    ckv = ckv_ref[...].astype(F32)
    ckvn = (rms(ckv, MLA_KV_LORA) * kva_ref[...]).astype(BF16)
    k_all = _dot(ckvn, wk_ref[...])
    k_rope = rope(rms(kr_ref[...].astype(F32), MLA_ROPE) * krg_ref[...])
    scale = MLA_QK ** -0.5 * LOG2_E
    v_t = _dot_nt(wv_ref[...], ckvn)
    ones_row = jnp.where(lax.broadcasted_iota(jnp.int32, (MLA_V_EXT - MLA_V, TM), 0) == 0, 1.0, 0.0)
    for h in range(MLA_HEADS):
        cols = slice(h * HEAD_PAD, (h + 1) * HEAD_PAD)
        qh = rope(rms(q_all[:, cols], MLA_QK) * qn_ref[...]) * scale
        q_ref[h] = qh.astype(BF16)
        kh = rms(k_all[:, cols], MLA_NOPE) * kn_ref[...] + k_rope
        k_ref[h] = kh.astype(BF16)
        v_ref[h] = jnp.concatenate([v_t[h * MLA_V:(h + 1) * MLA_V, :], ones_row], axis=0).astype(BF16)


def _mla_prep(p, cos_t, sin_t, swap, qa_g, kva_g, qn_p, kn_p, kr_p, wq_p, wk_p, wv, B, lt):
    tiles_per_b = lt // TM
    hw = MLA_HEADS * HEAD_PAD
    const = lambda shape: pl.BlockSpec(shape, lambda b, j: (0,) * len(shape))
    head_out = pl.BlockSpec((None, MLA_HEADS, TM, HEAD_PAD), lambda b, j: (b, 0, j, 0))
    shp = jax.ShapeDtypeStruct((B, MLA_HEADS, lt, HEAD_PAD), BF16)
    v_out = pl.BlockSpec((None, MLA_HEADS, None, MLA_V_EXT, TM), lambda b, j: (b, 0, j, 0, 0))
    v_shp = jax.ShapeDtypeStruct((B, MLA_HEADS, tiles_per_b, MLA_V_EXT, TM), BF16)
    return pl.pallas_call(
        _mla_prep_kernel,
        grid=(B, tiles_per_b),
        in_specs=[pl.BlockSpec((TM, MLA_Q_LORA), lambda b, j: (b * tiles_per_b + j, C_CQ // MLA_Q_LORA)),
                  pl.BlockSpec((TM, MLA_KV_LORA), lambda b, j: (b * tiles_per_b + j, C_CKV // MLA_KV_LORA)),
                  pl.BlockSpec((TM, HEAD_PAD), lambda b, j: (b * tiles_per_b + j, C_KR // HEAD_PAD)),
                  pl.BlockSpec((TM, HEAD_PAD), lambda b, j: (j, 0)),
                  pl.BlockSpec((TM, HEAD_PAD), lambda b, j: (j, 0)),
                  const((HEAD_PAD, HEAD_PAD)),
                  const((1, MLA_Q_LORA)), const((1, MLA_KV_LORA)),
                  const((1, HEAD_PAD)), const((1, HEAD_PAD)), const((1, HEAD_PAD)),
                  const((MLA_Q_LORA, hw)), const((MLA_KV_LORA, hw)), const((MLA_HEADS * MLA_V, MLA_KV_LORA))],
        out_specs=[head_out, head_out, v_out],
        out_shape=[shp, shp, v_shp],
        compiler_params=_cparams(2),
        name="mla_prep",
    )(p, p, p, cos_t, sin_t, swap, qa_g, kva_g, qn_p, kn_p, kr_p, wq_p, wk_p, wv)


def _attention_stages(seq, ctx):
    n_blk = ATT_KV_CHUNK // TM
    total = (seq + ctx) // TM
    first = n_blk + total % n_blk if total >= n_blk else total
    return [(total - first, first)] + [(c * n_blk, n_blk) for c in range((total - first) // n_blk)]


def _attention_kernel(q_ref, k_ref, v_ref, o_ref, s_scr, *, seq, ctx, ctx_tile):
    i = pl.program_id(2)

    def scores(hh, slot, blk, nb):
        s_scr[hh, slot, 0:nb * TM, :] = _dot_nt(k_ref[hh, blk * TM:(blk + nb) * TM, :], q_ref[hh])

    def absorb(hh, slot, blk, nb, carry):
        m, acc = carry
        s = s_scr[hh, slot, 0:nb * TM, :]
        m_new = jnp.maximum(m, jnp.max(s, axis=0, keepdims=True))
        p = jnp.exp2(s - m_new).astype(BF16)
        acc = jnp.exp2(m - m_new) * acc
        for j in range(nb):
            acc = acc + _dot(v_ref[hh, blk + j], p[j * TM:(j + 1) * TM, :])
        return m_new, acc

    def attend(stages):
        for hh in range(ATT_HEADS):
            scores(hh, 0, *stages[0])
        carries = [(jnp.full((1, TM), -jnp.inf, F32), jnp.zeros((MLA_V_EXT, TM), F32))] * ATT_HEADS
        for n, stage in enumerate(stages):
            for hh in range(ATT_HEADS):
                if n + 1 < len(stages):
                    scores(hh, (n + 1) % 2, *stages[n + 1])
                carries[hh] = absorb(hh, n % 2, *stage, carries[hh])
        outs = [acc[0:MLA_V, :] / acc[MLA_V:MLA_V + 1, :] for _, acc in carries]
        o_ref[...] = jnp.concatenate(outs, axis=0).T.astype(BF16)

    @pl.when(i != ctx_tile)
    def _():
        attend(_attention_stages(seq, ctx))

    @pl.when(i == ctx_tile)
    def _():
        attend([(seq // TM, ctx // TM)])


def _attention(q, k, v, seq, ctx, n_q_tiles):
    B, H, lt, _ = q.shape
    kern = functools.partial(_attention_kernel, seq=seq, ctx=ctx, ctx_tile=seq // TM)
    slot_rows = max(nb for _, nb in _attention_stages(seq, ctx)) * TM
    return pl.pallas_call(
        kern,
        grid=(B, H // ATT_HEADS, n_q_tiles),
        in_specs=[pl.BlockSpec((None, ATT_HEADS, TM, HEAD_PAD), lambda b, h, i: (b, h, i, 0)),
                  pl.BlockSpec((None, ATT_HEADS, lt, HEAD_PAD), lambda b, h, i: (b, h, 0, 0)),
                  pl.BlockSpec((None, ATT_HEADS, lt // TM, MLA_V_EXT, TM), lambda b, h, i: (b, h, 0, 0, 0))],
        out_specs=pl.BlockSpec((None, TM, ATT_HEADS * MLA_V), lambda b, h, i: (b, i, h)),
        out_shape=jax.ShapeDtypeStruct((B, lt, H * MLA_V), BF16),
        scratch_shapes=[pltpu.VMEM((ATT_HEADS, 2, slot_rows, TM), F32)],
        compiler_params=_cparams(3),
        name="attention",
    )(q, k, v)


def _merge_kernel(*refs, n_x, tile, tiles_per_b):
    (mod_ref, mg_ref, uv_ref, yr_ref, om_ref, lng_ref, lnb_ref, ws_ref, bs_ref,
     wr_ref, wm_ref, wg_ref, wo_ref, n2_ref, x1_ref, h2_ref) = refs[n_x:]
    yr = _dot(yr_ref[...], wr_ref[...])
    ym = _dot(om_ref[...], wm_ref[...])
    z = jax.nn.gelu(uv_ref[...].astype(F32))
    u = z[:, :GMLP_W]
    v = z[:, GMLP_W:]
    mu = jnp.mean(v, axis=-1, keepdims=True)
    var = jnp.mean(jnp.square(v - mu), axis=-1, keepdims=True)
    vn = ((v - mu) * lax.rsqrt(var + EPS) * lng_ref[...] + lnb_ref[...]).astype(BF16)
    gw = GMLP_W // GMLP_GROUPS
    chunks = []
    for c in range(TM // GMLP_CHUNK):
        rows = slice(c * GMLP_CHUNK, (c + 1) * GMLP_CHUNK)
        groups = [_dot(ws_ref[g], vn[rows, g * gw:(g + 1) * gw]) + bs_ref[g] for g in range(GMLP_GROUPS)]
        chunks.append(jnp.concatenate(groups, axis=1))
    sv = jnp.concatenate(chunks, axis=0)
    yg = _dot((u * sv).astype(BF16), wg_ref[...])
    gate = jax.nn.sigmoid(mg_ref[...].astype(F32))
    y = gate[:, :D] * yr + gate[:, D:2 * D] * ym + gate[:, 2 * D:] * yg
    out = _dot(y.astype(BF16), wo_ref[...])
    x1 = _read_stream(refs[:n_x], tile, tiles_per_b) + mod_ref[:, 2 * D:3 * D] * out
    x1_ref[...] = x1
    h2_ref[...] = _modulated_rmsnorm(x1, n2_ref[...], mod_ref[:, 3 * D:4 * D], mod_ref[:, 4 * D:5 * D])


def _merge(x_parts, mod3, p, y_ret, o_mla, ln_g, ln_b, ws, bs_full, w_br_ret, w_br_mla, w_br_gmlp, w_out, n2_g,
           n_tiles, tile, tiles_per_b, mod_row):
    n_rows = p.shape[0]
    const = lambda shape: pl.BlockSpec(shape, lambda t: (0,) * len(shape))
    row = lambda w, cb=0: pl.BlockSpec((TM, w), lambda t: (tile(t), cb))
    shp = jax.ShapeDtypeStruct((n_rows, D), F32)
    return pl.pallas_call(
        functools.partial(_merge_kernel, n_x=len(x_parts), tile=tile, tiles_per_b=tiles_per_b),
        grid=(n_tiles,),
        in_specs=_stream_specs(x_parts, tile, tiles_per_b) + [
                  pl.BlockSpec((None, 1, 6 * D), lambda t: (mod_row(tile(t)), 0, 0)),
                  row(3 * D, C_MERGE // (3 * D)), row(D, C_UV // D),
                  row(RET_HEADS * RET_D), row(MLA_HEADS * MLA_V),
                  const((1, GMLP_W)), const((1, GMLP_W)),
                  const((GMLP_GROUPS, GMLP_CHUNK, GMLP_CHUNK)), const((GMLP_GROUPS, GMLP_CHUNK, GMLP_CHUNK)),
                  const((RET_HEADS * RET_D, D)), const((MLA_HEADS * MLA_V, D)), const((GMLP_W, D)),
                  const((D, D)), const((1, D))],
        out_specs=[row(D), row(D)],
        out_shape=[shp, shp],
        compiler_params=_cparams(1),
        name="merge",
    )(*x_parts, mod3, p, p, y_ret, o_mla, ln_g, ln_b, ws, bs_full, w_br_ret, w_br_mla, w_br_gmlp, w_out, n2_g)


def _route_kernel(ha_ref, hb_ref, rt_ref, bt_ref, idx_ref, w_ref, rank_ref, cnt_ref, hp_ref, cnt_scr):
    @pl.when(pl.program_id(0) == 0)
    def _():
        cnt_scr[...] = jnp.zeros_like(cnt_scr)

    h = jnp.concatenate([ha_ref[...], hb_ref[...]], axis=0)
    logits = lax.dot_general(rt_ref[...], h, (((1,), (1,)), ((), ())), preferred_element_type=F32,
                             precision=lax.Precision.HIGHEST)
    scores = jax.nn.sigmoid(logits)
    sel = scores + bt_ref[:, 0:1]
    row_e = lax.broadcasted_iota(jnp.int32, (N_EXPERTS, RT), 0).astype(F32)
    row_o = lax.broadcasted_iota(jnp.int32, (SUBLANES, RT), 0)
    idx_out = jnp.zeros((SUBLANES, RT), F32)
    w_out = jnp.zeros((SUBLANES, RT), F32)
    hits = []
    for k in range(TOP_K):
        best = jnp.max(sel, axis=0, keepdims=True)
        pick = jnp.min(jnp.where(sel == best, row_e, float(N_EXPERTS)), axis=0, keepdims=True)
        hit = row_e == pick
        hits.append(hit)
        wk = jnp.sum(jnp.where(hit, scores, 0.0), axis=0, keepdims=True)
        sel = jnp.where(hit, -jnp.inf, sel)
        idx_out = jnp.where(row_o == k, pick, idx_out)
        w_out = jnp.where(row_o == k, wk, w_out)
    w_out = w_out / jnp.sum(w_out, axis=0, keepdims=True) * ROUTED_SCALE
    idx_ref[...] = idx_out.astype(jnp.int32)
    w_ref[...] = w_out
    chosen = jnp.zeros((N_EXPERTS, RT), F32)
    for hit in hits:
        chosen = jnp.where(hit, 1.0, chosen)
    earlier = (lax.broadcasted_iota(jnp.int32, (RT, RT), 0) < lax.broadcasted_iota(jnp.int32, (RT, RT), 1))
    before = _dot(chosen.astype(BF16), jnp.where(earlier, 1.0, 0.0).astype(BF16)) + cnt_scr[:, 0:1]
    rank_out = jnp.zeros((SUBLANES, RT), F32)
    for k, hit in enumerate(hits):
        rank_out = jnp.where(row_o == k, jnp.sum(jnp.where(hit, before, 0.0), axis=0, keepdims=True), rank_out)
    rank_ref[...] = rank_out.astype(jnp.int32)
    cnt_scr[...] += jnp.sum(chosen, axis=1, keepdims=True)
    cnt_ref[...] = cnt_scr[...]
    hp_ref[...] = _pack_bf16_pairs(h)


def _route(h2, router_t, bias_t, n_tiles, tile):
    const = lambda shape: pl.BlockSpec(shape, lambda t: (0,) * len(shape))
    n_act = n_tiles * TM
    assert n_tiles % 2 == 0
    n_steps = n_tiles // 2
    per_tok = pl.BlockSpec((None, SUBLANES, RT), lambda t: (t, 0, 0))
    idx_t, w_t, rank_t, counts, hp = pl.pallas_call(
        _route_kernel,
        grid=(n_steps,),
        in_specs=[pl.BlockSpec((TM, D), lambda t: (tile(2 * t), 0)),
                  pl.BlockSpec((TM, D), lambda t: (tile(2 * t + 1), 0)),
                  const((N_EXPERTS, D)), const((N_EXPERTS, LANES))],
        out_specs=[per_tok, per_tok, per_tok,
                   pl.BlockSpec((N_EXPERTS, LANES), lambda t: (0, 0)),
                   pl.BlockSpec((RT, D // 2), lambda t: (t, 0))],
        out_shape=[jax.ShapeDtypeStruct((n_steps, SUBLANES, RT), jnp.int32),
                   jax.ShapeDtypeStruct((n_steps, SUBLANES, RT), F32),
                   jax.ShapeDtypeStruct((n_steps, SUBLANES, RT), jnp.int32),
                   jax.ShapeDtypeStruct((N_EXPERTS, LANES), F32),
                   jax.ShapeDtypeStruct((n_act, D // 2), jnp.uint32)],
        scratch_shapes=[pltpu.VMEM((N_EXPERTS, LANES), F32)],
        compiler_params=_cparams(1),
        name="route",
    )(h2, h2, router_t, bias_t)
    token_major = lambda a: a.transpose(0, 2, 1).reshape(n_act, SUBLANES)
    return (token_major(idx_t)[:, :TOP_K], token_major(w_t), token_major(rank_t)[:, :TOP_K], counts[:, 0], hp)


def _shared_expert_kernel(h_ref, sg_ref, su_ref, sd_ref, o_ref):
    lo, hi = _unpack_bf16_pairs(h_ref[...])
    hb = jnp.concatenate([lo, hi], axis=1).astype(BF16)
    a = _silu(_dot(hb, sg_ref[...])) * _dot(hb, su_ref[...])
    o_ref[...] = _dot(a.astype(BF16), sd_ref[...]).astype(BF16)


def _shared_expert(hp, sg, su, sd):
    const = lambda shape: pl.BlockSpec(shape, lambda t: (0,) * len(shape))
    n_act = hp.shape[0]
    rows = math.gcd(n_act, SHARED_ROWS)
    return pl.pallas_call(
        _shared_expert_kernel,
        grid=(n_act // rows,),
        in_specs=[pl.BlockSpec((rows, D // 2), lambda t: (t, 0)),
                  const((D, D_EXPERT)), const((D, D_EXPERT)), const((D_EXPERT, D))],
        out_specs=pl.BlockSpec((rows, D), lambda t: (t, 0)),
        out_shape=jax.ShapeDtypeStruct((n_act, D), BF16),
        compiler_params=_cparams(1),
        name="shared_expert",
    )(hp, sg, su, sd)


def _dispatch(pos_sc, hp, n_rows):
    n_batches = pos_sc.shape[0]
    n_workers = SC_CORES * SC_SUBCORES
    mesh = plsc.VectorSubcoreMesh(core_axis_name="c", subcore_axis_name="s")

    @functools.partial(
        pl.kernel, mesh=mesh,
        out_type=jax.ShapeDtypeStruct((n_rows, D // 2), jnp.uint32),
        scratch_types=[pltpu.VMEM((TOP_K, SC_ROWS), jnp.int32),
                       pltpu.VMEM((SC_ROWS, D // 2), jnp.uint32),
                       pltpu.SemaphoreType.DMA],
        name="moe_dispatch")
    def scatter(pos_hbm, h_hbm, xs_hbm, idx_v, rows_v, sem):
        worker = lax.axis_index("s") * SC_CORES + lax.axis_index("c")

        @pl.loop(0, pl.cdiv(n_batches, n_workers))
        def _(j):
            b = j * n_workers + worker

            @pl.when(b < n_batches)
            def _():
                pltpu.sync_copy(pos_hbm.at[b], idx_v)
                pltpu.sync_copy(h_hbm.at[pl.ds(b * SC_ROWS, SC_ROWS)], rows_v)
                copies = [pltpu.async_copy(rows_v, xs_hbm.at[idx_v.at[k]], sem) for k in range(TOP_K)]
                for cp in copies:
                    cp.wait()

    return scatter(pos_sc, hp)


def _expert_kernel(blk_e_ref, valid_ref, x_ref, wg_ref, wu_ref, wd_ref, y_ref, wg_s, wu_s, wd_s):
    i = pl.program_id(0)
    n_valid = valid_ref[i]

    @pl.when(n_valid > 0)
    def _():
        @pl.when(jnp.logical_or(i == 0, blk_e_ref[i] != blk_e_ref[jnp.maximum(i - 1, 0)]))
        def _():
            wg_s[...] = wg_ref[...].astype(BF16)
            wu_s[...] = wu_ref[...].astype(BF16)
            wd_s[...] = wd_ref[...].astype(BF16)

        row = lax.broadcasted_iota(jnp.int32, (MOE_ROWS, 1), 0)
        lo, hi = _unpack_bf16_pairs(jnp.where(row < n_valid, x_ref[...], jnp.uint32(0)))
        x = jnp.concatenate([lo, hi], axis=1).astype(BF16)
        hb = _silu(_dot(x, wg_s[...])) * _dot(x, wu_s[...])
        y_ref[...] = _pack_bf16_pairs(_dot(hb.astype(BF16), wd_s[...]))

    @pl.when(n_valid == 0)
    def _():
        y_ref[...] = jnp.zeros_like(y_ref)


def _experts(blk_e, valid, xs, wg, wu, wd, layer, n_blocks):
    grid_spec = pltpu.PrefetchScalarGridSpec(
        num_scalar_prefetch=2,
        grid=(n_blocks,),
        in_specs=[pl.BlockSpec((MOE_ROWS, D // 2), lambda i, be, nv: (i, 0)),
                  pl.BlockSpec((None, None, D, D_EXPERT), lambda i, be, nv: (layer, be[i], 0, 0)),
                  pl.BlockSpec((None, None, D, D_EXPERT), lambda i, be, nv: (layer, be[i], 0, 0)),
                  pl.BlockSpec((None, None, D_EXPERT, D), lambda i, be, nv: (layer, be[i], 0, 0))],
        out_specs=pl.BlockSpec((MOE_ROWS, D // 2), lambda i, be, nv: (i, 0)),
        scratch_shapes=[pltpu.VMEM((D, D_EXPERT), BF16), pltpu.VMEM((D, D_EXPERT), BF16),
                        pltpu.VMEM((D_EXPERT, D), BF16)],
    )
    return pl.pallas_call(
        _expert_kernel,
        grid_spec=grid_spec,
        out_shape=jax.ShapeDtypeStruct((n_blocks * MOE_ROWS, D // 2), jnp.uint32),
        compiler_params=_cparams(1),
        name="routed_experts",
    )(blk_e, valid, xs, wg, wu, wd)


def _gather_rows(pos_sc, ys):
    n_batches = pos_sc.shape[0]
    n_workers = SC_CORES * SC_SUBCORES
    half = SC_ROWS // 2
    mesh = plsc.VectorSubcoreMesh(core_axis_name="c", subcore_axis_name="s")

    @functools.partial(
        pl.kernel, mesh=mesh,
        out_type=jax.ShapeDtypeStruct((TOP_K, n_batches * SC_ROWS, D // 2), jnp.uint32),
        scratch_types=[pltpu.VMEM((TOP_K, SC_ROWS), jnp.int32),
                       pltpu.VMEM((2, half, D // 2), jnp.uint32),
                       pltpu.SemaphoreType.DMA, pltpu.SemaphoreType.DMA],
        name="moe_gather")
    def gather(pos_hbm, y_hbm, out_hbm, idx_v, bufs, gsem, wsem):
        worker = lax.axis_index("s") * SC_CORES + lax.axis_index("c")

        @pl.loop(0, pl.cdiv(n_batches, n_workers))
        def _(j):
            b = j * n_workers + worker

            @pl.when(b < n_batches)
            def _():
                pltpu.sync_copy(pos_hbm.at[b], idx_v)
                items = [(k, h) for k in range(TOP_K) for h in range(2)]

                def fetch(i):
                    k, h = items[i]
                    return pltpu.async_copy(y_hbm.at[idx_v.at[k, pl.ds(h * half, half)]], bufs.at[i % 2], gsem)

                pending_gather = fetch(0)
                pending_write = None
                for i, (k, h) in enumerate(items):
                    pending_gather.wait()
                    if pending_write is not None:
                        pending_write.wait()
                    if i + 1 < len(items):
                        pending_gather = fetch(i + 1)
                    pending_write = pltpu.async_copy(
                        bufs.at[i % 2], out_hbm.at[k, pl.ds(b * SC_ROWS + h * half, half)], wsem)
                pending_write.wait()

    return gather(pos_sc, ys)


def _combine_tile(y_refs, w_ref, x1_ref, sh_ref, mod_ref):
    f_lo = sh_ref[:, :D // 2].astype(F32)
    f_hi = sh_ref[:, D // 2:].astype(F32)
    for k in range(TOP_K):
        lo, hi = _unpack_bf16_pairs(y_refs[k][...])
        f_lo = f_lo + lo * w_ref[:, k:k + 1]
        f_hi = f_hi + hi * w_ref[:, k:k + 1]
    return x1_ref[...] + mod_ref[:, 5 * D:6 * D] * jnp.concatenate([f_lo, f_hi], axis=1)


def _combine_kernel(*refs):
    o_ref = refs[-1]
    o_ref[...] = _combine_tile(refs[:TOP_K], *refs[TOP_K:-1])


def _combine_specs(tile, mod_row):
    planes = [pl.BlockSpec((None, TM, D // 2), lambda t, k=k: (k, t, 0)) for k in range(TOP_K)]
    return planes + [pl.BlockSpec((TM, SUBLANES), lambda t: (t, 0)),
                     pl.BlockSpec((TM, D), lambda t: (tile(t), 0)),
                     pl.BlockSpec((TM, D), lambda t: (t, 0)),
                     pl.BlockSpec((None, 1, 6 * D), lambda t: (mod_row(tile(t)), 0, 0))]


def _combine(yg, w, x1, shared, mod3, n_tiles, tile, mod_row, out_rows, out_tile):
    return pl.pallas_call(
        _combine_kernel,
        grid=(n_tiles,),
        in_specs=_combine_specs(tile, mod_row),
        out_specs=pl.BlockSpec((TM, D), lambda t: (out_tile(t), 0)),
        out_shape=jax.ShapeDtypeStruct((out_rows, D), F32),
        compiler_params=_cparams(1),
        name="moe_combine",
    )(*([yg] * TOP_K), w, x1, shared, mod3)


def _combine_in_proj_kernel(*refs):
    xs_ref, o_ref, h_scr = refs[-3:]
    mod_ref, g_ref, w_ref = refs[TOP_K + 4:-3]
    x = _combine_tile(refs[:TOP_K], *refs[TOP_K:TOP_K + 4])
    xs_ref[...] = x
    _project_in(x, mod_ref, g_ref, w_ref, o_ref, h_scr)


def _combine_in_proj(yg, w, x1, shared, mod3_prev, mod3, g, w_in_r, n_tiles, mod_row):
    n_rows = x1.shape[0]
    ident = lambda t: t
    return pl.pallas_call(
        _combine_in_proj_kernel,
        grid=(n_tiles,),
        in_specs=_combine_specs(ident, mod_row) + [
            pl.BlockSpec((None, 1, 6 * D), lambda t: (mod_row(t), 0, 0)),
            pl.BlockSpec((1, D), lambda t: (0, 0)),
            pl.BlockSpec((D, N_IN_PAD), lambda t: (0, 0), pipeline_mode=pl.Buffered(1))],
        out_specs=[pl.BlockSpec((TM, D), lambda t: (t, 0)),
                   pl.BlockSpec((TM, N_IN_PAD), lambda t: (t, 0))],
        out_shape=[jax.ShapeDtypeStruct((n_rows, D), F32),
                   jax.ShapeDtypeStruct((n_rows, N_IN_PAD), BF16)],
        scratch_shapes=[pltpu.VMEM((TM, D), BF16)],
        compiler_params=_cparams(1),
        name="combine_in_proj",
    )(*([yg] * TOP_K), w, x1, shared, mod3_prev, mod3, g, w_in_r)


def _moe_plan(idx, rank, counts, n_blocks):
    n = idx.shape[0]
    cnt = counts.reshape(N_EXPERTS).astype(jnp.int32)
    padded = (cnt + MOE_ROWS - 1) // MOE_ROWS * MOE_ROWS
    pad_end = jnp.cumsum(padded)
    pad_start = pad_end - padded
    experts = jnp.arange(N_EXPERTS, dtype=jnp.int32)
    pos = rank + jnp.sum(jnp.where(idx[:, :, None] == experts, pad_start, 0), axis=-1)
    blk_start = jnp.arange(n_blocks, dtype=jnp.int32) * MOE_ROWS
    blk_e = jnp.minimum(jnp.sum(blk_start[:, None] >= pad_end[None, :], axis=1), N_EXPERTS - 1).astype(jnp.int32)
    mine = blk_e[:, None] == experts
    in_expert = blk_start - jnp.sum(jnp.where(mine, pad_start, 0), axis=1)
    valid = jnp.clip(jnp.sum(jnp.where(mine, cnt, 0), axis=1) - in_expert, 0, MOE_ROWS).astype(jnp.int32)
    pos_sc = pos.astype(jnp.int32).reshape(n // SC_ROWS, SC_ROWS, TOP_K).transpose(0, 2, 1)
    return blk_e, valid, pos_sc


def _rope_tables(seq, ctx):
    half = MLA_ROPE // 2
    n_freq = half // 2
    inv = ROPE_THETA ** (-2.0 * jnp.arange(n_freq, dtype=F32) / half)
    t = jnp.arange(seq)
    ang_r = (t // GRID_W).astype(F32)[:, None] * inv
    ang_c = (t % GRID_W).astype(F32)[:, None] * inv
    cos = jnp.concatenate([jnp.cos(ang_r), jnp.cos(ang_r), jnp.cos(ang_c), jnp.cos(ang_c)], axis=1)
    sin = jnp.concatenate([-jnp.sin(ang_r), jnp.sin(ang_r), -jnp.sin(ang_c), jnp.sin(ang_c)], axis=1)
    pad_l = MLA_NOPE
    pad_r = HEAD_PAD - MLA_NOPE - MLA_ROPE
    cos = jnp.pad(cos, ((0, ctx), (pad_l, pad_r)), constant_values=1.0)
    cos = cos.at[seq:, :].set(1.0)
    sin = jnp.pad(sin, ((0, ctx), (pad_l, pad_r)))
    lane = jnp.arange(HEAD_PAD)
    partner = jnp.where(lane % 16 < 8, lane + 8, lane - 8)
    swap = (lane[:, None] == partner[None, :]).astype(BF16)
    return cos, sin, swap


def _pad_heads(w, n_heads, width, offset=0):
    k = w.shape[0]
    w = w.reshape(k, n_heads, width)
    w = jnp.pad(w, ((0, 0), (0, 0), (offset, HEAD_PAD - width - offset)))
    return w.reshape(k, n_heads * HEAD_PAD)


def _pad_vec(g, offset):
    return jnp.pad(g, (offset, HEAD_PAD - g.shape[0] - offset)).reshape(1, HEAD_PAD)


def _reorder_w_in(w):
    off_cq, off_ckv, off_kr, off_uv, off_merge = 2048, 2432, 2688, 2720, 3744
    kr = jnp.pad(w[:, off_kr:off_uv], ((0, 0), (MLA_NOPE, HEAD_PAD - MLA_NOPE - MLA_ROPE)))
    return jnp.concatenate([w[:, off_merge:], w[:, off_uv:off_merge], w[:, :off_cq],
                            w[:, off_ckv:off_kr], kr, w[:, off_cq:off_ckv]], axis=1).astype(BF16)


def kernel(x, c, ctx, c_ctx, ada_w, ada_b, norm1_g, norm2_g, w_in, ret_decay_fwd, ret_decay_bwd, ret_gn_g,
           ret_gn_b, w_br_ret, mla_qa_g, mla_w_uq, mla_kva_g, mla_w_ukv, mla_qn_g, mla_kn_g, mla_kr_g, w_br_mla,
           gmlp_ln_g, gmlp_ln_b, gmlp_ws, gmlp_bs, w_br_gmlp, w_out, moe_router, moe_bias, moe_w_gate, moe_w_up,
           moe_w_down, sh_w_gate, sh_w_up, sh_w_down):
    B, seq, _ = x.shape
    n_ctx = ctx.shape[1]
    depth = ada_w.shape[0]
    assert n_ctx == TM and seq % TM == 0 and B + 1 <= SUBLANES and TOP_K <= SUBLANES
    lt = seq + n_ctx
    tiles_per_b = lt // TM
    lat_tiles_per_b = seq // TM
    ctx_tile = lat_tiles_per_b

    def mod_row(t):
        return jnp.where(t % tiles_per_b == ctx_tile, B, t // tiles_per_b)

    c_rows = jnp.concatenate([c, c_ctx[None, :], jnp.zeros((SUBLANES - B - 1, D), F32)], axis=0)
    mod = _ada(c_rows, ada_w, ada_b)
    cos_t, sin_t, swap = _rope_tables(seq, n_ctx)
    x_parts = (x.reshape(B * seq, D), ctx.reshape(B * n_ctx, D))

    pending = None
    for l in range(depth):
        last = l == depth - 1
        mod3 = mod[l].reshape(SUBLANES, 1, 6 * D)
        if pending is None:
            p = _in_proj(x_parts, mod3, norm1_g[l].reshape(1, D), _reorder_w_in(w_in[l]), B * tiles_per_b,
                         tiles_per_b, mod_row)
        else:
            xs, p = _combine_in_proj(*pending, mod3, norm1_g[l].reshape(1, D), _reorder_w_in(w_in[l]),
                                     B * tiles_per_b, mod_row)
            x_parts = (xs,)

        lg = jnp.stack([jax.nn.log_sigmoid(ret_decay_fwd[l].astype(F32)),
                        jax.nn.log_sigmoid(ret_decay_bwd[l].astype(F32))])
        y_ret = _retention(p.reshape(B, lt, N_IN_PAD), lg, ret_gn_g[l].reshape(1, -1), ret_gn_b[l].reshape(1, -1),
                           seq, n_ctx)

        w_ukv = mla_w_ukv[l].reshape(MLA_KV_LORA, MLA_HEADS, MLA_NOPE + MLA_V)
        wk_p = _pad_heads(w_ukv[:, :, :MLA_NOPE].reshape(MLA_KV_LORA, -1), MLA_HEADS, MLA_NOPE).astype(BF16)
        wv = w_ukv[:, :, MLA_NOPE:].reshape(MLA_KV_LORA, MLA_HEADS * MLA_V).T.astype(BF16)
        wq_p = _pad_heads(mla_w_uq[l], MLA_HEADS, MLA_QK).astype(BF16)
        q, k, v = _mla_prep(p, cos_t, sin_t, swap, mla_qa_g[l].reshape(1, -1), mla_kva_g[l].reshape(1, -1),
                            _pad_vec(mla_qn_g[l], 0), _pad_vec(mla_kn_g[l], 0), _pad_vec(mla_kr_g[l], MLA_NOPE),
                            wq_p, wk_p, wv, B, lt)
        o_mla = _attention(q, k, v, seq, n_ctx, lat_tiles_per_b if last else tiles_per_b)

        if last:
            n_tiles = B * lat_tiles_per_b
            tile = lambda t: (t // lat_tiles_per_b) * tiles_per_b + t % lat_tiles_per_b
        else:
            n_tiles = B * tiles_per_b
            tile = lambda t: t
        bs_full = jnp.broadcast_to(gmlp_bs[l][:, :, None], (GMLP_GROUPS, GMLP_CHUNK, GMLP_CHUNK))
        x1, h2 = _merge(x_parts, mod3, p, y_ret.reshape(B * lt, -1), o_mla.reshape(B * lt, -1),
                        gmlp_ln_g[l].reshape(1, -1), gmlp_ln_b[l].reshape(1, -1), gmlp_ws[l].astype(BF16), bs_full,
                        w_br_ret[l].astype(BF16), w_br_mla[l].astype(BF16), w_br_gmlp[l].astype(BF16),
                        w_out[l].astype(BF16), norm2_g[l].reshape(1, D), n_tiles, tile, tiles_per_b, mod_row)

        bias_t = jnp.broadcast_to(moe_bias[l][:, None], (N_EXPERTS, LANES))
        idx, w, rank, counts, hp = _route(h2, moe_router[l].T, bias_t, n_tiles, tile)
        n_act = n_tiles * TM
        n_blocks = -(-(n_act * TOP_K + N_EXPERTS * (MOE_ROWS - 1)) // MOE_ROWS)
        blk_e, valid, pos_sc = _moe_plan(idx, rank, counts, n_blocks)
        xg = _dispatch(pos_sc, hp, n_blocks * MOE_ROWS)
        shared = _shared_expert(hp, sh_w_gate[l].astype(BF16), sh_w_up[l].astype(BF16), sh_w_down[l].astype(BF16))
        ys = _experts(blk_e, valid, xg, moe_w_gate, moe_w_up, moe_w_down, l, n_blocks)
        yg = _gather_rows(pos_sc, ys)
        if last:
            xs = _combine(yg, w, x1, shared, mod3, n_tiles, tile, mod_row, B * seq, lambda t: t)
        else:
            pending = (yg, w, x1, shared, mod3)
    return xs.reshape(B, seq, D)
```

```python
import functools
import math

import jax
import jax.numpy as jnp
from jax import lax
from jax.experimental import pallas as pl
from jax.experimental.pallas import tpu as pltpu
from jax.experimental.pallas import tpu_sc as plsc

F32 = jnp.float32
BF16 = jnp.bfloat16

D = 1024
GRID_W = 64
RET_HEADS = 4
RET_D = 128
RET_CHUNK = 256
RET_UNROLL = 11
RET_OUT_ROWS = 256
MLA_HEADS = 8
MLA_Q_LORA = 384
MLA_KV_LORA = 256
MLA_NOPE = 64
MLA_ROPE = 32
MLA_V = 64
MLA_V_EXT = MLA_V + 16
MLA_QK = MLA_NOPE + MLA_ROPE
HEAD_PAD = 128
ROPE_THETA = 10000.0
GMLP_GROUPS = 4
GMLP_W = 512
GMLP_CHUNK = 128
N_EXPERTS = 64
TOP_K = 6
D_EXPERT = 256
ROUTED_SCALE = 2.5
EPS = 1e-6
LOG2_E = 1.4426950408889634

LANES = 128
SUBLANES = 8
TM = 256
MOE_ROWS = 512
RT = 2 * TM
SHARED_ROWS = 1024
ATT_KV_CHUNK = 1024
ATT_HEADS = 4

C_MERGE = 0
C_UV = 3072
C_RET = 4096
C_CKV = 6144
C_KR = 6400
C_CQ = 6528
N_IN_PAD = 6912
IN_CHUNK = 768
ADA_CHUNK = 1536

VMEM_LIMIT = 56 * 1024 * 1024

SC_CORES = 2
SC_SUBCORES = 16
SC_ROWS = 128


def _cparams(n_axes, vmem=VMEM_LIMIT):
    return pltpu.CompilerParams(dimension_semantics=("arbitrary",) * n_axes, vmem_limit_bytes=vmem)


def _silu(x):
    return x * jax.nn.sigmoid(x)


def _dot(a, b):
    return jnp.dot(a, b, preferred_element_type=F32)


def _dot_nt(a, b):
    return lax.dot_general(a, b, (((1,), (1,)), ((), ())), preferred_element_type=F32)


def _dot_tn(a, b):
    return lax.dot_general(a, b, (((0,), (0,)), ((), ())), preferred_element_type=F32)


def _pack_bf16_pairs(x):
    n = x.shape[1] // 2
    lo = lax.bitcast_convert_type(x[:, :n].astype(BF16).astype(F32), jnp.uint32)
    hi = lax.bitcast_convert_type(x[:, n:].astype(BF16).astype(F32), jnp.uint32)
    return (lo >> 16) | hi


def _unpack_bf16_pairs(u):
    lo = lax.bitcast_convert_type(u << 16, F32)
    hi = lax.bitcast_convert_type(u & jnp.uint32(0xFFFF0000), F32)
    return lo, hi


def _ada_kernel(c_ref, w_ref, b_ref, o_ref):
    s = _silu(c_ref[...])
    o_ref[...] = _dot(s.astype(BF16), w_ref[...].astype(BF16)) + b_ref[...]


def _ada(c_rows, ada_w, ada_b):
    depth = ada_w.shape[0]
    n = ada_w.shape[2]
    cw = ADA_CHUNK
    return pl.pallas_call(
        _ada_kernel,
        grid=(depth, n // cw),
        in_specs=[pl.BlockSpec((SUBLANES, D), lambda l, j: (0, 0)),
                  pl.BlockSpec((None, D, cw), lambda l, j: (l, 0, j)),
                  pl.BlockSpec((None, 1, cw), lambda l, j: (l, 0, j))],
        out_specs=pl.BlockSpec((None, SUBLANES, cw), lambda l, j: (l, 0, j)),
        out_shape=jax.ShapeDtypeStruct((depth, SUBLANES, n), F32),
        compiler_params=_cparams(2),
        name="ada_mod",
    )(c_rows, ada_w, ada_b.reshape(depth, 1, n))


def _modulated_rmsnorm(x, g, shift, scale):
    y = x * lax.rsqrt(jnp.mean(x * x, axis=-1, keepdims=True) + EPS) * g
    return y * (1.0 + scale) + shift


def _stream_specs(x_parts, tile, tiles_per_b):
    if len(x_parts) == 1:
        return [pl.BlockSpec((TM, D), lambda t: (tile(t), 0))]
    lat_tiles = tiles_per_b - 1

    def latent(t):
        s = tile(t)
        return ((s // tiles_per_b) * lat_tiles + jnp.minimum(s % tiles_per_b, lat_tiles - 1), 0)

    return [pl.BlockSpec((TM, D), latent), pl.BlockSpec((TM, D), lambda t: (tile(t) // tiles_per_b, 0))]


def _read_stream(x_refs, tile, tiles_per_b):
    if len(x_refs) == 1:
        return x_refs[0][...]
    is_ctx = tile(pl.program_id(0)) % tiles_per_b == tiles_per_b - 1
    return jnp.where(is_ctx, x_refs[1][...], x_refs[0][...])


def _project_in(x, mod_ref, g_ref, w_ref, o_ref, h_scr):
    h = _modulated_rmsnorm(x, g_ref[...], mod_ref[:, 0:D], mod_ref[:, D:2 * D])
    h_scr[...] = h.astype(BF16)
    for c in range(N_IN_PAD // IN_CHUNK):
        cols = slice(c * IN_CHUNK, (c + 1) * IN_CHUNK)
        o_ref[:, cols] = _dot(h_scr[...], w_ref[:, cols]).astype(BF16)


def _in_proj_kernel(*refs, n_x, tiles_per_b):
    mod_ref, g_ref, w_ref, o_ref, h_scr = refs[n_x:]
    _project_in(_read_stream(refs[:n_x], lambda t: t, tiles_per_b), mod_ref, g_ref, w_ref, o_ref, h_scr)


def _in_proj(x_parts, mod3, g, w_in_r, n_tiles, tiles_per_b, mod_row):
    n_rows = n_tiles * TM
    return pl.pallas_call(
        functools.partial(_in_proj_kernel, n_x=len(x_parts), tiles_per_b=tiles_per_b),
        grid=(n_tiles,),
        in_specs=_stream_specs(x_parts, lambda t: t, tiles_per_b) + [
                  pl.BlockSpec((None, 1, 6 * D), lambda t: (mod_row(t), 0, 0)),
                  pl.BlockSpec((1, D), lambda t: (0, 0)),
                  pl.BlockSpec((D, N_IN_PAD), lambda t: (0, 0), pipeline_mode=pl.Buffered(1))],
        out_specs=pl.BlockSpec((TM, N_IN_PAD), lambda t: (t, 0)),
        out_shape=jax.ShapeDtypeStruct((n_rows, N_IN_PAD), BF16),
        scratch_shapes=[pltpu.VMEM((TM, D), BF16)],
        compiler_params=_cparams(1),
        name="in_proj",
    )(*x_parts, mod3, g, w_in_r)


def _retention_kernel(lg_ref, q_ref, k_ref, v_ref, g_ref, gng_ref, gnb_ref, y_ref, of_scr, ob_scr,
                      *, n_lat_chunks, n_ctx_chunks):
    h = pl.program_id(1)
    lg_f = lg_ref[0, h]
    lg_b = lg_ref[1, h]
    C = RET_CHUNK
    k_scale = RET_D ** -0.5
    ri = lax.broadcasted_iota(jnp.int32, (C, C), 0).astype(F32)
    ci = lax.broadcasted_iota(jnp.int32, (C, C), 1).astype(F32)
    pos = lax.broadcasted_iota(jnp.int32, (C, 1), 0).astype(F32)
    diff = ri - ci
    d_f = jnp.where(diff >= 0, jnp.exp(lg_f * jnp.maximum(diff, 0.0)), 0.0) * k_scale
    d_b = jnp.where(diff < 0, jnp.exp(lg_b * jnp.maximum(-diff, 0.0)), 0.0) * k_scale
    qdec_f = jnp.exp(lg_f * (pos + 1.0))
    kdec_f = jnp.exp(lg_f * (C - 1.0 - pos)) * k_scale
    cdec_f = jnp.exp(lg_f * C)
    qdec_b = jnp.exp(lg_b * (C - pos))
    kdec_b = jnp.exp(lg_b * pos) * k_scale
    cdec_b = jnp.exp(lg_b * C)

    d_both = d_f + d_b

    def chunk(c, state, qdec, kdec, cdec, with_intra):
        rows = pl.ds(pl.multiple_of(c * C, C), C)
        q = q_ref[rows, :]
        k = k_ref[rows, :]
        v = v_ref[rows, :]
        o = _dot((q.astype(F32) * qdec).astype(BF16), state.astype(BF16))
        if with_intra:
            o = o + _dot((_dot_nt(q, k) * d_both).astype(BF16), v)
        kd = (k.astype(F32) * kdec).astype(BF16)
        return rows, o, state * cdec + _dot_tn(kd, v)

    n_all = n_lat_chunks + n_ctx_chunks

    def scan_body(i, states):
        s_f, s_b = states
        c_f = jnp.where(i < n_ctx_chunks, n_lat_chunks + i, i - n_ctx_chunks)
        rows, o, s_f = chunk(c_f, s_f, qdec_f, kdec_f, cdec_f, True)
        of_scr[rows, :] = o
        rows, o, s_b = chunk(n_all - 1 - i, s_b, qdec_b, kdec_b, cdec_b, False)
        ob_scr[rows, :] = o
        return s_f, s_b

    zero = jnp.zeros((RET_D, RET_D), F32)
    lax.fori_loop(0, n_all, scan_body, (zero, zero), unroll=math.gcd(n_all, RET_UNROLL))

    def out_body(c, _):
        rows = pl.ds(pl.multiple_of(c * RET_OUT_ROWS, RET_OUT_ROWS), RET_OUT_ROWS)
        o = of_scr[rows, :] + ob_scr[rows, :]
        mu = jnp.mean(o, axis=-1, keepdims=True)
        var = jnp.mean(jnp.square(o - mu), axis=-1, keepdims=True)
        on = (o - mu) * lax.rsqrt(var + EPS)
        y = _silu(g_ref[rows, :].astype(F32)) * (on * gng_ref[...] + gnb_ref[...])
        y_ref[rows, :] = y.astype(BF16)
        return 0

    lax.fori_loop(0, n_all * C // RET_OUT_ROWS, out_body, 0, unroll=3)


def _retention(p3, lg, gn_g, gn_b, seq, ctx):
    B, lt, _ = p3.shape
    base = C_RET // RET_D
    kern = functools.partial(_retention_kernel, n_lat_chunks=seq // RET_CHUNK, n_ctx_chunks=ctx // RET_CHUNK)

    def col(off):
        return pl.BlockSpec((None, lt, RET_D), lambda b, h: (b, 0, base + off * RET_HEADS + h))

    return pl.pallas_call(
        kern,
        grid=(B, RET_HEADS),
        in_specs=[pl.BlockSpec(memory_space=pltpu.SMEM),
                  col(0), col(1), col(2), col(3),
                  pl.BlockSpec((1, RET_D), lambda b, h: (0, h)),
                  pl.BlockSpec((1, RET_D), lambda b, h: (0, h))],
        out_specs=pl.BlockSpec((None, lt, RET_D), lambda b, h: (b, 0, h)),
        out_shape=jax.ShapeDtypeStruct((B, lt, RET_HEADS * RET_D), BF16),
        scratch_shapes=[pltpu.VMEM((lt, RET_D), F32), pltpu.VMEM((lt, RET_D), F32)],
        compiler_params=_cparams(2),
        name="retention",
    )(lg, p3, p3, p3, p3, gn_g, gn_b)


def _mla_prep_kernel(cq_ref, ckv_ref, kr_ref, cos_ref, sin_ref, swap_ref, qa_ref, kva_ref, qn_ref, kn_ref, krg_ref,
                     wq_ref, wk_ref, wv_ref, q_ref, k_ref, v_ref):
    cos = cos_ref[...]
    sin = sin_ref[...]

    def rms(x, n):
        return x * lax.rsqrt(jnp.sum(x * x, axis=-1, keepdims=True) * (1.0 / n) + EPS)

    def rope(x):
        return x * cos + _dot(x.astype(BF16), swap_ref[...]) * sin

    cq = cq_ref[...].astype(F32)
    cqn = (rms(cq, MLA_Q_LORA) * qa_ref[...]).astype(BF16)
    q_all = _dot(cqn, wq_ref[...])
    ckv = ckv_ref[...].astype(F32)
    ckvn = (rms(ckv, MLA_KV_LORA) * kva_ref[...]).astype(BF16)
    k_all = _dot(ckvn, wk_ref[...])
    k_rope = rope(rms(kr_ref[...].astype(F32), MLA_ROPE) * krg_ref[...])
    scale = MLA_QK ** -0.5 * LOG2_E
    v_t = _dot_nt(wv_ref[...], ckvn)
    ones_row = jnp.where(lax.broadcasted_iota(jnp.int32, (MLA_V_EXT - MLA_V, TM), 0) == 0, 1.0, 0.0)
    for h in range(MLA_HEADS):
        cols = slice(h * HEAD_PAD, (h + 1) * HEAD_PAD)
        qh = rope(rms(q_all[:, cols], MLA_QK) * qn_ref[...]) * scale
        q_ref[h] = qh.astype(BF16)
        kh = rms(k_all[:, cols], MLA_NOPE) * kn_ref[...] + k_rope
        k_ref[h] = kh.astype(BF16)
        v_ref[h] = jnp.concatenate([v_t[h * MLA_V:(h + 1) * MLA_V, :], ones_row], axis=0).astype(BF16)


def _mla_prep(p, cos_t, sin_t, swap, qa_g, kva_g, qn_p, kn_p, kr_p, wq_p, wk_p, wv, B, lt):
    tiles_per_b = lt // TM
    hw = MLA_HEADS * HEAD_PAD
    const = lambda shape: pl.BlockSpec(shape, lambda b, j: (0,) * len(shape))
    head_out = pl.BlockSpec((None, MLA_HEADS, TM, HEAD_PAD), lambda b, j: (b, 0, j, 0))
    shp = jax.ShapeDtypeStruct((B, MLA_HEADS, lt, HEAD_PAD), BF16)
    v_out = pl.BlockSpec((None, MLA_HEADS, None, MLA_V_EXT, TM), lambda b, j: (b, 0, j, 0, 0))
    v_shp = jax.ShapeDtypeStruct((B, MLA_HEADS, tiles_per_b, MLA_V_EXT, TM), BF16)
    return pl.pallas_call(
        _mla_prep_kernel,
        grid=(B, tiles_per_b),
        in_specs=[pl.BlockSpec((TM, MLA_Q_LORA), lambda b, j: (b * tiles_per_b + j, C_CQ // MLA_Q_LORA)),
                  pl.BlockSpec((TM, MLA_KV_LORA), lambda b, j: (b * tiles_per_b + j, C_CKV // MLA_KV_LORA)),
                  pl.BlockSpec((TM, HEAD_PAD), lambda b, j: (b * tiles_per_b + j, C_KR // HEAD_PAD)),
                  pl.BlockSpec((TM, HEAD_PAD), lambda b, j: (j, 0)),
                  pl.BlockSpec((TM, HEAD_PAD), lambda b, j: (j, 0)),
                  const((HEAD_PAD, HEAD_PAD)),
                  const((1, MLA_Q_LORA)), const((1, MLA_KV_LORA)),
                  const((1, HEAD_PAD)), const((1, HEAD_PAD)), const((1, HEAD_PAD)),
                  const((MLA_Q_LORA, hw)), const((MLA_KV_LORA, hw)), const((MLA_HEADS * MLA_V, MLA_KV_LORA))],
        out_specs=[head_out, head_out, v_out],
        out_shape=[shp, shp, v_shp],
        compiler_params=_cparams(2),
        name="mla_prep",
    )(p, p, p, cos_t, sin_t, swap, qa_g, kva_g, qn_p, kn_p, kr_p, wq_p, wk_p, wv)


def _attention_stages(seq, ctx):
    n_blk = ATT_KV_CHUNK // TM
    total = (seq + ctx) // TM
    first = n_blk + total % n_blk if total >= n_blk else total
    return [(total - first, first)] + [(c * n_blk, n_blk) for c in range((total - first) // n_blk)]


def _attention_kernel(q_ref, k_ref, v_ref, o_ref, s_scr, *, seq, ctx, ctx_tile):
    i = pl.program_id(2)

    def scores(hh, slot, blk, nb):
        s_scr[hh, slot, 0:nb * TM, :] = _dot_nt(k_ref[hh, blk * TM:(blk + nb) * TM, :], q_ref[hh])

    def absorb(hh, slot, blk, nb, carry):
        m, acc = carry
        s = s_scr[hh, slot, 0:nb * TM, :]
        m_new = jnp.maximum(m, jnp.max(s, axis=0, keepdims=True))
        p = jnp.exp2(s - m_new).astype(BF16)
        acc = jnp.exp2(m - m_new) * acc
        for j in range(nb):
            acc = acc + _dot(v_ref[hh, blk + j], p[j * TM:(j + 1) * TM, :])
        return m_new, acc

    def attend(stages):
        for hh in range(ATT_HEADS):
            scores(hh, 0, *stages[0])
        carries = [(jnp.full((1, TM), -jnp.inf, F32), jnp.zeros((MLA_V_EXT, TM), F32))] * ATT_HEADS
        for n, stage in enumerate(stages):
            for hh in range(ATT_HEADS):
                if n + 1 < len(stages):
                    scores(hh, (n + 1) % 2, *stages[n + 1])
                carries[hh] = absorb(hh, n % 2, *stage, carries[hh])
        outs = [acc[0:MLA_V, :] / acc[MLA_V:MLA_V + 1, :] for _, acc in carries]
        o_ref[...] = jnp.concatenate(outs, axis=0).T.astype(BF16)

    @pl.when(i != ctx_tile)
    def _():
        attend(_attention_stages(seq, ctx))

    @pl.when(i == ctx_tile)
    def _():
        attend([(seq // TM, ctx // TM)])


def _attention(q, k, v, seq, ctx, n_q_tiles):
    B, H, lt, _ = q.shape
    kern = functools.partial(_attention_kernel, seq=seq, ctx=ctx, ctx_tile=seq // TM)
    slot_rows = max(nb for _, nb in _attention_stages(seq, ctx)) * TM
    return pl.pallas_call(
        kern,
        grid=(B, H // ATT_HEADS, n_q_tiles),
        in_specs=[pl.BlockSpec((None, ATT_HEADS, TM, HEAD_PAD), lambda b, h, i: (b, h, i, 0)),
                  pl.BlockSpec((None, ATT_HEADS, lt, HEAD_PAD), lambda b, h, i: (b, h, 0, 0)),
                  pl.BlockSpec((None, ATT_HEADS, lt // TM, MLA_V_EXT, TM), lambda b, h, i: (b, h, 0, 0, 0))],
        out_specs=pl.BlockSpec((None, TM, ATT_HEADS * MLA_V), lambda b, h, i: (b, i, h)),
        out_shape=jax.ShapeDtypeStruct((B, lt, H * MLA_V), BF16),
        scratch_shapes=[pltpu.VMEM((ATT_HEADS, 2, slot_rows, TM), F32)],
        compiler_params=_cparams(3),
        name="attention",
    )(q, k, v)


def _merge_kernel(*refs, n_x, tile, tiles_per_b):
    (mod_ref, mg_ref, uv_ref, yr_ref, om_ref, lng_ref, lnb_ref, ws_ref, bs_ref,
     wr_ref, wm_ref, wg_ref, wo_ref, n2_ref, x1_ref, h2_ref) = refs[n_x:]
    yr = _dot(yr_ref[...], wr_ref[...])
    ym = _dot(om_ref[...], wm_ref[...])
    z = jax.nn.gelu(uv_ref[...].astype(F32))
    u = z[:, :GMLP_W]
    v = z[:, GMLP_W:]
    mu = jnp.mean(v, axis=-1, keepdims=True)
    var = jnp.mean(jnp.square(v - mu), axis=-1, keepdims=True)
    vn = ((v - mu) * lax.rsqrt(var + EPS) * lng_ref[...] + lnb_ref[...]).astype(BF16)
    gw = GMLP_W // GMLP_GROUPS
    chunks = []
    for c in range(TM // GMLP_CHUNK):
        rows = slice(c * GMLP_CHUNK, (c + 1) * GMLP_CHUNK)
        groups = [_dot(ws_ref[g], vn[rows, g * gw:(g + 1) * gw]) + bs_ref[g] for g in range(GMLP_GROUPS)]
        chunks.append(jnp.concatenate(groups, axis=1))
    sv = jnp.concatenate(chunks, axis=0)
    yg = _dot((u * sv).astype(BF16), wg_ref[...])
    gate = jax.nn.sigmoid(mg_ref[...].astype(F32))
    y = gate[:, :D] * yr + gate[:, D:2 * D] * ym + gate[:, 2 * D:] * yg
    out = _dot(y.astype(BF16), wo_ref[...])
    x1 = _read_stream(refs[:n_x], tile, tiles_per_b) + mod_ref[:, 2 * D:3 * D] * out
    x1_ref[...] = x1
    h2_ref[...] = _modulated_rmsnorm(x1, n2_ref[...], mod_ref[:, 3 * D:4 * D], mod_ref[:, 4 * D:5 * D])


def _merge(x_parts, mod3, p, y_ret, o_mla, ln_g, ln_b, ws, bs_full, w_br_ret, w_br_mla, w_br_gmlp, w_out, n2_g,
           n_tiles, tile, tiles_per_b, mod_row):
    n_rows = p.shape[0]
    const = lambda shape: pl.BlockSpec(shape, lambda t: (0,) * len(shape))
    row = lambda w, cb=0: pl.BlockSpec((TM, w), lambda t: (tile(t), cb))
    shp = jax.ShapeDtypeStruct((n_rows, D), F32)
    return pl.pallas_call(
        functools.partial(_merge_kernel, n_x=len(x_parts), tile=tile, tiles_per_b=tiles_per_b),
        grid=(n_tiles,),
        in_specs=_stream_specs(x_parts, tile, tiles_per_b) + [
                  pl.BlockSpec((None, 1, 6 * D), lambda t: (mod_row(tile(t)), 0, 0)),
                  row(3 * D, C_MERGE // (3 * D)), row(D, C_UV // D),
                  row(RET_HEADS * RET_D), row(MLA_HEADS * MLA_V),
                  const((1, GMLP_W)), const((1, GMLP_W)),
                  const((GMLP_GROUPS, GMLP_CHUNK, GMLP_CHUNK)), const((GMLP_GROUPS, GMLP_CHUNK, GMLP_CHUNK)),
                  const((RET_HEADS * RET_D, D)), const((MLA_HEADS * MLA_V, D)), const((GMLP_W, D)),
                  const((D, D)), const((1, D))],
        out_specs=[row(D), row(D)],
        out_shape=[shp, shp],
        compiler_params=_cparams(1),
        name="merge",
    )(*x_parts, mod3, p, p, y_ret, o_mla, ln_g, ln_b, ws, bs_full, w_br_ret, w_br_mla, w_br_gmlp, w_out, n2_g)


def _route_kernel(ha_ref, hb_ref, rt_ref, bt_ref, idx_ref, w_ref, rank_ref, cnt_ref, hp_ref, cnt_scr):
    @pl.when(pl.program_id(0) == 0)
    def _():
        cnt_scr[...] = jnp.zeros_like(cnt_scr)

    h = jnp.concatenate([ha_ref[...], hb_ref[...]], axis=0)
    logits = lax.dot_general(rt_ref[...], h, (((1,), (1,)), ((), ())), preferred_element_type=F32,
                             precision=lax.Precision.HIGHEST)
    scores = jax.nn.sigmoid(logits)
    sel = scores + bt_ref[:, 0:1]
    row_e = lax.broadcasted_iota(jnp.int32, (N_EXPERTS, RT), 0).astype(F32)
    row_o = lax.broadcasted_iota(jnp.int32, (SUBLANES, RT), 0)
    idx_out = jnp.zeros((SUBLANES, RT), F32)
    w_out = jnp.zeros((SUBLANES, RT), F32)
    hits = []
    for k in range(TOP_K):
        best = jnp.max(sel, axis=0, keepdims=True)
        pick = jnp.min(jnp.where(sel == best, row_e, float(N_EXPERTS)), axis=0, keepdims=True)
        hit = row_e == pick
        hits.append(hit)
        wk = jnp.sum(jnp.where(hit, scores, 0.0), axis=0, keepdims=True)
        sel = jnp.where(hit, -jnp.inf, sel)
        idx_out = jnp.where(row_o == k, pick, idx_out)
        w_out = jnp.where(row_o == k, wk, w_out)
    w_out = w_out / jnp.sum(w_out, axis=0, keepdims=True) * ROUTED_SCALE
    idx_ref[...] = idx_out.astype(jnp.int32)
    w_ref[...] = w_out
    chosen = jnp.zeros((N_EXPERTS, RT), F32)
    for hit in hits:
        chosen = jnp.where(hit, 1.0, chosen)
    earlier = (lax.broadcasted_iota(jnp.int32, (RT, RT), 0) < lax.broadcasted_iota(jnp.int32, (RT, RT), 1))
    before = _dot(chosen.astype(BF16), jnp.where(earlier, 1.0, 0.0).astype(BF16)) + cnt_scr[:, 0:1]
    rank_out = jnp.zeros((SUBLANES, RT), F32)
    for k, hit in enumerate(hits):
        rank_out = jnp.where(row_o == k, jnp.sum(jnp.where(hit, before, 0.0), axis=0, keepdims=True), rank_out)
    rank_ref[...] = rank_out.astype(jnp.int32)
    cnt_scr[...] += jnp.sum(chosen, axis=1, keepdims=True)
    cnt_ref[...] = cnt_scr[...]
    hp_ref[...] = _pack_bf16_pairs(h)


def _route(h2, router_t, bias_t, n_tiles, tile):
    const = lambda shape: pl.BlockSpec(shape, lambda t: (0,) * len(shape))
    n_act = n_tiles * TM
    assert n_tiles % 2 == 0
    n_steps = n_tiles // 2
    per_tok = pl.BlockSpec((None, SUBLANES, RT), lambda t: (t, 0, 0))
    idx_t, w_t, rank_t, counts, hp = pl.pallas_call(
        _route_kernel,
        grid=(n_steps,),
        in_specs=[pl.BlockSpec((TM, D), lambda t: (tile(2 * t), 0)),
                  pl.BlockSpec((TM, D), lambda t: (tile(2 * t + 1), 0)),
                  const((N_EXPERTS, D)), const((N_EXPERTS, LANES))],
        out_specs=[per_tok, per_tok, per_tok,
                   pl.BlockSpec((N_EXPERTS, LANES), lambda t: (0, 0)),
                   pl.BlockSpec((RT, D // 2), lambda t: (t, 0))],
        out_shape=[jax.ShapeDtypeStruct((n_steps, SUBLANES, RT), jnp.int32),
                   jax.ShapeDtypeStruct((n_steps, SUBLANES, RT), F32),
                   jax.ShapeDtypeStruct((n_steps, SUBLANES, RT), jnp.int32),
                   jax.ShapeDtypeStruct((N_EXPERTS, LANES), F32),
                   jax.ShapeDtypeStruct((n_act, D // 2), jnp.uint32)],
        scratch_shapes=[pltpu.VMEM((N_EXPERTS, LANES), F32)],
        compiler_params=_cparams(1),
        name="route",
    )(h2, h2, router_t, bias_t)
    token_major = lambda a: a.transpose(0, 2, 1).reshape(n_act, SUBLANES)
    return (token_major(idx_t)[:, :TOP_K], token_major(w_t), token_major(rank_t)[:, :TOP_K], counts[:, 0], hp)


def _shared_expert_kernel(h_ref, sg_ref, su_ref, sd_ref, o_ref):
    lo, hi = _unpack_bf16_pairs(h_ref[...])
    hb = jnp.concatenate([lo, hi], axis=1).astype(BF16)
    a = _silu(_dot(hb, sg_ref[...])) * _dot(hb, su_ref[...])
    o_ref[...] = _dot(a.astype(BF16), sd_ref[...]).astype(BF16)


def _shared_expert(hp, sg, su, sd):
    const = lambda shape: pl.BlockSpec(shape, lambda t: (0,) * len(shape))
    n_act = hp.shape[0]
    rows = math.gcd(n_act, SHARED_ROWS)
    return pl.pallas_call(
        _shared_expert_kernel,
        grid=(n_act // rows,),
        in_specs=[pl.BlockSpec((rows, D // 2), lambda t: (t, 0)),
                  const((D, D_EXPERT)), const((D, D_EXPERT)), const((D_EXPERT, D))],
        out_specs=pl.BlockSpec((rows, D), lambda t: (t, 0)),
        out_shape=jax.ShapeDtypeStruct((n_act, D), BF16),
        compiler_params=_cparams(1),
        name="shared_expert",
    )(hp, sg, su, sd)


def _dispatch(pos_sc, hp, n_rows):
    n_batches = pos_sc.shape[0]
    n_workers = SC_CORES * SC_SUBCORES
    mesh = plsc.VectorSubcoreMesh(core_axis_name="c", subcore_axis_name="s")

    @functools.partial(
        pl.kernel, mesh=mesh,
        out_type=jax.ShapeDtypeStruct((n_rows, D // 2), jnp.uint32),
        scratch_types=[pltpu.VMEM((TOP_K, SC_ROWS), jnp.int32),
                       pltpu.VMEM((SC_ROWS, D // 2), jnp.uint32),
                       pltpu.SemaphoreType.DMA],
        name="moe_dispatch")
    def scatter(pos_hbm, h_hbm, xs_hbm, idx_v, rows_v, sem):
        worker = lax.axis_index("s") * SC_CORES + lax.axis_index("c")

        @pl.loop(0, pl.cdiv(n_batches, n_workers))
        def _(j):
            b = j * n_workers + worker

            @pl.when(b < n_batches)
            def _():
                pltpu.sync_copy(pos_hbm.at[b], idx_v)
                pltpu.sync_copy(h_hbm.at[pl.ds(b * SC_ROWS, SC_ROWS)], rows_v)
                copies = [pltpu.async_copy(rows_v, xs_hbm.at[idx_v.at[k]], sem) for k in range(TOP_K)]
                for cp in copies:
                    cp.wait()

    return scatter(pos_sc, hp)


def _expert_kernel(blk_e_ref, valid_ref, x_ref, wg_ref, wu_ref, wd_ref, y_ref, wg_s, wu_s, wd_s):
    i = pl.program_id(0)
    n_valid = valid_ref[i]

    @pl.when(n_valid > 0)
    def _():
        @pl.when(jnp.logical_or(i == 0, blk_e_ref[i] != blk_e_ref[jnp.maximum(i - 1, 0)]))
        def _():
            wg_s[...] = wg_ref[...].astype(BF16)
            wu_s[...] = wu_ref[...].astype(BF16)
            wd_s[...] = wd_ref[...].astype(BF16)

        row = lax.broadcasted_iota(jnp.int32, (MOE_ROWS, 1), 0)
        lo, hi = _unpack_bf16_pairs(jnp.where(row < n_valid, x_ref[...], jnp.uint32(0)))
        x = jnp.concatenate([lo, hi], axis=1).astype(BF16)
        hb = _silu(_dot(x, wg_s[...])) * _dot(x, wu_s[...])
        y_ref[...] = _pack_bf16_pairs(_dot(hb.astype(BF16), wd_s[...]))

    @pl.when(n_valid == 0)
    def _():
        y_ref[...] = jnp.zeros_like(y_ref)


def _experts(blk_e, valid, xs, wg, wu, wd, layer, n_blocks):
    grid_spec = pltpu.PrefetchScalarGridSpec(
        num_scalar_prefetch=2,
        grid=(n_blocks,),
        in_specs=[pl.BlockSpec((MOE_ROWS, D // 2), lambda i, be, nv: (i, 0)),
                  pl.BlockSpec((None, None, D, D_EXPERT), lambda i, be, nv: (layer, be[i], 0, 0)),
                  pl.BlockSpec((None, None, D, D_EXPERT), lambda i, be, nv: (layer, be[i], 0, 0)),
                  pl.BlockSpec((None, None, D_EXPERT, D), lambda i, be, nv: (layer, be[i], 0, 0))],
        out_specs=pl.BlockSpec((MOE_ROWS, D // 2), lambda i, be, nv: (i, 0)),
        scratch_shapes=[pltpu.VMEM((D, D_EXPERT), BF16), pltpu.VMEM((D, D_EXPERT), BF16),
                        pltpu.VMEM((D_EXPERT, D), BF16)],
    )
    return pl.pallas_call(
        _expert_kernel,
        grid_spec=grid_spec,
        out_shape=jax.ShapeDtypeStruct((n_blocks * MOE_ROWS, D // 2), jnp.uint32),
        compiler_params=_cparams(1),
        name="routed_experts",
    )(blk_e, valid, xs, wg, wu, wd)


def _gather_rows(pos_sc, ys):
    n_batches = pos_sc.shape[0]
    n_workers = SC_CORES * SC_SUBCORES
    half = SC_ROWS // 2
    mesh = plsc.VectorSubcoreMesh(core_axis_name="c", subcore_axis_name="s")

    @functools.partial(
        pl.kernel, mesh=mesh,
        out_type=jax.ShapeDtypeStruct((TOP_K, n_batches * SC_ROWS, D // 2), jnp.uint32),
        scratch_types=[pltpu.VMEM((TOP_K, SC_ROWS), jnp.int32),
                       pltpu.VMEM((2, half, D // 2), jnp.uint32),
                       pltpu.SemaphoreType.DMA, pltpu.SemaphoreType.DMA],
        name="moe_gather")
    def gather(pos_hbm, y_hbm, out_hbm, idx_v, bufs, gsem, wsem):
        worker = lax.axis_index("s") * SC_CORES + lax.axis_index("c")

        @pl.loop(0, pl.cdiv(n_batches, n_workers))
        def _(j):
            b = j * n_workers + worker

            @pl.when(b < n_batches)
            def _():
                pltpu.sync_copy(pos_hbm.at[b], idx_v)
                items = [(k, h) for k in range(TOP_K) for h in range(2)]

                def fetch(i):
                    k, h = items[i]
                    return pltpu.async_copy(y_hbm.at[idx_v.at[k, pl.ds(h * half, half)]], bufs.at[i % 2], gsem)

                pending_gather = fetch(0)
                pending_write = None
                for i, (k, h) in enumerate(items):
                    pending_gather.wait()
                    if pending_write is not None:
                        pending_write.wait()
                    if i + 1 < len(items):
                        pending_gather = fetch(i + 1)
                    pending_write = pltpu.async_copy(
                        bufs.at[i % 2], out_hbm.at[k, pl.ds(b * SC_ROWS + h * half, half)], wsem)
                pending_write.wait()

    return gather(pos_sc, ys)


def _combine_tile(y_refs, w_ref, x1_ref, sh_ref, mod_ref):
    f_lo = sh_ref[:, :D // 2].astype(F32)
    f_hi = sh_ref[:, D // 2:].astype(F32)
    for k in range(TOP_K):
        lo, hi = _unpack_bf16_pairs(y_refs[k][...])
        f_lo = f_lo + lo * w_ref[:, k:k + 1]
        f_hi = f_hi + hi * w_ref[:, k:k + 1]
    return x1_ref[...] + mod_ref[:, 5 * D:6 * D] * jnp.concatenate([f_lo, f_hi], axis=1)


def _combine_kernel(*refs):
    o_ref = refs[-1]
    o_ref[...] = _combine_tile(refs[:TOP_K], *refs[TOP_K:-1])


def _combine_specs(tile, mod_row):
    planes = [pl.BlockSpec((None, TM, D // 2), lambda t, k=k: (k, t, 0)) for k in range(TOP_K)]
    return planes + [pl.BlockSpec((TM, SUBLANES), lambda t: (t, 0)),
                     pl.BlockSpec((TM, D), lambda t: (tile(t), 0)),
                     pl.BlockSpec((TM, D), lambda t: (t, 0)),
                     pl.BlockSpec((None, 1, 6 * D), lambda t: (mod_row(tile(t)), 0, 0))]


def _combine(yg, w, x1, shared, mod3, n_tiles, tile, mod_row, out_rows, out_tile):
    return pl.pallas_call(
        _combine_kernel,
        grid=(n_tiles,),
        in_specs=_combine_specs(tile, mod_row),
        out_specs=pl.BlockSpec((TM, D), lambda t: (out_tile(t), 0)),
        out_shape=jax.ShapeDtypeStruct((out_rows, D), F32),
        compiler_params=_cparams(1),
        name="moe_combine",
    )(*([yg] * TOP_K), w, x1, shared, mod3)


def _combine_in_proj_kernel(*refs):
    xs_ref, o_ref, h_scr = refs[-3:]
    mod_ref, g_ref, w_ref = refs[TOP_K + 4:-3]
    x = _combine_tile(refs[:TOP_K], *refs[TOP_K:TOP_K + 4])
    xs_ref[...] = x
    _project_in(x, mod_ref, g_ref, w_ref, o_ref, h_scr)


def _combine_in_proj(yg, w, x1, shared, mod3_prev, mod3, g, w_in_r, n_tiles, mod_row):
    n_rows = x1.shape[0]
    ident = lambda t: t
    return pl.pallas_call(
        _combine_in_proj_kernel,
        grid=(n_tiles,),
        in_specs=_combine_specs(ident, mod_row) + [
            pl.BlockSpec((None, 1, 6 * D), lambda t: (mod_row(t), 0, 0)),
            pl.BlockSpec((1, D), lambda t: (0, 0)),
            pl.BlockSpec((D, N_IN_PAD), lambda t: (0, 0), pipeline_mode=pl.Buffered(1))],
        out_specs=[pl.BlockSpec((TM, D), lambda t: (t, 0)),
                   pl.BlockSpec((TM, N_IN_PAD), lambda t: (t, 0))],
        out_shape=[jax.ShapeDtypeStruct((n_rows, D), F32),
                   jax.ShapeDtypeStruct((n_rows, N_IN_PAD), BF16)],
        scratch_shapes=[pltpu.VMEM((TM, D), BF16)],
        compiler_params=_cparams(1),
        name="combine_in_proj",
    )(*([yg] * TOP_K), w, x1, shared, mod3_prev, mod3, g, w_in_r)


def _moe_plan(idx, rank, counts, n_blocks):
    n = idx.shape[0]
    cnt = counts.reshape(N_EXPERTS).astype(jnp.int32)
    padded = (cnt + MOE_ROWS - 1) // MOE_ROWS * MOE_ROWS
    pad_end = jnp.cumsum(padded)
    pad_start = pad_end - padded
    experts = jnp.arange(N_EXPERTS, dtype=jnp.int32)
    pos = rank + jnp.sum(jnp.where(idx[:, :, None] == experts, pad_start, 0), axis=-1)
    blk_start = jnp.arange(n_blocks, dtype=jnp.int32) * MOE_ROWS
    blk_e = jnp.minimum(jnp.sum(blk_start[:, None] >= pad_end[None, :], axis=1), N_EXPERTS - 1).astype(jnp.int32)
    mine = blk_e[:, None] == experts
    in_expert = blk_start - jnp.sum(jnp.where(mine, pad_start, 0), axis=1)
    valid = jnp.clip(jnp.sum(jnp.where(mine, cnt, 0), axis=1) - in_expert, 0, MOE_ROWS).astype(jnp.int32)
    pos_sc = pos.astype(jnp.int32).reshape(n // SC_ROWS, SC_ROWS, TOP_K).transpose(0, 2, 1)
    return blk_e, valid, pos_sc


def _rope_tables(seq, ctx):
    half = MLA_ROPE // 2
    n_freq = half // 2
    inv = ROPE_THETA ** (-2.0 * jnp.arange(n_freq, dtype=F32) / half)
    t = jnp.arange(seq)
    ang_r = (t // GRID_W).astype(F32)[:, None] * inv
    ang_c = (t % GRID_W).astype(F32)[:, None] * inv
    cos = jnp.concatenate([jnp.cos(ang_r), jnp.cos(ang_r), jnp.cos(ang_c), jnp.cos(ang_c)], axis=1)
    sin = jnp.concatenate([-jnp.sin(ang_r), jnp.sin(ang_r), -jnp.sin(ang_c), jnp.sin(ang_c)], axis=1)
    pad_l = MLA_NOPE
    pad_r = HEAD_PAD - MLA_NOPE - MLA_ROPE
    cos = jnp.pad(cos, ((0, ctx), (pad_l, pad_r)), constant_values=1.0)
    cos = cos.at[seq:, :].set(1.0)
    sin = jnp.pad(sin, ((0, ctx), (pad_l, pad_r)))
    lane = jnp.arange(HEAD_PAD)
    partner = jnp.where(lane % 16 < 8, lane + 8, lane - 8)
    swap = (lane[:, None] == partner[None, :]).astype(BF16)
    return cos, sin, swap


def _pad_heads(w, n_heads, width, offset=0):
    k = w.shape[0]
    w = w.reshape(k, n_heads, width)
    w = jnp.pad(w, ((0, 0), (0, 0), (offset, HEAD_PAD - width - offset)))
    return w.reshape(k, n_heads * HEAD_PAD)


def _pad_vec(g, offset):
    return jnp.pad(g, (offset, HEAD_PAD - g.shape[0] - offset)).reshape(1, HEAD_PAD)


def _reorder_w_in(w):
    off_cq, off_ckv, off_kr, off_uv, off_merge = 2048, 2432, 2688, 2720, 3744
    kr = jnp.pad(w[:, off_kr:off_uv], ((0, 0), (MLA_NOPE, HEAD_PAD - MLA_NOPE - MLA_ROPE)))
    return jnp.concatenate([w[:, off_merge:], w[:, off_uv:off_merge], w[:, :off_cq],
                            w[:, off_ckv:off_kr], kr, w[:, off_cq:off_ckv]], axis=1).astype(BF16)


def kernel(x, c, ctx, c_ctx, ada_w, ada_b, norm1_g, norm2_g, w_in, ret_decay_fwd, ret_decay_bwd, ret_gn_g,
           ret_gn_b, w_br_ret, mla_qa_g, mla_w_uq, mla_kva_g, mla_w_ukv, mla_qn_g, mla_kn_g, mla_kr_g, w_br_mla,
           gmlp_ln_g, gmlp_ln_b, gmlp_ws, gmlp_bs, w_br_gmlp, w_out, moe_router, moe_bias, moe_w_gate, moe_w_up,
           moe_w_down, sh_w_gate, sh_w_up, sh_w_down):
    B, seq, _ = x.shape
    n_ctx = ctx.shape[1]
    depth = ada_w.shape[0]
    assert n_ctx == TM and seq % TM == 0 and B + 1 <= SUBLANES and TOP_K <= SUBLANES
    lt = seq + n_ctx
    tiles_per_b = lt // TM
    lat_tiles_per_b = seq // TM
    ctx_tile = lat_tiles_per_b

    def mod_row(t):
        return jnp.where(t % tiles_per_b == ctx_tile, B, t // tiles_per_b)

    c_rows = jnp.concatenate([c, c_ctx[None, :], jnp.zeros((SUBLANES - B - 1, D), F32)], axis=0)
    mod = _ada(c_rows, ada_w, ada_b)
    cos_t, sin_t, swap = _rope_tables(seq, n_ctx)
    x_parts = (x.reshape(B * seq, D), ctx.reshape(B * n_ctx, D))

    pending = None
    for l in range(depth):
        last = l == depth - 1
        mod3 = mod[l].reshape(SUBLANES, 1, 6 * D)
        if pending is None:
            p = _in_proj(x_parts, mod3, norm1_g[l].reshape(1, D), _reorder_w_in(w_in[l]), B * tiles_per_b,
                         tiles_per_b, mod_row)
        else:
            xs, p = _combine_in_proj(*pending, mod3, norm1_g[l].reshape(1, D), _reorder_w_in(w_in[l]),
                                     B * tiles_per_b, mod_row)
            x_parts = (xs,)

        lg = jnp.stack([jax.nn.log_sigmoid(ret_decay_fwd[l].astype(F32)),
                        jax.nn.log_sigmoid(ret_decay_bwd[l].astype(F32))])
        y_ret = _retention(p.reshape(B, lt, N_IN_PAD), lg, ret_gn_g[l].reshape(1, -1), ret_gn_b[l].reshape(1, -1),
                           seq, n_ctx)

        w_ukv = mla_w_ukv[l].reshape(MLA_KV_LORA, MLA_HEADS, MLA_NOPE + MLA_V)
        wk_p = _pad_heads(w_ukv[:, :, :MLA_NOPE].reshape(MLA_KV_LORA, -1), MLA_HEADS, MLA_NOPE).astype(BF16)
        wv = w_ukv[:, :, MLA_NOPE:].reshape(MLA_KV_LORA, MLA_HEADS * MLA_V).T.astype(BF16)
        wq_p = _pad_heads(mla_w_uq[l], MLA_HEADS, MLA_QK).astype(BF16)
        q, k, v = _mla_prep(p, cos_t, sin_t, swap, mla_qa_g[l].reshape(1, -1), mla_kva_g[l].reshape(1, -1),
                            _pad_vec(mla_qn_g[l], 0), _pad_vec(mla_kn_g[l], 0), _pad_vec(mla_kr_g[l], MLA_NOPE),
                            wq_p, wk_p, wv, B, lt)
        o_mla = _attention(q, k, v, seq, n_ctx, lat_tiles_per_b if last else tiles_per_b)

        if last:
            n_tiles = B * lat_tiles_per_b
            tile = lambda t: (t // lat_tiles_per_b) * tiles_per_b + t % lat_tiles_per_b
        else:
            n_tiles = B * tiles_per_b
            tile = lambda t: t
        bs_full = jnp.broadcast_to(gmlp_bs[l][:, :, None], (GMLP_GROUPS, GMLP_CHUNK, GMLP_CHUNK))
        x1, h2 = _merge(x_parts, mod3, p, y_ret.reshape(B * lt, -1), o_mla.reshape(B * lt, -1),
                        gmlp_ln_g[l].reshape(1, -1), gmlp_ln_b[l].reshape(1, -1), gmlp_ws[l].astype(BF16), bs_full,
                        w_br_ret[l].astype(BF16), w_br_mla[l].astype(BF16), w_br_gmlp[l].astype(BF16),
                        w_out[l].astype(BF16), norm2_g[l].reshape(1, D), n_tiles, tile, tiles_per_b, mod_row)

        bias_t = jnp.broadcast_to(moe_bias[l][:, None], (N_EXPERTS, LANES))
        idx, w, rank, counts, hp = _route(h2, moe_router[l].T, bias_t, n_tiles, tile)
        n_act = n_tiles * TM
        n_blocks = -(-(n_act * TOP_K + N_EXPERTS * (MOE_ROWS - 1)) // MOE_ROWS)
        blk_e, valid, pos_sc = _moe_plan(idx, rank, counts, n_blocks)
        xg = _dispatch(pos_sc, hp, n_blocks * MOE_ROWS)
        shared = _shared_expert(hp, sh_w_gate[l].astype(BF16), sh_w_up[l].astype(BF16), sh_w_down[l].astype(BF16))
        ys = _experts(blk_e, valid, xg, moe_w_gate, moe_w_up, moe_w_down, l, n_blocks)
        yg = _gather_rows(pos_sc, ys)
        if last:
            xs = _combine(yg, w, x1, shared, mod3, n_tiles, tile, mod_row, B * seq, lambda t: t)
        else:
            pending = (yg, w, x1, shared, mod3)
    return xs.reshape(B, seq, D)
```
